```python
import math
import jax, jax.numpy as jnp
from jax import lax
import numpy as np

D_MODEL = 1024
BATCH = 8
SEQ = 8192
DEPTH = 1

RET_HEADS = 4
RET_DK = 64
RET_DV = 128
RET_QK = RET_HEADS * RET_DK
RET_V = RET_HEADS * RET_DV
CHUNK = 128
FOX_HEADS = 8
FOX_DH = 64
FOX_W = FOX_HEADS * FOX_DH
Q_BLOCK = 128
D_FF = -(-8 * D_MODEL // (3 * 256)) * 256
ROPE_BASE = 10000.0
EPS = 1e-6
IN_SIZES = (RET_QK, RET_QK, RET_V, RET_V, FOX_W, FOX_W, FOX_W, FOX_HEADS, D_MODEL, D_MODEL)
IN_COLS = sum(IN_SIZES)

kernel_name = "hybrid_retention_fox_gated_block"


def rmsnorm(x, g):
    xf = x.astype(jnp.float32)
    y = xf * lax.rsqrt(jnp.mean(xf * xf, axis=-1, keepdims=True) + EPS)
    return (y * g.astype(jnp.float32)).astype(x.dtype)


def rotary(x, pos):
    half = x.shape[-1] // 2
    inv_freq = 1.0 / (ROPE_BASE ** (jnp.arange(half, dtype=jnp.float32) / half))
    ang = pos[:, None] * inv_freq[None, :]
    cos = jnp.cos(ang)[None, :, None, :]
    sin = jnp.sin(ang)[None, :, None, :]
    xf = x.astype(jnp.float32)
    x1, x2 = xf[..., :half], xf[..., half:]
    return jnp.concatenate([x1 * cos - x2 * sin, x1 * sin + x2 * cos], axis=-1)


def retention_chunkwise(q, k, v):
    B, S, H, dk = q.shape
    dv = v.shape[-1]
    n = S // CHUNK
    log_g = jnp.log1p(-(2.0 ** (-5.0 - jnp.arange(H, dtype=jnp.float32))))
    qc = q.astype(jnp.float32).reshape(B, n, CHUNK, H, dk)
    kc = k.astype(jnp.float32).reshape(B, n, CHUNK, H, dk)
    vc = v.astype(jnp.float32).reshape(B, n, CHUNK, H, dv)
    idx = jnp.arange(CHUNK, dtype=jnp.float32)
    diff = idx[:, None] - idx[None, :]
    decay = jnp.where(diff[None] >= 0, jnp.exp(jnp.maximum(diff, 0.0)[None] * log_g[:, None, None]), 0.0)
    scores = jnp.einsum('bnihd,bnjhd->bnhij', qc, kc) * decay[None, None]
    intra = jnp.einsum('bnhij,bnjhe->bnihe', scores, vc)
    zeta = jnp.exp((CHUNK - 1.0 - idx)[None, :] * log_g[:, None])
    chunk_kv = jnp.einsum('bnjhd,hj,bnjhe->bnhde', kc, zeta, vc)
    g_chunk = jnp.exp(CHUNK * log_g)[:, None, None]

    def step(r_prev, kv):
        return g_chunk * r_prev + kv, r_prev

    r0 = jnp.zeros((B, H, dk, dv), jnp.float32)
    _, states = lax.scan(step, r0, jnp.moveaxis(chunk_kv, 1, 0))
    states = jnp.moveaxis(states, 0, 1)
    xi = jnp.exp((idx + 1.0)[None, :] * log_g[:, None]).T
    inter = jnp.einsum('bnihd,bnhde->bnihe', qc, states) * xi[None, None, :, :, None]
    return (intra + inter).reshape(B, S, H, dv)


def forgetting_attention(q, k, v, log_f):
    B, S, H, d = q.shape
    nb = S // Q_BLOCK
    scale = 1.0 / math.sqrt(d)
    c = jnp.cumsum(log_f, axis=1).transpose(0, 2, 1)
    qh = q.transpose(0, 2, 1, 3)
    kh = k.transpose(0, 2, 1, 3)
    vh = v.transpose(0, 2, 1, 3)
    qb = qh.reshape(B, H, nb, Q_BLOCK, d).transpose(2, 0, 1, 3, 4)
    cb = c.reshape(B, H, nb, Q_BLOCK).transpose(2, 0, 1, 3)
    pos_k = jnp.arange(S)

    def block(args):
        i, q_blk, c_blk = args
        s = jnp.einsum('bhqd,bhkd->bhqk', q_blk, kh).astype(jnp.float32) * scale
        s = s + c_blk[..., None] - c[:, :, None, :]
        pos_q = i * Q_BLOCK + jnp.arange(Q_BLOCK)
        mask = pos_k[None, :] <= pos_q[:, None]
        s = jnp.where(mask[None, None], s, -jnp.inf)
        p = jax.nn.softmax(s, axis=-1)
        return jnp.einsum('bhqk,bhkd->bhqd', p.astype(vh.dtype), vh)

    out = lax.map(block, (jnp.arange(nb), qb, cb))
    return out.transpose(1, 2, 0, 3, 4).reshape(B, H, S, d).transpose(0, 2, 1, 3)


def split_cols(z):
    offs = np.cumsum(np.array(IN_SIZES))[:-1].tolist()
    return jnp.split(z, offs, axis=-1)


def _fwd_setup_inputs(seed: int = 0) -> dict:
    key = jax.random.key(seed)
    ks = jax.random.split(key, 16)
    f32 = jnp.float32
    nrm = lambda k, shp: jax.random.normal(k, shp, f32)
    L = DEPTH
    return {
        "x": nrm(ks[0], (BATCH, SEQ, D_MODEL)),
        "g_mix": 1.0 + 0.02 * nrm(ks[1], (L, D_MODEL)),
        "w_in": nrm(ks[2], (L, D_MODEL, IN_COLS)) * D_MODEL ** -0.5,
        "b_forget": 1.0 + 0.5 * nrm(ks[3], (L, FOX_HEADS)),
        "g_ret_norm": 1.0 + 0.02 * nrm(ks[4], (L, RET_V)),
        "w_ret_o": nrm(ks[5], (L, RET_V, D_MODEL)) * RET_V ** -0.5,
        "g_fox_q": 1.0 + 0.02 * nrm(ks[6], (L, FOX_DH)),
        "g_fox_k": 1.0 + 0.02 * nrm(ks[7], (L, FOX_DH)),
        "w_fox_o": nrm(ks[8], (L, FOX_W, D_MODEL)) * FOX_W ** -0.5,
        "w_out": nrm(ks[9], (L, D_MODEL, D_MODEL)) * D_MODEL ** -0.5,
        "g_ffn": 1.0 + 0.02 * nrm(ks[10], (L, D_MODEL)),
        "w_gate": nrm(ks[11], (L, D_MODEL, D_FF)) * D_MODEL ** -0.5,
        "w_up": nrm(ks[12], (L, D_MODEL, D_FF)) * D_MODEL ** -0.5,
        "w_down": nrm(ks[13], (L, D_FF, D_MODEL)) * D_FF ** -0.5,
    }


def _fwd_reference(x, g_mix, w_in, b_forget, g_ret_norm, w_ret_o, g_fox_q, g_fox_k, w_fox_o,
              w_out, g_ffn, w_gate, w_up, w_down):
    B, S, _ = x.shape
    pos = jnp.arange(S, dtype=jnp.float32)
    for l in range(DEPTH):
        h = rmsnorm(x, g_mix[l])
        z = h @ w_in[l]
        q_r, k_r, v_r, gt_r, q_f, k_f, v_f, f_f, a_r, a_f = split_cols(z)

        q_r = rotary(q_r.reshape(B, S, RET_HEADS, RET_DK), pos)
        k_r = rotary(k_r.reshape(B, S, RET_HEADS, RET_DK), pos) * (RET_DK ** -0.5)
        v_r = v_r.reshape(B, S, RET_HEADS, RET_DV)
        o_r = retention_chunkwise(q_r, k_r, v_r)
        mu = jnp.mean(o_r, axis=-1, keepdims=True)
        var = jnp.mean(jnp.square(o_r - mu), axis=-1, keepdims=True)
        o_r = ((o_r - mu) * lax.rsqrt(var + EPS)).reshape(B, S, RET_V) * g_ret_norm[l].astype(jnp.float32)
        o_r = (jax.nn.silu(gt_r.astype(jnp.float32)) * o_r).astype(x.dtype)
        y_r = o_r @ w_ret_o[l]

        q_f = rmsnorm(q_f.reshape(B, S, FOX_HEADS, FOX_DH), g_fox_q[l])
        k_f = rmsnorm(k_f.reshape(B, S, FOX_HEADS, FOX_DH), g_fox_k[l])
        v_f = v_f.reshape(B, S, FOX_HEADS, FOX_DH)
        log_f = jax.nn.log_sigmoid(f_f.astype(jnp.float32) + b_forget[l].astype(jnp.float32))
        o_f = forgetting_attention(q_f, k_f, v_f, log_f).reshape(B, S, FOX_W).astype(x.dtype)
        y_f = o_f @ w_fox_o[l]

        merged = (jax.nn.sigmoid(a_r.astype(jnp.float32)) * y_r.astype(jnp.float32)
                  + jax.nn.sigmoid(a_f.astype(jnp.float32)) * y_f.astype(jnp.float32)).astype(x.dtype)
        x = x + merged @ w_out[l]

        h2 = rmsnorm(x, g_ffn[l])
        ff = (jax.nn.silu(h2 @ w_gate[l]) * (h2 @ w_up[l])) @ w_down[l]
        x = x + ff
    return x


import jax as _jax
import jax.numpy as _jnp

TWIN_FORMAT = 'train_step'
FWD_PARAMS = ['x', 'g_mix', 'w_in', 'b_forget', 'g_ret_norm', 'w_ret_o', 'g_fox_q', 'g_fox_k', 'w_fox_o', 'w_out', 'g_ffn', 'w_gate', 'w_up', 'w_down']
TWIN_WEIGHTS = ['g_mix', 'w_in', 'b_forget', 'g_ret_norm', 'w_ret_o', 'g_fox_q', 'g_fox_k', 'w_fox_o', 'w_out', 'g_ffn', 'w_gate', 'w_up', 'w_down']
TWIN_DIFF_INPUT = 'x'
TWIN_INPUTS = ['x', 'g_mix', 'w_in', 'b_forget', 'g_ret_norm', 'w_ret_o', 'g_fox_q', 'g_fox_k', 'w_fox_o', 'w_out', 'g_ffn', 'w_gate', 'w_up', 'w_down', 'loss_target', 'm_g_mix', 'm_w_in', 'm_b_forget', 'm_g_ret_norm', 'm_w_ret_o', 'm_g_fox_q', 'm_g_fox_k', 'm_w_fox_o', 'm_w_out', 'm_g_ffn', 'm_w_gate', 'm_w_up', 'm_w_down', 'v_g_mix', 'v_w_in', 'v_b_forget', 'v_g_ret_norm', 'v_w_ret_o', 'v_g_fox_q', 'v_g_fox_k', 'v_w_fox_o', 'v_w_out', 'v_g_ffn', 'v_w_gate', 'v_w_up', 'v_w_down']
TWIN_OUTPUTS = ['loss', 'grad_x', 'grad_g_mix', 'grad_w_in', 'grad_b_forget', 'grad_g_ret_norm', 'grad_w_ret_o', 'grad_g_fox_q', 'grad_g_fox_k', 'grad_w_fox_o', 'grad_w_out', 'grad_g_ffn', 'grad_w_gate', 'grad_w_up', 'grad_w_down', 'delta_g_mix', 'delta_w_in', 'delta_b_forget', 'delta_g_ret_norm', 'delta_w_ret_o', 'delta_g_fox_q', 'delta_g_fox_k', 'delta_w_fox_o', 'delta_w_out', 'delta_g_ffn', 'delta_w_gate', 'delta_w_up', 'delta_w_down', 'new_m_g_mix', 'new_m_w_in', 'new_m_b_forget', 'new_m_g_ret_norm', 'new_m_w_ret_o', 'new_m_g_fox_q', 'new_m_g_fox_k', 'new_m_w_fox_o', 'new_m_w_out', 'new_m_g_ffn', 'new_m_w_gate', 'new_m_w_up', 'new_m_w_down', 'new_v_g_mix', 'new_v_w_in', 'new_v_b_forget', 'new_v_g_ret_norm', 'new_v_w_ret_o', 'new_v_g_fox_q', 'new_v_g_fox_k', 'new_v_w_fox_o', 'new_v_w_out', 'new_v_g_ffn', 'new_v_w_gate', 'new_v_w_up', 'new_v_w_down']
TWIN_LEAF_KINDS = {'loss': 'loss', 'grad_x': 'grad_x', 'grad_g_mix': 'grad_w', 'grad_w_in': 'grad_w', 'grad_b_forget': 'grad_w', 'grad_g_ret_norm': 'grad_w', 'grad_w_ret_o': 'grad_w', 'grad_g_fox_q': 'grad_w', 'grad_g_fox_k': 'grad_w', 'grad_w_fox_o': 'grad_w', 'grad_w_out': 'grad_w', 'grad_g_ffn': 'grad_w', 'grad_w_gate': 'grad_w', 'grad_w_up': 'grad_w', 'grad_w_down': 'grad_w', 'delta_g_mix': 'delta_w', 'delta_w_in': 'delta_w', 'delta_b_forget': 'delta_w', 'delta_g_ret_norm': 'delta_w', 'delta_w_ret_o': 'delta_w', 'delta_g_fox_q': 'delta_w', 'delta_g_fox_k': 'delta_w', 'delta_w_fox_o': 'delta_w', 'delta_w_out': 'delta_w', 'delta_g_ffn': 'delta_w', 'delta_w_gate': 'delta_w', 'delta_w_up': 'delta_w', 'delta_w_down': 'delta_w', 'new_m_g_mix': 'new_m', 'new_m_w_in': 'new_m', 'new_m_b_forget': 'new_m', 'new_m_g_ret_norm': 'new_m', 'new_m_w_ret_o': 'new_m', 'new_m_g_fox_q': 'new_m', 'new_m_g_fox_k': 'new_m', 'new_m_w_fox_o': 'new_m', 'new_m_w_out': 'new_m', 'new_m_g_ffn': 'new_m', 'new_m_w_gate': 'new_m', 'new_m_w_up': 'new_m', 'new_m_w_down': 'new_m', 'new_v_g_mix': 'new_v', 'new_v_w_in': 'new_v', 'new_v_b_forget': 'new_v', 'new_v_g_ret_norm': 'new_v', 'new_v_w_ret_o': 'new_v', 'new_v_g_fox_q': 'new_v', 'new_v_g_fox_k': 'new_v', 'new_v_w_fox_o': 'new_v', 'new_v_w_out': 'new_v', 'new_v_g_ffn': 'new_v', 'new_v_w_gate': 'new_v', 'new_v_w_up': 'new_v', 'new_v_w_down': 'new_v'}


def _forward(args):
    return _fwd_reference(*[args[k] for k in FWD_PARAMS])


def _output_shape():
    out = _jax.eval_shape(lambda: _forward(_fwd_setup_inputs(0)))
    return out.shape, out.dtype

N_MICROBATCH = 1
ADAM_LR = 0.001
ADAM_B1 = 0.9
ADAM_B2 = 0.999
ADAM_EPS = 1e-08
ADAM_WD = 0.01
ADAM_STEP = 10
PER_EXAMPLE_BATCH_AXIS = {'x': 0, 'loss_target': 0}
SHARED_INPUTS = []
_WEIGHT_DTYPES = {'g_mix': _jnp.float32, 'w_in': _jnp.float32, 'b_forget': _jnp.float32, 'g_ret_norm': _jnp.float32, 'w_ret_o': _jnp.float32, 'g_fox_q': _jnp.float32, 'g_fox_k': _jnp.float32, 'w_fox_o': _jnp.float32, 'w_out': _jnp.float32, 'g_ffn': _jnp.float32, 'w_gate': _jnp.float32, 'w_up': _jnp.float32, 'w_down': _jnp.float32}
MOMENT_SCALE = {'g_mix': 1.483625e+01, 'w_in': 2.580954e-01, 'b_forget': 1.656060e+02, 'g_ret_norm': 1.336587e+01, 'w_ret_o': 2.559904e-01, 'g_fox_q': 1.728317e+01, 'g_fox_k': 1.726829e+01, 'w_fox_o': 2.853957e-01, 'w_out': 3.783266e-01, 'g_ffn': 4.928011e+01, 'w_gate': 2.220663e-01, 'w_up': 2.418891e-01, 'w_down': 3.765248e-01}


def _to_microbatches(a, axis):
    t = _jnp.moveaxis(a, axis, 0)
    t = t.reshape((N_MICROBATCH, t.shape[0] // N_MICROBATCH) + t.shape[1:])
    return _jnp.moveaxis(t, 1, axis + 1)


def setup_inputs(seed: int = 0) -> dict:
    inp = _fwd_setup_inputs(seed)
    key = _jax.random.fold_in(_jax.random.key(seed), 7919)
    shape, _ = _output_shape()
    out = dict(inp)
    out["loss_target"] = _jax.random.normal(_jax.random.fold_in(key, 0), shape, _jnp.float32)
    for i, name in enumerate(TWIN_WEIGHTS):
        w = inp[name].astype(_jnp.float32)
        if MOMENT_SCALE is None:
            s = _jnp.sqrt(_jnp.mean(_jnp.square(w)) + 1e-30)
        else:
            s = MOMENT_SCALE[name]
        km, kv = _jax.random.split(_jax.random.fold_in(key, i + 1))
        out[name] = w
        out["m_" + name] = s * _jax.random.normal(km, w.shape, _jnp.float32)
        out["v_" + name] = (s * s) * _jax.random.uniform(kv, w.shape, _jnp.float32, 0.5, 1.5)
    if N_MICROBATCH > 1:
        for name, axis in PER_EXAMPLE_BATCH_AXIS.items():
            out[name] = _to_microbatches(out[name], axis)
    return {'x': out['x'], 'g_mix': out['g_mix'], 'w_in': out['w_in'], 'b_forget': out['b_forget'], 'g_ret_norm': out['g_ret_norm'], 'w_ret_o': out['w_ret_o'], 'g_fox_q': out['g_fox_q'], 'g_fox_k': out['g_fox_k'], 'w_fox_o': out['w_fox_o'], 'w_out': out['w_out'], 'g_ffn': out['g_ffn'], 'w_gate': out['w_gate'], 'w_up': out['w_up'], 'w_down': out['w_down'], 'loss_target': out['loss_target'], 'm_g_mix': out['m_g_mix'], 'm_w_in': out['m_w_in'], 'm_b_forget': out['m_b_forget'], 'm_g_ret_norm': out['m_g_ret_norm'], 'm_w_ret_o': out['m_w_ret_o'], 'm_g_fox_q': out['m_g_fox_q'], 'm_g_fox_k': out['m_g_fox_k'], 'm_w_fox_o': out['m_w_fox_o'], 'm_w_out': out['m_w_out'], 'm_g_ffn': out['m_g_ffn'], 'm_w_gate': out['m_w_gate'], 'm_w_up': out['m_w_up'], 'm_w_down': out['m_w_down'], 'v_g_mix': out['v_g_mix'], 'v_w_in': out['v_w_in'], 'v_b_forget': out['v_b_forget'], 'v_g_ret_norm': out['v_g_ret_norm'], 'v_w_ret_o': out['v_w_ret_o'], 'v_g_fox_q': out['v_g_fox_q'], 'v_g_fox_k': out['v_g_fox_k'], 'v_w_fox_o': out['v_w_fox_o'], 'v_w_out': out['v_w_out'], 'v_g_ffn': out['v_g_ffn'], 'v_w_gate': out['v_w_gate'], 'v_w_up': out['v_w_up'], 'v_w_down': out['v_w_down']}


def _loss(weights, diff, rest, loss_target):
    with _jax.named_scope("forward"):
        args = {**rest, TWIN_DIFF_INPUT: diff, **{k: w.astype(_WEIGHT_DTYPES[k]) for k, w in weights.items()}}
        y = _forward(args)
    with _jax.named_scope("loss_head"):
        err = _jnp.square(y.astype(_jnp.float32) - loss_target)
        return 0.5 * _jnp.sum(_jnp.mean(err, axis=-1)) if err.ndim else 0.5 * err


def _adamw(w, g, m, v):
    m = ADAM_B1 * m + (1.0 - ADAM_B1) * g
    v = ADAM_B2 * v + (1.0 - ADAM_B2) * _jnp.square(g)
    m_hat = m / (1.0 - ADAM_B1 ** ADAM_STEP)
    v_hat = v / (1.0 - ADAM_B2 ** ADAM_STEP)
    delta = -ADAM_LR * (m_hat / (_jnp.sqrt(v_hat) + ADAM_EPS) + ADAM_WD * w)
    return delta, m, v


def reference(x, g_mix, w_in, b_forget, g_ret_norm, w_ret_o, g_fox_q, g_fox_k, w_fox_o, w_out, g_ffn, w_gate, w_up, w_down, loss_target, m_g_mix, m_w_in, m_b_forget, m_g_ret_norm, m_w_ret_o, m_g_fox_q, m_g_fox_k, m_w_fox_o, m_w_out, m_g_ffn, m_w_gate, m_w_up, m_w_down, v_g_mix, v_w_in, v_b_forget, v_g_ret_norm, v_w_ret_o, v_g_fox_q, v_g_fox_k, v_w_fox_o, v_w_out, v_g_ffn, v_w_gate, v_w_up, v_w_down):
    given = dict(x=x, g_mix=g_mix, w_in=w_in, b_forget=b_forget, g_ret_norm=g_ret_norm, w_ret_o=w_ret_o, g_fox_q=g_fox_q, g_fox_k=g_fox_k, w_fox_o=w_fox_o, w_out=w_out, g_ffn=g_ffn, w_gate=w_gate, w_up=w_up, w_down=w_down, loss_target=loss_target, m_g_mix=m_g_mix, m_w_in=m_w_in, m_b_forget=m_b_forget, m_g_ret_norm=m_g_ret_norm, m_w_ret_o=m_w_ret_o, m_g_fox_q=m_g_fox_q, m_g_fox_k=m_g_fox_k, m_w_fox_o=m_w_fox_o, m_w_out=m_w_out, m_g_ffn=m_g_ffn, m_w_gate=m_w_gate, m_w_up=m_w_up, m_w_down=m_w_down, v_g_mix=v_g_mix, v_w_in=v_w_in, v_b_forget=v_b_forget, v_g_ret_norm=v_g_ret_norm, v_w_ret_o=v_w_ret_o, v_g_fox_q=v_g_fox_q, v_g_fox_k=v_g_fox_k, v_w_fox_o=v_w_fox_o, v_w_out=v_w_out, v_g_ffn=v_g_ffn, v_w_gate=v_w_gate, v_w_up=v_w_up, v_w_down=v_w_down)
    weights = {n: given[n] for n in TWIN_WEIGHTS}
    shared = {n: given[n] for n in SHARED_INPUTS}
    per_example = {n: given[n] for n in ['x']}
    grad_fn = _jax.value_and_grad(_loss, argnums=(0, 1))

    def one_microbatch(ex, loss_target):
        ex = dict(ex)
        diff = ex.pop(TWIN_DIFF_INPUT)
        return grad_fn(weights, diff, {**shared, **ex}, loss_target)

    if N_MICROBATCH == 1:
        loss, (grad_w, grad_x) = one_microbatch(per_example, given["loss_target"])
    else:
        def body(carry, xs):
            loss_sum, grad_sum = carry
            l_k, (gw_k, gx_k) = one_microbatch(xs[0], xs[1])
            with _jax.named_scope("update"):
                return (loss_sum + l_k, _jax.tree.map(_jnp.add, grad_sum, gw_k)), gx_k

        init = (_jnp.zeros((), _jnp.float32), _jax.tree.map(_jnp.zeros_like, weights))
        (loss, grad_w), grad_x = _jax.lax.scan(body, init, (per_example, given["loss_target"]))
    with _jax.named_scope("update"):
        delta_w, new_m, new_v = {}, {}, {}
        for n in TWIN_WEIGHTS:
            delta_w[n], new_m[n], new_v[n] = _adamw(weights[n], grad_w[n], given["m_" + n], given["v_" + n])
    return (loss, grad_x, *[grad_w[n] for n in TWIN_WEIGHTS], *[delta_w[n] for n in TWIN_WEIGHTS],
            *[new_m[n] for n in TWIN_WEIGHTS], *[new_v[n] for n in TWIN_WEIGHTS])
```

```python
import functools
import math

import numpy as np
import jax
import jax.numpy as jnp
from jax import lax
from jax.experimental import pallas as pl
from jax.experimental.pallas import tpu as pltpu

F32 = jnp.float32
BF = jnp.bfloat16
MESH = pl.DeviceIdType.MESH

D_MODEL = 1024
D_FF = 2816
N_CHIP = 4
FF_SH = D_FF // N_CHIP
IN_COLS = 5128
IN_SH = IN_COLS // N_CHIP
RET_H, RET_DV = 4, 128
FOX_H, FOX_D = 8, 64
CHUNK = 128
EPS = 1e-6
NEG = -1e30
LANE = 128
C_RET, C_GT, C_FOX, C_A, C_END = 0, 1024, 1536, 3072, 5120
L_CQ, L_CK, L_LSE = 64, 67, 70

ADAM_LR, ADAM_B1, ADAM_B2, ADAM_EPS, ADAM_WD, ADAM_STEP = 0.001, 0.9, 0.999, 1e-08, 0.01, 10
VMEM_BIG = 56 * 1024 * 1024


def _nn(a, b):
    return lax.dot_general(a, b, (((1,), (0,)), ((), ())), preferred_element_type=F32)


def _nt(a, b):
    return lax.dot_general(a, b, (((1,), (1,)), ((), ())), preferred_element_type=F32)


def _tn(a, b):
    return lax.dot_general(a, b, (((0,), (0,)), ((), ())), preferred_element_type=F32)


def _split3(x):
    hi = x.astype(BF)
    r = x - hi.astype(F32)
    mid = r.astype(BF)
    lo = (r - mid.astype(F32)).astype(BF)
    return hi, mid, lo


def _sigmoid(x):
    return 1.0 / (1.0 + jnp.exp(-x))


def _swap32(x):
    lane = lax.broadcasted_iota(jnp.int32, x.shape, 1)
    return jnp.where(lane < 32, pltpu.roll(x, 96, 1), pltpu.roll(x, 32, 1))


def _params(sem, vmem=None):
    return pltpu.CompilerParams(dimension_semantics=sem, vmem_limit_bytes=vmem)


def _row_tile(rows, cap, mult):
    return max(d for d in range(mult, cap + 1, mult) if rows % d == 0)


def _cast_shards(ws):
    n = len(ws)

    def body(*refs):
        for i in range(n):
            refs[n + i][...] = refs[i][...].astype(BF)

    return pl.pallas_call(
        body, name="cast_shards",
        out_shape=[jax.ShapeDtypeStruct(w.shape, BF) for w in ws],
        compiler_params=pltpu.CompilerParams(vmem_limit_bytes=VMEM_BIG),
    )(*ws)


def _assemble_w_in(stack, tr=256):
    def body(s_ref, a_ref, f_ref):
        full = jnp.concatenate([s_ref[k].astype(F32) for k in range(N_CHIP)], axis=-1)
        a_ref[...] = jnp.concatenate([full[:, :3072], full[:, 3080:IN_COLS]], axis=-1).astype(BF)
        f_ref[...] = jnp.concatenate([full[:, 3072:3080], jnp.zeros((tr, LANE - FOX_H), F32)], axis=-1).astype(BF)

    return pl.pallas_call(
        body, name="assemble_w_in", grid=(D_MODEL // tr,),
        in_specs=[pl.BlockSpec((N_CHIP, tr, IN_SH), lambda i: (0, i, 0))],
        out_specs=[pl.BlockSpec((tr, C_END), lambda i: (i, 0)), pl.BlockSpec((tr, LANE), lambda i: (i, 0))],
        out_shape=[jax.ShapeDtypeStruct((D_MODEL, C_END), BF), jax.ShapeDtypeStruct((D_MODEL, LANE), BF)],
        compiler_params=_params(("parallel",), VMEM_BIG),
    )(stack)


def _pack_g_in(g_ret, g_gt, g_fox, g_a, g_ff, tr=256):
    def body(r_ref, t_ref, x_ref, a_ref, f_ref, o_ref):
        full = jnp.concatenate([r_ref[...], t_ref[...], x_ref[...], f_ref[...][:, :FOX_H], a_ref[...]], axis=-1)
        for k in range(N_CHIP):
            o_ref[k] = full[:, k * IN_SH:(k + 1) * IN_SH].astype(BF)

    def spec(w):
        return pl.BlockSpec((tr, w), lambda i: (i, 0))

    return pl.pallas_call(
        body, name="pack_g_in", grid=(D_MODEL // tr,),
        in_specs=[spec(1024), spec(512), spec(1536), spec(2048), spec(LANE)],
        out_specs=pl.BlockSpec((N_CHIP, tr, IN_SH), lambda i: (0, i, 0)),
        out_shape=jax.ShapeDtypeStruct((N_CHIP, D_MODEL, IN_SH), BF),
        compiler_params=_params(("parallel",), VMEM_BIG),
    )(g_ret, g_gt, g_fox, g_a, g_ff)


def _rms_cast(x, g, tm=512):
    T = x.shape[0]

    def body(x_ref, g_ref, o_ref):
        xv = x_ref[...]
        r = lax.rsqrt(jnp.mean(xv * xv, axis=-1, keepdims=True) + EPS)
        o_ref[...] = (xv * r * g_ref[...]).astype(BF)

    return pl.pallas_call(
        body, name="rms_cast", grid=(T // tm,),
        in_specs=[pl.BlockSpec((tm, D_MODEL), lambda i: (i, 0)), pl.BlockSpec((1, D_MODEL), lambda i: (0, 0))],
        out_specs=pl.BlockSpec((tm, D_MODEL), lambda i: (i, 0)),
        out_shape=jax.ShapeDtypeStruct((T, D_MODEL), BF),
        compiler_params=_params(("parallel",)),
    )(x, g)


def _mm_nn(a, b, name, tm=512, tn=1024):
    M, K = a.shape
    N = b.shape[1]
    tn = min(tn, N)

    def body(a_ref, b_ref, o_ref):
        o_ref[...] = _nn(a_ref[...], b_ref[...])

    return pl.pallas_call(
        body, name=name, grid=(N // tn, M // tm),
        in_specs=[pl.BlockSpec((tm, K), lambda j, i: (i, 0)), pl.BlockSpec((K, tn), lambda j, i: (0, j))],
        out_specs=pl.BlockSpec((tm, tn), lambda j, i: (i, j)),
        out_shape=jax.ShapeDtypeStruct((M, N), F32),
        compiler_params=_params(("parallel", "parallel")),
    )(a, b)


def _mm_tn(a, b, name, grid, a_spec, b_spec, o_spec, out_shape, acc_shape):
    nk = grid[-1]

    def body(a_ref, b_ref, o_ref, acc):
        k = pl.program_id(len(grid) - 1)

        @pl.when(k == 0)
        def _():
            acc[...] = jnp.zeros(acc.shape, F32)

        acc[...] += _tn(a_ref[...].astype(BF), b_ref[...].astype(BF))

        @pl.when(k == nk - 1)
        def _():
            o_ref[...] = acc[...].astype(o_ref.dtype)

    return pl.pallas_call(
        body, name=name, grid=grid, in_specs=[a_spec, b_spec], out_specs=o_spec, out_shape=out_shape,
        scratch_shapes=[pltpu.VMEM(acc_shape, F32)],
        compiler_params=_params(("parallel",) * (len(grid) - 1) + ("arbitrary",), VMEM_BIG),
    )(a, b)


def _grad_plain(a, b, name, out_dtype, tk=512, tn=1024):
    T, M = a.shape
    N = b.shape[1]
    tn = min(tn, N)
    return _mm_tn(a, b, name, (N // tn, T // tk),
                  pl.BlockSpec((tk, M), lambda j, k: (k, 0)), pl.BlockSpec((tk, tn), lambda j, k: (k, j)),
                  pl.BlockSpec((M, tn), lambda j, k: (0, j)), jax.ShapeDtypeStruct((M, N), out_dtype), (M, tn))


def _grad_colstack(a, b, name, wcol, tk=512):
    T, M = a.shape
    S = b.shape[1] // wcol
    return _mm_tn(a, b, name, (S, T // tk),
                  pl.BlockSpec((tk, M), lambda s, k: (k, 0)), pl.BlockSpec((tk, wcol), lambda s, k: (k, s)),
                  pl.BlockSpec((None, M, wcol), lambda s, k: (s, 0, 0)),
                  jax.ShapeDtypeStruct((S, M, wcol), BF), (M, wcol))


def _grad_bstack(a, b, name, tk=512):
    T, M = a.shape
    S, _, n = b.shape
    return _mm_tn(a, b, name, (S, T // tk),
                  pl.BlockSpec((tk, M), lambda s, k: (k, 0)), pl.BlockSpec((None, tk, n), lambda s, k: (s, k, 0)),
                  pl.BlockSpec((None, M, n), lambda s, k: (s, 0, 0)),
                  jax.ShapeDtypeStruct((S, M, n), BF), (M, n))


def _grad_astack(a, b, name, tk=512):
    S, T, m = a.shape
    N = b.shape[1]
    return _mm_tn(a, b, name, (S, T // tk),
                  pl.BlockSpec((None, tk, m), lambda s, k: (s, k, 0)), pl.BlockSpec((tk, N), lambda s, k: (k, 0)),
                  pl.BlockSpec((None, m, N), lambda s, k: (s, 0, 0)),
                  jax.ShapeDtypeStruct((S, m, N), BF), (m, N))


def _rope_tables(T):
    half = 32
    pos = jnp.arange(T, dtype=F32)
    inv_freq = 1.0 / (10000.0 ** (jnp.arange(half, dtype=F32) / half))
    ang = pos[:, None] * inv_freq[None, :]
    cos, sin = jnp.cos(ang), jnp.sin(ang)
    z = jnp.zeros((T, 64), F32)
    return jnp.concatenate([cos, cos, z], axis=-1), jnp.concatenate([-sin, sin, z], axis=-1)


def _ret_consts():
    h = np.arange(RET_H, dtype=np.float32)
    log_g = np.log1p(-(np.float32(2.0) ** (-5.0 - h))).astype(np.float32)
    idx = np.arange(CHUNK, dtype=np.float32)
    diff = idx[:, None] - idx[None, :]
    decay = np.where(diff[None] >= 0, np.exp(np.maximum(diff, 0.0)[None] * log_g[:, None, None]), 0.0)
    zeta = np.exp((CHUNK - 1.0 - idx)[None, :] * log_g[:, None])
    xi = np.exp((idx + 1.0)[None, :] * log_g[:, None])
    gc = np.exp(CHUNK * log_g)
    bc = lambda v: np.broadcast_to(v[:, :, None], (RET_H, CHUNK, LANE)).astype(np.float32)
    gcb = np.broadcast_to(gc[:, None, None], (RET_H, CHUNK, LANE)).astype(np.float32)
    return (jnp.asarray(decay.astype(np.float32)), jnp.asarray(bc(zeta)), jnp.asarray(bc(xi)), jnp.asarray(gcb))


def _mix_prep(z_a, z_ff, cos_t, sin_t, b_f, g_q, g_k, tm=256):
    T = z_a.shape[0]

    def body(zqk_ref, zf_ref, zff_ref, cos_ref, sin_ref, b_ref, gq_ref, gk_ref,
             qr_ref, kr_ref, qf_ref, kf_ref, vf_ref, carry):
        i = pl.program_id(0)

        @pl.when(i == 0)
        def _():
            carry[...] = jnp.zeros(carry.shape, F32)

        lane = lax.broadcasted_iota(jnp.int32, (tm, LANE), 1)
        zpad = jnp.zeros((tm, 64), F32)
        cosv, sinv = cos_ref[...], sin_ref[...]
        zqk = zqk_ref[...]
        for h in range(RET_H):
            for src, dst, scale in ((0, qr_ref, 1.0), (256, kr_ref, 0.125)):
                xh = jnp.concatenate([zqk[:, src + 64 * h: src + 64 * h + 64], zpad], axis=-1)
                rot = xh * cosv + _swap32(xh) * sinv
                dst[h] = (rot * scale).astype(BF)

        lf_in = zff_ref[...] + b_ref[...]
        logf = jnp.minimum(lf_in, 0.0) - jnp.log(1.0 + jnp.exp(-jnp.abs(lf_in)))
        row = lax.broadcasted_iota(jnp.int32, (tm, tm), 0)
        col = lax.broadcasted_iota(jnp.int32, (tm, tm), 1)
        tri = (row >= col).astype(BF)
        hi, mid, lo = _split3(logf)
        cs = _nn(tri, hi) + _nn(tri, mid) + _nn(tri, lo) + carry[...]
        carry[...] = cs[tm - 1:tm, :]

        zf = zf_ref[...]
        one = jnp.ones((tm, LANE), F32)
        for h in range(FOX_H):
            c = cs[:, h:h + 1]
            chi, cmid, clo = [t.astype(F32) for t in _split3(c)]
            qh = zf[:, 64 * h:64 * h + 64]
            kh = zf[:, 512 + 64 * h:512 + 64 * h + 64]
            vh = zf[:, 1024 + 64 * h:1024 + 64 * h + 64]
            qn = qh * lax.rsqrt(jnp.mean(qh * qh, axis=-1, keepdims=True) + EPS) * gq_ref[...] * 0.125
            kn = kh * lax.rsqrt(jnp.mean(kh * kh, axis=-1, keepdims=True) + EPS) * gk_ref[...]
            qa = jnp.concatenate([qn, zpad], axis=-1)
            qa = jnp.where(lane == L_CQ, chi, jnp.where(lane == L_CQ + 1, cmid, jnp.where(lane == L_CQ + 2, clo, qa)))
            qa = jnp.where((lane >= L_CK) & (lane < L_CK + 3), one, qa)
            ka = jnp.concatenate([kn, zpad], axis=-1)
            ka = jnp.where(lane == L_CK, -chi, jnp.where(lane == L_CK + 1, -cmid, jnp.where(lane == L_CK + 2, -clo, ka)))
            ka = jnp.where(((lane >= L_CQ) & (lane < L_CQ + 3)) | ((lane >= L_LSE) & (lane < L_LSE + 3)), one, ka)
            va = jnp.concatenate([vh, zpad], axis=-1)
            va = jnp.where((lane >= 64) & (lane < 67), one, va)
            qf_ref[h] = qa.astype(BF)
            kf_ref[h] = ka.astype(BF)
            vf_ref[h] = va.astype(BF)

    hspec4 = pl.BlockSpec((RET_H, tm, LANE), lambda i: (0, i, 0))
    hspec8 = pl.BlockSpec((FOX_H, tm, LANE), lambda i: (0, i, 0))
    small = lambda w: pl.BlockSpec((1, w), lambda i: (0, 0))
    return pl.pallas_call(
        body, name="mix_prep", grid=(T // tm,),
        in_specs=[pl.BlockSpec((tm, 512), lambda i: (i, 0)), pl.BlockSpec((tm, 1536), lambda i: (i, 1)),
                  pl.BlockSpec((tm, LANE), lambda i: (i, 0)), pl.BlockSpec((tm, LANE), lambda i: (i, 0)),
                  pl.BlockSpec((tm, LANE), lambda i: (i, 0)), small(LANE), small(64), small(64)],
        out_specs=[hspec4, hspec4, hspec8, hspec8, hspec8],
        out_shape=[jax.ShapeDtypeStruct((RET_H, T, LANE), BF)] * 2 + [jax.ShapeDtypeStruct((FOX_H, T, LANE), BF)] * 3,
        scratch_shapes=[pltpu.VMEM((1, LANE), F32)],
        compiler_params=_params(("arbitrary",), VMEM_BIG),
    )(z_a, z_a, z_ff, cos_t, sin_t, b_f, g_q, g_k)


def _ret_fwd(qr, kr, z_a, g_ret, consts, tt=512):
    T = z_a.shape[0]
    nch = tt // CHUNK
    decay, zeta, xi, gcb = consts

    def body(q_ref, k_ref, v_ref, gt_ref, g_ref, d_ref, ze_ref, xi_ref, gc_ref, o_ref, u_ref, st_ref, r_sc):
        i = pl.program_id(0)

        @pl.when(i == 0)
        def _():
            r_sc[...] = jnp.zeros(r_sc.shape, F32)

        for c in range(nch):
            rows = slice(c * CHUNK, (c + 1) * CHUNK)
            for h in range(RET_H):
                cols = slice(h * RET_DV, (h + 1) * RET_DV)
                q, k = q_ref[h, rows, :], k_ref[h, rows, :]
                v32 = v_ref[rows, cols]
                r = r_sc[h]
                st_ref[h, rows, :] = r
                s = _nt(q, k) * d_ref[h]
                o = _nn(s.astype(BF), v32.astype(BF)) + _nn(q, r.astype(BF)) * xi_ref[h]
                r_sc[h] = gc_ref[h] * r + _tn(k, (v32 * ze_ref[h]).astype(BF))
                o_ref[rows, cols] = o
                mu = jnp.mean(o, axis=-1, keepdims=True)
                xc = o - mu
                on = xc * lax.rsqrt(jnp.mean(xc * xc, axis=-1, keepdims=True) + EPS)
                gt = gt_ref[rows, cols]
                u_ref[rows, cols] = (gt * _sigmoid(gt) * (on * g_ref[:, cols])).astype(BF)

    hspec = pl.BlockSpec((RET_H, tt, LANE), lambda i: (0, i, 0))
    cspec = pl.BlockSpec((RET_H, CHUNK, LANE), lambda i: (0, 0, 0))
    return pl.pallas_call(
        body, name="ret_fwd", grid=(T // tt,),
        in_specs=[hspec, hspec, pl.BlockSpec((tt, 512), lambda i: (i, 1)), pl.BlockSpec((tt, 512), lambda i: (i, 2)),
                  pl.BlockSpec((1, 512), lambda i: (0, 0)), cspec, cspec, cspec, cspec],
        out_specs=[pl.BlockSpec((tt, 512), lambda i: (i, 0)), pl.BlockSpec((tt, 512), lambda i: (i, 0)), hspec],
        out_shape=[jax.ShapeDtypeStruct((T, 512), F32), jax.ShapeDtypeStruct((T, 512), BF),
                   jax.ShapeDtypeStruct((RET_H, T, LANE), F32)],
        scratch_shapes=[pltpu.VMEM((RET_H, CHUNK, LANE), F32)],
        compiler_params=_params(("arbitrary",), VMEM_BIG),
    )(qr, kr, z_a, z_a, g_ret, decay, zeta, xi, gcb)


def _fox_fwd(q, k, v, tq=512):
    H, T, _ = q.shape

    def body(q_ref, k_ref, v_ref, o_ref, q2_ref, m_sc, acc_sc):
        i = pl.program_id(1)
        qv = q_ref[...]
        m_sc[...] = jnp.full(m_sc.shape, NEG, F32)
        acc_sc[...] = jnp.zeros(acc_sc.shape, F32)

        def step(j, masked):
            off = pl.multiple_of(j * tq, tq)
            s = _nt(qv, k_ref[pl.ds(off, tq), :])
            if masked:
                row = lax.broadcasted_iota(jnp.int32, (tq, tq), 0)
                col = lax.broadcasted_iota(jnp.int32, (tq, tq), 1)
                s = jnp.where(row >= col, s, NEG)
            m_prev = m_sc[...]
            m_new = jnp.maximum(m_prev, jnp.max(s, axis=1, keepdims=True))
            p = jnp.exp(s - m_new)
            acc_sc[...] = jnp.exp(m_prev - m_new) * acc_sc[...] + _nn(p.astype(BF), v_ref[pl.ds(off, tq), :])
            m_sc[...] = m_new

        def loop_body(j, carry):
            step(j, False)
            return carry

        lax.fori_loop(0, i, loop_body, 0)
        step(i, True)

        acc = acc_sc[...]
        l = acc[:, 64:65]
        lane = lax.broadcasted_iota(jnp.int32, (tq, LANE), 1)
        o_ref[...] = jnp.where(lane < 64, acc / l, 0.0)
        hi, mid, lo = _split3(-(m_sc[...] + jnp.log(l)))
        q2_ref[...] = jnp.where(lane == L_LSE, hi, jnp.where(lane == L_LSE + 1, mid, jnp.where(lane == L_LSE + 2, lo, qv)))

    blk = pl.BlockSpec((None, tq, LANE), lambda h, i: (h, i, 0))
    full = pl.BlockSpec((None, T, LANE), lambda h, i: (h, 0, 0))
    return pl.pallas_call(
        body, name="fox_fwd", grid=(H, T // tq),
        in_specs=[blk, full, full], out_specs=[blk, blk],
        out_shape=[jax.ShapeDtypeStruct((H, T, LANE), F32), jax.ShapeDtypeStruct((H, T, LANE), BF)],
        scratch_shapes=[pltpu.VMEM((tq, 1), F32), pltpu.VMEM((tq, LANE), F32)],
        compiler_params=_params(("parallel", "arbitrary"), VMEM_BIG),
    )(q, k, v)


def _merge_out(u_r, o_fox, z_a, x, g_ffn, w_ro, w_fo, w_out, tm=256):
    T = x.shape[0]

    def body(u_ref, of_ref, ar_ref, af_ref, x_ref, g_ref, wro_ref, wfo_ref, wout_ref,
             yr_ref, yf_ref, m_ref, x2_ref, h2_ref, oc_ref):
        u = u_ref[...]
        oc = jnp.concatenate([of_ref[h][:, :FOX_D] for h in range(FOX_H)], axis=-1).astype(BF)
        oc_ref[...] = oc
        yr = jnp.concatenate([_nn(u, wro_ref[k]) for k in range(N_CHIP)], axis=-1)
        yf = jnp.concatenate([_nn(oc, wfo_ref[k]) for k in range(N_CHIP)], axis=-1)
        yr_ref[...] = yr
        yf_ref[...] = yf
        m = (_sigmoid(ar_ref[...]) * yr + _sigmoid(af_ref[...]) * yf).astype(BF)
        m_ref[...] = m
        x2 = x_ref[...]
        for k in range(N_CHIP):
            x2 = x2 + _nn(m[:, 256 * k:256 * k + 256], wout_ref[k])
        x2_ref[...] = x2
        r = lax.rsqrt(jnp.mean(x2 * x2, axis=-1, keepdims=True) + EPS)
        h2_ref[...] = (x2 * r * g_ref[...]).astype(BF)

    row = lambda w: pl.BlockSpec((tm, w), lambda i: (i, 0))
    const = lambda shp: pl.BlockSpec(shp, lambda i: (0,) * len(shp))
    return pl.pallas_call(
        body, name="merge_out", grid=(T // tm,),
        in_specs=[row(512), pl.BlockSpec((FOX_H, tm, LANE), lambda i: (0, i, 0)),
                  pl.BlockSpec((tm, 1024), lambda i: (i, 3)), pl.BlockSpec((tm, 1024), lambda i: (i, 4)),
                  row(1024), const((1, 1024)), const((N_CHIP, 512, 256)), const((N_CHIP, 512, 256)),
                  const((N_CHIP, 256, 1024))],
        out_specs=[row(1024), row(1024), row(1024), row(1024), row(1024), row(512)],
        out_shape=[jax.ShapeDtypeStruct((T, 1024), F32), jax.ShapeDtypeStruct((T, 1024), F32),
                   jax.ShapeDtypeStruct((T, 1024), BF), jax.ShapeDtypeStruct((T, 1024), F32),
                   jax.ShapeDtypeStruct((T, 1024), BF), jax.ShapeDtypeStruct((T, 512), BF)],
        compiler_params=_params(("parallel",), VMEM_BIG),
    )(u_r, o_fox, z_a, z_a, x, g_ffn, w_ro, w_fo, w_out)


def _ffn_fwd(h2, x2, tgt, w_gate, w_up, w_down, tm=512):
    T = h2.shape[0]

    def body(h_ref, x2_ref, t_ref, wg_ref, wu_ref, wd_ref, gp_ref, up_ref, act_ref, dy_ref, ls_ref, acc):
        i, k = pl.program_id(0), pl.program_id(1)

        @pl.when(k == 0)
        def _():
            acc[...] = jnp.zeros(acc.shape, F32)

        @pl.when((i == 0) & (k == 0))
        def _():
            ls_ref[...] = jnp.zeros(ls_ref.shape, F32)

        h = h_ref[...]
        gp = _nn(h, wg_ref[...])
        up = _nn(h, wu_ref[...])
        gp_ref[...] = gp
        up_ref[...] = up
        act = (gp * _sigmoid(gp) * up).astype(BF)
        act_ref[...] = act
        acc[...] += _nn(act, wd_ref[...])

        @pl.when(k == N_CHIP - 1)
        def _():
            err = x2_ref[...] + acc[...] - t_ref[...]
            dy_ref[...] = err * (1.0 / D_MODEL)
            ls_ref[...] += jnp.sum(err * err, axis=0, keepdims=True)

    row = pl.BlockSpec((tm, D_MODEL), lambda i, k: (i, 0))
    hid = pl.BlockSpec((None, tm, FF_SH), lambda i, k: (k, i, 0))
    return pl.pallas_call(
        body, name="ffn_fwd", grid=(T // tm, N_CHIP),
        in_specs=[row, row, row, pl.BlockSpec((None, D_MODEL, FF_SH), lambda i, k: (k, 0, 0)),
                  pl.BlockSpec((None, D_MODEL, FF_SH), lambda i, k: (k, 0, 0)),
                  pl.BlockSpec((None, FF_SH, D_MODEL), lambda i, k: (k, 0, 0))],
        out_specs=[hid, hid, hid, row, pl.BlockSpec((1, D_MODEL), lambda i, k: (0, 0))],
        out_shape=[jax.ShapeDtypeStruct((N_CHIP, T, FF_SH), F32), jax.ShapeDtypeStruct((N_CHIP, T, FF_SH), F32),
                   jax.ShapeDtypeStruct((N_CHIP, T, FF_SH), BF), jax.ShapeDtypeStruct((T, D_MODEL), F32),
                   jax.ShapeDtypeStruct((1, D_MODEL), F32)],
        scratch_shapes=[pltpu.VMEM((tm, D_MODEL), F32)],
        compiler_params=_params(("arbitrary", "arbitrary"), VMEM_BIG),
    )(h2, x2, tgt, w_gate, w_up, w_down)


def _ffn_bwd(dy, gp, up, x2, g_ffn, w_gate, w_up, w_down, tm=512):
    T = dy.shape[0]

    def body(dy_ref, gp_ref, up_ref, x2_ref, g_ref, wg_ref, wu_ref, wd_ref, dgp_ref, dup_ref, dx_ref, dg_ref, acc):
        i, k = pl.program_id(0), pl.program_id(1)

        @pl.when(k == 0)
        def _():
            acc[...] = jnp.zeros(acc.shape, F32)

        @pl.when((i == 0) & (k == 0))
        def _():
            dg_ref[...] = jnp.zeros(dg_ref.shape, F32)

        dy = dy_ref[...]
        dact = _nt(dy.astype(BF), wd_ref[...])
        gp, up = gp_ref[...], up_ref[...]
        sg = _sigmoid(gp)
        dup = (dact * gp * sg).astype(BF)
        dgp = (dact * up * sg * (1.0 + gp * (1.0 - sg))).astype(BF)
        dgp_ref[...] = dgp
        dup_ref[...] = dup
        acc[...] += _nt(dgp, wg_ref[...]) + _nt(dup, wu_ref[...])

        @pl.when(k == N_CHIP - 1)
        def _():
            x2 = x2_ref[...]
            r = lax.rsqrt(jnp.mean(x2 * x2, axis=-1, keepdims=True) + EPS)
            xn = x2 * r
            dh = acc[...]
            dg_ref[...] += jnp.sum(dh * xn, axis=0, keepdims=True)
            dxn = dh * g_ref[...]
            dx_ref[...] = dy + r * (dxn - xn * jnp.mean(dxn * xn, axis=-1, keepdims=True))

    row = pl.BlockSpec((tm, D_MODEL), lambda i, k: (i, 0))
    hid = pl.BlockSpec((None, tm, FF_SH), lambda i, k: (k, i, 0))
    vec = pl.BlockSpec((1, D_MODEL), lambda i, k: (0, 0))
    return pl.pallas_call(
        body, name="ffn_bwd", grid=(T // tm, N_CHIP),
        in_specs=[row, hid, hid, row, vec, pl.BlockSpec((None, D_MODEL, FF_SH), lambda i, k: (k, 0, 0)),
                  pl.BlockSpec((None, D_MODEL, FF_SH), lambda i, k: (k, 0, 0)),
                  pl.BlockSpec((None, FF_SH, D_MODEL), lambda i, k: (k, 0, 0))],
        out_specs=[hid, hid, row, vec],
        out_shape=[jax.ShapeDtypeStruct((N_CHIP, T, FF_SH), BF), jax.ShapeDtypeStruct((N_CHIP, T, FF_SH), BF),
                   jax.ShapeDtypeStruct((T, D_MODEL), F32), jax.ShapeDtypeStruct((1, D_MODEL), F32)],
        scratch_shapes=[pltpu.VMEM((tm, D_MODEL), F32)],
        compiler_params=_params(("arbitrary", "arbitrary"), VMEM_BIG),
    )(dy, gp, up, x2, g_ffn, w_gate, w_up, w_down)


def _out_bwd(dx2, z_a, y_r, y_f, o_raw, o_fox, g_ret, w_ro, w_fo, w_out, tm=256):
    T = dx2.shape[0]

    def body(dx_ref, gt_ref, ar_ref, af_ref, yr_ref, yf_ref, o_ref, of_ref, g_ref, wro_ref, wfo_ref, wout_ref,
             dyr_ref, dyf_ref, dgt_ref, da_ref, do_ref, dof_ref, dg_ref):
        i = pl.program_id(0)

        @pl.when(i == 0)
        def _():
            dg_ref[...] = jnp.zeros(dg_ref.shape, F32)

        dxb = dx_ref[...].astype(BF)
        dm = jnp.concatenate([_nt(dxb, wout_ref[k]) for k in range(N_CHIP)], axis=-1)
        sr, sf = _sigmoid(ar_ref[...]), _sigmoid(af_ref[...])
        dyr = dm * sr
        dyf = dm * sf
        da_ref[:, :1024] = (dyr * yr_ref[...] * (1.0 - sr)).astype(BF)
        da_ref[:, 1024:] = (dyf * yf_ref[...] * (1.0 - sf)).astype(BF)
        dyr = dyr.astype(BF)
        dyf = dyf.astype(BF)
        dyr_ref[...] = dyr
        dyf_ref[...] = dyf
        du = jnp.zeros((tm, 512), F32)
        doc = jnp.zeros((tm, 512), F32)
        for k in range(N_CHIP):
            du = du + _nt(dyr[:, 256 * k:256 * k + 256], wro_ref[k])
            doc = doc + _nt(dyf[:, 256 * k:256 * k + 256], wfo_ref[k])

        for h in range(RET_H):
            cols = slice(h * RET_DV, (h + 1) * RET_DV)
            o = o_ref[:, cols]
            mu = jnp.mean(o, axis=-1, keepdims=True)
            xc = o - mu
            rstd = lax.rsqrt(jnp.mean(xc * xc, axis=-1, keepdims=True) + EPS)
            on = xc * rstd
            g = g_ref[:, cols]
            gt = gt_ref[:, cols]
            sg = _sigmoid(gt)
            duh = du[:, cols]
            dgt_ref[:, cols] = (duh * (on * g) * sg * (1.0 + gt * (1.0 - sg))).astype(BF)
            dog = duh * gt * sg
            dg_ref[:, cols] += jnp.sum(dog * on, axis=0, keepdims=True)
            don = dog * g
            do_ref[:, cols] = rstd * (don - jnp.mean(don, axis=-1, keepdims=True)
                                      - on * jnp.mean(don * on, axis=-1, keepdims=True))

        lane = lax.broadcasted_iota(jnp.int32, (tm, LANE), 1)
        zpad = jnp.zeros((tm, 64), F32)
        for h in range(FOX_H):
            doh = doc[:, 64 * h:64 * h + 64]
            delta = jnp.sum(doh * of_ref[h][:, :FOX_D], axis=-1, keepdims=True)
            hi, mid, lo = [t.astype(F32) for t in _split3(-delta)]
            da = jnp.concatenate([doh, zpad], axis=-1)
            da = jnp.where(lane == 64, hi, jnp.where(lane == 65, mid, jnp.where(lane == 66, lo, da)))
            dof_ref[h] = da.astype(BF)

    row = lambda w: pl.BlockSpec((tm, w), lambda i: (i, 0))
    const = lambda shp: pl.BlockSpec(shp, lambda i: (0,) * len(shp))
    hsp = pl.BlockSpec((FOX_H, tm, LANE), lambda i: (0, i, 0))
    return pl.pallas_call(
        body, name="out_bwd", grid=(T // tm,),
        in_specs=[row(1024), pl.BlockSpec((tm, 512), lambda i: (i, 2)), pl.BlockSpec((tm, 1024), lambda i: (i, 3)),
                  pl.BlockSpec((tm, 1024), lambda i: (i, 4)), row(1024), row(1024), row(512), hsp,
                  const((1, 512)), const((N_CHIP, 512, 256)), const((N_CHIP, 512, 256)), const((N_CHIP, 256, 1024))],
        out_specs=[row(1024), row(1024), row(512), row(2048), row(512), hsp, const((1, 512))],
        out_shape=[jax.ShapeDtypeStruct((T, 1024), BF), jax.ShapeDtypeStruct((T, 1024), BF),
                   jax.ShapeDtypeStruct((T, 512), BF), jax.ShapeDtypeStruct((T, 2048), BF),
                   jax.ShapeDtypeStruct((T, 512), F32), jax.ShapeDtypeStruct((FOX_H, T, LANE), BF),
                   jax.ShapeDtypeStruct((1, 512), F32)],
        compiler_params=_params(("arbitrary",), VMEM_BIG),
    )(dx2, z_a, z_a, z_a, y_r, y_f, o_raw, o_fox, g_ret, w_ro, w_fo, w_out)


def _ret_bwd(d_o, qr, kr, z_a, states, cos_t, sin_t, consts, tt=512):
    T = z_a.shape[0]
    nt = T // tt
    nch = tt // CHUNK
    decay, zeta, xi, gcb = consts

    def body(do_ref, q_ref, k_ref, v_ref, st_ref, cos_ref, sin_ref, d_ref, ze_ref, xi_ref, gc_ref, dz_ref, g_sc):
        i = pl.program_id(0)

        @pl.when(i == 0)
        def _():
            g_sc[...] = jnp.zeros(g_sc.shape, F32)

        for c in reversed(range(nch)):
            rows = slice(c * CHUNK, (c + 1) * CHUNK)
            cosv, sinv = cos_ref[rows, :], sin_ref[rows, :]
            dq_parts, dk_parts = [], []
            for h in range(RET_H):
                cols = slice(h * RET_DV, (h + 1) * RET_DV)
                q, k = q_ref[h, rows, :], k_ref[h, rows, :]
                v32 = v_ref[rows, cols]
                vb = v32.astype(BF)
                r = st_ref[h, rows, :]
                g = g_sc[h]
                gb = g.astype(BF)
                d_o = do_ref[rows, cols]
                dob = d_o.astype(BF)
                dox = (d_o * xi_ref[h]).astype(BF)
                dec = d_ref[h]
                s = (_nt(q, k) * dec).astype(BF)
                ds = (_nt(dob, vb) * dec).astype(BF)
                dv = _tn(s, dob) + ze_ref[h] * _nn(k, gb)
                dq = _nn(ds, k) + _nt(dox, r.astype(BF))
                dk = _tn(ds, q) + _nt((v32 * ze_ref[h]).astype(BF), gb)
                g_sc[h] = gc_ref[h] * g + _tn(q, dox)
                dq_parts.append((dq * cosv - _swap32(dq) * sinv)[:, :64])
                dk_parts.append(((dk * cosv - _swap32(dk) * sinv) * 0.125)[:, :64])
                dz_ref[rows, 512 + h * RET_DV:512 + (h + 1) * RET_DV] = dv.astype(BF)
            dz_ref[rows, 0:256] = jnp.concatenate(dq_parts, axis=-1).astype(BF)
            dz_ref[rows, 256:512] = jnp.concatenate(dk_parts, axis=-1).astype(BF)

    rev = lambda i: nt - 1 - i
    hspec = pl.BlockSpec((RET_H, tt, LANE), lambda i: (0, rev(i), 0))
    cspec = pl.BlockSpec((RET_H, CHUNK, LANE), lambda i: (0, 0, 0))
    tab = pl.BlockSpec((tt, LANE), lambda i: (rev(i), 0))
    return pl.pallas_call(
        body, name="ret_bwd", grid=(nt,),
        in_specs=[pl.BlockSpec((tt, 512), lambda i: (rev(i), 0)), hspec, hspec,
                  pl.BlockSpec((tt, 512), lambda i: (rev(i), 1)), hspec, tab, tab, cspec, cspec, cspec, cspec],
        out_specs=pl.BlockSpec((tt, 1024), lambda i: (rev(i), 0)),
        out_shape=jax.ShapeDtypeStruct((T, 1024), BF),
        scratch_shapes=[pltpu.VMEM((RET_H, CHUNK, LANE), F32)],
        compiler_params=_params(("arbitrary",), VMEM_BIG),
    )(d_o, qr, kr, z_a, states, cos_t, sin_t, decay, zeta, xi, gcb)


def _fox_bwd(q2, k, v, do, tk=512):
    H, T, _ = k.shape
    n = T // tk

    def body(q_ref, do_ref, k_ref, v_ref, dq_ref, dk_ref, dv_ref, dk_sc, dv_sc):
        j = pl.program_id(1)

        @pl.when(j == 0)
        def _():
            dq_ref[...] = jnp.zeros(dq_ref.shape, F32)

        kk, vv = k_ref[...], v_ref[...]
        dk_sc[...] = jnp.zeros(dk_sc.shape, F32)
        dv_sc[...] = jnp.zeros(dv_sc.shape, F32)

        def step(i, masked):
            off = pl.multiple_of(i * tk, tk)
            qq = q_ref[pl.ds(off, tk), :]
            dd = do_ref[pl.ds(off, tk), :]
            p = jnp.exp(_nt(kk, qq))
            if masked:
                krow = lax.broadcasted_iota(jnp.int32, (tk, tk), 0)
                qcol = lax.broadcasted_iota(jnp.int32, (tk, tk), 1)
                p = jnp.where(qcol >= krow, p, 0.0)
            ds = (p * _nt(vv, dd)).astype(BF)
            dv_sc[...] += _nn(p.astype(BF), dd)
            dk_sc[...] += _nn(ds, qq)
            dq_ref[pl.ds(off, tk), :] += _tn(ds, kk)

        step(j, True)

        def loop_body(i, carry):
            step(i, False)
            return carry

        lax.fori_loop(j + 1, n, loop_body, 0)
        dk_ref[...] = dk_sc[...]
        dv_ref[...] = dv_sc[...]

    blk = pl.BlockSpec((None, tk, LANE), lambda h, j: (h, j, 0))
    full = pl.BlockSpec((None, T, LANE), lambda h, j: (h, 0, 0))
    shp = jax.ShapeDtypeStruct((H, T, LANE), F32)
    return pl.pallas_call(
        body, name="fox_bwd", grid=(H, n),
        in_specs=[full, full, blk, blk], out_specs=[full, blk, blk], out_shape=[shp, shp, shp],
        scratch_shapes=[pltpu.VMEM((tk, LANE), F32), pltpu.VMEM((tk, LANE), F32)],
        compiler_params=_params(("arbitrary", "arbitrary"), VMEM_BIG),
    )(q2, do, k, v)


def _fox_post_bwd(dq, dk, dv, z_a, z_ff, b_f, g_q, g_k, tm=256):
    T = z_a.shape[0]
    nt = T // tm

    def body(dq_ref, dk_ref, dv_ref, zf_ref, zff_ref, b_ref, gq_ref, gk_ref,
             dz_ref, dff_ref, dgq_ref, dgk_ref, db_ref, carry):
        i = pl.program_id(0)

        @pl.when(i == 0)
        def _():
            carry[...] = jnp.zeros(carry.shape, F32)
            dgq_ref[...] = jnp.zeros(dgq_ref.shape, F32)
            dgk_ref[...] = jnp.zeros(dgk_ref.shape, F32)
            db_ref[...] = jnp.zeros(db_ref.shape, F32)

        lane = lax.broadcasted_iota(jnp.int32, (tm, LANE), 1)
        zf = zf_ref[...]
        dcm = jnp.zeros((tm, LANE), F32)
        dq_parts, dk_parts, dv_parts = [], [], []
        gq_acc = jnp.zeros((1, 64), F32)
        gk_acc = jnp.zeros((1, 64), F32)
        for h in range(FOX_H):
            dqa, dka = dq_ref[h], dk_ref[h]
            dcm = jnp.where(lane == h, dqa[:, L_CQ:L_CQ + 1] - dka[:, L_CK:L_CK + 1], dcm)
            for src, dya, g_ref, scale, parts in ((0, dqa, gq_ref, 0.125, dq_parts), (512, dka, gk_ref, 1.0, dk_parts)):
                xh = zf[:, src + 64 * h:src + 64 * h + 64]
                r = lax.rsqrt(jnp.mean(xh * xh, axis=-1, keepdims=True) + EPS)
                xn = xh * r
                dy = dya[:, :FOX_D] * scale
                if src == 0:
                    gq_acc = gq_acc + jnp.sum(dy * xn, axis=0, keepdims=True)
                else:
                    gk_acc = gk_acc + jnp.sum(dy * xn, axis=0, keepdims=True)
                dxn = dy * g_ref[...]
                parts.append(r * (dxn - xn * jnp.mean(dxn * xn, axis=-1, keepdims=True)))
            dv_parts.append(dv_ref[h][:, :FOX_D])
        dz_ref[...] = jnp.concatenate(dq_parts + dk_parts + dv_parts, axis=-1).astype(BF)
        zpad = jnp.zeros((1, 64), F32)
        dgq_ref[...] += jnp.concatenate([gq_acc, zpad], axis=-1)
        dgk_ref[...] += jnp.concatenate([gk_acc, zpad], axis=-1)

        row = lax.broadcasted_iota(jnp.int32, (tm, tm), 0)
        col = lax.broadcasted_iota(jnp.int32, (tm, tm), 1)
        tri = (row <= col).astype(BF)
        hi, mid, lo = _split3(dcm)
        dlogf = _nn(tri, hi) + _nn(tri, mid) + _nn(tri, lo) + carry[...]
        carry[...] = dlogf[0:1, :]
        dff = jnp.where(lane < FOX_H, dlogf * _sigmoid(-(zff_ref[...] + b_ref[...])), 0.0)
        dff_ref[...] = dff.astype(BF)
        db_ref[...] += jnp.sum(dff, axis=0, keepdims=True)

    rev = lambda i: nt - 1 - i
    hsp = pl.BlockSpec((FOX_H, tm, LANE), lambda i: (0, rev(i), 0))
    small = lambda w: pl.BlockSpec((1, w), lambda i: (0, 0))
    return pl.pallas_call(
        body, name="fox_post_bwd", grid=(nt,),
        in_specs=[hsp, hsp, hsp, pl.BlockSpec((tm, 1536), lambda i: (rev(i), 1)),
                  pl.BlockSpec((tm, LANE), lambda i: (rev(i), 0)), small(LANE), small(64), small(64)],
        out_specs=[pl.BlockSpec((tm, 1536), lambda i: (rev(i), 0)), pl.BlockSpec((tm, LANE), lambda i: (rev(i), 0)),
                   small(LANE), small(LANE), small(LANE)],
        out_shape=[jax.ShapeDtypeStruct((T, 1536), BF), jax.ShapeDtypeStruct((T, LANE), BF),
                   jax.ShapeDtypeStruct((1, LANE), F32), jax.ShapeDtypeStruct((1, LANE), F32),
                   jax.ShapeDtypeStruct((1, LANE), F32)],
        scratch_shapes=[pltpu.VMEM((1, LANE), F32)],
        compiler_params=_params(("arbitrary",), VMEM_BIG),
    )(dq, dk, dv, z_a, z_ff, b_f, g_q, g_k)


def _in_bwd(dz_ret, dz_gt, dz_fox, dz_a, dz_ff, w_a, w_ff, x, g_mix, dx2, tm=256):
    T = x.shape[0]

    def body(r_ref, t_ref, f_ref, a_ref, ff_ref, wa_ref, wf_ref, x_ref, g_ref, dx2_ref, dx_ref, dg_ref):
        i = pl.program_id(0)

        @pl.when(i == 0)
        def _():
            dg_ref[...] = jnp.zeros(dg_ref.shape, F32)

        dh = (_nt(r_ref[...], wa_ref[:, C_RET:C_GT]) + _nt(t_ref[...], wa_ref[:, C_GT:C_FOX])
              + _nt(f_ref[...], wa_ref[:, C_FOX:C_A]) + _nt(a_ref[...], wa_ref[:, C_A:C_END])
              + _nt(ff_ref[...], wf_ref[...]))
        xv = x_ref[...]
        r = lax.rsqrt(jnp.mean(xv * xv, axis=-1, keepdims=True) + EPS)
        xn = xv * r
        dg_ref[...] += jnp.sum(dh * xn, axis=0, keepdims=True)
        dxn = dh * g_ref[...]
        dx_ref[...] = dx2_ref[...] + r * (dxn - xn * jnp.mean(dxn * xn, axis=-1, keepdims=True))

    row = lambda w: pl.BlockSpec((tm, w), lambda i: (i, 0))
    const = lambda shp: pl.BlockSpec(shp, lambda i: (0,) * len(shp))
    return pl.pallas_call(
        body, name="in_bwd", grid=(T // tm,),
        in_specs=[row(1024), row(512), row(1536), row(2048), row(LANE), const((D_MODEL, C_END)),
                  const((D_MODEL, LANE)), row(1024), const((1, 1024)), row(1024)],
        out_specs=[row(1024), const((1, 1024))],
        out_shape=[jax.ShapeDtypeStruct((T, 1024), F32), jax.ShapeDtypeStruct((1, 1024), F32)],
        compiler_params=_params(("arbitrary",), VMEM_BIG),
    )(dz_ret, dz_gt, dz_fox, dz_a, dz_ff, w_a, w_ff, x, g_mix, dx2)


def _mesh_pos():
    return lax.axis_index("x"), lax.axis_index("y"), lax.axis_index("c")


def _all_gather_shards(shards):
    n = len(shards)

    def body(*refs):
        ins, outs = refs[:n], refs[n:2 * n]
        send_sems, recv_sems, loc_sems = refs[2 * n:]
        x, y, c = _mesh_pos()
        kme = 2 * x + y
        chips = [(1 - x, y), (x, 1 - y), (1 - x, 1 - y)]
        local = [pltpu.make_async_copy(ins[w], outs[w].at[kme], loc_sems.at[w]) for w in range(n)]
        for cp in local:
            cp.start()
        sends = []
        for w in range(n):
            for j, (px, py) in enumerate(chips):
                cp = pltpu.make_async_remote_copy(
                    src_ref=ins[w], dst_ref=outs[w].at[kme], send_sem=send_sems.at[w, j], recv_sem=recv_sems.at[w, j],
                    device_id=(px, py, c), device_id_type=MESH)
                cp.start()
                sends.append(cp)
        for w in range(n):
            for j, (px, py) in enumerate(chips):
                pltpu.make_async_remote_copy(
                    src_ref=ins[w], dst_ref=outs[w].at[2 * px + py], send_sem=send_sems.at[w, j],
                    recv_sem=recv_sems.at[w, j], device_id=(px, py, c), device_id_type=MESH).wait_recv()
        for cp in sends:
            cp.wait_send()
        for cp in local:
            cp.wait()

    anyspec = pl.BlockSpec(memory_space=pl.ANY)
    return pl.pallas_call(
        body, name="all_gather_shards",
        in_specs=[anyspec] * n, out_specs=[anyspec] * n,
        out_shape=[jax.ShapeDtypeStruct((N_CHIP,) + s.shape, s.dtype) for s in shards],
        scratch_shapes=[pltpu.SemaphoreType.DMA((n, 3)), pltpu.SemaphoreType.DMA((n, 3)), pltpu.SemaphoreType.DMA((n,))],
    )(*shards)


def _scatter_partials(stacks, small):
    n = len(stacks)

    def body(*refs):
        ins, sv = refs[:n], refs[n]
        outs, svo = refs[n + 1:2 * n + 1], refs[2 * n + 1]
        send_sems, recv_sems, loc_sems, ssend, srecv, sloc = refs[2 * n + 2:]
        x, y, c = _mesh_pos()
        kme = 2 * x + y
        me = 4 * x + 2 * y + c
        chips = [(1 - x, y), (x, 1 - y), (1 - x, 1 - y)]
        flips = [(b >> 2 & 1, b >> 1 & 1, b & 1) for b in range(1, 8)]
        others = [(1 - x if fx else x, 1 - y if fy else y, 1 - c if fc else c) for fx, fy, fc in flips]
        local = [pltpu.make_async_copy(ins[w].at[kme], outs[w].at[kme], loc_sems.at[w]) for w in range(n)]
        local.append(pltpu.make_async_copy(sv, svo.at[me], sloc))
        for cp in local:
            cp.start()
        sends = []
        for j, (px, py, pc) in enumerate(others):
            cp = pltpu.make_async_remote_copy(
                src_ref=sv, dst_ref=svo.at[me], send_sem=ssend.at[j], recv_sem=srecv.at[j],
                device_id=(px, py, pc), device_id_type=MESH)
            cp.start()
            sends.append(cp)
        for w in range(n):
            for j, (px, py) in enumerate(chips):
                cp = pltpu.make_async_remote_copy(
                    src_ref=ins[w].at[2 * px + py], dst_ref=outs[w].at[kme], send_sem=send_sems.at[w, j],
                    recv_sem=recv_sems.at[w, j], device_id=(px, py, c), device_id_type=MESH)
                cp.start()
                sends.append(cp)
        for j, (px, py, pc) in enumerate(others):
            pltpu.make_async_remote_copy(
                src_ref=sv, dst_ref=svo.at[4 * px + 2 * py + pc], send_sem=ssend.at[j], recv_sem=srecv.at[j],
                device_id=(px, py, pc), device_id_type=MESH).wait_recv()
        for w in range(n):
            for j, (px, py) in enumerate(chips):
                pltpu.make_async_remote_copy(
                    src_ref=ins[w].at[kme], dst_ref=outs[w].at[2 * px + py], send_sem=send_sems.at[w, j],
                    recv_sem=recv_sems.at[w, j], device_id=(px, py, c), device_id_type=MESH).wait_recv()
        for cp in sends:
            cp.wait_send()
        for cp in local:
            cp.wait()

    anyspec = pl.BlockSpec(memory_space=pl.ANY)
    return pl.pallas_call(
        body, name="scatter_partials",
        in_specs=[anyspec] * (n + 1), out_specs=[anyspec] * (n + 1),
        out_shape=[jax.ShapeDtypeStruct(s.shape, s.dtype) for s in stacks]
        + [jax.ShapeDtypeStruct((8,) + small.shape, small.dtype)],
        scratch_shapes=[pltpu.SemaphoreType.DMA((n, 3)), pltpu.SemaphoreType.DMA((n, 3)), pltpu.SemaphoreType.DMA((n,)),
                        pltpu.SemaphoreType.DMA((7,)), pltpu.SemaphoreType.DMA((7,)), pltpu.SemaphoreType.DMA],
    )(*stacks, small)


def _sibling_exchange(arrs):
    n = len(arrs)

    def body(*refs):
        ins, outs = refs[:n], refs[n:2 * n]
        send_sems, recv_sems = refs[2 * n:]
        x, y, c = _mesh_pos()
        cps = [pltpu.make_async_remote_copy(
            src_ref=ins[w], dst_ref=outs[w], send_sem=send_sems.at[w], recv_sem=recv_sems.at[w],
            device_id=(x, y, 1 - c), device_id_type=MESH) for w in range(n)]
        for cp in cps:
            cp.start()
        for cp in cps:
            cp.wait_recv()
        for cp in cps:
            cp.wait_send()

    anyspec = pl.BlockSpec(memory_space=pl.ANY)
    return pl.pallas_call(
        body, name="sibling_exchange",
        in_specs=[anyspec] * n, out_specs=[anyspec] * n,
        out_shape=[jax.ShapeDtypeStruct(a.shape, a.dtype) for a in arrs],
        scratch_shapes=[pltpu.SemaphoreType.DMA((n,)), pltpu.SemaphoreType.DMA((n,))],
    )(*arrs)


def _sum_stack(stack, name):
    _, R, C = stack.shape
    tr = _row_tile(R, 256, 16)

    def body(s_ref, o_ref):
        acc = s_ref[0].astype(F32)
        for k in range(1, N_CHIP):
            acc = acc + s_ref[k].astype(F32)
        o_ref[...] = acc

    return pl.pallas_call(
        body, name=name, grid=(R // tr,),
        in_specs=[pl.BlockSpec((N_CHIP, tr, C), lambda i: (0, i, 0))],
        out_specs=pl.BlockSpec((tr, C), lambda i: (i, 0)),
        out_shape=jax.ShapeDtypeStruct((R, C), F32),
        compiler_params=_params(("parallel",)),
    )(stack)


def _adam_math(w, g, m, v):
    m2 = ADAM_B1 * m + (1.0 - ADAM_B1) * g
    v2 = ADAM_B2 * v + (1.0 - ADAM_B2) * (g * g)
    m_hat = m2 / (1.0 - ADAM_B1 ** ADAM_STEP)
    v_hat = v2 / (1.0 - ADAM_B2 ** ADAM_STEP)
    delta = -ADAM_LR * (m_hat / (jnp.sqrt(v_hat) + ADAM_EPS) + ADAM_WD * w)
    return delta, m2, v2


def _adamw(w, m, v, s0, s1, name):
    R, C = w.shape
    tr = _row_tile(R, 128, 8)

    def body(w_ref, m_ref, v_ref, a_ref, b_ref, g_ref, d_ref, m2_ref, v2_ref):
        g = a_ref[...] + b_ref[...]
        delta, m2, v2 = _adam_math(w_ref[...], g, m_ref[...], v_ref[...])
        g_ref[...] = g
        d_ref[...] = delta
        m2_ref[...] = m2
        v2_ref[...] = v2

    spec = pl.BlockSpec((tr, C), lambda i: (i, 0))
    shp = jax.ShapeDtypeStruct((R, C), F32)
    return pl.pallas_call(
        body, name=name, grid=(R // tr,), in_specs=[spec] * 5, out_specs=[spec] * 4, out_shape=[shp] * 4,
        compiler_params=_params(("parallel",), VMEM_BIG),
    )(w, m, v, s0, s1)


def _adamw_small(w, m, v, gathered):
    def body(w_ref, m_ref, v_ref, s_ref, g_ref, d_ref, m2_ref, v2_ref):
        g = s_ref[0]
        for d in range(1, 8):
            g = g + s_ref[d]
        delta, m2, v2 = _adam_math(w_ref[...], g, m_ref[...], v_ref[...])
        g_ref[...] = g
        d_ref[...] = delta
        m2_ref[...] = m2
        v2_ref[...] = v2

    shp = jax.ShapeDtypeStruct(w.shape, F32)
    return pl.pallas_call(body, name="adamw_small", out_shape=[shp] * 4)(w, m, v, gathered)


SMALL = (("g_mix", 1024), ("g_ffn", 1024), ("g_ret_norm", 512), ("g_fox_q", 64), ("g_fox_k", 64), ("b_forget", 8))
SMALL_W = 3072


def _pack_small(parts):
    cols = []
    for (name, n) in SMALL:
        p = parts[name].reshape(1, -1)[:, :n]
        pad = -n % LANE
        cols.append(jnp.pad(p, ((0, 0), (0, pad))) if pad else p)
    used = sum(c.shape[1] for c in cols)
    cols.append(jnp.zeros((1, SMALL_W - used), F32))
    return jnp.concatenate(cols, axis=1)


def _unpack_small(vec):
    out, off = {}, 0
    for (name, n) in SMALL:
        out[name] = vec[:, off:off + n]
        off += n + (-n % LANE)
    return out


def kernel(x, g_mix, w_in, b_forget, g_ret_norm, w_ret_o, g_fox_q, g_fox_k, w_fox_o, w_out, g_ffn, w_gate, w_up, w_down, loss_target, m_g_mix, m_w_in, m_b_forget, m_g_ret_norm, m_w_ret_o, m_g_fox_q, m_g_fox_k, m_w_fox_o, m_w_out, m_g_ffn, m_w_gate, m_w_up, m_w_down, v_g_mix, v_w_in, v_b_forget, v_g_ret_norm, v_w_ret_o, v_g_fox_q, v_g_fox_k, v_w_fox_o, v_w_out, v_g_ffn, v_w_gate, v_w_up, v_w_down):
    T = x.shape[1]
    xs = x[0]
    tgt = loss_target[0]
    big_names = ("w_in", "w_ret_o", "w_fox_o", "w_out", "w_gate", "w_up", "w_down")
    big_w = dict(w_in=w_in[0], w_ret_o=w_ret_o[0], w_fox_o=w_fox_o[0], w_out=w_out[0], w_gate=w_gate[0],
                 w_up=w_up[0], w_down=w_down[0])
    big_m = dict(w_in=m_w_in[0], w_ret_o=m_w_ret_o[0], w_fox_o=m_w_fox_o[0], w_out=m_w_out[0], w_gate=m_w_gate[0],
                 w_up=m_w_up[0], w_down=m_w_down[0])
    big_v = dict(w_in=v_w_in[0], w_ret_o=v_w_ret_o[0], w_fox_o=v_w_fox_o[0], w_out=v_w_out[0], w_gate=v_w_gate[0],
                 w_up=v_w_up[0], w_down=v_w_down[0])
    small_w = dict(g_mix=g_mix, g_ffn=g_ffn, g_ret_norm=g_ret_norm, g_fox_q=g_fox_q, g_fox_k=g_fox_k, b_forget=b_forget)
    small_m = dict(g_mix=m_g_mix, g_ffn=m_g_ffn, g_ret_norm=m_g_ret_norm, g_fox_q=m_g_fox_q, g_fox_k=m_g_fox_k,
                   b_forget=m_b_forget)
    small_v = dict(g_mix=v_g_mix, g_ffn=v_g_ffn, g_ret_norm=v_g_ret_norm, g_fox_q=v_g_fox_q, g_fox_k=v_g_fox_k,
                   b_forget=v_b_forget)

    shards = _cast_shards([big_w[n] for n in big_names])
    s_in, s_ro, s_fo, s_out, s_gate, s_up, s_down = _all_gather_shards(shards)
    w_a, w_ff = _assemble_w_in(s_in)
    b_pad = jnp.pad(b_forget, ((0, 0), (0, LANE - FOX_H)))
    cos_t, sin_t = _rope_tables(T)
    consts = _ret_consts()

    h = _rms_cast(xs, g_mix)
    z_a = _mm_nn(h, w_a, "proj_in")
    z_ff = _mm_nn(h, w_ff, "proj_ff")
    qr, kr, qf, kf, vf = _mix_prep(z_a, z_ff, cos_t, sin_t, b_pad, g_fox_q, g_fox_k)
    o_raw, u_r, states = _ret_fwd(qr, kr, z_a, g_ret_norm, consts)
    o_fox, q2 = _fox_fwd(qf, kf, vf)
    y_r, y_f, mrg, x2, h2, o_cat = _merge_out(u_r, o_fox, z_a, xs, g_ffn, s_ro, s_fo, s_out)
    gp, up, act, dy, loss_vec = _ffn_fwd(h2, x2, tgt, s_gate, s_up, s_down)
    loss = lax.psum(0.5 / D_MODEL * jnp.sum(loss_vec), ("x", "y", "c"))

    dgp, dup, dx2, dg_ffn = _ffn_bwd(dy, gp, up, x2, g_ffn, s_gate, s_up, s_down)
    d_yr, d_yf, dz_gt, dz_a, d_o, do_fox, dg_ret = _out_bwd(dx2, z_a, y_r, y_f, o_raw, o_fox, g_ret_norm,
                                                            s_ro, s_fo, s_out)
    dz_ret = _ret_bwd(d_o, qr, kr, z_a, states, cos_t, sin_t, consts)
    dq_f, dk_f, dv_f = _fox_bwd(q2, kf, vf, do_fox)
    dz_fox, dz_ff, dg_q, dg_k, db_f = _fox_post_bwd(dq_f, dk_f, dv_f, z_a, z_ff, b_pad, g_fox_q, g_fox_k)
    grad_x, dg_mix = _in_bwd(dz_ret, dz_gt, dz_fox, dz_a, dz_ff, w_a, w_ff, xs, g_mix, dx2)

    g_in = _pack_g_in(_grad_plain(h, dz_ret, "gw_in_ret", F32), _grad_plain(h, dz_gt, "gw_in_gt", F32),
                      _grad_plain(h, dz_fox, "gw_in_fox", F32, tn=768), _grad_plain(h, dz_a, "gw_in_a", F32),
                      _grad_plain(h, dz_ff, "gw_in_ff", F32))
    partial = dict(
        w_in=g_in,
        w_ret_o=_grad_colstack(u_r, d_yr, "gw_ret_o", 256),
        w_fox_o=_grad_colstack(o_cat, d_yf, "gw_fox_o", 256),
        w_out=_grad_plain(mrg, dx2, "gw_out", BF).reshape(N_CHIP, 256, D_MODEL),
        w_gate=_grad_bstack(h2, dgp, "gw_gate"),
        w_up=_grad_bstack(h2, dup, "gw_up"),
        w_down=_grad_astack(act, dy, "gw_down"),
    )
    small_g = _pack_small(dict(g_mix=dg_mix, g_ffn=dg_ffn, g_ret_norm=dg_ret, g_fox_q=dg_q, g_fox_k=dg_k, b_forget=db_f))

    recv = _scatter_partials([partial[n] for n in big_names], small_g)
    sums = [_sum_stack(recv[i], "sum_" + n) for i, n in enumerate(big_names)]
    sib = _sibling_exchange(sums)
    big_out = {n: _adamw(big_w[n], big_m[n], big_v[n], sums[i], sib[i], "adamw_" + n) for i, n in enumerate(big_names)}
    sg, sd, sm, sv = _adamw_small(_pack_small(small_w), _pack_small(small_m), _pack_small(small_v), recv[-1])
    small_out = [_unpack_small(t) for t in (sg, sd, sm, sv)]

    order = ("g_mix", "w_in", "b_forget", "g_ret_norm", "w_ret_o", "g_fox_q", "g_fox_k", "w_fox_o", "w_out", "g_ffn",
             "w_gate", "w_up", "w_down")
    outs = [loss, grad_x[None]]
    for idx in range(4):
        for n in order:
            outs.append(big_out[n][idx][None] if n in big_out else small_out[idx][n])
    return tuple(outs)
```

```python
import functools
import math

import numpy as np
import jax
import jax.numpy as jnp
from jax import lax
from jax.experimental import pallas as pl
from jax.experimental.pallas import tpu as pltpu

F32 = jnp.float32
BF = jnp.bfloat16
MESH = pl.DeviceIdType.MESH

D_MODEL = 1024
D_FF = 2816
N_CHIP = 4
FF_SH = D_FF // N_CHIP
IN_COLS = 5128
IN_SH = IN_COLS // N_CHIP
RET_H, RET_DV = 4, 128
FOX_H, FOX_D = 8, 64
CHUNK = 128
EPS = 1e-6
NEG = -1e30
LANE = 128
C_RET, C_GT, C_FOX, C_A, C_END = 0, 1024, 1536, 3072, 5120
L_CQ, L_CK, L_LSE, L_MAX = 64, 67, 70, 73

ADAM_LR, ADAM_B1, ADAM_B2, ADAM_EPS, ADAM_WD, ADAM_STEP = 0.001, 0.9, 0.999, 1e-08, 0.01, 10
VMEM_BIG = 56 * 1024 * 1024


def _nn(a, b):
    return lax.dot_general(a, b, (((1,), (0,)), ((), ())), preferred_element_type=F32)


def _nt(a, b):
    return lax.dot_general(a, b, (((1,), (1,)), ((), ())), preferred_element_type=F32)


def _tn(a, b):
    return lax.dot_general(a, b, (((0,), (0,)), ((), ())), preferred_element_type=F32)


def _split3(x):
    hi = x.astype(BF)
    r = x - hi.astype(F32)
    mid = r.astype(BF)
    lo = (r - mid.astype(F32)).astype(BF)
    return hi, mid, lo


def _sigmoid(x):
    return 1.0 / (1.0 + jnp.exp(-x))


def _swap32(x):
    lane = lax.broadcasted_iota(jnp.int32, x.shape, 1)
    return jnp.where(lane < 32, pltpu.roll(x, 96, 1), pltpu.roll(x, 32, 1))


def _params(sem, vmem=None):
    return pltpu.CompilerParams(dimension_semantics=sem, vmem_limit_bytes=vmem)


def _row_tile(rows, cap, mult):
    return max(d for d in range(mult, cap + 1, mult) if rows % d == 0)


def _cast_shards(ws):
    n = len(ws)

    def body(*refs):
        for i in range(n):
            refs[n + i][...] = refs[i][...].astype(BF)

    return pl.pallas_call(
        body, name="cast_shards",
        out_shape=[jax.ShapeDtypeStruct(w.shape, BF) for w in ws],
        compiler_params=pltpu.CompilerParams(vmem_limit_bytes=VMEM_BIG),
    )(*ws)


def _assemble_w_in(stack, tr=256):
    def body(s_ref, a_ref, f_ref):
        full = jnp.concatenate([s_ref[k].astype(F32) for k in range(N_CHIP)], axis=-1)
        a_ref[...] = jnp.concatenate([full[:, :3072], full[:, 3080:IN_COLS]], axis=-1).astype(BF)
        f_ref[...] = jnp.concatenate([full[:, 3072:3080], jnp.zeros((tr, LANE - FOX_H), F32)], axis=-1).astype(BF)

    return pl.pallas_call(
        body, name="assemble_w_in", grid=(D_MODEL // tr,),
        in_specs=[pl.BlockSpec((N_CHIP, tr, IN_SH), lambda i: (0, i, 0))],
        out_specs=[pl.BlockSpec((tr, C_END), lambda i: (i, 0)), pl.BlockSpec((tr, LANE), lambda i: (i, 0))],
        out_shape=[jax.ShapeDtypeStruct((D_MODEL, C_END), BF), jax.ShapeDtypeStruct((D_MODEL, LANE), BF)],
        compiler_params=_params(("parallel",), VMEM_BIG),
    )(stack)


def _pack_g_in(g_ret, g_gt, g_fox, g_a, g_ff, tr=256):
    def body(r_ref, t_ref, x_ref, a_ref, f_ref, o_ref):
        full = jnp.concatenate([r_ref[...], t_ref[...], x_ref[...], f_ref[...][:, :FOX_H], a_ref[...]], axis=-1)
        for k in range(N_CHIP):
            o_ref[k] = full[:, k * IN_SH:(k + 1) * IN_SH].astype(BF)

    def spec(w):
        return pl.BlockSpec((tr, w), lambda i: (i, 0))

    return pl.pallas_call(
        body, name="pack_g_in", grid=(D_MODEL // tr,),
        in_specs=[spec(1024), spec(512), spec(1536), spec(2048), spec(LANE)],
        out_specs=pl.BlockSpec((N_CHIP, tr, IN_SH), lambda i: (0, i, 0)),
        out_shape=jax.ShapeDtypeStruct((N_CHIP, D_MODEL, IN_SH), BF),
        compiler_params=_params(("parallel",), VMEM_BIG),
    )(g_ret, g_gt, g_fox, g_a, g_ff)


def _rms_cast(x, g, tm=512):
    T = x.shape[0]

    def body(x_ref, g_ref, o_ref):
        xv = x_ref[...]
        r = lax.rsqrt(jnp.mean(xv * xv, axis=-1, keepdims=True) + EPS)
        o_ref[...] = (xv * r * g_ref[...]).astype(BF)

    return pl.pallas_call(
        body, name="rms_cast", grid=(T // tm,),
        in_specs=[pl.BlockSpec((tm, D_MODEL), lambda i: (i, 0)), pl.BlockSpec((1, D_MODEL), lambda i: (0, 0))],
        out_specs=pl.BlockSpec((tm, D_MODEL), lambda i: (i, 0)),
        out_shape=jax.ShapeDtypeStruct((T, D_MODEL), BF),
        compiler_params=_params(("parallel",)),
    )(x, g)


def _mm_nn(a, b, name, tm=512, tn=1024):
    M, K = a.shape
    N = b.shape[1]
    tn = min(tn, N)

    def body(a_ref, b_ref, o_ref):
        o_ref[...] = _nn(a_ref[...], b_ref[...])

    return pl.pallas_call(
        body, name=name, grid=(N // tn, M // tm),
        in_specs=[pl.BlockSpec((tm, K), lambda j, i: (i, 0)), pl.BlockSpec((K, tn), lambda j, i: (0, j))],
        out_specs=pl.BlockSpec((tm, tn), lambda j, i: (i, j)),
        out_shape=jax.ShapeDtypeStruct((M, N), F32),
        compiler_params=_params(("parallel", "parallel")),
    )(a, b)


def _mm_tn(a, b, name, grid, a_spec, b_spec, o_spec, out_shape, acc_shape):
    nk = grid[-1]

    def body(a_ref, b_ref, o_ref, acc):
        k = pl.program_id(len(grid) - 1)

        @pl.when(k == 0)
        def _():
            acc[...] = jnp.zeros(acc.shape, F32)

        acc[...] += _tn(a_ref[...].astype(BF), b_ref[...].astype(BF))

        @pl.when(k == nk - 1)
        def _():
            o_ref[...] = acc[...].astype(o_ref.dtype)

    return pl.pallas_call(
        body, name=name, grid=grid, in_specs=[a_spec, b_spec], out_specs=o_spec, out_shape=out_shape,
        scratch_shapes=[pltpu.VMEM(acc_shape, F32)],
        compiler_params=_params(("parallel",) * (len(grid) - 1) + ("arbitrary",), VMEM_BIG),
    )(a, b)


def _grad_plain(a, b, name, out_dtype, tk=512, tn=1024):
    T, M = a.shape
    N = b.shape[1]
    tn = min(tn, N)
    return _mm_tn(a, b, name, (N // tn, T // tk),
                  pl.BlockSpec((tk, M), lambda j, k: (k, 0)), pl.BlockSpec((tk, tn), lambda j, k: (k, j)),
                  pl.BlockSpec((M, tn), lambda j, k: (0, j)), jax.ShapeDtypeStruct((M, N), out_dtype), (M, tn))


def _grad_colstack(a, b, name, wcol, tk=512):
    T, M = a.shape
    S = b.shape[1] // wcol
    return _mm_tn(a, b, name, (S, T // tk),
                  pl.BlockSpec((tk, M), lambda s, k: (k, 0)), pl.BlockSpec((tk, wcol), lambda s, k: (k, s)),
                  pl.BlockSpec((None, M, wcol), lambda s, k: (s, 0, 0)),
                  jax.ShapeDtypeStruct((S, M, wcol), BF), (M, wcol))


def _grad_bstack(a, b, name, tk=512):
    T, M = a.shape
    S, _, n = b.shape
    return _mm_tn(a, b, name, (S, T // tk),
                  pl.BlockSpec((tk, M), lambda s, k: (k, 0)), pl.BlockSpec((None, tk, n), lambda s, k: (s, k, 0)),
                  pl.BlockSpec((None, M, n), lambda s, k: (s, 0, 0)),
                  jax.ShapeDtypeStruct((S, M, n), BF), (M, n))


def _grad_astack(a, b, name, tk=512):
    S, T, m = a.shape
    N = b.shape[1]
    return _mm_tn(a, b, name, (S, T // tk),
                  pl.BlockSpec((None, tk, m), lambda s, k: (s, k, 0)), pl.BlockSpec((tk, N), lambda s, k: (k, 0)),
                  pl.BlockSpec((None, m, N), lambda s, k: (s, 0, 0)),
                  jax.ShapeDtypeStruct((S, m, N), BF), (m, N))


def _rope_tables(T):
    half = 32
    pos = jnp.arange(T, dtype=F32)
    inv_freq = 1.0 / (10000.0 ** (jnp.arange(half, dtype=F32) / half))
    ang = pos[:, None] * inv_freq[None, :]
    cos, sin = jnp.cos(ang), jnp.sin(ang)
    z = jnp.zeros((T, 64), F32)
    return jnp.concatenate([cos, cos, z], axis=-1), jnp.concatenate([-sin, sin, z], axis=-1)


def _ret_consts():
    h = np.arange(RET_H, dtype=np.float32)
    log_g = np.log1p(-(np.float32(2.0) ** (-5.0 - h))).astype(np.float32)
    idx = np.arange(CHUNK, dtype=np.float32)
    diff = idx[:, None] - idx[None, :]
    decay = np.where(diff[None] >= 0, np.exp(np.maximum(diff, 0.0)[None] * log_g[:, None, None]), 0.0)
    zeta = np.exp((CHUNK - 1.0 - idx)[None, :] * log_g[:, None])
    xi = np.exp((idx + 1.0)[None, :] * log_g[:, None])
    gc = np.exp(CHUNK * log_g)
    bc = lambda v: np.broadcast_to(v[:, :, None], (RET_H, CHUNK, LANE)).astype(np.float32)
    gcb = np.broadcast_to(gc[:, None, None], (RET_H, CHUNK, LANE)).astype(np.float32)
    return (jnp.asarray(decay.astype(np.float32)), jnp.asarray(bc(zeta)), jnp.asarray(bc(xi)), jnp.asarray(gcb))


def _mix_prep(z_a, z_ff, cos_t, sin_t, b_f, g_q, g_k, tm=256):
    T = z_a.shape[0]

    def body(zqk_ref, zf_ref, zff_ref, cos_ref, sin_ref, b_ref, gq_ref, gk_ref,
             qr_ref, kr_ref, qf_ref, kf_ref, vf_ref, carry):
        i = pl.program_id(0)

        @pl.when(i == 0)
        def _():
            carry[...] = jnp.zeros(carry.shape, F32)

        lane = lax.broadcasted_iota(jnp.int32, (tm, LANE), 1)
        zpad = jnp.zeros((tm, 64), F32)
        cosv, sinv = cos_ref[...], sin_ref[...]
        zqk = zqk_ref[...]
        for h in range(RET_H):
            for src, dst, scale in ((0, qr_ref, 1.0), (256, kr_ref, 0.125)):
                xh = jnp.concatenate([zqk[:, src + 64 * h: src + 64 * h + 64], zpad], axis=-1)
                rot = xh * cosv + _swap32(xh) * sinv
                dst[h] = (rot * scale).astype(BF)

        lf_in = zff_ref[...] + b_ref[...]
        logf = jnp.minimum(lf_in, 0.0) - jnp.log(1.0 + jnp.exp(-jnp.abs(lf_in)))
        row = lax.broadcasted_iota(jnp.int32, (tm, tm), 0)
        col = lax.broadcasted_iota(jnp.int32, (tm, tm), 1)
        tri = (row >= col).astype(BF)
        hi, mid, lo = _split3(logf)
        cs = _nn(tri, hi) + _nn(tri, mid) + _nn(tri, lo) + carry[...]
        carry[...] = cs[tm - 1:tm, :]

        zf = zf_ref[...]
        one = jnp.ones((tm, LANE), F32)
        for h in range(FOX_H):
            c = cs[:, h:h + 1]
            chi, cmid, clo = [t.astype(F32) for t in _split3(c)]
            qh = zf[:, 64 * h:64 * h + 64]
            kh = zf[:, 512 + 64 * h:512 + 64 * h + 64]
            vh = zf[:, 1024 + 64 * h:1024 + 64 * h + 64]
            qn = qh * lax.rsqrt(jnp.mean(qh * qh, axis=-1, keepdims=True) + EPS) * gq_ref[...] * 0.125
            kn = kh * lax.rsqrt(jnp.mean(kh * kh, axis=-1, keepdims=True) + EPS) * gk_ref[...]
            qa = jnp.concatenate([qn, zpad], axis=-1)
            qa = jnp.where(lane == L_CQ, chi, jnp.where(lane == L_CQ + 1, cmid, jnp.where(lane == L_CQ + 2, clo, qa)))
            qa = jnp.where((lane >= L_CK) & (lane < L_CK + 3), one, qa)
            ka = jnp.concatenate([kn, zpad], axis=-1)
            ka = jnp.where(lane == L_CK, -chi, jnp.where(lane == L_CK + 1, -cmid, jnp.where(lane == L_CK + 2, -clo, ka)))
            ka = jnp.where(((lane >= L_CQ) & (lane < L_CQ + 3)) | ((lane >= L_LSE) & (lane < L_MAX + 3)), one, ka)
            va = jnp.concatenate([vh, zpad], axis=-1)
            va = jnp.where((lane >= 64) & (lane < 67), one, va)
            qf_ref[h] = qa.astype(BF)
            kf_ref[h] = ka.astype(BF)
            vf_ref[h] = va.astype(BF)

    hspec4 = pl.BlockSpec((RET_H, tm, LANE), lambda i: (0, i, 0))
    hspec8 = pl.BlockSpec((FOX_H, tm, LANE), lambda i: (0, i, 0))
    small = lambda w: pl.BlockSpec((1, w), lambda i: (0, 0))
    return pl.pallas_call(
        body, name="mix_prep", grid=(T // tm,),
        in_specs=[pl.BlockSpec((tm, 512), lambda i: (i, 0)), pl.BlockSpec((tm, 1536), lambda i: (i, 1)),
                  pl.BlockSpec((tm, LANE), lambda i: (i, 0)), pl.BlockSpec((tm, LANE), lambda i: (i, 0)),
                  pl.BlockSpec((tm, LANE), lambda i: (i, 0)), small(LANE), small(64), small(64)],
        out_specs=[hspec4, hspec4, hspec8, hspec8, hspec8],
        out_shape=[jax.ShapeDtypeStruct((RET_H, T, LANE), BF)] * 2 + [jax.ShapeDtypeStruct((FOX_H, T, LANE), BF)] * 3,
        scratch_shapes=[pltpu.VMEM((1, LANE), F32)],
        compiler_params=_params(("arbitrary",), VMEM_BIG),
    )(z_a, z_a, z_ff, cos_t, sin_t, b_f, g_q, g_k)


def _ret_fwd(qr, kr, z_a, g_ret, consts, tt=512):
    T = z_a.shape[0]
    nch = tt // CHUNK
    decay, zeta, xi, gcb = consts

    def body(q_ref, k_ref, v_ref, gt_ref, g_ref, d_ref, ze_ref, xi_ref, gc_ref, o_ref, u_ref, st_ref, r_sc):
        i = pl.program_id(0)

        @pl.when(i == 0)
        def _():
            r_sc[...] = jnp.zeros(r_sc.shape, F32)

        for c in range(nch):
            rows = slice(c * CHUNK, (c + 1) * CHUNK)
            for h in range(RET_H):
                cols = slice(h * RET_DV, (h + 1) * RET_DV)
                q, k = q_ref[h, rows, :], k_ref[h, rows, :]
                v32 = v_ref[rows, cols]
                r = r_sc[h]
                st_ref[h, rows, :] = r
                s = _nt(q, k) * d_ref[h]
                o = _nn(s.astype(BF), v32.astype(BF)) + _nn(q, r.astype(BF)) * xi_ref[h]
                r_sc[h] = gc_ref[h] * r + _tn(k, (v32 * ze_ref[h]).astype(BF))
                o_ref[rows, cols] = o
                mu = jnp.mean(o, axis=-1, keepdims=True)
                xc = o - mu
                on = xc * lax.rsqrt(jnp.mean(xc * xc, axis=-1, keepdims=True) + EPS)
                gt = gt_ref[rows, cols]
                u_ref[rows, cols] = (gt * _sigmoid(gt) * (on * g_ref[:, cols])).astype(BF)

    hspec = pl.BlockSpec((RET_H, tt, LANE), lambda i: (0, i, 0))
    cspec = pl.BlockSpec((RET_H, CHUNK, LANE), lambda i: (0, 0, 0))
    return pl.pallas_call(
        body, name="ret_fwd", grid=(T // tt,),
        in_specs=[hspec, hspec, pl.BlockSpec((tt, 512), lambda i: (i, 1)), pl.BlockSpec((tt, 512), lambda i: (i, 2)),
                  pl.BlockSpec((1, 512), lambda i: (0, 0)), cspec, cspec, cspec, cspec],
        out_specs=[pl.BlockSpec((tt, 512), lambda i: (i, 0)), pl.BlockSpec((tt, 512), lambda i: (i, 0)), hspec],
        out_shape=[jax.ShapeDtypeStruct((T, 512), F32), jax.ShapeDtypeStruct((T, 512), BF),
                   jax.ShapeDtypeStruct((RET_H, T, LANE), F32)],
        scratch_shapes=[pltpu.VMEM((RET_H, CHUNK, LANE), F32)],
        compiler_params=_params(("arbitrary",), VMEM_BIG),
    )(qr, kr, z_a, z_a, g_ret, decay, zeta, xi, gcb)


def _fox_fwd(q, k, v, sub=512):
    H, T, _ = q.shape
    tb = 2 * sub

    def body(q_ref, k_ref, v_ref, o_ref, q2_ref, mx_sc, acc_sc):
        i = pl.program_id(1)
        lane = lax.broadcasted_iota(jnp.int32, (sub, LANE), 1)
        row = lax.broadcasted_iota(jnp.int32, (sub, sub), 0)
        col = lax.broadcasted_iota(jnp.int32, (sub, sub), 1)
        causal = row >= col
        qs = [q_ref[0:sub, :], q_ref[sub:tb, :]]
        d0 = pl.multiple_of(i * tb, tb)
        d1 = pl.multiple_of(i * tb + sub, sub)

        def lane_max(s):
            m = s[:, 0:LANE]
            for c in range(1, s.shape[1] // LANE):
                m = jnp.maximum(m, s[:, c * LANE:(c + 1) * LANE])
            return m

        mx_sc[...] = jnp.full(mx_sc.shape, NEG, F32)

        def max_body(j, carry):
            kb = k_ref[pl.ds(pl.multiple_of(j * tb, tb), tb), :]
            for a in range(2):
                mx_sc[a] = jnp.maximum(mx_sc[a], lane_max(_nt(qs[a], kb)))
            return carry

        lax.fori_loop(0, i, max_body, 0)
        k0, k1 = k_ref[pl.ds(d0, sub), :], k_ref[pl.ds(d1, sub), :]
        v0, v1 = v_ref[pl.ds(d0, sub), :], v_ref[pl.ds(d1, sub), :]
        mx = [jnp.maximum(mx_sc[0], lane_max(jnp.where(causal, _nt(qs[0], k0), NEG))),
              jnp.maximum(jnp.maximum(mx_sc[1], lane_max(_nt(qs[1], k0))),
                          lane_max(jnp.where(causal, _nt(qs[1], k1), NEG)))]
        ms = [jnp.max(t, axis=1, keepdims=True) for t in mx]

        def put3(base, first, val):
            hi, mid, lo = _split3(val)
            return jnp.where(lane == first, hi, jnp.where(lane == first + 1, mid, jnp.where(lane == first + 2, lo, base)))

        qm = [put3(qs[a], L_MAX, -ms[a]) for a in range(2)]

        acc_sc[...] = jnp.zeros(acc_sc.shape, F32)

        def acc_body(j, carry):
            off = pl.multiple_of(j * tb, tb)
            kb, vb = k_ref[pl.ds(off, tb), :], v_ref[pl.ds(off, tb), :]
            for a in range(2):
                acc_sc[a] += _nn(jnp.exp(_nt(qm[a], kb)).astype(BF), vb)
            return carry

        lax.fori_loop(0, i, acc_body, 0)

        def pv(qa, kk, vv, masked):
            p = jnp.exp(_nt(qa, kk))
            if masked:
                p = jnp.where(causal, p, 0.0)
            return _nn(p.astype(BF), vv)

        accs = [acc_sc[0] + pv(qm[0], k0, v0, True),
                acc_sc[1] + pv(qm[1], k0, v0, False) + pv(qm[1], k1, v1, True)]
        for a in range(2):
            rows = slice(a * sub, (a + 1) * sub)
            l = accs[a][:, 64:65]
            o_ref[rows, :] = jnp.where(lane < 64, accs[a] / l, 0.0)
            q2_ref[rows, :] = put3(qs[a], L_LSE, -(ms[a] + jnp.log(l)))

    blk = pl.BlockSpec((None, tb, LANE), lambda h, i: (h, i, 0))
    full = pl.BlockSpec((None, T, LANE), lambda h, i: (h, 0, 0))
    return pl.pallas_call(
        body, name="fox_fwd", grid=(H, T // tb),
        in_specs=[blk, full, full], out_specs=[blk, blk],
        out_shape=[jax.ShapeDtypeStruct((H, T, LANE), F32), jax.ShapeDtypeStruct((H, T, LANE), BF)],
        scratch_shapes=[pltpu.VMEM((2, sub, LANE), F32), pltpu.VMEM((2, sub, LANE), F32)],
        compiler_params=_params(("parallel", "arbitrary"), VMEM_BIG),
    )(q, k, v)


def _merge_out(u_r, o_fox, z_a, x, g_ffn, w_ro, w_fo, w_out, tm=256):
    T = x.shape[0]

    def body(u_ref, of_ref, ar_ref, af_ref, x_ref, g_ref, wro_ref, wfo_ref, wout_ref,
             yr_ref, yf_ref, m_ref, x2_ref, h2_ref, oc_ref):
        u = u_ref[...]
        oc = jnp.concatenate([of_ref[h][:, :FOX_D] for h in range(FOX_H)], axis=-1).astype(BF)
        oc_ref[...] = oc
        yr = jnp.concatenate([_nn(u, wro_ref[k]) for k in range(N_CHIP)], axis=-1)
        yf = jnp.concatenate([_nn(oc, wfo_ref[k]) for k in range(N_CHIP)], axis=-1)
        yr_ref[...] = yr
        yf_ref[...] = yf
        m = (_sigmoid(ar_ref[...]) * yr + _sigmoid(af_ref[...]) * yf).astype(BF)
        m_ref[...] = m
        x2 = x_ref[...]
        for k in range(N_CHIP):
            x2 = x2 + _nn(m[:, 256 * k:256 * k + 256], wout_ref[k])
        x2_ref[...] = x2
        r = lax.rsqrt(jnp.mean(x2 * x2, axis=-1, keepdims=True) + EPS)
        h2_ref[...] = (x2 * r * g_ref[...]).astype(BF)

    row = lambda w: pl.BlockSpec((tm, w), lambda i: (i, 0))
    const = lambda shp: pl.BlockSpec(shp, lambda i: (0,) * len(shp))
    return pl.pallas_call(
        body, name="merge_out", grid=(T // tm,),
        in_specs=[row(512), pl.BlockSpec((FOX_H, tm, LANE), lambda i: (0, i, 0)),
                  pl.BlockSpec((tm, 1024), lambda i: (i, 3)), pl.BlockSpec((tm, 1024), lambda i: (i, 4)),
                  row(1024), const((1, 1024)), const((N_CHIP, 512, 256)), const((N_CHIP, 512, 256)),
                  const((N_CHIP, 256, 1024))],
        out_specs=[row(1024), row(1024), row(1024), row(1024), row(1024), row(512)],
        out_shape=[jax.ShapeDtypeStruct((T, 1024), F32), jax.ShapeDtypeStruct((T, 1024), F32),
                   jax.ShapeDtypeStruct((T, 1024), BF), jax.ShapeDtypeStruct((T, 1024), F32),
                   jax.ShapeDtypeStruct((T, 1024), BF), jax.ShapeDtypeStruct((T, 512), BF)],
        compiler_params=_params(("parallel",), VMEM_BIG),
    )(u_r, o_fox, z_a, z_a, x, g_ffn, w_ro, w_fo, w_out)


def _ffn_fwd(h2, x2, tgt, w_gate, w_up, w_down, tm=512):
    T = h2.shape[0]

    def body(h_ref, x2_ref, t_ref, wg_ref, wu_ref, wd_ref, gp_ref, up_ref, act_ref, dy_ref, ls_ref, acc):
        i, k = pl.program_id(0), pl.program_id(1)

        @pl.when(k == 0)
        def _():
            acc[...] = jnp.zeros(acc.shape, F32)

        @pl.when((i == 0) & (k == 0))
        def _():
            ls_ref[...] = jnp.zeros(ls_ref.shape, F32)

        h = h_ref[...]
        gp = _nn(h, wg_ref[...])
        up = _nn(h, wu_ref[...])
        gp_ref[...] = gp
        up_ref[...] = up
        act = (gp * _sigmoid(gp) * up).astype(BF)
        act_ref[...] = act
        acc[...] += _nn(act, wd_ref[...])

        @pl.when(k == N_CHIP - 1)
        def _():
            err = x2_ref[...] + acc[...] - t_ref[...]
            dy_ref[...] = err * (1.0 / D_MODEL)
            ls_ref[...] += jnp.sum(err * err, axis=0, keepdims=True)

    row = pl.BlockSpec((tm, D_MODEL), lambda i, k: (i, 0))
    hid = pl.BlockSpec((None, tm, FF_SH), lambda i, k: (k, i, 0))
    return pl.pallas_call(
        body, name="ffn_fwd", grid=(T // tm, N_CHIP),
        in_specs=[row, row, row, pl.BlockSpec((None, D_MODEL, FF_SH), lambda i, k: (k, 0, 0)),
                  pl.BlockSpec((None, D_MODEL, FF_SH), lambda i, k: (k, 0, 0)),
                  pl.BlockSpec((None, FF_SH, D_MODEL), lambda i, k: (k, 0, 0))],
        out_specs=[hid, hid, hid, row, pl.BlockSpec((1, D_MODEL), lambda i, k: (0, 0))],
        out_shape=[jax.ShapeDtypeStruct((N_CHIP, T, FF_SH), F32), jax.ShapeDtypeStruct((N_CHIP, T, FF_SH), F32),
                   jax.ShapeDtypeStruct((N_CHIP, T, FF_SH), BF), jax.ShapeDtypeStruct((T, D_MODEL), F32),
                   jax.ShapeDtypeStruct((1, D_MODEL), F32)],
        scratch_shapes=[pltpu.VMEM((tm, D_MODEL), F32)],
        compiler_params=_params(("arbitrary", "arbitrary"), VMEM_BIG),
    )(h2, x2, tgt, w_gate, w_up, w_down)


def _ffn_bwd(dy, gp, up, x2, g_ffn, w_gate, w_up, w_down, tm=512):
    T = dy.shape[0]

    def body(dy_ref, gp_ref, up_ref, x2_ref, g_ref, wg_ref, wu_ref, wd_ref, dgp_ref, dup_ref, dx_ref, dg_ref, acc):
        i, k = pl.program_id(0), pl.program_id(1)

        @pl.when(k == 0)
        def _():
            acc[...] = jnp.zeros(acc.shape, F32)

        @pl.when((i == 0) & (k == 0))
        def _():
            dg_ref[...] = jnp.zeros(dg_ref.shape, F32)

        dy = dy_ref[...]
        dact = _nt(dy.astype(BF), wd_ref[...])
        gp, up = gp_ref[...], up_ref[...]
        sg = _sigmoid(gp)
        dup = (dact * gp * sg).astype(BF)
        dgp = (dact * up * sg * (1.0 + gp * (1.0 - sg))).astype(BF)
        dgp_ref[...] = dgp
        dup_ref[...] = dup
        acc[...] += _nt(dgp, wg_ref[...]) + _nt(dup, wu_ref[...])

        @pl.when(k == N_CHIP - 1)
        def _():
            x2 = x2_ref[...]
            r = lax.rsqrt(jnp.mean(x2 * x2, axis=-1, keepdims=True) + EPS)
            xn = x2 * r
            dh = acc[...]
            dg_ref[...] += jnp.sum(dh * xn, axis=0, keepdims=True)
            dxn = dh * g_ref[...]
            dx_ref[...] = dy + r * (dxn - xn * jnp.mean(dxn * xn, axis=-1, keepdims=True))

    row = pl.BlockSpec((tm, D_MODEL), lambda i, k: (i, 0))
    hid = pl.BlockSpec((None, tm, FF_SH), lambda i, k: (k, i, 0))
    vec = pl.BlockSpec((1, D_MODEL), lambda i, k: (0, 0))
    return pl.pallas_call(
        body, name="ffn_bwd", grid=(T // tm, N_CHIP),
        in_specs=[row, hid, hid, row, vec, pl.BlockSpec((None, D_MODEL, FF_SH), lambda i, k: (k, 0, 0)),
                  pl.BlockSpec((None, D_MODEL, FF_SH), lambda i, k: (k, 0, 0)),
                  pl.BlockSpec((None, FF_SH, D_MODEL), lambda i, k: (k, 0, 0))],
        out_specs=[hid, hid, row, vec],
        out_shape=[jax.ShapeDtypeStruct((N_CHIP, T, FF_SH), BF), jax.ShapeDtypeStruct((N_CHIP, T, FF_SH), BF),
                   jax.ShapeDtypeStruct((T, D_MODEL), F32), jax.ShapeDtypeStruct((1, D_MODEL), F32)],
        scratch_shapes=[pltpu.VMEM((tm, D_MODEL), F32)],
        compiler_params=_params(("arbitrary", "arbitrary"), VMEM_BIG),
    )(dy, gp, up, x2, g_ffn, w_gate, w_up, w_down)


def _out_bwd(dx2, z_a, y_r, y_f, o_raw, o_fox, g_ret, w_ro, w_fo, w_out, tm=256):
    T = dx2.shape[0]

    def body(dx_ref, gt_ref, ar_ref, af_ref, yr_ref, yf_ref, o_ref, of_ref, g_ref, wro_ref, wfo_ref, wout_ref,
             dyr_ref, dyf_ref, dgt_ref, da_ref, do_ref, dof_ref, dg_ref):
        i = pl.program_id(0)

        @pl.when(i == 0)
        def _():
            dg_ref[...] = jnp.zeros(dg_ref.shape, F32)

        dxb = dx_ref[...].astype(BF)
        dm = jnp.concatenate([_nt(dxb, wout_ref[k]) for k in range(N_CHIP)], axis=-1)
        sr, sf = _sigmoid(ar_ref[...]), _sigmoid(af_ref[...])
        dyr = dm * sr
        dyf = dm * sf
        da_ref[:, :1024] = (dyr * yr_ref[...] * (1.0 - sr)).astype(BF)
        da_ref[:, 1024:] = (dyf * yf_ref[...] * (1.0 - sf)).astype(BF)
        dyr = dyr.astype(BF)
        dyf = dyf.astype(BF)
        dyr_ref[...] = dyr
        dyf_ref[...] = dyf
        du = jnp.zeros((tm, 512), F32)
        doc = jnp.zeros((tm, 512), F32)
        for k in range(N_CHIP):
            du = du + _nt(dyr[:, 256 * k:256 * k + 256], wro_ref[k])
            doc = doc + _nt(dyf[:, 256 * k:256 * k + 256], wfo_ref[k])

        for h in range(RET_H):
            cols = slice(h * RET_DV, (h + 1) * RET_DV)
            o = o_ref[:, cols]
            mu = jnp.mean(o, axis=-1, keepdims=True)
            xc = o - mu
            rstd = lax.rsqrt(jnp.mean(xc * xc, axis=-1, keepdims=True) + EPS)
            on = xc * rstd
            g = g_ref[:, cols]
            gt = gt_ref[:, cols]
            sg = _sigmoid(gt)
            duh = du[:, cols]
            dgt_ref[:, cols] = (duh * (on * g) * sg * (1.0 + gt * (1.0 - sg))).astype(BF)
            dog = duh * gt * sg
            dg_ref[:, cols] += jnp.sum(dog * on, axis=0, keepdims=True)
            don = dog * g
            do_ref[:, cols] = rstd * (don - jnp.mean(don, axis=-1, keepdims=True)
                                      - on * jnp.mean(don * on, axis=-1, keepdims=True))

        lane = lax.broadcasted_iota(jnp.int32, (tm, LANE), 1)
        zpad = jnp.zeros((tm, 64), F32)
        for h in range(FOX_H):
            doh = doc[:, 64 * h:64 * h + 64]
            delta = jnp.sum(doh * of_ref[h][:, :FOX_D], axis=-1, keepdims=True)
            hi, mid, lo = [t.astype(F32) for t in _split3(-delta)]
            da = jnp.concatenate([doh, zpad], axis=-1)
            da = jnp.where(lane == 64, hi, jnp.where(lane == 65, mid, jnp.where(lane == 66, lo, da)))
            dof_ref[h] = da.astype(BF)

    row = lambda w: pl.BlockSpec((tm, w), lambda i: (i, 0))
    const = lambda shp: pl.BlockSpec(shp, lambda i: (0,) * len(shp))
    hsp = pl.BlockSpec((FOX_H, tm, LANE), lambda i: (0, i, 0))
    return pl.pallas_call(
        body, name="out_bwd", grid=(T // tm,),
        in_specs=[row(1024), pl.BlockSpec((tm, 512), lambda i: (i, 2)), pl.BlockSpec((tm, 1024), lambda i: (i, 3)),
                  pl.BlockSpec((tm, 1024), lambda i: (i, 4)), row(1024), row(1024), row(512), hsp,
                  const((1, 512)), const((N_CHIP, 512, 256)), const((N_CHIP, 512, 256)), const((N_CHIP, 256, 1024))],
        out_specs=[row(1024), row(1024), row(512), row(2048), row(512), hsp, const((1, 512))],
        out_shape=[jax.ShapeDtypeStruct((T, 1024), BF), jax.ShapeDtypeStruct((T, 1024), BF),
                   jax.ShapeDtypeStruct((T, 512), BF), jax.ShapeDtypeStruct((T, 2048), BF),
                   jax.ShapeDtypeStruct((T, 512), F32), jax.ShapeDtypeStruct((FOX_H, T, LANE), BF),
                   jax.ShapeDtypeStruct((1, 512), F32)],
        compiler_params=_params(("arbitrary",), VMEM_BIG),
    )(dx2, z_a, z_a, z_a, y_r, y_f, o_raw, o_fox, g_ret, w_ro, w_fo, w_out)


def _ret_bwd(d_o, qr, kr, z_a, states, cos_t, sin_t, consts, tt=512):
    T = z_a.shape[0]
    nt = T // tt
    nch = tt // CHUNK
    decay, zeta, xi, gcb = consts

    def body(do_ref, q_ref, k_ref, v_ref, st_ref, cos_ref, sin_ref, d_ref, ze_ref, xi_ref, gc_ref, dz_ref, g_sc):
        i = pl.program_id(0)

        @pl.when(i == 0)
        def _():
            g_sc[...] = jnp.zeros(g_sc.shape, F32)

        for c in reversed(range(nch)):
            rows = slice(c * CHUNK, (c + 1) * CHUNK)
            cosv, sinv = cos_ref[rows, :], sin_ref[rows, :]
            dq_parts, dk_parts = [], []
            for h in range(RET_H):
                cols = slice(h * RET_DV, (h + 1) * RET_DV)
                q, k = q_ref[h, rows, :], k_ref[h, rows, :]
                v32 = v_ref[rows, cols]
                vb = v32.astype(BF)
                r = st_ref[h, rows, :]
                g = g_sc[h]
                gb = g.astype(BF)
                d_o = do_ref[rows, cols]
                dob = d_o.astype(BF)
                dox = (d_o * xi_ref[h]).astype(BF)
                dec = d_ref[h]
                s = (_nt(q, k) * dec).astype(BF)
                ds = (_nt(dob, vb) * dec).astype(BF)
                dv = _tn(s, dob) + ze_ref[h] * _nn(k, gb)
                dq = _nn(ds, k) + _nt(dox, r.astype(BF))
                dk = _tn(ds, q) + _nt((v32 * ze_ref[h]).astype(BF), gb)
                g_sc[h] = gc_ref[h] * g + _tn(q, dox)
                dq_parts.append((dq * cosv - _swap32(dq) * sinv)[:, :64])
                dk_parts.append(((dk * cosv - _swap32(dk) * sinv) * 0.125)[:, :64])
                dz_ref[rows, 512 + h * RET_DV:512 + (h + 1) * RET_DV] = dv.astype(BF)
            dz_ref[rows, 0:256] = jnp.concatenate(dq_parts, axis=-1).astype(BF)
            dz_ref[rows, 256:512] = jnp.concatenate(dk_parts, axis=-1).astype(BF)

    rev = lambda i: nt - 1 - i
    hspec = pl.BlockSpec((RET_H, tt, LANE), lambda i: (0, rev(i), 0))
    cspec = pl.BlockSpec((RET_H, CHUNK, LANE), lambda i: (0, 0, 0))
    tab = pl.BlockSpec((tt, LANE), lambda i: (rev(i), 0))
    return pl.pallas_call(
        body, name="ret_bwd", grid=(nt,),
        in_specs=[pl.BlockSpec((tt, 512), lambda i: (rev(i), 0)), hspec, hspec,
                  pl.BlockSpec((tt, 512), lambda i: (rev(i), 1)), hspec, tab, tab, cspec, cspec, cspec, cspec],
        out_specs=pl.BlockSpec((tt, 1024), lambda i: (rev(i), 0)),
        out_shape=jax.ShapeDtypeStruct((T, 1024), BF),
        scratch_shapes=[pltpu.VMEM((RET_H, CHUNK, LANE), F32)],
        compiler_params=_params(("arbitrary",), VMEM_BIG),
    )(d_o, qr, kr, z_a, states, cos_t, sin_t, decay, zeta, xi, gcb)


def _fox_bwd(q2, k, v, do, sub=512):
    H, T, _ = k.shape
    tb = 2 * sub
    n = T // sub

    def body(q_ref, do_ref, k_ref, v_ref, dq_ref, dk_ref, dv_ref, dk_sc, dv_sc):
        j = pl.program_id(1)

        @pl.when(j == 0)
        def _():
            dq_ref[...] = jnp.zeros(dq_ref.shape, F32)

        kk, vv = k_ref[...], v_ref[...]
        dk_sc[...] = jnp.zeros(dk_sc.shape, F32)
        dv_sc[...] = jnp.zeros(dv_sc.shape, F32)
        krow = lax.broadcasted_iota(jnp.int32, (tb, sub), 0)
        qcol = lax.broadcasted_iota(jnp.int32, (tb, sub), 1)

        def step(i, shift):
            off = pl.multiple_of(i * sub, sub)
            qq = q_ref[pl.ds(off, sub), :]
            dd = do_ref[pl.ds(off, sub), :]
            p = jnp.exp(_nt(kk, qq))
            if shift is not None:
                p = jnp.where(qcol + shift >= krow, p, 0.0)
            ds = (p * _nt(vv, dd)).astype(BF)
            dv_sc[...] += _nn(p.astype(BF), dd)
            dk_sc[...] += _nn(ds, qq)
            dq_ref[pl.ds(off, sub), :] += _tn(ds, kk)

        step(2 * j, 0)
        step(2 * j + 1, sub)

        def loop_body(i, carry):
            step(i, None)
            return carry

        lax.fori_loop(2 * j + 2, n, loop_body, 0)
        dk_ref[...] = dk_sc[...]
        dv_ref[...] = dv_sc[...]

    blk = pl.BlockSpec((None, tb, LANE), lambda h, j: (h, j, 0))
    full = pl.BlockSpec((None, T, LANE), lambda h, j: (h, 0, 0))
    shp = jax.ShapeDtypeStruct((H, T, LANE), F32)
    return pl.pallas_call(
        body, name="fox_bwd", grid=(H, T // tb),
        in_specs=[full, full, blk, blk], out_specs=[full, blk, blk], out_shape=[shp, shp, shp],
        scratch_shapes=[pltpu.VMEM((tb, LANE), F32), pltpu.VMEM((tb, LANE), F32)],
        compiler_params=_params(("arbitrary", "arbitrary"), VMEM_BIG),
    )(q2, do, k, v)


def _fox_post_bwd(dq, dk, dv, z_a, z_ff, b_f, g_q, g_k, tm=256):
    T = z_a.shape[0]
    nt = T // tm

    def body(dq_ref, dk_ref, dv_ref, zf_ref, zff_ref, b_ref, gq_ref, gk_ref,
             dz_ref, dff_ref, dgq_ref, dgk_ref, db_ref, carry):
        i = pl.program_id(0)

        @pl.when(i == 0)
        def _():
            carry[...] = jnp.zeros(carry.shape, F32)
            dgq_ref[...] = jnp.zeros(dgq_ref.shape, F32)
            dgk_ref[...] = jnp.zeros(dgk_ref.shape, F32)
            db_ref[...] = jnp.zeros(db_ref.shape, F32)

        lane = lax.broadcasted_iota(jnp.int32, (tm, LANE), 1)
        zf = zf_ref[...]
        dcm = jnp.zeros((tm, LANE), F32)
        dq_parts, dk_parts, dv_parts = [], [], []
        gq_acc = jnp.zeros((1, 64), F32)
        gk_acc = jnp.zeros((1, 64), F32)
        for h in range(FOX_H):
            dqa, dka = dq_ref[h], dk_ref[h]
            dcm = jnp.where(lane == h, dqa[:, L_CQ:L_CQ + 1] - dka[:, L_CK:L_CK + 1], dcm)
            for src, dya, g_ref, scale, parts in ((0, dqa, gq_ref, 0.125, dq_parts), (512, dka, gk_ref, 1.0, dk_parts)):
                xh = zf[:, src + 64 * h:src + 64 * h + 64]
                r = lax.rsqrt(jnp.mean(xh * xh, axis=-1, keepdims=True) + EPS)
                xn = xh * r
                dy = dya[:, :FOX_D] * scale
                if src == 0:
                    gq_acc = gq_acc + jnp.sum(dy * xn, axis=0, keepdims=True)
                else:
                    gk_acc = gk_acc + jnp.sum(dy * xn, axis=0, keepdims=True)
                dxn = dy * g_ref[...]
                parts.append(r * (dxn - xn * jnp.mean(dxn * xn, axis=-1, keepdims=True)))
            dv_parts.append(dv_ref[h][:, :FOX_D])
        dz_ref[...] = jnp.concatenate(dq_parts + dk_parts + dv_parts, axis=-1).astype(BF)
        zpad = jnp.zeros((1, 64), F32)
        dgq_ref[...] += jnp.concatenate([gq_acc, zpad], axis=-1)
        dgk_ref[...] += jnp.concatenate([gk_acc, zpad], axis=-1)

        row = lax.broadcasted_iota(jnp.int32, (tm, tm), 0)
        col = lax.broadcasted_iota(jnp.int32, (tm, tm), 1)
        tri = (row <= col).astype(BF)
        hi, mid, lo = _split3(dcm)
        dlogf = _nn(tri, hi) + _nn(tri, mid) + _nn(tri, lo) + carry[...]
        carry[...] = dlogf[0:1, :]
        dff = jnp.where(lane < FOX_H, dlogf * _sigmoid(-(zff_ref[...] + b_ref[...])), 0.0)
        dff_ref[...] = dff.astype(BF)
        db_ref[...] += jnp.sum(dff, axis=0, keepdims=True)

    rev = lambda i: nt - 1 - i
    hsp = pl.BlockSpec((FOX_H, tm, LANE), lambda i: (0, rev(i), 0))
    small = lambda w: pl.BlockSpec((1, w), lambda i: (0, 0))
    return pl.pallas_call(
        body, name="fox_post_bwd", grid=(nt,),
        in_specs=[hsp, hsp, hsp, pl.BlockSpec((tm, 1536), lambda i: (rev(i), 1)),
                  pl.BlockSpec((tm, LANE), lambda i: (rev(i), 0)), small(LANE), small(64), small(64)],
        out_specs=[pl.BlockSpec((tm, 1536), lambda i: (rev(i), 0)), pl.BlockSpec((tm, LANE), lambda i: (rev(i), 0)),
                   small(LANE), small(LANE), small(LANE)],
        out_shape=[jax.ShapeDtypeStruct((T, 1536), BF), jax.ShapeDtypeStruct((T, LANE), BF),
                   jax.ShapeDtypeStruct((1, LANE), F32), jax.ShapeDtypeStruct((1, LANE), F32),
                   jax.ShapeDtypeStruct((1, LANE), F32)],
        scratch_shapes=[pltpu.VMEM((1, LANE), F32)],
        compiler_params=_params(("arbitrary",), VMEM_BIG),
    )(dq, dk, dv, z_a, z_ff, b_f, g_q, g_k)


def _in_bwd(dz_ret, dz_gt, dz_fox, dz_a, dz_ff, w_a, w_ff, x, g_mix, dx2, tm=256):
    T = x.shape[0]

    def body(r_ref, t_ref, f_ref, a_ref, ff_ref, wa_ref, wf_ref, x_ref, g_ref, dx2_ref, dx_ref, dg_ref):
        i = pl.program_id(0)

        @pl.when(i == 0)
        def _():
            dg_ref[...] = jnp.zeros(dg_ref.shape, F32)

        dh = (_nt(r_ref[...], wa_ref[:, C_RET:C_GT]) + _nt(t_ref[...], wa_ref[:, C_GT:C_FOX])
              + _nt(f_ref[...], wa_ref[:, C_FOX:C_A]) + _nt(a_ref[...], wa_ref[:, C_A:C_END])
              + _nt(ff_ref[...], wf_ref[...]))
        xv = x_ref[...]
        r = lax.rsqrt(jnp.mean(xv * xv, axis=-1, keepdims=True) + EPS)
        xn = xv * r
        dg_ref[...] += jnp.sum(dh * xn, axis=0, keepdims=True)
        dxn = dh * g_ref[...]
        dx_ref[...] = dx2_ref[...] + r * (dxn - xn * jnp.mean(dxn * xn, axis=-1, keepdims=True))

    row = lambda w: pl.BlockSpec((tm, w), lambda i: (i, 0))
    const = lambda shp: pl.BlockSpec(shp, lambda i: (0,) * len(shp))
    return pl.pallas_call(
        body, name="in_bwd", grid=(T // tm,),
        in_specs=[row(1024), row(512), row(1536), row(2048), row(LANE), const((D_MODEL, C_END)),
                  const((D_MODEL, LANE)), row(1024), const((1, 1024)), row(1024)],
        out_specs=[row(1024), const((1, 1024))],
        out_shape=[jax.ShapeDtypeStruct((T, 1024), F32), jax.ShapeDtypeStruct((1, 1024), F32)],
        compiler_params=_params(("arbitrary",), VMEM_BIG),
    )(dz_ret, dz_gt, dz_fox, dz_a, dz_ff, w_a, w_ff, x, g_mix, dx2)


def _mesh_pos():
    return lax.axis_index("x"), lax.axis_index("y"), lax.axis_index("c")


def _all_gather_shards(shards):
    n = len(shards)

    def body(*refs):
        ins, outs = refs[:n], refs[n:2 * n]
        send_sems, recv_sems, loc_sems = refs[2 * n:]
        x, y, c = _mesh_pos()
        kme = 2 * x + y
        chips = [(1 - x, y), (x, 1 - y), (1 - x, 1 - y)]
        local = [pltpu.make_async_copy(ins[w], outs[w].at[kme], loc_sems.at[w]) for w in range(n)]
        for cp in local:
            cp.start()
        sends = []
        for w in range(n):
            for j, (px, py) in enumerate(chips):
                cp = pltpu.make_async_remote_copy(
                    src_ref=ins[w], dst_ref=outs[w].at[kme], send_sem=send_sems.at[w, j], recv_sem=recv_sems.at[w, j],
                    device_id=(px, py, c), device_id_type=MESH)
                cp.start()
                sends.append(cp)
        for w in range(n):
            for j, (px, py) in enumerate(chips):
                pltpu.make_async_remote_copy(
                    src_ref=ins[w], dst_ref=outs[w].at[2 * px + py], send_sem=send_sems.at[w, j],
                    recv_sem=recv_sems.at[w, j], device_id=(px, py, c), device_id_type=MESH).wait_recv()
        for cp in sends:
            cp.wait_send()
        for cp in local:
            cp.wait()

    anyspec = pl.BlockSpec(memory_space=pl.ANY)
    return pl.pallas_call(
        body, name="all_gather_shards",
        in_specs=[anyspec] * n, out_specs=[anyspec] * n,
        out_shape=[jax.ShapeDtypeStruct((N_CHIP,) + s.shape, s.dtype) for s in shards],
        scratch_shapes=[pltpu.SemaphoreType.DMA((n, 3)), pltpu.SemaphoreType.DMA((n, 3)), pltpu.SemaphoreType.DMA((n,))],
    )(*shards)


def _scatter_partials(stacks, small):
    n = len(stacks)

    def body(*refs):
        ins, sv = refs[:n], refs[n]
        outs, svo = refs[n + 1:2 * n + 1], refs[2 * n + 1]
        send_sems, recv_sems, loc_sems, ssend, srecv, sloc = refs[2 * n + 2:]
        x, y, c = _mesh_pos()
        kme = 2 * x + y
        me = 4 * x + 2 * y + c
        chips = [(1 - x, y), (x, 1 - y), (1 - x, 1 - y)]
        flips = [(b >> 2 & 1, b >> 1 & 1, b & 1) for b in range(1, 8)]
        others = [(1 - x if fx else x, 1 - y if fy else y, 1 - c if fc else c) for fx, fy, fc in flips]
        local = [pltpu.make_async_copy(ins[w].at[kme], outs[w].at[kme], loc_sems.at[w]) for w in range(n)]
        local.append(pltpu.make_async_copy(sv, svo.at[me], sloc))
        for cp in local:
            cp.start()
        sends = []
        for j, (px, py, pc) in enumerate(others):
            cp = pltpu.make_async_remote_copy(
                src_ref=sv, dst_ref=svo.at[me], send_sem=ssend.at[j], recv_sem=srecv.at[j],
                device_id=(px, py, pc), device_id_type=MESH)
            cp.start()
            sends.append(cp)
        for w in range(n):
            for j, (px, py) in enumerate(chips):
                cp = pltpu.make_async_remote_copy(
                    src_ref=ins[w].at[2 * px + py], dst_ref=outs[w].at[kme], send_sem=send_sems.at[w, j],
                    recv_sem=recv_sems.at[w, j], device_id=(px, py, c), device_id_type=MESH)
                cp.start()
                sends.append(cp)
        for j, (px, py, pc) in enumerate(others):
            pltpu.make_async_remote_copy(
                src_ref=sv, dst_ref=svo.at[4 * px + 2 * py + pc], send_sem=ssend.at[j], recv_sem=srecv.at[j],
                device_id=(px, py, pc), device_id_type=MESH).wait_recv()
        for w in range(n):
            for j, (px, py) in enumerate(chips):
                pltpu.make_async_remote_copy(
                    src_ref=ins[w].at[kme], dst_ref=outs[w].at[2 * px + py], send_sem=send_sems.at[w, j],
                    recv_sem=recv_sems.at[w, j], device_id=(px, py, c), device_id_type=MESH).wait_recv()
        for cp in sends:
            cp.wait_send()
        for cp in local:
            cp.wait()

    anyspec = pl.BlockSpec(memory_space=pl.ANY)
    return pl.pallas_call(
        body, name="scatter_partials",
        in_specs=[anyspec] * (n + 1), out_specs=[anyspec] * (n + 1),
        out_shape=[jax.ShapeDtypeStruct(s.shape, s.dtype) for s in stacks]
        + [jax.ShapeDtypeStruct((8,) + small.shape, small.dtype)],
        scratch_shapes=[pltpu.SemaphoreType.DMA((n, 3)), pltpu.SemaphoreType.DMA((n, 3)), pltpu.SemaphoreType.DMA((n,)),
                        pltpu.SemaphoreType.DMA((7,)), pltpu.SemaphoreType.DMA((7,)), pltpu.SemaphoreType.DMA],
    )(*stacks, small)


def _place_own(srcs, name, stacked):
    n = len(srcs)

    def body(*refs):
        ins, outs, sems = refs[:n], refs[n:2 * n], refs[2 * n]
        x, y, _ = _mesh_pos()
        kme = 2 * x + y
        cps = [pltpu.make_async_copy(ins[w].at[kme] if stacked else ins[w], outs[w].at[kme], sems.at[w]) for w in range(n)]
        for cp in cps:
            cp.start()
        for cp in cps:
            cp.wait()

    anyspec = pl.BlockSpec(memory_space=pl.ANY)
    shapes = [s.shape if stacked else (N_CHIP,) + s.shape for s in srcs]
    return pl.pallas_call(
        body, name=name, in_specs=[anyspec] * n, out_specs=[anyspec] * n,
        out_shape=[jax.ShapeDtypeStruct(shp, s.dtype) for shp, s in zip(shapes, srcs)],
        scratch_shapes=[pltpu.SemaphoreType.DMA((n,))],
    )(*srcs)


_HBM_SPEC = pl.BlockSpec(memory_space=pltpu.HBM)
_SEM_SPEC = pl.BlockSpec(memory_space=pltpu.SEMAPHORE)


def _push_copies(src, land, send_sem, recv_sem, stacked, receiving):
    x, y, c = _mesh_pos()
    kme = 2 * x + y
    cps = []
    for w in range(len(src)):
        for j, (px, py) in enumerate([(1 - x, y), (x, 1 - y), (1 - x, 1 - y)]):
            kpeer = 2 * px + py
            cps.append(pltpu.make_async_remote_copy(
                src_ref=src[w].at[kpeer] if stacked else src[w],
                dst_ref=land[w].at[kpeer if receiving else kme],
                send_sem=send_sem.at[3 * w + j], recv_sem=recv_sem.at[3 * w + j],
                device_id=(px, py, c), device_id_type=MESH))
    return cps


def _push_start(srcs, lands, name, stacked):
    n = len(srcs)

    def body(*refs):
        src, land = refs[:n], refs[n:2 * n]
        send_sem, recv_sem = refs[2 * n], refs[2 * n + 1]
        token = refs[-1]
        for cp in _push_copies(src, land, send_sem, recv_sem, stacked, False):
            cp.start()
        token[...] = jnp.zeros(token.shape, F32)

    ops = [pltpu.with_memory_space_constraint(a, pltpu.HBM) for a in list(srcs) + list(lands)]
    res = pl.pallas_call(
        body, name=name,
        out_shape=(pltpu.SemaphoreType.DMA((3 * n,)), pltpu.SemaphoreType.DMA((3 * n,)),
                   *[pltpu.HBM(a.shape, a.dtype) for a in ops], jax.ShapeDtypeStruct((8, LANE), F32)),
        in_specs=[_HBM_SPEC] * (2 * n),
        out_specs=(_SEM_SPEC, _SEM_SPEC, *([_HBM_SPEC] * (2 * n)), pl.BlockSpec(memory_space=pltpu.VMEM)),
        input_output_aliases={i: 2 + i for i in range(2 * n)},
        compiler_params=pltpu.CompilerParams(has_side_effects=pltpu.SideEffectType.DATAFLOW_SIDE_EFFECTING),
    )(*ops)
    return res[0], res[1], list(res[2:2 + n]), list(res[2 + n:2 + 2 * n]), res[-1]


def _push_wait(send_sem, recv_sem, srcs, lands, after, name, stacked):
    n = len(srcs)

    def body(*refs):
        src, land = refs[:n], refs[n:2 * n]
        s_sem, r_sem = refs[2 * n], refs[2 * n + 1]
        for cp in _push_copies(src, land, s_sem, r_sem, stacked, True):
            cp.wait_send()
            cp.wait_recv()

    ops = list(srcs) + list(lands)
    res = pl.pallas_call(
        body, name=name,
        out_shape=tuple(pltpu.HBM(a.shape, a.dtype) for a in ops),
        in_specs=[_HBM_SPEC] * (2 * n) + [_SEM_SPEC, _SEM_SPEC, pl.BlockSpec(memory_space=pl.ANY)],
        out_specs=tuple([_HBM_SPEC] * (2 * n)),
        input_output_aliases={i: i for i in range(2 * n)},
        compiler_params=pltpu.CompilerParams(has_side_effects=pltpu.SideEffectType.DATAFLOW_SIDE_EFFECTING),
    )(*ops, send_sem, recv_sem, after)
    return list(res[n:2 * n])


def _sibling_exchange(arrs):
    n = len(arrs)

    def body(*refs):
        ins, outs = refs[:n], refs[n:2 * n]
        send_sems, recv_sems = refs[2 * n:]
        x, y, c = _mesh_pos()
        cps = [pltpu.make_async_remote_copy(
            src_ref=ins[w], dst_ref=outs[w], send_sem=send_sems.at[w], recv_sem=recv_sems.at[w],
            device_id=(x, y, 1 - c), device_id_type=MESH) for w in range(n)]
        for cp in cps:
            cp.start()
        for cp in cps:
            cp.wait_recv()
        for cp in cps:
            cp.wait_send()

    anyspec = pl.BlockSpec(memory_space=pl.ANY)
    return pl.pallas_call(
        body, name="sibling_exchange",
        in_specs=[anyspec] * n, out_specs=[anyspec] * n,
        out_shape=[jax.ShapeDtypeStruct(a.shape, a.dtype) for a in arrs],
        scratch_shapes=[pltpu.SemaphoreType.DMA((n,)), pltpu.SemaphoreType.DMA((n,))],
    )(*arrs)


def _sum_stack(stack, name):
    _, R, C = stack.shape
    tr = _row_tile(R, 256, 16)

    def body(s_ref, o_ref):
        acc = s_ref[0].astype(F32)
        for k in range(1, N_CHIP):
            acc = acc + s_ref[k].astype(F32)
        o_ref[...] = acc

    return pl.pallas_call(
        body, name=name, grid=(R // tr,),
        in_specs=[pl.BlockSpec((N_CHIP, tr, C), lambda i: (0, i, 0))],
        out_specs=pl.BlockSpec((tr, C), lambda i: (i, 0)),
        out_shape=jax.ShapeDtypeStruct((R, C), F32),
        compiler_params=_params(("parallel",)),
    )(stack)


def _adam_math(w, g, m, v):
    m2 = ADAM_B1 * m + (1.0 - ADAM_B1) * g
    v2 = ADAM_B2 * v + (1.0 - ADAM_B2) * (g * g)
    m_hat = m2 / (1.0 - ADAM_B1 ** ADAM_STEP)
    v_hat = v2 / (1.0 - ADAM_B2 ** ADAM_STEP)
    delta = -ADAM_LR * (m_hat / (jnp.sqrt(v_hat) + ADAM_EPS) + ADAM_WD * w)
    return delta, m2, v2


def _adamw(w, m, v, s0, s1, name):
    R, C = w.shape
    tr = _row_tile(R, 128, 8)

    def body(w_ref, m_ref, v_ref, a_ref, b_ref, g_ref, d_ref, m2_ref, v2_ref):
        g = a_ref[...] + b_ref[...]
        delta, m2, v2 = _adam_math(w_ref[...], g, m_ref[...], v_ref[...])
        g_ref[...] = g
        d_ref[...] = delta
        m2_ref[...] = m2
        v2_ref[...] = v2

    spec = pl.BlockSpec((tr, C), lambda i: (i, 0))
    shp = jax.ShapeDtypeStruct((R, C), F32)
    return pl.pallas_call(
        body, name=name, grid=(R // tr,), in_specs=[spec] * 5, out_specs=[spec] * 4, out_shape=[shp] * 4,
        compiler_params=_params(("parallel",), VMEM_BIG),
    )(w, m, v, s0, s1)


def _adamw_small(w, m, v, gathered):
    def body(w_ref, m_ref, v_ref, s_ref, g_ref, d_ref, m2_ref, v2_ref):
        g = s_ref[0]
        for d in range(1, 8):
            g = g + s_ref[d]
        delta, m2, v2 = _adam_math(w_ref[...], g, m_ref[...], v_ref[...])
        g_ref[...] = g
        d_ref[...] = delta
        m2_ref[...] = m2
        v2_ref[...] = v2

    shp = jax.ShapeDtypeStruct(w.shape, F32)
    return pl.pallas_call(body, name="adamw_small", out_shape=[shp] * 4)(w, m, v, gathered)


SMALL = (("g_mix", 1024), ("g_ffn", 1024), ("g_ret_norm", 512), ("g_fox_q", 64), ("g_fox_k", 64), ("b_forget", 8))
SMALL_W = 3072


def _pack_small(parts):
    cols = []
    for (name, n) in SMALL:
        p = parts[name].reshape(1, -1)[:, :n]
        pad = -n % LANE
        cols.append(jnp.pad(p, ((0, 0), (0, pad))) if pad else p)
    used = sum(c.shape[1] for c in cols)
    cols.append(jnp.zeros((1, SMALL_W - used), F32))
    return jnp.concatenate(cols, axis=1)


def _unpack_small(vec):
    out, off = {}, 0
    for (name, n) in SMALL:
        out[name] = vec[:, off:off + n]
        off += n + (-n % LANE)
    return out


def kernel(x, g_mix, w_in, b_forget, g_ret_norm, w_ret_o, g_fox_q, g_fox_k, w_fox_o, w_out, g_ffn, w_gate, w_up, w_down, loss_target, m_g_mix, m_w_in, m_b_forget, m_g_ret_norm, m_w_ret_o, m_g_fox_q, m_g_fox_k, m_w_fox_o, m_w_out, m_g_ffn, m_w_gate, m_w_up, m_w_down, v_g_mix, v_w_in, v_b_forget, v_g_ret_norm, v_w_ret_o, v_g_fox_q, v_g_fox_k, v_w_fox_o, v_w_out, v_g_ffn, v_w_gate, v_w_up, v_w_down):
    T = x.shape[1]
    xs = x[0]
    tgt = loss_target[0]
    big_names = ("w_in", "w_ret_o", "w_fox_o", "w_out", "w_gate", "w_up", "w_down")
    big_w = dict(w_in=w_in[0], w_ret_o=w_ret_o[0], w_fox_o=w_fox_o[0], w_out=w_out[0], w_gate=w_gate[0],
                 w_up=w_up[0], w_down=w_down[0])
    big_m = dict(w_in=m_w_in[0], w_ret_o=m_w_ret_o[0], w_fox_o=m_w_fox_o[0], w_out=m_w_out[0], w_gate=m_w_gate[0],
                 w_up=m_w_up[0], w_down=m_w_down[0])
    big_v = dict(w_in=v_w_in[0], w_ret_o=v_w_ret_o[0], w_fox_o=v_w_fox_o[0], w_out=v_w_out[0], w_gate=v_w_gate[0],
                 w_up=v_w_up[0], w_down=v_w_down[0])
    small_w = dict(g_mix=g_mix, g_ffn=g_ffn, g_ret_norm=g_ret_norm, g_fox_q=g_fox_q, g_fox_k=g_fox_k, b_forget=b_forget)
    small_m = dict(g_mix=m_g_mix, g_ffn=m_g_ffn, g_ret_norm=m_g_ret_norm, g_fox_q=m_g_fox_q, g_fox_k=m_g_fox_k,
                   b_forget=m_b_forget)
    small_v = dict(g_mix=v_g_mix, g_ffn=v_g_ffn, g_ret_norm=v_g_ret_norm, g_fox_q=v_g_fox_q, g_fox_k=v_g_fox_k,
                   b_forget=v_b_forget)

    shards = _cast_shards([big_w[n] for n in big_names])
    (s_in,) = _all_gather_shards(shards[:1])
    w_send, w_recv, w_src, w_land, w_tok = _push_start(
        shards[1:], _place_own(shards[1:], "place_weights", False), "gather_rest_start", False)
    w_a, w_ff = _assemble_w_in(s_in)
    b_pad = jnp.pad(b_forget, ((0, 0), (0, LANE - FOX_H)))
    cos_t, sin_t = _rope_tables(T)
    consts = _ret_consts()

    h = _rms_cast(xs, g_mix + w_tok[0:1, 0:1])
    z_a = _mm_nn(h, w_a, "proj_in")
    z_ff = _mm_nn(h, w_ff, "proj_ff")
    qr, kr, qf, kf, vf = _mix_prep(z_a, z_ff, cos_t, sin_t, b_pad, g_fox_q, g_fox_k)
    o_raw, u_r, states = _ret_fwd(qr, kr, z_a, g_ret_norm, consts)
    o_fox, q2 = _fox_fwd(qf, kf, vf)
    s_ro, s_fo, s_out, s_gate, s_up, s_down = _push_wait(w_send, w_recv, w_src, w_land, q2, "gather_rest_wait", False)
    y_r, y_f, mrg, x2, h2, o_cat = _merge_out(u_r, o_fox, z_a, xs, g_ffn, s_ro, s_fo, s_out)
    gp, up, act, dy, loss_vec = _ffn_fwd(h2, x2, tgt, s_gate, s_up, s_down)
    loss = lax.psum(0.5 / D_MODEL * jnp.sum(loss_vec), ("x", "y", "c"))

    dgp, dup, dx2, dg_ffn = _ffn_bwd(dy, gp, up, x2, g_ffn, s_gate, s_up, s_down)
    ffn_part = [_grad_bstack(h2, dgp, "gw_gate"), _grad_bstack(h2, dup, "gw_up"), _grad_astack(act, dy, "gw_down")]
    f_send, f_recv, f_src, f_land, f_tok = _push_start(
        ffn_part, _place_own(ffn_part, "place_ffn_grads", True), "scatter_ffn_start", True)
    d_yr, d_yf, dz_gt, dz_a, d_o, do_fox, dg_ret = _out_bwd(dx2, z_a, y_r, y_f, o_raw, o_fox,
                                                            g_ret_norm + f_tok[0:1, 0:1], s_ro, s_fo, s_out)
    dz_ret = _ret_bwd(d_o, qr, kr, z_a, states, cos_t, sin_t, consts)
    dq_f, dk_f, dv_f = _fox_bwd(q2, kf, vf, do_fox)
    dz_fox, dz_ff, dg_q, dg_k, db_f = _fox_post_bwd(dq_f, dk_f, dv_f, z_a, z_ff, b_pad, g_fox_q, g_fox_k)
    grad_x, dg_mix = _in_bwd(dz_ret, dz_gt, dz_fox, dz_a, dz_ff, w_a, w_ff, xs, g_mix, dx2)

    g_in = _pack_g_in(_grad_plain(h, dz_ret, "gw_in_ret", F32), _grad_plain(h, dz_gt, "gw_in_gt", F32),
                      _grad_plain(h, dz_fox, "gw_in_fox", F32, tn=768), _grad_plain(h, dz_a, "gw_in_a", F32),
                      _grad_plain(h, dz_ff, "gw_in_ff", F32))
    late_part = [g_in, _grad_colstack(u_r, d_yr, "gw_ret_o", 256), _grad_colstack(o_cat, d_yf, "gw_fox_o", 256),
                 _grad_plain(mrg, dx2, "gw_out", BF).reshape(N_CHIP, 256, D_MODEL)]
    small_g = _pack_small(dict(g_mix=dg_mix, g_ffn=dg_ffn, g_ret_norm=dg_ret, g_fox_q=dg_q, g_fox_k=dg_k, b_forget=db_f))

    recv = _scatter_partials(late_part, small_g)
    recv_ffn = _push_wait(f_send, f_recv, f_src, f_land, recv[0], "scatter_ffn_wait", True)
    sums = [_sum_stack(r, "sum_" + n) for r, n in zip(list(recv[:4]) + recv_ffn, big_names)]
    sib = _sibling_exchange(sums)
    big_out = {n: _adamw(big_w[n], big_m[n], big_v[n], sums[i], sib[i], "adamw_" + n) for i, n in enumerate(big_names)}
    sg, sd, sm, sv = _adamw_small(_pack_small(small_w), _pack_small(small_m), _pack_small(small_v), recv[-1])
    small_out = [_unpack_small(t) for t in (sg, sd, sm, sv)]

    order = ("g_mix", "w_in", "b_forget", "g_ret_norm", "w_ret_o", "g_fox_q", "g_fox_k", "w_fox_o", "w_out", "g_ffn",
             "w_gate", "w_up", "w_down")
    outs = [loss, grad_x[None]]
    for idx in range(4):
        for n in order:
            outs.append(big_out[n][idx][None] if n in big_out else small_out[idx][n])
    return tuple(outs)
```

```python
import functools
import math

import numpy as np
import jax
import jax.numpy as jnp
from jax import lax
from jax.experimental import pallas as pl
from jax.experimental.pallas import tpu as pltpu

F32 = jnp.float32
BF = jnp.bfloat16
MESH = pl.DeviceIdType.MESH

D_MODEL = 1024
D_FF = 2816
N_CHIP = 4
FF_SH = D_FF // N_CHIP
IN_COLS = 5128
IN_SH = IN_COLS // N_CHIP
RET_H, RET_DV = 4, 128
FOX_H, FOX_D = 8, 64
CHUNK = 128
EPS = 1e-6
NEG = -1e30
LANE = 128
C_RET, C_GT, C_FOX, C_A, C_END = 0, 1024, 1536, 3072, 5120
L_CQ, L_CK, L_LSE, L_MAX = 64, 67, 70, 73

ADAM_LR, ADAM_B1, ADAM_B2, ADAM_EPS, ADAM_WD, ADAM_STEP = 0.001, 0.9, 0.999, 1e-08, 0.01, 10
VMEM_BIG = 56 * 1024 * 1024


def _nn(a, b):
    return lax.dot_general(a, b, (((1,), (0,)), ((), ())), preferred_element_type=F32)


def _nt(a, b):
    return lax.dot_general(a, b, (((1,), (1,)), ((), ())), preferred_element_type=F32)


def _tn(a, b):
    return lax.dot_general(a, b, (((0,), (0,)), ((), ())), preferred_element_type=F32)


def _split3(x):
    hi = x.astype(BF)
    r = x - hi.astype(F32)
    mid = r.astype(BF)
    lo = (r - mid.astype(F32)).astype(BF)
    return hi, mid, lo


def _sigmoid(x):
    return 1.0 / (1.0 + jnp.exp(-x))


def _swap32(x):
    lane = lax.broadcasted_iota(jnp.int32, x.shape, 1)
    return jnp.where(lane < 32, pltpu.roll(x, 96, 1), pltpu.roll(x, 32, 1))


def _params(sem, vmem=None):
    return pltpu.CompilerParams(dimension_semantics=sem, vmem_limit_bytes=vmem)


def _row_tile(rows, cap, mult):
    return max(d for d in range(mult, cap + 1, mult) if rows % d == 0)


def _assemble_w_in(stack, tr=256):
    def body(s_ref, a_ref, f_ref):
        full = jnp.concatenate([s_ref[k].astype(F32) for k in range(N_CHIP)], axis=-1)
        a_ref[...] = jnp.concatenate([full[:, :3072], full[:, 3080:IN_COLS]], axis=-1).astype(BF)
        f_ref[...] = jnp.concatenate([full[:, 3072:3080], jnp.zeros((tr, LANE - FOX_H), F32)], axis=-1).astype(BF)

    return pl.pallas_call(
        body, name="assemble_w_in", grid=(D_MODEL // tr,),
        in_specs=[pl.BlockSpec((N_CHIP, tr, IN_SH), lambda i: (0, i, 0))],
        out_specs=[pl.BlockSpec((tr, C_END), lambda i: (i, 0)), pl.BlockSpec((tr, LANE), lambda i: (i, 0))],
        out_shape=[jax.ShapeDtypeStruct((D_MODEL, C_END), BF), jax.ShapeDtypeStruct((D_MODEL, LANE), BF)],
        compiler_params=_params(("parallel",), VMEM_BIG),
    )(stack)


def _pack_g_in(g_ret, g_gt, g_fox, g_a, g_ff, tr=256):
    def body(r_ref, t_ref, x_ref, a_ref, f_ref, o_ref):
        full = jnp.concatenate([r_ref[...], t_ref[...], x_ref[...], f_ref[...][:, :FOX_H], a_ref[...]], axis=-1)
        for k in range(N_CHIP):
            o_ref[k] = full[:, k * IN_SH:(k + 1) * IN_SH].astype(BF)

    def spec(w):
        return pl.BlockSpec((tr, w), lambda i: (i, 0))

    return pl.pallas_call(
        body, name="pack_g_in", grid=(D_MODEL // tr,),
        in_specs=[spec(1024), spec(512), spec(1536), spec(2048), spec(LANE)],
        out_specs=pl.BlockSpec((N_CHIP, tr, IN_SH), lambda i: (0, i, 0)),
        out_shape=jax.ShapeDtypeStruct((N_CHIP, D_MODEL, IN_SH), BF),
        compiler_params=_params(("parallel",), VMEM_BIG),
    )(g_ret, g_gt, g_fox, g_a, g_ff)


def _rms_cast(x, g, tm=512):
    T = x.shape[0]

    def body(x_ref, g_ref, o_ref):
        xv = x_ref[...]
        r = lax.rsqrt(jnp.mean(xv * xv, axis=-1, keepdims=True) + EPS)
        o_ref[...] = (xv * r * g_ref[...]).astype(BF)

    return pl.pallas_call(
        body, name="rms_cast", grid=(T // tm,),
        in_specs=[pl.BlockSpec((tm, D_MODEL), lambda i: (i, 0)), pl.BlockSpec((1, D_MODEL), lambda i: (0, 0))],
        out_specs=pl.BlockSpec((tm, D_MODEL), lambda i: (i, 0)),
        out_shape=jax.ShapeDtypeStruct((T, D_MODEL), BF),
        compiler_params=_params(("parallel",)),
    )(x, g)


def _mm_nn(a, b, name, tm=512, tn=1024):
    M, K = a.shape
    N = b.shape[1]
    tn = min(tn, N)

    def body(a_ref, b_ref, o_ref):
        o_ref[...] = _nn(a_ref[...], b_ref[...])

    return pl.pallas_call(
        body, name=name, grid=(N // tn, M // tm),
        in_specs=[pl.BlockSpec((tm, K), lambda j, i: (i, 0)), pl.BlockSpec((K, tn), lambda j, i: (0, j))],
        out_specs=pl.BlockSpec((tm, tn), lambda j, i: (i, j)),
        out_shape=jax.ShapeDtypeStruct((M, N), F32),
        compiler_params=_params(("parallel", "parallel")),
    )(a, b)


def _mm_tn(a, b, name, grid, a_spec, b_spec, o_spec, out_shape, acc_shape):
    nk = grid[-1]

    def body(a_ref, b_ref, o_ref, acc):
        k = pl.program_id(len(grid) - 1)

        @pl.when(k == 0)
        def _():
            acc[...] = jnp.zeros(acc.shape, F32)

        acc[...] += _tn(a_ref[...].astype(BF), b_ref[...].astype(BF))

        @pl.when(k == nk - 1)
        def _():
            o_ref[...] = acc[...].astype(o_ref.dtype)

    return pl.pallas_call(
        body, name=name, grid=grid, in_specs=[a_spec, b_spec], out_specs=o_spec, out_shape=out_shape,
        scratch_shapes=[pltpu.VMEM(acc_shape, F32)],
        compiler_params=_params(("parallel",) * (len(grid) - 1) + ("arbitrary",), VMEM_BIG),
    )(a, b)


def _grad_plain(a, b, name, out_dtype, tk=512, tn=1024):
    T, M = a.shape
    N = b.shape[1]
    tn = min(tn, N)
    return _mm_tn(a, b, name, (N // tn, T // tk),
                  pl.BlockSpec((tk, M), lambda j, k: (k, 0)), pl.BlockSpec((tk, tn), lambda j, k: (k, j)),
                  pl.BlockSpec((M, tn), lambda j, k: (0, j)), jax.ShapeDtypeStruct((M, N), out_dtype), (M, tn))


def _grad_colstack(a, b, name, wcol, tk=512):
    T, M = a.shape
    S = b.shape[1] // wcol
    return _mm_tn(a, b, name, (S, T // tk),
                  pl.BlockSpec((tk, M), lambda s, k: (k, 0)), pl.BlockSpec((tk, wcol), lambda s, k: (k, s)),
                  pl.BlockSpec((None, M, wcol), lambda s, k: (s, 0, 0)),
                  jax.ShapeDtypeStruct((S, M, wcol), BF), (M, wcol))


def _grad_bstack(a, b, name, tk=512):
    T, M = a.shape
    S, _, n = b.shape
    return _mm_tn(a, b, name, (S, T // tk),
                  pl.BlockSpec((tk, M), lambda s, k: (k, 0)), pl.BlockSpec((None, tk, n), lambda s, k: (s, k, 0)),
                  pl.BlockSpec((None, M, n), lambda s, k: (s, 0, 0)),
                  jax.ShapeDtypeStruct((S, M, n), BF), (M, n))


def _grad_astack(a, b, name, tk=512):
    S, T, m = a.shape
    N = b.shape[1]
    return _mm_tn(a, b, name, (S, T // tk),
                  pl.BlockSpec((None, tk, m), lambda s, k: (s, k, 0)), pl.BlockSpec((tk, N), lambda s, k: (k, 0)),
                  pl.BlockSpec((None, m, N), lambda s, k: (s, 0, 0)),
                  jax.ShapeDtypeStruct((S, m, N), BF), (m, N))


def _rope_tables(T):
    half = 32
    pos = jnp.arange(T, dtype=F32)
    inv_freq = 1.0 / (10000.0 ** (jnp.arange(half, dtype=F32) / half))
    ang = pos[:, None] * inv_freq[None, :]
    cos, sin = jnp.cos(ang), jnp.sin(ang)
    z = jnp.zeros((T, 64), F32)
    return jnp.concatenate([cos, cos, z], axis=-1), jnp.concatenate([-sin, sin, z], axis=-1)


def _ret_consts():
    h = np.arange(RET_H, dtype=np.float32)
    log_g = np.log1p(-(np.float32(2.0) ** (-5.0 - h))).astype(np.float32)
    idx = np.arange(CHUNK, dtype=np.float32)
    diff = idx[:, None] - idx[None, :]
    decay = np.where(diff[None] >= 0, np.exp(np.maximum(diff, 0.0)[None] * log_g[:, None, None]), 0.0)
    zeta = np.exp((CHUNK - 1.0 - idx)[None, :] * log_g[:, None])
    xi = np.exp((idx + 1.0)[None, :] * log_g[:, None])
    gc = np.exp(CHUNK * log_g)
    bc = lambda v: np.broadcast_to(v[:, :, None], (RET_H, CHUNK, LANE)).astype(np.float32)
    gcb = np.broadcast_to(gc[:, None, None], (RET_H, CHUNK, LANE)).astype(np.float32)
    return (jnp.asarray(decay.astype(np.float32)), jnp.asarray(bc(zeta)), jnp.asarray(bc(xi)), jnp.asarray(gcb))


def _mix_prep(z_a, z_ff, cos_t, sin_t, b_f, g_q, g_k, tm=256):
    T = z_a.shape[0]

    def body(zqk_ref, zf_ref, zff_ref, cos_ref, sin_ref, b_ref, gq_ref, gk_ref,
             qr_ref, kr_ref, qf_ref, kf_ref, vf_ref, carry):
        i = pl.program_id(0)

        @pl.when(i == 0)
        def _():
            carry[...] = jnp.zeros(carry.shape, F32)

        lane = lax.broadcasted_iota(jnp.int32, (tm, LANE), 1)
        zpad = jnp.zeros((tm, 64), F32)
        cosv, sinv = cos_ref[...], sin_ref[...]
        zqk = zqk_ref[...]
        for h in range(RET_H):
            for src, dst, scale in ((0, qr_ref, 1.0), (256, kr_ref, 0.125)):
                xh = jnp.concatenate([zqk[:, src + 64 * h: src + 64 * h + 64], zpad], axis=-1)
                rot = xh * cosv + _swap32(xh) * sinv
                dst[h] = (rot * scale).astype(BF)

        lf_in = zff_ref[...] + b_ref[...]
        logf = jnp.minimum(lf_in, 0.0) - jnp.log(1.0 + jnp.exp(-jnp.abs(lf_in)))
        row = lax.broadcasted_iota(jnp.int32, (tm, tm), 0)
        col = lax.broadcasted_iota(jnp.int32, (tm, tm), 1)
        tri = (row >= col).astype(BF)
        hi, mid, lo = _split3(logf)
        cs = _nn(tri, hi) + _nn(tri, mid) + _nn(tri, lo) + carry[...]
        carry[...] = cs[tm - 1:tm, :]

        zf = zf_ref[...]
        one = jnp.ones((tm, LANE), F32)
        for h in range(FOX_H):
            c = cs[:, h:h + 1]
            chi, cmid, clo = [t.astype(F32) for t in _split3(c)]
            qh = zf[:, 64 * h:64 * h + 64]
            kh = zf[:, 512 + 64 * h:512 + 64 * h + 64]
            vh = zf[:, 1024 + 64 * h:1024 + 64 * h + 64]
            qn = qh * lax.rsqrt(jnp.mean(qh * qh, axis=-1, keepdims=True) + EPS) * gq_ref[...] * 0.125
            kn = kh * lax.rsqrt(jnp.mean(kh * kh, axis=-1, keepdims=True) + EPS) * gk_ref[...]
            qa = jnp.concatenate([qn, zpad], axis=-1)
            qa = jnp.where(lane == L_CQ, chi, jnp.where(lane == L_CQ + 1, cmid, jnp.where(lane == L_CQ + 2, clo, qa)))
            qa = jnp.where((lane >= L_CK) & (lane < L_CK + 3), one, qa)
            ka = jnp.concatenate([kn, zpad], axis=-1)
            ka = jnp.where(lane == L_CK, -chi, jnp.where(lane == L_CK + 1, -cmid, jnp.where(lane == L_CK + 2, -clo, ka)))
            ka = jnp.where(((lane >= L_CQ) & (lane < L_CQ + 3)) | ((lane >= L_LSE) & (lane < L_MAX + 3)), one, ka)
            va = jnp.concatenate([vh, zpad], axis=-1)
            va = jnp.where((lane >= 64) & (lane < 67), one, va)
            qf_ref[h] = qa.astype(BF)
            kf_ref[h] = ka.astype(BF)
            vf_ref[h] = va.astype(BF)

    hspec4 = pl.BlockSpec((RET_H, tm, LANE), lambda i: (0, i, 0))
    hspec8 = pl.BlockSpec((FOX_H, tm, LANE), lambda i: (0, i, 0))
    small = lambda w: pl.BlockSpec((1, w), lambda i: (0, 0))
    return pl.pallas_call(
        body, name="mix_prep", grid=(T // tm,),
        in_specs=[pl.BlockSpec((tm, 512), lambda i: (i, 0)), pl.BlockSpec((tm, 1536), lambda i: (i, 1)),
                  pl.BlockSpec((tm, LANE), lambda i: (i, 0)), pl.BlockSpec((tm, LANE), lambda i: (i, 0)),
                  pl.BlockSpec((tm, LANE), lambda i: (i, 0)), small(LANE), small(64), small(64)],
        out_specs=[hspec4, hspec4, hspec8, hspec8, hspec8],
        out_shape=[jax.ShapeDtypeStruct((RET_H, T, LANE), BF)] * 2 + [jax.ShapeDtypeStruct((FOX_H, T, LANE), BF)] * 3,
        scratch_shapes=[pltpu.VMEM((1, LANE), F32)],
        compiler_params=_params(("arbitrary",), VMEM_BIG),
    )(z_a, z_a, z_ff, cos_t, sin_t, b_f, g_q, g_k)


def _ret_fwd(qr, kr, z_a, g_ret, consts, tt=512):
    T = z_a.shape[0]
    nch = tt // CHUNK
    decay, zeta, xi, gcb = consts

    def body(q_ref, k_ref, v_ref, gt_ref, g_ref, d_ref, ze_ref, xi_ref, gc_ref, o_ref, u_ref, st_ref, r_sc):
        i = pl.program_id(0)

        @pl.when(i == 0)
        def _():
            r_sc[...] = jnp.zeros(r_sc.shape, F32)

        for c in range(nch):
            rows = slice(c * CHUNK, (c + 1) * CHUNK)
            for h in range(RET_H):
                cols = slice(h * RET_DV, (h + 1) * RET_DV)
                q, k = q_ref[h, rows, :], k_ref[h, rows, :]
                v32 = v_ref[rows, cols]
                r = r_sc[h]
                st_ref[h, rows, :] = r
                s = _nt(q, k) * d_ref[h]
                o = _nn(s.astype(BF), v32.astype(BF)) + _nn(q, r.astype(BF)) * xi_ref[h]
                r_sc[h] = gc_ref[h] * r + _tn(k, (v32 * ze_ref[h]).astype(BF))
                o_ref[rows, cols] = o
                mu = jnp.mean(o, axis=-1, keepdims=True)
                xc = o - mu
                on = xc * lax.rsqrt(jnp.mean(xc * xc, axis=-1, keepdims=True) + EPS)
                gt = gt_ref[rows, cols]
                u_ref[rows, cols] = (gt * _sigmoid(gt) * (on * g_ref[:, cols])).astype(BF)

    hspec = pl.BlockSpec((RET_H, tt, LANE), lambda i: (0, i, 0))
    cspec = pl.BlockSpec((RET_H, CHUNK, LANE), lambda i: (0, 0, 0))
    return pl.pallas_call(
        body, name="ret_fwd", grid=(T // tt,),
        in_specs=[hspec, hspec, pl.BlockSpec((tt, 512), lambda i: (i, 1)), pl.BlockSpec((tt, 512), lambda i: (i, 2)),
                  pl.BlockSpec((1, 512), lambda i: (0, 0)), cspec, cspec, cspec, cspec],
        out_specs=[pl.BlockSpec((tt, 512), lambda i: (i, 0)), pl.BlockSpec((tt, 512), lambda i: (i, 0)), hspec],
        out_shape=[jax.ShapeDtypeStruct((T, 512), F32), jax.ShapeDtypeStruct((T, 512), BF),
                   jax.ShapeDtypeStruct((RET_H, T, LANE), F32)],
        scratch_shapes=[pltpu.VMEM((RET_H, CHUNK, LANE), F32)],
        compiler_params=_params(("arbitrary",), VMEM_BIG),
    )(qr, kr, z_a, z_a, g_ret, decay, zeta, xi, gcb)


def _fox_fwd(q, k, v, sub=512):
    H, T, _ = q.shape
    tb = 2 * sub

    def body(q_ref, k_ref, v_ref, o_ref, q2_ref, mx_sc, acc_sc):
        i = pl.program_id(1)
        lane = lax.broadcasted_iota(jnp.int32, (sub, LANE), 1)
        row = lax.broadcasted_iota(jnp.int32, (sub, sub), 0)
        col = lax.broadcasted_iota(jnp.int32, (sub, sub), 1)
        causal = row >= col
        qs = [q_ref[0:sub, :], q_ref[sub:tb, :]]
        d0 = pl.multiple_of(i * tb, tb)
        d1 = pl.multiple_of(i * tb + sub, sub)

        def lane_max(s):
            m = s[:, 0:LANE]
            for c in range(1, s.shape[1] // LANE):
                m = jnp.maximum(m, s[:, c * LANE:(c + 1) * LANE])
            return m

        mx_sc[...] = jnp.full(mx_sc.shape, NEG, F32)

        def max_body(j, carry):
            kb = k_ref[pl.ds(pl.multiple_of(j * tb, tb), tb), :]
            for a in range(2):
                mx_sc[a] = jnp.maximum(mx_sc[a], lane_max(_nt(qs[a], kb)))
            return carry

        lax.fori_loop(0, i, max_body, 0)
        k0, k1 = k_ref[pl.ds(d0, sub), :], k_ref[pl.ds(d1, sub), :]
        v0, v1 = v_ref[pl.ds(d0, sub), :], v_ref[pl.ds(d1, sub), :]
        mx = [jnp.maximum(mx_sc[0], lane_max(jnp.where(causal, _nt(qs[0], k0), NEG))),
              jnp.maximum(jnp.maximum(mx_sc[1], lane_max(_nt(qs[1], k0))),
                          lane_max(jnp.where(causal, _nt(qs[1], k1), NEG)))]
        ms = [jnp.max(t, axis=1, keepdims=True) for t in mx]

        def put3(base, first, val):
            hi, mid, lo = _split3(val)
            return jnp.where(lane == first, hi, jnp.where(lane == first + 1, mid, jnp.where(lane == first + 2, lo, base)))

        qm = [put3(qs[a], L_MAX, -ms[a]) for a in range(2)]

        acc_sc[...] = jnp.zeros(acc_sc.shape, F32)

        def acc_body(j, carry):
            off = pl.multiple_of(j * tb, tb)
            kb, vb = k_ref[pl.ds(off, tb), :], v_ref[pl.ds(off, tb), :]
            for a in range(2):
                acc_sc[a] += _nn(jnp.exp(_nt(qm[a], kb)).astype(BF), vb)
            return carry

        lax.fori_loop(0, i, acc_body, 0)

        def pv(qa, kk, vv, masked):
            p = jnp.exp(_nt(qa, kk))
            if masked:
                p = jnp.where(causal, p, 0.0)
            return _nn(p.astype(BF), vv)

        accs = [acc_sc[0] + pv(qm[0], k0, v0, True),
                acc_sc[1] + pv(qm[1], k0, v0, False) + pv(qm[1], k1, v1, True)]
        for a in range(2):
            rows = slice(a * sub, (a + 1) * sub)
            l = accs[a][:, 64:65]
            o_ref[rows, :] = jnp.where(lane < 64, accs[a] / l, 0.0)
            q2_ref[rows, :] = put3(qs[a], L_LSE, -(ms[a] + jnp.log(l)))

    blk = pl.BlockSpec((None, tb, LANE), lambda h, i: (h, i, 0))
    full = pl.BlockSpec((None, T, LANE), lambda h, i: (h, 0, 0))
    return pl.pallas_call(
        body, name="fox_fwd", grid=(H, T // tb),
        in_specs=[blk, full, full], out_specs=[blk, blk],
        out_shape=[jax.ShapeDtypeStruct((H, T, LANE), F32), jax.ShapeDtypeStruct((H, T, LANE), BF)],
        scratch_shapes=[pltpu.VMEM((2, sub, LANE), F32), pltpu.VMEM((2, sub, LANE), F32)],
        compiler_params=_params(("parallel", "arbitrary"), VMEM_BIG),
    )(q, k, v)


def _merge_out(u_r, o_fox, z_a, x, g_ffn, w_ro, w_fo, w_out, tm=256):
    T = x.shape[0]

    def body(u_ref, of_ref, ar_ref, af_ref, x_ref, g_ref, wro_ref, wfo_ref, wout_ref,
             yr_ref, yf_ref, m_ref, x2_ref, h2_ref, oc_ref):
        u = u_ref[...]
        oc = jnp.concatenate([of_ref[h][:, :FOX_D] for h in range(FOX_H)], axis=-1).astype(BF)
        oc_ref[...] = oc
        yr = jnp.concatenate([_nn(u, wro_ref[k]) for k in range(N_CHIP)], axis=-1)
        yf = jnp.concatenate([_nn(oc, wfo_ref[k]) for k in range(N_CHIP)], axis=-1)
        yr_ref[...] = yr
        yf_ref[...] = yf
        m = (_sigmoid(ar_ref[...]) * yr + _sigmoid(af_ref[...]) * yf).astype(BF)
        m_ref[...] = m
        x2 = x_ref[...]
        for k in range(N_CHIP):
            x2 = x2 + _nn(m[:, 256 * k:256 * k + 256], wout_ref[k])
        x2_ref[...] = x2
        r = lax.rsqrt(jnp.mean(x2 * x2, axis=-1, keepdims=True) + EPS)
        h2_ref[...] = (x2 * r * g_ref[...]).astype(BF)

    row = lambda w: pl.BlockSpec((tm, w), lambda i: (i, 0))
    const = lambda shp: pl.BlockSpec(shp, lambda i: (0,) * len(shp))
    return pl.pallas_call(
        body, name="merge_out", grid=(T // tm,),
        in_specs=[row(512), pl.BlockSpec((FOX_H, tm, LANE), lambda i: (0, i, 0)),
                  pl.BlockSpec((tm, 1024), lambda i: (i, 3)), pl.BlockSpec((tm, 1024), lambda i: (i, 4)),
                  row(1024), const((1, 1024)), const((N_CHIP, 512, 256)), const((N_CHIP, 512, 256)),
                  const((N_CHIP, 256, 1024))],
        out_specs=[row(1024), row(1024), row(1024), row(1024), row(1024), row(512)],
        out_shape=[jax.ShapeDtypeStruct((T, 1024), F32), jax.ShapeDtypeStruct((T, 1024), F32),
                   jax.ShapeDtypeStruct((T, 1024), BF), jax.ShapeDtypeStruct((T, 1024), F32),
                   jax.ShapeDtypeStruct((T, 1024), BF), jax.ShapeDtypeStruct((T, 512), BF)],
        compiler_params=_params(("parallel",), VMEM_BIG),
    )(u_r, o_fox, z_a, z_a, x, g_ffn, w_ro, w_fo, w_out)


def _ffn_fwd(h2, x2, tgt, w_gate, w_up, w_down, tm=512):
    T = h2.shape[0]

    def body(h_ref, x2_ref, t_ref, wg_ref, wu_ref, wd_ref, gp_ref, up_ref, act_ref, dy_ref, ls_ref, acc):
        i, k = pl.program_id(0), pl.program_id(1)

        @pl.when(k == 0)
        def _():
            acc[...] = jnp.zeros(acc.shape, F32)

        @pl.when((i == 0) & (k == 0))
        def _():
            ls_ref[...] = jnp.zeros(ls_ref.shape, F32)

        h = h_ref[...]
        gp = _nn(h, wg_ref[...])
        up = _nn(h, wu_ref[...])
        gp_ref[...] = gp
        up_ref[...] = up
        act = (gp * _sigmoid(gp) * up).astype(BF)
        act_ref[...] = act
        acc[...] += _nn(act, wd_ref[...])

        @pl.when(k == N_CHIP - 1)
        def _():
            err = x2_ref[...] + acc[...] - t_ref[...]
            dy_ref[...] = err * (1.0 / D_MODEL)
            ls_ref[...] += jnp.sum(err * err, axis=0, keepdims=True)

    row = pl.BlockSpec((tm, D_MODEL), lambda i, k: (i, 0))
    hid = pl.BlockSpec((None, tm, FF_SH), lambda i, k: (k, i, 0))
    return pl.pallas_call(
        body, name="ffn_fwd", grid=(T // tm, N_CHIP),
        in_specs=[row, row, row, pl.BlockSpec((None, D_MODEL, FF_SH), lambda i, k: (k, 0, 0)),
                  pl.BlockSpec((None, D_MODEL, FF_SH), lambda i, k: (k, 0, 0)),
                  pl.BlockSpec((None, FF_SH, D_MODEL), lambda i, k: (k, 0, 0))],
        out_specs=[hid, hid, hid, row, pl.BlockSpec((1, D_MODEL), lambda i, k: (0, 0))],
        out_shape=[jax.ShapeDtypeStruct((N_CHIP, T, FF_SH), F32), jax.ShapeDtypeStruct((N_CHIP, T, FF_SH), F32),
                   jax.ShapeDtypeStruct((N_CHIP, T, FF_SH), BF), jax.ShapeDtypeStruct((T, D_MODEL), F32),
                   jax.ShapeDtypeStruct((1, D_MODEL), F32)],
        scratch_shapes=[pltpu.VMEM((tm, D_MODEL), F32)],
        compiler_params=_params(("arbitrary", "arbitrary"), VMEM_BIG),
    )(h2, x2, tgt, w_gate, w_up, w_down)


def _ffn_bwd(dy, gp, up, x2, g_ffn, w_gate, w_up, w_down, tm=512):
    T = dy.shape[0]

    def body(dy_ref, gp_ref, up_ref, x2_ref, g_ref, wg_ref, wu_ref, wd_ref, dgp_ref, dup_ref, dx_ref, dg_ref, acc):
        i, k = pl.program_id(0), pl.program_id(1)

        @pl.when(k == 0)
        def _():
            acc[...] = jnp.zeros(acc.shape, F32)

        @pl.when((i == 0) & (k == 0))
        def _():
            dg_ref[...] = jnp.zeros(dg_ref.shape, F32)

        dy = dy_ref[...]
        dact = _nt(dy.astype(BF), wd_ref[...])
        gp, up = gp_ref[...], up_ref[...]
        sg = _sigmoid(gp)
        dup = (dact * gp * sg).astype(BF)
        dgp = (dact * up * sg * (1.0 + gp * (1.0 - sg))).astype(BF)
        dgp_ref[...] = dgp
        dup_ref[...] = dup
        acc[...] += _nt(dgp, wg_ref[...]) + _nt(dup, wu_ref[...])

        @pl.when(k == N_CHIP - 1)
        def _():
            x2 = x2_ref[...]
            r = lax.rsqrt(jnp.mean(x2 * x2, axis=-1, keepdims=True) + EPS)
            xn = x2 * r
            dh = acc[...]
            dg_ref[...] += jnp.sum(dh * xn, axis=0, keepdims=True)
            dxn = dh * g_ref[...]
            dx_ref[...] = dy + r * (dxn - xn * jnp.mean(dxn * xn, axis=-1, keepdims=True))

    row = pl.BlockSpec((tm, D_MODEL), lambda i, k: (i, 0))
    hid = pl.BlockSpec((None, tm, FF_SH), lambda i, k: (k, i, 0))
    vec = pl.BlockSpec((1, D_MODEL), lambda i, k: (0, 0))
    return pl.pallas_call(
        body, name="ffn_bwd", grid=(T // tm, N_CHIP),
        in_specs=[row, hid, hid, row, vec, pl.BlockSpec((None, D_MODEL, FF_SH), lambda i, k: (k, 0, 0)),
                  pl.BlockSpec((None, D_MODEL, FF_SH), lambda i, k: (k, 0, 0)),
                  pl.BlockSpec((None, FF_SH, D_MODEL), lambda i, k: (k, 0, 0))],
        out_specs=[hid, hid, row, vec],
        out_shape=[jax.ShapeDtypeStruct((N_CHIP, T, FF_SH), BF), jax.ShapeDtypeStruct((N_CHIP, T, FF_SH), BF),
                   jax.ShapeDtypeStruct((T, D_MODEL), F32), jax.ShapeDtypeStruct((1, D_MODEL), F32)],
        scratch_shapes=[pltpu.VMEM((tm, D_MODEL), F32)],
        compiler_params=_params(("arbitrary", "arbitrary"), VMEM_BIG),
    )(dy, gp, up, x2, g_ffn, w_gate, w_up, w_down)


def _out_bwd(dx2, z_a, y_r, y_f, o_raw, o_fox, g_ret, w_ro, w_fo, w_out, tm=256):
    T = dx2.shape[0]

    def body(dx_ref, gt_ref, ar_ref, af_ref, yr_ref, yf_ref, o_ref, of_ref, g_ref, wro_ref, wfo_ref, wout_ref,
             dyr_ref, dyf_ref, dgt_ref, da_ref, do_ref, dof_ref, dg_ref):
        i = pl.program_id(0)

        @pl.when(i == 0)
        def _():
            dg_ref[...] = jnp.zeros(dg_ref.shape, F32)

        dxb = dx_ref[...].astype(BF)
        dm = jnp.concatenate([_nt(dxb, wout_ref[k]) for k in range(N_CHIP)], axis=-1)
        sr, sf = _sigmoid(ar_ref[...]), _sigmoid(af_ref[...])
        dyr = dm * sr
        dyf = dm * sf
        da_ref[:, :1024] = (dyr * yr_ref[...] * (1.0 - sr)).astype(BF)
        da_ref[:, 1024:] = (dyf * yf_ref[...] * (1.0 - sf)).astype(BF)
        dyr = dyr.astype(BF)
        dyf = dyf.astype(BF)
        dyr_ref[...] = dyr
        dyf_ref[...] = dyf
        du = jnp.zeros((tm, 512), F32)
        doc = jnp.zeros((tm, 512), F32)
        for k in range(N_CHIP):
            du = du + _nt(dyr[:, 256 * k:256 * k + 256], wro_ref[k])
            doc = doc + _nt(dyf[:, 256 * k:256 * k + 256], wfo_ref[k])

        for h in range(RET_H):
            cols = slice(h * RET_DV, (h + 1) * RET_DV)
            o = o_ref[:, cols]
            mu = jnp.mean(o, axis=-1, keepdims=True)
            xc = o - mu
            rstd = lax.rsqrt(jnp.mean(xc * xc, axis=-1, keepdims=True) + EPS)
            on = xc * rstd
            g = g_ref[:, cols]
            gt = gt_ref[:, cols]
            sg = _sigmoid(gt)
            duh = du[:, cols]
            dgt_ref[:, cols] = (duh * (on * g) * sg * (1.0 + gt * (1.0 - sg))).astype(BF)
            dog = duh * gt * sg
            dg_ref[:, cols] += jnp.sum(dog * on, axis=0, keepdims=True)
            don = dog * g
            do_ref[:, cols] = rstd * (don - jnp.mean(don, axis=-1, keepdims=True)
                                      - on * jnp.mean(don * on, axis=-1, keepdims=True))

        lane = lax.broadcasted_iota(jnp.int32, (tm, LANE), 1)
        zpad = jnp.zeros((tm, 64), F32)
        for h in range(FOX_H):
            doh = doc[:, 64 * h:64 * h + 64]
            delta = jnp.sum(doh * of_ref[h][:, :FOX_D], axis=-1, keepdims=True)
            hi, mid, lo = [t.astype(F32) for t in _split3(-delta)]
            da = jnp.concatenate([doh, zpad], axis=-1)
            da = jnp.where(lane == 64, hi, jnp.where(lane == 65, mid, jnp.where(lane == 66, lo, da)))
            dof_ref[h] = da.astype(BF)

    row = lambda w: pl.BlockSpec((tm, w), lambda i: (i, 0))
    const = lambda shp: pl.BlockSpec(shp, lambda i: (0,) * len(shp))
    hsp = pl.BlockSpec((FOX_H, tm, LANE), lambda i: (0, i, 0))
    return pl.pallas_call(
        body, name="out_bwd", grid=(T // tm,),
        in_specs=[row(1024), pl.BlockSpec((tm, 512), lambda i: (i, 2)), pl.BlockSpec((tm, 1024), lambda i: (i, 3)),
                  pl.BlockSpec((tm, 1024), lambda i: (i, 4)), row(1024), row(1024), row(512), hsp,
                  const((1, 512)), const((N_CHIP, 512, 256)), const((N_CHIP, 512, 256)), const((N_CHIP, 256, 1024))],
        out_specs=[row(1024), row(1024), row(512), row(2048), row(512), hsp, const((1, 512))],
        out_shape=[jax.ShapeDtypeStruct((T, 1024), BF), jax.ShapeDtypeStruct((T, 1024), BF),
                   jax.ShapeDtypeStruct((T, 512), BF), jax.ShapeDtypeStruct((T, 2048), BF),
                   jax.ShapeDtypeStruct((T, 512), F32), jax.ShapeDtypeStruct((FOX_H, T, LANE), BF),
                   jax.ShapeDtypeStruct((1, 512), F32)],
        compiler_params=_params(("arbitrary",), VMEM_BIG),
    )(dx2, z_a, z_a, z_a, y_r, y_f, o_raw, o_fox, g_ret, w_ro, w_fo, w_out)


def _ret_bwd(d_o, qr, kr, z_a, states, cos_t, sin_t, consts, tt=512):
    T = z_a.shape[0]
    nt = T // tt
    nch = tt // CHUNK
    decay, zeta, xi, gcb = consts

    def body(do_ref, q_ref, k_ref, v_ref, st_ref, cos_ref, sin_ref, d_ref, ze_ref, xi_ref, gc_ref, dz_ref, g_sc):
        i = pl.program_id(0)

        @pl.when(i == 0)
        def _():
            g_sc[...] = jnp.zeros(g_sc.shape, F32)

        for c in reversed(range(nch)):
            rows = slice(c * CHUNK, (c + 1) * CHUNK)
            cosv, sinv = cos_ref[rows, :], sin_ref[rows, :]
            dq_parts, dk_parts = [], []
            for h in range(RET_H):
                cols = slice(h * RET_DV, (h + 1) * RET_DV)
                q, k = q_ref[h, rows, :], k_ref[h, rows, :]
                v32 = v_ref[rows, cols]
                vb = v32.astype(BF)
                r = st_ref[h, rows, :]
                g = g_sc[h]
                gb = g.astype(BF)
                d_o = do_ref[rows, cols]
                dob = d_o.astype(BF)
                dox = (d_o * xi_ref[h]).astype(BF)
                dec = d_ref[h]
                s = (_nt(q, k) * dec).astype(BF)
                ds = (_nt(dob, vb) * dec).astype(BF)
                dv = _tn(s, dob) + ze_ref[h] * _nn(k, gb)
                dq = _nn(ds, k) + _nt(dox, r.astype(BF))
                dk = _tn(ds, q) + _nt((v32 * ze_ref[h]).astype(BF), gb)
                g_sc[h] = gc_ref[h] * g + _tn(q, dox)
                dq_parts.append((dq * cosv - _swap32(dq) * sinv)[:, :64])
                dk_parts.append(((dk * cosv - _swap32(dk) * sinv) * 0.125)[:, :64])
                dz_ref[rows, 512 + h * RET_DV:512 + (h + 1) * RET_DV] = dv.astype(BF)
            dz_ref[rows, 0:256] = jnp.concatenate(dq_parts, axis=-1).astype(BF)
            dz_ref[rows, 256:512] = jnp.concatenate(dk_parts, axis=-1).astype(BF)

    rev = lambda i: nt - 1 - i
    hspec = pl.BlockSpec((RET_H, tt, LANE), lambda i: (0, rev(i), 0))
    cspec = pl.BlockSpec((RET_H, CHUNK, LANE), lambda i: (0, 0, 0))
    tab = pl.BlockSpec((tt, LANE), lambda i: (rev(i), 0))
    return pl.pallas_call(
        body, name="ret_bwd", grid=(nt,),
        in_specs=[pl.BlockSpec((tt, 512), lambda i: (rev(i), 0)), hspec, hspec,
                  pl.BlockSpec((tt, 512), lambda i: (rev(i), 1)), hspec, tab, tab, cspec, cspec, cspec, cspec],
        out_specs=pl.BlockSpec((tt, 1024), lambda i: (rev(i), 0)),
        out_shape=jax.ShapeDtypeStruct((T, 1024), BF),
        scratch_shapes=[pltpu.VMEM((RET_H, CHUNK, LANE), F32)],
        compiler_params=_params(("arbitrary",), VMEM_BIG),
    )(d_o, qr, kr, z_a, states, cos_t, sin_t, decay, zeta, xi, gcb)


def _fox_bwd(q2, k, v, do, sub=512):
    H, T, _ = k.shape
    tb = 2 * sub
    n = T // sub

    def body(q_ref, do_ref, k_ref, v_ref, dq_ref, dk_ref, dv_ref, dk_sc, dv_sc):
        j = pl.program_id(1)

        @pl.when(j == 0)
        def _():
            dq_ref[...] = jnp.zeros(dq_ref.shape, F32)

        kk, vv = k_ref[...], v_ref[...]
        dk_sc[...] = jnp.zeros(dk_sc.shape, F32)
        dv_sc[...] = jnp.zeros(dv_sc.shape, F32)
        krow = lax.broadcasted_iota(jnp.int32, (tb, sub), 0)
        qcol = lax.broadcasted_iota(jnp.int32, (tb, sub), 1)

        def step(i, shift):
            off = pl.multiple_of(i * sub, sub)
            qq = q_ref[pl.ds(off, sub), :]
            dd = do_ref[pl.ds(off, sub), :]
            p = jnp.exp(_nt(kk, qq))
            if shift is not None:
                p = jnp.where(qcol + shift >= krow, p, 0.0)
            ds = (p * _nt(vv, dd)).astype(BF)
            dv_sc[...] += _nn(p.astype(BF), dd)
            dk_sc[...] += _nn(ds, qq)
            dq_ref[pl.ds(off, sub), :] += _tn(ds, kk)

        step(2 * j, 0)
        step(2 * j + 1, sub)

        def loop_body(i, carry):
            step(i, None)
            return carry

        lax.fori_loop(2 * j + 2, n, loop_body, 0)
        dk_ref[...] = dk_sc[...]
        dv_ref[...] = dv_sc[...]

    blk = pl.BlockSpec((None, tb, LANE), lambda h, j: (h, j, 0))
    full = pl.BlockSpec((None, T, LANE), lambda h, j: (h, 0, 0))
    shp = jax.ShapeDtypeStruct((H, T, LANE), F32)
    return pl.pallas_call(
        body, name="fox_bwd", grid=(H, T // tb),
        in_specs=[full, full, blk, blk], out_specs=[full, blk, blk], out_shape=[shp, shp, shp],
        scratch_shapes=[pltpu.VMEM((tb, LANE), F32), pltpu.VMEM((tb, LANE), F32)],
        compiler_params=_params(("arbitrary", "arbitrary"), VMEM_BIG),
    )(q2, do, k, v)


def _fox_post_bwd(dq, dk, dv, z_a, z_ff, b_f, g_q, g_k, tm=256):
    T = z_a.shape[0]
    nt = T // tm

    def body(dq_ref, dk_ref, dv_ref, zf_ref, zff_ref, b_ref, gq_ref, gk_ref,
             dz_ref, dff_ref, dgq_ref, dgk_ref, db_ref, carry):
        i = pl.program_id(0)

        @pl.when(i == 0)
        def _():
            carry[...] = jnp.zeros(carry.shape, F32)
            dgq_ref[...] = jnp.zeros(dgq_ref.shape, F32)
            dgk_ref[...] = jnp.zeros(dgk_ref.shape, F32)
            db_ref[...] = jnp.zeros(db_ref.shape, F32)

        lane = lax.broadcasted_iota(jnp.int32, (tm, LANE), 1)
        zf = zf_ref[...]
        dcm = jnp.zeros((tm, LANE), F32)
        dq_parts, dk_parts, dv_parts = [], [], []
        gq_acc = jnp.zeros((1, 64), F32)
        gk_acc = jnp.zeros((1, 64), F32)
        for h in range(FOX_H):
            dqa, dka = dq_ref[h], dk_ref[h]
            dcm = jnp.where(lane == h, dqa[:, L_CQ:L_CQ + 1] - dka[:, L_CK:L_CK + 1], dcm)
            for src, dya, g_ref, scale, parts in ((0, dqa, gq_ref, 0.125, dq_parts), (512, dka, gk_ref, 1.0, dk_parts)):
                xh = zf[:, src + 64 * h:src + 64 * h + 64]
                r = lax.rsqrt(jnp.mean(xh * xh, axis=-1, keepdims=True) + EPS)
                xn = xh * r
                dy = dya[:, :FOX_D] * scale
                if src == 0:
                    gq_acc = gq_acc + jnp.sum(dy * xn, axis=0, keepdims=True)
                else:
                    gk_acc = gk_acc + jnp.sum(dy * xn, axis=0, keepdims=True)
                dxn = dy * g_ref[...]
                parts.append(r * (dxn - xn * jnp.mean(dxn * xn, axis=-1, keepdims=True)))
            dv_parts.append(dv_ref[h][:, :FOX_D])
        dz_ref[...] = jnp.concatenate(dq_parts + dk_parts + dv_parts, axis=-1).astype(BF)
        zpad = jnp.zeros((1, 64), F32)
        dgq_ref[...] += jnp.concatenate([gq_acc, zpad], axis=-1)
        dgk_ref[...] += jnp.concatenate([gk_acc, zpad], axis=-1)

        row = lax.broadcasted_iota(jnp.int32, (tm, tm), 0)
        col = lax.broadcasted_iota(jnp.int32, (tm, tm), 1)
        tri = (row <= col).astype(BF)
        hi, mid, lo = _split3(dcm)
        dlogf = _nn(tri, hi) + _nn(tri, mid) + _nn(tri, lo) + carry[...]
        carry[...] = dlogf[0:1, :]
        dff = jnp.where(lane < FOX_H, dlogf * _sigmoid(-(zff_ref[...] + b_ref[...])), 0.0)
        dff_ref[...] = dff.astype(BF)
        db_ref[...] += jnp.sum(dff, axis=0, keepdims=True)

    rev = lambda i: nt - 1 - i
    hsp = pl.BlockSpec((FOX_H, tm, LANE), lambda i: (0, rev(i), 0))
    small = lambda w: pl.BlockSpec((1, w), lambda i: (0, 0))
    return pl.pallas_call(
        body, name="fox_post_bwd", grid=(nt,),
        in_specs=[hsp, hsp, hsp, pl.BlockSpec((tm, 1536), lambda i: (rev(i), 1)),
                  pl.BlockSpec((tm, LANE), lambda i: (rev(i), 0)), small(LANE), small(64), small(64)],
        out_specs=[pl.BlockSpec((tm, 1536), lambda i: (rev(i), 0)), pl.BlockSpec((tm, LANE), lambda i: (rev(i), 0)),
                   small(LANE), small(LANE), small(LANE)],
        out_shape=[jax.ShapeDtypeStruct((T, 1536), BF), jax.ShapeDtypeStruct((T, LANE), BF),
                   jax.ShapeDtypeStruct((1, LANE), F32), jax.ShapeDtypeStruct((1, LANE), F32),
                   jax.ShapeDtypeStruct((1, LANE), F32)],
        scratch_shapes=[pltpu.VMEM((1, LANE), F32)],
        compiler_params=_params(("arbitrary",), VMEM_BIG),
    )(dq, dk, dv, z_a, z_ff, b_f, g_q, g_k)


def _in_bwd(dz_ret, dz_gt, dz_fox, dz_a, dz_ff, w_a, w_ff, x, g_mix, dx2, tm=256):
    T = x.shape[0]

    def body(r_ref, t_ref, f_ref, a_ref, ff_ref, wa_ref, wf_ref, x_ref, g_ref, dx2_ref, dx_ref, dg_ref):
        i = pl.program_id(0)

        @pl.when(i == 0)
        def _():
            dg_ref[...] = jnp.zeros(dg_ref.shape, F32)

        dh = (_nt(r_ref[...], wa_ref[:, C_RET:C_GT]) + _nt(t_ref[...], wa_ref[:, C_GT:C_FOX])
              + _nt(f_ref[...], wa_ref[:, C_FOX:C_A]) + _nt(a_ref[...], wa_ref[:, C_A:C_END])
              + _nt(ff_ref[...], wf_ref[...]))
        xv = x_ref[...]
        r = lax.rsqrt(jnp.mean(xv * xv, axis=-1, keepdims=True) + EPS)
        xn = xv * r
        dg_ref[...] += jnp.sum(dh * xn, axis=0, keepdims=True)
        dxn = dh * g_ref[...]
        dx_ref[...] = dx2_ref[...] + r * (dxn - xn * jnp.mean(dxn * xn, axis=-1, keepdims=True))

    row = lambda w: pl.BlockSpec((tm, w), lambda i: (i, 0))
    const = lambda shp: pl.BlockSpec(shp, lambda i: (0,) * len(shp))
    return pl.pallas_call(
        body, name="in_bwd", grid=(T // tm,),
        in_specs=[row(1024), row(512), row(1536), row(2048), row(LANE), const((D_MODEL, C_END)),
                  const((D_MODEL, LANE)), row(1024), const((1, 1024)), row(1024)],
        out_specs=[row(1024), const((1, 1024))],
        out_shape=[jax.ShapeDtypeStruct((T, 1024), F32), jax.ShapeDtypeStruct((1, 1024), F32)],
        compiler_params=_params(("arbitrary",), VMEM_BIG),
    )(dz_ret, dz_gt, dz_fox, dz_a, dz_ff, w_a, w_ff, x, g_mix, dx2)


def _mesh_pos():
    return lax.axis_index("x"), lax.axis_index("y"), lax.axis_index("c")


def _staged_place(src, name):
    stacked = src.ndim == 3
    R, C = src.shape[-2:]
    tr = _row_tile(R, 128, 16)
    n = R // tr
    assert n >= 2

    def body(s_ref, o_ref, buf, sem):
        i = pl.program_id(0)
        slot = i % 2
        x, y, _ = _mesh_pos()
        kme = 2 * x + y

        def out_copy(s, step):
            return pltpu.make_async_copy(buf.at[s], o_ref.at[kme, pl.ds(pl.multiple_of(step * tr, tr), tr), :], sem.at[s])

        @pl.when(i >= 2)
        def _():
            out_copy(slot, i - 2).wait()

        buf[slot] = (s_ref[kme] if stacked else s_ref[...]).astype(BF)
        out_copy(slot, i).start()

        @pl.when(i == n - 1)
        def _():
            out_copy(1 - slot, i - 1).wait()
            out_copy(slot, i).wait()

    in_spec = (pl.BlockSpec((N_CHIP, tr, C), lambda i: (0, i, 0)) if stacked else pl.BlockSpec((tr, C), lambda i: (i, 0)))
    return pl.pallas_call(
        body, name=name, grid=(n,), in_specs=[in_spec], out_specs=pl.BlockSpec(memory_space=pl.ANY),
        out_shape=jax.ShapeDtypeStruct((N_CHIP, R, C), BF),
        scratch_shapes=[pltpu.VMEM((2, tr, C), BF), pltpu.SemaphoreType.DMA((2,))],
        compiler_params=_params(("arbitrary",)),
    )(src)


def _push_copies(src, land, send_sem, recv_sem, receiving):
    x, y, c = _mesh_pos()
    kme = 2 * x + y
    cps = []
    for w in range(len(land)):
        for j, (px, py) in enumerate([(1 - x, y), (x, 1 - y), (1 - x, 1 - y)]):
            kpeer = 2 * px + py
            cps.append(pltpu.make_async_remote_copy(
                src_ref=land[w].at[kme] if src is None else src[w].at[kpeer],
                dst_ref=land[w].at[kpeer if receiving else kme],
                send_sem=send_sem.at[3 * w + j], recv_sem=recv_sem.at[3 * w + j],
                device_id=(px, py, c), device_id_type=MESH))
    return cps


def _gather_in_place(stacks):
    n = len(stacks)

    def body(*refs):
        land, send_sem, recv_sem = refs[n:2 * n], refs[2 * n], refs[2 * n + 1]
        for cp in _push_copies(None, land, send_sem, recv_sem, False):
            cp.start()
        for cp in _push_copies(None, land, send_sem, recv_sem, True):
            cp.wait_recv()
            cp.wait_send()

    anyspec = pl.BlockSpec(memory_space=pl.ANY)
    return pl.pallas_call(
        body, name="gather_in_place", in_specs=[anyspec] * n, out_specs=[anyspec] * n,
        out_shape=[jax.ShapeDtypeStruct(s.shape, s.dtype) for s in stacks],
        input_output_aliases={i: i for i in range(n)},
        scratch_shapes=[pltpu.SemaphoreType.DMA((3 * n,)), pltpu.SemaphoreType.DMA((3 * n,))],
    )(*stacks)


def _scatter_partials(srcs, lands, small):
    n = len(srcs)

    def body(*refs):
        src, sv = refs[:n], refs[2 * n]
        land, svo = refs[2 * n + 1:3 * n + 1], refs[3 * n + 1]
        send_sem, recv_sem, ssend, srecv, sloc = refs[3 * n + 2:]
        x, y, c = _mesh_pos()
        me = 4 * x + 2 * y + c
        flips = [(b >> 2 & 1, b >> 1 & 1, b & 1) for b in range(1, 8)]
        others = [(1 - x if fx else x, 1 - y if fy else y, 1 - c if fc else c) for fx, fy, fc in flips]
        local = pltpu.make_async_copy(sv, svo.at[me], sloc)
        local.start()
        sends = []
        for j, (px, py, pc) in enumerate(others):
            cp = pltpu.make_async_remote_copy(
                src_ref=sv, dst_ref=svo.at[me], send_sem=ssend.at[j], recv_sem=srecv.at[j],
                device_id=(px, py, pc), device_id_type=MESH)
            cp.start()
            sends.append(cp)
        for cp in _push_copies(src, land, send_sem, recv_sem, False):
            cp.start()
            sends.append(cp)
        for j, (px, py, pc) in enumerate(others):
            pltpu.make_async_remote_copy(
                src_ref=sv, dst_ref=svo.at[4 * px + 2 * py + pc], send_sem=ssend.at[j], recv_sem=srecv.at[j],
                device_id=(px, py, pc), device_id_type=MESH).wait_recv()
        for cp in _push_copies(src, land, send_sem, recv_sem, True):
            cp.wait_recv()
        for cp in sends:
            cp.wait_send()
        local.wait()

    anyspec = pl.BlockSpec(memory_space=pl.ANY)
    return pl.pallas_call(
        body, name="scatter_partials",
        in_specs=[anyspec] * (2 * n + 1), out_specs=[anyspec] * (n + 1),
        out_shape=[jax.ShapeDtypeStruct(s.shape, s.dtype) for s in lands]
        + [jax.ShapeDtypeStruct((8,) + small.shape, small.dtype)],
        input_output_aliases={n + i: i for i in range(n)},
        scratch_shapes=[pltpu.SemaphoreType.DMA((3 * n,)), pltpu.SemaphoreType.DMA((3 * n,)),
                        pltpu.SemaphoreType.DMA((7,)), pltpu.SemaphoreType.DMA((7,)), pltpu.SemaphoreType.DMA],
    )(*srcs, *lands, small)


_HBM_SPEC = pl.BlockSpec(memory_space=pltpu.HBM)
_SEM_SPEC = pl.BlockSpec(memory_space=pltpu.SEMAPHORE)
_SPLIT_PARAMS = pltpu.CompilerParams(has_side_effects=pltpu.SideEffectType.DATAFLOW_SIDE_EFFECTING)


def _push_start(srcs, lands, after, name):
    n = len(lands)
    ns = 0 if srcs is None else n

    def body(*refs):
        src = None if srcs is None else refs[:n]
        land = refs[ns:ns + n]
        send_sem, recv_sem = refs[ns + n + 1], refs[ns + n + 2]
        for cp in _push_copies(src, land, send_sem, recv_sem, False):
            cp.start()
        refs[-1][...] = jnp.zeros(refs[-1].shape, F32)

    ops = [pltpu.with_memory_space_constraint(a, pltpu.HBM) for a in ([] if srcs is None else list(srcs)) + list(lands)]
    res = pl.pallas_call(
        body, name=name,
        out_shape=(pltpu.SemaphoreType.DMA((3 * n,)), pltpu.SemaphoreType.DMA((3 * n,)),
                   *[pltpu.HBM(a.shape, a.dtype) for a in ops], jax.ShapeDtypeStruct((8, LANE), F32)),
        in_specs=[_HBM_SPEC] * len(ops) + [pl.BlockSpec(memory_space=pl.ANY)],
        out_specs=(_SEM_SPEC, _SEM_SPEC, *([_HBM_SPEC] * len(ops)), pl.BlockSpec(memory_space=pltpu.VMEM)),
        input_output_aliases={i: 2 + i for i in range(len(ops))},
        compiler_params=_SPLIT_PARAMS,
    )(*ops, after)
    return res[0], res[1], list(res[2:2 + len(ops)]), res[-1]


def _push_wait(send_sem, recv_sem, bufs, after, name, has_src):
    n = len(bufs) // 2 if has_src else len(bufs)
    ns = n if has_src else 0

    def body(*refs):
        src = refs[:n] if has_src else None
        land = refs[ns:ns + n]
        for cp in _push_copies(src, land, refs[ns + n], refs[ns + n + 1], True):
            cp.wait_send()
            cp.wait_recv()

    res = pl.pallas_call(
        body, name=name,
        out_shape=tuple(pltpu.HBM(a.shape, a.dtype) for a in bufs),
        in_specs=[_HBM_SPEC] * len(bufs) + [_SEM_SPEC, _SEM_SPEC, pl.BlockSpec(memory_space=pl.ANY)],
        out_specs=tuple([_HBM_SPEC] * len(bufs)),
        input_output_aliases={i: i for i in range(len(bufs))},
        compiler_params=_SPLIT_PARAMS,
    )(*bufs, send_sem, recv_sem, after)
    return list(res[ns:ns + n])


def _sibling_exchange(arrs):
    n = len(arrs)

    def body(*refs):
        ins, outs = refs[:n], refs[n:2 * n]
        send_sems, recv_sems = refs[2 * n:]
        x, y, c = _mesh_pos()
        cps = [pltpu.make_async_remote_copy(
            src_ref=ins[w], dst_ref=outs[w], send_sem=send_sems.at[w], recv_sem=recv_sems.at[w],
            device_id=(x, y, 1 - c), device_id_type=MESH) for w in range(n)]
        for cp in cps:
            cp.start()
        for cp in cps:
            cp.wait_recv()
        for cp in cps:
            cp.wait_send()

    anyspec = pl.BlockSpec(memory_space=pl.ANY)
    return pl.pallas_call(
        body, name="sibling_exchange",
        in_specs=[anyspec] * n, out_specs=[anyspec] * n,
        out_shape=[jax.ShapeDtypeStruct(a.shape, a.dtype) for a in arrs],
        scratch_shapes=[pltpu.SemaphoreType.DMA((n,)), pltpu.SemaphoreType.DMA((n,))],
    )(*arrs)


def _sum_stack(stack, name):
    _, R, C = stack.shape
    tr = _row_tile(R, 256, 16)

    def body(s_ref, o_ref):
        acc = s_ref[0].astype(F32)
        for k in range(1, N_CHIP):
            acc = acc + s_ref[k].astype(F32)
        o_ref[...] = acc

    return pl.pallas_call(
        body, name=name, grid=(R // tr,),
        in_specs=[pl.BlockSpec((N_CHIP, tr, C), lambda i: (0, i, 0))],
        out_specs=pl.BlockSpec((tr, C), lambda i: (i, 0)),
        out_shape=jax.ShapeDtypeStruct((R, C), F32),
        compiler_params=_params(("parallel",)),
    )(stack)


def _adam_math(w, g, m, v):
    m2 = ADAM_B1 * m + (1.0 - ADAM_B1) * g
    v2 = ADAM_B2 * v + (1.0 - ADAM_B2) * (g * g)
    m_hat = m2 / (1.0 - ADAM_B1 ** ADAM_STEP)
    v_hat = v2 / (1.0 - ADAM_B2 ** ADAM_STEP)
    delta = -ADAM_LR * (m_hat / (jnp.sqrt(v_hat) + ADAM_EPS) + ADAM_WD * w)
    return delta, m2, v2


def _adamw(w, m, v, s0, s1, name):
    R, C = w.shape
    tr = _row_tile(R, 128, 8)

    def body(w_ref, m_ref, v_ref, a_ref, b_ref, g_ref, d_ref, m2_ref, v2_ref):
        g = a_ref[...] + b_ref[...]
        delta, m2, v2 = _adam_math(w_ref[...], g, m_ref[...], v_ref[...])
        g_ref[...] = g
        d_ref[...] = delta
        m2_ref[...] = m2
        v2_ref[...] = v2

    spec = pl.BlockSpec((tr, C), lambda i: (i, 0))
    shp = jax.ShapeDtypeStruct((R, C), F32)
    return pl.pallas_call(
        body, name=name, grid=(R // tr,), in_specs=[spec] * 5, out_specs=[spec] * 4, out_shape=[shp] * 4,
        compiler_params=_params(("parallel",), VMEM_BIG),
    )(w, m, v, s0, s1)


def _adamw_small(w, m, v, gathered):
    def body(w_ref, m_ref, v_ref, s_ref, g_ref, d_ref, m2_ref, v2_ref):
        g = s_ref[0]
        for d in range(1, 8):
            g = g + s_ref[d]
        delta, m2, v2 = _adam_math(w_ref[...], g, m_ref[...], v_ref[...])
        g_ref[...] = g
        d_ref[...] = delta
        m2_ref[...] = m2
        v2_ref[...] = v2

    shp = jax.ShapeDtypeStruct(w.shape, F32)
    return pl.pallas_call(body, name="adamw_small", out_shape=[shp] * 4)(w, m, v, gathered)


SMALL = (("g_mix", 1024), ("g_ffn", 1024), ("g_ret_norm", 512), ("g_fox_q", 64), ("g_fox_k", 64), ("b_forget", 8))
SMALL_W = 3072


def _pack_small(parts):
    cols = []
    for (name, n) in SMALL:
        p = parts[name].reshape(1, -1)[:, :n]
        pad = -n % LANE
        cols.append(jnp.pad(p, ((0, 0), (0, pad))) if pad else p)
    used = sum(c.shape[1] for c in cols)
    cols.append(jnp.zeros((1, SMALL_W - used), F32))
    return jnp.concatenate(cols, axis=1)


def _unpack_small(vec):
    out, off = {}, 0
    for (name, n) in SMALL:
        out[name] = vec[:, off:off + n]
        off += n + (-n % LANE)
    return out


def kernel(x, g_mix, w_in, b_forget, g_ret_norm, w_ret_o, g_fox_q, g_fox_k, w_fox_o, w_out, g_ffn, w_gate, w_up, w_down, loss_target, m_g_mix, m_w_in, m_b_forget, m_g_ret_norm, m_w_ret_o, m_g_fox_q, m_g_fox_k, m_w_fox_o, m_w_out, m_g_ffn, m_w_gate, m_w_up, m_w_down, v_g_mix, v_w_in, v_b_forget, v_g_ret_norm, v_w_ret_o, v_g_fox_q, v_g_fox_k, v_w_fox_o, v_w_out, v_g_ffn, v_w_gate, v_w_up, v_w_down):
    T = x.shape[1]
    xs = x[0]
    tgt = loss_target[0]
    big_names = ("w_in", "w_ret_o", "w_fox_o", "w_out", "w_gate", "w_up", "w_down")
    big_w = dict(w_in=w_in[0], w_ret_o=w_ret_o[0], w_fox_o=w_fox_o[0], w_out=w_out[0], w_gate=w_gate[0],
                 w_up=w_up[0], w_down=w_down[0])
    big_m = dict(w_in=m_w_in[0], w_ret_o=m_w_ret_o[0], w_fox_o=m_w_fox_o[0], w_out=m_w_out[0], w_gate=m_w_gate[0],
                 w_up=m_w_up[0], w_down=m_w_down[0])
    big_v = dict(w_in=v_w_in[0], w_ret_o=v_w_ret_o[0], w_fox_o=v_w_fox_o[0], w_out=v_w_out[0], w_gate=v_w_gate[0],
                 w_up=v_w_up[0], w_down=v_w_down[0])
    small_w = dict(g_mix=g_mix, g_ffn=g_ffn, g_ret_norm=g_ret_norm, g_fox_q=g_fox_q, g_fox_k=g_fox_k, b_forget=b_forget)
    small_m = dict(g_mix=m_g_mix, g_ffn=m_g_ffn, g_ret_norm=m_g_ret_norm, g_fox_q=m_g_fox_q, g_fox_k=m_g_fox_k,
                   b_forget=m_b_forget)
    small_v = dict(g_mix=v_g_mix, g_ffn=v_g_ffn, g_ret_norm=v_g_ret_norm, g_fox_q=v_g_fox_q, g_fox_k=v_g_fox_k,
                   b_forget=v_b_forget)

    stacks = [_staged_place(big_w[n], "place_" + n) for n in big_names]
    (s_in,) = _gather_in_place(stacks[:1])
    w_send, w_recv, w_bufs, w_tok = _push_start(None, stacks[1:], s_in, "gather_rest_start")
    w_a, w_ff = _assemble_w_in(s_in)
    b_pad = jnp.pad(b_forget, ((0, 0), (0, LANE - FOX_H)))
    cos_t, sin_t = _rope_tables(T)
    consts = _ret_consts()

    h = _rms_cast(xs, g_mix + w_tok[0:1, 0:1])
    z_a = _mm_nn(h, w_a, "proj_in")
    z_ff = _mm_nn(h, w_ff, "proj_ff")
    qr, kr, qf, kf, vf = _mix_prep(z_a, z_ff, cos_t, sin_t, b_pad, g_fox_q, g_fox_k)
    o_raw, u_r, states = _ret_fwd(qr, kr, z_a, g_ret_norm, consts)
    o_fox, q2 = _fox_fwd(qf, kf, vf)
    s_ro, s_fo, s_out, s_gate, s_up, s_down = _push_wait(w_send, w_recv, w_bufs, q2, "gather_rest_wait", False)
    y_r, y_f, mrg, x2, h2, o_cat = _merge_out(u_r, o_fox, z_a, xs, g_ffn, s_ro, s_fo, s_out)
    gp, up, act, dy, loss_vec = _ffn_fwd(h2, x2, tgt, s_gate, s_up, s_down)
    loss = lax.psum(0.5 / D_MODEL * jnp.sum(loss_vec), ("x", "y", "c"))

    dgp, dup, dx2, dg_ffn = _ffn_bwd(dy, gp, up, x2, g_ffn, s_gate, s_up, s_down)
    ffn_part = [_grad_bstack(h2, dgp, "gw_gate"), _grad_bstack(h2, dup, "gw_up"), _grad_astack(act, dy, "gw_down")]
    f_send, f_recv, f_bufs, f_tok = _push_start(
        ffn_part, [_staged_place(g, "place_g_" + n) for g, n in zip(ffn_part, big_names[4:])], dx2, "scatter_ffn_start")
    d_yr, d_yf, dz_gt, dz_a, d_o, do_fox, dg_ret = _out_bwd(dx2, z_a, y_r, y_f, o_raw, o_fox,
                                                            g_ret_norm + f_tok[0:1, 0:1], s_ro, s_fo, s_out)
    dz_ret = _ret_bwd(d_o, qr, kr, z_a, states, cos_t, sin_t, consts)
    dq_f, dk_f, dv_f = _fox_bwd(q2, kf, vf, do_fox)
    dz_fox, dz_ff, dg_q, dg_k, db_f = _fox_post_bwd(dq_f, dk_f, dv_f, z_a, z_ff, b_pad, g_fox_q, g_fox_k)
    grad_x, dg_mix = _in_bwd(dz_ret, dz_gt, dz_fox, dz_a, dz_ff, w_a, w_ff, xs, g_mix, dx2)

    g_in = _pack_g_in(_grad_plain(h, dz_ret, "gw_in_ret", F32), _grad_plain(h, dz_gt, "gw_in_gt", F32),
                      _grad_plain(h, dz_fox, "gw_in_fox", F32, tn=768), _grad_plain(h, dz_a, "gw_in_a", F32),
                      _grad_plain(h, dz_ff, "gw_in_ff", F32))
    late_part = [g_in, _grad_colstack(u_r, d_yr, "gw_ret_o", 256), _grad_colstack(o_cat, d_yf, "gw_fox_o", 256),
                 _grad_plain(mrg, dx2, "gw_out", BF).reshape(N_CHIP, 256, D_MODEL)]
    small_g = _pack_small(dict(g_mix=dg_mix, g_ffn=dg_ffn, g_ret_norm=dg_ret, g_fox_q=dg_q, g_fox_k=dg_k, b_forget=db_f))

    late_land = [_staged_place(g, "place_g_" + n) for g, n in zip(late_part, big_names[:4])]
    recv = _scatter_partials(late_part, late_land, small_g)
    recv_ffn = _push_wait(f_send, f_recv, f_bufs, recv[0], "scatter_ffn_wait", True)
    sums = [_sum_stack(r, "sum_" + n) for r, n in zip(list(recv[:4]) + recv_ffn, big_names)]
    sib = _sibling_exchange(sums)
    big_out = {n: _adamw(big_w[n], big_m[n], big_v[n], sums[i], sib[i], "adamw_" + n) for i, n in enumerate(big_names)}
    sg, sd, sm, sv = _adamw_small(_pack_small(small_w), _pack_small(small_m), _pack_small(small_v), recv[-1])
    small_out = [_unpack_small(t) for t in (sg, sd, sm, sv)]

    order = ("g_mix", "w_in", "b_forget", "g_ret_norm", "w_ret_o", "g_fox_q", "g_fox_k", "w_fox_o", "w_out", "g_ffn",
             "w_gate", "w_up", "w_down")
    outs = [loss, grad_x[None]]
    for idx in range(4):
        for n in order:
            outs.append(big_out[n][idx][None] if n in big_out else small_out[idx][n])
    return tuple(outs)
```

```python
import functools
import math

import numpy as np
import jax
import jax.numpy as jnp
from jax import lax
from jax.experimental import pallas as pl
from jax.experimental.pallas import tpu as pltpu

F32 = jnp.float32
BF = jnp.bfloat16
MESH = pl.DeviceIdType.MESH

D_MODEL = 1024
D_FF = 2816
N_CHIP = 4
FF_SH = D_FF // N_CHIP
IN_COLS = 5128
IN_SH = IN_COLS // N_CHIP
RET_H, RET_DV = 4, 128
FOX_H, FOX_D = 8, 64
CHUNK = 128
EPS = 1e-6
NEG = -1e30
LANE = 128
C_RET, C_GT, C_FOX, C_A, C_END = 0, 1024, 1536, 3072, 5120
L_CQ, L_CK, L_LSE, L_MAX = 64, 67, 70, 73

ADAM_LR, ADAM_B1, ADAM_B2, ADAM_EPS, ADAM_WD, ADAM_STEP = 0.001, 0.9, 0.999, 1e-08, 0.01, 10
VMEM_BIG = 56 * 1024 * 1024
GRAD_TK = 2048


def _nn(a, b):
    return lax.dot_general(a, b, (((1,), (0,)), ((), ())), preferred_element_type=F32)


def _nt(a, b):
    return lax.dot_general(a, b, (((1,), (1,)), ((), ())), preferred_element_type=F32)


def _tn(a, b):
    return lax.dot_general(a, b, (((0,), (0,)), ((), ())), preferred_element_type=F32)


def _split3(x):
    hi = x.astype(BF)
    r = x - hi.astype(F32)
    mid = r.astype(BF)
    lo = (r - mid.astype(F32)).astype(BF)
    return hi, mid, lo


def _sigmoid(x):
    return 0.5 * jnp.tanh(0.5 * x) + 0.5


def _swap32(x):
    lane = lax.broadcasted_iota(jnp.int32, x.shape, 1)
    return jnp.where(lane < 32, pltpu.roll(x, 96, 1), pltpu.roll(x, 32, 1))


def _params(sem, vmem=None):
    return pltpu.CompilerParams(dimension_semantics=sem, vmem_limit_bytes=vmem)


def _row_tile(rows, cap, mult):
    return max(d for d in range(mult, cap + 1, mult) if rows % d == 0)


def _assemble_w_in(stack, tr=256):
    def body(s_ref, a_ref, f_ref):
        full = jnp.concatenate([s_ref[k].astype(F32) for k in range(N_CHIP)], axis=-1)
        a_ref[...] = jnp.concatenate([full[:, :3072], full[:, 3080:IN_COLS]], axis=-1).astype(BF)
        f_ref[...] = jnp.concatenate([full[:, 3072:3080], jnp.zeros((tr, LANE - FOX_H), F32)], axis=-1).astype(BF)

    return pl.pallas_call(
        body, name="assemble_w_in", grid=(D_MODEL // tr,),
        in_specs=[pl.BlockSpec((N_CHIP, tr, IN_SH), lambda i: (0, i, 0))],
        out_specs=[pl.BlockSpec((tr, C_END), lambda i: (i, 0)), pl.BlockSpec((tr, LANE), lambda i: (i, 0))],
        out_shape=[jax.ShapeDtypeStruct((D_MODEL, C_END), BF), jax.ShapeDtypeStruct((D_MODEL, LANE), BF)],
        compiler_params=_params(("parallel",), VMEM_BIG),
    )(stack)


def _pack_g_in(g_ret, g_gt, g_fox, g_a, g_ff, tr=256):
    def body(r_ref, t_ref, x_ref, a_ref, f_ref, o_ref):
        full = jnp.concatenate([r_ref[...], t_ref[...], x_ref[...], f_ref[...][:, :FOX_H], a_ref[...]], axis=-1)
        for k in range(N_CHIP):
            o_ref[k] = full[:, k * IN_SH:(k + 1) * IN_SH].astype(BF)

    def spec(w):
        return pl.BlockSpec((tr, w), lambda i: (i, 0))

    return pl.pallas_call(
        body, name="pack_g_in", grid=(D_MODEL // tr,),
        in_specs=[spec(1024), spec(512), spec(1536), spec(2048), spec(LANE)],
        out_specs=pl.BlockSpec((N_CHIP, tr, IN_SH), lambda i: (0, i, 0)),
        out_shape=jax.ShapeDtypeStruct((N_CHIP, D_MODEL, IN_SH), BF),
        compiler_params=_params(("parallel",), VMEM_BIG),
    )(g_ret, g_gt, g_fox, g_a, g_ff)


def _rms_cast(x, g, tm=512):
    T = x.shape[0]

    def body(x_ref, g_ref, o_ref):
        xv = x_ref[...]
        r = lax.rsqrt(jnp.mean(xv * xv, axis=-1, keepdims=True) + EPS)
        o_ref[...] = (xv * r * g_ref[...]).astype(BF)

    return pl.pallas_call(
        body, name="rms_cast", grid=(T // tm,),
        in_specs=[pl.BlockSpec((tm, D_MODEL), lambda i: (i, 0)), pl.BlockSpec((1, D_MODEL), lambda i: (0, 0))],
        out_specs=pl.BlockSpec((tm, D_MODEL), lambda i: (i, 0)),
        out_shape=jax.ShapeDtypeStruct((T, D_MODEL), BF),
        compiler_params=_params(("parallel",)),
    )(x, g)


def _mm_nn(a, b, name, tm=512, tn=1024):
    M, K = a.shape
    N = b.shape[1]
    tn = min(tn, N)

    def body(a_ref, b_ref, o_ref):
        o_ref[...] = _nn(a_ref[...], b_ref[...])

    return pl.pallas_call(
        body, name=name, grid=(N // tn, M // tm),
        in_specs=[pl.BlockSpec((tm, K), lambda j, i: (i, 0)), pl.BlockSpec((K, tn), lambda j, i: (0, j))],
        out_specs=pl.BlockSpec((tm, tn), lambda j, i: (i, j)),
        out_shape=jax.ShapeDtypeStruct((M, N), F32),
        compiler_params=_params(("parallel", "parallel")),
    )(a, b)


def _mm_tn(a, b, name, grid, a_spec, b_spec, o_spec, out_shape, acc_shape):
    nk = grid[-1]

    def body(a_ref, b_ref, o_ref, acc):
        k = pl.program_id(len(grid) - 1)

        @pl.when(k == 0)
        def _():
            acc[...] = jnp.zeros(acc.shape, F32)

        acc[...] += _tn(a_ref[...].astype(BF), b_ref[...].astype(BF))

        @pl.when(k == nk - 1)
        def _():
            o_ref[...] = acc[...].astype(o_ref.dtype)

    return pl.pallas_call(
        body, name=name, grid=grid, in_specs=[a_spec, b_spec], out_specs=o_spec, out_shape=out_shape,
        scratch_shapes=[pltpu.VMEM(acc_shape, F32)],
        compiler_params=_params(("parallel",) * (len(grid) - 1) + ("arbitrary",), VMEM_BIG),
    )(a, b)


def _grad_plain(a, b, name, out_dtype, tk=GRAD_TK, tn=1024):
    T, M = a.shape
    N = b.shape[1]
    tn = min(tn, N)
    return _mm_tn(a, b, name, (N // tn, T // tk),
                  pl.BlockSpec((tk, M), lambda j, k: (k, 0)), pl.BlockSpec((tk, tn), lambda j, k: (k, j)),
                  pl.BlockSpec((M, tn), lambda j, k: (0, j)), jax.ShapeDtypeStruct((M, N), out_dtype), (M, tn))


def _grad_colstack(a, b, name, wcol, tk=GRAD_TK):
    T, M = a.shape
    S = b.shape[1] // wcol
    return _mm_tn(a, b, name, (S, T // tk),
                  pl.BlockSpec((tk, M), lambda s, k: (k, 0)), pl.BlockSpec((tk, wcol), lambda s, k: (k, s)),
                  pl.BlockSpec((None, M, wcol), lambda s, k: (s, 0, 0)),
                  jax.ShapeDtypeStruct((S, M, wcol), BF), (M, wcol))


def _grad_bstack(a, b, name, tk=GRAD_TK):
    T, M = a.shape
    S, _, n = b.shape
    return _mm_tn(a, b, name, (S, T // tk),
                  pl.BlockSpec((tk, M), lambda s, k: (k, 0)), pl.BlockSpec((None, tk, n), lambda s, k: (s, k, 0)),
                  pl.BlockSpec((None, M, n), lambda s, k: (s, 0, 0)),
                  jax.ShapeDtypeStruct((S, M, n), BF), (M, n))


def _grad_astack(a, b, name, tk=GRAD_TK):
    S, T, m = a.shape
    N = b.shape[1]
    return _mm_tn(a, b, name, (S, T // tk),
                  pl.BlockSpec((None, tk, m), lambda s, k: (s, k, 0)), pl.BlockSpec((tk, N), lambda s, k: (k, 0)),
                  pl.BlockSpec((None, m, N), lambda s, k: (s, 0, 0)),
                  jax.ShapeDtypeStruct((S, m, N), BF), (m, N))


def _rope_tables(T):
    half = 32
    pos = jnp.arange(T, dtype=F32)
    inv_freq = 1.0 / (10000.0 ** (jnp.arange(half, dtype=F32) / half))
    ang = pos[:, None] * inv_freq[None, :]
    cos, sin = jnp.cos(ang), jnp.sin(ang)
    z = jnp.zeros((T, 64), F32)
    return jnp.concatenate([cos, cos, z], axis=-1), jnp.concatenate([-sin, sin, z], axis=-1)


def _ret_consts():
    h = np.arange(RET_H, dtype=np.float32)
    log_g = np.log1p(-(np.float32(2.0) ** (-5.0 - h))).astype(np.float32)
    idx = np.arange(CHUNK, dtype=np.float32)
    diff = idx[:, None] - idx[None, :]
    decay = np.where(diff[None] >= 0, np.exp(np.maximum(diff, 0.0)[None] * log_g[:, None, None]), 0.0)
    zeta = np.exp((CHUNK - 1.0 - idx)[None, :] * log_g[:, None])
    xi = np.exp((idx + 1.0)[None, :] * log_g[:, None])
    gc = np.exp(CHUNK * log_g)
    bc = lambda v: np.broadcast_to(v[:, :, None], (RET_H, CHUNK, LANE)).astype(np.float32)
    gcb = np.broadcast_to(gc[:, None, None], (RET_H, CHUNK, LANE)).astype(np.float32)
    return (jnp.asarray(decay.astype(np.float32)), jnp.asarray(bc(zeta)), jnp.asarray(bc(xi)), jnp.asarray(gcb))


def _mix_prep(z_a, z_ff, cos_t, sin_t, b_f, g_q, g_k, tm=256):
    T = z_a.shape[0]

    def body(zqk_ref, zf_ref, zff_ref, cos_ref, sin_ref, b_ref, gq_ref, gk_ref,
             qr_ref, kr_ref, qf_ref, kf_ref, vf_ref, carry):
        i = pl.program_id(0)

        @pl.when(i == 0)
        def _():
            carry[...] = jnp.zeros(carry.shape, F32)

        lane = lax.broadcasted_iota(jnp.int32, (tm, LANE), 1)
        zpad = jnp.zeros((tm, 64), F32)
        cosv, sinv = cos_ref[...], sin_ref[...]
        zqk = zqk_ref[...]
        for h in range(RET_H):
            for src, dst, scale in ((0, qr_ref, 1.0), (256, kr_ref, 0.125)):
                xh = jnp.concatenate([zqk[:, src + 64 * h: src + 64 * h + 64], zpad], axis=-1)
                rot = xh * cosv + _swap32(xh) * sinv
                dst[h] = (rot * scale).astype(BF)

        lf_in = zff_ref[...] + b_ref[...]
        logf = jnp.minimum(lf_in, 0.0) - jnp.log(1.0 + jnp.exp(-jnp.abs(lf_in)))
        row = lax.broadcasted_iota(jnp.int32, (tm, tm), 0)
        col = lax.broadcasted_iota(jnp.int32, (tm, tm), 1)
        tri = (row >= col).astype(BF)
        hi, mid, lo = _split3(logf)
        cs = _nn(tri, hi) + _nn(tri, mid) + _nn(tri, lo) + carry[...]
        carry[...] = cs[tm - 1:tm, :]

        zf = zf_ref[...]
        one = jnp.ones((tm, LANE), F32)
        for h in range(FOX_H):
            c = cs[:, h:h + 1]
            chi, cmid, clo = [t.astype(F32) for t in _split3(c)]
            qh = zf[:, 64 * h:64 * h + 64]
            kh = zf[:, 512 + 64 * h:512 + 64 * h + 64]
            vh = zf[:, 1024 + 64 * h:1024 + 64 * h + 64]
            qn = qh * lax.rsqrt(jnp.mean(qh * qh, axis=-1, keepdims=True) + EPS) * gq_ref[...] * 0.125
            kn = kh * lax.rsqrt(jnp.mean(kh * kh, axis=-1, keepdims=True) + EPS) * gk_ref[...]
            qa = jnp.concatenate([qn, zpad], axis=-1)
            qa = jnp.where(lane == L_CQ, chi, jnp.where(lane == L_CQ + 1, cmid, jnp.where(lane == L_CQ + 2, clo, qa)))
            qa = jnp.where((lane >= L_CK) & (lane < L_CK + 3), one, qa)
            ka = jnp.concatenate([kn, zpad], axis=-1)
            ka = jnp.where(lane == L_CK, -chi, jnp.where(lane == L_CK + 1, -cmid, jnp.where(lane == L_CK + 2, -clo, ka)))
            ka = jnp.where(((lane >= L_CQ) & (lane < L_CQ + 3)) | ((lane >= L_LSE) & (lane < L_MAX + 3)), one, ka)
            va = jnp.concatenate([vh, zpad], axis=-1)
            va = jnp.where((lane >= 64) & (lane < 67), one, va)
            qf_ref[h] = qa.astype(BF)
            kf_ref[h] = ka.astype(BF)
            vf_ref[h] = va.astype(BF)

    hspec4 = pl.BlockSpec((RET_H, tm, LANE), lambda i: (0, i, 0))
    hspec8 = pl.BlockSpec((FOX_H, tm, LANE), lambda i: (0, i, 0))
    small = lambda w: pl.BlockSpec((1, w), lambda i: (0, 0))
    return pl.pallas_call(
        body, name="mix_prep", grid=(T // tm,),
        in_specs=[pl.BlockSpec((tm, 512), lambda i: (i, 0)), pl.BlockSpec((tm, 1536), lambda i: (i, 1)),
                  pl.BlockSpec((tm, LANE), lambda i: (i, 0)), pl.BlockSpec((tm, LANE), lambda i: (i, 0)),
                  pl.BlockSpec((tm, LANE), lambda i: (i, 0)), small(LANE), small(64), small(64)],
        out_specs=[hspec4, hspec4, hspec8, hspec8, hspec8],
        out_shape=[jax.ShapeDtypeStruct((RET_H, T, LANE), BF)] * 2 + [jax.ShapeDtypeStruct((FOX_H, T, LANE), BF)] * 3,
        scratch_shapes=[pltpu.VMEM((1, LANE), F32)],
        compiler_params=_params(("arbitrary",), VMEM_BIG),
    )(z_a, z_a, z_ff, cos_t, sin_t, b_f, g_q, g_k)


def _ret_fwd(qr, kr, z_a, g_ret, consts, tt=512):
    T = z_a.shape[0]
    nch = tt // CHUNK
    decay, zeta, xi, gcb = consts

    def body(q_ref, k_ref, v_ref, gt_ref, g_ref, d_ref, ze_ref, xi_ref, gc_ref, o_ref, u_ref, st_ref, r_sc):
        i = pl.program_id(0)

        @pl.when(i == 0)
        def _():
            r_sc[...] = jnp.zeros(r_sc.shape, F32)

        for c in range(nch):
            rows = slice(c * CHUNK, (c + 1) * CHUNK)
            for h in range(RET_H):
                cols = slice(h * RET_DV, (h + 1) * RET_DV)
                q, k = q_ref[h, rows, :], k_ref[h, rows, :]
                v32 = v_ref[rows, cols]
                r = r_sc[h]
                st_ref[h, rows, :] = r
                s = _nt(q, k) * d_ref[h]
                o = _nn(s.astype(BF), v32.astype(BF)) + _nn(q, r.astype(BF)) * xi_ref[h]
                r_sc[h] = gc_ref[h] * r + _tn(k, (v32 * ze_ref[h]).astype(BF))
                o_ref[rows, cols] = o
                mu = jnp.mean(o, axis=-1, keepdims=True)
                xc = o - mu
                on = xc * lax.rsqrt(jnp.mean(xc * xc, axis=-1, keepdims=True) + EPS)
                gt = gt_ref[rows, cols]
                u_ref[rows, cols] = (gt * _sigmoid(gt) * (on * g_ref[:, cols])).astype(BF)

    hspec = pl.BlockSpec((RET_H, tt, LANE), lambda i: (0, i, 0))
    cspec = pl.BlockSpec((RET_H, CHUNK, LANE), lambda i: (0, 0, 0))
    return pl.pallas_call(
        body, name="ret_fwd", grid=(T // tt,),
        in_specs=[hspec, hspec, pl.BlockSpec((tt, 512), lambda i: (i, 1)), pl.BlockSpec((tt, 512), lambda i: (i, 2)),
                  pl.BlockSpec((1, 512), lambda i: (0, 0)), cspec, cspec, cspec, cspec],
        out_specs=[pl.BlockSpec((tt, 512), lambda i: (i, 0)), pl.BlockSpec((tt, 512), lambda i: (i, 0)), hspec],
        out_shape=[jax.ShapeDtypeStruct((T, 512), F32), jax.ShapeDtypeStruct((T, 512), BF),
                   jax.ShapeDtypeStruct((RET_H, T, LANE), F32)],
        scratch_shapes=[pltpu.VMEM((RET_H, CHUNK, LANE), F32)],
        compiler_params=_params(("arbitrary",), VMEM_BIG),
    )(qr, kr, z_a, z_a, g_ret, decay, zeta, xi, gcb)


def _fox_fwd(q, k, v, sub=512):
    H, T, _ = q.shape
    tb = 2 * sub

    def body(q_ref, k_ref, v_ref, o_ref, q2_ref, mx_sc, acc_sc):
        i = pl.program_id(1)
        lane = lax.broadcasted_iota(jnp.int32, (sub, LANE), 1)
        row = lax.broadcasted_iota(jnp.int32, (sub, sub), 0)
        col = lax.broadcasted_iota(jnp.int32, (sub, sub), 1)
        causal = row >= col
        qs = [q_ref[0:sub, :], q_ref[sub:tb, :]]
        d0 = pl.multiple_of(i * tb, tb)
        d1 = pl.multiple_of(i * tb + sub, sub)

        def lane_max(s):
            m = s[:, 0:LANE]
            for c in range(1, s.shape[1] // LANE):
                m = jnp.maximum(m, s[:, c * LANE:(c + 1) * LANE])
            return m

        mx_sc[...] = jnp.full(mx_sc.shape, NEG, F32)

        def max_body(j, carry):
            kb = k_ref[pl.ds(pl.multiple_of(j * tb, tb), tb), :]
            for a in range(2):
                mx_sc[a] = jnp.maximum(mx_sc[a], lane_max(_nt(qs[a], kb)))
            return carry

        lax.fori_loop(0, i, max_body, 0)
        k0, k1 = k_ref[pl.ds(d0, sub), :], k_ref[pl.ds(d1, sub), :]
        v0, v1 = v_ref[pl.ds(d0, sub), :], v_ref[pl.ds(d1, sub), :]
        mx = [jnp.maximum(mx_sc[0], lane_max(jnp.where(causal, _nt(qs[0], k0), NEG))),
              jnp.maximum(jnp.maximum(mx_sc[1], lane_max(_nt(qs[1], k0))),
                          lane_max(jnp.where(causal, _nt(qs[1], k1), NEG)))]
        ms = [jnp.max(t, axis=1, keepdims=True) for t in mx]

        def put3(base, first, val):
            hi, mid, lo = _split3(val)
            return jnp.where(lane == first, hi, jnp.where(lane == first + 1, mid, jnp.where(lane == first + 2, lo, base)))

        qm = [put3(qs[a], L_MAX, -ms[a]) for a in range(2)]

        acc_sc[...] = jnp.zeros(acc_sc.shape, F32)

        def acc_body(j, carry):
            off = pl.multiple_of(j * tb, tb)
            kb, vb = k_ref[pl.ds(off, tb), :], v_ref[pl.ds(off, tb), :]
            for a in range(2):
                acc_sc[a] += _nn(jnp.exp(_nt(qm[a], kb)).astype(BF), vb)
            return carry

        lax.fori_loop(0, i, acc_body, 0)

        def pv(qa, kk, vv, masked):
            p = jnp.exp(_nt(qa, kk))
            if masked:
                p = jnp.where(causal, p, 0.0)
            return _nn(p.astype(BF), vv)

        accs = [acc_sc[0] + pv(qm[0], k0, v0, True),
                acc_sc[1] + pv(qm[1], k0, v0, False) + pv(qm[1], k1, v1, True)]
        for a in range(2):
            rows = slice(a * sub, (a + 1) * sub)
            l = accs[a][:, 64:65]
            o_ref[rows, :] = jnp.where(lane < 64, accs[a] / l, 0.0)
            q2_ref[rows, :] = put3(qs[a], L_LSE, -(ms[a] + jnp.log(l)))

    blk = pl.BlockSpec((None, tb, LANE), lambda h, i: (h, i, 0))
    full = pl.BlockSpec((None, T, LANE), lambda h, i: (h, 0, 0))
    return pl.pallas_call(
        body, name="fox_fwd", grid=(H, T // tb),
        in_specs=[blk, full, full], out_specs=[blk, blk],
        out_shape=[jax.ShapeDtypeStruct((H, T, LANE), F32), jax.ShapeDtypeStruct((H, T, LANE), BF)],
        scratch_shapes=[pltpu.VMEM((2, sub, LANE), F32), pltpu.VMEM((2, sub, LANE), F32)],
        compiler_params=_params(("parallel", "arbitrary"), VMEM_BIG),
    )(q, k, v)


def _merge_out(u_r, o_fox, z_a, x, g_ffn, w_ro, w_fo, w_out, tm=256):
    T = x.shape[0]

    def body(u_ref, of_ref, ar_ref, af_ref, x_ref, g_ref, wro_ref, wfo_ref, wout_ref,
             yr_ref, yf_ref, m_ref, x2_ref, h2_ref, oc_ref):
        u = u_ref[...]
        oc = jnp.concatenate([of_ref[h][:, :FOX_D] for h in range(FOX_H)], axis=-1).astype(BF)
        oc_ref[...] = oc
        yr = jnp.concatenate([_nn(u, wro_ref[k]) for k in range(N_CHIP)], axis=-1)
        yf = jnp.concatenate([_nn(oc, wfo_ref[k]) for k in range(N_CHIP)], axis=-1)
        yr_ref[...] = yr
        yf_ref[...] = yf
        m = (_sigmoid(ar_ref[...]) * yr + _sigmoid(af_ref[...]) * yf).astype(BF)
        m_ref[...] = m
        x2 = x_ref[...]
        for k in range(N_CHIP):
            x2 = x2 + _nn(m[:, 256 * k:256 * k + 256], wout_ref[k])
        x2_ref[...] = x2
        r = lax.rsqrt(jnp.mean(x2 * x2, axis=-1, keepdims=True) + EPS)
        h2_ref[...] = (x2 * r * g_ref[...]).astype(BF)

    row = lambda w: pl.BlockSpec((tm, w), lambda i: (i, 0))
    const = lambda shp: pl.BlockSpec(shp, lambda i: (0,) * len(shp))
    return pl.pallas_call(
        body, name="merge_out", grid=(T // tm,),
        in_specs=[row(512), pl.BlockSpec((FOX_H, tm, LANE), lambda i: (0, i, 0)),
                  pl.BlockSpec((tm, 1024), lambda i: (i, 3)), pl.BlockSpec((tm, 1024), lambda i: (i, 4)),
                  row(1024), const((1, 1024)), const((N_CHIP, 512, 256)), const((N_CHIP, 512, 256)),
                  const((N_CHIP, 256, 1024))],
        out_specs=[row(1024), row(1024), row(1024), row(1024), row(1024), row(512)],
        out_shape=[jax.ShapeDtypeStruct((T, 1024), F32), jax.ShapeDtypeStruct((T, 1024), F32),
                   jax.ShapeDtypeStruct((T, 1024), BF), jax.ShapeDtypeStruct((T, 1024), F32),
                   jax.ShapeDtypeStruct((T, 1024), BF), jax.ShapeDtypeStruct((T, 512), BF)],
        compiler_params=_params(("parallel",), VMEM_BIG),
    )(u_r, o_fox, z_a, z_a, x, g_ffn, w_ro, w_fo, w_out)


def _ffn_fwd(h2, x2, tgt, w_gate, w_up, w_down, tm=512):
    T = h2.shape[0]

    def body(h_ref, x2_ref, t_ref, wg_ref, wu_ref, wd_ref, gp_ref, up_ref, act_ref, dy_ref, ls_ref, acc):
        i, k = pl.program_id(0), pl.program_id(1)

        @pl.when(k == 0)
        def _():
            acc[...] = jnp.zeros(acc.shape, F32)

        @pl.when((i == 0) & (k == 0))
        def _():
            ls_ref[...] = jnp.zeros(ls_ref.shape, F32)

        h = h_ref[...]
        gp = _nt(h, wg_ref[...])
        up = _nt(h, wu_ref[...])
        gp_ref[...] = gp
        up_ref[...] = up
        act = (gp * _sigmoid(gp) * up).astype(BF)
        act_ref[...] = act
        acc[...] += _nn(act, wd_ref[...])

        @pl.when(k == N_CHIP - 1)
        def _():
            err = x2_ref[...] + acc[...] - t_ref[...]
            dy_ref[...] = err * (1.0 / D_MODEL)
            ls_ref[...] += jnp.sum(err * err, axis=0, keepdims=True)

    row = pl.BlockSpec((tm, D_MODEL), lambda i, k: (i, 0))
    hid = pl.BlockSpec((None, tm, FF_SH), lambda i, k: (k, i, 0))
    return pl.pallas_call(
        body, name="ffn_fwd", grid=(T // tm, N_CHIP),
        in_specs=[row, row, row] + [pl.BlockSpec((None, FF_SH, D_MODEL), lambda i, k: (k, 0, 0))] * 3,
        out_specs=[hid, hid, hid, row, pl.BlockSpec((1, D_MODEL), lambda i, k: (0, 0))],
        out_shape=[jax.ShapeDtypeStruct((N_CHIP, T, FF_SH), F32), jax.ShapeDtypeStruct((N_CHIP, T, FF_SH), F32),
                   jax.ShapeDtypeStruct((N_CHIP, T, FF_SH), BF), jax.ShapeDtypeStruct((T, D_MODEL), F32),
                   jax.ShapeDtypeStruct((1, D_MODEL), F32)],
        scratch_shapes=[pltpu.VMEM((tm, D_MODEL), F32)],
        compiler_params=_params(("arbitrary", "arbitrary"), VMEM_BIG),
    )(h2, x2, tgt, w_gate, w_up, w_down)


def _ffn_bwd(dy, gp, up, x2, g_ffn, w_gate, w_up, w_down, tm=512):
    T = dy.shape[0]

    def body(dy_ref, gp_ref, up_ref, x2_ref, g_ref, wg_ref, wu_ref, wd_ref, dgp_ref, dup_ref, dx_ref, dg_ref, acc):
        i, k = pl.program_id(0), pl.program_id(1)

        @pl.when(k == 0)
        def _():
            acc[...] = jnp.zeros(acc.shape, F32)

        @pl.when((i == 0) & (k == 0))
        def _():
            dg_ref[...] = jnp.zeros(dg_ref.shape, F32)

        dy = dy_ref[...]
        dact = _nt(dy.astype(BF), wd_ref[...])
        gp, up = gp_ref[...], up_ref[...]
        sg = _sigmoid(gp)
        dup = (dact * gp * sg).astype(BF)
        dgp = (dact * up * sg * (1.0 + gp * (1.0 - sg))).astype(BF)
        dgp_ref[...] = dgp
        dup_ref[...] = dup
        acc[...] += _nn(dgp, wg_ref[...]) + _nn(dup, wu_ref[...])

        @pl.when(k == N_CHIP - 1)
        def _():
            x2 = x2_ref[...]
            r = lax.rsqrt(jnp.mean(x2 * x2, axis=-1, keepdims=True) + EPS)
            xn = x2 * r
            dh = acc[...]
            dg_ref[...] += jnp.sum(dh * xn, axis=0, keepdims=True)
            dxn = dh * g_ref[...]
            dx_ref[...] = dy + r * (dxn - xn * jnp.mean(dxn * xn, axis=-1, keepdims=True))

    row = pl.BlockSpec((tm, D_MODEL), lambda i, k: (i, 0))
    hid = pl.BlockSpec((None, tm, FF_SH), lambda i, k: (k, i, 0))
    vec = pl.BlockSpec((1, D_MODEL), lambda i, k: (0, 0))
    return pl.pallas_call(
        body, name="ffn_bwd", grid=(T // tm, N_CHIP),
        in_specs=[row, hid, hid, row, vec] + [pl.BlockSpec((None, FF_SH, D_MODEL), lambda i, k: (k, 0, 0))] * 3,
        out_specs=[hid, hid, row, vec],
        out_shape=[jax.ShapeDtypeStruct((N_CHIP, T, FF_SH), BF), jax.ShapeDtypeStruct((N_CHIP, T, FF_SH), BF),
                   jax.ShapeDtypeStruct((T, D_MODEL), F32), jax.ShapeDtypeStruct((1, D_MODEL), F32)],
        scratch_shapes=[pltpu.VMEM((tm, D_MODEL), F32)],
        compiler_params=_params(("arbitrary", "arbitrary"), VMEM_BIG),
    )(dy, gp, up, x2, g_ffn, w_gate, w_up, w_down)


def _out_bwd(dx2, z_a, y_r, y_f, o_raw, o_fox, g_ret, w_ro, w_fo, w_out, tm=256):
    T = dx2.shape[0]

    def body(dx_ref, gt_ref, ar_ref, af_ref, yr_ref, yf_ref, o_ref, of_ref, g_ref, wro_ref, wfo_ref, wout_ref,
             dyr_ref, dyf_ref, dgt_ref, da_ref, do_ref, dof_ref, dg_ref):
        i = pl.program_id(0)

        @pl.when(i == 0)
        def _():
            dg_ref[...] = jnp.zeros(dg_ref.shape, F32)

        dxb = dx_ref[...].astype(BF)
        dm = jnp.concatenate([_nt(dxb, wout_ref[k]) for k in range(N_CHIP)], axis=-1)
        sr, sf = _sigmoid(ar_ref[...]), _sigmoid(af_ref[...])
        dyr = dm * sr
        dyf = dm * sf
        da_ref[:, :1024] = (dyr * yr_ref[...] * (1.0 - sr)).astype(BF)
        da_ref[:, 1024:] = (dyf * yf_ref[...] * (1.0 - sf)).astype(BF)
        dyr = dyr.astype(BF)
        dyf = dyf.astype(BF)
        dyr_ref[...] = dyr
        dyf_ref[...] = dyf
        du = jnp.zeros((tm, 512), F32)
        doc = jnp.zeros((tm, 512), F32)
        for k in range(N_CHIP):
            du = du + _nt(dyr[:, 256 * k:256 * k + 256], wro_ref[k])
            doc = doc + _nt(dyf[:, 256 * k:256 * k + 256], wfo_ref[k])

        for h in range(RET_H):
            cols = slice(h * RET_DV, (h + 1) * RET_DV)
            o = o_ref[:, cols]
            mu = jnp.mean(o, axis=-1, keepdims=True)
            xc = o - mu
            rstd = lax.rsqrt(jnp.mean(xc * xc, axis=-1, keepdims=True) + EPS)
            on = xc * rstd
            g = g_ref[:, cols]
            gt = gt_ref[:, cols]
            sg = _sigmoid(gt)
            duh = du[:, cols]
            dgt_ref[:, cols] = (duh * (on * g) * sg * (1.0 + gt * (1.0 - sg))).astype(BF)
            dog = duh * gt * sg
            dg_ref[:, cols] += jnp.sum(dog * on, axis=0, keepdims=True)
            don = dog * g
            do_ref[:, cols] = rstd * (don - jnp.mean(don, axis=-1, keepdims=True)
                                      - on * jnp.mean(don * on, axis=-1, keepdims=True))

        lane = lax.broadcasted_iota(jnp.int32, (tm, LANE), 1)
        zpad = jnp.zeros((tm, 64), F32)
        for h in range(FOX_H):
            doh = doc[:, 64 * h:64 * h + 64]
            delta = jnp.sum(doh * of_ref[h][:, :FOX_D], axis=-1, keepdims=True)
            hi, mid, lo = [t.astype(F32) for t in _split3(-delta)]
            da = jnp.concatenate([doh, zpad], axis=-1)
            da = jnp.where(lane == 64, hi, jnp.where(lane == 65, mid, jnp.where(lane == 66, lo, da)))
            dof_ref[h] = da.astype(BF)

    row = lambda w: pl.BlockSpec((tm, w), lambda i: (i, 0))
    const = lambda shp: pl.BlockSpec(shp, lambda i: (0,) * len(shp))
    hsp = pl.BlockSpec((FOX_H, tm, LANE), lambda i: (0, i, 0))
    return pl.pallas_call(
        body, name="out_bwd", grid=(T // tm,),
        in_specs=[row(1024), pl.BlockSpec((tm, 512), lambda i: (i, 2)), pl.BlockSpec((tm, 1024), lambda i: (i, 3)),
                  pl.BlockSpec((tm, 1024), lambda i: (i, 4)), row(1024), row(1024), row(512), hsp,
                  const((1, 512)), const((N_CHIP, 512, 256)), const((N_CHIP, 512, 256)), const((N_CHIP, 256, 1024))],
        out_specs=[row(1024), row(1024), row(512), row(2048), row(512), hsp, const((1, 512))],
        out_shape=[jax.ShapeDtypeStruct((T, 1024), BF), jax.ShapeDtypeStruct((T, 1024), BF),
                   jax.ShapeDtypeStruct((T, 512), BF), jax.ShapeDtypeStruct((T, 2048), BF),
                   jax.ShapeDtypeStruct((T, 512), F32), jax.ShapeDtypeStruct((FOX_H, T, LANE), BF),
                   jax.ShapeDtypeStruct((1, 512), F32)],
        compiler_params=_params(("arbitrary",), VMEM_BIG),
    )(dx2, z_a, z_a, z_a, y_r, y_f, o_raw, o_fox, g_ret, w_ro, w_fo, w_out)


def _ret_bwd(d_o, qr, kr, z_a, states, cos_t, sin_t, consts, tt=512):
    T = z_a.shape[0]
    nt = T // tt
    nch = tt // CHUNK
    decay, zeta, xi, gcb = consts

    def body(do_ref, q_ref, k_ref, v_ref, st_ref, cos_ref, sin_ref, d_ref, ze_ref, xi_ref, gc_ref, dz_ref, g_sc):
        i = pl.program_id(0)

        @pl.when(i == 0)
        def _():
            g_sc[...] = jnp.zeros(g_sc.shape, F32)

        for c in reversed(range(nch)):
            rows = slice(c * CHUNK, (c + 1) * CHUNK)
            cosv, sinv = cos_ref[rows, :], sin_ref[rows, :]
            dq_parts, dk_parts = [], []
            for h in range(RET_H):
                cols = slice(h * RET_DV, (h + 1) * RET_DV)
                q, k = q_ref[h, rows, :], k_ref[h, rows, :]
                v32 = v_ref[rows, cols]
                vb = v32.astype(BF)
                r = st_ref[h, rows, :]
                g = g_sc[h]
                gb = g.astype(BF)
                d_o = do_ref[rows, cols]
                dob = d_o.astype(BF)
                dox = (d_o * xi_ref[h]).astype(BF)
                dec = d_ref[h]
                s = (_nt(q, k) * dec).astype(BF)
                ds = (_nt(dob, vb) * dec).astype(BF)
                dv = _tn(s, dob) + ze_ref[h] * _nn(k, gb)
                dq = _nn(ds, k) + _nt(dox, r.astype(BF))
                dk = _tn(ds, q) + _nt((v32 * ze_ref[h]).astype(BF), gb)
                g_sc[h] = gc_ref[h] * g + _tn(q, dox)
                dq_parts.append((dq * cosv - _swap32(dq) * sinv)[:, :64])
                dk_parts.append(((dk * cosv - _swap32(dk) * sinv) * 0.125)[:, :64])
                dz_ref[rows, 512 + h * RET_DV:512 + (h + 1) * RET_DV] = dv.astype(BF)
            dz_ref[rows, 0:256] = jnp.concatenate(dq_parts, axis=-1).astype(BF)
            dz_ref[rows, 256:512] = jnp.concatenate(dk_parts, axis=-1).astype(BF)

    rev = lambda i: nt - 1 - i
    hspec = pl.BlockSpec((RET_H, tt, LANE), lambda i: (0, rev(i), 0))
    cspec = pl.BlockSpec((RET_H, CHUNK, LANE), lambda i: (0, 0, 0))
    tab = pl.BlockSpec((tt, LANE), lambda i: (rev(i), 0))
    return pl.pallas_call(
        body, name="ret_bwd", grid=(nt,),
        in_specs=[pl.BlockSpec((tt, 512), lambda i: (rev(i), 0)), hspec, hspec,
                  pl.BlockSpec((tt, 512), lambda i: (rev(i), 1)), hspec, tab, tab, cspec, cspec, cspec, cspec],
        out_specs=pl.BlockSpec((tt, 1024), lambda i: (rev(i), 0)),
        out_shape=jax.ShapeDtypeStruct((T, 1024), BF),
        scratch_shapes=[pltpu.VMEM((RET_H, CHUNK, LANE), F32)],
        compiler_params=_params(("arbitrary",), VMEM_BIG),
    )(d_o, qr, kr, z_a, states, cos_t, sin_t, decay, zeta, xi, gcb)


def _fox_bwd(q2, k, v, do, sub=512):
    H, T, _ = k.shape
    tb = 2 * sub
    n = T // sub

    def body(q_ref, do_ref, k_ref, v_ref, dq_ref, dk_ref, dv_ref, dk_sc, dv_sc):
        j = pl.program_id(1)

        @pl.when(j == 0)
        def _():
            dq_ref[...] = jnp.zeros(dq_ref.shape, F32)

        kk, vv = k_ref[...], v_ref[...]
        dk_sc[...] = jnp.zeros(dk_sc.shape, F32)
        dv_sc[...] = jnp.zeros(dv_sc.shape, F32)
        krow = lax.broadcasted_iota(jnp.int32, (tb, sub), 0)
        qcol = lax.broadcasted_iota(jnp.int32, (tb, sub), 1)

        def step(i, shift):
            off = pl.multiple_of(i * sub, sub)
            qq = q_ref[pl.ds(off, sub), :]
            dd = do_ref[pl.ds(off, sub), :]
            p = jnp.exp(_nt(kk, qq))
            if shift is not None:
                p = jnp.where(qcol + shift >= krow, p, 0.0)
            ds = (p * _nt(vv, dd)).astype(BF)
            dv_sc[...] += _nn(p.astype(BF), dd)
            dk_sc[...] += _nn(ds, qq)
            dq_ref[pl.ds(off, sub), :] += _tn(ds, kk)

        off0 = pl.multiple_of(2 * j * sub, sub)
        q0, d0 = q_ref[pl.ds(off0, sub), :], do_ref[pl.ds(off0, sub), :]
        k0, v0 = k_ref[0:sub, :], v_ref[0:sub, :]
        p0 = jnp.where(qcol[0:sub, :] >= krow[0:sub, :], jnp.exp(_nt(k0, q0)), 0.0)
        ds0 = (p0 * _nt(v0, d0)).astype(BF)
        dv_sc[0:sub, :] += _nn(p0.astype(BF), d0)
        dk_sc[0:sub, :] += _nn(ds0, q0)
        dq_ref[pl.ds(off0, sub), :] += _tn(ds0, k0)
        step(2 * j + 1, sub)

        def loop_body(i, carry):
            step(i, None)
            return carry

        lax.fori_loop(2 * j + 2, n, loop_body, 0)
        dk_ref[...] = dk_sc[...]
        dv_ref[...] = dv_sc[...]

    blk = pl.BlockSpec((None, tb, LANE), lambda h, j: (h, j, 0))
    full = pl.BlockSpec((None, T, LANE), lambda h, j: (h, 0, 0))
    shp = jax.ShapeDtypeStruct((H, T, LANE), F32)
    return pl.pallas_call(
        body, name="fox_bwd", grid=(H, T // tb),
        in_specs=[full, full, blk, blk], out_specs=[full, blk, blk], out_shape=[shp, shp, shp],
        scratch_shapes=[pltpu.VMEM((tb, LANE), F32), pltpu.VMEM((tb, LANE), F32)],
        compiler_params=_params(("arbitrary", "arbitrary"), VMEM_BIG),
    )(q2, do, k, v)


def _fox_post_bwd(dq, dk, dv, z_a, z_ff, b_f, g_q, g_k, tm=256):
    T = z_a.shape[0]
    nt = T // tm

    def body(dq_ref, dk_ref, dv_ref, zf_ref, zff_ref, b_ref, gq_ref, gk_ref,
             dz_ref, dff_ref, dgq_ref, dgk_ref, db_ref, carry):
        i = pl.program_id(0)

        @pl.when(i == 0)
        def _():
            carry[...] = jnp.zeros(carry.shape, F32)
            dgq_ref[...] = jnp.zeros(dgq_ref.shape, F32)
            dgk_ref[...] = jnp.zeros(dgk_ref.shape, F32)
            db_ref[...] = jnp.zeros(db_ref.shape, F32)

        lane = lax.broadcasted_iota(jnp.int32, (tm, LANE), 1)
        zf = zf_ref[...]
        dcm = jnp.zeros((tm, LANE), F32)
        dq_parts, dk_parts, dv_parts = [], [], []
        gq_acc = jnp.zeros((1, 64), F32)
        gk_acc = jnp.zeros((1, 64), F32)
        for h in range(FOX_H):
            dqa, dka = dq_ref[h], dk_ref[h]
            dcm = jnp.where(lane == h, dqa[:, L_CQ:L_CQ + 1] - dka[:, L_CK:L_CK + 1], dcm)
            for src, dya, g_ref, scale, parts in ((0, dqa, gq_ref, 0.125, dq_parts), (512, dka, gk_ref, 1.0, dk_parts)):
                xh = zf[:, src + 64 * h:src + 64 * h + 64]
                r = lax.rsqrt(jnp.mean(xh * xh, axis=-1, keepdims=True) + EPS)
                xn = xh * r
                dy = dya[:, :FOX_D] * scale
                if src == 0:
                    gq_acc = gq_acc + jnp.sum(dy * xn, axis=0, keepdims=True)
                else:
                    gk_acc = gk_acc + jnp.sum(dy * xn, axis=0, keepdims=True)
                dxn = dy * g_ref[...]
                parts.append(r * (dxn - xn * jnp.mean(dxn * xn, axis=-1, keepdims=True)))
            dv_parts.append(dv_ref[h][:, :FOX_D])
        dz_ref[...] = jnp.concatenate(dq_parts + dk_parts + dv_parts, axis=-1).astype(BF)
        zpad = jnp.zeros((1, 64), F32)
        dgq_ref[...] += jnp.concatenate([gq_acc, zpad], axis=-1)
        dgk_ref[...] += jnp.concatenate([gk_acc, zpad], axis=-1)

        row = lax.broadcasted_iota(jnp.int32, (tm, tm), 0)
        col = lax.broadcasted_iota(jnp.int32, (tm, tm), 1)
        tri = (row <= col).astype(BF)
        hi, mid, lo = _split3(dcm)
        dlogf = _nn(tri, hi) + _nn(tri, mid) + _nn(tri, lo) + carry[...]
        carry[...] = dlogf[0:1, :]
        dff = jnp.where(lane < FOX_H, dlogf * _sigmoid(-(zff_ref[...] + b_ref[...])), 0.0)
        dff_ref[...] = dff.astype(BF)
        db_ref[...] += jnp.sum(dff, axis=0, keepdims=True)

    rev = lambda i: nt - 1 - i
    hsp = pl.BlockSpec((FOX_H, tm, LANE), lambda i: (0, rev(i), 0))
    small = lambda w: pl.BlockSpec((1, w), lambda i: (0, 0))
    return pl.pallas_call(
        body, name="fox_post_bwd", grid=(nt,),
        in_specs=[hsp, hsp, hsp, pl.BlockSpec((tm, 1536), lambda i: (rev(i), 1)),
                  pl.BlockSpec((tm, LANE), lambda i: (rev(i), 0)), small(LANE), small(64), small(64)],
        out_specs=[pl.BlockSpec((tm, 1536), lambda i: (rev(i), 0)), pl.BlockSpec((tm, LANE), lambda i: (rev(i), 0)),
                   small(LANE), small(LANE), small(LANE)],
        out_shape=[jax.ShapeDtypeStruct((T, 1536), BF), jax.ShapeDtypeStruct((T, LANE), BF),
                   jax.ShapeDtypeStruct((1, LANE), F32), jax.ShapeDtypeStruct((1, LANE), F32),
                   jax.ShapeDtypeStruct((1, LANE), F32)],
        scratch_shapes=[pltpu.VMEM((1, LANE), F32)],
        compiler_params=_params(("arbitrary",), VMEM_BIG),
    )(dq, dk, dv, z_a, z_ff, b_f, g_q, g_k)


def _in_bwd(dz_ret, dz_gt, dz_fox, dz_a, dz_ff, w_a, w_ff, x, g_mix, dx2, tm=256):
    T = x.shape[0]

    def body(r_ref, t_ref, f_ref, a_ref, ff_ref, wa_ref, wf_ref, x_ref, g_ref, dx2_ref, dx_ref, dg_ref):
        i = pl.program_id(0)

        @pl.when(i == 0)
        def _():
            dg_ref[...] = jnp.zeros(dg_ref.shape, F32)

        dh = (_nt(r_ref[...], wa_ref[:, C_RET:C_GT]) + _nt(t_ref[...], wa_ref[:, C_GT:C_FOX])
              + _nt(f_ref[...], wa_ref[:, C_FOX:C_A]) + _nt(a_ref[...], wa_ref[:, C_A:C_END])
              + _nt(ff_ref[...], wf_ref[...]))
        xv = x_ref[...]
        r = lax.rsqrt(jnp.mean(xv * xv, axis=-1, keepdims=True) + EPS)
        xn = xv * r
        dg_ref[...] += jnp.sum(dh * xn, axis=0, keepdims=True)
        dxn = dh * g_ref[...]
        dx_ref[...] = dx2_ref[...] + r * (dxn - xn * jnp.mean(dxn * xn, axis=-1, keepdims=True))

    row = lambda w: pl.BlockSpec((tm, w), lambda i: (i, 0))
    const = lambda shp: pl.BlockSpec(shp, lambda i: (0,) * len(shp))
    return pl.pallas_call(
        body, name="in_bwd", grid=(T // tm,),
        in_specs=[row(1024), row(512), row(1536), row(2048), row(LANE), const((D_MODEL, C_END)),
                  const((D_MODEL, LANE)), row(1024), const((1, 1024)), row(1024)],
        out_specs=[row(1024), const((1, 1024))],
        out_shape=[jax.ShapeDtypeStruct((T, 1024), F32), jax.ShapeDtypeStruct((1, 1024), F32)],
        compiler_params=_params(("arbitrary",), VMEM_BIG),
    )(dz_ret, dz_gt, dz_fox, dz_a, dz_ff, w_a, w_ff, x, g_mix, dx2)


def _mesh_pos():
    return lax.axis_index("x"), lax.axis_index("y"), lax.axis_index("c")


def _staged_place(src, name):
    stacked = src.ndim == 3
    R, C = src.shape[-2:]
    tr = _row_tile(R, 128, 16)
    n = R // tr
    assert n >= 2

    def body(s_ref, o_ref, buf, sem):
        i = pl.program_id(0)
        slot = i % 2
        x, y, _ = _mesh_pos()
        kme = 2 * x + y

        def out_copy(s, step):
            return pltpu.make_async_copy(buf.at[s], o_ref.at[kme, pl.ds(pl.multiple_of(step * tr, tr), tr), :], sem.at[s])

        @pl.when(i >= 2)
        def _():
            out_copy(slot, i - 2).wait()

        buf[slot] = (s_ref[kme] if stacked else s_ref[...]).astype(BF)
        out_copy(slot, i).start()

        @pl.when(i == n - 1)
        def _():
            out_copy(1 - slot, i - 1).wait()
            out_copy(slot, i).wait()

    in_spec = (pl.BlockSpec((N_CHIP, tr, C), lambda i: (0, i, 0)) if stacked else pl.BlockSpec((tr, C), lambda i: (i, 0)))
    return pl.pallas_call(
        body, name=name, grid=(n,), in_specs=[in_spec], out_specs=pl.BlockSpec(memory_space=pl.ANY),
        out_shape=jax.ShapeDtypeStruct((N_CHIP, R, C), BF),
        scratch_shapes=[pltpu.VMEM((2, tr, C), BF), pltpu.SemaphoreType.DMA((2,))],
        compiler_params=_params(("arbitrary",)),
    )(src)


def _push_copies(src, land, send_sem, recv_sem, receiving):
    x, y, c = _mesh_pos()
    kme = 2 * x + y
    cps = []
    for w in range(len(land)):
        for j, (px, py) in enumerate([(1 - x, y), (x, 1 - y), (1 - x, 1 - y)]):
            kpeer = 2 * px + py
            cps.append(pltpu.make_async_remote_copy(
                src_ref=land[w].at[kme] if src is None else src[w].at[kpeer],
                dst_ref=land[w].at[kpeer if receiving else kme],
                send_sem=send_sem.at[3 * w + j], recv_sem=recv_sem.at[3 * w + j],
                device_id=(px, py, c), device_id_type=MESH))
    return cps


def _gather_in_place(stacks):
    n = len(stacks)

    def body(*refs):
        land, send_sem, recv_sem = refs[n:2 * n], refs[2 * n], refs[2 * n + 1]
        for cp in _push_copies(None, land, send_sem, recv_sem, False):
            cp.start()
        for cp in _push_copies(None, land, send_sem, recv_sem, True):
            cp.wait_recv()
            cp.wait_send()

    anyspec = pl.BlockSpec(memory_space=pl.ANY)
    return pl.pallas_call(
        body, name="gather_in_place", in_specs=[anyspec] * n, out_specs=[anyspec] * n,
        out_shape=[jax.ShapeDtypeStruct(s.shape, s.dtype) for s in stacks],
        input_output_aliases={i: i for i in range(n)},
        scratch_shapes=[pltpu.SemaphoreType.DMA((3 * n,)), pltpu.SemaphoreType.DMA((3 * n,))],
    )(*stacks)


def _scatter_partials(srcs, lands, small):
    n = len(srcs)

    def body(*refs):
        src, sv = refs[:n], refs[2 * n]
        land, svo = refs[2 * n + 1:3 * n + 1], refs[3 * n + 1]
        send_sem, recv_sem, ssend, srecv, sloc = refs[3 * n + 2:]
        x, y, c = _mesh_pos()
        me = 4 * x + 2 * y + c
        flips = [(b >> 2 & 1, b >> 1 & 1, b & 1) for b in range(1, 8)]
        others = [(1 - x if fx else x, 1 - y if fy else y, 1 - c if fc else c) for fx, fy, fc in flips]
        local = pltpu.make_async_copy(sv, svo.at[me], sloc)
        local.start()
        sends = []
        for j, (px, py, pc) in enumerate(others):
            cp = pltpu.make_async_remote_copy(
                src_ref=sv, dst_ref=svo.at[me], send_sem=ssend.at[j], recv_sem=srecv.at[j],
                device_id=(px, py, pc), device_id_type=MESH)
            cp.start()
            sends.append(cp)
        for cp in _push_copies(src, land, send_sem, recv_sem, False):
            cp.start()
            sends.append(cp)
        for j, (px, py, pc) in enumerate(others):
            pltpu.make_async_remote_copy(
                src_ref=sv, dst_ref=svo.at[4 * px + 2 * py + pc], send_sem=ssend.at[j], recv_sem=srecv.at[j],
                device_id=(px, py, pc), device_id_type=MESH).wait_recv()
        for cp in _push_copies(src, land, send_sem, recv_sem, True):
            cp.wait_recv()
        for cp in sends:
            cp.wait_send()
        local.wait()

    anyspec = pl.BlockSpec(memory_space=pl.ANY)
    return pl.pallas_call(
        body, name="scatter_partials",
        in_specs=[anyspec] * (2 * n + 1), out_specs=[anyspec] * (n + 1),
        out_shape=[jax.ShapeDtypeStruct(s.shape, s.dtype) for s in lands]
        + [jax.ShapeDtypeStruct((8,) + small.shape, small.dtype)],
        input_output_aliases={n + i: i for i in range(n)},
        scratch_shapes=[pltpu.SemaphoreType.DMA((3 * n,)), pltpu.SemaphoreType.DMA((3 * n,)),
                        pltpu.SemaphoreType.DMA((7,)), pltpu.SemaphoreType.DMA((7,)), pltpu.SemaphoreType.DMA],
    )(*srcs, *lands, small)


_HBM_SPEC = pl.BlockSpec(memory_space=pltpu.HBM)
_SEM_SPEC = pl.BlockSpec(memory_space=pltpu.SEMAPHORE)
_SPLIT_PARAMS = pltpu.CompilerParams(has_side_effects=pltpu.SideEffectType.DATAFLOW_SIDE_EFFECTING)


def _push_start(srcs, lands, after, name):
    n = len(lands)
    ns = 0 if srcs is None else n

    def body(*refs):
        src = None if srcs is None else refs[:n]
        land = refs[ns:ns + n]
        send_sem, recv_sem = refs[ns + n + 1], refs[ns + n + 2]
        for cp in _push_copies(src, land, send_sem, recv_sem, False):
            cp.start()
        refs[-1][...] = jnp.zeros(refs[-1].shape, F32)

    ops = [pltpu.with_memory_space_constraint(a, pltpu.HBM) for a in ([] if srcs is None else list(srcs)) + list(lands)]
    res = pl.pallas_call(
        body, name=name,
        out_shape=(pltpu.SemaphoreType.DMA((3 * n,)), pltpu.SemaphoreType.DMA((3 * n,)),
                   *[pltpu.HBM(a.shape, a.dtype) for a in ops], jax.ShapeDtypeStruct((8, LANE), F32)),
        in_specs=[_HBM_SPEC] * len(ops) + [pl.BlockSpec(memory_space=pl.ANY)],
        out_specs=(_SEM_SPEC, _SEM_SPEC, *([_HBM_SPEC] * len(ops)), pl.BlockSpec(memory_space=pltpu.VMEM)),
        input_output_aliases={i: 2 + i for i in range(len(ops))},
        compiler_params=_SPLIT_PARAMS,
    )(*ops, after)
    return res[0], res[1], list(res[2:2 + len(ops)]), res[-1]


def _push_wait(send_sem, recv_sem, bufs, after, name, has_src):
    n = len(bufs) // 2 if has_src else len(bufs)
    ns = n if has_src else 0

    def body(*refs):
        src = refs[:n] if has_src else None
        land = refs[ns:ns + n]
        for cp in _push_copies(src, land, refs[ns + n], refs[ns + n + 1], True):
            cp.wait_send()
            cp.wait_recv()

    res = pl.pallas_call(
        body, name=name,
        out_shape=tuple(pltpu.HBM(a.shape, a.dtype) for a in bufs),
        in_specs=[_HBM_SPEC] * len(bufs) + [_SEM_SPEC, _SEM_SPEC, pl.BlockSpec(memory_space=pl.ANY)],
        out_specs=tuple([_HBM_SPEC] * len(bufs)),
        input_output_aliases={i: i for i in range(len(bufs))},
        compiler_params=_SPLIT_PARAMS,
    )(*bufs, send_sem, recv_sem, after)
    return list(res[ns:ns + n])


def _sibling_exchange(arrs):
    n = len(arrs)

    def body(*refs):
        ins, outs = refs[:n], refs[n:2 * n]
        send_sems, recv_sems = refs[2 * n:]
        x, y, c = _mesh_pos()
        cps = [pltpu.make_async_remote_copy(
            src_ref=ins[w], dst_ref=outs[w], send_sem=send_sems.at[w], recv_sem=recv_sems.at[w],
            device_id=(x, y, 1 - c), device_id_type=MESH) for w in range(n)]
        for cp in cps:
            cp.start()
        for cp in cps:
            cp.wait_recv()
        for cp in cps:
            cp.wait_send()

    anyspec = pl.BlockSpec(memory_space=pl.ANY)
    return pl.pallas_call(
        body, name="sibling_exchange",
        in_specs=[anyspec] * n, out_specs=[anyspec] * n,
        out_shape=[jax.ShapeDtypeStruct(a.shape, a.dtype) for a in arrs],
        scratch_shapes=[pltpu.SemaphoreType.DMA((n,)), pltpu.SemaphoreType.DMA((n,))],
    )(*arrs)


def _sum_stack(stack, name):
    _, R, C = stack.shape
    tr = _row_tile(R, 256, 16)

    def body(s_ref, o_ref):
        acc = s_ref[0].astype(F32)
        for k in range(1, N_CHIP):
            acc = acc + s_ref[k].astype(F32)
        o_ref[...] = acc

    return pl.pallas_call(
        body, name=name, grid=(R // tr,),
        in_specs=[pl.BlockSpec((N_CHIP, tr, C), lambda i: (0, i, 0))],
        out_specs=pl.BlockSpec((tr, C), lambda i: (i, 0)),
        out_shape=jax.ShapeDtypeStruct((R, C), F32),
        compiler_params=_params(("parallel",)),
    )(stack)


def _adam_math(w, g, m, v):
    m2 = ADAM_B1 * m + (1.0 - ADAM_B1) * g
    v2 = ADAM_B2 * v + (1.0 - ADAM_B2) * (g * g)
    m_hat = m2 / (1.0 - ADAM_B1 ** ADAM_STEP)
    v_hat = v2 / (1.0 - ADAM_B2 ** ADAM_STEP)
    delta = -ADAM_LR * (m_hat / (jnp.sqrt(v_hat) + ADAM_EPS) + ADAM_WD * w)
    return delta, m2, v2


def _adamw(w, m, v, s0, s1, name):
    R, C = w.shape
    tr = _row_tile(R, 128, 8)

    def body(w_ref, m_ref, v_ref, a_ref, b_ref, g_ref, d_ref, m2_ref, v2_ref):
        g = a_ref[...] + b_ref[...]
        delta, m2, v2 = _adam_math(w_ref[...], g, m_ref[...], v_ref[...])
        g_ref[...] = g
        d_ref[...] = delta
        m2_ref[...] = m2
        v2_ref[...] = v2

    spec = pl.BlockSpec((tr, C), lambda i: (i, 0))
    shp = jax.ShapeDtypeStruct((R, C), F32)
    return pl.pallas_call(
        body, name=name, grid=(R // tr,), in_specs=[spec] * 5, out_specs=[spec] * 4, out_shape=[shp] * 4,
        compiler_params=_params(("parallel",), VMEM_BIG),
    )(w, m, v, s0, s1)


def _adamw_small(w, m, v, gathered):
    def body(w_ref, m_ref, v_ref, s_ref, g_ref, d_ref, m2_ref, v2_ref):
        g = s_ref[0]
        for d in range(1, 8):
            g = g + s_ref[d]
        delta, m2, v2 = _adam_math(w_ref[...], g, m_ref[...], v_ref[...])
        g_ref[...] = g
        d_ref[...] = delta
        m2_ref[...] = m2
        v2_ref[...] = v2

    shp = jax.ShapeDtypeStruct(w.shape, F32)
    return pl.pallas_call(body, name="adamw_small", out_shape=[shp] * 4)(w, m, v, gathered)


SMALL = (("g_mix", 1024), ("g_ffn", 1024), ("g_ret_norm", 512), ("g_fox_q", 64), ("g_fox_k", 64), ("b_forget", 8))
SMALL_W = 3072


def _pack_small(parts):
    cols = []
    for (name, n) in SMALL:
        p = parts[name].reshape(1, -1)[:, :n]
        pad = -n % LANE
        cols.append(jnp.pad(p, ((0, 0), (0, pad))) if pad else p)
    used = sum(c.shape[1] for c in cols)
    cols.append(jnp.zeros((1, SMALL_W - used), F32))
    return jnp.concatenate(cols, axis=1)


def _unpack_small(vec):
    out, off = {}, 0
    for (name, n) in SMALL:
        out[name] = vec[:, off:off + n]
        off += n + (-n % LANE)
    return out


def kernel(x, g_mix, w_in, b_forget, g_ret_norm, w_ret_o, g_fox_q, g_fox_k, w_fox_o, w_out, g_ffn, w_gate, w_up, w_down, loss_target, m_g_mix, m_w_in, m_b_forget, m_g_ret_norm, m_w_ret_o, m_g_fox_q, m_g_fox_k, m_w_fox_o, m_w_out, m_g_ffn, m_w_gate, m_w_up, m_w_down, v_g_mix, v_w_in, v_b_forget, v_g_ret_norm, v_w_ret_o, v_g_fox_q, v_g_fox_k, v_w_fox_o, v_w_out, v_g_ffn, v_w_gate, v_w_up, v_w_down):
    T = x.shape[1]
    xs = x[0]
    tgt = loss_target[0]
    big_names = ("w_in", "w_ret_o", "w_fox_o", "w_out", "w_gate", "w_up", "w_down")
    tr = lambda a: jnp.swapaxes(a[0], 0, 1)
    big_w = dict(w_in=w_in[0], w_ret_o=w_ret_o[0], w_fox_o=w_fox_o[0], w_out=w_out[0], w_gate=tr(w_gate),
                 w_up=tr(w_up), w_down=w_down[0])
    big_m = dict(w_in=m_w_in[0], w_ret_o=m_w_ret_o[0], w_fox_o=m_w_fox_o[0], w_out=m_w_out[0], w_gate=tr(m_w_gate),
                 w_up=tr(m_w_up), w_down=m_w_down[0])
    big_v = dict(w_in=v_w_in[0], w_ret_o=v_w_ret_o[0], w_fox_o=v_w_fox_o[0], w_out=v_w_out[0], w_gate=tr(v_w_gate),
                 w_up=tr(v_w_up), w_down=v_w_down[0])
    small_w = dict(g_mix=g_mix, g_ffn=g_ffn, g_ret_norm=g_ret_norm, g_fox_q=g_fox_q, g_fox_k=g_fox_k, b_forget=b_forget)
    small_m = dict(g_mix=m_g_mix, g_ffn=m_g_ffn, g_ret_norm=m_g_ret_norm, g_fox_q=m_g_fox_q, g_fox_k=m_g_fox_k,
                   b_forget=m_b_forget)
    small_v = dict(g_mix=v_g_mix, g_ffn=v_g_ffn, g_ret_norm=v_g_ret_norm, g_fox_q=v_g_fox_q, g_fox_k=v_g_fox_k,
                   b_forget=v_b_forget)

    stacks = [_staged_place(big_w[n], "place_" + n) for n in big_names]
    (s_in,) = _gather_in_place(stacks[:1])
    w_send, w_recv, w_bufs, w_tok = _push_start(None, stacks[1:], s_in, "gather_rest_start")
    w_a, w_ff = _assemble_w_in(s_in)
    b_pad = jnp.pad(b_forget, ((0, 0), (0, LANE - FOX_H)))
    cos_t, sin_t = _rope_tables(T)
    consts = _ret_consts()

    h = _rms_cast(xs, g_mix + w_tok[0:1, 0:1])
    z_a = _mm_nn(h, w_a, "proj_in")
    z_ff = _mm_nn(h, w_ff, "proj_ff")
    qr, kr, qf, kf, vf = _mix_prep(z_a, z_ff, cos_t, sin_t, b_pad, g_fox_q, g_fox_k)
    o_raw, u_r, states = _ret_fwd(qr, kr, z_a, g_ret_norm, consts)
    o_fox, q2 = _fox_fwd(qf, kf, vf)
    s_ro, s_fo, s_out, s_gate, s_up, s_down = _push_wait(w_send, w_recv, w_bufs, q2, "gather_rest_wait", False)
    y_r, y_f, mrg, x2, h2, o_cat = _merge_out(u_r, o_fox, z_a, xs, g_ffn, s_ro, s_fo, s_out)
    gp, up, act, dy, loss_vec = _ffn_fwd(h2, x2, tgt, s_gate, s_up, s_down)
    loss = lax.psum(0.5 / D_MODEL * jnp.sum(loss_vec), ("x", "y", "c"))

    dgp, dup, dx2, dg_ffn = _ffn_bwd(dy, gp, up, x2, g_ffn, s_gate, s_up, s_down)
    ffn_part = [_grad_astack(dgp, h2, "gw_gate"), _grad_astack(dup, h2, "gw_up"), _grad_astack(act, dy, "gw_down")]
    f_send, f_recv, f_bufs, f_tok = _push_start(
        ffn_part, [_staged_place(g, "place_g_" + n) for g, n in zip(ffn_part, big_names[4:])], dx2, "scatter_ffn_start")
    d_yr, d_yf, dz_gt, dz_a, d_o, do_fox, dg_ret = _out_bwd(dx2, z_a, y_r, y_f, o_raw, o_fox,
                                                            g_ret_norm + f_tok[0:1, 0:1], s_ro, s_fo, s_out)
    dz_ret = _ret_bwd(d_o, qr, kr, z_a, states, cos_t, sin_t, consts)
    dq_f, dk_f, dv_f = _fox_bwd(q2, kf, vf, do_fox)
    dz_fox, dz_ff, dg_q, dg_k, db_f = _fox_post_bwd(dq_f, dk_f, dv_f, z_a, z_ff, b_pad, g_fox_q, g_fox_k)
    grad_x, dg_mix = _in_bwd(dz_ret, dz_gt, dz_fox, dz_a, dz_ff, w_a, w_ff, xs, g_mix, dx2)

    g_in = _pack_g_in(_grad_plain(h, dz_ret, "gw_in_ret", F32), _grad_plain(h, dz_gt, "gw_in_gt", F32),
                      _grad_plain(h, dz_fox, "gw_in_fox", F32, tn=768), _grad_plain(h, dz_a, "gw_in_a", F32),
                      _grad_plain(h, dz_ff, "gw_in_ff", F32))
    late_part = [g_in, _grad_colstack(u_r, d_yr, "gw_ret_o", 256), _grad_colstack(o_cat, d_yf, "gw_fox_o", 256),
                 _grad_plain(mrg, dx2, "gw_out", BF).reshape(N_CHIP, 256, D_MODEL)]
    small_g = _pack_small(dict(g_mix=dg_mix, g_ffn=dg_ffn, g_ret_norm=dg_ret, g_fox_q=dg_q, g_fox_k=dg_k, b_forget=db_f))

    late_land = [_staged_place(g, "place_g_" + n) for g, n in zip(late_part, big_names[:4])]
    recv = _scatter_partials(late_part, late_land, small_g)
    recv_ffn = _push_wait(f_send, f_recv, f_bufs, recv[0], "scatter_ffn_wait", True)
    sums = [_sum_stack(r, "sum_" + n) for r, n in zip(list(recv[:4]) + recv_ffn, big_names)]
    sib = _sibling_exchange(sums)
    big_out = {n: _adamw(big_w[n], big_m[n], big_v[n], sums[i], sib[i], "adamw_" + n) for i, n in enumerate(big_names)}
    sg, sd, sm, sv = _adamw_small(_pack_small(small_w), _pack_small(small_m), _pack_small(small_v), recv[-1])
    small_out = [_unpack_small(t) for t in (sg, sd, sm, sv)]

    order = ("g_mix", "w_in", "b_forget", "g_ret_norm", "w_ret_o", "g_fox_q", "g_fox_k", "w_fox_o", "w_out", "g_ffn",
             "w_gate", "w_up", "w_down")
    outs = [loss, grad_x[None]]
    for idx in range(4):
        for n in order:
            if n in ("w_gate", "w_up"):
                outs.append(jnp.swapaxes(big_out[n][idx], 0, 1)[None])
            else:
                outs.append(big_out[n][idx][None] if n in big_out else small_out[idx][n])
    return tuple(outs)
```

```python
import functools
import math

import numpy as np
import jax
import jax.numpy as jnp
from jax import lax
from jax.experimental import pallas as pl
from jax.experimental.pallas import tpu as pltpu

F32 = jnp.float32
BF = jnp.bfloat16
MESH = pl.DeviceIdType.MESH

D_MODEL = 1024
D_FF = 2816
N_CHIP = 4
FF_SH = D_FF // N_CHIP
IN_COLS = 5128
IN_SH = IN_COLS // N_CHIP
RET_H, RET_DV = 4, 128
FOX_H, FOX_D = 8, 64
CHUNK = 128
EPS = 1e-6
NEG = -1e30
LANE = 128
C_RET, C_GT, C_FOX, C_A, C_END = 0, 1024, 1536, 3072, 5120
L_CQ, L_CK, L_LSE, L_MAX = 64, 67, 70, 73

ADAM_LR, ADAM_B1, ADAM_B2, ADAM_EPS, ADAM_WD, ADAM_STEP = 0.001, 0.9, 0.999, 1e-08, 0.01, 10
VMEM_BIG = 56 * 1024 * 1024
VMEM_HUGE = 60 * 1024 * 1024
GRAD_TK = 2048
FFN_TM = 512


def _nn(a, b):
    return lax.dot_general(a, b, (((1,), (0,)), ((), ())), preferred_element_type=F32)


def _nt(a, b):
    return lax.dot_general(a, b, (((1,), (1,)), ((), ())), preferred_element_type=F32)


def _tn(a, b):
    return lax.dot_general(a, b, (((0,), (0,)), ((), ())), preferred_element_type=F32)


def _split3(x):
    hi = x.astype(BF)
    r = x - hi.astype(F32)
    mid = r.astype(BF)
    lo = (r - mid.astype(F32)).astype(BF)
    return hi, mid, lo


def _sigmoid(x):
    return 0.5 * jnp.tanh(0.5 * x) + 0.5


def _swap32(x):
    lane = lax.broadcasted_iota(jnp.int32, x.shape, 1)
    return jnp.where(lane < 32, pltpu.roll(x, 96, 1), pltpu.roll(x, 32, 1))


def _params(sem, vmem=None):
    return pltpu.CompilerParams(dimension_semantics=sem, vmem_limit_bytes=vmem)


def _row_tile(rows, cap, mult):
    return max(d for d in range(mult, cap + 1, mult) if rows % d == 0)


def _assemble_w_in(stack, tr=256):
    def body(s_ref, a_ref, f_ref):
        full = jnp.concatenate([s_ref[k].astype(F32) for k in range(N_CHIP)], axis=-1)
        a_ref[...] = jnp.concatenate([full[:, :3072], full[:, 3080:IN_COLS]], axis=-1).astype(BF)
        f_ref[...] = jnp.concatenate([full[:, 3072:3080], jnp.zeros((tr, LANE - FOX_H), F32)], axis=-1).astype(BF)

    return pl.pallas_call(
        body, name="assemble_w_in", grid=(D_MODEL // tr,),
        in_specs=[pl.BlockSpec((N_CHIP, tr, IN_SH), lambda i: (0, i, 0))],
        out_specs=[pl.BlockSpec((tr, C_END), lambda i: (i, 0)), pl.BlockSpec((tr, LANE), lambda i: (i, 0))],
        out_shape=[jax.ShapeDtypeStruct((D_MODEL, C_END), BF), jax.ShapeDtypeStruct((D_MODEL, LANE), BF)],
        compiler_params=_params(("parallel",), VMEM_BIG),
    )(stack)


def _pack_g_in(g_ret, g_gt, g_fox, g_a, g_ff, tr=256):
    def body(r_ref, t_ref, x_ref, a_ref, f_ref, o_ref):
        full = jnp.concatenate([r_ref[...], t_ref[...], x_ref[...], f_ref[...][:, :FOX_H], a_ref[...]], axis=-1)
        for k in range(N_CHIP):
            o_ref[k] = full[:, k * IN_SH:(k + 1) * IN_SH].astype(BF)

    def spec(w):
        return pl.BlockSpec((tr, w), lambda i: (i, 0))

    return pl.pallas_call(
        body, name="pack_g_in", grid=(D_MODEL // tr,),
        in_specs=[spec(1024), spec(512), spec(1536), spec(2048), spec(LANE)],
        out_specs=pl.BlockSpec((N_CHIP, tr, IN_SH), lambda i: (0, i, 0)),
        out_shape=jax.ShapeDtypeStruct((N_CHIP, D_MODEL, IN_SH), BF),
        compiler_params=_params(("parallel",), VMEM_BIG),
    )(g_ret, g_gt, g_fox, g_a, g_ff)


def _rms_cast(x, g, tm=512):
    T = x.shape[0]

    def body(x_ref, g_ref, o_ref):
        xv = x_ref[...]
        r = lax.rsqrt(jnp.mean(xv * xv, axis=-1, keepdims=True) + EPS)
        o_ref[...] = (xv * r * g_ref[...]).astype(BF)

    return pl.pallas_call(
        body, name="rms_cast", grid=(T // tm,),
        in_specs=[pl.BlockSpec((tm, D_MODEL), lambda i: (i, 0)), pl.BlockSpec((1, D_MODEL), lambda i: (0, 0))],
        out_specs=pl.BlockSpec((tm, D_MODEL), lambda i: (i, 0)),
        out_shape=jax.ShapeDtypeStruct((T, D_MODEL), BF),
        compiler_params=_params(("parallel",)),
    )(x, g)


def _mm_nn(a, b, name, tm=512, tn=1024):
    M, K = a.shape
    N = b.shape[1]
    tn = min(tn, N)

    def body(a_ref, b_ref, o_ref):
        o_ref[...] = _nn(a_ref[...], b_ref[...])

    return pl.pallas_call(
        body, name=name, grid=(N // tn, M // tm),
        in_specs=[pl.BlockSpec((tm, K), lambda j, i: (i, 0)), pl.BlockSpec((K, tn), lambda j, i: (0, j))],
        out_specs=pl.BlockSpec((tm, tn), lambda j, i: (i, j)),
        out_shape=jax.ShapeDtypeStruct((M, N), F32),
        compiler_params=_params(("parallel", "parallel")),
    )(a, b)


def _mm_tn(a, b, name, grid, a_spec, b_spec, o_spec, out_shape, acc_shape):
    nk = grid[-1]

    def body(a_ref, b_ref, o_ref, acc):
        k = pl.program_id(len(grid) - 1)

        @pl.when(k == 0)
        def _():
            acc[...] = jnp.zeros(acc.shape, F32)

        acc[...] += _tn(a_ref[...].astype(BF), b_ref[...].astype(BF))

        @pl.when(k == nk - 1)
        def _():
            o_ref[...] = acc[...].astype(o_ref.dtype)

    return pl.pallas_call(
        body, name=name, grid=grid, in_specs=[a_spec, b_spec], out_specs=o_spec, out_shape=out_shape,
        scratch_shapes=[pltpu.VMEM(acc_shape, F32)],
        compiler_params=_params(("parallel",) * (len(grid) - 1) + ("arbitrary",), VMEM_BIG),
    )(a, b)


def _grad_plain(a, b, name, out_dtype, tk=GRAD_TK, tn=1024):
    T, M = a.shape
    N = b.shape[1]
    tn = min(tn, N)
    return _mm_tn(a, b, name, (N // tn, T // tk),
                  pl.BlockSpec((tk, M), lambda j, k: (k, 0)), pl.BlockSpec((tk, tn), lambda j, k: (k, j)),
                  pl.BlockSpec((M, tn), lambda j, k: (0, j)), jax.ShapeDtypeStruct((M, N), out_dtype), (M, tn))


def _grad_colstack(a, b, name, wcol, tk=GRAD_TK):
    T, M = a.shape
    S = b.shape[1] // wcol
    return _mm_tn(a, b, name, (S, T // tk),
                  pl.BlockSpec((tk, M), lambda s, k: (k, 0)), pl.BlockSpec((tk, wcol), lambda s, k: (k, s)),
                  pl.BlockSpec((None, M, wcol), lambda s, k: (s, 0, 0)),
                  jax.ShapeDtypeStruct((S, M, wcol), BF), (M, wcol))


def _grad_bstack(a, b, name, tk=GRAD_TK):
    T, M = a.shape
    S, _, n = b.shape
    return _mm_tn(a, b, name, (S, T // tk),
                  pl.BlockSpec((tk, M), lambda s, k: (k, 0)), pl.BlockSpec((None, tk, n), lambda s, k: (s, k, 0)),
                  pl.BlockSpec((None, M, n), lambda s, k: (s, 0, 0)),
                  jax.ShapeDtypeStruct((S, M, n), BF), (M, n))


def _grad_astack(a, b, name, tk=GRAD_TK):
    S, T, m = a.shape
    N = b.shape[1]
    return _mm_tn(a, b, name, (S, T // tk),
                  pl.BlockSpec((None, tk, m), lambda s, k: (s, k, 0)), pl.BlockSpec((tk, N), lambda s, k: (k, 0)),
                  pl.BlockSpec((None, m, N), lambda s, k: (s, 0, 0)),
                  jax.ShapeDtypeStruct((S, m, N), BF), (m, N))


def _rope_tables(T):
    half = 32
    pos = jnp.arange(T, dtype=F32)
    inv_freq = 1.0 / (10000.0 ** (jnp.arange(half, dtype=F32) / half))
    ang = pos[:, None] * inv_freq[None, :]
    cos, sin = jnp.cos(ang), jnp.sin(ang)
    z = jnp.zeros((T, 64), F32)
    return jnp.concatenate([cos, cos, z], axis=-1), jnp.concatenate([-sin, sin, z], axis=-1)


def _ret_consts():
    h = np.arange(RET_H, dtype=np.float32)
    log_g = np.log1p(-(np.float32(2.0) ** (-5.0 - h))).astype(np.float32)
    idx = np.arange(CHUNK, dtype=np.float32)
    diff = idx[:, None] - idx[None, :]
    decay = np.where(diff[None] >= 0, np.exp(np.maximum(diff, 0.0)[None] * log_g[:, None, None]), 0.0)
    zeta = np.exp((CHUNK - 1.0 - idx)[None, :] * log_g[:, None])
    xi = np.exp((idx + 1.0)[None, :] * log_g[:, None])
    gc = np.exp(CHUNK * log_g)
    bc = lambda v: np.broadcast_to(v[:, :, None], (RET_H, CHUNK, LANE)).astype(np.float32)
    gcb = np.broadcast_to(gc[:, None, None], (RET_H, CHUNK, LANE)).astype(np.float32)
    return (jnp.asarray(decay.astype(np.float32)), jnp.asarray(bc(zeta)), jnp.asarray(bc(xi)), jnp.asarray(gcb))


def _mix_prep(z_a, z_ff, cos_t, sin_t, b_f, g_q, g_k, tm=256):
    T = z_a.shape[0]

    def body(zqk_ref, zf_ref, zff_ref, cos_ref, sin_ref, b_ref, gq_ref, gk_ref,
             qr_ref, kr_ref, qf_ref, kf_ref, vf_ref, carry):
        i = pl.program_id(0)

        @pl.when(i == 0)
        def _():
            carry[...] = jnp.zeros(carry.shape, F32)

        lane = lax.broadcasted_iota(jnp.int32, (tm, LANE), 1)
        zpad = jnp.zeros((tm, 64), F32)
        cosv, sinv = cos_ref[...], sin_ref[...]
        zqk = zqk_ref[...]
        for h in range(RET_H):
            for src, dst, scale in ((0, qr_ref, 1.0), (256, kr_ref, 0.125)):
                xh = jnp.concatenate([zqk[:, src + 64 * h: src + 64 * h + 64], zpad], axis=-1)
                rot = xh * cosv + _swap32(xh) * sinv
                dst[h] = (rot * scale).astype(BF)

        lf_in = zff_ref[...] + b_ref[...]
        logf = jnp.minimum(lf_in, 0.0) - jnp.log(1.0 + jnp.exp(-jnp.abs(lf_in)))
        row = lax.broadcasted_iota(jnp.int32, (tm, tm), 0)
        col = lax.broadcasted_iota(jnp.int32, (tm, tm), 1)
        tri = (row >= col).astype(BF)
        hi, mid, lo = _split3(logf)
        cs = _nn(tri, hi) + _nn(tri, mid) + _nn(tri, lo) + carry[...]
        carry[...] = cs[tm - 1:tm, :]

        zf = zf_ref[...]
        one = jnp.ones((tm, LANE), F32)
        for h in range(FOX_H):
            c = cs[:, h:h + 1]
            chi, cmid, clo = [t.astype(F32) for t in _split3(c)]
            qh = zf[:, 64 * h:64 * h + 64]
            kh = zf[:, 512 + 64 * h:512 + 64 * h + 64]
            vh = zf[:, 1024 + 64 * h:1024 + 64 * h + 64]
            qn = qh * lax.rsqrt(jnp.mean(qh * qh, axis=-1, keepdims=True) + EPS) * gq_ref[...] * 0.125
            kn = kh * lax.rsqrt(jnp.mean(kh * kh, axis=-1, keepdims=True) + EPS) * gk_ref[...]
            qa = jnp.concatenate([qn, zpad], axis=-1)
            qa = jnp.where(lane == L_CQ, chi, jnp.where(lane == L_CQ + 1, cmid, jnp.where(lane == L_CQ + 2, clo, qa)))
            qa = jnp.where((lane >= L_CK) & (lane < L_CK + 3), one, qa)
            ka = jnp.concatenate([kn, zpad], axis=-1)
            ka = jnp.where(lane == L_CK, -chi, jnp.where(lane == L_CK + 1, -cmid, jnp.where(lane == L_CK + 2, -clo, ka)))
            ka = jnp.where(((lane >= L_CQ) & (lane < L_CQ + 3)) | ((lane >= L_LSE) & (lane < L_MAX + 3)), one, ka)
            va = jnp.concatenate([vh, zpad], axis=-1)
            va = jnp.where((lane >= 64) & (lane < 67), one, va)
            qf_ref[h] = qa.astype(BF)
            kf_ref[h] = ka.astype(BF)
            vf_ref[h] = va.astype(BF)

    hspec4 = pl.BlockSpec((RET_H, tm, LANE), lambda i: (0, i, 0))
    hspec8 = pl.BlockSpec((FOX_H, tm, LANE), lambda i: (0, i, 0))
    small = lambda w: pl.BlockSpec((1, w), lambda i: (0, 0))
    return pl.pallas_call(
        body, name="mix_prep", grid=(T // tm,),
        in_specs=[pl.BlockSpec((tm, 512), lambda i: (i, 0)), pl.BlockSpec((tm, 1536), lambda i: (i, 1)),
                  pl.BlockSpec((tm, LANE), lambda i: (i, 0)), pl.BlockSpec((tm, LANE), lambda i: (i, 0)),
                  pl.BlockSpec((tm, LANE), lambda i: (i, 0)), small(LANE), small(64), small(64)],
        out_specs=[hspec4, hspec4, hspec8, hspec8, hspec8],
        out_shape=[jax.ShapeDtypeStruct((RET_H, T, LANE), BF)] * 2 + [jax.ShapeDtypeStruct((FOX_H, T, LANE), BF)] * 3,
        scratch_shapes=[pltpu.VMEM((1, LANE), F32)],
        compiler_params=_params(("arbitrary",), VMEM_BIG),
    )(z_a, z_a, z_ff, cos_t, sin_t, b_f, g_q, g_k)


def _ret_fwd(qr, kr, z_a, g_ret, consts, tt=512):
    T = z_a.shape[0]
    nch = tt // CHUNK
    decay, zeta, xi, gcb = consts

    def body(q_ref, k_ref, v_ref, gt_ref, g_ref, d_ref, ze_ref, xi_ref, gc_ref, o_ref, u_ref, st_ref, r_sc):
        i = pl.program_id(0)

        @pl.when(i == 0)
        def _():
            r_sc[...] = jnp.zeros(r_sc.shape, F32)

        for c in range(nch):
            rows = slice(c * CHUNK, (c + 1) * CHUNK)
            for h in range(RET_H):
                cols = slice(h * RET_DV, (h + 1) * RET_DV)
                q, k = q_ref[h, rows, :], k_ref[h, rows, :]
                v32 = v_ref[rows, cols]
                r = r_sc[h]
                st_ref[h, rows, :] = r
                s = _nt(q, k) * d_ref[h]
                o = _nn(s.astype(BF), v32.astype(BF)) + _nn(q, r.astype(BF)) * xi_ref[h]
                r_sc[h] = gc_ref[h] * r + _tn(k, (v32 * ze_ref[h]).astype(BF))
                o_ref[rows, cols] = o
                mu = jnp.mean(o, axis=-1, keepdims=True)
                xc = o - mu
                on = xc * lax.rsqrt(jnp.mean(xc * xc, axis=-1, keepdims=True) + EPS)
                gt = gt_ref[rows, cols]
                u_ref[rows, cols] = (gt * _sigmoid(gt) * (on * g_ref[:, cols])).astype(BF)

    hspec = pl.BlockSpec((RET_H, tt, LANE), lambda i: (0, i, 0))
    cspec = pl.BlockSpec((RET_H, CHUNK, LANE), lambda i: (0, 0, 0))
    return pl.pallas_call(
        body, name="ret_fwd", grid=(T // tt,),
        in_specs=[hspec, hspec, pl.BlockSpec((tt, 512), lambda i: (i, 1)), pl.BlockSpec((tt, 512), lambda i: (i, 2)),
                  pl.BlockSpec((1, 512), lambda i: (0, 0)), cspec, cspec, cspec, cspec],
        out_specs=[pl.BlockSpec((tt, 512), lambda i: (i, 0)), pl.BlockSpec((tt, 512), lambda i: (i, 0)), hspec],
        out_shape=[jax.ShapeDtypeStruct((T, 512), F32), jax.ShapeDtypeStruct((T, 512), BF),
                   jax.ShapeDtypeStruct((RET_H, T, LANE), F32)],
        scratch_shapes=[pltpu.VMEM((RET_H, CHUNK, LANE), F32)],
        compiler_params=_params(("arbitrary",), VMEM_BIG),
    )(qr, kr, z_a, z_a, g_ret, decay, zeta, xi, gcb)


def _fox_fwd(q, k, v, sub=512):
    H, T, _ = q.shape
    tb = 2 * sub

    def body(q_ref, k_ref, v_ref, o_ref, q2_ref, mx_sc, acc_sc):
        i = pl.program_id(1)
        lane = lax.broadcasted_iota(jnp.int32, (sub, LANE), 1)
        row = lax.broadcasted_iota(jnp.int32, (sub, sub), 0)
        col = lax.broadcasted_iota(jnp.int32, (sub, sub), 1)
        causal = row >= col
        qs = [q_ref[0:sub, :], q_ref[sub:tb, :]]
        d0 = pl.multiple_of(i * tb, tb)
        d1 = pl.multiple_of(i * tb + sub, sub)

        def lane_max(s):
            m = s[:, 0:LANE]
            for c in range(1, s.shape[1] // LANE):
                m = jnp.maximum(m, s[:, c * LANE:(c + 1) * LANE])
            return m

        mx_sc[...] = jnp.full(mx_sc.shape, NEG, F32)

        def max_body(j, carry):
            kb = k_ref[pl.ds(pl.multiple_of(j * tb, tb), tb), :]
            for a in range(2):
                mx_sc[a] = jnp.maximum(mx_sc[a], lane_max(_nt(qs[a], kb)))
            return carry

        lax.fori_loop(0, i, max_body, 0)
        k0, k1 = k_ref[pl.ds(d0, sub), :], k_ref[pl.ds(d1, sub), :]
        v0, v1 = v_ref[pl.ds(d0, sub), :], v_ref[pl.ds(d1, sub), :]
        mx = [jnp.maximum(mx_sc[0], lane_max(jnp.where(causal, _nt(qs[0], k0), NEG))),
              jnp.maximum(jnp.maximum(mx_sc[1], lane_max(_nt(qs[1], k0))),
                          lane_max(jnp.where(causal, _nt(qs[1], k1), NEG)))]
        ms = [jnp.max(t, axis=1, keepdims=True) for t in mx]

        def put3(base, first, val):
            hi, mid, lo = _split3(val)
            return jnp.where(lane == first, hi, jnp.where(lane == first + 1, mid, jnp.where(lane == first + 2, lo, base)))

        qm = [put3(qs[a], L_MAX, -ms[a]) for a in range(2)]

        acc_sc[...] = jnp.zeros(acc_sc.shape, F32)

        def acc_body(j, carry):
            off = pl.multiple_of(j * tb, tb)
            kb, vb = k_ref[pl.ds(off, tb), :], v_ref[pl.ds(off, tb), :]
            for a in range(2):
                acc_sc[a] += _nn(jnp.exp(_nt(qm[a], kb)).astype(BF), vb)
            return carry

        lax.fori_loop(0, i, acc_body, 0)

        def pv(qa, kk, vv, masked):
            p = jnp.exp(_nt(qa, kk))
            if masked:
                p = jnp.where(causal, p, 0.0)
            return _nn(p.astype(BF), vv)

        accs = [acc_sc[0] + pv(qm[0], k0, v0, True),
                acc_sc[1] + pv(qm[1], k0, v0, False) + pv(qm[1], k1, v1, True)]
        for a in range(2):
            rows = slice(a * sub, (a + 1) * sub)
            l = accs[a][:, 64:65]
            o_ref[rows, :] = jnp.where(lane < 64, accs[a] / l, 0.0)
            q2_ref[rows, :] = put3(qs[a], L_LSE, -(ms[a] + jnp.log(l)))

    blk = pl.BlockSpec((None, tb, LANE), lambda h, i: (h, i, 0))
    full = pl.BlockSpec((None, T, LANE), lambda h, i: (h, 0, 0))
    return pl.pallas_call(
        body, name="fox_fwd", grid=(H, T // tb),
        in_specs=[blk, full, full], out_specs=[blk, blk],
        out_shape=[jax.ShapeDtypeStruct((H, T, LANE), F32), jax.ShapeDtypeStruct((H, T, LANE), BF)],
        scratch_shapes=[pltpu.VMEM((2, sub, LANE), F32), pltpu.VMEM((2, sub, LANE), F32)],
        compiler_params=_params(("parallel", "arbitrary"), VMEM_BIG),
    )(q, k, v)


def _merge_out(u_r, o_fox, z_a, x, g_ffn, w_ro, w_fo, w_out, tm=256):
    T = x.shape[0]

    def body(u_ref, of_ref, ar_ref, af_ref, x_ref, g_ref, wro_ref, wfo_ref, wout_ref,
             yr_ref, yf_ref, m_ref, x2_ref, h2_ref, oc_ref):
        u = u_ref[...]
        oc = jnp.concatenate([of_ref[h][:, :FOX_D] for h in range(FOX_H)], axis=-1).astype(BF)
        oc_ref[...] = oc
        yr = jnp.concatenate([_nn(u, wro_ref[k]) for k in range(N_CHIP)], axis=-1)
        yf = jnp.concatenate([_nn(oc, wfo_ref[k]) for k in range(N_CHIP)], axis=-1)
        yr_ref[...] = yr
        yf_ref[...] = yf
        m = (_sigmoid(ar_ref[...]) * yr + _sigmoid(af_ref[...]) * yf).astype(BF)
        m_ref[...] = m
        x2 = x_ref[...]
        for k in range(N_CHIP):
            x2 = x2 + _nn(m[:, 256 * k:256 * k + 256], wout_ref[k])
        x2_ref[...] = x2
        r = lax.rsqrt(jnp.mean(x2 * x2, axis=-1, keepdims=True) + EPS)
        h2_ref[...] = (x2 * r * g_ref[...]).astype(BF)

    row = lambda w: pl.BlockSpec((tm, w), lambda i: (i, 0))
    const = lambda shp: pl.BlockSpec(shp, lambda i: (0,) * len(shp))
    return pl.pallas_call(
        body, name="merge_out", grid=(T // tm,),
        in_specs=[row(512), pl.BlockSpec((FOX_H, tm, LANE), lambda i: (0, i, 0)),
                  pl.BlockSpec((tm, 1024), lambda i: (i, 3)), pl.BlockSpec((tm, 1024), lambda i: (i, 4)),
                  row(1024), const((1, 1024)), const((N_CHIP, 512, 256)), const((N_CHIP, 512, 256)),
                  const((N_CHIP, 256, 1024))],
        out_specs=[row(1024), row(1024), row(1024), row(1024), row(1024), row(512)],
        out_shape=[jax.ShapeDtypeStruct((T, 1024), F32), jax.ShapeDtypeStruct((T, 1024), F32),
                   jax.ShapeDtypeStruct((T, 1024), BF), jax.ShapeDtypeStruct((T, 1024), F32),
                   jax.ShapeDtypeStruct((T, 1024), BF), jax.ShapeDtypeStruct((T, 512), BF)],
        compiler_params=_params(("parallel",), VMEM_BIG),
    )(u_r, o_fox, z_a, z_a, x, g_ffn, w_ro, w_fo, w_out)


def _load_resident(hbm_refs, vmem_refs, sem):
    cps = [pltpu.make_async_copy(h, v, sem.at[i]) for i, (h, v) in enumerate(zip(hbm_refs, vmem_refs))]
    for cp in cps:
        cp.start()
    for cp in cps:
        cp.wait()


def _ffn_fwd(h2, x2, tgt, w_gate, w_up, w_down, tm=FFN_TM):
    T = h2.shape[0]

    def body(h_ref, x2_ref, t_ref, wg_hbm, wu_hbm, wd_hbm, a_ref, b_ref, act_ref, dy_ref, ls_ref, wg, wu, wd, sem):
        @pl.when(pl.program_id(0) == 0)
        def _():
            _load_resident((wg_hbm, wu_hbm, wd_hbm), (wg, wu, wd), sem)
            ls_ref[...] = jnp.zeros(ls_ref.shape, F32)

        h = h_ref[...]
        err = x2_ref[...] - t_ref[...]
        for k in range(N_CHIP):
            gp = _nt(h, wg[k])
            up = _nt(h, wu[k])
            sg = _sigmoid(gp)
            silu = gp * sg
            a_ref[k] = silu.astype(BF)
            b_ref[k] = (up * sg * (1.0 + gp * (1.0 - sg))).astype(BF)
            act = (silu * up).astype(BF)
            act_ref[k] = act
            err = err + _nn(act, wd[k])
        dy_ref[...] = err * (1.0 / D_MODEL)
        ls_ref[...] += jnp.sum(err * err, axis=0, keepdims=True)

    row = pl.BlockSpec((tm, D_MODEL), lambda i: (i, 0))
    hid = pl.BlockSpec((N_CHIP, tm, FF_SH), lambda i: (0, i, 0))
    anyspec = pl.BlockSpec(memory_space=pl.ANY)
    wshape = pltpu.VMEM((N_CHIP, FF_SH, D_MODEL), BF)
    return pl.pallas_call(
        body, name="ffn_fwd", grid=(T // tm,),
        in_specs=[row, row, row, anyspec, anyspec, anyspec],
        out_specs=[hid, hid, hid, row, pl.BlockSpec((1, D_MODEL), lambda i: (0, 0))],
        out_shape=[jax.ShapeDtypeStruct((N_CHIP, T, FF_SH), BF)] * 3
        + [jax.ShapeDtypeStruct((T, D_MODEL), F32), jax.ShapeDtypeStruct((1, D_MODEL), F32)],
        scratch_shapes=[wshape, wshape, wshape, pltpu.SemaphoreType.DMA((3,))],
        compiler_params=_params(("arbitrary",), VMEM_HUGE),
    )(h2, x2, tgt, w_gate, w_up, w_down)


def _ffn_bwd(dy, sa, sb, x2, g_ffn, w_gate, w_up, w_down, tm=FFN_TM):
    T = dy.shape[0]

    def body(dy_ref, a_ref, b_ref, x2_ref, g_ref, wg_hbm, wu_hbm, wd_hbm, dgp_ref, dup_ref, dx_ref, dg_ref,
             wg, wu, wd, sem):
        @pl.when(pl.program_id(0) == 0)
        def _():
            _load_resident((wg_hbm, wu_hbm, wd_hbm), (wg, wu, wd), sem)
            dg_ref[...] = jnp.zeros(dg_ref.shape, F32)

        dy = dy_ref[...]
        dyb = dy.astype(BF)
        dh = jnp.zeros((tm, D_MODEL), F32)
        for k in range(N_CHIP):
            dact = _nt(dyb, wd[k])
            dup = (dact * a_ref[k]).astype(BF)
            dgp = (dact * b_ref[k]).astype(BF)
            dgp_ref[k] = dgp
            dup_ref[k] = dup
            dh = dh + _nn(dgp, wg[k]) + _nn(dup, wu[k])
        x2 = x2_ref[...]
        r = lax.rsqrt(jnp.mean(x2 * x2, axis=-1, keepdims=True) + EPS)
        xn = x2 * r
        dg_ref[...] += jnp.sum(dh * xn, axis=0, keepdims=True)
        dxn = dh * g_ref[...]
        dx_ref[...] = dy + r * (dxn - xn * jnp.mean(dxn * xn, axis=-1, keepdims=True))

    row = pl.BlockSpec((tm, D_MODEL), lambda i: (i, 0))
    hid = pl.BlockSpec((N_CHIP, tm, FF_SH), lambda i: (0, i, 0))
    vec = pl.BlockSpec((1, D_MODEL), lambda i: (0, 0))
    anyspec = pl.BlockSpec(memory_space=pl.ANY)
    wshape = pltpu.VMEM((N_CHIP, FF_SH, D_MODEL), BF)
    return pl.pallas_call(
        body, name="ffn_bwd", grid=(T // tm,),
        in_specs=[row, hid, hid, row, vec, anyspec, anyspec, anyspec],
        out_specs=[hid, hid, row, vec],
        out_shape=[jax.ShapeDtypeStruct((N_CHIP, T, FF_SH), BF), jax.ShapeDtypeStruct((N_CHIP, T, FF_SH), BF),
                   jax.ShapeDtypeStruct((T, D_MODEL), F32), jax.ShapeDtypeStruct((1, D_MODEL), F32)],
        scratch_shapes=[wshape, wshape, wshape, pltpu.SemaphoreType.DMA((3,))],
        compiler_params=_params(("arbitrary",), VMEM_HUGE),
    )(dy, sa, sb, x2, g_ffn, w_gate, w_up, w_down)


def _out_bwd(dx2, z_a, y_r, y_f, o_raw, o_fox, g_ret, w_ro, w_fo, w_out, tm=256):
    T = dx2.shape[0]

    def body(dx_ref, gt_ref, ar_ref, af_ref, yr_ref, yf_ref, o_ref, of_ref, g_ref, wro_ref, wfo_ref, wout_ref,
             dyr_ref, dyf_ref, dgt_ref, da_ref, do_ref, dof_ref, dg_ref):
        i = pl.program_id(0)

        @pl.when(i == 0)
        def _():
            dg_ref[...] = jnp.zeros(dg_ref.shape, F32)

        dxb = dx_ref[...].astype(BF)
        dm = jnp.concatenate([_nt(dxb, wout_ref[k]) for k in range(N_CHIP)], axis=-1)
        sr, sf = _sigmoid(ar_ref[...]), _sigmoid(af_ref[...])
        dyr = dm * sr
        dyf = dm * sf
        da_ref[:, :1024] = (dyr * yr_ref[...] * (1.0 - sr)).astype(BF)
        da_ref[:, 1024:] = (dyf * yf_ref[...] * (1.0 - sf)).astype(BF)
        dyr = dyr.astype(BF)
        dyf = dyf.astype(BF)
        dyr_ref[...] = dyr
        dyf_ref[...] = dyf
        du = jnp.zeros((tm, 512), F32)
        doc = jnp.zeros((tm, 512), F32)
        for k in range(N_CHIP):
            du = du + _nt(dyr[:, 256 * k:256 * k + 256], wro_ref[k])
            doc = doc + _nt(dyf[:, 256 * k:256 * k + 256], wfo_ref[k])

        for h in range(RET_H):
            cols = slice(h * RET_DV, (h + 1) * RET_DV)
            o = o_ref[:, cols]
            mu = jnp.mean(o, axis=-1, keepdims=True)
            xc = o - mu
            rstd = lax.rsqrt(jnp.mean(xc * xc, axis=-1, keepdims=True) + EPS)
            on = xc * rstd
            g = g_ref[:, cols]
            gt = gt_ref[:, cols]
            sg = _sigmoid(gt)
            duh = du[:, cols]
            dgt_ref[:, cols] = (duh * (on * g) * sg * (1.0 + gt * (1.0 - sg))).astype(BF)
            dog = duh * gt * sg
            dg_ref[:, cols] += jnp.sum(dog * on, axis=0, keepdims=True)
            don = dog * g
            do_ref[:, cols] = rstd * (don - jnp.mean(don, axis=-1, keepdims=True)
                                      - on * jnp.mean(don * on, axis=-1, keepdims=True))

        lane = lax.broadcasted_iota(jnp.int32, (tm, LANE), 1)
        zpad = jnp.zeros((tm, 64), F32)
        for h in range(FOX_H):
            doh = doc[:, 64 * h:64 * h + 64]
            delta = jnp.sum(doh * of_ref[h][:, :FOX_D], axis=-1, keepdims=True)
            hi, mid, lo = [t.astype(F32) for t in _split3(-delta)]
            da = jnp.concatenate([doh, zpad], axis=-1)
            da = jnp.where(lane == 64, hi, jnp.where(lane == 65, mid, jnp.where(lane == 66, lo, da)))
            dof_ref[h] = da.astype(BF)

    row = lambda w: pl.BlockSpec((tm, w), lambda i: (i, 0))
    const = lambda shp: pl.BlockSpec(shp, lambda i: (0,) * len(shp))
    hsp = pl.BlockSpec((FOX_H, tm, LANE), lambda i: (0, i, 0))
    return pl.pallas_call(
        body, name="out_bwd", grid=(T // tm,),
        in_specs=[row(1024), pl.BlockSpec((tm, 512), lambda i: (i, 2)), pl.BlockSpec((tm, 1024), lambda i: (i, 3)),
                  pl.BlockSpec((tm, 1024), lambda i: (i, 4)), row(1024), row(1024), row(512), hsp,
                  const((1, 512)), const((N_CHIP, 512, 256)), const((N_CHIP, 512, 256)), const((N_CHIP, 256, 1024))],
        out_specs=[row(1024), row(1024), row(512), row(2048), row(512), hsp, const((1, 512))],
        out_shape=[jax.ShapeDtypeStruct((T, 1024), BF), jax.ShapeDtypeStruct((T, 1024), BF),
                   jax.ShapeDtypeStruct((T, 512), BF), jax.ShapeDtypeStruct((T, 2048), BF),
                   jax.ShapeDtypeStruct((T, 512), F32), jax.ShapeDtypeStruct((FOX_H, T, LANE), BF),
                   jax.ShapeDtypeStruct((1, 512), F32)],
        compiler_params=_params(("arbitrary",), VMEM_BIG),
    )(dx2, z_a, z_a, z_a, y_r, y_f, o_raw, o_fox, g_ret, w_ro, w_fo, w_out)


def _ret_bwd(d_o, qr, kr, z_a, states, cos_t, sin_t, consts, tt=512):
    T = z_a.shape[0]
    nt = T // tt
    nch = tt // CHUNK
    decay, zeta, xi, gcb = consts

    def body(do_ref, q_ref, k_ref, v_ref, st_ref, cos_ref, sin_ref, d_ref, ze_ref, xi_ref, gc_ref, dz_ref, g_sc):
        i = pl.program_id(0)

        @pl.when(i == 0)
        def _():
            g_sc[...] = jnp.zeros(g_sc.shape, F32)

        for c in reversed(range(nch)):
            rows = slice(c * CHUNK, (c + 1) * CHUNK)
            cosv, sinv = cos_ref[rows, :], sin_ref[rows, :]
            dq_parts, dk_parts = [], []
            for h in range(RET_H):
                cols = slice(h * RET_DV, (h + 1) * RET_DV)
                q, k = q_ref[h, rows, :], k_ref[h, rows, :]
                v32 = v_ref[rows, cols]
                vb = v32.astype(BF)
                r = st_ref[h, rows, :]
                g = g_sc[h]
                gb = g.astype(BF)
                d_o = do_ref[rows, cols]
                dob = d_o.astype(BF)
                dox = (d_o * xi_ref[h]).astype(BF)
                dec = d_ref[h]
                s = (_nt(q, k) * dec).astype(BF)
                ds = (_nt(dob, vb) * dec).astype(BF)
                dv = _tn(s, dob) + ze_ref[h] * _nn(k, gb)
                dq = _nn(ds, k) + _nt(dox, r.astype(BF))
                dk = _tn(ds, q) + _nt((v32 * ze_ref[h]).astype(BF), gb)
                g_sc[h] = gc_ref[h] * g + _tn(q, dox)
                dq_parts.append((dq * cosv - _swap32(dq) * sinv)[:, :64])
                dk_parts.append(((dk * cosv - _swap32(dk) * sinv) * 0.125)[:, :64])
                dz_ref[rows, 512 + h * RET_DV:512 + (h + 1) * RET_DV] = dv.astype(BF)
            dz_ref[rows, 0:256] = jnp.concatenate(dq_parts, axis=-1).astype(BF)
            dz_ref[rows, 256:512] = jnp.concatenate(dk_parts, axis=-1).astype(BF)

    rev = lambda i: nt - 1 - i
    hspec = pl.BlockSpec((RET_H, tt, LANE), lambda i: (0, rev(i), 0))
    cspec = pl.BlockSpec((RET_H, CHUNK, LANE), lambda i: (0, 0, 0))
    tab = pl.BlockSpec((tt, LANE), lambda i: (rev(i), 0))
    return pl.pallas_call(
        body, name="ret_bwd", grid=(nt,),
        in_specs=[pl.BlockSpec((tt, 512), lambda i: (rev(i), 0)), hspec, hspec,
                  pl.BlockSpec((tt, 512), lambda i: (rev(i), 1)), hspec, tab, tab, cspec, cspec, cspec, cspec],
        out_specs=pl.BlockSpec((tt, 1024), lambda i: (rev(i), 0)),
        out_shape=jax.ShapeDtypeStruct((T, 1024), BF),
        scratch_shapes=[pltpu.VMEM((RET_H, CHUNK, LANE), F32)],
        compiler_params=_params(("arbitrary",), VMEM_BIG),
    )(d_o, qr, kr, z_a, states, cos_t, sin_t, decay, zeta, xi, gcb)


def _fox_bwd(q2, k, v, do, sub=512):
    H, T, _ = k.shape
    tb = 2 * sub
    n = T // sub

    def body(q_ref, do_ref, k_ref, v_ref, dq_ref, dk_ref, dv_ref, dk_sc, dv_sc):
        j = pl.program_id(1)

        @pl.when(j == 0)
        def _():
            dq_ref[...] = jnp.zeros(dq_ref.shape, F32)

        kk, vv = k_ref[...], v_ref[...]
        dk_sc[...] = jnp.zeros(dk_sc.shape, F32)
        dv_sc[...] = jnp.zeros(dv_sc.shape, F32)
        krow = lax.broadcasted_iota(jnp.int32, (tb, sub), 0)
        qcol = lax.broadcasted_iota(jnp.int32, (tb, sub), 1)

        def step(i, shift):
            off = pl.multiple_of(i * sub, sub)
            qq = q_ref[pl.ds(off, sub), :]
            dd = do_ref[pl.ds(off, sub), :]
            p = jnp.exp(_nt(kk, qq))
            if shift is not None:
                p = jnp.where(qcol + shift >= krow, p, 0.0)
            ds = (p * _nt(vv, dd)).astype(BF)
            dv_sc[...] += _nn(p.astype(BF), dd)
            dk_sc[...] += _nn(ds, qq)
            dq_ref[pl.ds(off, sub), :] += _tn(ds, kk)

        off0 = pl.multiple_of(2 * j * sub, sub)
        q0, d0 = q_ref[pl.ds(off0, sub), :], do_ref[pl.ds(off0, sub), :]
        k0, v0 = k_ref[0:sub, :], v_ref[0:sub, :]
        p0 = jnp.where(qcol[0:sub, :] >= krow[0:sub, :], jnp.exp(_nt(k0, q0)), 0.0)
        ds0 = (p0 * _nt(v0, d0)).astype(BF)
        dv_sc[0:sub, :] += _nn(p0.astype(BF), d0)
        dk_sc[0:sub, :] += _nn(ds0, q0)
        dq_ref[pl.ds(off0, sub), :] += _tn(ds0, k0)
        step(2 * j + 1, sub)

        def loop_body(i, carry):
            step(i, None)
            return carry

        lax.fori_loop(2 * j + 2, n, loop_body, 0)
        dk_ref[...] = dk_sc[...]
        dv_ref[...] = dv_sc[...]

    blk = pl.BlockSpec((None, tb, LANE), lambda h, j: (h, j, 0))
    full = pl.BlockSpec((None, T, LANE), lambda h, j: (h, 0, 0))
    shp = jax.ShapeDtypeStruct((H, T, LANE), F32)
    return pl.pallas_call(
        body, name="fox_bwd", grid=(H, T // tb),
        in_specs=[full, full, blk, blk], out_specs=[full, blk, blk], out_shape=[shp, shp, shp],
        scratch_shapes=[pltpu.VMEM((tb, LANE), F32), pltpu.VMEM((tb, LANE), F32)],
        compiler_params=_params(("arbitrary", "arbitrary"), VMEM_BIG),
    )(q2, do, k, v)


def _fox_post_bwd(dq, dk, dv, z_a, z_ff, b_f, g_q, g_k, tm=256):
    T = z_a.shape[0]
    nt = T // tm

    def body(dq_ref, dk_ref, dv_ref, zf_ref, zff_ref, b_ref, gq_ref, gk_ref,
             dz_ref, dff_ref, dgq_ref, dgk_ref, db_ref, carry):
        i = pl.program_id(0)

        @pl.when(i == 0)
        def _():
            carry[...] = jnp.zeros(carry.shape, F32)
            dgq_ref[...] = jnp.zeros(dgq_ref.shape, F32)
            dgk_ref[...] = jnp.zeros(dgk_ref.shape, F32)
            db_ref[...] = jnp.zeros(db_ref.shape, F32)

        lane = lax.broadcasted_iota(jnp.int32, (tm, LANE), 1)
        zf = zf_ref[...]
        dcm = jnp.zeros((tm, LANE), F32)
        dq_parts, dk_parts, dv_parts = [], [], []
        gq_acc = jnp.zeros((1, 64), F32)
        gk_acc = jnp.zeros((1, 64), F32)
        for h in range(FOX_H):
            dqa, dka = dq_ref[h], dk_ref[h]
            dcm = jnp.where(lane == h, dqa[:, L_CQ:L_CQ + 1] - dka[:, L_CK:L_CK + 1], dcm)
            for src, dya, g_ref, scale, parts in ((0, dqa, gq_ref, 0.125, dq_parts), (512, dka, gk_ref, 1.0, dk_parts)):
                xh = zf[:, src + 64 * h:src + 64 * h + 64]
                r = lax.rsqrt(jnp.mean(xh * xh, axis=-1, keepdims=True) + EPS)
                xn = xh * r
                dy = dya[:, :FOX_D] * scale
                if src == 0:
                    gq_acc = gq_acc + jnp.sum(dy * xn, axis=0, keepdims=True)
                else:
                    gk_acc = gk_acc + jnp.sum(dy * xn, axis=0, keepdims=True)
                dxn = dy * g_ref[...]
                parts.append(r * (dxn - xn * jnp.mean(dxn * xn, axis=-1, keepdims=True)))
            dv_parts.append(dv_ref[h][:, :FOX_D])
        dz_ref[...] = jnp.concatenate(dq_parts + dk_parts + dv_parts, axis=-1).astype(BF)
        zpad = jnp.zeros((1, 64), F32)
        dgq_ref[...] += jnp.concatenate([gq_acc, zpad], axis=-1)
        dgk_ref[...] += jnp.concatenate([gk_acc, zpad], axis=-1)

        row = lax.broadcasted_iota(jnp.int32, (tm, tm), 0)
        col = lax.broadcasted_iota(jnp.int32, (tm, tm), 1)
        tri = (row <= col).astype(BF)
        hi, mid, lo = _split3(dcm)
        dlogf = _nn(tri, hi) + _nn(tri, mid) + _nn(tri, lo) + carry[...]
        carry[...] = dlogf[0:1, :]
        dff = jnp.where(lane < FOX_H, dlogf * _sigmoid(-(zff_ref[...] + b_ref[...])), 0.0)
        dff_ref[...] = dff.astype(BF)
        db_ref[...] += jnp.sum(dff, axis=0, keepdims=True)

    rev = lambda i: nt - 1 - i
    hsp = pl.BlockSpec((FOX_H, tm, LANE), lambda i: (0, rev(i), 0))
    small = lambda w: pl.BlockSpec((1, w), lambda i: (0, 0))
    return pl.pallas_call(
        body, name="fox_post_bwd", grid=(nt,),
        in_specs=[hsp, hsp, hsp, pl.BlockSpec((tm, 1536), lambda i: (rev(i), 1)),
                  pl.BlockSpec((tm, LANE), lambda i: (rev(i), 0)), small(LANE), small(64), small(64)],
        out_specs=[pl.BlockSpec((tm, 1536), lambda i: (rev(i), 0)), pl.BlockSpec((tm, LANE), lambda i: (rev(i), 0)),
                   small(LANE), small(LANE), small(LANE)],
        out_shape=[jax.ShapeDtypeStruct((T, 1536), BF), jax.ShapeDtypeStruct((T, LANE), BF),
                   jax.ShapeDtypeStruct((1, LANE), F32), jax.ShapeDtypeStruct((1, LANE), F32),
                   jax.ShapeDtypeStruct((1, LANE), F32)],
        scratch_shapes=[pltpu.VMEM((1, LANE), F32)],
        compiler_params=_params(("arbitrary",), VMEM_BIG),
    )(dq, dk, dv, z_a, z_ff, b_f, g_q, g_k)


def _in_bwd(dz_ret, dz_gt, dz_fox, dz_a, dz_ff, w_a, w_ff, x, g_mix, dx2, tm=256):
    T = x.shape[0]

    def body(r_ref, t_ref, f_ref, a_ref, ff_ref, wa_ref, wf_ref, x_ref, g_ref, dx2_ref, dx_ref, dg_ref):
        i = pl.program_id(0)

        @pl.when(i == 0)
        def _():
            dg_ref[...] = jnp.zeros(dg_ref.shape, F32)

        dh = (_nt(r_ref[...], wa_ref[:, C_RET:C_GT]) + _nt(t_ref[...], wa_ref[:, C_GT:C_FOX])
              + _nt(f_ref[...], wa_ref[:, C_FOX:C_A]) + _nt(a_ref[...], wa_ref[:, C_A:C_END])
              + _nt(ff_ref[...], wf_ref[...]))
        xv = x_ref[...]
        r = lax.rsqrt(jnp.mean(xv * xv, axis=-1, keepdims=True) + EPS)
        xn = xv * r
        dg_ref[...] += jnp.sum(dh * xn, axis=0, keepdims=True)
        dxn = dh * g_ref[...]
        dx_ref[...] = dx2_ref[...] + r * (dxn - xn * jnp.mean(dxn * xn, axis=-1, keepdims=True))

    row = lambda w: pl.BlockSpec((tm, w), lambda i: (i, 0))
    const = lambda shp: pl.BlockSpec(shp, lambda i: (0,) * len(shp))
    return pl.pallas_call(
        body, name="in_bwd", grid=(T // tm,),
        in_specs=[row(1024), row(512), row(1536), row(2048), row(LANE), const((D_MODEL, C_END)),
                  const((D_MODEL, LANE)), row(1024), const((1, 1024)), row(1024)],
        out_specs=[row(1024), const((1, 1024))],
        out_shape=[jax.ShapeDtypeStruct((T, 1024), F32), jax.ShapeDtypeStruct((1, 1024), F32)],
        compiler_params=_params(("arbitrary",), VMEM_BIG),
    )(dz_ret, dz_gt, dz_fox, dz_a, dz_ff, w_a, w_ff, x, g_mix, dx2)


def _mesh_pos():
    return lax.axis_index("x"), lax.axis_index("y"), lax.axis_index("c")


def _staged_place(src, name):
    stacked = src.ndim == 3
    R, C = src.shape[-2:]
    tr = _row_tile(R, 128, 16)
    n = R // tr
    assert n >= 2

    def body(s_ref, o_ref, buf, sem):
        i = pl.program_id(0)
        slot = i % 2
        x, y, _ = _mesh_pos()
        kme = 2 * x + y

        def out_copy(s, step):
            return pltpu.make_async_copy(buf.at[s], o_ref.at[kme, pl.ds(pl.multiple_of(step * tr, tr), tr), :], sem.at[s])

        @pl.when(i >= 2)
        def _():
            out_copy(slot, i - 2).wait()

        buf[slot] = (s_ref[kme] if stacked else s_ref[...]).astype(BF)
        out_copy(slot, i).start()

        @pl.when(i == n - 1)
        def _():
            out_copy(1 - slot, i - 1).wait()
            out_copy(slot, i).wait()

    in_spec = (pl.BlockSpec((N_CHIP, tr, C), lambda i: (0, i, 0)) if stacked else pl.BlockSpec((tr, C), lambda i: (i, 0)))
    return pl.pallas_call(
        body, name=name, grid=(n,), in_specs=[in_spec], out_specs=pl.BlockSpec(memory_space=pl.ANY),
        out_shape=jax.ShapeDtypeStruct((N_CHIP, R, C), BF),
        scratch_shapes=[pltpu.VMEM((2, tr, C), BF), pltpu.SemaphoreType.DMA((2,))],
        compiler_params=_params(("arbitrary",)),
    )(src)


def _push_copies(src, land, send_sem, recv_sem, receiving):
    x, y, c = _mesh_pos()
    kme = 2 * x + y
    cps = []
    for w in range(len(land)):
        for j, (px, py) in enumerate([(1 - x, y), (x, 1 - y), (1 - x, 1 - y)]):
            kpeer = 2 * px + py
            cps.append(pltpu.make_async_remote_copy(
                src_ref=land[w].at[kme] if src is None else src[w].at[kpeer],
                dst_ref=land[w].at[kpeer if receiving else kme],
                send_sem=send_sem.at[3 * w + j], recv_sem=recv_sem.at[3 * w + j],
                device_id=(px, py, c), device_id_type=MESH))
    return cps


def _gather_in_place(stacks):
    n = len(stacks)

    def body(*refs):
        land, send_sem, recv_sem = refs[n:2 * n], refs[2 * n], refs[2 * n + 1]
        for cp in _push_copies(None, land, send_sem, recv_sem, False):
            cp.start()
        for cp in _push_copies(None, land, send_sem, recv_sem, True):
            cp.wait_recv()
            cp.wait_send()

    anyspec = pl.BlockSpec(memory_space=pl.ANY)
    return pl.pallas_call(
        body, name="gather_in_place", in_specs=[anyspec] * n, out_specs=[anyspec] * n,
        out_shape=[jax.ShapeDtypeStruct(s.shape, s.dtype) for s in stacks],
        input_output_aliases={i: i for i in range(n)},
        scratch_shapes=[pltpu.SemaphoreType.DMA((3 * n,)), pltpu.SemaphoreType.DMA((3 * n,))],
    )(*stacks)


def _scatter_partials(srcs, lands, small):
    n = len(srcs)

    def body(*refs):
        src, sv = refs[:n], refs[2 * n]
        land, svo = refs[2 * n + 1:3 * n + 1], refs[3 * n + 1]
        send_sem, recv_sem, ssend, srecv, sloc = refs[3 * n + 2:]
        x, y, c = _mesh_pos()
        me = 4 * x + 2 * y + c
        flips = [(b >> 2 & 1, b >> 1 & 1, b & 1) for b in range(1, 8)]
        others = [(1 - x if fx else x, 1 - y if fy else y, 1 - c if fc else c) for fx, fy, fc in flips]
        local = pltpu.make_async_copy(sv, svo.at[me], sloc)
        local.start()
        sends = []
        for j, (px, py, pc) in enumerate(others):
            cp = pltpu.make_async_remote_copy(
                src_ref=sv, dst_ref=svo.at[me], send_sem=ssend.at[j], recv_sem=srecv.at[j],
                device_id=(px, py, pc), device_id_type=MESH)
            cp.start()
            sends.append(cp)
        for cp in _push_copies(src, land, send_sem, recv_sem, False):
            cp.start()
            sends.append(cp)
        for j, (px, py, pc) in enumerate(others):
            pltpu.make_async_remote_copy(
                src_ref=sv, dst_ref=svo.at[4 * px + 2 * py + pc], send_sem=ssend.at[j], recv_sem=srecv.at[j],
                device_id=(px, py, pc), device_id_type=MESH).wait_recv()
        for cp in _push_copies(src, land, send_sem, recv_sem, True):
            cp.wait_recv()
        for cp in sends:
            cp.wait_send()
        local.wait()

    anyspec = pl.BlockSpec(memory_space=pl.ANY)
    return pl.pallas_call(
        body, name="scatter_partials",
        in_specs=[anyspec] * (2 * n + 1), out_specs=[anyspec] * (n + 1),
        out_shape=[jax.ShapeDtypeStruct(s.shape, s.dtype) for s in lands]
        + [jax.ShapeDtypeStruct((8,) + small.shape, small.dtype)],
        input_output_aliases={n + i: i for i in range(n)},
        scratch_shapes=[pltpu.SemaphoreType.DMA((3 * n,)), pltpu.SemaphoreType.DMA((3 * n,)),
                        pltpu.SemaphoreType.DMA((7,)), pltpu.SemaphoreType.DMA((7,)), pltpu.SemaphoreType.DMA],
    )(*srcs, *lands, small)


_HBM_SPEC = pl.BlockSpec(memory_space=pltpu.HBM)
_SEM_SPEC = pl.BlockSpec(memory_space=pltpu.SEMAPHORE)
_SPLIT_PARAMS = pltpu.CompilerParams(has_side_effects=pltpu.SideEffectType.DATAFLOW_SIDE_EFFECTING)


def _push_start(srcs, lands, after, name):
    n = len(lands)
    ns = 0 if srcs is None else n

    def body(*refs):
        src = None if srcs is None else refs[:n]
        land = refs[ns:ns + n]
        send_sem, recv_sem = refs[ns + n + 1], refs[ns + n + 2]
        for cp in _push_copies(src, land, send_sem, recv_sem, False):
            cp.start()
        refs[-1][...] = jnp.zeros(refs[-1].shape, F32)

    ops = [pltpu.with_memory_space_constraint(a, pltpu.HBM) for a in ([] if srcs is None else list(srcs)) + list(lands)]
    res = pl.pallas_call(
        body, name=name,
        out_shape=(pltpu.SemaphoreType.DMA((3 * n,)), pltpu.SemaphoreType.DMA((3 * n,)),
                   *[pltpu.HBM(a.shape, a.dtype) for a in ops], jax.ShapeDtypeStruct((8, LANE), F32)),
        in_specs=[_HBM_SPEC] * len(ops) + [pl.BlockSpec(memory_space=pl.ANY)],
        out_specs=(_SEM_SPEC, _SEM_SPEC, *([_HBM_SPEC] * len(ops)), pl.BlockSpec(memory_space=pltpu.VMEM)),
        input_output_aliases={i: 2 + i for i in range(len(ops))},
        compiler_params=_SPLIT_PARAMS,
    )(*ops, after)
    return res[0], res[1], list(res[2:2 + len(ops)]), res[-1]


def _push_wait(send_sem, recv_sem, bufs, after, name, has_src):
    n = len(bufs) // 2 if has_src else len(bufs)
    ns = n if has_src else 0

    def body(*refs):
        src = refs[:n] if has_src else None
        land = refs[ns:ns + n]
        for cp in _push_copies(src, land, refs[ns + n], refs[ns + n + 1], True):
            cp.wait_send()
            cp.wait_recv()

    res = pl.pallas_call(
        body, name=name,
        out_shape=tuple(pltpu.HBM(a.shape, a.dtype) for a in bufs),
        in_specs=[_HBM_SPEC] * len(bufs) + [_SEM_SPEC, _SEM_SPEC, pl.BlockSpec(memory_space=pl.ANY)],
        out_specs=tuple([_HBM_SPEC] * len(bufs)),
        input_output_aliases={i: i for i in range(len(bufs))},
        compiler_params=_SPLIT_PARAMS,
    )(*bufs, send_sem, recv_sem, after)
    return list(res[ns:ns + n])


def _sibling_exchange(arrs):
    n = len(arrs)

    def body(*refs):
        ins, outs = refs[:n], refs[n:2 * n]
        send_sems, recv_sems = refs[2 * n:]
        x, y, c = _mesh_pos()
        cps = [pltpu.make_async_remote_copy(
            src_ref=ins[w], dst_ref=outs[w], send_sem=send_sems.at[w], recv_sem=recv_sems.at[w],
            device_id=(x, y, 1 - c), device_id_type=MESH) for w in range(n)]
        for cp in cps:
            cp.start()
        for cp in cps:
            cp.wait_recv()
        for cp in cps:
            cp.wait_send()

    anyspec = pl.BlockSpec(memory_space=pl.ANY)
    return pl.pallas_call(
        body, name="sibling_exchange",
        in_specs=[anyspec] * n, out_specs=[anyspec] * n,
        out_shape=[jax.ShapeDtypeStruct(a.shape, a.dtype) for a in arrs],
        scratch_shapes=[pltpu.SemaphoreType.DMA((n,)), pltpu.SemaphoreType.DMA((n,))],
    )(*arrs)


def _sum_stack(stack, name):
    _, R, C = stack.shape
    tr = _row_tile(R, 256, 16)

    def body(s_ref, o_ref):
        acc = s_ref[0].astype(F32)
        for k in range(1, N_CHIP):
            acc = acc + s_ref[k].astype(F32)
        o_ref[...] = acc

    return pl.pallas_call(
        body, name=name, grid=(R // tr,),
        in_specs=[pl.BlockSpec((N_CHIP, tr, C), lambda i: (0, i, 0))],
        out_specs=pl.BlockSpec((tr, C), lambda i: (i, 0)),
        out_shape=jax.ShapeDtypeStruct((R, C), F32),
        compiler_params=_params(("parallel",)),
    )(stack)


def _adam_math(w, g, m, v):
    m2 = ADAM_B1 * m + (1.0 - ADAM_B1) * g
    v2 = ADAM_B2 * v + (1.0 - ADAM_B2) * (g * g)
    m_hat = m2 / (1.0 - ADAM_B1 ** ADAM_STEP)
    v_hat = v2 / (1.0 - ADAM_B2 ** ADAM_STEP)
    delta = -ADAM_LR * (m_hat / (jnp.sqrt(v_hat) + ADAM_EPS) + ADAM_WD * w)
    return delta, m2, v2


def _adamw(w, m, v, s0, s1, name):
    R, C = w.shape
    tr = _row_tile(R, 128, 8)

    def body(w_ref, m_ref, v_ref, a_ref, b_ref, g_ref, d_ref, m2_ref, v2_ref):
        g = a_ref[...] + b_ref[...]
        delta, m2, v2 = _adam_math(w_ref[...], g, m_ref[...], v_ref[...])
        g_ref[...] = g
        d_ref[...] = delta
        m2_ref[...] = m2
        v2_ref[...] = v2

    spec = pl.BlockSpec((tr, C), lambda i: (i, 0))
    shp = jax.ShapeDtypeStruct((R, C), F32)
    return pl.pallas_call(
        body, name=name, grid=(R // tr,), in_specs=[spec] * 5, out_specs=[spec] * 4, out_shape=[shp] * 4,
        compiler_params=_params(("parallel",), VMEM_BIG),
    )(w, m, v, s0, s1)


def _adamw_small(w, m, v, gathered):
    def body(w_ref, m_ref, v_ref, s_ref, g_ref, d_ref, m2_ref, v2_ref):
        g = s_ref[0]
        for d in range(1, 8):
            g = g + s_ref[d]
        delta, m2, v2 = _adam_math(w_ref[...], g, m_ref[...], v_ref[...])
        g_ref[...] = g
        d_ref[...] = delta
        m2_ref[...] = m2
        v2_ref[...] = v2

    shp = jax.ShapeDtypeStruct(w.shape, F32)
    return pl.pallas_call(body, name="adamw_small", out_shape=[shp] * 4)(w, m, v, gathered)


SMALL = (("g_mix", 1024), ("g_ffn", 1024), ("g_ret_norm", 512), ("g_fox_q", 64), ("g_fox_k", 64), ("b_forget", 8))
SMALL_W = 3072


def _pack_small(parts):
    cols = []
    for (name, n) in SMALL:
        p = parts[name].reshape(1, -1)[:, :n]
        pad = -n % LANE
        cols.append(jnp.pad(p, ((0, 0), (0, pad))) if pad else p)
    used = sum(c.shape[1] for c in cols)
    cols.append(jnp.zeros((1, SMALL_W - used), F32))
    return jnp.concatenate(cols, axis=1)


def _unpack_small(vec):
    out, off = {}, 0
    for (name, n) in SMALL:
        out[name] = vec[:, off:off + n]
        off += n + (-n % LANE)
    return out


def kernel(x, g_mix, w_in, b_forget, g_ret_norm, w_ret_o, g_fox_q, g_fox_k, w_fox_o, w_out, g_ffn, w_gate, w_up, w_down, loss_target, m_g_mix, m_w_in, m_b_forget, m_g_ret_norm, m_w_ret_o, m_g_fox_q, m_g_fox_k, m_w_fox_o, m_w_out, m_g_ffn, m_w_gate, m_w_up, m_w_down, v_g_mix, v_w_in, v_b_forget, v_g_ret_norm, v_w_ret_o, v_g_fox_q, v_g_fox_k, v_w_fox_o, v_w_out, v_g_ffn, v_w_gate, v_w_up, v_w_down):
    T = x.shape[1]
    xs = x[0]
    tgt = loss_target[0]
    big_names = ("w_in", "w_ret_o", "w_fox_o", "w_out", "w_gate", "w_up", "w_down")
    tr = lambda a: jnp.swapaxes(a[0], 0, 1)
    big_w = dict(w_in=w_in[0], w_ret_o=w_ret_o[0], w_fox_o=w_fox_o[0], w_out=w_out[0], w_gate=tr(w_gate),
                 w_up=tr(w_up), w_down=w_down[0])
    big_m = dict(w_in=m_w_in[0], w_ret_o=m_w_ret_o[0], w_fox_o=m_w_fox_o[0], w_out=m_w_out[0], w_gate=tr(m_w_gate),
                 w_up=tr(m_w_up), w_down=m_w_down[0])
    big_v = dict(w_in=v_w_in[0], w_ret_o=v_w_ret_o[0], w_fox_o=v_w_fox_o[0], w_out=v_w_out[0], w_gate=tr(v_w_gate),
                 w_up=tr(v_w_up), w_down=v_w_down[0])
    small_w = dict(g_mix=g_mix, g_ffn=g_ffn, g_ret_norm=g_ret_norm, g_fox_q=g_fox_q, g_fox_k=g_fox_k, b_forget=b_forget)
    small_m = dict(g_mix=m_g_mix, g_ffn=m_g_ffn, g_ret_norm=m_g_ret_norm, g_fox_q=m_g_fox_q, g_fox_k=m_g_fox_k,
                   b_forget=m_b_forget)
    small_v = dict(g_mix=v_g_mix, g_ffn=v_g_ffn, g_ret_norm=v_g_ret_norm, g_fox_q=v_g_fox_q, g_fox_k=v_g_fox_k,
                   b_forget=v_b_forget)

    stacks = [_staged_place(big_w[n], "place_" + n) for n in big_names]
    (s_in,) = _gather_in_place(stacks[:1])
    w_send, w_recv, w_bufs, w_tok = _push_start(None, stacks[1:], s_in, "gather_rest_start")
    w_a, w_ff = _assemble_w_in(s_in)
    b_pad = jnp.pad(b_forget, ((0, 0), (0, LANE - FOX_H)))
    cos_t, sin_t = _rope_tables(T)
    consts = _ret_consts()

    h = _rms_cast(xs, g_mix + w_tok[0:1, 0:1])
    z_a = _mm_nn(h, w_a, "proj_in")
    z_ff = _mm_nn(h, w_ff, "proj_ff")
    qr, kr, qf, kf, vf = _mix_prep(z_a, z_ff, cos_t, sin_t, b_pad, g_fox_q, g_fox_k)
    o_raw, u_r, states = _ret_fwd(qr, kr, z_a, g_ret_norm, consts)
    o_fox, q2 = _fox_fwd(qf, kf, vf)
    s_ro, s_fo, s_out, s_gate, s_up, s_down = _push_wait(w_send, w_recv, w_bufs, q2, "gather_rest_wait", False)
    y_r, y_f, mrg, x2, h2, o_cat = _merge_out(u_r, o_fox, z_a, xs, g_ffn, s_ro, s_fo, s_out)
    sa, sb, act, dy, loss_vec = _ffn_fwd(h2, x2, tgt, s_gate, s_up, s_down)
    loss = lax.psum(0.5 / D_MODEL * jnp.sum(loss_vec), ("x", "y", "c"))

    dgp, dup, dx2, dg_ffn = _ffn_bwd(dy, sa, sb, x2, g_ffn, s_gate, s_up, s_down)
    ffn_part = [_grad_astack(dgp, h2, "gw_gate"), _grad_astack(dup, h2, "gw_up"), _grad_astack(act, dy, "gw_down")]
    f_send, f_recv, f_bufs, f_tok = _push_start(
        ffn_part, [_staged_place(g, "place_g_" + n) for g, n in zip(ffn_part, big_names[4:])], dx2, "scatter_ffn_start")
    d_yr, d_yf, dz_gt, dz_a, d_o, do_fox, dg_ret = _out_bwd(dx2, z_a, y_r, y_f, o_raw, o_fox,
                                                            g_ret_norm + f_tok[0:1, 0:1], s_ro, s_fo, s_out)
    dz_ret = _ret_bwd(d_o, qr, kr, z_a, states, cos_t, sin_t, consts)
    dq_f, dk_f, dv_f = _fox_bwd(q2, kf, vf, do_fox)
    dz_fox, dz_ff, dg_q, dg_k, db_f = _fox_post_bwd(dq_f, dk_f, dv_f, z_a, z_ff, b_pad, g_fox_q, g_fox_k)
    grad_x, dg_mix = _in_bwd(dz_ret, dz_gt, dz_fox, dz_a, dz_ff, w_a, w_ff, xs, g_mix, dx2)

    g_in = _pack_g_in(_grad_plain(h, dz_ret, "gw_in_ret", F32), _grad_plain(h, dz_gt, "gw_in_gt", F32),
                      _grad_plain(h, dz_fox, "gw_in_fox", F32, tn=768), _grad_plain(h, dz_a, "gw_in_a", F32),
                      _grad_plain(h, dz_ff, "gw_in_ff", F32))
    late_part = [g_in, _grad_colstack(u_r, d_yr, "gw_ret_o", 256), _grad_colstack(o_cat, d_yf, "gw_fox_o", 256),
                 _grad_plain(mrg, dx2, "gw_out", BF).reshape(N_CHIP, 256, D_MODEL)]
    small_g = _pack_small(dict(g_mix=dg_mix, g_ffn=dg_ffn, g_ret_norm=dg_ret, g_fox_q=dg_q, g_fox_k=dg_k, b_forget=db_f))

    late_land = [_staged_place(g, "place_g_" + n) for g, n in zip(late_part, big_names[:4])]
    recv = _scatter_partials(late_part, late_land, small_g)
    recv_ffn = _push_wait(f_send, f_recv, f_bufs, recv[0], "scatter_ffn_wait", True)
    sums = [_sum_stack(r, "sum_" + n) for r, n in zip(list(recv[:4]) + recv_ffn, big_names)]
    sib = _sibling_exchange(sums)
    big_out = {n: _adamw(big_w[n], big_m[n], big_v[n], sums[i], sib[i], "adamw_" + n) for i, n in enumerate(big_names)}
    sg, sd, sm, sv = _adamw_small(_pack_small(small_w), _pack_small(small_m), _pack_small(small_v), recv[-1])
    small_out = [_unpack_small(t) for t in (sg, sd, sm, sv)]

    order = ("g_mix", "w_in", "b_forget", "g_ret_norm", "w_ret_o", "g_fox_q", "g_fox_k", "w_fox_o", "w_out", "g_ffn",
             "w_gate", "w_up", "w_down")
    outs = [loss, grad_x[None]]
    for idx in range(4):
        for n in order:
            if n in ("w_gate", "w_up"):
                outs.append(jnp.swapaxes(big_out[n][idx], 0, 1)[None])
            else:
                outs.append(big_out[n][idx][None] if n in big_out else small_out[idx][n])
    return tuple(outs)
```

```python
import functools
import math

import numpy as np
import jax
import jax.numpy as jnp
from jax import lax
from jax.experimental import pallas as pl
from jax.experimental.pallas import tpu as pltpu

F32 = jnp.float32
BF = jnp.bfloat16
MESH = pl.DeviceIdType.MESH

D_MODEL = 1024
D_FF = 2816
N_CHIP = 4
FF_SH = D_FF // N_CHIP
IN_COLS = 5128
IN_SH = IN_COLS // N_CHIP
RET_H, RET_DV = 4, 128
FOX_H, FOX_D = 8, 64
CHUNK = 128
EPS = 1e-6
NEG = -1e30
LANE = 128
C_RET, C_GT, C_FOX, C_A, C_END = 0, 1024, 1536, 3072, 5120
L_CQ, L_CK, L_LSE, L_MAX = 64, 67, 70, 73

ADAM_LR, ADAM_B1, ADAM_B2, ADAM_EPS, ADAM_WD, ADAM_STEP = 0.001, 0.9, 0.999, 1e-08, 0.01, 10
VMEM_BIG = 56 * 1024 * 1024
VMEM_HUGE = 60 * 1024 * 1024
GRAD_TK = 2048
FFN_TM = 512


def _nn(a, b):
    return lax.dot_general(a, b, (((1,), (0,)), ((), ())), preferred_element_type=F32)


def _nt(a, b):
    return lax.dot_general(a, b, (((1,), (1,)), ((), ())), preferred_element_type=F32)


def _tn(a, b):
    return lax.dot_general(a, b, (((0,), (0,)), ((), ())), preferred_element_type=F32)


def _split3(x):
    hi = x.astype(BF)
    r = x - hi.astype(F32)
    mid = r.astype(BF)
    lo = (r - mid.astype(F32)).astype(BF)
    return hi, mid, lo


def _sigmoid(x):
    return 0.5 * jnp.tanh(0.5 * x) + 0.5


def _swap32(x):
    lane = lax.broadcasted_iota(jnp.int32, x.shape, 1)
    return jnp.where(lane < 32, pltpu.roll(x, 96, 1), pltpu.roll(x, 32, 1))


def _params(sem, vmem=None):
    return pltpu.CompilerParams(dimension_semantics=sem, vmem_limit_bytes=vmem)


def _row_tile(rows, cap, mult):
    return max(d for d in range(mult, cap + 1, mult) if rows % d == 0)


def _assemble_w_in(stack, tr=256):
    def body(s_ref, a_ref, f_ref):
        full = jnp.concatenate([s_ref[k].astype(F32) for k in range(N_CHIP)], axis=-1)
        a_ref[...] = jnp.concatenate([full[:, :3072], full[:, 3080:IN_COLS]], axis=-1).astype(BF)
        f_ref[...] = jnp.concatenate([full[:, 3072:3080], jnp.zeros((tr, LANE - FOX_H), F32)], axis=-1).astype(BF)

    return pl.pallas_call(
        body, name="assemble_w_in", grid=(D_MODEL // tr,),
        in_specs=[pl.BlockSpec((N_CHIP, tr, IN_SH), lambda i: (0, i, 0))],
        out_specs=[pl.BlockSpec((tr, C_END), lambda i: (i, 0)), pl.BlockSpec((tr, LANE), lambda i: (i, 0))],
        out_shape=[jax.ShapeDtypeStruct((D_MODEL, C_END), BF), jax.ShapeDtypeStruct((D_MODEL, LANE), BF)],
        compiler_params=_params(("parallel",), VMEM_BIG),
    )(stack)


def _pack_g_in(g_ret, g_gt, g_fox, g_a, g_ff, tr=256):
    def body(r_ref, t_ref, x_ref, a_ref, f_ref, o_ref):
        full = jnp.concatenate([r_ref[...], t_ref[...], x_ref[...], f_ref[...][:, :FOX_H], a_ref[...]], axis=-1)
        for k in range(N_CHIP):
            o_ref[k] = full[:, k * IN_SH:(k + 1) * IN_SH].astype(BF)

    def spec(w):
        return pl.BlockSpec((tr, w), lambda i: (i, 0))

    return pl.pallas_call(
        body, name="pack_g_in", grid=(D_MODEL // tr,),
        in_specs=[spec(1024), spec(512), spec(1536), spec(2048), spec(LANE)],
        out_specs=pl.BlockSpec((N_CHIP, tr, IN_SH), lambda i: (0, i, 0)),
        out_shape=jax.ShapeDtypeStruct((N_CHIP, D_MODEL, IN_SH), BF),
        compiler_params=_params(("parallel",), VMEM_BIG),
    )(g_ret, g_gt, g_fox, g_a, g_ff)


def _rms_cast(x, g, tm=512):
    T = x.shape[0]

    def body(x_ref, g_ref, o_ref):
        xv = x_ref[...]
        r = lax.rsqrt(jnp.mean(xv * xv, axis=-1, keepdims=True) + EPS)
        o_ref[...] = (xv * r * g_ref[...]).astype(BF)

    return pl.pallas_call(
        body, name="rms_cast", grid=(T // tm,),
        in_specs=[pl.BlockSpec((tm, D_MODEL), lambda i: (i, 0)), pl.BlockSpec((1, D_MODEL), lambda i: (0, 0))],
        out_specs=pl.BlockSpec((tm, D_MODEL), lambda i: (i, 0)),
        out_shape=jax.ShapeDtypeStruct((T, D_MODEL), BF),
        compiler_params=_params(("parallel",)),
    )(x, g)


def _mm_nn(a, b, name, tm=512, tn=1024):
    M, K = a.shape
    N = b.shape[1]
    tn = min(tn, N)

    def body(a_ref, b_ref, o_ref):
        o_ref[...] = _nn(a_ref[...], b_ref[...])

    return pl.pallas_call(
        body, name=name, grid=(N // tn, M // tm),
        in_specs=[pl.BlockSpec((tm, K), lambda j, i: (i, 0)), pl.BlockSpec((K, tn), lambda j, i: (0, j))],
        out_specs=pl.BlockSpec((tm, tn), lambda j, i: (i, j)),
        out_shape=jax.ShapeDtypeStruct((M, N), F32),
        compiler_params=_params(("parallel", "parallel")),
    )(a, b)


def _mm_tn(a, b, name, grid, a_spec, b_spec, o_spec, out_shape, acc_shape):
    nk = grid[-1]

    def body(a_ref, b_ref, o_ref, acc):
        k = pl.program_id(len(grid) - 1)

        @pl.when(k == 0)
        def _():
            acc[...] = jnp.zeros(acc.shape, F32)

        acc[...] += _tn(a_ref[...].astype(BF), b_ref[...].astype(BF))

        @pl.when(k == nk - 1)
        def _():
            o_ref[...] = acc[...].astype(o_ref.dtype)

    return pl.pallas_call(
        body, name=name, grid=grid, in_specs=[a_spec, b_spec], out_specs=o_spec, out_shape=out_shape,
        scratch_shapes=[pltpu.VMEM(acc_shape, F32)],
        compiler_params=_params(("parallel",) * (len(grid) - 1) + ("arbitrary",), VMEM_BIG),
    )(a, b)


def _grad_plain(a, b, name, out_dtype, tk=GRAD_TK, tn=1024):
    T, M = a.shape
    N = b.shape[1]
    tn = min(tn, N)
    return _mm_tn(a, b, name, (N // tn, T // tk),
                  pl.BlockSpec((tk, M), lambda j, k: (k, 0)), pl.BlockSpec((tk, tn), lambda j, k: (k, j)),
                  pl.BlockSpec((M, tn), lambda j, k: (0, j)), jax.ShapeDtypeStruct((M, N), out_dtype), (M, tn))


def _grad_colstack(a, b, name, wcol, tk=GRAD_TK):
    T, M = a.shape
    S = b.shape[1] // wcol
    return _mm_tn(a, b, name, (S, T // tk),
                  pl.BlockSpec((tk, M), lambda s, k: (k, 0)), pl.BlockSpec((tk, wcol), lambda s, k: (k, s)),
                  pl.BlockSpec((None, M, wcol), lambda s, k: (s, 0, 0)),
                  jax.ShapeDtypeStruct((S, M, wcol), BF), (M, wcol))


def _grad_bstack(a, b, name, tk=GRAD_TK):
    T, M = a.shape
    S, _, n = b.shape
    return _mm_tn(a, b, name, (S, T // tk),
                  pl.BlockSpec((tk, M), lambda s, k: (k, 0)), pl.BlockSpec((None, tk, n), lambda s, k: (s, k, 0)),
                  pl.BlockSpec((None, M, n), lambda s, k: (s, 0, 0)),
                  jax.ShapeDtypeStruct((S, M, n), BF), (M, n))


def _grad_astack(a, b, name, tk=GRAD_TK):
    S, T, m = a.shape
    N = b.shape[1]
    return _mm_tn(a, b, name, (S, T // tk),
                  pl.BlockSpec((None, tk, m), lambda s, k: (s, k, 0)), pl.BlockSpec((tk, N), lambda s, k: (k, 0)),
                  pl.BlockSpec((None, m, N), lambda s, k: (s, 0, 0)),
                  jax.ShapeDtypeStruct((S, m, N), BF), (m, N))


def _rope_tables(T):
    half = 32
    pos = jnp.arange(T, dtype=F32)
    inv_freq = 1.0 / (10000.0 ** (jnp.arange(half, dtype=F32) / half))
    ang = pos[:, None] * inv_freq[None, :]
    cos, sin = jnp.cos(ang), jnp.sin(ang)
    z = jnp.zeros((T, 64), F32)
    return jnp.concatenate([cos, cos, z], axis=-1), jnp.concatenate([-sin, sin, z], axis=-1)


def _ret_consts():
    h = np.arange(RET_H, dtype=np.float32)
    log_g = np.log1p(-(np.float32(2.0) ** (-5.0 - h))).astype(np.float32)
    idx = np.arange(CHUNK, dtype=np.float32)
    diff = idx[:, None] - idx[None, :]
    decay = np.where(diff[None] >= 0, np.exp(np.maximum(diff, 0.0)[None] * log_g[:, None, None]), 0.0)
    zeta = np.exp((CHUNK - 1.0 - idx)[None, :] * log_g[:, None])
    xi = np.exp((idx + 1.0)[None, :] * log_g[:, None])
    gc = np.exp(CHUNK * log_g)
    bc = lambda v: np.broadcast_to(v[:, :, None], (RET_H, CHUNK, LANE)).astype(np.float32)
    gcb = np.broadcast_to(gc[:, None, None], (RET_H, CHUNK, LANE)).astype(np.float32)
    return (jnp.asarray(decay.astype(np.float32)), jnp.asarray(bc(zeta)), jnp.asarray(bc(xi)), jnp.asarray(gcb))


def _mix_prep(z_a, z_ff, cos_t, sin_t, b_f, g_q, g_k, tm=256):
    T = z_a.shape[0]

    def body(zqk_ref, zf_ref, zff_ref, cos_ref, sin_ref, b_ref, gq_ref, gk_ref,
             qr_ref, kr_ref, qf_ref, kf_ref, vf_ref, c_ref, qmax_ref, kmax_ref, carry):
        i = pl.program_id(0)

        @pl.when(i == 0)
        def _():
            carry[...] = jnp.zeros(carry.shape, F32)
            qmax_ref[...] = jnp.zeros(qmax_ref.shape, F32)
            kmax_ref[...] = jnp.zeros(kmax_ref.shape, F32)

        lane = lax.broadcasted_iota(jnp.int32, (tm, LANE), 1)
        lane1 = lax.broadcasted_iota(jnp.int32, (1, LANE), 1)
        zpad = jnp.zeros((tm, 64), F32)
        cosv, sinv = cos_ref[...], sin_ref[...]
        zqk = zqk_ref[...]
        for h in range(RET_H):
            for src, dst, scale in ((0, qr_ref, 1.0), (256, kr_ref, 0.125)):
                xh = jnp.concatenate([zqk[:, src + 64 * h: src + 64 * h + 64], zpad], axis=-1)
                rot = xh * cosv + _swap32(xh) * sinv
                dst[h] = (rot * scale).astype(BF)

        lf_in = zff_ref[...] + b_ref[...]
        logf = jnp.minimum(lf_in, 0.0) - jnp.log(1.0 + jnp.exp(-jnp.abs(lf_in)))
        row = lax.broadcasted_iota(jnp.int32, (tm, tm), 0)
        col = lax.broadcasted_iota(jnp.int32, (tm, tm), 1)
        tri = (row >= col).astype(BF)
        hi, mid, lo = _split3(logf)
        cs = _nn(tri, hi) + _nn(tri, mid) + _nn(tri, lo) + carry[...]
        carry[...] = cs[tm - 1:tm, :]
        c_ref[...] = cs

        zf = zf_ref[...]
        one = jnp.ones((tm, LANE), F32)
        qmax, kmax = qmax_ref[...], kmax_ref[...]
        for h in range(FOX_H):
            c = cs[:, h:h + 1]
            chi, cmid, clo = [t.astype(F32) for t in _split3(c)]
            qh = zf[:, 64 * h:64 * h + 64]
            kh = zf[:, 512 + 64 * h:512 + 64 * h + 64]
            vh = zf[:, 1024 + 64 * h:1024 + 64 * h + 64]
            qn = qh * lax.rsqrt(jnp.mean(qh * qh, axis=-1, keepdims=True) + EPS) * gq_ref[...] * 0.125
            kn = kh * lax.rsqrt(jnp.mean(kh * kh, axis=-1, keepdims=True) + EPS) * gk_ref[...]
            qa = jnp.concatenate([qn, zpad], axis=-1)
            qa = jnp.where(lane == L_CQ, chi, jnp.where(lane == L_CQ + 1, cmid, jnp.where(lane == L_CQ + 2, clo, qa)))
            qa = jnp.where((lane >= L_CK) & (lane < L_CK + 3), one, qa)
            ka = jnp.concatenate([kn, zpad], axis=-1)
            ka = jnp.where(lane == L_CK, -chi, jnp.where(lane == L_CK + 1, -cmid, jnp.where(lane == L_CK + 2, -clo, ka)))
            ka = jnp.where(((lane >= L_CQ) & (lane < L_CQ + 3)) | ((lane >= L_LSE) & (lane < L_MAX + 3)), one, ka)
            va = jnp.concatenate([vh, zpad], axis=-1)
            va = jnp.where((lane >= 64) & (lane < 67), one, va)
            qf_ref[h] = qa.astype(BF)
            kf_ref[h] = ka.astype(BF)
            vf_ref[h] = va.astype(BF)
            qmax = jnp.where(lane1 == h, jnp.maximum(qmax, jnp.max(jnp.sum(qn * qn, axis=-1, keepdims=True), axis=0,
                                                                   keepdims=True)), qmax)
            kmax = jnp.where(lane1 == h, jnp.maximum(kmax, jnp.max(jnp.sum(kn * kn, axis=-1, keepdims=True), axis=0,
                                                                   keepdims=True)), kmax)
        qmax_ref[...] = qmax
        kmax_ref[...] = kmax

    hspec4 = pl.BlockSpec((RET_H, tm, LANE), lambda i: (0, i, 0))
    hspec8 = pl.BlockSpec((FOX_H, tm, LANE), lambda i: (0, i, 0))
    small = lambda w: pl.BlockSpec((1, w), lambda i: (0, 0))
    return pl.pallas_call(
        body, name="mix_prep", grid=(T // tm,),
        in_specs=[pl.BlockSpec((tm, 512), lambda i: (i, 0)), pl.BlockSpec((tm, 1536), lambda i: (i, 1)),
                  pl.BlockSpec((tm, LANE), lambda i: (i, 0)), pl.BlockSpec((tm, LANE), lambda i: (i, 0)),
                  pl.BlockSpec((tm, LANE), lambda i: (i, 0)), small(LANE), small(64), small(64)],
        out_specs=[hspec4, hspec4, hspec8, hspec8, hspec8, pl.BlockSpec((tm, LANE), lambda i: (i, 0)),
                   small(LANE), small(LANE)],
        out_shape=[jax.ShapeDtypeStruct((RET_H, T, LANE), BF)] * 2 + [jax.ShapeDtypeStruct((FOX_H, T, LANE), BF)] * 3
        + [jax.ShapeDtypeStruct((T, LANE), F32), jax.ShapeDtypeStruct((1, LANE), F32), jax.ShapeDtypeStruct((1, LANE), F32)],
        scratch_shapes=[pltpu.VMEM((1, LANE), F32)],
        compiler_params=_params(("arbitrary",), VMEM_BIG),
    )(z_a, z_a, z_ff, cos_t, sin_t, b_f, g_q, g_k)


def _ret_fwd(qr, kr, z_a, g_ret, consts, tt=512):
    T = z_a.shape[0]
    nch = tt // CHUNK
    decay, zeta, xi, gcb = consts

    def body(q_ref, k_ref, v_ref, gt_ref, g_ref, d_ref, ze_ref, xi_ref, gc_ref, o_ref, u_ref, st_ref, r_sc):
        i = pl.program_id(0)

        @pl.when(i == 0)
        def _():
            r_sc[...] = jnp.zeros(r_sc.shape, F32)

        for c in range(nch):
            rows = slice(c * CHUNK, (c + 1) * CHUNK)
            for h in range(RET_H):
                cols = slice(h * RET_DV, (h + 1) * RET_DV)
                q, k = q_ref[h, rows, :], k_ref[h, rows, :]
                v32 = v_ref[rows, cols]
                r = r_sc[h]
                st_ref[h, rows, :] = r
                s = _nt(q, k) * d_ref[h]
                o = _nn(s.astype(BF), v32.astype(BF)) + _nn(q, r.astype(BF)) * xi_ref[h]
                r_sc[h] = gc_ref[h] * r + _tn(k, (v32 * ze_ref[h]).astype(BF))
                o_ref[rows, cols] = o
                mu = jnp.mean(o, axis=-1, keepdims=True)
                xc = o - mu
                on = xc * lax.rsqrt(jnp.mean(xc * xc, axis=-1, keepdims=True) + EPS)
                gt = gt_ref[rows, cols]
                u_ref[rows, cols] = (gt * _sigmoid(gt) * (on * g_ref[:, cols])).astype(BF)

    hspec = pl.BlockSpec((RET_H, tt, LANE), lambda i: (0, i, 0))
    cspec = pl.BlockSpec((RET_H, CHUNK, LANE), lambda i: (0, 0, 0))
    return pl.pallas_call(
        body, name="ret_fwd", grid=(T // tt,),
        in_specs=[hspec, hspec, pl.BlockSpec((tt, 512), lambda i: (i, 1)), pl.BlockSpec((tt, 512), lambda i: (i, 2)),
                  pl.BlockSpec((1, 512), lambda i: (0, 0)), cspec, cspec, cspec, cspec],
        out_specs=[pl.BlockSpec((tt, 512), lambda i: (i, 0)), pl.BlockSpec((tt, 512), lambda i: (i, 0)), hspec],
        out_shape=[jax.ShapeDtypeStruct((T, 512), F32), jax.ShapeDtypeStruct((T, 512), BF),
                   jax.ShapeDtypeStruct((RET_H, T, LANE), F32)],
        scratch_shapes=[pltpu.VMEM((RET_H, CHUNK, LANE), F32)],
        compiler_params=_params(("arbitrary",), VMEM_BIG),
    )(qr, kr, z_a, z_a, g_ret, decay, zeta, xi, gcb)


PRUNE_LOG = -110.0


def _prune_tables(c, qmax, kmax, sub):
    n = c.shape[0] // sub
    u = jnp.sqrt(qmax[0, :FOX_H] * kmax[0, :FOX_H]) * 1.02 + 0.5
    first = c[0::sub, :FOX_H].T
    last = c[sub - 1::sub, :FOX_H].T
    blk = jnp.arange(n, dtype=jnp.int32)
    needed = (2.0 * u[:, None, None] + first[:, :, None] - last[:, None, :] >= PRUNE_LOG) | (blk[None, :] >= blk[:, None])[None]
    jlo = jnp.argmax(needed, axis=2).astype(jnp.int32)
    jstart = jnp.minimum(jlo[:, 0::2], jlo[:, 1::2]) // 2
    need_q = jlo[:, None, :] <= (2 * jnp.arange(n // 2, dtype=jnp.int32) + 1)[None, :, None]
    iend = n - jnp.argmax(need_q[:, :, ::-1], axis=2).astype(jnp.int32)
    return jstart.astype(jnp.int32), iend.astype(jnp.int32)


def _fox_fwd(jstart, q, k, v, sub=512):
    H, T, _ = q.shape
    tb = 2 * sub

    def body(js_ref, q_ref, k_ref, v_ref, o_ref, q2_ref, mx_sc, acc_sc):
        i = pl.program_id(1)
        j0 = js_ref[pl.program_id(0), i]
        lane = lax.broadcasted_iota(jnp.int32, (sub, LANE), 1)
        row = lax.broadcasted_iota(jnp.int32, (sub, sub), 0)
        col = lax.broadcasted_iota(jnp.int32, (sub, sub), 1)
        causal = row >= col
        qs = [q_ref[0:sub, :], q_ref[sub:tb, :]]
        d0 = pl.multiple_of(i * tb, tb)
        d1 = pl.multiple_of(i * tb + sub, sub)

        def lane_max(s):
            m = s[:, 0:LANE]
            for c in range(1, s.shape[1] // LANE):
                m = jnp.maximum(m, s[:, c * LANE:(c + 1) * LANE])
            return m

        mx_sc[...] = jnp.full(mx_sc.shape, NEG, F32)

        def max_body(j, carry):
            kb = k_ref[pl.ds(pl.multiple_of(j * tb, tb), tb), :]
            for a in range(2):
                mx_sc[a] = jnp.maximum(mx_sc[a], lane_max(_nt(qs[a], kb)))
            return carry

        lax.fori_loop(j0, i, max_body, 0)
        k0, k1 = k_ref[pl.ds(d0, sub), :], k_ref[pl.ds(d1, sub), :]
        v0, v1 = v_ref[pl.ds(d0, sub), :], v_ref[pl.ds(d1, sub), :]
        mx = [jnp.maximum(mx_sc[0], lane_max(jnp.where(causal, _nt(qs[0], k0), NEG))),
              jnp.maximum(jnp.maximum(mx_sc[1], lane_max(_nt(qs[1], k0))),
                          lane_max(jnp.where(causal, _nt(qs[1], k1), NEG)))]
        ms = [jnp.max(t, axis=1, keepdims=True) for t in mx]

        def put3(base, first, val):
            hi, mid, lo = _split3(val)
            return jnp.where(lane == first, hi, jnp.where(lane == first + 1, mid, jnp.where(lane == first + 2, lo, base)))

        qm = [put3(qs[a], L_MAX, -ms[a]) for a in range(2)]

        acc_sc[...] = jnp.zeros(acc_sc.shape, F32)

        def acc_body(j, carry):
            off = pl.multiple_of(j * tb, tb)
            kb, vb = k_ref[pl.ds(off, tb), :], v_ref[pl.ds(off, tb), :]
            for a in range(2):
                acc_sc[a] += _nn(jnp.exp(_nt(qm[a], kb)).astype(BF), vb)
            return carry

        lax.fori_loop(j0, i, acc_body, 0)

        def pv(qa, kk, vv, masked):
            p = jnp.exp(_nt(qa, kk))
            if masked:
                p = jnp.where(causal, p, 0.0)
            return _nn(p.astype(BF), vv)

        accs = [acc_sc[0] + pv(qm[0], k0, v0, True),
                acc_sc[1] + pv(qm[1], k0, v0, False) + pv(qm[1], k1, v1, True)]
        for a in range(2):
            rows = slice(a * sub, (a + 1) * sub)
            l = accs[a][:, 64:65]
            o_ref[rows, :] = jnp.where(lane < 64, accs[a] / l, 0.0)
            q2_ref[rows, :] = put3(qs[a], L_LSE, -(ms[a] + jnp.log(l)))

    blk = pl.BlockSpec((None, tb, LANE), lambda h, i, js: (h, i, 0))
    full = pl.BlockSpec((None, T, LANE), lambda h, i, js: (h, 0, 0))
    return pl.pallas_call(
        body, name="fox_fwd",
        grid_spec=pltpu.PrefetchScalarGridSpec(
            num_scalar_prefetch=1, grid=(H, T // tb), in_specs=[blk, full, full], out_specs=[blk, blk],
            scratch_shapes=[pltpu.VMEM((2, sub, LANE), F32), pltpu.VMEM((2, sub, LANE), F32)]),
        out_shape=[jax.ShapeDtypeStruct((H, T, LANE), F32), jax.ShapeDtypeStruct((H, T, LANE), BF)],
        compiler_params=_params(("parallel", "arbitrary"), VMEM_BIG),
    )(jstart, q, k, v)


def _merge_out(u_r, o_fox, z_a, x, g_ffn, w_ro, w_fo, w_out, tm=256):
    T = x.shape[0]

    def body(u_ref, of_ref, ar_ref, af_ref, x_ref, g_ref, wro_ref, wfo_ref, wout_ref,
             yr_ref, yf_ref, m_ref, x2_ref, h2_ref, oc_ref):
        u = u_ref[...]
        oc = jnp.concatenate([of_ref[h][:, :FOX_D] for h in range(FOX_H)], axis=-1).astype(BF)
        oc_ref[...] = oc
        yr = jnp.concatenate([_nn(u, wro_ref[k]) for k in range(N_CHIP)], axis=-1)
        yf = jnp.concatenate([_nn(oc, wfo_ref[k]) for k in range(N_CHIP)], axis=-1)
        yr_ref[...] = yr
        yf_ref[...] = yf
        m = (_sigmoid(ar_ref[...]) * yr + _sigmoid(af_ref[...]) * yf).astype(BF)
        m_ref[...] = m
        x2 = x_ref[...]
        for k in range(N_CHIP):
            x2 = x2 + _nn(m[:, 256 * k:256 * k + 256], wout_ref[k])
        x2_ref[...] = x2
        r = lax.rsqrt(jnp.mean(x2 * x2, axis=-1, keepdims=True) + EPS)
        h2_ref[...] = (x2 * r * g_ref[...]).astype(BF)

    row = lambda w: pl.BlockSpec((tm, w), lambda i: (i, 0))
    const = lambda shp: pl.BlockSpec(shp, lambda i: (0,) * len(shp))
    return pl.pallas_call(
        body, name="merge_out", grid=(T // tm,),
        in_specs=[row(512), pl.BlockSpec((FOX_H, tm, LANE), lambda i: (0, i, 0)),
                  pl.BlockSpec((tm, 1024), lambda i: (i, 3)), pl.BlockSpec((tm, 1024), lambda i: (i, 4)),
                  row(1024), const((1, 1024)), const((N_CHIP, 512, 256)), const((N_CHIP, 512, 256)),
                  const((N_CHIP, 256, 1024))],
        out_specs=[row(1024), row(1024), row(1024), row(1024), row(1024), row(512)],
        out_shape=[jax.ShapeDtypeStruct((T, 1024), F32), jax.ShapeDtypeStruct((T, 1024), F32),
                   jax.ShapeDtypeStruct((T, 1024), BF), jax.ShapeDtypeStruct((T, 1024), F32),
                   jax.ShapeDtypeStruct((T, 1024), BF), jax.ShapeDtypeStruct((T, 512), BF)],
        compiler_params=_params(("parallel",), VMEM_BIG),
    )(u_r, o_fox, z_a, z_a, x, g_ffn, w_ro, w_fo, w_out)


def _load_resident(hbm_refs, vmem_refs, sem):
    cps = [pltpu.make_async_copy(h, v, sem.at[i]) for i, (h, v) in enumerate(zip(hbm_refs, vmem_refs))]
    for cp in cps:
        cp.start()
    for cp in cps:
        cp.wait()


def _ffn_fwd(h2, x2, tgt, w_gate, w_up, w_down, tm=FFN_TM):
    T = h2.shape[0]

    def body(h_ref, x2_ref, t_ref, wg_hbm, wu_hbm, wd_hbm, a_ref, b_ref, act_ref, dy_ref, ls_ref, wg, wu, wd, sem):
        @pl.when(pl.program_id(0) == 0)
        def _():
            _load_resident((wg_hbm, wu_hbm, wd_hbm), (wg, wu, wd), sem)
            ls_ref[...] = jnp.zeros(ls_ref.shape, F32)

        h = h_ref[...]
        err = x2_ref[...] - t_ref[...]
        for k in range(N_CHIP):
            gp = _nt(h, wg[k])
            up = _nt(h, wu[k])
            sg = _sigmoid(gp)
            silu = gp * sg
            a_ref[k] = silu.astype(BF)
            b_ref[k] = (up * sg * (1.0 + gp * (1.0 - sg))).astype(BF)
            act = (silu * up).astype(BF)
            act_ref[k] = act
            err = err + _nn(act, wd[k])
        dy_ref[...] = err * (1.0 / D_MODEL)
        ls_ref[...] += jnp.sum(err * err, axis=0, keepdims=True)

    row = pl.BlockSpec((tm, D_MODEL), lambda i: (i, 0))
    hid = pl.BlockSpec((N_CHIP, tm, FF_SH), lambda i: (0, i, 0))
    anyspec = pl.BlockSpec(memory_space=pl.ANY)
    wshape = pltpu.VMEM((N_CHIP, FF_SH, D_MODEL), BF)
    return pl.pallas_call(
        body, name="ffn_fwd", grid=(T // tm,),
        in_specs=[row, row, row, anyspec, anyspec, anyspec],
        out_specs=[hid, hid, hid, row, pl.BlockSpec((1, D_MODEL), lambda i: (0, 0))],
        out_shape=[jax.ShapeDtypeStruct((N_CHIP, T, FF_SH), BF)] * 3
        + [jax.ShapeDtypeStruct((T, D_MODEL), F32), jax.ShapeDtypeStruct((1, D_MODEL), F32)],
        scratch_shapes=[wshape, wshape, wshape, pltpu.SemaphoreType.DMA((3,))],
        compiler_params=_params(("arbitrary",), VMEM_HUGE),
    )(h2, x2, tgt, w_gate, w_up, w_down)


def _ffn_bwd(dy, sa, sb, x2, g_ffn, w_gate, w_up, w_down, tm=FFN_TM):
    T = dy.shape[0]

    def body(dy_ref, a_ref, b_ref, x2_ref, g_ref, wg_hbm, wu_hbm, wd_hbm, dgp_ref, dup_ref, dx_ref, dg_ref,
             wg, wu, wd, sem):
        @pl.when(pl.program_id(0) == 0)
        def _():
            _load_resident((wg_hbm, wu_hbm, wd_hbm), (wg, wu, wd), sem)
            dg_ref[...] = jnp.zeros(dg_ref.shape, F32)

        dy = dy_ref[...]
        dyb = dy.astype(BF)
        dh = jnp.zeros((tm, D_MODEL), F32)
        for k in range(N_CHIP):
            dact = _nt(dyb, wd[k])
            dup = (dact * a_ref[k]).astype(BF)
            dgp = (dact * b_ref[k]).astype(BF)
            dgp_ref[k] = dgp
            dup_ref[k] = dup
            dh = dh + _nn(dgp, wg[k]) + _nn(dup, wu[k])
        x2 = x2_ref[...]
        r = lax.rsqrt(jnp.mean(x2 * x2, axis=-1, keepdims=True) + EPS)
        xn = x2 * r
        dg_ref[...] += jnp.sum(dh * xn, axis=0, keepdims=True)
        dxn = dh * g_ref[...]
        dx_ref[...] = dy + r * (dxn - xn * jnp.mean(dxn * xn, axis=-1, keepdims=True))

    row = pl.BlockSpec((tm, D_MODEL), lambda i: (i, 0))
    hid = pl.BlockSpec((N_CHIP, tm, FF_SH), lambda i: (0, i, 0))
    vec = pl.BlockSpec((1, D_MODEL), lambda i: (0, 0))
    anyspec = pl.BlockSpec(memory_space=pl.ANY)
    wshape = pltpu.VMEM((N_CHIP, FF_SH, D_MODEL), BF)
    return pl.pallas_call(
        body, name="ffn_bwd", grid=(T // tm,),
        in_specs=[row, hid, hid, row, vec, anyspec, anyspec, anyspec],
        out_specs=[hid, hid, row, vec],
        out_shape=[jax.ShapeDtypeStruct((N_CHIP, T, FF_SH), BF), jax.ShapeDtypeStruct((N_CHIP, T, FF_SH), BF),
                   jax.ShapeDtypeStruct((T, D_MODEL), F32), jax.ShapeDtypeStruct((1, D_MODEL), F32)],
        scratch_shapes=[wshape, wshape, wshape, pltpu.SemaphoreType.DMA((3,))],
        compiler_params=_params(("arbitrary",), VMEM_HUGE),
    )(dy, sa, sb, x2, g_ffn, w_gate, w_up, w_down)


def _out_bwd(dx2, z_a, y_r, y_f, o_raw, o_fox, g_ret, w_ro, w_fo, w_out, tm=256):
    T = dx2.shape[0]

    def body(dx_ref, gt_ref, ar_ref, af_ref, yr_ref, yf_ref, o_ref, of_ref, g_ref, wro_ref, wfo_ref, wout_ref,
             dyr_ref, dyf_ref, dgt_ref, da_ref, do_ref, dof_ref, dg_ref):
        i = pl.program_id(0)

        @pl.when(i == 0)
        def _():
            dg_ref[...] = jnp.zeros(dg_ref.shape, F32)

        dxb = dx_ref[...].astype(BF)
        dm = jnp.concatenate([_nt(dxb, wout_ref[k]) for k in range(N_CHIP)], axis=-1)
        sr, sf = _sigmoid(ar_ref[...]), _sigmoid(af_ref[...])
        dyr = dm * sr
        dyf = dm * sf
        da_ref[:, :1024] = (dyr * yr_ref[...] * (1.0 - sr)).astype(BF)
        da_ref[:, 1024:] = (dyf * yf_ref[...] * (1.0 - sf)).astype(BF)
        dyr = dyr.astype(BF)
        dyf = dyf.astype(BF)
        dyr_ref[...] = dyr
        dyf_ref[...] = dyf
        du = jnp.zeros((tm, 512), F32)
        doc = jnp.zeros((tm, 512), F32)
        for k in range(N_CHIP):
            du = du + _nt(dyr[:, 256 * k:256 * k + 256], wro_ref[k])
            doc = doc + _nt(dyf[:, 256 * k:256 * k + 256], wfo_ref[k])

        for h in range(RET_H):
            cols = slice(h * RET_DV, (h + 1) * RET_DV)
            o = o_ref[:, cols]
            mu = jnp.mean(o, axis=-1, keepdims=True)
            xc = o - mu
            rstd = lax.rsqrt(jnp.mean(xc * xc, axis=-1, keepdims=True) + EPS)
            on = xc * rstd
            g = g_ref[:, cols]
            gt = gt_ref[:, cols]
            sg = _sigmoid(gt)
            duh = du[:, cols]
            dgt_ref[:, cols] = (duh * (on * g) * sg * (1.0 + gt * (1.0 - sg))).astype(BF)
            dog = duh * gt * sg
            dg_ref[:, cols] += jnp.sum(dog * on, axis=0, keepdims=True)
            don = dog * g
            do_ref[:, cols] = rstd * (don - jnp.mean(don, axis=-1, keepdims=True)
                                      - on * jnp.mean(don * on, axis=-1, keepdims=True))

        lane = lax.broadcasted_iota(jnp.int32, (tm, LANE), 1)
        zpad = jnp.zeros((tm, 64), F32)
        for h in range(FOX_H):
            doh = doc[:, 64 * h:64 * h + 64]
            delta = jnp.sum(doh * of_ref[h][:, :FOX_D], axis=-1, keepdims=True)
            hi, mid, lo = [t.astype(F32) for t in _split3(-delta)]
            da = jnp.concatenate([doh, zpad], axis=-1)
            da = jnp.where(lane == 64, hi, jnp.where(lane == 65, mid, jnp.where(lane == 66, lo, da)))
            dof_ref[h] = da.astype(BF)

    row = lambda w: pl.BlockSpec((tm, w), lambda i: (i, 0))
    const = lambda shp: pl.BlockSpec(shp, lambda i: (0,) * len(shp))
    hsp = pl.BlockSpec((FOX_H, tm, LANE), lambda i: (0, i, 0))
    return pl.pallas_call(
        body, name="out_bwd", grid=(T // tm,),
        in_specs=[row(1024), pl.BlockSpec((tm, 512), lambda i: (i, 2)), pl.BlockSpec((tm, 1024), lambda i: (i, 3)),
                  pl.BlockSpec((tm, 1024), lambda i: (i, 4)), row(1024), row(1024), row(512), hsp,
                  const((1, 512)), const((N_CHIP, 512, 256)), const((N_CHIP, 512, 256)), const((N_CHIP, 256, 1024))],
        out_specs=[row(1024), row(1024), row(512), row(2048), row(512), hsp, const((1, 512))],
        out_shape=[jax.ShapeDtypeStruct((T, 1024), BF), jax.ShapeDtypeStruct((T, 1024), BF),
                   jax.ShapeDtypeStruct((T, 512), BF), jax.ShapeDtypeStruct((T, 2048), BF),
                   jax.ShapeDtypeStruct((T, 512), F32), jax.ShapeDtypeStruct((FOX_H, T, LANE), BF),
                   jax.ShapeDtypeStruct((1, 512), F32)],
        compiler_params=_params(("arbitrary",), VMEM_BIG),
    )(dx2, z_a, z_a, z_a, y_r, y_f, o_raw, o_fox, g_ret, w_ro, w_fo, w_out)


def _ret_bwd(d_o, qr, kr, z_a, states, cos_t, sin_t, consts, tt=512):
    T = z_a.shape[0]
    nt = T // tt
    nch = tt // CHUNK
    decay, zeta, xi, gcb = consts

    def body(do_ref, q_ref, k_ref, v_ref, st_ref, cos_ref, sin_ref, d_ref, ze_ref, xi_ref, gc_ref, dz_ref, g_sc):
        i = pl.program_id(0)

        @pl.when(i == 0)
        def _():
            g_sc[...] = jnp.zeros(g_sc.shape, F32)

        for c in reversed(range(nch)):
            rows = slice(c * CHUNK, (c + 1) * CHUNK)
            cosv, sinv = cos_ref[rows, :], sin_ref[rows, :]
            dq_parts, dk_parts = [], []
            for h in range(RET_H):
                cols = slice(h * RET_DV, (h + 1) * RET_DV)
                q, k = q_ref[h, rows, :], k_ref[h, rows, :]
                v32 = v_ref[rows, cols]
                vb = v32.astype(BF)
                r = st_ref[h, rows, :]
                g = g_sc[h]
                gb = g.astype(BF)
                d_o = do_ref[rows, cols]
                dob = d_o.astype(BF)
                dox = (d_o * xi_ref[h]).astype(BF)
                dec = d_ref[h]
                s = (_nt(q, k) * dec).astype(BF)
                ds = (_nt(dob, vb) * dec).astype(BF)
                dv = _tn(s, dob) + ze_ref[h] * _nn(k, gb)
                dq = _nn(ds, k) + _nt(dox, r.astype(BF))
                dk = _tn(ds, q) + _nt((v32 * ze_ref[h]).astype(BF), gb)
                g_sc[h] = gc_ref[h] * g + _tn(q, dox)
                dq_parts.append((dq * cosv - _swap32(dq) * sinv)[:, :64])
                dk_parts.append(((dk * cosv - _swap32(dk) * sinv) * 0.125)[:, :64])
                dz_ref[rows, 512 + h * RET_DV:512 + (h + 1) * RET_DV] = dv.astype(BF)
            dz_ref[rows, 0:256] = jnp.concatenate(dq_parts, axis=-1).astype(BF)
            dz_ref[rows, 256:512] = jnp.concatenate(dk_parts, axis=-1).astype(BF)

    rev = lambda i: nt - 1 - i
    hspec = pl.BlockSpec((RET_H, tt, LANE), lambda i: (0, rev(i), 0))
    cspec = pl.BlockSpec((RET_H, CHUNK, LANE), lambda i: (0, 0, 0))
    tab = pl.BlockSpec((tt, LANE), lambda i: (rev(i), 0))
    return pl.pallas_call(
        body, name="ret_bwd", grid=(nt,),
        in_specs=[pl.BlockSpec((tt, 512), lambda i: (rev(i), 0)), hspec, hspec,
                  pl.BlockSpec((tt, 512), lambda i: (rev(i), 1)), hspec, tab, tab, cspec, cspec, cspec, cspec],
        out_specs=pl.BlockSpec((tt, 1024), lambda i: (rev(i), 0)),
        out_shape=jax.ShapeDtypeStruct((T, 1024), BF),
        scratch_shapes=[pltpu.VMEM((RET_H, CHUNK, LANE), F32)],
        compiler_params=_params(("arbitrary",), VMEM_BIG),
    )(d_o, qr, kr, z_a, states, cos_t, sin_t, decay, zeta, xi, gcb)


def _fox_bwd(iend, q2, k, v, do, sub=512):
    H, T, _ = k.shape
    tb = 2 * sub

    def body(ie_ref, q_ref, do_ref, k_ref, v_ref, dq_ref, dk_ref, dv_ref, dk_sc, dv_sc):
        j = pl.program_id(1)
        n = ie_ref[pl.program_id(0), j]

        @pl.when(j == 0)
        def _():
            dq_ref[...] = jnp.zeros(dq_ref.shape, F32)

        kk, vv = k_ref[...], v_ref[...]
        dk_sc[...] = jnp.zeros(dk_sc.shape, F32)
        dv_sc[...] = jnp.zeros(dv_sc.shape, F32)
        krow = lax.broadcasted_iota(jnp.int32, (tb, sub), 0)
        qcol = lax.broadcasted_iota(jnp.int32, (tb, sub), 1)

        def step(i, shift):
            off = pl.multiple_of(i * sub, sub)
            qq = q_ref[pl.ds(off, sub), :]
            dd = do_ref[pl.ds(off, sub), :]
            p = jnp.exp(_nt(kk, qq))
            if shift is not None:
                p = jnp.where(qcol + shift >= krow, p, 0.0)
            ds = (p * _nt(vv, dd)).astype(BF)
            dv_sc[...] += _nn(p.astype(BF), dd)
            dk_sc[...] += _nn(ds, qq)
            dq_ref[pl.ds(off, sub), :] += _tn(ds, kk)

        off0 = pl.multiple_of(2 * j * sub, sub)
        q0, d0 = q_ref[pl.ds(off0, sub), :], do_ref[pl.ds(off0, sub), :]
        k0, v0 = k_ref[0:sub, :], v_ref[0:sub, :]
        p0 = jnp.where(qcol[0:sub, :] >= krow[0:sub, :], jnp.exp(_nt(k0, q0)), 0.0)
        ds0 = (p0 * _nt(v0, d0)).astype(BF)
        dv_sc[0:sub, :] += _nn(p0.astype(BF), d0)
        dk_sc[0:sub, :] += _nn(ds0, q0)
        dq_ref[pl.ds(off0, sub), :] += _tn(ds0, k0)
        step(2 * j + 1, sub)

        def loop_body(i, carry):
            step(i, None)
            return carry

        lax.fori_loop(2 * j + 2, n, loop_body, 0)
        dk_ref[...] = dk_sc[...]
        dv_ref[...] = dv_sc[...]

    blk = pl.BlockSpec((None, tb, LANE), lambda h, j, ie: (h, j, 0))
    full = pl.BlockSpec((None, T, LANE), lambda h, j, ie: (h, 0, 0))
    shp = jax.ShapeDtypeStruct((H, T, LANE), F32)
    return pl.pallas_call(
        body, name="fox_bwd",
        grid_spec=pltpu.PrefetchScalarGridSpec(
            num_scalar_prefetch=1, grid=(H, T // tb), in_specs=[full, full, blk, blk], out_specs=[full, blk, blk],
            scratch_shapes=[pltpu.VMEM((tb, LANE), F32), pltpu.VMEM((tb, LANE), F32)]),
        out_shape=[shp, shp, shp],
        compiler_params=_params(("arbitrary", "arbitrary"), VMEM_BIG),
    )(iend, q2, do, k, v)


def _fox_post_bwd(dq, dk, dv, z_a, z_ff, b_f, g_q, g_k, tm=256):
    T = z_a.shape[0]
    nt = T // tm

    def body(dq_ref, dk_ref, dv_ref, zf_ref, zff_ref, b_ref, gq_ref, gk_ref,
             dz_ref, dff_ref, dgq_ref, dgk_ref, db_ref, carry):
        i = pl.program_id(0)

        @pl.when(i == 0)
        def _():
            carry[...] = jnp.zeros(carry.shape, F32)
            dgq_ref[...] = jnp.zeros(dgq_ref.shape, F32)
            dgk_ref[...] = jnp.zeros(dgk_ref.shape, F32)
            db_ref[...] = jnp.zeros(db_ref.shape, F32)

        lane = lax.broadcasted_iota(jnp.int32, (tm, LANE), 1)
        zf = zf_ref[...]
        dcm = jnp.zeros((tm, LANE), F32)
        dq_parts, dk_parts, dv_parts = [], [], []
        gq_acc = jnp.zeros((1, 64), F32)
        gk_acc = jnp.zeros((1, 64), F32)
        for h in range(FOX_H):
            dqa, dka = dq_ref[h], dk_ref[h]
            dcm = jnp.where(lane == h, dqa[:, L_CQ:L_CQ + 1] - dka[:, L_CK:L_CK + 1], dcm)
            for src, dya, g_ref, scale, parts in ((0, dqa, gq_ref, 0.125, dq_parts), (512, dka, gk_ref, 1.0, dk_parts)):
                xh = zf[:, src + 64 * h:src + 64 * h + 64]
                r = lax.rsqrt(jnp.mean(xh * xh, axis=-1, keepdims=True) + EPS)
                xn = xh * r
                dy = dya[:, :FOX_D] * scale
                if src == 0:
                    gq_acc = gq_acc + jnp.sum(dy * xn, axis=0, keepdims=True)
                else:
                    gk_acc = gk_acc + jnp.sum(dy * xn, axis=0, keepdims=True)
                dxn = dy * g_ref[...]
                parts.append(r * (dxn - xn * jnp.mean(dxn * xn, axis=-1, keepdims=True)))
            dv_parts.append(dv_ref[h][:, :FOX_D])
        dz_ref[...] = jnp.concatenate(dq_parts + dk_parts + dv_parts, axis=-1).astype(BF)
        zpad = jnp.zeros((1, 64), F32)
        dgq_ref[...] += jnp.concatenate([gq_acc, zpad], axis=-1)
        dgk_ref[...] += jnp.concatenate([gk_acc, zpad], axis=-1)

        row = lax.broadcasted_iota(jnp.int32, (tm, tm), 0)
        col = lax.broadcasted_iota(jnp.int32, (tm, tm), 1)
        tri = (row <= col).astype(BF)
        hi, mid, lo = _split3(dcm)
        dlogf = _nn(tri, hi) + _nn(tri, mid) + _nn(tri, lo) + carry[...]
        carry[...] = dlogf[0:1, :]
        dff = jnp.where(lane < FOX_H, dlogf * _sigmoid(-(zff_ref[...] + b_ref[...])), 0.0)
        dff_ref[...] = dff.astype(BF)
        db_ref[...] += jnp.sum(dff, axis=0, keepdims=True)

    rev = lambda i: nt - 1 - i
    hsp = pl.BlockSpec((FOX_H, tm, LANE), lambda i: (0, rev(i), 0))
    small = lambda w: pl.BlockSpec((1, w), lambda i: (0, 0))
    return pl.pallas_call(
        body, name="fox_post_bwd", grid=(nt,),
        in_specs=[hsp, hsp, hsp, pl.BlockSpec((tm, 1536), lambda i: (rev(i), 1)),
                  pl.BlockSpec((tm, LANE), lambda i: (rev(i), 0)), small(LANE), small(64), small(64)],
        out_specs=[pl.BlockSpec((tm, 1536), lambda i: (rev(i), 0)), pl.BlockSpec((tm, LANE), lambda i: (rev(i), 0)),
                   small(LANE), small(LANE), small(LANE)],
        out_shape=[jax.ShapeDtypeStruct((T, 1536), BF), jax.ShapeDtypeStruct((T, LANE), BF),
                   jax.ShapeDtypeStruct((1, LANE), F32), jax.ShapeDtypeStruct((1, LANE), F32),
                   jax.ShapeDtypeStruct((1, LANE), F32)],
        scratch_shapes=[pltpu.VMEM((1, LANE), F32)],
        compiler_params=_params(("arbitrary",), VMEM_BIG),
    )(dq, dk, dv, z_a, z_ff, b_f, g_q, g_k)


def _in_bwd(dz_ret, dz_gt, dz_fox, dz_a, dz_ff, w_a, w_ff, x, g_mix, dx2, tm=256):
    T = x.shape[0]

    def body(r_ref, t_ref, f_ref, a_ref, ff_ref, wa_ref, wf_ref, x_ref, g_ref, dx2_ref, dx_ref, dg_ref):
        i = pl.program_id(0)

        @pl.when(i == 0)
        def _():
            dg_ref[...] = jnp.zeros(dg_ref.shape, F32)

        dh = (_nt(r_ref[...], wa_ref[:, C_RET:C_GT]) + _nt(t_ref[...], wa_ref[:, C_GT:C_FOX])
              + _nt(f_ref[...], wa_ref[:, C_FOX:C_A]) + _nt(a_ref[...], wa_ref[:, C_A:C_END])
              + _nt(ff_ref[...], wf_ref[...]))
        xv = x_ref[...]
        r = lax.rsqrt(jnp.mean(xv * xv, axis=-1, keepdims=True) + EPS)
        xn = xv * r
        dg_ref[...] += jnp.sum(dh * xn, axis=0, keepdims=True)
        dxn = dh * g_ref[...]
        dx_ref[...] = dx2_ref[...] + r * (dxn - xn * jnp.mean(dxn * xn, axis=-1, keepdims=True))

    row = lambda w: pl.BlockSpec((tm, w), lambda i: (i, 0))
    const = lambda shp: pl.BlockSpec(shp, lambda i: (0,) * len(shp))
    return pl.pallas_call(
        body, name="in_bwd", grid=(T // tm,),
        in_specs=[row(1024), row(512), row(1536), row(2048), row(LANE), const((D_MODEL, C_END)),
                  const((D_MODEL, LANE)), row(1024), const((1, 1024)), row(1024)],
        out_specs=[row(1024), const((1, 1024))],
        out_shape=[jax.ShapeDtypeStruct((T, 1024), F32), jax.ShapeDtypeStruct((1, 1024), F32)],
        compiler_params=_params(("arbitrary",), VMEM_BIG),
    )(dz_ret, dz_gt, dz_fox, dz_a, dz_ff, w_a, w_ff, x, g_mix, dx2)


def _mesh_pos():
    return lax.axis_index("x"), lax.axis_index("y"), lax.axis_index("c")


def _staged_place(src, name):
    stacked = src.ndim == 3
    R, C = src.shape[-2:]
    tr = _row_tile(R, 128, 16)
    n = R // tr
    assert n >= 2

    def body(s_ref, o_ref, buf, sem):
        i = pl.program_id(0)
        slot = i % 2
        x, y, _ = _mesh_pos()
        kme = 2 * x + y

        def out_copy(s, step):
            return pltpu.make_async_copy(buf.at[s], o_ref.at[kme, pl.ds(pl.multiple_of(step * tr, tr), tr), :], sem.at[s])

        @pl.when(i >= 2)
        def _():
            out_copy(slot, i - 2).wait()

        buf[slot] = (s_ref[kme] if stacked else s_ref[...]).astype(BF)
        out_copy(slot, i).start()

        @pl.when(i == n - 1)
        def _():
            out_copy(1 - slot, i - 1).wait()
            out_copy(slot, i).wait()

    in_spec = (pl.BlockSpec((N_CHIP, tr, C), lambda i: (0, i, 0)) if stacked else pl.BlockSpec((tr, C), lambda i: (i, 0)))
    return pl.pallas_call(
        body, name=name, grid=(n,), in_specs=[in_spec], out_specs=pl.BlockSpec(memory_space=pl.ANY),
        out_shape=jax.ShapeDtypeStruct((N_CHIP, R, C), BF),
        scratch_shapes=[pltpu.VMEM((2, tr, C), BF), pltpu.SemaphoreType.DMA((2,))],
        compiler_params=_params(("arbitrary",)),
    )(src)


def _push_copies(src, land, send_sem, recv_sem, receiving):
    x, y, c = _mesh_pos()
    kme = 2 * x + y
    cps = []
    for w in range(len(land)):
        for j, (px, py) in enumerate([(1 - x, y), (x, 1 - y), (1 - x, 1 - y)]):
            kpeer = 2 * px + py
            cps.append(pltpu.make_async_remote_copy(
                src_ref=land[w].at[kme] if src is None else src[w].at[kpeer],
                dst_ref=land[w].at[kpeer if receiving else kme],
                send_sem=send_sem.at[3 * w + j], recv_sem=recv_sem.at[3 * w + j],
                device_id=(px, py, c), device_id_type=MESH))
    return cps


def _gather_in_place(stacks, name):
    n = len(stacks)

    def body(*refs):
        land, send_sem, recv_sem = refs[n:2 * n], refs[2 * n], refs[2 * n + 1]
        for cp in _push_copies(None, land, send_sem, recv_sem, False):
            cp.start()
        for cp in _push_copies(None, land, send_sem, recv_sem, True):
            cp.wait_recv()
            cp.wait_send()

    anyspec = pl.BlockSpec(memory_space=pl.ANY)
    return pl.pallas_call(
        body, name=name, in_specs=[anyspec] * n, out_specs=[anyspec] * n,
        out_shape=[jax.ShapeDtypeStruct(s.shape, s.dtype) for s in stacks],
        input_output_aliases={i: i for i in range(n)},
        scratch_shapes=[pltpu.SemaphoreType.DMA((3 * n,)), pltpu.SemaphoreType.DMA((3 * n,))],
    )(*stacks)


def _scatter_partials(srcs, lands, small):
    n = len(srcs)

    def body(*refs):
        src, sv = refs[:n], refs[2 * n]
        land, svo = refs[2 * n + 1:3 * n + 1], refs[3 * n + 1]
        send_sem, recv_sem, ssend, srecv, sloc = refs[3 * n + 2:]
        x, y, c = _mesh_pos()
        me = 4 * x + 2 * y + c
        flips = [(b >> 2 & 1, b >> 1 & 1, b & 1) for b in range(1, 8)]
        others = [(1 - x if fx else x, 1 - y if fy else y, 1 - c if fc else c) for fx, fy, fc in flips]
        local = pltpu.make_async_copy(sv, svo.at[me], sloc)
        local.start()
        sends = []
        for j, (px, py, pc) in enumerate(others):
            cp = pltpu.make_async_remote_copy(
                src_ref=sv, dst_ref=svo.at[me], send_sem=ssend.at[j], recv_sem=srecv.at[j],
                device_id=(px, py, pc), device_id_type=MESH)
            cp.start()
            sends.append(cp)
        for cp in _push_copies(src, land, send_sem, recv_sem, False):
            cp.start()
            sends.append(cp)
        for j, (px, py, pc) in enumerate(others):
            pltpu.make_async_remote_copy(
                src_ref=sv, dst_ref=svo.at[4 * px + 2 * py + pc], send_sem=ssend.at[j], recv_sem=srecv.at[j],
                device_id=(px, py, pc), device_id_type=MESH).wait_recv()
        for cp in _push_copies(src, land, send_sem, recv_sem, True):
            cp.wait_recv()
        for cp in sends:
            cp.wait_send()
        local.wait()

    anyspec = pl.BlockSpec(memory_space=pl.ANY)
    return pl.pallas_call(
        body, name="scatter_partials",
        in_specs=[anyspec] * (2 * n + 1), out_specs=[anyspec] * (n + 1),
        out_shape=[jax.ShapeDtypeStruct(s.shape, s.dtype) for s in lands]
        + [jax.ShapeDtypeStruct((8,) + small.shape, small.dtype)],
        input_output_aliases={n + i: i for i in range(n)},
        scratch_shapes=[pltpu.SemaphoreType.DMA((3 * n,)), pltpu.SemaphoreType.DMA((3 * n,)),
                        pltpu.SemaphoreType.DMA((7,)), pltpu.SemaphoreType.DMA((7,)), pltpu.SemaphoreType.DMA],
    )(*srcs, *lands, small)


_HBM_SPEC = pl.BlockSpec(memory_space=pltpu.HBM)
_SEM_SPEC = pl.BlockSpec(memory_space=pltpu.SEMAPHORE)
_SPLIT_PARAMS = pltpu.CompilerParams(has_side_effects=pltpu.SideEffectType.DATAFLOW_SIDE_EFFECTING)


def _push_start(srcs, lands, after, name):
    n = len(lands)
    ns = 0 if srcs is None else n

    def body(*refs):
        src = None if srcs is None else refs[:n]
        land = refs[ns:ns + n]
        send_sem, recv_sem = refs[ns + n + 1], refs[ns + n + 2]
        for cp in _push_copies(src, land, send_sem, recv_sem, False):
            cp.start()
        refs[-1][...] = jnp.zeros(refs[-1].shape, F32)

    ops = [pltpu.with_memory_space_constraint(a, pltpu.HBM) for a in ([] if srcs is None else list(srcs)) + list(lands)]
    res = pl.pallas_call(
        body, name=name,
        out_shape=(pltpu.SemaphoreType.DMA((3 * n,)), pltpu.SemaphoreType.DMA((3 * n,)),
                   *[pltpu.HBM(a.shape, a.dtype) for a in ops], jax.ShapeDtypeStruct((8, LANE), F32)),
        in_specs=[_HBM_SPEC] * len(ops) + [pl.BlockSpec(memory_space=pl.ANY)],
        out_specs=(_SEM_SPEC, _SEM_SPEC, *([_HBM_SPEC] * len(ops)), pl.BlockSpec(memory_space=pltpu.VMEM)),
        input_output_aliases={i: 2 + i for i in range(len(ops))},
        compiler_params=_SPLIT_PARAMS,
    )(*ops, after)
    return res[0], res[1], list(res[2:2 + len(ops)]), res[-1]


def _push_wait(send_sem, recv_sem, bufs, after, name, has_src):
    n = len(bufs) // 2 if has_src else len(bufs)
    ns = n if has_src else 0

    def body(*refs):
        src = refs[:n] if has_src else None
        land = refs[ns:ns + n]
        for cp in _push_copies(src, land, refs[ns + n], refs[ns + n + 1], True):
            cp.wait_send()
            cp.wait_recv()

    res = pl.pallas_call(
        body, name=name,
        out_shape=tuple(pltpu.HBM(a.shape, a.dtype) for a in bufs),
        in_specs=[_HBM_SPEC] * len(bufs) + [_SEM_SPEC, _SEM_SPEC, pl.BlockSpec(memory_space=pl.ANY)],
        out_specs=tuple([_HBM_SPEC] * len(bufs)),
        input_output_aliases={i: i for i in range(len(bufs))},
        compiler_params=_SPLIT_PARAMS,
    )(*bufs, send_sem, recv_sem, after)
    return list(res[ns:ns + n])


def _sibling_exchange(arrs):
    n = len(arrs)

    def body(*refs):
        ins, outs = refs[:n], refs[n:2 * n]
        send_sems, recv_sems = refs[2 * n:]
        x, y, c = _mesh_pos()
        cps = [pltpu.make_async_remote_copy(
            src_ref=ins[w], dst_ref=outs[w], send_sem=send_sems.at[w], recv_sem=recv_sems.at[w],
            device_id=(x, y, 1 - c), device_id_type=MESH) for w in range(n)]
        for cp in cps:
            cp.start()
        for cp in cps:
            cp.wait_recv()
        for cp in cps:
            cp.wait_send()

    anyspec = pl.BlockSpec(memory_space=pl.ANY)
    return pl.pallas_call(
        body, name="sibling_exchange",
        in_specs=[anyspec] * n, out_specs=[anyspec] * n,
        out_shape=[jax.ShapeDtypeStruct(a.shape, a.dtype) for a in arrs],
        scratch_shapes=[pltpu.SemaphoreType.DMA((n,)), pltpu.SemaphoreType.DMA((n,))],
    )(*arrs)


def _sum_stack(stack, name):
    _, R, C = stack.shape
    tr = _row_tile(R, 256, 16)

    def body(s_ref, o_ref):
        acc = s_ref[0].astype(F32)
        for k in range(1, N_CHIP):
            acc = acc + s_ref[k].astype(F32)
        o_ref[...] = acc

    return pl.pallas_call(
        body, name=name, grid=(R // tr,),
        in_specs=[pl.BlockSpec((N_CHIP, tr, C), lambda i: (0, i, 0))],
        out_specs=pl.BlockSpec((tr, C), lambda i: (i, 0)),
        out_shape=jax.ShapeDtypeStruct((R, C), F32),
        compiler_params=_params(("parallel",)),
    )(stack)


def _adam_math(w, g, m, v):
    m2 = ADAM_B1 * m + (1.0 - ADAM_B1) * g
    v2 = ADAM_B2 * v + (1.0 - ADAM_B2) * (g * g)
    m_hat = m2 / (1.0 - ADAM_B1 ** ADAM_STEP)
    v_hat = v2 / (1.0 - ADAM_B2 ** ADAM_STEP)
    delta = -ADAM_LR * (m_hat / (jnp.sqrt(v_hat) + ADAM_EPS) + ADAM_WD * w)
    return delta, m2, v2


def _adamw(w, m, v, s0, s1, name):
    R, C = w.shape
    tr = _row_tile(R, 128, 8)

    def body(w_ref, m_ref, v_ref, a_ref, b_ref, g_ref, d_ref, m2_ref, v2_ref):
        g = a_ref[...] + b_ref[...]
        delta, m2, v2 = _adam_math(w_ref[...], g, m_ref[...], v_ref[...])
        g_ref[...] = g
        d_ref[...] = delta
        m2_ref[...] = m2
        v2_ref[...] = v2

    spec = pl.BlockSpec((tr, C), lambda i: (i, 0))
    shp = jax.ShapeDtypeStruct((R, C), F32)
    return pl.pallas_call(
        body, name=name, grid=(R // tr,), in_specs=[spec] * 5, out_specs=[spec] * 4, out_shape=[shp] * 4,
        compiler_params=_params(("parallel",), VMEM_BIG),
    )(w, m, v, s0, s1)


def _adamw_small(w, m, v, gathered):
    def body(w_ref, m_ref, v_ref, s_ref, g_ref, d_ref, m2_ref, v2_ref):
        g = s_ref[0]
        for d in range(1, 8):
            g = g + s_ref[d]
        delta, m2, v2 = _adam_math(w_ref[...], g, m_ref[...], v_ref[...])
        g_ref[...] = g
        d_ref[...] = delta
        m2_ref[...] = m2
        v2_ref[...] = v2

    shp = jax.ShapeDtypeStruct(w.shape, F32)
    return pl.pallas_call(body, name="adamw_small", out_shape=[shp] * 4)(w, m, v, gathered)


SMALL = (("g_mix", 1024), ("g_ffn", 1024), ("g_ret_norm", 512), ("g_fox_q", 64), ("g_fox_k", 64), ("b_forget", 8))
SMALL_W = 3072


def _pack_small(parts):
    cols = []
    for (name, n) in SMALL:
        p = parts[name].reshape(1, -1)[:, :n]
        pad = -n % LANE
        cols.append(jnp.pad(p, ((0, 0), (0, pad))) if pad else p)
    used = sum(c.shape[1] for c in cols)
    cols.append(jnp.zeros((1, SMALL_W - used), F32))
    return jnp.concatenate(cols, axis=1)


def _unpack_small(vec):
    out, off = {}, 0
    for (name, n) in SMALL:
        out[name] = vec[:, off:off + n]
        off += n + (-n % LANE)
    return out


def kernel(x, g_mix, w_in, b_forget, g_ret_norm, w_ret_o, g_fox_q, g_fox_k, w_fox_o, w_out, g_ffn, w_gate, w_up, w_down, loss_target, m_g_mix, m_w_in, m_b_forget, m_g_ret_norm, m_w_ret_o, m_g_fox_q, m_g_fox_k, m_w_fox_o, m_w_out, m_g_ffn, m_w_gate, m_w_up, m_w_down, v_g_mix, v_w_in, v_b_forget, v_g_ret_norm, v_w_ret_o, v_g_fox_q, v_g_fox_k, v_w_fox_o, v_w_out, v_g_ffn, v_w_gate, v_w_up, v_w_down):
    T = x.shape[1]
    xs = x[0]
    tgt = loss_target[0]
    big_names = ("w_in", "w_ret_o", "w_fox_o", "w_out", "w_gate", "w_up", "w_down")
    tr = lambda a: jnp.swapaxes(a[0], 0, 1)
    big_w = dict(w_in=w_in[0], w_ret_o=w_ret_o[0], w_fox_o=w_fox_o[0], w_out=w_out[0], w_gate=tr(w_gate),
                 w_up=tr(w_up), w_down=w_down[0])
    big_m = dict(w_in=m_w_in[0], w_ret_o=m_w_ret_o[0], w_fox_o=m_w_fox_o[0], w_out=m_w_out[0], w_gate=tr(m_w_gate),
                 w_up=tr(m_w_up), w_down=m_w_down[0])
    big_v = dict(w_in=v_w_in[0], w_ret_o=v_w_ret_o[0], w_fox_o=v_w_fox_o[0], w_out=v_w_out[0], w_gate=tr(v_w_gate),
                 w_up=tr(v_w_up), w_down=v_w_down[0])
    small_w = dict(g_mix=g_mix, g_ffn=g_ffn, g_ret_norm=g_ret_norm, g_fox_q=g_fox_q, g_fox_k=g_fox_k, b_forget=b_forget)
    small_m = dict(g_mix=m_g_mix, g_ffn=m_g_ffn, g_ret_norm=m_g_ret_norm, g_fox_q=m_g_fox_q, g_fox_k=m_g_fox_k,
                   b_forget=m_b_forget)
    small_v = dict(g_mix=v_g_mix, g_ffn=v_g_ffn, g_ret_norm=v_g_ret_norm, g_fox_q=v_g_fox_q, g_fox_k=v_g_fox_k,
                   b_forget=v_b_forget)

    stacks = [_staged_place(big_w[n], "place_" + n) for n in big_names]
    (s_in,) = _gather_in_place(stacks[:1], "gather_w_in")
    s_ro, s_fo, s_out, s_gate, s_up, s_down = _gather_in_place(stacks[1:], "gather_rest")
    w_a, w_ff = _assemble_w_in(s_in)
    b_pad = jnp.pad(b_forget, ((0, 0), (0, LANE - FOX_H)))
    cos_t, sin_t = _rope_tables(T)
    consts = _ret_consts()

    h = _rms_cast(xs, g_mix)
    z_a = _mm_nn(h, w_a, "proj_in")
    z_ff = _mm_nn(h, w_ff, "proj_ff")
    qr, kr, qf, kf, vf, c_cum, qmax, kmax = _mix_prep(z_a, z_ff, cos_t, sin_t, b_pad, g_fox_q, g_fox_k)
    jstart, iend = _prune_tables(c_cum, qmax, kmax, 512)
    o_raw, u_r, states = _ret_fwd(qr, kr, z_a, g_ret_norm, consts)
    o_fox, q2 = _fox_fwd(jstart, qf, kf, vf)
    y_r, y_f, mrg, x2, h2, o_cat = _merge_out(u_r, o_fox, z_a, xs, g_ffn, s_ro, s_fo, s_out)
    sa, sb, act, dy, loss_vec = _ffn_fwd(h2, x2, tgt, s_gate, s_up, s_down)
    loss = lax.psum(0.5 / D_MODEL * jnp.sum(loss_vec), ("x", "y", "c"))

    dgp, dup, dx2, dg_ffn = _ffn_bwd(dy, sa, sb, x2, g_ffn, s_gate, s_up, s_down)
    ffn_part = [_grad_astack(dgp, h2, "gw_gate"), _grad_astack(dup, h2, "gw_up"), _grad_astack(act, dy, "gw_down")]
    d_yr, d_yf, dz_gt, dz_a, d_o, do_fox, dg_ret = _out_bwd(dx2, z_a, y_r, y_f, o_raw, o_fox, g_ret_norm,
                                                            s_ro, s_fo, s_out)
    dz_ret = _ret_bwd(d_o, qr, kr, z_a, states, cos_t, sin_t, consts)
    dq_f, dk_f, dv_f = _fox_bwd(iend, q2, kf, vf, do_fox)
    dz_fox, dz_ff, dg_q, dg_k, db_f = _fox_post_bwd(dq_f, dk_f, dv_f, z_a, z_ff, b_pad, g_fox_q, g_fox_k)
    grad_x, dg_mix = _in_bwd(dz_ret, dz_gt, dz_fox, dz_a, dz_ff, w_a, w_ff, xs, g_mix, dx2)

    g_in = _pack_g_in(_grad_plain(h, dz_ret, "gw_in_ret", F32), _grad_plain(h, dz_gt, "gw_in_gt", F32),
                      _grad_plain(h, dz_fox, "gw_in_fox", F32, tn=768), _grad_plain(h, dz_a, "gw_in_a", F32),
                      _grad_plain(h, dz_ff, "gw_in_ff", F32))
    late_part = [g_in, _grad_colstack(u_r, d_yr, "gw_ret_o", 256), _grad_colstack(o_cat, d_yf, "gw_fox_o", 256),
                 _grad_plain(mrg, dx2, "gw_out", BF).reshape(N_CHIP, 256, D_MODEL)]
    small_g = _pack_small(dict(g_mix=dg_mix, g_ffn=dg_ffn, g_ret_norm=dg_ret, g_fox_q=dg_q, g_fox_k=dg_k, b_forget=db_f))

    parts = late_part + ffn_part
    lands = [_staged_place(g, "place_g_" + n) for g, n in zip(parts, big_names)]
    recv = _scatter_partials(parts, lands, small_g)
    sums = [_sum_stack(r, "sum_" + n) for r, n in zip(recv[:7], big_names)]
    sib = _sibling_exchange(sums)
    big_out = {n: _adamw(big_w[n], big_m[n], big_v[n], sums[i], sib[i], "adamw_" + n) for i, n in enumerate(big_names)}
    sg, sd, sm, sv = _adamw_small(_pack_small(small_w), _pack_small(small_m), _pack_small(small_v), recv[-1])
    small_out = [_unpack_small(t) for t in (sg, sd, sm, sv)]

    order = ("g_mix", "w_in", "b_forget", "g_ret_norm", "w_ret_o", "g_fox_q", "g_fox_k", "w_fox_o", "w_out", "g_ffn",
             "w_gate", "w_up", "w_down")
    outs = [loss, grad_x[None]]
    for idx in range(4):
        for n in order:
            if n in ("w_gate", "w_up"):
                outs.append(jnp.swapaxes(big_out[n][idx], 0, 1)[None])
            else:
                outs.append(big_out[n][idx][None] if n in big_out else small_out[idx][n])
    return tuple(outs)
```

```python
import functools
import math

import numpy as np
import jax
import jax.numpy as jnp
from jax import lax
from jax.experimental import pallas as pl
from jax.experimental.pallas import tpu as pltpu

F32 = jnp.float32
BF = jnp.bfloat16
MESH = pl.DeviceIdType.MESH

D_MODEL = 1024
D_FF = 2816
N_CHIP = 4
FF_SH = D_FF // N_CHIP
IN_COLS = 5128
IN_SH = IN_COLS // N_CHIP
RET_H, RET_DV = 4, 128
FOX_H, FOX_D = 8, 64
CHUNK = 128
EPS = 1e-6
NEG = -1e30
LANE = 128
C_RET, C_GT, C_FOX, C_A, C_END = 0, 1024, 1536, 3072, 5120
L_CQ, L_CK, L_LSE, L_MAX = 64, 67, 70, 73

ADAM_LR, ADAM_B1, ADAM_B2, ADAM_EPS, ADAM_WD, ADAM_STEP = 0.001, 0.9, 0.999, 1e-08, 0.01, 10
VMEM_BIG = 56 * 1024 * 1024
VMEM_HUGE = 60 * 1024 * 1024
GRAD_TK = 2048
FFN_TM = 512


def _nn(a, b):
    return lax.dot_general(a, b, (((1,), (0,)), ((), ())), preferred_element_type=F32)


def _nt(a, b):
    return lax.dot_general(a, b, (((1,), (1,)), ((), ())), preferred_element_type=F32)


def _tn(a, b):
    return lax.dot_general(a, b, (((0,), (0,)), ((), ())), preferred_element_type=F32)


def _split3(x):
    hi = x.astype(BF)
    r = x - hi.astype(F32)
    mid = r.astype(BF)
    lo = (r - mid.astype(F32)).astype(BF)
    return hi, mid, lo


def _sigmoid(x):
    return 0.5 * jnp.tanh(0.5 * x) + 0.5


def _swap32(x):
    lane = lax.broadcasted_iota(jnp.int32, x.shape, 1)
    return jnp.where(lane < 32, pltpu.roll(x, 96, 1), pltpu.roll(x, 32, 1))


def _params(sem, vmem=None):
    return pltpu.CompilerParams(dimension_semantics=sem, vmem_limit_bytes=vmem)


def _row_tile(rows, cap, mult):
    return max(d for d in range(mult, cap + 1, mult) if rows % d == 0)


def _assemble_w_in(stack, tr=256):
    def body(s_ref, a_ref, f_ref):
        full = jnp.concatenate([s_ref[k].astype(F32) for k in range(N_CHIP)], axis=-1)
        a_ref[...] = jnp.concatenate([full[:, :3072], full[:, 3080:IN_COLS]], axis=-1).astype(BF)
        f_ref[...] = jnp.concatenate([full[:, 3072:3080], jnp.zeros((tr, LANE - FOX_H), F32)], axis=-1).astype(BF)

    return pl.pallas_call(
        body, name="assemble_w_in", grid=(D_MODEL // tr,),
        in_specs=[pl.BlockSpec((N_CHIP, tr, IN_SH), lambda i: (0, i, 0))],
        out_specs=[pl.BlockSpec((tr, C_END), lambda i: (i, 0)), pl.BlockSpec((tr, LANE), lambda i: (i, 0))],
        out_shape=[jax.ShapeDtypeStruct((D_MODEL, C_END), BF), jax.ShapeDtypeStruct((D_MODEL, LANE), BF)],
        compiler_params=_params(("parallel",), VMEM_BIG),
    )(stack)


def _pack_g_in(g_ret, g_gt, g_fox, g_a, g_ff, tr=256):
    def body(r_ref, t_ref, x_ref, a_ref, f_ref, o_ref):
        full = jnp.concatenate([r_ref[...], t_ref[...], x_ref[...], f_ref[...][:, :FOX_H], a_ref[...]], axis=-1)
        for k in range(N_CHIP):
            o_ref[k] = full[:, k * IN_SH:(k + 1) * IN_SH].astype(BF)

    def spec(w):
        return pl.BlockSpec((tr, w), lambda i: (i, 0))

    return pl.pallas_call(
        body, name="pack_g_in", grid=(D_MODEL // tr,),
        in_specs=[spec(1024), spec(512), spec(1536), spec(2048), spec(LANE)],
        out_specs=pl.BlockSpec((N_CHIP, tr, IN_SH), lambda i: (0, i, 0)),
        out_shape=jax.ShapeDtypeStruct((N_CHIP, D_MODEL, IN_SH), BF),
        compiler_params=_params(("parallel",), VMEM_BIG),
    )(g_ret, g_gt, g_fox, g_a, g_ff)


def _rms_cast(x, g, tm=512):
    T = x.shape[0]

    def body(x_ref, g_ref, o_ref):
        xv = x_ref[...]
        r = lax.rsqrt(jnp.mean(xv * xv, axis=-1, keepdims=True) + EPS)
        o_ref[...] = (xv * r * g_ref[...]).astype(BF)

    return pl.pallas_call(
        body, name="rms_cast", grid=(T // tm,),
        in_specs=[pl.BlockSpec((tm, D_MODEL), lambda i: (i, 0)), pl.BlockSpec((1, D_MODEL), lambda i: (0, 0))],
        out_specs=pl.BlockSpec((tm, D_MODEL), lambda i: (i, 0)),
        out_shape=jax.ShapeDtypeStruct((T, D_MODEL), BF),
        compiler_params=_params(("parallel",)),
    )(x, g)


def _hosted_call(body, name, grid, in_specs, out_specs, out_shape, scratch_shapes, vmem, args, push):
    sem = ("arbitrary",) * len(grid)
    if push is None:
        res = pl.pallas_call(body, name=name, grid=grid, in_specs=in_specs, out_specs=out_specs, out_shape=out_shape,
                             scratch_shapes=scratch_shapes, compiler_params=_params(sem, vmem))(*args)
        return list(res), []
    srcs, lands = push
    ns, nl, n_in, n_out = (0 if srcs is None else len(srcs)), len(lands), len(in_specs), len(out_specs)
    n_scr = len(scratch_shapes)

    def wrapped(*refs):
        pos = n_in + ns + nl
        ins, x_in = refs[:n_in], refs[n_in:pos]
        outs, x_out = refs[pos:pos + n_out], refs[pos + n_out:pos + n_out + nl]
        scr = refs[pos + n_out + nl:pos + n_out + nl + n_scr]
        ssem, rsem = refs[-2], refs[-1]
        src = None if srcs is None else x_in[:ns]
        ids = [pl.program_id(a) for a in range(len(grid))]
        first = functools.reduce(lambda p, q: p & q, [ids[a] == 0 for a in range(len(grid))])
        last = functools.reduce(lambda p, q: p & q, [ids[a] == grid[a] - 1 for a in range(len(grid))])

        @pl.when(first)
        def _():
            for cp in _push_copies(src, x_out, ssem, rsem, False):
                cp.start()

        body(*ins, *outs, *scr)

        @pl.when(last)
        def _():
            for cp in _push_copies(src, x_out, ssem, rsem, True):
                cp.wait_recv()
                cp.wait_send()

    anyspec = pl.BlockSpec(memory_space=pl.ANY)
    extra = ([] if srcs is None else list(srcs)) + list(lands)
    res = pl.pallas_call(
        wrapped, name=name, grid=grid,
        in_specs=list(in_specs) + [anyspec] * len(extra), out_specs=list(out_specs) + [anyspec] * nl,
        out_shape=list(out_shape) + [jax.ShapeDtypeStruct(a.shape, a.dtype) for a in lands],
        input_output_aliases={n_in + ns + i: n_out + i for i in range(nl)},
        scratch_shapes=list(scratch_shapes) + [pltpu.SemaphoreType.DMA((3 * nl,)), pltpu.SemaphoreType.DMA((3 * nl,))],
        compiler_params=_params(sem, vmem),
    )(*args, *extra)
    return list(res[:n_out]), list(res[n_out:])


def _mm_nn(a, b, name, tm=512, tn=1024, push=None):
    M, K = a.shape
    N = b.shape[1]
    tn = min(tn, N)

    def body(a_ref, b_ref, o_ref):
        o_ref[...] = _nn(a_ref[...], b_ref[...])

    (out,), lands = _hosted_call(
        body, name, (N // tn, M // tm),
        [pl.BlockSpec((tm, K), lambda j, i: (i, 0)), pl.BlockSpec((K, tn), lambda j, i: (0, j))],
        [pl.BlockSpec((tm, tn), lambda j, i: (i, j))], [jax.ShapeDtypeStruct((M, N), F32)], [], None, (a, b), push)
    return out, lands


def _mm_tn(a, b, name, grid, a_spec, b_spec, o_spec, out_shape, acc_shape):
    nk = grid[-1]

    def body(a_ref, b_ref, o_ref, acc):
        k = pl.program_id(len(grid) - 1)

        @pl.when(k == 0)
        def _():
            acc[...] = jnp.zeros(acc.shape, F32)

        acc[...] += _tn(a_ref[...].astype(BF), b_ref[...].astype(BF))

        @pl.when(k == nk - 1)
        def _():
            o_ref[...] = acc[...].astype(o_ref.dtype)

    return pl.pallas_call(
        body, name=name, grid=grid, in_specs=[a_spec, b_spec], out_specs=o_spec, out_shape=out_shape,
        scratch_shapes=[pltpu.VMEM(acc_shape, F32)],
        compiler_params=_params(("parallel",) * (len(grid) - 1) + ("arbitrary",), VMEM_BIG),
    )(a, b)


def _grad_plain(a, b, name, out_dtype, tk=GRAD_TK, tn=1024):
    T, M = a.shape
    N = b.shape[1]
    tn = min(tn, N)
    return _mm_tn(a, b, name, (N // tn, T // tk),
                  pl.BlockSpec((tk, M), lambda j, k: (k, 0)), pl.BlockSpec((tk, tn), lambda j, k: (k, j)),
                  pl.BlockSpec((M, tn), lambda j, k: (0, j)), jax.ShapeDtypeStruct((M, N), out_dtype), (M, tn))


def _grad_colstack(a, b, name, wcol, tk=GRAD_TK):
    T, M = a.shape
    S = b.shape[1] // wcol
    return _mm_tn(a, b, name, (S, T // tk),
                  pl.BlockSpec((tk, M), lambda s, k: (k, 0)), pl.BlockSpec((tk, wcol), lambda s, k: (k, s)),
                  pl.BlockSpec((None, M, wcol), lambda s, k: (s, 0, 0)),
                  jax.ShapeDtypeStruct((S, M, wcol), BF), (M, wcol))


def _grad_bstack(a, b, name, tk=GRAD_TK):
    T, M = a.shape
    S, _, n = b.shape
    return _mm_tn(a, b, name, (S, T // tk),
                  pl.BlockSpec((tk, M), lambda s, k: (k, 0)), pl.BlockSpec((None, tk, n), lambda s, k: (s, k, 0)),
                  pl.BlockSpec((None, M, n), lambda s, k: (s, 0, 0)),
                  jax.ShapeDtypeStruct((S, M, n), BF), (M, n))


def _grad_astack(a, b, name, tk=GRAD_TK):
    S, T, m = a.shape
    N = b.shape[1]
    return _mm_tn(a, b, name, (S, T // tk),
                  pl.BlockSpec((None, tk, m), lambda s, k: (s, k, 0)), pl.BlockSpec((tk, N), lambda s, k: (k, 0)),
                  pl.BlockSpec((None, m, N), lambda s, k: (s, 0, 0)),
                  jax.ShapeDtypeStruct((S, m, N), BF), (m, N))


def _rope_tables(T):
    half = 32
    pos = jnp.arange(T, dtype=F32)
    inv_freq = 1.0 / (10000.0 ** (jnp.arange(half, dtype=F32) / half))
    ang = pos[:, None] * inv_freq[None, :]
    cos, sin = jnp.cos(ang), jnp.sin(ang)
    z = jnp.zeros((T, 64), F32)
    return jnp.concatenate([cos, cos, z], axis=-1), jnp.concatenate([-sin, sin, z], axis=-1)


def _ret_consts():
    h = np.arange(RET_H, dtype=np.float32)
    log_g = np.log1p(-(np.float32(2.0) ** (-5.0 - h))).astype(np.float32)
    idx = np.arange(CHUNK, dtype=np.float32)
    diff = idx[:, None] - idx[None, :]
    decay = np.where(diff[None] >= 0, np.exp(np.maximum(diff, 0.0)[None] * log_g[:, None, None]), 0.0)
    zeta = np.exp((CHUNK - 1.0 - idx)[None, :] * log_g[:, None])
    xi = np.exp((idx + 1.0)[None, :] * log_g[:, None])
    gc = np.exp(CHUNK * log_g)
    bc = lambda v: np.broadcast_to(v[:, :, None], (RET_H, CHUNK, LANE)).astype(np.float32)
    gcb = np.broadcast_to(gc[:, None, None], (RET_H, CHUNK, LANE)).astype(np.float32)
    return (jnp.asarray(decay.astype(np.float32)), jnp.asarray(bc(zeta)), jnp.asarray(bc(xi)), jnp.asarray(gcb))


def _mix_prep(z_a, z_ff, cos_t, sin_t, b_f, g_q, g_k, tm=256, push=None):
    T = z_a.shape[0]

    def body(zqk_ref, zf_ref, zff_ref, cos_ref, sin_ref, b_ref, gq_ref, gk_ref,
             qr_ref, kr_ref, qf_ref, kf_ref, vf_ref, c_ref, qmax_ref, kmax_ref, carry):
        i = pl.program_id(0)

        @pl.when(i == 0)
        def _():
            carry[...] = jnp.zeros(carry.shape, F32)
            qmax_ref[...] = jnp.zeros(qmax_ref.shape, F32)
            kmax_ref[...] = jnp.zeros(kmax_ref.shape, F32)

        lane = lax.broadcasted_iota(jnp.int32, (tm, LANE), 1)
        lane1 = lax.broadcasted_iota(jnp.int32, (1, LANE), 1)
        zpad = jnp.zeros((tm, 64), F32)
        cosv, sinv = cos_ref[...], sin_ref[...]
        zqk = zqk_ref[...]
        for h in range(RET_H):
            for src, dst, scale in ((0, qr_ref, 1.0), (256, kr_ref, 0.125)):
                xh = jnp.concatenate([zqk[:, src + 64 * h: src + 64 * h + 64], zpad], axis=-1)
                rot = xh * cosv + _swap32(xh) * sinv
                dst[h] = (rot * scale).astype(BF)

        lf_in = zff_ref[...] + b_ref[...]
        logf = jnp.minimum(lf_in, 0.0) - jnp.log(1.0 + jnp.exp(-jnp.abs(lf_in)))
        row = lax.broadcasted_iota(jnp.int32, (tm, tm), 0)
        col = lax.broadcasted_iota(jnp.int32, (tm, tm), 1)
        tri = (row >= col).astype(BF)
        hi, mid, lo = _split3(logf)
        cs = _nn(tri, hi) + _nn(tri, mid) + _nn(tri, lo) + carry[...]
        carry[...] = cs[tm - 1:tm, :]
        c_ref[...] = cs

        zf = zf_ref[...]
        one = jnp.ones((tm, LANE), F32)
        qmax, kmax = qmax_ref[...], kmax_ref[...]
        for h in range(FOX_H):
            c = cs[:, h:h + 1]
            chi, cmid, clo = [t.astype(F32) for t in _split3(c)]
            qh = zf[:, 64 * h:64 * h + 64]
            kh = zf[:, 512 + 64 * h:512 + 64 * h + 64]
            vh = zf[:, 1024 + 64 * h:1024 + 64 * h + 64]
            qn = qh * lax.rsqrt(jnp.mean(qh * qh, axis=-1, keepdims=True) + EPS) * gq_ref[...] * 0.125
            kn = kh * lax.rsqrt(jnp.mean(kh * kh, axis=-1, keepdims=True) + EPS) * gk_ref[...]
            qa = jnp.concatenate([qn, zpad], axis=-1)
            qa = jnp.where(lane == L_CQ, chi, jnp.where(lane == L_CQ + 1, cmid, jnp.where(lane == L_CQ + 2, clo, qa)))
            qa = jnp.where((lane >= L_CK) & (lane < L_CK + 3), one, qa)
            ka = jnp.concatenate([kn, zpad], axis=-1)
            ka = jnp.where(lane == L_CK, -chi, jnp.where(lane == L_CK + 1, -cmid, jnp.where(lane == L_CK + 2, -clo, ka)))
            ka = jnp.where(((lane >= L_CQ) & (lane < L_CQ + 3)) | ((lane >= L_LSE) & (lane < L_MAX + 3)), one, ka)
            va = jnp.concatenate([vh, zpad], axis=-1)
            va = jnp.where((lane >= 64) & (lane < 67), one, va)
            qf_ref[h] = qa.astype(BF)
            kf_ref[h] = ka.astype(BF)
            vf_ref[h] = va.astype(BF)
            qmax = jnp.where(lane1 == h, jnp.maximum(qmax, jnp.max(jnp.sum(qn * qn, axis=-1, keepdims=True), axis=0,
                                                                   keepdims=True)), qmax)
            kmax = jnp.where(lane1 == h, jnp.maximum(kmax, jnp.max(jnp.sum(kn * kn, axis=-1, keepdims=True), axis=0,
                                                                   keepdims=True)), kmax)
        qmax_ref[...] = qmax
        kmax_ref[...] = kmax

    hspec4 = pl.BlockSpec((RET_H, tm, LANE), lambda i: (0, i, 0))
    hspec8 = pl.BlockSpec((FOX_H, tm, LANE), lambda i: (0, i, 0))
    small = lambda w: pl.BlockSpec((1, w), lambda i: (0, 0))
    return _hosted_call(
        body, "mix_prep", (T // tm,),
        [pl.BlockSpec((tm, 512), lambda i: (i, 0)), pl.BlockSpec((tm, 1536), lambda i: (i, 1)),
         pl.BlockSpec((tm, LANE), lambda i: (i, 0)), pl.BlockSpec((tm, LANE), lambda i: (i, 0)),
         pl.BlockSpec((tm, LANE), lambda i: (i, 0)), small(LANE), small(64), small(64)],
        [hspec4, hspec4, hspec8, hspec8, hspec8, pl.BlockSpec((tm, LANE), lambda i: (i, 0)), small(LANE), small(LANE)],
        [jax.ShapeDtypeStruct((RET_H, T, LANE), BF)] * 2 + [jax.ShapeDtypeStruct((FOX_H, T, LANE), BF)] * 3
        + [jax.ShapeDtypeStruct((T, LANE), F32), jax.ShapeDtypeStruct((1, LANE), F32), jax.ShapeDtypeStruct((1, LANE), F32)],
        [pltpu.VMEM((1, LANE), F32)], VMEM_BIG, (z_a, z_a, z_ff, cos_t, sin_t, b_f, g_q, g_k), push)


def _ret_fwd(qr, kr, z_a, g_ret, consts, tt=512):
    T = z_a.shape[0]
    nch = tt // CHUNK
    decay, zeta, xi, gcb = consts

    def body(q_ref, k_ref, v_ref, gt_ref, g_ref, d_ref, ze_ref, xi_ref, gc_ref, o_ref, u_ref, st_ref, r_sc):
        i = pl.program_id(0)

        @pl.when(i == 0)
        def _():
            r_sc[...] = jnp.zeros(r_sc.shape, F32)

        for c in range(nch):
            rows = slice(c * CHUNK, (c + 1) * CHUNK)
            for h in range(RET_H):
                cols = slice(h * RET_DV, (h + 1) * RET_DV)
                q, k = q_ref[h, rows, :], k_ref[h, rows, :]
                v32 = v_ref[rows, cols]
                r = r_sc[h]
                st_ref[h, rows, :] = r
                s = _nt(q, k) * d_ref[h]
                o = _nn(s.astype(BF), v32.astype(BF)) + _nn(q, r.astype(BF)) * xi_ref[h]
                r_sc[h] = gc_ref[h] * r + _tn(k, (v32 * ze_ref[h]).astype(BF))
                o_ref[rows, cols] = o
                mu = jnp.mean(o, axis=-1, keepdims=True)
                xc = o - mu
                on = xc * lax.rsqrt(jnp.mean(xc * xc, axis=-1, keepdims=True) + EPS)
                gt = gt_ref[rows, cols]
                u_ref[rows, cols] = (gt * _sigmoid(gt) * (on * g_ref[:, cols])).astype(BF)

    hspec = pl.BlockSpec((RET_H, tt, LANE), lambda i: (0, i, 0))
    cspec = pl.BlockSpec((RET_H, CHUNK, LANE), lambda i: (0, 0, 0))
    return pl.pallas_call(
        body, name="ret_fwd", grid=(T // tt,),
        in_specs=[hspec, hspec, pl.BlockSpec((tt, 512), lambda i: (i, 1)), pl.BlockSpec((tt, 512), lambda i: (i, 2)),
                  pl.BlockSpec((1, 512), lambda i: (0, 0)), cspec, cspec, cspec, cspec],
        out_specs=[pl.BlockSpec((tt, 512), lambda i: (i, 0)), pl.BlockSpec((tt, 512), lambda i: (i, 0)), hspec],
        out_shape=[jax.ShapeDtypeStruct((T, 512), F32), jax.ShapeDtypeStruct((T, 512), BF),
                   jax.ShapeDtypeStruct((RET_H, T, LANE), F32)],
        scratch_shapes=[pltpu.VMEM((RET_H, CHUNK, LANE), F32)],
        compiler_params=_params(("arbitrary",), VMEM_BIG),
    )(qr, kr, z_a, z_a, g_ret, decay, zeta, xi, gcb)


PRUNE_LOG = -110.0


def _prune_tables(c, qmax, kmax, sub):
    n = c.shape[0] // sub
    u = jnp.sqrt(qmax[0, :FOX_H] * kmax[0, :FOX_H]) * 1.02 + 0.5
    first = c[0::sub, :FOX_H].T
    last = c[sub - 1::sub, :FOX_H].T
    blk = jnp.arange(n, dtype=jnp.int32)
    needed = (2.0 * u[:, None, None] + first[:, :, None] - last[:, None, :] >= PRUNE_LOG) | (blk[None, :] >= blk[:, None])[None]
    jlo = jnp.argmax(needed, axis=2).astype(jnp.int32)
    jstart = jnp.minimum(jlo[:, 0::2], jlo[:, 1::2]) // 2
    need_q = jlo[:, None, :] <= (2 * jnp.arange(n // 2, dtype=jnp.int32) + 1)[None, :, None]
    iend = n - jnp.argmax(need_q[:, :, ::-1], axis=2).astype(jnp.int32)
    return jstart.astype(jnp.int32), iend.astype(jnp.int32)


def _fox_fwd(jstart, q, k, v, sub=512):
    H, T, _ = q.shape
    tb = 2 * sub

    def body(js_ref, q_ref, k_ref, v_ref, o_ref, q2_ref, mx_sc, acc_sc):
        i = pl.program_id(1)
        j0 = js_ref[pl.program_id(0), i]
        lane = lax.broadcasted_iota(jnp.int32, (sub, LANE), 1)
        row = lax.broadcasted_iota(jnp.int32, (sub, sub), 0)
        col = lax.broadcasted_iota(jnp.int32, (sub, sub), 1)
        causal = row >= col
        qs = [q_ref[0:sub, :], q_ref[sub:tb, :]]
        d0 = pl.multiple_of(i * tb, tb)
        d1 = pl.multiple_of(i * tb + sub, sub)

        def lane_max(s):
            m = s[:, 0:LANE]
            for c in range(1, s.shape[1] // LANE):
                m = jnp.maximum(m, s[:, c * LANE:(c + 1) * LANE])
            return m

        mx_sc[...] = jnp.full(mx_sc.shape, NEG, F32)

        def max_body(j, carry):
            kb = k_ref[pl.ds(pl.multiple_of(j * tb, tb), tb), :]
            for a in range(2):
                mx_sc[a] = jnp.maximum(mx_sc[a], lane_max(_nt(qs[a], kb)))
            return carry

        lax.fori_loop(j0, i, max_body, 0)
        k0, k1 = k_ref[pl.ds(d0, sub), :], k_ref[pl.ds(d1, sub), :]
        v0, v1 = v_ref[pl.ds(d0, sub), :], v_ref[pl.ds(d1, sub), :]
        mx = [jnp.maximum(mx_sc[0], lane_max(jnp.where(causal, _nt(qs[0], k0), NEG))),
              jnp.maximum(jnp.maximum(mx_sc[1], lane_max(_nt(qs[1], k0))),
                          lane_max(jnp.where(causal, _nt(qs[1], k1), NEG)))]
        ms = [jnp.max(t, axis=1, keepdims=True) for t in mx]

        def put3(base, first, val):
            hi, mid, lo = _split3(val)
            return jnp.where(lane == first, hi, jnp.where(lane == first + 1, mid, jnp.where(lane == first + 2, lo, base)))

        qm = [put3(qs[a], L_MAX, -ms[a]) for a in range(2)]

        acc_sc[...] = jnp.zeros(acc_sc.shape, F32)

        def acc_body(j, carry):
            off = pl.multiple_of(j * tb, tb)
            kb, vb = k_ref[pl.ds(off, tb), :], v_ref[pl.ds(off, tb), :]
            for a in range(2):
                acc_sc[a] += _nn(jnp.exp(_nt(qm[a], kb)).astype(BF), vb)
            return carry

        lax.fori_loop(j0, i, acc_body, 0)

        def pv(qa, kk, vv, masked):
            p = jnp.exp(_nt(qa, kk))
            if masked:
                p = jnp.where(causal, p, 0.0)
            return _nn(p.astype(BF), vv)

        accs = [acc_sc[0] + pv(qm[0], k0, v0, True),
                acc_sc[1] + pv(qm[1], k0, v0, False) + pv(qm[1], k1, v1, True)]
        for a in range(2):
            rows = slice(a * sub, (a + 1) * sub)
            l = accs[a][:, 64:65]
            o_ref[rows, :] = jnp.where(lane < 64, accs[a] / l, 0.0)
            q2_ref[rows, :] = put3(qs[a], L_LSE, -(ms[a] + jnp.log(l)))

    blk = pl.BlockSpec((None, tb, LANE), lambda h, i, js: (h, i, 0))
    full = pl.BlockSpec((None, T, LANE), lambda h, i, js: (h, 0, 0))
    return pl.pallas_call(
        body, name="fox_fwd",
        grid_spec=pltpu.PrefetchScalarGridSpec(
            num_scalar_prefetch=1, grid=(H, T // tb), in_specs=[blk, full, full], out_specs=[blk, blk],
            scratch_shapes=[pltpu.VMEM((2, sub, LANE), F32), pltpu.VMEM((2, sub, LANE), F32)]),
        out_shape=[jax.ShapeDtypeStruct((H, T, LANE), F32), jax.ShapeDtypeStruct((H, T, LANE), BF)],
        compiler_params=_params(("parallel", "arbitrary"), VMEM_BIG),
    )(jstart, q, k, v)


def _merge_out(u_r, o_fox, z_a, x, g_ffn, w_ro, w_fo, w_out, tm=256):
    T = x.shape[0]

    def body(u_ref, of_ref, ar_ref, af_ref, x_ref, g_ref, wro_ref, wfo_ref, wout_ref,
             yr_ref, yf_ref, m_ref, x2_ref, h2_ref, oc_ref):
        u = u_ref[...]
        oc = jnp.concatenate([of_ref[h][:, :FOX_D] for h in range(FOX_H)], axis=-1).astype(BF)
        oc_ref[...] = oc
        yr = jnp.concatenate([_nn(u, wro_ref[k]) for k in range(N_CHIP)], axis=-1)
        yf = jnp.concatenate([_nn(oc, wfo_ref[k]) for k in range(N_CHIP)], axis=-1)
        yr_ref[...] = yr
        yf_ref[...] = yf
        m = (_sigmoid(ar_ref[...]) * yr + _sigmoid(af_ref[...]) * yf).astype(BF)
        m_ref[...] = m
        x2 = x_ref[...]
        for k in range(N_CHIP):
            x2 = x2 + _nn(m[:, 256 * k:256 * k + 256], wout_ref[k])
        x2_ref[...] = x2
        r = lax.rsqrt(jnp.mean(x2 * x2, axis=-1, keepdims=True) + EPS)
        h2_ref[...] = (x2 * r * g_ref[...]).astype(BF)

    row = lambda w: pl.BlockSpec((tm, w), lambda i: (i, 0))
    const = lambda shp: pl.BlockSpec(shp, lambda i: (0,) * len(shp))
    return pl.pallas_call(
        body, name="merge_out", grid=(T // tm,),
        in_specs=[row(512), pl.BlockSpec((FOX_H, tm, LANE), lambda i: (0, i, 0)),
                  pl.BlockSpec((tm, 1024), lambda i: (i, 3)), pl.BlockSpec((tm, 1024), lambda i: (i, 4)),
                  row(1024), const((1, 1024)), const((N_CHIP, 512, 256)), const((N_CHIP, 512, 256)),
                  const((N_CHIP, 256, 1024))],
        out_specs=[row(1024), row(1024), row(1024), row(1024), row(1024), row(512)],
        out_shape=[jax.ShapeDtypeStruct((T, 1024), F32), jax.ShapeDtypeStruct((T, 1024), F32),
                   jax.ShapeDtypeStruct((T, 1024), BF), jax.ShapeDtypeStruct((T, 1024), F32),
                   jax.ShapeDtypeStruct((T, 1024), BF), jax.ShapeDtypeStruct((T, 512), BF)],
        compiler_params=_params(("parallel",), VMEM_BIG),
    )(u_r, o_fox, z_a, z_a, x, g_ffn, w_ro, w_fo, w_out)


def _load_resident(hbm_refs, vmem_refs, sem):
    cps = [pltpu.make_async_copy(h, v, sem.at[i]) for i, (h, v) in enumerate(zip(hbm_refs, vmem_refs))]
    for cp in cps:
        cp.start()
    for cp in cps:
        cp.wait()


def _ffn_fwd(h2, x2, tgt, w_gate, w_up, w_down, tm=FFN_TM):
    T = h2.shape[0]

    def body(h_ref, x2_ref, t_ref, wg_hbm, wu_hbm, wd_hbm, a_ref, b_ref, act_ref, dy_ref, ls_ref, wg, wu, wd, sem):
        @pl.when(pl.program_id(0) == 0)
        def _():
            _load_resident((wg_hbm, wu_hbm, wd_hbm), (wg, wu, wd), sem)
            ls_ref[...] = jnp.zeros(ls_ref.shape, F32)

        h = h_ref[...]
        err = x2_ref[...] - t_ref[...]
        for k in range(N_CHIP):
            gp = _nt(h, wg[k])
            up = _nt(h, wu[k])
            sg = _sigmoid(gp)
            silu = gp * sg
            a_ref[k] = silu.astype(BF)
            b_ref[k] = (up * sg * (1.0 + gp * (1.0 - sg))).astype(BF)
            act = (silu * up).astype(BF)
            act_ref[k] = act
            err = err + _nn(act, wd[k])
        dy_ref[...] = err * (1.0 / D_MODEL)
        ls_ref[...] += jnp.sum(err * err, axis=0, keepdims=True)

    row = pl.BlockSpec((tm, D_MODEL), lambda i: (i, 0))
    hid = pl.BlockSpec((N_CHIP, tm, FF_SH), lambda i: (0, i, 0))
    anyspec = pl.BlockSpec(memory_space=pl.ANY)
    wshape = pltpu.VMEM((N_CHIP, FF_SH, D_MODEL), BF)
    return pl.pallas_call(
        body, name="ffn_fwd", grid=(T // tm,),
        in_specs=[row, row, row, anyspec, anyspec, anyspec],
        out_specs=[hid, hid, hid, row, pl.BlockSpec((1, D_MODEL), lambda i: (0, 0))],
        out_shape=[jax.ShapeDtypeStruct((N_CHIP, T, FF_SH), BF)] * 3
        + [jax.ShapeDtypeStruct((T, D_MODEL), F32), jax.ShapeDtypeStruct((1, D_MODEL), F32)],
        scratch_shapes=[wshape, wshape, wshape, pltpu.SemaphoreType.DMA((3,))],
        compiler_params=_params(("arbitrary",), VMEM_HUGE),
    )(h2, x2, tgt, w_gate, w_up, w_down)


def _ffn_bwd(dy, sa, sb, x2, g_ffn, w_gate, w_up, w_down, tm=FFN_TM):
    T = dy.shape[0]

    def body(dy_ref, a_ref, b_ref, x2_ref, g_ref, wg_hbm, wu_hbm, wd_hbm, dgp_ref, dup_ref, dx_ref, dg_ref,
             wg, wu, wd, sem):
        @pl.when(pl.program_id(0) == 0)
        def _():
            _load_resident((wg_hbm, wu_hbm, wd_hbm), (wg, wu, wd), sem)
            dg_ref[...] = jnp.zeros(dg_ref.shape, F32)

        dy = dy_ref[...]
        dyb = dy.astype(BF)
        dh = jnp.zeros((tm, D_MODEL), F32)
        for k in range(N_CHIP):
            dact = _nt(dyb, wd[k])
            dup = (dact * a_ref[k]).astype(BF)
            dgp = (dact * b_ref[k]).astype(BF)
            dgp_ref[k] = dgp
            dup_ref[k] = dup
            dh = dh + _nn(dgp, wg[k]) + _nn(dup, wu[k])
        x2 = x2_ref[...]
        r = lax.rsqrt(jnp.mean(x2 * x2, axis=-1, keepdims=True) + EPS)
        xn = x2 * r
        dg_ref[...] += jnp.sum(dh * xn, axis=0, keepdims=True)
        dxn = dh * g_ref[...]
        dx_ref[...] = dy + r * (dxn - xn * jnp.mean(dxn * xn, axis=-1, keepdims=True))

    row = pl.BlockSpec((tm, D_MODEL), lambda i: (i, 0))
    hid = pl.BlockSpec((N_CHIP, tm, FF_SH), lambda i: (0, i, 0))
    vec = pl.BlockSpec((1, D_MODEL), lambda i: (0, 0))
    anyspec = pl.BlockSpec(memory_space=pl.ANY)
    wshape = pltpu.VMEM((N_CHIP, FF_SH, D_MODEL), BF)
    return pl.pallas_call(
        body, name="ffn_bwd", grid=(T // tm,),
        in_specs=[row, hid, hid, row, vec, anyspec, anyspec, anyspec],
        out_specs=[hid, hid, row, vec],
        out_shape=[jax.ShapeDtypeStruct((N_CHIP, T, FF_SH), BF), jax.ShapeDtypeStruct((N_CHIP, T, FF_SH), BF),
                   jax.ShapeDtypeStruct((T, D_MODEL), F32), jax.ShapeDtypeStruct((1, D_MODEL), F32)],
        scratch_shapes=[wshape, wshape, wshape, pltpu.SemaphoreType.DMA((3,))],
        compiler_params=_params(("arbitrary",), VMEM_HUGE),
    )(dy, sa, sb, x2, g_ffn, w_gate, w_up, w_down)


def _out_bwd(dx2, z_a, y_r, y_f, o_raw, o_fox, g_ret, w_ro, w_fo, w_out, tm=256, push=None):
    T = dx2.shape[0]

    def body(dx_ref, gt_ref, ar_ref, af_ref, yr_ref, yf_ref, o_ref, of_ref, g_ref, wro_ref, wfo_ref, wout_ref,
             dyr_ref, dyf_ref, dgt_ref, da_ref, do_ref, dof_ref, dg_ref):
        i = pl.program_id(0)

        @pl.when(i == 0)
        def _():
            dg_ref[...] = jnp.zeros(dg_ref.shape, F32)

        dxb = dx_ref[...].astype(BF)
        dm = jnp.concatenate([_nt(dxb, wout_ref[k]) for k in range(N_CHIP)], axis=-1)
        sr, sf = _sigmoid(ar_ref[...]), _sigmoid(af_ref[...])
        dyr = dm * sr
        dyf = dm * sf
        da_ref[:, :1024] = (dyr * yr_ref[...] * (1.0 - sr)).astype(BF)
        da_ref[:, 1024:] = (dyf * yf_ref[...] * (1.0 - sf)).astype(BF)
        dyr = dyr.astype(BF)
        dyf = dyf.astype(BF)
        dyr_ref[...] = dyr
        dyf_ref[...] = dyf
        du = jnp.zeros((tm, 512), F32)
        doc = jnp.zeros((tm, 512), F32)
        for k in range(N_CHIP):
            du = du + _nt(dyr[:, 256 * k:256 * k + 256], wro_ref[k])
            doc = doc + _nt(dyf[:, 256 * k:256 * k + 256], wfo_ref[k])

        for h in range(RET_H):
            cols = slice(h * RET_DV, (h + 1) * RET_DV)
            o = o_ref[:, cols]
            mu = jnp.mean(o, axis=-1, keepdims=True)
            xc = o - mu
            rstd = lax.rsqrt(jnp.mean(xc * xc, axis=-1, keepdims=True) + EPS)
            on = xc * rstd
            g = g_ref[:, cols]
            gt = gt_ref[:, cols]
            sg = _sigmoid(gt)
            duh = du[:, cols]
            dgt_ref[:, cols] = (duh * (on * g) * sg * (1.0 + gt * (1.0 - sg))).astype(BF)
            dog = duh * gt * sg
            dg_ref[:, cols] += jnp.sum(dog * on, axis=0, keepdims=True)
            don = dog * g
            do_ref[:, cols] = rstd * (don - jnp.mean(don, axis=-1, keepdims=True)
                                      - on * jnp.mean(don * on, axis=-1, keepdims=True))

        lane = lax.broadcasted_iota(jnp.int32, (tm, LANE), 1)
        zpad = jnp.zeros((tm, 64), F32)
        for h in range(FOX_H):
            doh = doc[:, 64 * h:64 * h + 64]
            delta = jnp.sum(doh * of_ref[h][:, :FOX_D], axis=-1, keepdims=True)
            hi, mid, lo = [t.astype(F32) for t in _split3(-delta)]
            da = jnp.concatenate([doh, zpad], axis=-1)
            da = jnp.where(lane == 64, hi, jnp.where(lane == 65, mid, jnp.where(lane == 66, lo, da)))
            dof_ref[h] = da.astype(BF)

    row = lambda w: pl.BlockSpec((tm, w), lambda i: (i, 0))
    const = lambda shp: pl.BlockSpec(shp, lambda i: (0,) * len(shp))
    hsp = pl.BlockSpec((FOX_H, tm, LANE), lambda i: (0, i, 0))
    return _hosted_call(
        body, "out_bwd", (T // tm,),
        [row(1024), pl.BlockSpec((tm, 512), lambda i: (i, 2)), pl.BlockSpec((tm, 1024), lambda i: (i, 3)),
         pl.BlockSpec((tm, 1024), lambda i: (i, 4)), row(1024), row(1024), row(512), hsp,
         const((1, 512)), const((N_CHIP, 512, 256)), const((N_CHIP, 512, 256)), const((N_CHIP, 256, 1024))],
        [row(1024), row(1024), row(512), row(2048), row(512), hsp, const((1, 512))],
        [jax.ShapeDtypeStruct((T, 1024), BF), jax.ShapeDtypeStruct((T, 1024), BF),
         jax.ShapeDtypeStruct((T, 512), BF), jax.ShapeDtypeStruct((T, 2048), BF),
         jax.ShapeDtypeStruct((T, 512), F32), jax.ShapeDtypeStruct((FOX_H, T, LANE), BF),
         jax.ShapeDtypeStruct((1, 512), F32)],
        [], VMEM_BIG, (dx2, z_a, z_a, z_a, y_r, y_f, o_raw, o_fox, g_ret, w_ro, w_fo, w_out), push)


def _ret_bwd(d_o, qr, kr, z_a, states, cos_t, sin_t, consts, tt=512, push=None):
    T = z_a.shape[0]
    nt = T // tt
    nch = tt // CHUNK
    decay, zeta, xi, gcb = consts

    def body(do_ref, q_ref, k_ref, v_ref, st_ref, cos_ref, sin_ref, d_ref, ze_ref, xi_ref, gc_ref, dz_ref, g_sc):
        i = pl.program_id(0)

        @pl.when(i == 0)
        def _():
            g_sc[...] = jnp.zeros(g_sc.shape, F32)

        for c in reversed(range(nch)):
            rows = slice(c * CHUNK, (c + 1) * CHUNK)
            cosv, sinv = cos_ref[rows, :], sin_ref[rows, :]
            dq_parts, dk_parts = [], []
            for h in range(RET_H):
                cols = slice(h * RET_DV, (h + 1) * RET_DV)
                q, k = q_ref[h, rows, :], k_ref[h, rows, :]
                v32 = v_ref[rows, cols]
                vb = v32.astype(BF)
                r = st_ref[h, rows, :]
                g = g_sc[h]
                gb = g.astype(BF)
                d_o = do_ref[rows, cols]
                dob = d_o.astype(BF)
                dox = (d_o * xi_ref[h]).astype(BF)
                dec = d_ref[h]
                s = (_nt(q, k) * dec).astype(BF)
                ds = (_nt(dob, vb) * dec).astype(BF)
                dv = _tn(s, dob) + ze_ref[h] * _nn(k, gb)
                dq = _nn(ds, k) + _nt(dox, r.astype(BF))
                dk = _tn(ds, q) + _nt((v32 * ze_ref[h]).astype(BF), gb)
                g_sc[h] = gc_ref[h] * g + _tn(q, dox)
                dq_parts.append((dq * cosv - _swap32(dq) * sinv)[:, :64])
                dk_parts.append(((dk * cosv - _swap32(dk) * sinv) * 0.125)[:, :64])
                dz_ref[rows, 512 + h * RET_DV:512 + (h + 1) * RET_DV] = dv.astype(BF)
            dz_ref[rows, 0:256] = jnp.concatenate(dq_parts, axis=-1).astype(BF)
            dz_ref[rows, 256:512] = jnp.concatenate(dk_parts, axis=-1).astype(BF)

    rev = lambda i: nt - 1 - i
    hspec = pl.BlockSpec((RET_H, tt, LANE), lambda i: (0, rev(i), 0))
    cspec = pl.BlockSpec((RET_H, CHUNK, LANE), lambda i: (0, 0, 0))
    tab = pl.BlockSpec((tt, LANE), lambda i: (rev(i), 0))
    (dz,), lands = _hosted_call(
        body, "ret_bwd", (nt,),
        [pl.BlockSpec((tt, 512), lambda i: (rev(i), 0)), hspec, hspec,
         pl.BlockSpec((tt, 512), lambda i: (rev(i), 1)), hspec, tab, tab, cspec, cspec, cspec, cspec],
        [pl.BlockSpec((tt, 1024), lambda i: (rev(i), 0))], [jax.ShapeDtypeStruct((T, 1024), BF)],
        [pltpu.VMEM((RET_H, CHUNK, LANE), F32)], VMEM_BIG,
        (d_o, qr, kr, z_a, states, cos_t, sin_t, decay, zeta, xi, gcb), push)
    return dz, lands


def _fox_bwd(iend, q2, k, v, do, sub=512):
    H, T, _ = k.shape
    tb = 2 * sub

    def body(ie_ref, q_ref, do_ref, k_ref, v_ref, dq_ref, dk_ref, dv_ref, dk_sc, dv_sc):
        j = pl.program_id(1)
        n = ie_ref[pl.program_id(0), j]

        @pl.when(j == 0)
        def _():
            dq_ref[...] = jnp.zeros(dq_ref.shape, F32)

        kk, vv = k_ref[...], v_ref[...]
        dk_sc[...] = jnp.zeros(dk_sc.shape, F32)
        dv_sc[...] = jnp.zeros(dv_sc.shape, F32)
        krow = lax.broadcasted_iota(jnp.int32, (tb, sub), 0)
        qcol = lax.broadcasted_iota(jnp.int32, (tb, sub), 1)

        def step(i, shift):
            off = pl.multiple_of(i * sub, sub)
            qq = q_ref[pl.ds(off, sub), :]
            dd = do_ref[pl.ds(off, sub), :]
            p = jnp.exp(_nt(kk, qq))
            if shift is not None:
                p = jnp.where(qcol + shift >= krow, p, 0.0)
            ds = (p * _nt(vv, dd)).astype(BF)
            dv_sc[...] += _nn(p.astype(BF), dd)
            dk_sc[...] += _nn(ds, qq)
            dq_ref[pl.ds(off, sub), :] += _tn(ds, kk)

        off0 = pl.multiple_of(2 * j * sub, sub)
        q0, d0 = q_ref[pl.ds(off0, sub), :], do_ref[pl.ds(off0, sub), :]
        k0, v0 = k_ref[0:sub, :], v_ref[0:sub, :]
        p0 = jnp.where(qcol[0:sub, :] >= krow[0:sub, :], jnp.exp(_nt(k0, q0)), 0.0)
        ds0 = (p0 * _nt(v0, d0)).astype(BF)
        dv_sc[0:sub, :] += _nn(p0.astype(BF), d0)
        dk_sc[0:sub, :] += _nn(ds0, q0)
        dq_ref[pl.ds(off0, sub), :] += _tn(ds0, k0)
        step(2 * j + 1, sub)

        def loop_body(i, carry):
            step(i, None)
            return carry

        lax.fori_loop(2 * j + 2, n, loop_body, 0)
        dk_ref[...] = dk_sc[...]
        dv_ref[...] = dv_sc[...]

    blk = pl.BlockSpec((None, tb, LANE), lambda h, j, ie: (h, j, 0))
    full = pl.BlockSpec((None, T, LANE), lambda h, j, ie: (h, 0, 0))
    shp = jax.ShapeDtypeStruct((H, T, LANE), F32)
    return pl.pallas_call(
        body, name="fox_bwd",
        grid_spec=pltpu.PrefetchScalarGridSpec(
            num_scalar_prefetch=1, grid=(H, T // tb), in_specs=[full, full, blk, blk], out_specs=[full, blk, blk],
            scratch_shapes=[pltpu.VMEM((tb, LANE), F32), pltpu.VMEM((tb, LANE), F32)]),
        out_shape=[shp, shp, shp],
        compiler_params=_params(("arbitrary", "arbitrary"), VMEM_BIG),
    )(iend, q2, do, k, v)


def _fox_post_bwd(dq, dk, dv, z_a, z_ff, b_f, g_q, g_k, tm=256):
    T = z_a.shape[0]
    nt = T // tm

    def body(dq_ref, dk_ref, dv_ref, zf_ref, zff_ref, b_ref, gq_ref, gk_ref,
             dz_ref, dff_ref, dgq_ref, dgk_ref, db_ref, carry):
        i = pl.program_id(0)

        @pl.when(i == 0)
        def _():
            carry[...] = jnp.zeros(carry.shape, F32)
            dgq_ref[...] = jnp.zeros(dgq_ref.shape, F32)
            dgk_ref[...] = jnp.zeros(dgk_ref.shape, F32)
            db_ref[...] = jnp.zeros(db_ref.shape, F32)

        lane = lax.broadcasted_iota(jnp.int32, (tm, LANE), 1)
        zf = zf_ref[...]
        dcm = jnp.zeros((tm, LANE), F32)
        dq_parts, dk_parts, dv_parts = [], [], []
        gq_acc = jnp.zeros((1, 64), F32)
        gk_acc = jnp.zeros((1, 64), F32)
        for h in range(FOX_H):
            dqa, dka = dq_ref[h], dk_ref[h]
            dcm = jnp.where(lane == h, dqa[:, L_CQ:L_CQ + 1] - dka[:, L_CK:L_CK + 1], dcm)
            for src, dya, g_ref, scale, parts in ((0, dqa, gq_ref, 0.125, dq_parts), (512, dka, gk_ref, 1.0, dk_parts)):
                xh = zf[:, src + 64 * h:src + 64 * h + 64]
                r = lax.rsqrt(jnp.mean(xh * xh, axis=-1, keepdims=True) + EPS)
                xn = xh * r
                dy = dya[:, :FOX_D] * scale
                if src == 0:
                    gq_acc = gq_acc + jnp.sum(dy * xn, axis=0, keepdims=True)
                else:
                    gk_acc = gk_acc + jnp.sum(dy * xn, axis=0, keepdims=True)
                dxn = dy * g_ref[...]
                parts.append(r * (dxn - xn * jnp.mean(dxn * xn, axis=-1, keepdims=True)))
            dv_parts.append(dv_ref[h][:, :FOX_D])
        dz_ref[...] = jnp.concatenate(dq_parts + dk_parts + dv_parts, axis=-1).astype(BF)
        zpad = jnp.zeros((1, 64), F32)
        dgq_ref[...] += jnp.concatenate([gq_acc, zpad], axis=-1)
        dgk_ref[...] += jnp.concatenate([gk_acc, zpad], axis=-1)

        row = lax.broadcasted_iota(jnp.int32, (tm, tm), 0)
        col = lax.broadcasted_iota(jnp.int32, (tm, tm), 1)
        tri = (row <= col).astype(BF)
        hi, mid, lo = _split3(dcm)
        dlogf = _nn(tri, hi) + _nn(tri, mid) + _nn(tri, lo) + carry[...]
        carry[...] = dlogf[0:1, :]
        dff = jnp.where(lane < FOX_H, dlogf * _sigmoid(-(zff_ref[...] + b_ref[...])), 0.0)
        dff_ref[...] = dff.astype(BF)
        db_ref[...] += jnp.sum(dff, axis=0, keepdims=True)

    rev = lambda i: nt - 1 - i
    hsp = pl.BlockSpec((FOX_H, tm, LANE), lambda i: (0, rev(i), 0))
    small = lambda w: pl.BlockSpec((1, w), lambda i: (0, 0))
    return pl.pallas_call(
        body, name="fox_post_bwd", grid=(nt,),
        in_specs=[hsp, hsp, hsp, pl.BlockSpec((tm, 1536), lambda i: (rev(i), 1)),
                  pl.BlockSpec((tm, LANE), lambda i: (rev(i), 0)), small(LANE), small(64), small(64)],
        out_specs=[pl.BlockSpec((tm, 1536), lambda i: (rev(i), 0)), pl.BlockSpec((tm, LANE), lambda i: (rev(i), 0)),
                   small(LANE), small(LANE), small(LANE)],
        out_shape=[jax.ShapeDtypeStruct((T, 1536), BF), jax.ShapeDtypeStruct((T, LANE), BF),
                   jax.ShapeDtypeStruct((1, LANE), F32), jax.ShapeDtypeStruct((1, LANE), F32),
                   jax.ShapeDtypeStruct((1, LANE), F32)],
        scratch_shapes=[pltpu.VMEM((1, LANE), F32)],
        compiler_params=_params(("arbitrary",), VMEM_BIG),
    )(dq, dk, dv, z_a, z_ff, b_f, g_q, g_k)


def _in_bwd(dz_ret, dz_gt, dz_fox, dz_a, dz_ff, w_a, w_ff, x, g_mix, dx2, tm=256, push=None):
    T = x.shape[0]

    def body(r_ref, t_ref, f_ref, a_ref, ff_ref, wa_ref, wf_ref, x_ref, g_ref, dx2_ref, dx_ref, dg_ref):
        i = pl.program_id(0)

        @pl.when(i == 0)
        def _():
            dg_ref[...] = jnp.zeros(dg_ref.shape, F32)

        dh = (_nt(r_ref[...], wa_ref[:, C_RET:C_GT]) + _nt(t_ref[...], wa_ref[:, C_GT:C_FOX])
              + _nt(f_ref[...], wa_ref[:, C_FOX:C_A]) + _nt(a_ref[...], wa_ref[:, C_A:C_END])
              + _nt(ff_ref[...], wf_ref[...]))
        xv = x_ref[...]
        r = lax.rsqrt(jnp.mean(xv * xv, axis=-1, keepdims=True) + EPS)
        xn = xv * r
        dg_ref[...] += jnp.sum(dh * xn, axis=0, keepdims=True)
        dxn = dh * g_ref[...]
        dx_ref[...] = dx2_ref[...] + r * (dxn - xn * jnp.mean(dxn * xn, axis=-1, keepdims=True))

    row = lambda w: pl.BlockSpec((tm, w), lambda i: (i, 0))
    const = lambda shp: pl.BlockSpec(shp, lambda i: (0,) * len(shp))
    return _hosted_call(
        body, "in_bwd", (T // tm,),
        [row(1024), row(512), row(1536), row(2048), row(LANE), const((D_MODEL, C_END)),
         const((D_MODEL, LANE)), row(1024), const((1, 1024)), row(1024)],
        [row(1024), const((1, 1024))],
        [jax.ShapeDtypeStruct((T, 1024), F32), jax.ShapeDtypeStruct((1, 1024), F32)],
        [], VMEM_BIG, (dz_ret, dz_gt, dz_fox, dz_a, dz_ff, w_a, w_ff, x, g_mix, dx2), push)


def _mesh_pos():
    return lax.axis_index("x"), lax.axis_index("y"), lax.axis_index("c")


def _staged_place(src, name):
    stacked = src.ndim == 3
    R, C = src.shape[-2:]
    tr = _row_tile(R, 128, 16)
    n = R // tr
    assert n >= 2

    def body(s_ref, o_ref, buf, sem):
        i = pl.program_id(0)
        slot = i % 2
        x, y, _ = _mesh_pos()
        kme = 2 * x + y

        def out_copy(s, step):
            return pltpu.make_async_copy(buf.at[s], o_ref.at[kme, pl.ds(pl.multiple_of(step * tr, tr), tr), :], sem.at[s])

        @pl.when(i >= 2)
        def _():
            out_copy(slot, i - 2).wait()

        buf[slot] = (s_ref[kme] if stacked else s_ref[...]).astype(BF)
        out_copy(slot, i).start()

        @pl.when(i == n - 1)
        def _():
            out_copy(1 - slot, i - 1).wait()
            out_copy(slot, i).wait()

    in_spec = (pl.BlockSpec((N_CHIP, tr, C), lambda i: (0, i, 0)) if stacked else pl.BlockSpec((tr, C), lambda i: (i, 0)))
    return pl.pallas_call(
        body, name=name, grid=(n,), in_specs=[in_spec], out_specs=pl.BlockSpec(memory_space=pl.ANY),
        out_shape=jax.ShapeDtypeStruct((N_CHIP, R, C), BF),
        scratch_shapes=[pltpu.VMEM((2, tr, C), BF), pltpu.SemaphoreType.DMA((2,))],
        compiler_params=_params(("arbitrary",)),
    )(src)


def _push_copies(src, land, send_sem, recv_sem, receiving):
    x, y, c = _mesh_pos()
    kme = 2 * x + y
    cps = []
    for w in range(len(land)):
        for j, (px, py) in enumerate([(1 - x, y), (x, 1 - y), (1 - x, 1 - y)]):
            kpeer = 2 * px + py
            cps.append(pltpu.make_async_remote_copy(
                src_ref=land[w].at[kme] if src is None else src[w].at[kpeer],
                dst_ref=land[w].at[kpeer if receiving else kme],
                send_sem=send_sem.at[3 * w + j], recv_sem=recv_sem.at[3 * w + j],
                device_id=(px, py, c), device_id_type=MESH))
    return cps


def _gather_in_place(stacks, name):
    n = len(stacks)

    def body(*refs):
        land, send_sem, recv_sem = refs[n:2 * n], refs[2 * n], refs[2 * n + 1]
        for cp in _push_copies(None, land, send_sem, recv_sem, False):
            cp.start()
        for cp in _push_copies(None, land, send_sem, recv_sem, True):
            cp.wait_recv()
            cp.wait_send()

    anyspec = pl.BlockSpec(memory_space=pl.ANY)
    return pl.pallas_call(
        body, name=name, in_specs=[anyspec] * n, out_specs=[anyspec] * n,
        out_shape=[jax.ShapeDtypeStruct(s.shape, s.dtype) for s in stacks],
        input_output_aliases={i: i for i in range(n)},
        scratch_shapes=[pltpu.SemaphoreType.DMA((3 * n,)), pltpu.SemaphoreType.DMA((3 * n,))],
    )(*stacks)


def _scatter_partials(srcs, lands, small):
    n = len(srcs)

    def body(*refs):
        src, sv = refs[:n], refs[2 * n]
        land, svo = refs[2 * n + 1:3 * n + 1], refs[3 * n + 1]
        send_sem, recv_sem, ssend, srecv, sloc = refs[3 * n + 2:]
        x, y, c = _mesh_pos()
        me = 4 * x + 2 * y + c
        flips = [(b >> 2 & 1, b >> 1 & 1, b & 1) for b in range(1, 8)]
        others = [(1 - x if fx else x, 1 - y if fy else y, 1 - c if fc else c) for fx, fy, fc in flips]
        local = pltpu.make_async_copy(sv, svo.at[me], sloc)
        local.start()
        sends = []
        for j, (px, py, pc) in enumerate(others):
            cp = pltpu.make_async_remote_copy(
                src_ref=sv, dst_ref=svo.at[me], send_sem=ssend.at[j], recv_sem=srecv.at[j],
                device_id=(px, py, pc), device_id_type=MESH)
            cp.start()
            sends.append(cp)
        for cp in _push_copies(src, land, send_sem, recv_sem, False):
            cp.start()
            sends.append(cp)
        for j, (px, py, pc) in enumerate(others):
            pltpu.make_async_remote_copy(
                src_ref=sv, dst_ref=svo.at[4 * px + 2 * py + pc], send_sem=ssend.at[j], recv_sem=srecv.at[j],
                device_id=(px, py, pc), device_id_type=MESH).wait_recv()
        for cp in _push_copies(src, land, send_sem, recv_sem, True):
            cp.wait_recv()
        for cp in sends:
            cp.wait_send()
        local.wait()

    anyspec = pl.BlockSpec(memory_space=pl.ANY)
    return pl.pallas_call(
        body, name="scatter_partials",
        in_specs=[anyspec] * (2 * n + 1), out_specs=[anyspec] * (n + 1),
        out_shape=[jax.ShapeDtypeStruct(s.shape, s.dtype) for s in lands]
        + [jax.ShapeDtypeStruct((8,) + small.shape, small.dtype)],
        input_output_aliases={n + i: i for i in range(n)},
        scratch_shapes=[pltpu.SemaphoreType.DMA((3 * n,)), pltpu.SemaphoreType.DMA((3 * n,)),
                        pltpu.SemaphoreType.DMA((7,)), pltpu.SemaphoreType.DMA((7,)), pltpu.SemaphoreType.DMA],
    )(*srcs, *lands, small)


def _sibling_exchange(arrs):
    n = len(arrs)

    def body(*refs):
        ins, outs = refs[:n], refs[n:2 * n]
        send_sems, recv_sems = refs[2 * n:]
        x, y, c = _mesh_pos()
        cps = [pltpu.make_async_remote_copy(
            src_ref=ins[w], dst_ref=outs[w], send_sem=send_sems.at[w], recv_sem=recv_sems.at[w],
            device_id=(x, y, 1 - c), device_id_type=MESH) for w in range(n)]
        for cp in cps:
            cp.start()
        for cp in cps:
            cp.wait_recv()
        for cp in cps:
            cp.wait_send()

    anyspec = pl.BlockSpec(memory_space=pl.ANY)
    return pl.pallas_call(
        body, name="sibling_exchange",
        in_specs=[anyspec] * n, out_specs=[anyspec] * n,
        out_shape=[jax.ShapeDtypeStruct(a.shape, a.dtype) for a in arrs],
        scratch_shapes=[pltpu.SemaphoreType.DMA((n,)), pltpu.SemaphoreType.DMA((n,))],
    )(*arrs)


def _sum_stack(stack, name):
    _, R, C = stack.shape
    tr = _row_tile(R, 256, 16)

    def body(s_ref, o_ref):
        acc = s_ref[0].astype(F32)
        for k in range(1, N_CHIP):
            acc = acc + s_ref[k].astype(F32)
        o_ref[...] = acc

    return pl.pallas_call(
        body, name=name, grid=(R // tr,),
        in_specs=[pl.BlockSpec((N_CHIP, tr, C), lambda i: (0, i, 0))],
        out_specs=pl.BlockSpec((tr, C), lambda i: (i, 0)),
        out_shape=jax.ShapeDtypeStruct((R, C), F32),
        compiler_params=_params(("parallel",)),
    )(stack)


def _adam_math(w, g, m, v):
    m2 = ADAM_B1 * m + (1.0 - ADAM_B1) * g
    v2 = ADAM_B2 * v + (1.0 - ADAM_B2) * (g * g)
    m_hat = m2 / (1.0 - ADAM_B1 ** ADAM_STEP)
    v_hat = v2 / (1.0 - ADAM_B2 ** ADAM_STEP)
    delta = -ADAM_LR * (m_hat / (jnp.sqrt(v_hat) + ADAM_EPS) + ADAM_WD * w)
    return delta, m2, v2


def _adamw(w, m, v, s0, s1, name):
    R, C = w.shape
    tr = _row_tile(R, 128, 8)

    def body(w_ref, m_ref, v_ref, a_ref, b_ref, g_ref, d_ref, m2_ref, v2_ref):
        g = a_ref[...] + b_ref[...]
        delta, m2, v2 = _adam_math(w_ref[...], g, m_ref[...], v_ref[...])
        g_ref[...] = g
        d_ref[...] = delta
        m2_ref[...] = m2
        v2_ref[...] = v2

    spec = pl.BlockSpec((tr, C), lambda i: (i, 0))
    shp = jax.ShapeDtypeStruct((R, C), F32)
    return pl.pallas_call(
        body, name=name, grid=(R // tr,), in_specs=[spec] * 5, out_specs=[spec] * 4, out_shape=[shp] * 4,
        compiler_params=_params(("parallel",), VMEM_BIG),
    )(w, m, v, s0, s1)


def _adamw_small(w, m, v, gathered):
    def body(w_ref, m_ref, v_ref, s_ref, g_ref, d_ref, m2_ref, v2_ref):
        g = s_ref[0]
        for d in range(1, 8):
            g = g + s_ref[d]
        delta, m2, v2 = _adam_math(w_ref[...], g, m_ref[...], v_ref[...])
        g_ref[...] = g
        d_ref[...] = delta
        m2_ref[...] = m2
        v2_ref[...] = v2

    shp = jax.ShapeDtypeStruct(w.shape, F32)
    return pl.pallas_call(body, name="adamw_small", out_shape=[shp] * 4)(w, m, v, gathered)


SMALL = (("g_mix", 1024), ("g_ffn", 1024), ("g_ret_norm", 512), ("g_fox_q", 64), ("g_fox_k", 64), ("b_forget", 8))
SMALL_W = 3072


def _pack_small(parts):
    cols = []
    for (name, n) in SMALL:
        p = parts[name].reshape(1, -1)[:, :n]
        pad = -n % LANE
        cols.append(jnp.pad(p, ((0, 0), (0, pad))) if pad else p)
    used = sum(c.shape[1] for c in cols)
    cols.append(jnp.zeros((1, SMALL_W - used), F32))
    return jnp.concatenate(cols, axis=1)


def _unpack_small(vec):
    out, off = {}, 0
    for (name, n) in SMALL:
        out[name] = vec[:, off:off + n]
        off += n + (-n % LANE)
    return out


def kernel(x, g_mix, w_in, b_forget, g_ret_norm, w_ret_o, g_fox_q, g_fox_k, w_fox_o, w_out, g_ffn, w_gate, w_up, w_down, loss_target, m_g_mix, m_w_in, m_b_forget, m_g_ret_norm, m_w_ret_o, m_g_fox_q, m_g_fox_k, m_w_fox_o, m_w_out, m_g_ffn, m_w_gate, m_w_up, m_w_down, v_g_mix, v_w_in, v_b_forget, v_g_ret_norm, v_w_ret_o, v_g_fox_q, v_g_fox_k, v_w_fox_o, v_w_out, v_g_ffn, v_w_gate, v_w_up, v_w_down):
    T = x.shape[1]
    xs = x[0]
    tgt = loss_target[0]
    big_names = ("w_in", "w_ret_o", "w_fox_o", "w_out", "w_gate", "w_up", "w_down")
    tr = lambda a: jnp.swapaxes(a[0], 0, 1)
    big_w = dict(w_in=w_in[0], w_ret_o=w_ret_o[0], w_fox_o=w_fox_o[0], w_out=w_out[0], w_gate=tr(w_gate),
                 w_up=tr(w_up), w_down=w_down[0])
    big_m = dict(w_in=m_w_in[0], w_ret_o=m_w_ret_o[0], w_fox_o=m_w_fox_o[0], w_out=m_w_out[0], w_gate=tr(m_w_gate),
                 w_up=tr(m_w_up), w_down=m_w_down[0])
    big_v = dict(w_in=v_w_in[0], w_ret_o=v_w_ret_o[0], w_fox_o=v_w_fox_o[0], w_out=v_w_out[0], w_gate=tr(v_w_gate),
                 w_up=tr(v_w_up), w_down=v_w_down[0])
    small_w = dict(g_mix=g_mix, g_ffn=g_ffn, g_ret_norm=g_ret_norm, g_fox_q=g_fox_q, g_fox_k=g_fox_k, b_forget=b_forget)
    small_m = dict(g_mix=m_g_mix, g_ffn=m_g_ffn, g_ret_norm=m_g_ret_norm, g_fox_q=m_g_fox_q, g_fox_k=m_g_fox_k,
                   b_forget=m_b_forget)
    small_v = dict(g_mix=v_g_mix, g_ffn=v_g_ffn, g_ret_norm=v_g_ret_norm, g_fox_q=v_g_fox_q, g_fox_k=v_g_fox_k,
                   b_forget=v_b_forget)

    stacks = {n: _staged_place(big_w[n], "place_" + n) for n in big_names}
    (s_in,) = _gather_in_place([stacks["w_in"]], "gather_w_in")
    w_a, w_ff = _assemble_w_in(s_in)
    b_pad = jnp.pad(b_forget, ((0, 0), (0, LANE - FOX_H)))
    cos_t, sin_t = _rope_tables(T)
    consts = _ret_consts()

    h = _rms_cast(xs, g_mix)
    z_a, (s_gate, s_up) = _mm_nn(h, w_a, "proj_in", push=(None, [stacks["w_gate"], stacks["w_up"]]))
    z_ff, _ = _mm_nn(h, w_ff, "proj_ff")
    (qr, kr, qf, kf, vf, c_cum, qmax, kmax), (s_down, s_ro, s_fo, s_out) = _mix_prep(
        z_a, z_ff, cos_t, sin_t, b_pad, g_fox_q, g_fox_k,
        push=(None, [stacks["w_down"], stacks["w_ret_o"], stacks["w_fox_o"], stacks["w_out"]]))
    jstart, iend = _prune_tables(c_cum, qmax, kmax, 512)
    o_raw, u_r, states = _ret_fwd(qr, kr, z_a, g_ret_norm, consts)
    o_fox, q2 = _fox_fwd(jstart, qf, kf, vf)
    y_r, y_f, mrg, x2, h2, o_cat = _merge_out(u_r, o_fox, z_a, xs, g_ffn, s_ro, s_fo, s_out)
    sa, sb, act, dy, loss_vec = _ffn_fwd(h2, x2, tgt, s_gate, s_up, s_down)
    loss = lax.psum(0.5 / D_MODEL * jnp.sum(loss_vec), ("x", "y", "c"))

    def scatter_job(grads, names):
        return (grads, [_staged_place(g, "place_g_" + n) for g, n in zip(grads, names)])

    dgp, dup, dx2, dg_ffn = _ffn_bwd(dy, sa, sb, x2, g_ffn, s_gate, s_up, s_down)
    g_gate, g_up, g_down = (_grad_astack(dgp, h2, "gw_gate"), _grad_astack(dup, h2, "gw_up"),
                            _grad_astack(act, dy, "gw_down"))
    (d_yr, d_yf, dz_gt, dz_a, d_o, do_fox, dg_ret), (r_gate, r_up) = _out_bwd(
        dx2, z_a, y_r, y_f, o_raw, o_fox, g_ret_norm, s_ro, s_fo, s_out,
        push=scatter_job([g_gate, g_up], ["w_gate", "w_up"]))
    dz_ret, (r_down,) = _ret_bwd(d_o, qr, kr, z_a, states, cos_t, sin_t, consts, push=scatter_job([g_down], ["w_down"]))
    dq_f, dk_f, dv_f = _fox_bwd(iend, q2, kf, vf, do_fox)
    dz_fox, dz_ff, dg_q, dg_k, db_f = _fox_post_bwd(dq_f, dk_f, dv_f, z_a, z_ff, b_pad, g_fox_q, g_fox_k)
    g_mid = [_grad_colstack(u_r, d_yr, "gw_ret_o", 256), _grad_colstack(o_cat, d_yf, "gw_fox_o", 256),
             _grad_plain(mrg, dx2, "gw_out", BF).reshape(N_CHIP, 256, D_MODEL)]
    (grad_x, dg_mix), (r_ro, r_fo, r_out) = _in_bwd(dz_ret, dz_gt, dz_fox, dz_a, dz_ff, w_a, w_ff, xs, g_mix, dx2,
                                                     push=scatter_job(g_mid, ["w_ret_o", "w_fox_o", "w_out"]))
    g_in = _pack_g_in(_grad_plain(h, dz_ret, "gw_in_ret", F32), _grad_plain(h, dz_gt, "gw_in_gt", F32),
                      _grad_plain(h, dz_fox, "gw_in_fox", F32, tn=768), _grad_plain(h, dz_a, "gw_in_a", F32),
                      _grad_plain(h, dz_ff, "gw_in_ff", F32))
    small_g = _pack_small(dict(g_mix=dg_mix, g_ffn=dg_ffn, g_ret_norm=dg_ret, g_fox_q=dg_q, g_fox_k=dg_k, b_forget=db_f))

    r_in, small_all = _scatter_partials(*scatter_job([g_in], ["w_in"]), small_g)
    sums = [_sum_stack(r, "sum_" + n) for r, n in zip([r_in, r_ro, r_fo, r_out, r_gate, r_up, r_down], big_names)]
    sib = _sibling_exchange(sums)
    big_out = {n: _adamw(big_w[n], big_m[n], big_v[n], sums[i], sib[i], "adamw_" + n) for i, n in enumerate(big_names)}
    sg, sd, sm, sv = _adamw_small(_pack_small(small_w), _pack_small(small_m), _pack_small(small_v), small_all)
    small_out = [_unpack_small(t) for t in (sg, sd, sm, sv)]

    order = ("g_mix", "w_in", "b_forget", "g_ret_norm", "w_ret_o", "g_fox_q", "g_fox_k", "w_fox_o", "w_out", "g_ffn",
             "w_gate", "w_up", "w_down")
    outs = [loss, grad_x[None]]
    for idx in range(4):
        for n in order:
            if n in ("w_gate", "w_up"):
                outs.append(jnp.swapaxes(big_out[n][idx], 0, 1)[None])
            else:
                outs.append(big_out[n][idx][None] if n in big_out else small_out[idx][n])
    return tuple(outs)
```

```python
import functools
import math

import numpy as np
import jax
import jax.numpy as jnp
from jax import lax
from jax.experimental import pallas as pl
from jax.experimental.pallas import tpu as pltpu

F32 = jnp.float32
BF = jnp.bfloat16
MESH = pl.DeviceIdType.MESH

D_MODEL = 1024
D_FF = 2816
N_CHIP = 4
FF_SH = D_FF // N_CHIP
IN_COLS = 5128
IN_SH = IN_COLS // N_CHIP
RET_H, RET_DV = 4, 128
FOX_H, FOX_D = 8, 64
CHUNK = 128
EPS = 1e-6
NEG = -1e30
LANE = 128
C_RET, C_GT, C_FOX, C_A, C_END = 0, 1024, 1536, 3072, 5120
L_CQ, L_CK, L_LSE, L_MAX = 64, 67, 70, 73

ADAM_LR, ADAM_B1, ADAM_B2, ADAM_EPS, ADAM_WD, ADAM_STEP = 0.001, 0.9, 0.999, 1e-08, 0.01, 10
VMEM_BIG = 56 * 1024 * 1024
VMEM_HUGE = 60 * 1024 * 1024
GRAD_TK = 2048
FFN_TM = 512


def _nn(a, b):
    return lax.dot_general(a, b, (((1,), (0,)), ((), ())), preferred_element_type=F32)


def _nt(a, b):
    return lax.dot_general(a, b, (((1,), (1,)), ((), ())), preferred_element_type=F32)


def _tn(a, b):
    return lax.dot_general(a, b, (((0,), (0,)), ((), ())), preferred_element_type=F32)


def _split3(x):
    hi = x.astype(BF)
    r = x - hi.astype(F32)
    mid = r.astype(BF)
    lo = (r - mid.astype(F32)).astype(BF)
    return hi, mid, lo


def _sigmoid(x):
    return 0.5 * jnp.tanh(0.5 * x) + 0.5


def _swap32(x):
    lane = lax.broadcasted_iota(jnp.int32, x.shape, 1)
    return jnp.where(lane < 32, pltpu.roll(x, 96, 1), pltpu.roll(x, 32, 1))


def _params(sem, vmem=None):
    return pltpu.CompilerParams(dimension_semantics=sem, vmem_limit_bytes=vmem)


def _row_tile(rows, cap, mult):
    return max(d for d in range(mult, cap + 1, mult) if rows % d == 0)


def _assemble_w_in(stack, tr=256):
    def body(s_ref, a_ref, f_ref):
        full = jnp.concatenate([s_ref[k].astype(F32) for k in range(N_CHIP)], axis=-1)
        a_ref[...] = jnp.concatenate([full[:, :3072], full[:, 3080:IN_COLS]], axis=-1).astype(BF)
        f_ref[...] = jnp.concatenate([full[:, 3072:3080], jnp.zeros((tr, LANE - FOX_H), F32)], axis=-1).astype(BF)

    return pl.pallas_call(
        body, name="assemble_w_in", grid=(D_MODEL // tr,),
        in_specs=[pl.BlockSpec((N_CHIP, tr, IN_SH), lambda i: (0, i, 0))],
        out_specs=[pl.BlockSpec((tr, C_END), lambda i: (i, 0)), pl.BlockSpec((tr, LANE), lambda i: (i, 0))],
        out_shape=[jax.ShapeDtypeStruct((D_MODEL, C_END), BF), jax.ShapeDtypeStruct((D_MODEL, LANE), BF)],
        compiler_params=_params(("parallel",), VMEM_BIG),
    )(stack)


def _pack_g_in(g_ret, g_gt, g_fox, g_a, g_ff, tr=256):
    def body(r_ref, t_ref, x_ref, a_ref, f_ref, o_ref):
        full = jnp.concatenate([r_ref[...], t_ref[...], x_ref[...], f_ref[...][:, :FOX_H], a_ref[...]], axis=-1)
        for k in range(N_CHIP):
            o_ref[k] = full[:, k * IN_SH:(k + 1) * IN_SH].astype(BF)

    def spec(w):
        return pl.BlockSpec((tr, w), lambda i: (i, 0))

    return pl.pallas_call(
        body, name="pack_g_in", grid=(D_MODEL // tr,),
        in_specs=[spec(1024), spec(512), spec(1536), spec(2048), spec(LANE)],
        out_specs=pl.BlockSpec((N_CHIP, tr, IN_SH), lambda i: (0, i, 0)),
        out_shape=jax.ShapeDtypeStruct((N_CHIP, D_MODEL, IN_SH), BF),
        compiler_params=_params(("parallel",), VMEM_BIG),
    )(g_ret, g_gt, g_fox, g_a, g_ff)


def _rms_cast(x, g, tm=512):
    T = x.shape[0]

    def body(x_ref, g_ref, o_ref):
        xv = x_ref[...]
        r = lax.rsqrt(jnp.mean(xv * xv, axis=-1, keepdims=True) + EPS)
        o_ref[...] = (xv * r * g_ref[...]).astype(BF)

    return pl.pallas_call(
        body, name="rms_cast", grid=(T // tm,),
        in_specs=[pl.BlockSpec((tm, D_MODEL), lambda i: (i, 0)), pl.BlockSpec((1, D_MODEL), lambda i: (0, 0))],
        out_specs=pl.BlockSpec((tm, D_MODEL), lambda i: (i, 0)),
        out_shape=jax.ShapeDtypeStruct((T, D_MODEL), BF),
        compiler_params=_params(("parallel",)),
    )(x, g)


def _hosted_call(body, name, grid, in_specs, out_specs, out_shape, scratch_shapes, vmem, args, push):
    sem = ("arbitrary",) * len(grid)
    if push is None:
        res = pl.pallas_call(body, name=name, grid=grid, in_specs=in_specs, out_specs=out_specs, out_shape=out_shape,
                             scratch_shapes=scratch_shapes, compiler_params=_params(sem, vmem))(*args)
        return list(res), []
    srcs, lands = push
    ns, nl, n_in, n_out = (0 if srcs is None else len(srcs)), len(lands), len(in_specs), len(out_specs)
    n_scr = len(scratch_shapes)

    def wrapped(*refs):
        pos = n_in + ns + nl
        ins, x_in = refs[:n_in], refs[n_in:pos]
        outs, x_out = refs[pos:pos + n_out], refs[pos + n_out:pos + n_out + nl]
        scr = refs[pos + n_out + nl:pos + n_out + nl + n_scr]
        ssem, rsem = refs[-2], refs[-1]
        src = None if srcs is None else x_in[:ns]
        ids = [pl.program_id(a) for a in range(len(grid))]
        first = functools.reduce(lambda p, q: p & q, [ids[a] == 0 for a in range(len(grid))])
        last = functools.reduce(lambda p, q: p & q, [ids[a] == grid[a] - 1 for a in range(len(grid))])

        @pl.when(first)
        def _():
            for cp in _push_copies(src, x_out, ssem, rsem, False):
                cp.start()

        body(*ins, *outs, *scr)

        @pl.when(last)
        def _():
            for cp in _push_copies(src, x_out, ssem, rsem, True):
                cp.wait_recv()
                cp.wait_send()

    anyspec = pl.BlockSpec(memory_space=pl.ANY)
    extra = ([] if srcs is None else list(srcs)) + list(lands)
    res = pl.pallas_call(
        wrapped, name=name, grid=grid,
        in_specs=list(in_specs) + [anyspec] * len(extra), out_specs=list(out_specs) + [anyspec] * nl,
        out_shape=list(out_shape) + [jax.ShapeDtypeStruct(a.shape, a.dtype) for a in lands],
        input_output_aliases={n_in + ns + i: n_out + i for i in range(nl)},
        scratch_shapes=list(scratch_shapes) + [pltpu.SemaphoreType.DMA((3 * nl,)), pltpu.SemaphoreType.DMA((3 * nl,))],
        compiler_params=_params(sem, vmem),
    )(*args, *extra)
    return list(res[:n_out]), list(res[n_out:])


def _mm_nn(a, b, name, tm=512, tn=1024, push=None):
    M, K = a.shape
    N = b.shape[1]
    tn = min(tn, N)

    def body(a_ref, b_ref, o_ref):
        o_ref[...] = _nn(a_ref[...], b_ref[...])

    (out,), lands = _hosted_call(
        body, name, (N // tn, M // tm),
        [pl.BlockSpec((tm, K), lambda j, i: (i, 0)), pl.BlockSpec((K, tn), lambda j, i: (0, j))],
        [pl.BlockSpec((tm, tn), lambda j, i: (i, j))], [jax.ShapeDtypeStruct((M, N), F32)], [], None, (a, b), push)
    return out, lands


def _mm_tn(a, b, name, grid, a_spec, b_spec, o_spec, out_shape, acc_shape):
    nk = grid[-1]

    def body(a_ref, b_ref, o_ref, acc):
        k = pl.program_id(len(grid) - 1)

        @pl.when(k == 0)
        def _():
            acc[...] = jnp.zeros(acc.shape, F32)

        acc[...] += _tn(a_ref[...].astype(BF), b_ref[...].astype(BF))

        @pl.when(k == nk - 1)
        def _():
            o_ref[...] = acc[...].astype(o_ref.dtype)

    return pl.pallas_call(
        body, name=name, grid=grid, in_specs=[a_spec, b_spec], out_specs=o_spec, out_shape=out_shape,
        scratch_shapes=[pltpu.VMEM(acc_shape, F32)],
        compiler_params=_params(("parallel",) * (len(grid) - 1) + ("arbitrary",), VMEM_BIG),
    )(a, b)


def _grad_plain(a, b, name, out_dtype, tk=GRAD_TK, tn=1024):
    T, M = a.shape
    N = b.shape[1]
    tn = min(tn, N)
    return _mm_tn(a, b, name, (N // tn, T // tk),
                  pl.BlockSpec((tk, M), lambda j, k: (k, 0)), pl.BlockSpec((tk, tn), lambda j, k: (k, j)),
                  pl.BlockSpec((M, tn), lambda j, k: (0, j)), jax.ShapeDtypeStruct((M, N), out_dtype), (M, tn))


def _grad_colstack(a, b, name, wcol, tk=GRAD_TK):
    T, M = a.shape
    S = b.shape[1] // wcol
    return _mm_tn(a, b, name, (S, T // tk),
                  pl.BlockSpec((tk, M), lambda s, k: (k, 0)), pl.BlockSpec((tk, wcol), lambda s, k: (k, s)),
                  pl.BlockSpec((None, M, wcol), lambda s, k: (s, 0, 0)),
                  jax.ShapeDtypeStruct((S, M, wcol), BF), (M, wcol))


def _grad_bstack(a, b, name, tk=GRAD_TK):
    T, M = a.shape
    S, _, n = b.shape
    return _mm_tn(a, b, name, (S, T // tk),
                  pl.BlockSpec((tk, M), lambda s, k: (k, 0)), pl.BlockSpec((None, tk, n), lambda s, k: (s, k, 0)),
                  pl.BlockSpec((None, M, n), lambda s, k: (s, 0, 0)),
                  jax.ShapeDtypeStruct((S, M, n), BF), (M, n))


def _grad_astack(a, b, name, tk=GRAD_TK):
    S, T, m = a.shape
    N = b.shape[1]
    return _mm_tn(a, b, name, (S, T // tk),
                  pl.BlockSpec((None, tk, m), lambda s, k: (s, k, 0)), pl.BlockSpec((tk, N), lambda s, k: (k, 0)),
                  pl.BlockSpec((None, m, N), lambda s, k: (s, 0, 0)),
                  jax.ShapeDtypeStruct((S, m, N), BF), (m, N))


def _rope_tables(T):
    half = 32
    pos = np.arange(T, dtype=np.float32)
    inv_freq = (np.float32(1.0) / (np.float32(10000.0) ** (np.arange(half, dtype=np.float32) / np.float32(half)))).astype(np.float32)
    ang = (pos[:, None] * inv_freq[None, :]).astype(np.float32)
    cos, sin = np.cos(ang).astype(np.float32), np.sin(ang).astype(np.float32)
    z = np.zeros((T, 64), np.float32)
    return (jnp.asarray(np.concatenate([cos, cos, z], axis=-1)), jnp.asarray(np.concatenate([-sin, sin, z], axis=-1)))


def _ret_consts():
    h = np.arange(RET_H, dtype=np.float32)
    log_g = np.log1p(-(np.float32(2.0) ** (-5.0 - h))).astype(np.float32)
    idx = np.arange(CHUNK, dtype=np.float32)
    diff = idx[:, None] - idx[None, :]
    decay = np.where(diff[None] >= 0, np.exp(np.maximum(diff, 0.0)[None] * log_g[:, None, None]), 0.0)
    zeta = np.exp((CHUNK - 1.0 - idx)[None, :] * log_g[:, None])
    xi = np.exp((idx + 1.0)[None, :] * log_g[:, None])
    gc = np.exp(CHUNK * log_g)
    bc = lambda v: np.broadcast_to(v[:, :, None], (RET_H, CHUNK, LANE)).astype(np.float32)
    gcb = np.broadcast_to(gc[:, None, None], (RET_H, CHUNK, LANE)).astype(np.float32)
    return (jnp.asarray(decay.astype(np.float32)), jnp.asarray(bc(zeta)), jnp.asarray(bc(xi)), jnp.asarray(gcb))


def _mix_prep(z_a, z_ff, cos_t, sin_t, b_f, g_q, g_k, tm=256, push=None):
    T = z_a.shape[0]

    def body(zqk_ref, zf_ref, zff_ref, cos_ref, sin_ref, b_ref, gq_ref, gk_ref,
             qr_ref, kr_ref, qf_ref, kf_ref, vf_ref, c_ref, qmax_ref, kmax_ref, carry):
        i = pl.program_id(0)

        @pl.when(i == 0)
        def _():
            carry[...] = jnp.zeros(carry.shape, F32)
            qmax_ref[...] = jnp.zeros(qmax_ref.shape, F32)
            kmax_ref[...] = jnp.zeros(kmax_ref.shape, F32)

        lane = lax.broadcasted_iota(jnp.int32, (tm, LANE), 1)
        lane1 = lax.broadcasted_iota(jnp.int32, (1, LANE), 1)
        zpad = jnp.zeros((tm, 64), F32)
        cosv, sinv = cos_ref[...], sin_ref[...]
        zqk = zqk_ref[...]
        for h in range(RET_H):
            for src, dst, scale in ((0, qr_ref, 1.0), (256, kr_ref, 0.125)):
                xh = jnp.concatenate([zqk[:, src + 64 * h: src + 64 * h + 64], zpad], axis=-1)
                rot = xh * cosv + _swap32(xh) * sinv
                dst[h] = (rot * scale).astype(BF)

        lf_in = zff_ref[...] + b_ref[...]
        logf = jnp.minimum(lf_in, 0.0) - jnp.log(1.0 + jnp.exp(-jnp.abs(lf_in)))
        row = lax.broadcasted_iota(jnp.int32, (tm, tm), 0)
        col = lax.broadcasted_iota(jnp.int32, (tm, tm), 1)
        tri = (row >= col).astype(BF)
        hi, mid, lo = _split3(logf)
        cs = _nn(tri, hi) + _nn(tri, mid) + _nn(tri, lo) + carry[...]
        carry[...] = cs[tm - 1:tm, :]
        c_ref[...] = cs

        zf = zf_ref[...]
        one = jnp.ones((tm, LANE), F32)
        qmax, kmax = qmax_ref[...], kmax_ref[...]
        for h in range(FOX_H):
            c = cs[:, h:h + 1]
            chi, cmid, clo = [t.astype(F32) for t in _split3(c)]
            qh = zf[:, 64 * h:64 * h + 64]
            kh = zf[:, 512 + 64 * h:512 + 64 * h + 64]
            vh = zf[:, 1024 + 64 * h:1024 + 64 * h + 64]
            qn = qh * lax.rsqrt(jnp.mean(qh * qh, axis=-1, keepdims=True) + EPS) * gq_ref[...] * 0.125
            kn = kh * lax.rsqrt(jnp.mean(kh * kh, axis=-1, keepdims=True) + EPS) * gk_ref[...]
            qa = jnp.concatenate([qn, zpad], axis=-1)
            qa = jnp.where(lane == L_CQ, chi, jnp.where(lane == L_CQ + 1, cmid, jnp.where(lane == L_CQ + 2, clo, qa)))
            qa = jnp.where((lane >= L_CK) & (lane < L_CK + 3), one, qa)
            ka = jnp.concatenate([kn, zpad], axis=-1)
            ka = jnp.where(lane == L_CK, -chi, jnp.where(lane == L_CK + 1, -cmid, jnp.where(lane == L_CK + 2, -clo, ka)))
            ka = jnp.where(((lane >= L_CQ) & (lane < L_CQ + 3)) | ((lane >= L_LSE) & (lane < L_MAX + 3)), one, ka)
            va = jnp.concatenate([vh, zpad], axis=-1)
            va = jnp.where((lane >= 64) & (lane < 67), one, va)
            qf_ref[h] = qa.astype(BF)
            kf_ref[h] = ka.astype(BF)
            vf_ref[h] = va.astype(BF)
            qmax = jnp.where(lane1 == h, jnp.maximum(qmax, jnp.max(jnp.sum(qn * qn, axis=-1, keepdims=True), axis=0,
                                                                   keepdims=True)), qmax)
            kmax = jnp.where(lane1 == h, jnp.maximum(kmax, jnp.max(jnp.sum(kn * kn, axis=-1, keepdims=True), axis=0,
                                                                   keepdims=True)), kmax)
        qmax_ref[...] = qmax
        kmax_ref[...] = kmax

    hspec4 = pl.BlockSpec((RET_H, tm, LANE), lambda i: (0, i, 0))
    hspec8 = pl.BlockSpec((FOX_H, tm, LANE), lambda i: (0, i, 0))
    small = lambda w: pl.BlockSpec((1, w), lambda i: (0, 0))
    return _hosted_call(
        body, "mix_prep", (T // tm,),
        [pl.BlockSpec((tm, 512), lambda i: (i, 0)), pl.BlockSpec((tm, 1536), lambda i: (i, 1)),
         pl.BlockSpec((tm, LANE), lambda i: (i, 0)), pl.BlockSpec((tm, LANE), lambda i: (i, 0)),
         pl.BlockSpec((tm, LANE), lambda i: (i, 0)), small(LANE), small(64), small(64)],
        [hspec4, hspec4, hspec8, hspec8, hspec8, pl.BlockSpec((tm, LANE), lambda i: (i, 0)), small(LANE), small(LANE)],
        [jax.ShapeDtypeStruct((RET_H, T, LANE), BF)] * 2 + [jax.ShapeDtypeStruct((FOX_H, T, LANE), BF)] * 3
        + [jax.ShapeDtypeStruct((T, LANE), F32), jax.ShapeDtypeStruct((1, LANE), F32), jax.ShapeDtypeStruct((1, LANE), F32)],
        [pltpu.VMEM((1, LANE), F32)], VMEM_BIG, (z_a, z_a, z_ff, cos_t, sin_t, b_f, g_q, g_k), push)


def _ret_fwd(qr, kr, z_a, g_ret, consts, tt=512):
    T = z_a.shape[0]
    nch = tt // CHUNK
    decay, zeta, xi, gcb = consts

    def body(q_ref, k_ref, v_ref, gt_ref, g_ref, d_ref, ze_ref, xi_ref, gc_ref, o_ref, u_ref, st_ref, r_sc):
        i = pl.program_id(0)

        @pl.when(i == 0)
        def _():
            r_sc[...] = jnp.zeros(r_sc.shape, F32)

        for c in range(nch):
            rows = slice(c * CHUNK, (c + 1) * CHUNK)
            for h in range(RET_H):
                cols = slice(h * RET_DV, (h + 1) * RET_DV)
                q, k = q_ref[h, rows, :], k_ref[h, rows, :]
                v32 = v_ref[rows, cols]
                r = r_sc[h]
                st_ref[h, rows, :] = r
                s = _nt(q, k) * d_ref[h]
                o = _nn(s.astype(BF), v32.astype(BF)) + _nn(q, r.astype(BF)) * xi_ref[h]
                r_sc[h] = gc_ref[h] * r + _tn(k, (v32 * ze_ref[h]).astype(BF))
                o_ref[rows, cols] = o
                mu = jnp.mean(o, axis=-1, keepdims=True)
                xc = o - mu
                on = xc * lax.rsqrt(jnp.mean(xc * xc, axis=-1, keepdims=True) + EPS)
                gt = gt_ref[rows, cols]
                u_ref[rows, cols] = (gt * _sigmoid(gt) * (on * g_ref[:, cols])).astype(BF)

    hspec = pl.BlockSpec((RET_H, tt, LANE), lambda i: (0, i, 0))
    cspec = pl.BlockSpec((RET_H, CHUNK, LANE), lambda i: (0, 0, 0))
    return pl.pallas_call(
        body, name="ret_fwd", grid=(T // tt,),
        in_specs=[hspec, hspec, pl.BlockSpec((tt, 512), lambda i: (i, 1)), pl.BlockSpec((tt, 512), lambda i: (i, 2)),
                  pl.BlockSpec((1, 512), lambda i: (0, 0)), cspec, cspec, cspec, cspec],
        out_specs=[pl.BlockSpec((tt, 512), lambda i: (i, 0)), pl.BlockSpec((tt, 512), lambda i: (i, 0)), hspec],
        out_shape=[jax.ShapeDtypeStruct((T, 512), F32), jax.ShapeDtypeStruct((T, 512), BF),
                   jax.ShapeDtypeStruct((RET_H, T, LANE), F32)],
        scratch_shapes=[pltpu.VMEM((RET_H, CHUNK, LANE), F32)],
        compiler_params=_params(("arbitrary",), VMEM_BIG),
    )(qr, kr, z_a, z_a, g_ret, decay, zeta, xi, gcb)


PRUNE_LOG = -110.0


def _prune_tables(c, qmax, kmax, sub):
    n = c.shape[0] // sub
    u = jnp.sqrt(qmax[0, :FOX_H] * kmax[0, :FOX_H]) * 1.02 + 0.5
    first = c[0::sub, :FOX_H].T
    last = c[sub - 1::sub, :FOX_H].T
    blk = jnp.arange(n, dtype=jnp.int32)
    needed = (2.0 * u[:, None, None] + first[:, :, None] - last[:, None, :] >= PRUNE_LOG) | (blk[None, :] >= blk[:, None])[None]
    jlo = jnp.argmax(needed, axis=2).astype(jnp.int32)
    jstart = jnp.minimum(jlo[:, 0::2], jlo[:, 1::2]) // 2
    need_q = jlo[:, None, :] <= (2 * jnp.arange(n // 2, dtype=jnp.int32) + 1)[None, :, None]
    iend = n - jnp.argmax(need_q[:, :, ::-1], axis=2).astype(jnp.int32)
    return jstart.astype(jnp.int32), iend.astype(jnp.int32)


def _fox_fwd(jstart, q, k, v, sub=512):
    H, T, _ = q.shape
    tb = 2 * sub

    def body(js_ref, q_ref, k_ref, v_ref, o_ref, q2_ref, mx_sc, acc_sc):
        i = pl.program_id(1)
        j0 = js_ref[pl.program_id(0), i]
        lane = lax.broadcasted_iota(jnp.int32, (sub, LANE), 1)
        row = lax.broadcasted_iota(jnp.int32, (sub, sub), 0)
        col = lax.broadcasted_iota(jnp.int32, (sub, sub), 1)
        causal = row >= col
        qs = [q_ref[0:sub, :], q_ref[sub:tb, :]]
        d0 = pl.multiple_of(i * tb, tb)
        d1 = pl.multiple_of(i * tb + sub, sub)

        def lane_max(s):
            m = s[:, 0:LANE]
            for c in range(1, s.shape[1] // LANE):
                m = jnp.maximum(m, s[:, c * LANE:(c + 1) * LANE])
            return m

        mx_sc[...] = jnp.full(mx_sc.shape, NEG, F32)

        def max_body(j, carry):
            kb = k_ref[pl.ds(pl.multiple_of(j * tb, tb), tb), :]
            for a in range(2):
                mx_sc[a] = jnp.maximum(mx_sc[a], lane_max(_nt(qs[a], kb)))
            return carry

        lax.fori_loop(j0, i, max_body, 0)
        k0, k1 = k_ref[pl.ds(d0, sub), :], k_ref[pl.ds(d1, sub), :]
        v0, v1 = v_ref[pl.ds(d0, sub), :], v_ref[pl.ds(d1, sub), :]
        mx = [jnp.maximum(mx_sc[0], lane_max(jnp.where(causal, _nt(qs[0], k0), NEG))),
              jnp.maximum(jnp.maximum(mx_sc[1], lane_max(_nt(qs[1], k0))),
                          lane_max(jnp.where(causal, _nt(qs[1], k1), NEG)))]
        ms = [jnp.max(t, axis=1, keepdims=True) for t in mx]

        def put3(base, first, val):
            hi, mid, lo = _split3(val)
            return jnp.where(lane == first, hi, jnp.where(lane == first + 1, mid, jnp.where(lane == first + 2, lo, base)))

        qm = [put3(qs[a], L_MAX, -ms[a]) for a in range(2)]

        acc_sc[...] = jnp.zeros(acc_sc.shape, F32)

        def acc_body(j, carry):
            off = pl.multiple_of(j * tb, tb)
            kb, vb = k_ref[pl.ds(off, tb), :], v_ref[pl.ds(off, tb), :]
            for a in range(2):
                acc_sc[a] += _nn(jnp.exp(_nt(qm[a], kb)).astype(BF), vb)
            return carry

        lax.fori_loop(j0, i, acc_body, 0)

        def pv(qa, kk, vv, masked):
            p = jnp.exp(_nt(qa, kk))
            if masked:
                p = jnp.where(causal, p, 0.0)
            return _nn(p.astype(BF), vv)

        accs = [acc_sc[0] + pv(qm[0], k0, v0, True),
                acc_sc[1] + pv(qm[1], k0, v0, False) + pv(qm[1], k1, v1, True)]
        for a in range(2):
            rows = slice(a * sub, (a + 1) * sub)
            l = accs[a][:, 64:65]
            o_ref[rows, :] = jnp.where(lane < 64, accs[a] / l, 0.0)
            q2_ref[rows, :] = put3(qs[a], L_LSE, -(ms[a] + jnp.log(l)))

    blk = pl.BlockSpec((None, tb, LANE), lambda h, i, js: (h, i, 0))
    full = pl.BlockSpec((None, T, LANE), lambda h, i, js: (h, 0, 0))
    return pl.pallas_call(
        body, name="fox_fwd",
        grid_spec=pltpu.PrefetchScalarGridSpec(
            num_scalar_prefetch=1, grid=(H, T // tb), in_specs=[blk, full, full], out_specs=[blk, blk],
            scratch_shapes=[pltpu.VMEM((2, sub, LANE), F32), pltpu.VMEM((2, sub, LANE), F32)]),
        out_shape=[jax.ShapeDtypeStruct((H, T, LANE), F32), jax.ShapeDtypeStruct((H, T, LANE), BF)],
        compiler_params=_params(("parallel", "arbitrary"), VMEM_BIG),
    )(jstart, q, k, v)


def _merge_out(u_r, o_fox, z_a, x, g_ffn, w_ro, w_fo, w_out, tm=256):
    T = x.shape[0]

    def body(u_ref, of_ref, ar_ref, af_ref, x_ref, g_ref, wro_ref, wfo_ref, wout_ref,
             yr_ref, yf_ref, m_ref, x2_ref, h2_ref, oc_ref):
        u = u_ref[...]
        oc = jnp.concatenate([of_ref[h][:, :FOX_D] for h in range(FOX_H)], axis=-1).astype(BF)
        oc_ref[...] = oc
        yr = jnp.concatenate([_nn(u, wro_ref[k]) for k in range(N_CHIP)], axis=-1)
        yf = jnp.concatenate([_nn(oc, wfo_ref[k]) for k in range(N_CHIP)], axis=-1)
        yr_ref[...] = yr
        yf_ref[...] = yf
        m = (_sigmoid(ar_ref[...]) * yr + _sigmoid(af_ref[...]) * yf).astype(BF)
        m_ref[...] = m
        x2 = x_ref[...]
        for k in range(N_CHIP):
            x2 = x2 + _nn(m[:, 256 * k:256 * k + 256], wout_ref[k])
        x2_ref[...] = x2
        r = lax.rsqrt(jnp.mean(x2 * x2, axis=-1, keepdims=True) + EPS)
        h2_ref[...] = (x2 * r * g_ref[...]).astype(BF)

    row = lambda w: pl.BlockSpec((tm, w), lambda i: (i, 0))
    const = lambda shp: pl.BlockSpec(shp, lambda i: (0,) * len(shp))
    return pl.pallas_call(
        body, name="merge_out", grid=(T // tm,),
        in_specs=[row(512), pl.BlockSpec((FOX_H, tm, LANE), lambda i: (0, i, 0)),
                  pl.BlockSpec((tm, 1024), lambda i: (i, 3)), pl.BlockSpec((tm, 1024), lambda i: (i, 4)),
                  row(1024), const((1, 1024)), const((N_CHIP, 512, 256)), const((N_CHIP, 512, 256)),
                  const((N_CHIP, 256, 1024))],
        out_specs=[row(1024), row(1024), row(1024), row(1024), row(1024), row(512)],
        out_shape=[jax.ShapeDtypeStruct((T, 1024), F32), jax.ShapeDtypeStruct((T, 1024), F32),
                   jax.ShapeDtypeStruct((T, 1024), BF), jax.ShapeDtypeStruct((T, 1024), F32),
                   jax.ShapeDtypeStruct((T, 1024), BF), jax.ShapeDtypeStruct((T, 512), BF)],
        compiler_params=_params(("parallel",), VMEM_BIG),
    )(u_r, o_fox, z_a, z_a, x, g_ffn, w_ro, w_fo, w_out)


def _load_resident(hbm_refs, vmem_refs, sem):
    cps = [pltpu.make_async_copy(h, v, sem.at[i]) for i, (h, v) in enumerate(zip(hbm_refs, vmem_refs))]
    for cp in cps:
        cp.start()
    for cp in cps:
        cp.wait()


def _ffn_fwd(h2, x2, tgt, w_gate, w_up, w_down, tm=FFN_TM):
    T = h2.shape[0]

    def body(h_ref, x2_ref, t_ref, wg_hbm, wu_hbm, wd_hbm, a_ref, b_ref, act_ref, dy_ref, ls_ref, wg, wu, wd, sem):
        @pl.when(pl.program_id(0) == 0)
        def _():
            _load_resident((wg_hbm, wu_hbm, wd_hbm), (wg, wu, wd), sem)
            ls_ref[...] = jnp.zeros(ls_ref.shape, F32)

        h = h_ref[...]
        err = x2_ref[...] - t_ref[...]
        for k in range(N_CHIP):
            gp = _nt(h, wg[k])
            up = _nt(h, wu[k])
            sg = _sigmoid(gp)
            silu = gp * sg
            a_ref[k] = silu.astype(BF)
            b_ref[k] = (up * sg * (1.0 + gp * (1.0 - sg))).astype(BF)
            act = (silu * up).astype(BF)
            act_ref[k] = act
            err = err + _nn(act, wd[k])
        dy_ref[...] = err * (1.0 / D_MODEL)
        ls_ref[...] += jnp.sum(err * err, axis=0, keepdims=True)

    row = pl.BlockSpec((tm, D_MODEL), lambda i: (i, 0))
    hid = pl.BlockSpec((N_CHIP, tm, FF_SH), lambda i: (0, i, 0))
    anyspec = pl.BlockSpec(memory_space=pl.ANY)
    wshape = pltpu.VMEM((N_CHIP, FF_SH, D_MODEL), BF)
    return pl.pallas_call(
        body, name="ffn_fwd", grid=(T // tm,),
        in_specs=[row, row, row, anyspec, anyspec, anyspec],
        out_specs=[hid, hid, hid, row, pl.BlockSpec((1, D_MODEL), lambda i: (0, 0))],
        out_shape=[jax.ShapeDtypeStruct((N_CHIP, T, FF_SH), BF)] * 3
        + [jax.ShapeDtypeStruct((T, D_MODEL), F32), jax.ShapeDtypeStruct((1, D_MODEL), F32)],
        scratch_shapes=[wshape, wshape, wshape, pltpu.SemaphoreType.DMA((3,))],
        compiler_params=_params(("arbitrary",), VMEM_HUGE),
    )(h2, x2, tgt, w_gate, w_up, w_down)


def _ffn_bwd(dy, sa, sb, x2, g_ffn, w_gate, w_up, w_down, tm=FFN_TM):
    T = dy.shape[0]

    def body(dy_ref, a_ref, b_ref, x2_ref, g_ref, wg_hbm, wu_hbm, wd_hbm, dgp_ref, dup_ref, dx_ref, dg_ref,
             wg, wu, wd, sem):
        @pl.when(pl.program_id(0) == 0)
        def _():
            _load_resident((wg_hbm, wu_hbm, wd_hbm), (wg, wu, wd), sem)
            dg_ref[...] = jnp.zeros(dg_ref.shape, F32)

        dy = dy_ref[...]
        dyb = dy.astype(BF)
        dh = jnp.zeros((tm, D_MODEL), F32)
        for k in range(N_CHIP):
            dact = _nt(dyb, wd[k])
            dup = (dact * a_ref[k]).astype(BF)
            dgp = (dact * b_ref[k]).astype(BF)
            dgp_ref[k] = dgp
            dup_ref[k] = dup
            dh = dh + _nn(dgp, wg[k]) + _nn(dup, wu[k])
        x2 = x2_ref[...]
        r = lax.rsqrt(jnp.mean(x2 * x2, axis=-1, keepdims=True) + EPS)
        xn = x2 * r
        dg_ref[...] += jnp.sum(dh * xn, axis=0, keepdims=True)
        dxn = dh * g_ref[...]
        dx_ref[...] = dy + r * (dxn - xn * jnp.mean(dxn * xn, axis=-1, keepdims=True))

    row = pl.BlockSpec((tm, D_MODEL), lambda i: (i, 0))
    hid = pl.BlockSpec((N_CHIP, tm, FF_SH), lambda i: (0, i, 0))
    vec = pl.BlockSpec((1, D_MODEL), lambda i: (0, 0))
    anyspec = pl.BlockSpec(memory_space=pl.ANY)
    wshape = pltpu.VMEM((N_CHIP, FF_SH, D_MODEL), BF)
    return pl.pallas_call(
        body, name="ffn_bwd", grid=(T // tm,),
        in_specs=[row, hid, hid, row, vec, anyspec, anyspec, anyspec],
        out_specs=[hid, hid, row, vec],
        out_shape=[jax.ShapeDtypeStruct((N_CHIP, T, FF_SH), BF), jax.ShapeDtypeStruct((N_CHIP, T, FF_SH), BF),
                   jax.ShapeDtypeStruct((T, D_MODEL), F32), jax.ShapeDtypeStruct((1, D_MODEL), F32)],
        scratch_shapes=[wshape, wshape, wshape, pltpu.SemaphoreType.DMA((3,))],
        compiler_params=_params(("arbitrary",), VMEM_HUGE),
    )(dy, sa, sb, x2, g_ffn, w_gate, w_up, w_down)


def _out_bwd(dx2, z_a, y_r, y_f, o_raw, o_fox, g_ret, w_ro, w_fo, w_out, tm=256, push=None):
    T = dx2.shape[0]

    def body(dx_ref, gt_ref, ar_ref, af_ref, yr_ref, yf_ref, o_ref, of_ref, g_ref, wro_ref, wfo_ref, wout_ref,
             dyr_ref, dyf_ref, dgt_ref, da_ref, do_ref, dof_ref, dg_ref):
        i = pl.program_id(0)

        @pl.when(i == 0)
        def _():
            dg_ref[...] = jnp.zeros(dg_ref.shape, F32)

        dxb = dx_ref[...].astype(BF)
        dm = jnp.concatenate([_nt(dxb, wout_ref[k]) for k in range(N_CHIP)], axis=-1)
        sr, sf = _sigmoid(ar_ref[...]), _sigmoid(af_ref[...])
        dyr = dm * sr
        dyf = dm * sf
        da_ref[:, :1024] = (dyr * yr_ref[...] * (1.0 - sr)).astype(BF)
        da_ref[:, 1024:] = (dyf * yf_ref[...] * (1.0 - sf)).astype(BF)
        dyr = dyr.astype(BF)
        dyf = dyf.astype(BF)
        dyr_ref[...] = dyr
        dyf_ref[...] = dyf
        du = jnp.zeros((tm, 512), F32)
        doc = jnp.zeros((tm, 512), F32)
        for k in range(N_CHIP):
            du = du + _nt(dyr[:, 256 * k:256 * k + 256], wro_ref[k])
            doc = doc + _nt(dyf[:, 256 * k:256 * k + 256], wfo_ref[k])

        for h in range(RET_H):
            cols = slice(h * RET_DV, (h + 1) * RET_DV)
            o = o_ref[:, cols]
            mu = jnp.mean(o, axis=-1, keepdims=True)
            xc = o - mu
            rstd = lax.rsqrt(jnp.mean(xc * xc, axis=-1, keepdims=True) + EPS)
            on = xc * rstd
            g = g_ref[:, cols]
            gt = gt_ref[:, cols]
            sg = _sigmoid(gt)
            duh = du[:, cols]
            dgt_ref[:, cols] = (duh * (on * g) * sg * (1.0 + gt * (1.0 - sg))).astype(BF)
            dog = duh * gt * sg
            dg_ref[:, cols] += jnp.sum(dog * on, axis=0, keepdims=True)
            don = dog * g
            do_ref[:, cols] = rstd * (don - jnp.mean(don, axis=-1, keepdims=True)
                                      - on * jnp.mean(don * on, axis=-1, keepdims=True))

        lane = lax.broadcasted_iota(jnp.int32, (tm, LANE), 1)
        zpad = jnp.zeros((tm, 64), F32)
        for h in range(FOX_H):
            doh = doc[:, 64 * h:64 * h + 64]
            delta = jnp.sum(doh * of_ref[h][:, :FOX_D], axis=-1, keepdims=True)
            hi, mid, lo = [t.astype(F32) for t in _split3(-delta)]
            da = jnp.concatenate([doh, zpad], axis=-1)
            da = jnp.where(lane == 64, hi, jnp.where(lane == 65, mid, jnp.where(lane == 66, lo, da)))
            dof_ref[h] = da.astype(BF)

    row = lambda w: pl.BlockSpec((tm, w), lambda i: (i, 0))
    const = lambda shp: pl.BlockSpec(shp, lambda i: (0,) * len(shp))
    hsp = pl.BlockSpec((FOX_H, tm, LANE), lambda i: (0, i, 0))
    return _hosted_call(
        body, "out_bwd", (T // tm,),
        [row(1024), pl.BlockSpec((tm, 512), lambda i: (i, 2)), pl.BlockSpec((tm, 1024), lambda i: (i, 3)),
         pl.BlockSpec((tm, 1024), lambda i: (i, 4)), row(1024), row(1024), row(512), hsp,
         const((1, 512)), const((N_CHIP, 512, 256)), const((N_CHIP, 512, 256)), const((N_CHIP, 256, 1024))],
        [row(1024), row(1024), row(512), row(2048), row(512), hsp, const((1, 512))],
        [jax.ShapeDtypeStruct((T, 1024), BF), jax.ShapeDtypeStruct((T, 1024), BF),
         jax.ShapeDtypeStruct((T, 512), BF), jax.ShapeDtypeStruct((T, 2048), BF),
         jax.ShapeDtypeStruct((T, 512), F32), jax.ShapeDtypeStruct((FOX_H, T, LANE), BF),
         jax.ShapeDtypeStruct((1, 512), F32)],
        [], VMEM_BIG, (dx2, z_a, z_a, z_a, y_r, y_f, o_raw, o_fox, g_ret, w_ro, w_fo, w_out), push)


def _ret_bwd(d_o, qr, kr, z_a, states, cos_t, sin_t, consts, tt=512, push=None):
    T = z_a.shape[0]
    nt = T // tt
    nch = tt // CHUNK
    decay, zeta, xi, gcb = consts

    def body(do_ref, q_ref, k_ref, v_ref, st_ref, cos_ref, sin_ref, d_ref, ze_ref, xi_ref, gc_ref, dz_ref, g_sc):
        i = pl.program_id(0)

        @pl.when(i == 0)
        def _():
            g_sc[...] = jnp.zeros(g_sc.shape, F32)

        for c in reversed(range(nch)):
            rows = slice(c * CHUNK, (c + 1) * CHUNK)
            cosv, sinv = cos_ref[rows, :], sin_ref[rows, :]
            dq_parts, dk_parts = [], []
            for h in range(RET_H):
                cols = slice(h * RET_DV, (h + 1) * RET_DV)
                q, k = q_ref[h, rows, :], k_ref[h, rows, :]
                v32 = v_ref[rows, cols]
                vb = v32.astype(BF)
                r = st_ref[h, rows, :]
                g = g_sc[h]
                gb = g.astype(BF)
                d_o = do_ref[rows, cols]
                dob = d_o.astype(BF)
                dox = (d_o * xi_ref[h]).astype(BF)
                dec = d_ref[h]
                s = (_nt(q, k) * dec).astype(BF)
                ds = (_nt(dob, vb) * dec).astype(BF)
                dv = _tn(s, dob) + ze_ref[h] * _nn(k, gb)
                dq = _nn(ds, k) + _nt(dox, r.astype(BF))
                dk = _tn(ds, q) + _nt((v32 * ze_ref[h]).astype(BF), gb)
                g_sc[h] = gc_ref[h] * g + _tn(q, dox)
                dq_parts.append((dq * cosv - _swap32(dq) * sinv)[:, :64])
                dk_parts.append(((dk * cosv - _swap32(dk) * sinv) * 0.125)[:, :64])
                dz_ref[rows, 512 + h * RET_DV:512 + (h + 1) * RET_DV] = dv.astype(BF)
            dz_ref[rows, 0:256] = jnp.concatenate(dq_parts, axis=-1).astype(BF)
            dz_ref[rows, 256:512] = jnp.concatenate(dk_parts, axis=-1).astype(BF)

    rev = lambda i: nt - 1 - i
    hspec = pl.BlockSpec((RET_H, tt, LANE), lambda i: (0, rev(i), 0))
    cspec = pl.BlockSpec((RET_H, CHUNK, LANE), lambda i: (0, 0, 0))
    tab = pl.BlockSpec((tt, LANE), lambda i: (rev(i), 0))
    (dz,), lands = _hosted_call(
        body, "ret_bwd", (nt,),
        [pl.BlockSpec((tt, 512), lambda i: (rev(i), 0)), hspec, hspec,
         pl.BlockSpec((tt, 512), lambda i: (rev(i), 1)), hspec, tab, tab, cspec, cspec, cspec, cspec],
        [pl.BlockSpec((tt, 1024), lambda i: (rev(i), 0))], [jax.ShapeDtypeStruct((T, 1024), BF)],
        [pltpu.VMEM((RET_H, CHUNK, LANE), F32)], VMEM_BIG,
        (d_o, qr, kr, z_a, states, cos_t, sin_t, decay, zeta, xi, gcb), push)
    return dz, lands


def _fox_bwd(iend, q2, k, v, do, sub=512):
    H, T, _ = k.shape
    tb = 2 * sub

    def body(ie_ref, q_ref, do_ref, k_ref, v_ref, dq_ref, dk_ref, dv_ref, dk_sc, dv_sc):
        j = pl.program_id(1)
        n = ie_ref[pl.program_id(0), j]

        @pl.when(j == 0)
        def _():
            dq_ref[...] = jnp.zeros(dq_ref.shape, F32)

        kk, vv = k_ref[...], v_ref[...]
        dk_sc[...] = jnp.zeros(dk_sc.shape, F32)
        dv_sc[...] = jnp.zeros(dv_sc.shape, F32)
        krow = lax.broadcasted_iota(jnp.int32, (tb, sub), 0)
        qcol = lax.broadcasted_iota(jnp.int32, (tb, sub), 1)

        def step(i, shift):
            off = pl.multiple_of(i * sub, sub)
            qq = q_ref[pl.ds(off, sub), :]
            dd = do_ref[pl.ds(off, sub), :]
            p = jnp.exp(_nt(kk, qq))
            if shift is not None:
                p = jnp.where(qcol + shift >= krow, p, 0.0)
            ds = (p * _nt(vv, dd)).astype(BF)
            dv_sc[...] += _nn(p.astype(BF), dd)
            dk_sc[...] += _nn(ds, qq)
            dq_ref[pl.ds(off, sub), :] += _tn(ds, kk)

        off0 = pl.multiple_of(2 * j * sub, sub)
        q0, d0 = q_ref[pl.ds(off0, sub), :], do_ref[pl.ds(off0, sub), :]
        k0, v0 = k_ref[0:sub, :], v_ref[0:sub, :]
        p0 = jnp.where(qcol[0:sub, :] >= krow[0:sub, :], jnp.exp(_nt(k0, q0)), 0.0)
        ds0 = (p0 * _nt(v0, d0)).astype(BF)
        dv_sc[0:sub, :] += _nn(p0.astype(BF), d0)
        dk_sc[0:sub, :] += _nn(ds0, q0)
        dq_ref[pl.ds(off0, sub), :] += _tn(ds0, k0)
        step(2 * j + 1, sub)

        def loop_body(i, carry):
            step(i, None)
            return carry

        lax.fori_loop(2 * j + 2, n, loop_body, 0)
        dk_ref[...] = dk_sc[...]
        dv_ref[...] = dv_sc[...]

    blk = pl.BlockSpec((None, tb, LANE), lambda h, j, ie: (h, j, 0))
    full = pl.BlockSpec((None, T, LANE), lambda h, j, ie: (h, 0, 0))
    shp = jax.ShapeDtypeStruct((H, T, LANE), F32)
    return pl.pallas_call(
        body, name="fox_bwd",
        grid_spec=pltpu.PrefetchScalarGridSpec(
            num_scalar_prefetch=1, grid=(H, T // tb), in_specs=[full, full, blk, blk], out_specs=[full, blk, blk],
            scratch_shapes=[pltpu.VMEM((tb, LANE), F32), pltpu.VMEM((tb, LANE), F32)]),
        out_shape=[shp, shp, shp],
        compiler_params=_params(("arbitrary", "arbitrary"), VMEM_BIG),
    )(iend, q2, do, k, v)


def _fox_post_bwd(dq, dk, dv, z_a, z_ff, b_f, g_q, g_k, tm=256, push=None):
    T = z_a.shape[0]
    nt = T // tm

    def body(dq_ref, dk_ref, dv_ref, zf_ref, zff_ref, b_ref, gq_ref, gk_ref,
             dz_ref, dff_ref, dgq_ref, dgk_ref, db_ref, carry):
        i = pl.program_id(0)

        @pl.when(i == 0)
        def _():
            carry[...] = jnp.zeros(carry.shape, F32)
            dgq_ref[...] = jnp.zeros(dgq_ref.shape, F32)
            dgk_ref[...] = jnp.zeros(dgk_ref.shape, F32)
            db_ref[...] = jnp.zeros(db_ref.shape, F32)

        lane = lax.broadcasted_iota(jnp.int32, (tm, LANE), 1)
        zf = zf_ref[...]
        dcm = jnp.zeros((tm, LANE), F32)
        dq_parts, dk_parts, dv_parts = [], [], []
        gq_acc = jnp.zeros((1, 64), F32)
        gk_acc = jnp.zeros((1, 64), F32)
        for h in range(FOX_H):
            dqa, dka = dq_ref[h], dk_ref[h]
            dcm = jnp.where(lane == h, dqa[:, L_CQ:L_CQ + 1] - dka[:, L_CK:L_CK + 1], dcm)
            for src, dya, g_ref, scale, parts in ((0, dqa, gq_ref, 0.125, dq_parts), (512, dka, gk_ref, 1.0, dk_parts)):
                xh = zf[:, src + 64 * h:src + 64 * h + 64]
                r = lax.rsqrt(jnp.mean(xh * xh, axis=-1, keepdims=True) + EPS)
                xn = xh * r
                dy = dya[:, :FOX_D] * scale
                if src == 0:
                    gq_acc = gq_acc + jnp.sum(dy * xn, axis=0, keepdims=True)
                else:
                    gk_acc = gk_acc + jnp.sum(dy * xn, axis=0, keepdims=True)
                dxn = dy * g_ref[...]
                parts.append(r * (dxn - xn * jnp.mean(dxn * xn, axis=-1, keepdims=True)))
            dv_parts.append(dv_ref[h][:, :FOX_D])
        dz_ref[...] = jnp.concatenate(dq_parts + dk_parts + dv_parts, axis=-1).astype(BF)
        zpad = jnp.zeros((1, 64), F32)
        dgq_ref[...] += jnp.concatenate([gq_acc, zpad], axis=-1)
        dgk_ref[...] += jnp.concatenate([gk_acc, zpad], axis=-1)

        row = lax.broadcasted_iota(jnp.int32, (tm, tm), 0)
        col = lax.broadcasted_iota(jnp.int32, (tm, tm), 1)
        tri = (row <= col).astype(BF)
        hi, mid, lo = _split3(dcm)
        dlogf = _nn(tri, hi) + _nn(tri, mid) + _nn(tri, lo) + carry[...]
        carry[...] = dlogf[0:1, :]
        dff = jnp.where(lane < FOX_H, dlogf * _sigmoid(-(zff_ref[...] + b_ref[...])), 0.0)
        dff_ref[...] = dff.astype(BF)
        db_ref[...] += jnp.sum(dff, axis=0, keepdims=True)

    rev = lambda i: nt - 1 - i
    hsp = pl.BlockSpec((FOX_H, tm, LANE), lambda i: (0, rev(i), 0))
    small = lambda w: pl.BlockSpec((1, w), lambda i: (0, 0))
    return _hosted_call(
        body, "fox_post_bwd", (nt,),
        [hsp, hsp, hsp, pl.BlockSpec((tm, 1536), lambda i: (rev(i), 1)),
         pl.BlockSpec((tm, LANE), lambda i: (rev(i), 0)), small(LANE), small(64), small(64)],
        [pl.BlockSpec((tm, 1536), lambda i: (rev(i), 0)), pl.BlockSpec((tm, LANE), lambda i: (rev(i), 0)),
         small(LANE), small(LANE), small(LANE)],
        [jax.ShapeDtypeStruct((T, 1536), BF), jax.ShapeDtypeStruct((T, LANE), BF),
         jax.ShapeDtypeStruct((1, LANE), F32), jax.ShapeDtypeStruct((1, LANE), F32),
         jax.ShapeDtypeStruct((1, LANE), F32)],
        [pltpu.VMEM((1, LANE), F32)], VMEM_BIG, (dq, dk, dv, z_a, z_ff, b_f, g_q, g_k), push)


def _in_bwd(dz_ret, dz_gt, dz_fox, dz_a, dz_ff, w_a, w_ff, x, g_mix, dx2, tm=256, push=None):
    T = x.shape[0]

    def body(r_ref, t_ref, f_ref, a_ref, ff_ref, wa_ref, wf_ref, x_ref, g_ref, dx2_ref, dx_ref, dg_ref):
        i = pl.program_id(0)

        @pl.when(i == 0)
        def _():
            dg_ref[...] = jnp.zeros(dg_ref.shape, F32)

        dh = (_nt(r_ref[...], wa_ref[:, C_RET:C_GT]) + _nt(t_ref[...], wa_ref[:, C_GT:C_FOX])
              + _nt(f_ref[...], wa_ref[:, C_FOX:C_A]) + _nt(a_ref[...], wa_ref[:, C_A:C_END])
              + _nt(ff_ref[...], wf_ref[...]))
        xv = x_ref[...]
        r = lax.rsqrt(jnp.mean(xv * xv, axis=-1, keepdims=True) + EPS)
        xn = xv * r
        dg_ref[...] += jnp.sum(dh * xn, axis=0, keepdims=True)
        dxn = dh * g_ref[...]
        dx_ref[...] = dx2_ref[...] + r * (dxn - xn * jnp.mean(dxn * xn, axis=-1, keepdims=True))

    row = lambda w: pl.BlockSpec((tm, w), lambda i: (i, 0))
    const = lambda shp: pl.BlockSpec(shp, lambda i: (0,) * len(shp))
    return _hosted_call(
        body, "in_bwd", (T // tm,),
        [row(1024), row(512), row(1536), row(2048), row(LANE), const((D_MODEL, C_END)),
         const((D_MODEL, LANE)), row(1024), const((1, 1024)), row(1024)],
        [row(1024), const((1, 1024))],
        [jax.ShapeDtypeStruct((T, 1024), F32), jax.ShapeDtypeStruct((1, 1024), F32)],
        [], VMEM_BIG, (dz_ret, dz_gt, dz_fox, dz_a, dz_ff, w_a, w_ff, x, g_mix, dx2), push)


def _mesh_pos():
    return lax.axis_index("x"), lax.axis_index("y"), lax.axis_index("c")


def _staged_place(src, name):
    stacked = src.ndim == 3
    R, C = src.shape[-2:]
    tr = _row_tile(R, 128, 16)
    n = R // tr
    assert n >= 2

    def body(s_ref, o_ref, buf, sem):
        i = pl.program_id(0)
        slot = i % 2
        x, y, _ = _mesh_pos()
        kme = 2 * x + y

        def out_copy(s, step):
            return pltpu.make_async_copy(buf.at[s], o_ref.at[kme, pl.ds(pl.multiple_of(step * tr, tr), tr), :], sem.at[s])

        @pl.when(i >= 2)
        def _():
            out_copy(slot, i - 2).wait()

        buf[slot] = (s_ref[kme] if stacked else s_ref[...]).astype(BF)
        out_copy(slot, i).start()

        @pl.when(i == n - 1)
        def _():
            out_copy(1 - slot, i - 1).wait()
            out_copy(slot, i).wait()

    in_spec = (pl.BlockSpec((N_CHIP, tr, C), lambda i: (0, i, 0)) if stacked else pl.BlockSpec((tr, C), lambda i: (i, 0)))
    return pl.pallas_call(
        body, name=name, grid=(n,), in_specs=[in_spec], out_specs=pl.BlockSpec(memory_space=pl.ANY),
        out_shape=jax.ShapeDtypeStruct((N_CHIP, R, C), BF),
        scratch_shapes=[pltpu.VMEM((2, tr, C), BF), pltpu.SemaphoreType.DMA((2,))],
        compiler_params=_params(("arbitrary",)),
    )(src)


def _push_copies(src, land, send_sem, recv_sem, receiving):
    x, y, c = _mesh_pos()
    kme = 2 * x + y
    cps = []
    for w in range(len(land)):
        for j, (px, py) in enumerate([(1 - x, y), (x, 1 - y), (1 - x, 1 - y)]):
            kpeer = 2 * px + py
            cps.append(pltpu.make_async_remote_copy(
                src_ref=land[w].at[kme] if src is None else src[w].at[kpeer],
                dst_ref=land[w].at[kpeer if receiving else kme],
                send_sem=send_sem.at[3 * w + j], recv_sem=recv_sem.at[3 * w + j],
                device_id=(px, py, c), device_id_type=MESH))
    return cps


def _gather_in_place(stacks, name):
    n = len(stacks)

    def body(*refs):
        land, send_sem, recv_sem = refs[n:2 * n], refs[2 * n], refs[2 * n + 1]
        for cp in _push_copies(None, land, send_sem, recv_sem, False):
            cp.start()
        for cp in _push_copies(None, land, send_sem, recv_sem, True):
            cp.wait_recv()
            cp.wait_send()

    anyspec = pl.BlockSpec(memory_space=pl.ANY)
    return pl.pallas_call(
        body, name=name, in_specs=[anyspec] * n, out_specs=[anyspec] * n,
        out_shape=[jax.ShapeDtypeStruct(s.shape, s.dtype) for s in stacks],
        input_output_aliases={i: i for i in range(n)},
        scratch_shapes=[pltpu.SemaphoreType.DMA((3 * n,)), pltpu.SemaphoreType.DMA((3 * n,))],
    )(*stacks)


def _scatter_partials(srcs, lands, small):
    n = len(srcs)

    def body(*refs):
        src, sv = refs[:n], refs[2 * n]
        land, svo = refs[2 * n + 1:3 * n + 1], refs[3 * n + 1]
        send_sem, recv_sem, ssend, srecv, sloc = refs[3 * n + 2:]
        x, y, c = _mesh_pos()
        me = 4 * x + 2 * y + c
        flips = [(b >> 2 & 1, b >> 1 & 1, b & 1) for b in range(1, 8)]
        others = [(1 - x if fx else x, 1 - y if fy else y, 1 - c if fc else c) for fx, fy, fc in flips]
        local = pltpu.make_async_copy(sv, svo.at[me], sloc)
        local.start()
        sends = []
        for j, (px, py, pc) in enumerate(others):
            cp = pltpu.make_async_remote_copy(
                src_ref=sv, dst_ref=svo.at[me], send_sem=ssend.at[j], recv_sem=srecv.at[j],
                device_id=(px, py, pc), device_id_type=MESH)
            cp.start()
            sends.append(cp)
        for cp in _push_copies(src, land, send_sem, recv_sem, False):
            cp.start()
            sends.append(cp)
        for j, (px, py, pc) in enumerate(others):
            pltpu.make_async_remote_copy(
                src_ref=sv, dst_ref=svo.at[4 * px + 2 * py + pc], send_sem=ssend.at[j], recv_sem=srecv.at[j],
                device_id=(px, py, pc), device_id_type=MESH).wait_recv()
        for cp in _push_copies(src, land, send_sem, recv_sem, True):
            cp.wait_recv()
        for cp in sends:
            cp.wait_send()
        local.wait()

    anyspec = pl.BlockSpec(memory_space=pl.ANY)
    return pl.pallas_call(
        body, name="scatter_partials",
        in_specs=[anyspec] * (2 * n + 1), out_specs=[anyspec] * (n + 1),
        out_shape=[jax.ShapeDtypeStruct(s.shape, s.dtype) for s in lands]
        + [jax.ShapeDtypeStruct((8,) + small.shape, small.dtype)],
        input_output_aliases={n + i: i for i in range(n)},
        scratch_shapes=[pltpu.SemaphoreType.DMA((3 * n,)), pltpu.SemaphoreType.DMA((3 * n,)),
                        pltpu.SemaphoreType.DMA((7,)), pltpu.SemaphoreType.DMA((7,)), pltpu.SemaphoreType.DMA],
    )(*srcs, *lands, small)


def _sibling_exchange(arrs):
    n = len(arrs)

    def body(*refs):
        ins, outs = refs[:n], refs[n:2 * n]
        send_sems, recv_sems = refs[2 * n:]
        x, y, c = _mesh_pos()
        cps = [pltpu.make_async_remote_copy(
            src_ref=ins[w], dst_ref=outs[w], send_sem=send_sems.at[w], recv_sem=recv_sems.at[w],
            device_id=(x, y, 1 - c), device_id_type=MESH) for w in range(n)]
        for cp in cps:
            cp.start()
        for cp in cps:
            cp.wait_recv()
        for cp in cps:
            cp.wait_send()

    anyspec = pl.BlockSpec(memory_space=pl.ANY)
    return pl.pallas_call(
        body, name="sibling_exchange",
        in_specs=[anyspec] * n, out_specs=[anyspec] * n,
        out_shape=[jax.ShapeDtypeStruct(a.shape, a.dtype) for a in arrs],
        scratch_shapes=[pltpu.SemaphoreType.DMA((n,)), pltpu.SemaphoreType.DMA((n,))],
    )(*arrs)


def _sum_stack(own, recv, name):
    _, R, C = recv.shape
    tr = _row_tile(R, 256, 16)

    def body(g_ref, r_ref, o_ref):
        x, y, _ = _mesh_pos()
        kme = 2 * x + y
        acc = g_ref[kme].astype(F32)
        for d in range(1, N_CHIP):
            acc = acc + r_ref[(kme + d) % N_CHIP].astype(F32)
        o_ref[...] = acc

    spec = pl.BlockSpec((N_CHIP, tr, C), lambda i: (0, i, 0))
    return pl.pallas_call(
        body, name=name, grid=(R // tr,), in_specs=[spec, spec],
        out_specs=pl.BlockSpec((tr, C), lambda i: (i, 0)),
        out_shape=jax.ShapeDtypeStruct((R, C), F32),
        compiler_params=_params(("parallel",)),
    )(own, recv)


def _adam_math(w, g, m, v):
    m2 = ADAM_B1 * m + (1.0 - ADAM_B1) * g
    v2 = ADAM_B2 * v + (1.0 - ADAM_B2) * (g * g)
    m_hat = m2 / (1.0 - ADAM_B1 ** ADAM_STEP)
    v_hat = v2 / (1.0 - ADAM_B2 ** ADAM_STEP)
    delta = -ADAM_LR * (m_hat / (jnp.sqrt(v_hat) + ADAM_EPS) + ADAM_WD * w)
    return delta, m2, v2


def _adamw(w, m, v, s0, s1, name):
    R, C = w.shape
    tr = _row_tile(R, 128, 8)

    def body(w_ref, m_ref, v_ref, a_ref, b_ref, g_ref, d_ref, m2_ref, v2_ref):
        g = a_ref[...] + b_ref[...]
        delta, m2, v2 = _adam_math(w_ref[...], g, m_ref[...], v_ref[...])
        g_ref[...] = g
        d_ref[...] = delta
        m2_ref[...] = m2
        v2_ref[...] = v2

    spec = pl.BlockSpec((tr, C), lambda i: (i, 0))
    shp = jax.ShapeDtypeStruct((R, C), F32)
    return pl.pallas_call(
        body, name=name, grid=(R // tr,), in_specs=[spec] * 5, out_specs=[spec] * 4, out_shape=[shp] * 4,
        compiler_params=_params(("parallel",), VMEM_BIG),
    )(w, m, v, s0, s1)


def _adamw_small(w, m, v, gathered):
    def body(w_ref, m_ref, v_ref, s_ref, g_ref, d_ref, m2_ref, v2_ref):
        g = s_ref[0]
        for d in range(1, 8):
            g = g + s_ref[d]
        delta, m2, v2 = _adam_math(w_ref[...], g, m_ref[...], v_ref[...])
        g_ref[...] = g
        d_ref[...] = delta
        m2_ref[...] = m2
        v2_ref[...] = v2

    shp = jax.ShapeDtypeStruct(w.shape, F32)
    return pl.pallas_call(body, name="adamw_small", out_shape=[shp] * 4)(w, m, v, gathered)


SMALL = (("g_mix", 1024), ("g_ffn", 1024), ("g_ret_norm", 512), ("g_fox_q", 64), ("g_fox_k", 64), ("b_forget", 8))
SMALL_W = 3072


def _pack_small(parts):
    cols = []
    for (name, n) in SMALL:
        p = parts[name].reshape(1, -1)[:, :n]
        pad = -n % LANE
        cols.append(jnp.pad(p, ((0, 0), (0, pad))) if pad else p)
    used = sum(c.shape[1] for c in cols)
    cols.append(jnp.zeros((1, SMALL_W - used), F32))
    return jnp.concatenate(cols, axis=1)


def _unpack_small(vec):
    out, off = {}, 0
    for (name, n) in SMALL:
        out[name] = vec[:, off:off + n]
        off += n + (-n % LANE)
    return out


def kernel(x, g_mix, w_in, b_forget, g_ret_norm, w_ret_o, g_fox_q, g_fox_k, w_fox_o, w_out, g_ffn, w_gate, w_up, w_down, loss_target, m_g_mix, m_w_in, m_b_forget, m_g_ret_norm, m_w_ret_o, m_g_fox_q, m_g_fox_k, m_w_fox_o, m_w_out, m_g_ffn, m_w_gate, m_w_up, m_w_down, v_g_mix, v_w_in, v_b_forget, v_g_ret_norm, v_w_ret_o, v_g_fox_q, v_g_fox_k, v_w_fox_o, v_w_out, v_g_ffn, v_w_gate, v_w_up, v_w_down):
    T = x.shape[1]
    xs = x[0]
    tgt = loss_target[0]
    big_names = ("w_in", "w_ret_o", "w_fox_o", "w_out", "w_gate", "w_up", "w_down")
    tr = lambda a: jnp.swapaxes(a[0], 0, 1)
    big_w = dict(w_in=w_in[0], w_ret_o=w_ret_o[0], w_fox_o=w_fox_o[0], w_out=w_out[0], w_gate=tr(w_gate),
                 w_up=tr(w_up), w_down=w_down[0])
    big_m = dict(w_in=m_w_in[0], w_ret_o=m_w_ret_o[0], w_fox_o=m_w_fox_o[0], w_out=m_w_out[0], w_gate=tr(m_w_gate),
                 w_up=tr(m_w_up), w_down=m_w_down[0])
    big_v = dict(w_in=v_w_in[0], w_ret_o=v_w_ret_o[0], w_fox_o=v_w_fox_o[0], w_out=v_w_out[0], w_gate=tr(v_w_gate),
                 w_up=tr(v_w_up), w_down=v_w_down[0])
    small_w = dict(g_mix=g_mix, g_ffn=g_ffn, g_ret_norm=g_ret_norm, g_fox_q=g_fox_q, g_fox_k=g_fox_k, b_forget=b_forget)
    small_m = dict(g_mix=m_g_mix, g_ffn=m_g_ffn, g_ret_norm=m_g_ret_norm, g_fox_q=m_g_fox_q, g_fox_k=m_g_fox_k,
                   b_forget=m_b_forget)
    small_v = dict(g_mix=v_g_mix, g_ffn=v_g_ffn, g_ret_norm=v_g_ret_norm, g_fox_q=v_g_fox_q, g_fox_k=v_g_fox_k,
                   b_forget=v_b_forget)

    stacks = {n: _staged_place(big_w[n], "place_" + n) for n in big_names}
    (s_in,) = _gather_in_place([stacks["w_in"]], "gather_w_in")
    w_a, w_ff = _assemble_w_in(s_in)
    b_pad = jnp.pad(b_forget, ((0, 0), (0, LANE - FOX_H)))
    cos_t, sin_t = _rope_tables(T)
    consts = _ret_consts()

    h = _rms_cast(xs, g_mix)
    z_a, (s_gate, s_up) = _mm_nn(h, w_a, "proj_in", push=(None, [stacks["w_gate"], stacks["w_up"]]))
    z_ff, _ = _mm_nn(h, w_ff, "proj_ff")
    (qr, kr, qf, kf, vf, c_cum, qmax, kmax), (s_down, s_ro, s_fo, s_out) = _mix_prep(
        z_a, z_ff, cos_t, sin_t, b_pad, g_fox_q, g_fox_k,
        push=(None, [stacks["w_down"], stacks["w_ret_o"], stacks["w_fox_o"], stacks["w_out"]]))
    jstart, iend = _prune_tables(c_cum, qmax, kmax, 512)
    o_raw, u_r, states = _ret_fwd(qr, kr, z_a, g_ret_norm, consts)
    o_fox, q2 = _fox_fwd(jstart, qf, kf, vf)
    y_r, y_f, mrg, x2, h2, o_cat = _merge_out(u_r, o_fox, z_a, xs, g_ffn, s_ro, s_fo, s_out)
    sa, sb, act, dy, loss_vec = _ffn_fwd(h2, x2, tgt, s_gate, s_up, s_down)
    loss = lax.psum(0.5 / D_MODEL * jnp.sum(loss_vec), ("x", "y", "c"))

    def scatter_job(grads):
        return (grads, [lax.empty(g.shape, g.dtype) for g in grads])

    dgp, dup, dx2, dg_ffn = _ffn_bwd(dy, sa, sb, x2, g_ffn, s_gate, s_up, s_down)
    g_gate, g_up, g_down = (_grad_astack(dgp, h2, "gw_gate"), _grad_astack(dup, h2, "gw_up"),
                            _grad_astack(act, dy, "gw_down"))
    (d_yr, d_yf, dz_gt, dz_a, d_o, do_fox, dg_ret), (r_gate, r_up) = _out_bwd(
        dx2, z_a, y_r, y_f, o_raw, o_fox, g_ret_norm, s_ro, s_fo, s_out,
        push=scatter_job([g_gate, g_up]))
    dz_ret, (r_down,) = _ret_bwd(d_o, qr, kr, z_a, states, cos_t, sin_t, consts, push=scatter_job([g_down]))
    dq_f, dk_f, dv_f = _fox_bwd(iend, q2, kf, vf, do_fox)
    g_mid = [_grad_colstack(u_r, d_yr, "gw_ret_o", 256), _grad_colstack(o_cat, d_yf, "gw_fox_o", 256),
             _grad_plain(mrg, dx2, "gw_out", BF).reshape(N_CHIP, 256, D_MODEL)]
    (dz_fox, dz_ff, dg_q, dg_k, db_f), (r_ro, r_fo, r_out) = _fox_post_bwd(
        dq_f, dk_f, dv_f, z_a, z_ff, b_pad, g_fox_q, g_fox_k, push=scatter_job(g_mid))
    (grad_x, dg_mix), _ = _in_bwd(dz_ret, dz_gt, dz_fox, dz_a, dz_ff, w_a, w_ff, xs, g_mix, dx2)
    g_in = _pack_g_in(_grad_plain(h, dz_ret, "gw_in_ret", F32), _grad_plain(h, dz_gt, "gw_in_gt", F32),
                      _grad_plain(h, dz_fox, "gw_in_fox", F32, tn=768), _grad_plain(h, dz_a, "gw_in_a", F32),
                      _grad_plain(h, dz_ff, "gw_in_ff", F32))
    small_g = _pack_small(dict(g_mix=dg_mix, g_ffn=dg_ffn, g_ret_norm=dg_ret, g_fox_q=dg_q, g_fox_k=dg_k, b_forget=db_f))

    r_in, small_all = _scatter_partials(*scatter_job([g_in]), small_g)
    sums = [_sum_stack(g, r, "sum_" + n) for g, r, n in zip(
        [g_in] + g_mid + [g_gate, g_up, g_down], [r_in, r_ro, r_fo, r_out, r_gate, r_up, r_down], big_names)]
    sib = _sibling_exchange(sums)
    big_out = {n: _adamw(big_w[n], big_m[n], big_v[n], sums[i], sib[i], "adamw_" + n) for i, n in enumerate(big_names)}
    sg, sd, sm, sv = _adamw_small(_pack_small(small_w), _pack_small(small_m), _pack_small(small_v), small_all)
    small_out = [_unpack_small(t) for t in (sg, sd, sm, sv)]

    order = ("g_mix", "w_in", "b_forget", "g_ret_norm", "w_ret_o", "g_fox_q", "g_fox_k", "w_fox_o", "w_out", "g_ffn",
             "w_gate", "w_up", "w_down")
    outs = [loss, grad_x[None]]
    for idx in range(4):
        for n in order:
            if n in ("w_gate", "w_up"):
                outs.append(jnp.swapaxes(big_out[n][idx], 0, 1)[None])
            else:
                outs.append(big_out[n][idx][None] if n in big_out else small_out[idx][n])
    return tuple(outs)
```

```python
import functools
import math

import numpy as np
import jax
import jax.numpy as jnp
from jax import lax
from jax.experimental import pallas as pl
from jax.experimental.pallas import tpu as pltpu

F32 = jnp.float32
BF = jnp.bfloat16
MESH = pl.DeviceIdType.MESH

D_MODEL = 1024
D_FF = 2816
N_CHIP = 4
FF_SH = D_FF // N_CHIP
IN_COLS = 5128
IN_SH = IN_COLS // N_CHIP
RET_H, RET_DV = 4, 128
FOX_H, FOX_D = 8, 64
CHUNK = 128
EPS = 1e-6
NEG = -1e30
LANE = 128
C_RET, C_GT, C_FOX, C_A, C_END = 0, 1024, 1536, 3072, 5120
L_CQ, L_CK, L_LSE, L_MAX = 64, 67, 70, 73

ADAM_LR, ADAM_B1, ADAM_B2, ADAM_EPS, ADAM_WD, ADAM_STEP = 0.001, 0.9, 0.999, 1e-08, 0.01, 10
VMEM_BIG = 56 * 1024 * 1024
VMEM_HUGE = 60 * 1024 * 1024
GRAD_TK = 2048
FFN_TM = 512


def _nn(a, b):
    return lax.dot_general(a, b, (((1,), (0,)), ((), ())), preferred_element_type=F32)


def _nt(a, b):
    return lax.dot_general(a, b, (((1,), (1,)), ((), ())), preferred_element_type=F32)


def _tn(a, b):
    return lax.dot_general(a, b, (((0,), (0,)), ((), ())), preferred_element_type=F32)


def _split3(x):
    hi = x.astype(BF)
    r = x - hi.astype(F32)
    mid = r.astype(BF)
    lo = (r - mid.astype(F32)).astype(BF)
    return hi, mid, lo


def _sigmoid(x):
    return 0.5 * jnp.tanh(0.5 * x) + 0.5


def _swap32(x):
    lane = lax.broadcasted_iota(jnp.int32, x.shape, 1)
    return jnp.where(lane < 32, pltpu.roll(x, 96, 1), pltpu.roll(x, 32, 1))


def _params(sem, vmem=None):
    return pltpu.CompilerParams(dimension_semantics=sem, vmem_limit_bytes=vmem)


def _row_tile(rows, cap, mult):
    return max(d for d in range(mult, cap + 1, mult) if rows % d == 0)


def _assemble_w_in(stack, tr=256):
    def body(s_ref, a_ref, f_ref):
        full = jnp.concatenate([s_ref[k].astype(F32) for k in range(N_CHIP)], axis=-1)
        a_ref[...] = jnp.concatenate([full[:, :3072], full[:, 3080:IN_COLS]], axis=-1).astype(BF)
        f_ref[...] = jnp.concatenate([full[:, 3072:3080], jnp.zeros((tr, LANE - FOX_H), F32)], axis=-1).astype(BF)

    return pl.pallas_call(
        body, name="assemble_w_in", grid=(D_MODEL // tr,),
        in_specs=[pl.BlockSpec((N_CHIP, tr, IN_SH), lambda i: (0, i, 0))],
        out_specs=[pl.BlockSpec((tr, C_END), lambda i: (i, 0)), pl.BlockSpec((tr, LANE), lambda i: (i, 0))],
        out_shape=[jax.ShapeDtypeStruct((D_MODEL, C_END), BF), jax.ShapeDtypeStruct((D_MODEL, LANE), BF)],
        compiler_params=_params(("parallel",), VMEM_BIG),
    )(stack)


def _pack_g_in(g_ret, g_gt, g_fox, g_a, g_ff, tr=256):
    def body(r_ref, t_ref, x_ref, a_ref, f_ref, o_ref):
        full = jnp.concatenate([r_ref[...], t_ref[...], x_ref[...], f_ref[...][:, :FOX_H], a_ref[...]], axis=-1)
        for k in range(N_CHIP):
            o_ref[k] = full[:, k * IN_SH:(k + 1) * IN_SH].astype(BF)

    def spec(w):
        return pl.BlockSpec((tr, w), lambda i: (i, 0))

    return pl.pallas_call(
        body, name="pack_g_in", grid=(D_MODEL // tr,),
        in_specs=[spec(1024), spec(512), spec(1536), spec(2048), spec(LANE)],
        out_specs=pl.BlockSpec((N_CHIP, tr, IN_SH), lambda i: (0, i, 0)),
        out_shape=jax.ShapeDtypeStruct((N_CHIP, D_MODEL, IN_SH), BF),
        compiler_params=_params(("parallel",), VMEM_BIG),
    )(g_ret, g_gt, g_fox, g_a, g_ff)


def _rms_cast(x, g, tm=512):
    T = x.shape[0]

    def body(x_ref, g_ref, o_ref):
        xv = x_ref[...]
        r = lax.rsqrt(jnp.mean(xv * xv, axis=-1, keepdims=True) + EPS)
        o_ref[...] = (xv * r * g_ref[...]).astype(BF)

    return pl.pallas_call(
        body, name="rms_cast", grid=(T // tm,),
        in_specs=[pl.BlockSpec((tm, D_MODEL), lambda i: (i, 0)), pl.BlockSpec((1, D_MODEL), lambda i: (0, 0))],
        out_specs=pl.BlockSpec((tm, D_MODEL), lambda i: (i, 0)),
        out_shape=jax.ShapeDtypeStruct((T, D_MODEL), BF),
        compiler_params=_params(("parallel",)),
    )(x, g)


def _hosted_call(body, name, grid, in_specs, out_specs, out_shape, scratch_shapes, vmem, args, push):
    sem = ("arbitrary",) * len(grid)
    if push is None:
        res = pl.pallas_call(body, name=name, grid=grid, in_specs=in_specs, out_specs=out_specs, out_shape=out_shape,
                             scratch_shapes=scratch_shapes, compiler_params=_params(sem, vmem))(*args)
        return list(res), []
    srcs, lands = push
    ns, nl, n_in, n_out = (0 if srcs is None else len(srcs)), len(lands), len(in_specs), len(out_specs)
    n_scr = len(scratch_shapes)

    def wrapped(*refs):
        pos = n_in + ns + nl
        ins, x_in = refs[:n_in], refs[n_in:pos]
        outs, x_out = refs[pos:pos + n_out], refs[pos + n_out:pos + n_out + nl]
        scr = refs[pos + n_out + nl:pos + n_out + nl + n_scr]
        ssem, rsem = refs[-2], refs[-1]
        src = None if srcs is None else x_in[:ns]
        ids = [pl.program_id(a) for a in range(len(grid))]
        first = functools.reduce(lambda p, q: p & q, [ids[a] == 0 for a in range(len(grid))])
        last = functools.reduce(lambda p, q: p & q, [ids[a] == grid[a] - 1 for a in range(len(grid))])

        @pl.when(first)
        def _():
            for cp in _push_copies(src, x_out, ssem, rsem, False):
                cp.start()

        body(*ins, *outs, *scr)

        @pl.when(last)
        def _():
            for cp in _push_copies(src, x_out, ssem, rsem, True):
                cp.wait_recv()
                cp.wait_send()

    anyspec = pl.BlockSpec(memory_space=pl.ANY)
    extra = ([] if srcs is None else list(srcs)) + list(lands)
    res = pl.pallas_call(
        wrapped, name=name, grid=grid,
        in_specs=list(in_specs) + [anyspec] * len(extra), out_specs=list(out_specs) + [anyspec] * nl,
        out_shape=list(out_shape) + [jax.ShapeDtypeStruct(a.shape, a.dtype) for a in lands],
        input_output_aliases={n_in + ns + i: n_out + i for i in range(nl)},
        scratch_shapes=list(scratch_shapes) + [pltpu.SemaphoreType.DMA((3 * nl,)), pltpu.SemaphoreType.DMA((3 * nl,))],
        compiler_params=_params(sem, vmem),
    )(*args, *extra)
    return list(res[:n_out]), list(res[n_out:])


def _mm_nn(a, b, name, tm=512, tn=1024, push=None):
    M, K = a.shape
    N = b.shape[1]
    tn = min(tn, N)

    def body(a_ref, b_ref, o_ref):
        o_ref[...] = _nn(a_ref[...], b_ref[...])

    (out,), lands = _hosted_call(
        body, name, (N // tn, M // tm),
        [pl.BlockSpec((tm, K), lambda j, i: (i, 0)), pl.BlockSpec((K, tn), lambda j, i: (0, j))],
        [pl.BlockSpec((tm, tn), lambda j, i: (i, j))], [jax.ShapeDtypeStruct((M, N), F32)], [], None, (a, b), push)
    return out, lands


def _mm_tn(a, b, name, grid, a_spec, b_spec, o_spec, out_shape, acc_shape):
    nk = grid[-1]

    def body(a_ref, b_ref, o_ref, acc):
        k = pl.program_id(len(grid) - 1)

        @pl.when(k == 0)
        def _():
            acc[...] = jnp.zeros(acc.shape, F32)

        acc[...] += _tn(a_ref[...].astype(BF), b_ref[...].astype(BF))

        @pl.when(k == nk - 1)
        def _():
            o_ref[...] = acc[...].astype(o_ref.dtype)

    return pl.pallas_call(
        body, name=name, grid=grid, in_specs=[a_spec, b_spec], out_specs=o_spec, out_shape=out_shape,
        scratch_shapes=[pltpu.VMEM(acc_shape, F32)],
        compiler_params=_params(("parallel",) * (len(grid) - 1) + ("arbitrary",), VMEM_BIG),
    )(a, b)


def _grad_plain(a, b, name, out_dtype, tk=GRAD_TK, tn=1024):
    T, M = a.shape
    N = b.shape[1]
    tn = min(tn, N)
    return _mm_tn(a, b, name, (N // tn, T // tk),
                  pl.BlockSpec((tk, M), lambda j, k: (k, 0)), pl.BlockSpec((tk, tn), lambda j, k: (k, j)),
                  pl.BlockSpec((M, tn), lambda j, k: (0, j)), jax.ShapeDtypeStruct((M, N), out_dtype), (M, tn))


def _grad_colstack(a, b, name, wcol, tk=GRAD_TK):
    T, M = a.shape
    S = b.shape[1] // wcol
    return _mm_tn(a, b, name, (S, T // tk),
                  pl.BlockSpec((tk, M), lambda s, k: (k, 0)), pl.BlockSpec((tk, wcol), lambda s, k: (k, s)),
                  pl.BlockSpec((None, M, wcol), lambda s, k: (s, 0, 0)),
                  jax.ShapeDtypeStruct((S, M, wcol), BF), (M, wcol))


def _grad_bstack(a, b, name, tk=GRAD_TK):
    T, M = a.shape
    S, _, n = b.shape
    return _mm_tn(a, b, name, (S, T // tk),
                  pl.BlockSpec((tk, M), lambda s, k: (k, 0)), pl.BlockSpec((None, tk, n), lambda s, k: (s, k, 0)),
                  pl.BlockSpec((None, M, n), lambda s, k: (s, 0, 0)),
                  jax.ShapeDtypeStruct((S, M, n), BF), (M, n))


def _grad_astack(a, b, name, tk=GRAD_TK):
    S, T, m = a.shape
    N = b.shape[1]
    return _mm_tn(a, b, name, (S, T // tk),
                  pl.BlockSpec((None, tk, m), lambda s, k: (s, k, 0)), pl.BlockSpec((tk, N), lambda s, k: (k, 0)),
                  pl.BlockSpec((None, m, N), lambda s, k: (s, 0, 0)),
                  jax.ShapeDtypeStruct((S, m, N), BF), (m, N))


def _rope_tables(T):
    half = 32
    pos = np.arange(T, dtype=np.float32)
    inv_freq = (np.float32(1.0) / (np.float32(10000.0) ** (np.arange(half, dtype=np.float32) / np.float32(half)))).astype(np.float32)
    ang = (pos[:, None] * inv_freq[None, :]).astype(np.float32)
    cos, sin = np.cos(ang).astype(np.float32), np.sin(ang).astype(np.float32)
    z = np.zeros((T, 64), np.float32)
    return (jnp.asarray(np.concatenate([cos, cos, z], axis=-1)), jnp.asarray(np.concatenate([-sin, sin, z], axis=-1)))


def _ret_consts():
    h = np.arange(RET_H, dtype=np.float32)
    log_g = np.log1p(-(np.float32(2.0) ** (-5.0 - h))).astype(np.float32)
    idx = np.arange(CHUNK, dtype=np.float32)
    diff = idx[:, None] - idx[None, :]
    decay = np.where(diff[None] >= 0, np.exp(np.maximum(diff, 0.0)[None] * log_g[:, None, None]), 0.0)
    zeta = np.exp((CHUNK - 1.0 - idx)[None, :] * log_g[:, None])
    xi = np.exp((idx + 1.0)[None, :] * log_g[:, None])
    gc = np.exp(CHUNK * log_g)
    bc = lambda v: np.broadcast_to(v[:, :, None], (RET_H, CHUNK, LANE)).astype(np.float32)
    gcb = np.broadcast_to(gc[:, None, None], (RET_H, CHUNK, LANE)).astype(np.float32)
    return (jnp.asarray(decay.astype(np.float32)), jnp.asarray(bc(zeta)), jnp.asarray(bc(xi)), jnp.asarray(gcb))


def _mix_prep(z_a, z_ff, cos_t, sin_t, b_f, g_q, g_k, tm=256, push=None):
    T = z_a.shape[0]

    def body(zqk_ref, zf_ref, zff_ref, cos_ref, sin_ref, b_ref, gq_ref, gk_ref,
             qr_ref, kr_ref, qf_ref, kf_ref, vf_ref, c_ref, qmax_ref, kmax_ref, carry):
        i = pl.program_id(0)

        @pl.when(i == 0)
        def _():
            carry[...] = jnp.zeros(carry.shape, F32)
            qmax_ref[...] = jnp.zeros(qmax_ref.shape, F32)
            kmax_ref[...] = jnp.zeros(kmax_ref.shape, F32)

        lane = lax.broadcasted_iota(jnp.int32, (tm, LANE), 1)
        lane1 = lax.broadcasted_iota(jnp.int32, (1, LANE), 1)
        zpad = jnp.zeros((tm, 64), F32)
        cosv, sinv = cos_ref[...], sin_ref[...]
        zqk = zqk_ref[...]
        for h in range(RET_H):
            for src, dst, scale in ((0, qr_ref, 1.0), (256, kr_ref, 0.125)):
                xh = jnp.concatenate([zqk[:, src + 64 * h: src + 64 * h + 64], zpad], axis=-1)
                rot = xh * cosv + _swap32(xh) * sinv
                dst[h] = (rot * scale).astype(BF)

        lf_in = zff_ref[...] + b_ref[...]
        logf = jnp.minimum(lf_in, 0.0) - jnp.log(1.0 + jnp.exp(-jnp.abs(lf_in)))
        row = lax.broadcasted_iota(jnp.int32, (tm, tm), 0)
        col = lax.broadcasted_iota(jnp.int32, (tm, tm), 1)
        tri = (row >= col).astype(BF)
        hi, mid, lo = _split3(logf)
        cs = _nn(tri, hi) + _nn(tri, mid) + _nn(tri, lo) + carry[...]
        carry[...] = cs[tm - 1:tm, :]
        c_ref[...] = cs

        zf = zf_ref[...]
        one = jnp.ones((tm, LANE), F32)
        qmax, kmax = qmax_ref[...], kmax_ref[...]
        for h in range(FOX_H):
            c = cs[:, h:h + 1]
            chi, cmid, clo = [t.astype(F32) for t in _split3(c)]
            qh = zf[:, 64 * h:64 * h + 64]
            kh = zf[:, 512 + 64 * h:512 + 64 * h + 64]
            vh = zf[:, 1024 + 64 * h:1024 + 64 * h + 64]
            qn = qh * lax.rsqrt(jnp.mean(qh * qh, axis=-1, keepdims=True) + EPS) * gq_ref[...] * 0.125
            kn = kh * lax.rsqrt(jnp.mean(kh * kh, axis=-1, keepdims=True) + EPS) * gk_ref[...]
            qa = jnp.concatenate([qn, zpad], axis=-1)
            qa = jnp.where(lane == L_CQ, chi, jnp.where(lane == L_CQ + 1, cmid, jnp.where(lane == L_CQ + 2, clo, qa)))
            qa = jnp.where((lane >= L_CK) & (lane < L_CK + 3), one, qa)
            ka = jnp.concatenate([kn, zpad], axis=-1)
            ka = jnp.where(lane == L_CK, -chi, jnp.where(lane == L_CK + 1, -cmid, jnp.where(lane == L_CK + 2, -clo, ka)))
            ka = jnp.where(((lane >= L_CQ) & (lane < L_CQ + 3)) | ((lane >= L_LSE) & (lane < L_MAX + 3)), one, ka)
            va = jnp.concatenate([vh, zpad], axis=-1)
            va = jnp.where((lane >= 64) & (lane < 67), one, va)
            qf_ref[h] = qa.astype(BF)
            kf_ref[h] = ka.astype(BF)
            vf_ref[h] = va.astype(BF)
            qmax = jnp.where(lane1 == h, jnp.maximum(qmax, jnp.max(jnp.sum(qn * qn, axis=-1, keepdims=True), axis=0,
                                                                   keepdims=True)), qmax)
            kmax = jnp.where(lane1 == h, jnp.maximum(kmax, jnp.max(jnp.sum(kn * kn, axis=-1, keepdims=True), axis=0,
                                                                   keepdims=True)), kmax)
        qmax_ref[...] = qmax
        kmax_ref[...] = kmax

    hspec4 = pl.BlockSpec((RET_H, tm, LANE), lambda i: (0, i, 0))
    hspec8 = pl.BlockSpec((FOX_H, tm, LANE), lambda i: (0, i, 0))
    small = lambda w: pl.BlockSpec((1, w), lambda i: (0, 0))
    return _hosted_call(
        body, "mix_prep", (T // tm,),
        [pl.BlockSpec((tm, 512), lambda i: (i, 0)), pl.BlockSpec((tm, 1536), lambda i: (i, 1)),
         pl.BlockSpec((tm, LANE), lambda i: (i, 0)), pl.BlockSpec((tm, LANE), lambda i: (i, 0)),
         pl.BlockSpec((tm, LANE), lambda i: (i, 0)), small(LANE), small(64), small(64)],
        [hspec4, hspec4, hspec8, hspec8, hspec8, pl.BlockSpec((tm, LANE), lambda i: (i, 0)), small(LANE), small(LANE)],
        [jax.ShapeDtypeStruct((RET_H, T, LANE), BF)] * 2 + [jax.ShapeDtypeStruct((FOX_H, T, LANE), BF)] * 3
        + [jax.ShapeDtypeStruct((T, LANE), F32), jax.ShapeDtypeStruct((1, LANE), F32), jax.ShapeDtypeStruct((1, LANE), F32)],
        [pltpu.VMEM((1, LANE), F32)], VMEM_BIG, (z_a, z_a, z_ff, cos_t, sin_t, b_f, g_q, g_k), push)


def _ret_fwd(qr, kr, z_a, g_ret, consts, tt=512):
    T = z_a.shape[0]
    nch = tt // CHUNK
    decay, zeta, xi, gcb = consts

    def body(q_ref, k_ref, v_ref, gt_ref, g_ref, d_ref, ze_ref, xi_ref, gc_ref, o_ref, u_ref, st_ref, r_sc):
        i = pl.program_id(0)

        @pl.when(i == 0)
        def _():
            r_sc[...] = jnp.zeros(r_sc.shape, F32)

        for c in range(nch):
            rows = slice(c * CHUNK, (c + 1) * CHUNK)
            for h in range(RET_H):
                cols = slice(h * RET_DV, (h + 1) * RET_DV)
                q, k = q_ref[h, rows, :], k_ref[h, rows, :]
                v32 = v_ref[rows, cols]
                r = r_sc[h]
                st_ref[h, rows, :] = r
                s = _nt(q, k) * d_ref[h]
                o = _nn(s.astype(BF), v32.astype(BF)) + _nn(q, r.astype(BF)) * xi_ref[h]
                r_sc[h] = gc_ref[h] * r + _tn(k, (v32 * ze_ref[h]).astype(BF))
                o_ref[rows, cols] = o
                mu = jnp.mean(o, axis=-1, keepdims=True)
                xc = o - mu
                on = xc * lax.rsqrt(jnp.mean(xc * xc, axis=-1, keepdims=True) + EPS)
                gt = gt_ref[rows, cols]
                u_ref[rows, cols] = (gt * _sigmoid(gt) * (on * g_ref[:, cols])).astype(BF)

    hspec = pl.BlockSpec((RET_H, tt, LANE), lambda i: (0, i, 0))
    cspec = pl.BlockSpec((RET_H, CHUNK, LANE), lambda i: (0, 0, 0))
    return pl.pallas_call(
        body, name="ret_fwd", grid=(T // tt,),
        in_specs=[hspec, hspec, pl.BlockSpec((tt, 512), lambda i: (i, 1)), pl.BlockSpec((tt, 512), lambda i: (i, 2)),
                  pl.BlockSpec((1, 512), lambda i: (0, 0)), cspec, cspec, cspec, cspec],
        out_specs=[pl.BlockSpec((tt, 512), lambda i: (i, 0)), pl.BlockSpec((tt, 512), lambda i: (i, 0)), hspec],
        out_shape=[jax.ShapeDtypeStruct((T, 512), F32), jax.ShapeDtypeStruct((T, 512), BF),
                   jax.ShapeDtypeStruct((RET_H, T, LANE), F32)],
        scratch_shapes=[pltpu.VMEM((RET_H, CHUNK, LANE), F32)],
        compiler_params=_params(("arbitrary",), VMEM_BIG),
    )(qr, kr, z_a, z_a, g_ret, decay, zeta, xi, gcb)


PRUNE_LOG = -110.0


def _prune_tables(c, qmax, kmax, sub):
    n = c.shape[0] // sub
    u = jnp.sqrt(qmax[0, :FOX_H] * kmax[0, :FOX_H]) * 1.02 + 0.5
    first = c[0::sub, :FOX_H].T
    last = c[sub - 1::sub, :FOX_H].T
    blk = jnp.arange(n, dtype=jnp.int32)
    needed = (2.0 * u[:, None, None] + first[:, :, None] - last[:, None, :] >= PRUNE_LOG) | (blk[None, :] >= blk[:, None])[None]
    jlo = jnp.argmax(needed, axis=2).astype(jnp.int32)
    jstart = jnp.minimum(jlo[:, 0::2], jlo[:, 1::2]) // 2
    need_q = jlo[:, None, :] <= (2 * jnp.arange(n // 2, dtype=jnp.int32) + 1)[None, :, None]
    iend = n - jnp.argmax(need_q[:, :, ::-1], axis=2).astype(jnp.int32)
    return jstart.astype(jnp.int32), iend.astype(jnp.int32)


def _fox_fwd(jstart, q, k, v, sub=512):
    H, T, _ = q.shape
    tb = 2 * sub

    def body(js_ref, q_ref, k_ref, v_ref, o_ref, q2_ref, mx_sc, acc_sc):
        i = pl.program_id(1)
        j0 = js_ref[pl.program_id(0), i]
        lane = lax.broadcasted_iota(jnp.int32, (sub, LANE), 1)
        row = lax.broadcasted_iota(jnp.int32, (sub, sub), 0)
        col = lax.broadcasted_iota(jnp.int32, (sub, sub), 1)
        causal = row >= col
        qs = [q_ref[0:sub, :], q_ref[sub:tb, :]]
        d0 = pl.multiple_of(i * tb, tb)
        d1 = pl.multiple_of(i * tb + sub, sub)

        def lane_max(s):
            m = s[:, 0:LANE]
            for c in range(1, s.shape[1] // LANE):
                m = jnp.maximum(m, s[:, c * LANE:(c + 1) * LANE])
            return m

        mx_sc[...] = jnp.full(mx_sc.shape, NEG, F32)

        def max_body(j, carry):
            kb = k_ref[pl.ds(pl.multiple_of(j * tb, tb), tb), :]
            for a in range(2):
                mx_sc[a] = jnp.maximum(mx_sc[a], lane_max(_nt(qs[a], kb)))
            return carry

        lax.fori_loop(j0, i, max_body, 0)
        k0, k1 = k_ref[pl.ds(d0, sub), :], k_ref[pl.ds(d1, sub), :]
        v0, v1 = v_ref[pl.ds(d0, sub), :], v_ref[pl.ds(d1, sub), :]
        mx = [jnp.maximum(mx_sc[0], lane_max(jnp.where(causal, _nt(qs[0], k0), NEG))),
              jnp.maximum(jnp.maximum(mx_sc[1], lane_max(_nt(qs[1], k0))),
                          lane_max(jnp.where(causal, _nt(qs[1], k1), NEG)))]
        ms = [jnp.max(t, axis=1, keepdims=True) for t in mx]

        def put3(base, first, val):
            hi, mid, lo = _split3(val)
            return jnp.where(lane == first, hi, jnp.where(lane == first + 1, mid, jnp.where(lane == first + 2, lo, base)))

        qm = [put3(qs[a], L_MAX, -ms[a]) for a in range(2)]

        acc_sc[...] = jnp.zeros(acc_sc.shape, F32)

        def acc_body(j, carry):
            off = pl.multiple_of(j * tb, tb)
            kb, vb = k_ref[pl.ds(off, tb), :], v_ref[pl.ds(off, tb), :]
            for a in range(2):
                acc_sc[a] += _nn(jnp.exp(_nt(qm[a], kb)).astype(BF), vb)
            return carry

        lax.fori_loop(j0, i, acc_body, 0)

        def pv(qa, kk, vv, masked):
            p = jnp.exp(_nt(qa, kk))
            if masked:
                p = jnp.where(causal, p, 0.0)
            return _nn(p.astype(BF), vv)

        accs = [acc_sc[0] + pv(qm[0], k0, v0, True),
                acc_sc[1] + pv(qm[1], k0, v0, False) + pv(qm[1], k1, v1, True)]
        for a in range(2):
            rows = slice(a * sub, (a + 1) * sub)
            l = accs[a][:, 64:65]
            o_ref[rows, :] = jnp.where(lane < 64, accs[a] / l, 0.0)
            q2_ref[rows, :] = put3(qs[a], L_LSE, -(ms[a] + jnp.log(l)))

    blk = pl.BlockSpec((None, tb, LANE), lambda h, i, js: (h, i, 0))
    full = pl.BlockSpec((None, T, LANE), lambda h, i, js: (h, 0, 0))
    return pl.pallas_call(
        body, name="fox_fwd",
        grid_spec=pltpu.PrefetchScalarGridSpec(
            num_scalar_prefetch=1, grid=(H, T // tb), in_specs=[blk, full, full], out_specs=[blk, blk],
            scratch_shapes=[pltpu.VMEM((2, sub, LANE), F32), pltpu.VMEM((2, sub, LANE), F32)]),
        out_shape=[jax.ShapeDtypeStruct((H, T, LANE), F32), jax.ShapeDtypeStruct((H, T, LANE), BF)],
        compiler_params=_params(("parallel", "arbitrary"), VMEM_BIG),
    )(jstart, q, k, v)


def _merge_out(u_r, o_fox, z_a, x, g_ffn, w_ro, w_fo, w_out, tm=256):
    T = x.shape[0]

    def body(u_ref, of_ref, ar_ref, af_ref, x_ref, g_ref, wro_ref, wfo_ref, wout_ref,
             yr_ref, yf_ref, m_ref, x2_ref, h2_ref, oc_ref):
        u = u_ref[...]
        oc = jnp.concatenate([of_ref[h][:, :FOX_D] for h in range(FOX_H)], axis=-1).astype(BF)
        oc_ref[...] = oc
        yr = jnp.concatenate([_nn(u, wro_ref[k]) for k in range(N_CHIP)], axis=-1)
        yf = jnp.concatenate([_nn(oc, wfo_ref[k]) for k in range(N_CHIP)], axis=-1)
        yr_ref[...] = yr
        yf_ref[...] = yf
        m = (_sigmoid(ar_ref[...]) * yr + _sigmoid(af_ref[...]) * yf).astype(BF)
        m_ref[...] = m
        x2 = x_ref[...]
        for k in range(N_CHIP):
            x2 = x2 + _nn(m[:, 256 * k:256 * k + 256], wout_ref[k])
        x2_ref[...] = x2
        r = lax.rsqrt(jnp.mean(x2 * x2, axis=-1, keepdims=True) + EPS)
        h2_ref[...] = (x2 * r * g_ref[...]).astype(BF)

    row = lambda w: pl.BlockSpec((tm, w), lambda i: (i, 0))
    const = lambda shp: pl.BlockSpec(shp, lambda i: (0,) * len(shp))
    return pl.pallas_call(
        body, name="merge_out", grid=(T // tm,),
        in_specs=[row(512), pl.BlockSpec((FOX_H, tm, LANE), lambda i: (0, i, 0)),
                  pl.BlockSpec((tm, 1024), lambda i: (i, 3)), pl.BlockSpec((tm, 1024), lambda i: (i, 4)),
                  row(1024), const((1, 1024)), const((N_CHIP, 512, 256)), const((N_CHIP, 512, 256)),
                  const((N_CHIP, 256, 1024))],
        out_specs=[row(1024), row(1024), row(1024), row(1024), row(1024), row(512)],
        out_shape=[jax.ShapeDtypeStruct((T, 1024), F32), jax.ShapeDtypeStruct((T, 1024), F32),
                   jax.ShapeDtypeStruct((T, 1024), BF), jax.ShapeDtypeStruct((T, 1024), F32),
                   jax.ShapeDtypeStruct((T, 1024), BF), jax.ShapeDtypeStruct((T, 512), BF)],
        compiler_params=_params(("parallel",), VMEM_BIG),
    )(u_r, o_fox, z_a, z_a, x, g_ffn, w_ro, w_fo, w_out)


def _load_resident(hbm_refs, vmem_refs, sem):
    cps = [pltpu.make_async_copy(h, v, sem.at[i]) for i, (h, v) in enumerate(zip(hbm_refs, vmem_refs))]
    for cp in cps:
        cp.start()
    for cp in cps:
        cp.wait()


def _ffn_fwd(h2, x2, tgt, w_gate, w_up, w_down, tm=FFN_TM):
    T = h2.shape[0]

    def body(h_ref, x2_ref, t_ref, wg_hbm, wu_hbm, wd_hbm, a_ref, b_ref, act_ref, dy_ref, ls_ref, wg, wu, wd, sem):
        @pl.when(pl.program_id(0) == 0)
        def _():
            _load_resident((wg_hbm, wu_hbm, wd_hbm), (wg, wu, wd), sem)
            ls_ref[...] = jnp.zeros(ls_ref.shape, F32)

        h = h_ref[...]
        err = x2_ref[...] - t_ref[...]
        for k in range(N_CHIP):
            gp = _nt(h, wg[k])
            up = _nt(h, wu[k])
            sg = _sigmoid(gp)
            silu = gp * sg
            a_ref[k] = silu.astype(BF)
            b_ref[k] = (up * sg * (1.0 + gp * (1.0 - sg))).astype(BF)
            act = (silu * up).astype(BF)
            act_ref[k] = act
            err = err + _nn(act, wd[k])
        dy_ref[...] = err * (1.0 / D_MODEL)
        ls_ref[...] += jnp.sum(err * err, axis=0, keepdims=True)

    row = pl.BlockSpec((tm, D_MODEL), lambda i: (i, 0))
    hid = pl.BlockSpec((N_CHIP, tm, FF_SH), lambda i: (0, i, 0))
    anyspec = pl.BlockSpec(memory_space=pl.ANY)
    wshape = pltpu.VMEM((N_CHIP, FF_SH, D_MODEL), BF)
    return pl.pallas_call(
        body, name="ffn_fwd", grid=(T // tm,),
        in_specs=[row, row, row, anyspec, anyspec, anyspec],
        out_specs=[hid, hid, hid, row, pl.BlockSpec((1, D_MODEL), lambda i: (0, 0))],
        out_shape=[jax.ShapeDtypeStruct((N_CHIP, T, FF_SH), BF)] * 3
        + [jax.ShapeDtypeStruct((T, D_MODEL), F32), jax.ShapeDtypeStruct((1, D_MODEL), F32)],
        scratch_shapes=[wshape, wshape, wshape, pltpu.SemaphoreType.DMA((3,))],
        compiler_params=_params(("arbitrary",), VMEM_HUGE),
    )(h2, x2, tgt, w_gate, w_up, w_down)


def _ffn_bwd(dy, sa, sb, x2, g_ffn, w_gate, w_up, w_down, tm=FFN_TM):
    T = dy.shape[0]

    def body(dy_ref, a_ref, b_ref, x2_ref, g_ref, wg_hbm, wu_hbm, wd_hbm, dgp_ref, dup_ref, dx_ref, dg_ref,
             wg, wu, wd, sem):
        @pl.when(pl.program_id(0) == 0)
        def _():
            _load_resident((wg_hbm, wu_hbm, wd_hbm), (wg, wu, wd), sem)
            dg_ref[...] = jnp.zeros(dg_ref.shape, F32)

        dy = dy_ref[...]
        dyb = dy.astype(BF)
        dh = jnp.zeros((tm, D_MODEL), F32)
        for k in range(N_CHIP):
            dact = _nt(dyb, wd[k])
            dup = (dact * a_ref[k]).astype(BF)
            dgp = (dact * b_ref[k]).astype(BF)
            dgp_ref[k] = dgp
            dup_ref[k] = dup
            dh = dh + _nn(dgp, wg[k]) + _nn(dup, wu[k])
        x2 = x2_ref[...]
        r = lax.rsqrt(jnp.mean(x2 * x2, axis=-1, keepdims=True) + EPS)
        xn = x2 * r
        dg_ref[...] += jnp.sum(dh * xn, axis=0, keepdims=True)
        dxn = dh * g_ref[...]
        dx_ref[...] = dy + r * (dxn - xn * jnp.mean(dxn * xn, axis=-1, keepdims=True))

    row = pl.BlockSpec((tm, D_MODEL), lambda i: (i, 0))
    hid = pl.BlockSpec((N_CHIP, tm, FF_SH), lambda i: (0, i, 0))
    vec = pl.BlockSpec((1, D_MODEL), lambda i: (0, 0))
    anyspec = pl.BlockSpec(memory_space=pl.ANY)
    wshape = pltpu.VMEM((N_CHIP, FF_SH, D_MODEL), BF)
    return pl.pallas_call(
        body, name="ffn_bwd", grid=(T // tm,),
        in_specs=[row, hid, hid, row, vec, anyspec, anyspec, anyspec],
        out_specs=[hid, hid, row, vec],
        out_shape=[jax.ShapeDtypeStruct((N_CHIP, T, FF_SH), BF), jax.ShapeDtypeStruct((N_CHIP, T, FF_SH), BF),
                   jax.ShapeDtypeStruct((T, D_MODEL), F32), jax.ShapeDtypeStruct((1, D_MODEL), F32)],
        scratch_shapes=[wshape, wshape, wshape, pltpu.SemaphoreType.DMA((3,))],
        compiler_params=_params(("arbitrary",), VMEM_HUGE),
    )(dy, sa, sb, x2, g_ffn, w_gate, w_up, w_down)


def _out_bwd(dx2, z_a, y_r, y_f, o_raw, o_fox, g_ret, w_ro, w_fo, w_out, tm=256, push=None):
    T = dx2.shape[0]

    def body(dx_ref, gt_ref, ar_ref, af_ref, yr_ref, yf_ref, o_ref, of_ref, g_ref, wro_ref, wfo_ref, wout_ref,
             dyr_ref, dyf_ref, dgt_ref, da_ref, do_ref, dof_ref, dg_ref):
        i = pl.program_id(0)

        @pl.when(i == 0)
        def _():
            dg_ref[...] = jnp.zeros(dg_ref.shape, F32)

        dxb = dx_ref[...].astype(BF)
        dm = jnp.concatenate([_nt(dxb, wout_ref[k]) for k in range(N_CHIP)], axis=-1)
        sr, sf = _sigmoid(ar_ref[...]), _sigmoid(af_ref[...])
        dyr = dm * sr
        dyf = dm * sf
        da_ref[:, :1024] = (dyr * yr_ref[...] * (1.0 - sr)).astype(BF)
        da_ref[:, 1024:] = (dyf * yf_ref[...] * (1.0 - sf)).astype(BF)
        dyr = dyr.astype(BF)
        dyf = dyf.astype(BF)
        dyr_ref[...] = dyr
        dyf_ref[...] = dyf
        du = jnp.zeros((tm, 512), F32)
        doc = jnp.zeros((tm, 512), F32)
        for k in range(N_CHIP):
            du = du + _nt(dyr[:, 256 * k:256 * k + 256], wro_ref[k])
            doc = doc + _nt(dyf[:, 256 * k:256 * k + 256], wfo_ref[k])

        for h in range(RET_H):
            cols = slice(h * RET_DV, (h + 1) * RET_DV)
            o = o_ref[:, cols]
            mu = jnp.mean(o, axis=-1, keepdims=True)
            xc = o - mu
            rstd = lax.rsqrt(jnp.mean(xc * xc, axis=-1, keepdims=True) + EPS)
            on = xc * rstd
            g = g_ref[:, cols]
            gt = gt_ref[:, cols]
            sg = _sigmoid(gt)
            duh = du[:, cols]
            dgt_ref[:, cols] = (duh * (on * g) * sg * (1.0 + gt * (1.0 - sg))).astype(BF)
            dog = duh * gt * sg
            dg_ref[:, cols] += jnp.sum(dog * on, axis=0, keepdims=True)
            don = dog * g
            do_ref[:, cols] = rstd * (don - jnp.mean(don, axis=-1, keepdims=True)
                                      - on * jnp.mean(don * on, axis=-1, keepdims=True))

        lane = lax.broadcasted_iota(jnp.int32, (tm, LANE), 1)
        zpad = jnp.zeros((tm, 64), F32)
        for h in range(FOX_H):
            doh = doc[:, 64 * h:64 * h + 64]
            delta = jnp.sum(doh * of_ref[h][:, :FOX_D], axis=-1, keepdims=True)
            hi, mid, lo = [t.astype(F32) for t in _split3(-delta)]
            da = jnp.concatenate([doh, zpad], axis=-1)
            da = jnp.where(lane == 64, hi, jnp.where(lane == 65, mid, jnp.where(lane == 66, lo, da)))
            dof_ref[h] = da.astype(BF)

    row = lambda w: pl.BlockSpec((tm, w), lambda i: (i, 0))
    const = lambda shp: pl.BlockSpec(shp, lambda i: (0,) * len(shp))
    hsp = pl.BlockSpec((FOX_H, tm, LANE), lambda i: (0, i, 0))
    return _hosted_call(
        body, "out_bwd", (T // tm,),
        [row(1024), pl.BlockSpec((tm, 512), lambda i: (i, 2)), pl.BlockSpec((tm, 1024), lambda i: (i, 3)),
         pl.BlockSpec((tm, 1024), lambda i: (i, 4)), row(1024), row(1024), row(512), hsp,
         const((1, 512)), const((N_CHIP, 512, 256)), const((N_CHIP, 512, 256)), const((N_CHIP, 256, 1024))],
        [row(1024), row(1024), row(512), row(2048), row(512), hsp, const((1, 512))],
        [jax.ShapeDtypeStruct((T, 1024), BF), jax.ShapeDtypeStruct((T, 1024), BF),
         jax.ShapeDtypeStruct((T, 512), BF), jax.ShapeDtypeStruct((T, 2048), BF),
         jax.ShapeDtypeStruct((T, 512), F32), jax.ShapeDtypeStruct((FOX_H, T, LANE), BF),
         jax.ShapeDtypeStruct((1, 512), F32)],
        [], VMEM_BIG, (dx2, z_a, z_a, z_a, y_r, y_f, o_raw, o_fox, g_ret, w_ro, w_fo, w_out), push)


def _ret_bwd(d_o, qr, kr, z_a, states, cos_t, sin_t, consts, tt=512, push=None):
    T = z_a.shape[0]
    nt = T // tt
    nch = tt // CHUNK
    decay, zeta, xi, gcb = consts

    def body(do_ref, q_ref, k_ref, v_ref, st_ref, cos_ref, sin_ref, d_ref, ze_ref, xi_ref, gc_ref, dz_ref, g_sc):
        i = pl.program_id(0)

        @pl.when(i == 0)
        def _():
            g_sc[...] = jnp.zeros(g_sc.shape, F32)

        for c in reversed(range(nch)):
            rows = slice(c * CHUNK, (c + 1) * CHUNK)
            cosv, sinv = cos_ref[rows, :], sin_ref[rows, :]
            dq_parts, dk_parts = [], []
            for h in range(RET_H):
                cols = slice(h * RET_DV, (h + 1) * RET_DV)
                q, k = q_ref[h, rows, :], k_ref[h, rows, :]
                v32 = v_ref[rows, cols]
                vb = v32.astype(BF)
                r = st_ref[h, rows, :]
                g = g_sc[h]
                gb = g.astype(BF)
                d_o = do_ref[rows, cols]
                dob = d_o.astype(BF)
                dox = (d_o * xi_ref[h]).astype(BF)
                dec = d_ref[h]
                s = (_nt(q, k) * dec).astype(BF)
                ds = (_nt(dob, vb) * dec).astype(BF)
                dv = _tn(s, dob) + ze_ref[h] * _nn(k, gb)
                dq = _nn(ds, k) + _nt(dox, r.astype(BF))
                dk = _tn(ds, q) + _nt((v32 * ze_ref[h]).astype(BF), gb)
                g_sc[h] = gc_ref[h] * g + _tn(q, dox)
                dq_parts.append((dq * cosv - _swap32(dq) * sinv)[:, :64])
                dk_parts.append(((dk * cosv - _swap32(dk) * sinv) * 0.125)[:, :64])
                dz_ref[rows, 512 + h * RET_DV:512 + (h + 1) * RET_DV] = dv.astype(BF)
            dz_ref[rows, 0:256] = jnp.concatenate(dq_parts, axis=-1).astype(BF)
            dz_ref[rows, 256:512] = jnp.concatenate(dk_parts, axis=-1).astype(BF)

    rev = lambda i: nt - 1 - i
    hspec = pl.BlockSpec((RET_H, tt, LANE), lambda i: (0, rev(i), 0))
    cspec = pl.BlockSpec((RET_H, CHUNK, LANE), lambda i: (0, 0, 0))
    tab = pl.BlockSpec((tt, LANE), lambda i: (rev(i), 0))
    (dz,), lands = _hosted_call(
        body, "ret_bwd", (nt,),
        [pl.BlockSpec((tt, 512), lambda i: (rev(i), 0)), hspec, hspec,
         pl.BlockSpec((tt, 512), lambda i: (rev(i), 1)), hspec, tab, tab, cspec, cspec, cspec, cspec],
        [pl.BlockSpec((tt, 1024), lambda i: (rev(i), 0))], [jax.ShapeDtypeStruct((T, 1024), BF)],
        [pltpu.VMEM((RET_H, CHUNK, LANE), F32)], VMEM_BIG,
        (d_o, qr, kr, z_a, states, cos_t, sin_t, decay, zeta, xi, gcb), push)
    return dz, lands


def _fox_bwd(iend, q2, k, v, do, sub=512):
    H, T, _ = k.shape
    tb = 2 * sub

    def body(ie_ref, q_ref, do_ref, k_ref, v_ref, dq_ref, dk_ref, dv_ref, dk_sc, dv_sc):
        j = pl.program_id(1)
        n = ie_ref[pl.program_id(0), j]

        @pl.when(j == 0)
        def _():
            dq_ref[...] = jnp.zeros(dq_ref.shape, F32)

        kk, vv = k_ref[...], v_ref[...]
        dk_sc[...] = jnp.zeros(dk_sc.shape, F32)
        dv_sc[...] = jnp.zeros(dv_sc.shape, F32)
        krow = lax.broadcasted_iota(jnp.int32, (tb, sub), 0)
        qcol = lax.broadcasted_iota(jnp.int32, (tb, sub), 1)

        def step(i, shift):
            off = pl.multiple_of(i * sub, sub)
            qq = q_ref[pl.ds(off, sub), :]
            dd = do_ref[pl.ds(off, sub), :]
            p = jnp.exp(_nt(kk, qq))
            if shift is not None:
                p = jnp.where(qcol + shift >= krow, p, 0.0)
            ds = (p * _nt(vv, dd)).astype(BF)
            dv_sc[...] += _nn(p.astype(BF), dd)
            dk_sc[...] += _nn(ds, qq)
            dq_ref[pl.ds(off, sub), :] += _tn(ds, kk)

        off0 = pl.multiple_of(2 * j * sub, sub)
        q0, d0 = q_ref[pl.ds(off0, sub), :], do_ref[pl.ds(off0, sub), :]
        k0, v0 = k_ref[0:sub, :], v_ref[0:sub, :]
        p0 = jnp.where(qcol[0:sub, :] >= krow[0:sub, :], jnp.exp(_nt(k0, q0)), 0.0)
        ds0 = (p0 * _nt(v0, d0)).astype(BF)
        dv_sc[0:sub, :] += _nn(p0.astype(BF), d0)
        dk_sc[0:sub, :] += _nn(ds0, q0)
        dq_ref[pl.ds(off0, sub), :] += _tn(ds0, k0)
        step(2 * j + 1, sub)

        def loop_body(i, carry):
            step(i, None)
            return carry

        lax.fori_loop(2 * j + 2, n, loop_body, 0)
        dk_ref[...] = dk_sc[...]
        dv_ref[...] = dv_sc[...]

    blk = pl.BlockSpec((None, tb, LANE), lambda h, j, ie: (h, j, 0))
    full = pl.BlockSpec((None, T, LANE), lambda h, j, ie: (h, 0, 0))
    shp = jax.ShapeDtypeStruct((H, T, LANE), F32)
    return pl.pallas_call(
        body, name="fox_bwd",
        grid_spec=pltpu.PrefetchScalarGridSpec(
            num_scalar_prefetch=1, grid=(H, T // tb), in_specs=[full, full, blk, blk], out_specs=[full, blk, blk],
            scratch_shapes=[pltpu.VMEM((tb, LANE), F32), pltpu.VMEM((tb, LANE), F32)]),
        out_shape=[shp, shp, shp],
        compiler_params=_params(("arbitrary", "arbitrary"), VMEM_BIG),
    )(iend, q2, do, k, v)


def _fox_post_bwd(dq, dk, dv, z_a, z_ff, b_f, g_q, g_k, tm=256, push=None):
    T = z_a.shape[0]
    nt = T // tm

    def body(dq_ref, dk_ref, dv_ref, zf_ref, zff_ref, b_ref, gq_ref, gk_ref,
             dz_ref, dff_ref, dgq_ref, dgk_ref, db_ref, carry):
        i = pl.program_id(0)

        @pl.when(i == 0)
        def _():
            carry[...] = jnp.zeros(carry.shape, F32)
            dgq_ref[...] = jnp.zeros(dgq_ref.shape, F32)
            dgk_ref[...] = jnp.zeros(dgk_ref.shape, F32)
            db_ref[...] = jnp.zeros(db_ref.shape, F32)

        lane = lax.broadcasted_iota(jnp.int32, (tm, LANE), 1)
        zf = zf_ref[...]
        dcm = jnp.zeros((tm, LANE), F32)
        dq_parts, dk_parts, dv_parts = [], [], []
        gq_acc = jnp.zeros((1, 64), F32)
        gk_acc = jnp.zeros((1, 64), F32)
        for h in range(FOX_H):
            dqa, dka = dq_ref[h], dk_ref[h]
            dcm = jnp.where(lane == h, dqa[:, L_CQ:L_CQ + 1] - dka[:, L_CK:L_CK + 1], dcm)
            for src, dya, g_ref, scale, parts in ((0, dqa, gq_ref, 0.125, dq_parts), (512, dka, gk_ref, 1.0, dk_parts)):
                xh = zf[:, src + 64 * h:src + 64 * h + 64]
                r = lax.rsqrt(jnp.mean(xh * xh, axis=-1, keepdims=True) + EPS)
                xn = xh * r
                dy = dya[:, :FOX_D] * scale
                if src == 0:
                    gq_acc = gq_acc + jnp.sum(dy * xn, axis=0, keepdims=True)
                else:
                    gk_acc = gk_acc + jnp.sum(dy * xn, axis=0, keepdims=True)
                dxn = dy * g_ref[...]
                parts.append(r * (dxn - xn * jnp.mean(dxn * xn, axis=-1, keepdims=True)))
            dv_parts.append(dv_ref[h][:, :FOX_D])
        dz_ref[...] = jnp.concatenate(dq_parts + dk_parts + dv_parts, axis=-1).astype(BF)
        zpad = jnp.zeros((1, 64), F32)
        dgq_ref[...] += jnp.concatenate([gq_acc, zpad], axis=-1)
        dgk_ref[...] += jnp.concatenate([gk_acc, zpad], axis=-1)

        row = lax.broadcasted_iota(jnp.int32, (tm, tm), 0)
        col = lax.broadcasted_iota(jnp.int32, (tm, tm), 1)
        tri = (row <= col).astype(BF)
        hi, mid, lo = _split3(dcm)
        dlogf = _nn(tri, hi) + _nn(tri, mid) + _nn(tri, lo) + carry[...]
        carry[...] = dlogf[0:1, :]
        dff = jnp.where(lane < FOX_H, dlogf * _sigmoid(-(zff_ref[...] + b_ref[...])), 0.0)
        dff_ref[...] = dff.astype(BF)
        db_ref[...] += jnp.sum(dff, axis=0, keepdims=True)

    rev = lambda i: nt - 1 - i
    hsp = pl.BlockSpec((FOX_H, tm, LANE), lambda i: (0, rev(i), 0))
    small = lambda w: pl.BlockSpec((1, w), lambda i: (0, 0))
    return _hosted_call(
        body, "fox_post_bwd", (nt,),
        [hsp, hsp, hsp, pl.BlockSpec((tm, 1536), lambda i: (rev(i), 1)),
         pl.BlockSpec((tm, LANE), lambda i: (rev(i), 0)), small(LANE), small(64), small(64)],
        [pl.BlockSpec((tm, 1536), lambda i: (rev(i), 0)), pl.BlockSpec((tm, LANE), lambda i: (rev(i), 0)),
         small(LANE), small(LANE), small(LANE)],
        [jax.ShapeDtypeStruct((T, 1536), BF), jax.ShapeDtypeStruct((T, LANE), BF),
         jax.ShapeDtypeStruct((1, LANE), F32), jax.ShapeDtypeStruct((1, LANE), F32),
         jax.ShapeDtypeStruct((1, LANE), F32)],
        [pltpu.VMEM((1, LANE), F32)], VMEM_BIG, (dq, dk, dv, z_a, z_ff, b_f, g_q, g_k), push)


def _in_bwd(dz_ret, dz_gt, dz_fox, dz_a, dz_ff, w_a, w_ff, x, g_mix, dx2, tm=256, push=None):
    T = x.shape[0]

    def body(r_ref, t_ref, f_ref, a_ref, ff_ref, wa_ref, wf_ref, x_ref, g_ref, dx2_ref, dx_ref, dg_ref):
        i = pl.program_id(0)

        @pl.when(i == 0)
        def _():
            dg_ref[...] = jnp.zeros(dg_ref.shape, F32)

        dh = (_nt(r_ref[...], wa_ref[:, C_RET:C_GT]) + _nt(t_ref[...], wa_ref[:, C_GT:C_FOX])
              + _nt(f_ref[...], wa_ref[:, C_FOX:C_A]) + _nt(a_ref[...], wa_ref[:, C_A:C_END])
              + _nt(ff_ref[...], wf_ref[...]))
        xv = x_ref[...]
        r = lax.rsqrt(jnp.mean(xv * xv, axis=-1, keepdims=True) + EPS)
        xn = xv * r
        dg_ref[...] += jnp.sum(dh * xn, axis=0, keepdims=True)
        dxn = dh * g_ref[...]
        dx_ref[...] = dx2_ref[...] + r * (dxn - xn * jnp.mean(dxn * xn, axis=-1, keepdims=True))

    row = lambda w: pl.BlockSpec((tm, w), lambda i: (i, 0))
    const = lambda shp: pl.BlockSpec(shp, lambda i: (0,) * len(shp))
    return _hosted_call(
        body, "in_bwd", (T // tm,),
        [row(1024), row(512), row(1536), row(2048), row(LANE), const((D_MODEL, C_END)),
         const((D_MODEL, LANE)), row(1024), const((1, 1024)), row(1024)],
        [row(1024), const((1, 1024))],
        [jax.ShapeDtypeStruct((T, 1024), F32), jax.ShapeDtypeStruct((1, 1024), F32)],
        [], VMEM_BIG, (dz_ret, dz_gt, dz_fox, dz_a, dz_ff, w_a, w_ff, x, g_mix, dx2), push)


def _mesh_pos():
    return lax.axis_index("x"), lax.axis_index("y"), lax.axis_index("c")


def _staged_place(src, name):
    stacked = src.ndim == 3
    R, C = src.shape[-2:]
    tr = _row_tile(R, 128, 16)
    n = R // tr
    assert n >= 2

    def body(s_ref, o_ref, buf, sem):
        i = pl.program_id(0)
        slot = i % 2
        x, y, _ = _mesh_pos()
        kme = 2 * x + y

        def out_copy(s, step):
            return pltpu.make_async_copy(buf.at[s], o_ref.at[kme, pl.ds(pl.multiple_of(step * tr, tr), tr), :], sem.at[s])

        @pl.when(i >= 2)
        def _():
            out_copy(slot, i - 2).wait()

        buf[slot] = (s_ref[kme] if stacked else s_ref[...]).astype(BF)
        out_copy(slot, i).start()

        @pl.when(i == n - 1)
        def _():
            out_copy(1 - slot, i - 1).wait()
            out_copy(slot, i).wait()

    in_spec = (pl.BlockSpec((N_CHIP, tr, C), lambda i: (0, i, 0)) if stacked else pl.BlockSpec((tr, C), lambda i: (i, 0)))
    return pl.pallas_call(
        body, name=name, grid=(n,), in_specs=[in_spec], out_specs=pl.BlockSpec(memory_space=pl.ANY),
        out_shape=jax.ShapeDtypeStruct((N_CHIP, R, C), BF),
        scratch_shapes=[pltpu.VMEM((2, tr, C), BF), pltpu.SemaphoreType.DMA((2,))],
        compiler_params=_params(("arbitrary",)),
    )(src)


def _push_copies(src, land, send_sem, recv_sem, receiving):
    x, y, c = _mesh_pos()
    kme = 2 * x + y
    cps = []
    for w in range(len(land)):
        for j, (px, py) in enumerate([(1 - x, y), (x, 1 - y), (1 - x, 1 - y)]):
            kpeer = 2 * px + py
            cps.append(pltpu.make_async_remote_copy(
                src_ref=land[w].at[kme] if src is None else src[w].at[kpeer],
                dst_ref=land[w].at[kpeer if receiving else kme],
                send_sem=send_sem.at[3 * w + j], recv_sem=recv_sem.at[3 * w + j],
                device_id=(px, py, c), device_id_type=MESH))
    return cps


def _gather_two_level(stack, name):
    _, R, C = stack.shape
    hr = R // 2

    def body(_, land, send_sem, recv_sem):
        x, y, c = _mesh_pos()
        kme = 2 * x + y
        chips = [(1 - x, y), (x, 1 - y), (1 - x, 1 - y)]

        def rows(k, core):
            return land.at[k, pl.ds(pl.multiple_of(core * hr, hr), hr), :]

        def copy(idx, k, core, to):
            return pltpu.make_async_remote_copy(src_ref=rows(k, core), dst_ref=rows(k, core), send_sem=send_sem.at[idx],
                                                recv_sem=recv_sem.at[idx], device_id=to, device_id_type=MESH)

        first = [copy(j, kme, c, (px, py, c)) for j, (px, py) in enumerate(chips)]
        for cp in first:
            cp.start()
        passed = [copy(3 + j, 2 * px + py, c, (x, y, 1 - c)) for j, (px, py) in enumerate(chips)]
        for j, (px, py) in enumerate(chips):
            copy(j, 2 * px + py, c, (px, py, c)).wait_recv()
            passed[j].start()
        for j, (px, py) in enumerate(chips):
            copy(3 + j, 2 * px + py, 1 - c, (x, y, 1 - c)).wait_recv()
        for cp in first + passed:
            cp.wait_send()

    anyspec = pl.BlockSpec(memory_space=pl.ANY)
    return pl.pallas_call(
        body, name=name, in_specs=[anyspec], out_specs=anyspec,
        out_shape=jax.ShapeDtypeStruct(stack.shape, stack.dtype), input_output_aliases={0: 0},
        scratch_shapes=[pltpu.SemaphoreType.DMA((6,)), pltpu.SemaphoreType.DMA((6,))],
    )(stack)


def _gather_small(small):
    def body(sv, svo, ssend, srecv, sloc):
        x, y, c = _mesh_pos()
        me = 4 * x + 2 * y + c
        flips = [(b >> 2 & 1, b >> 1 & 1, b & 1) for b in range(1, 8)]
        others = [(1 - x if fx else x, 1 - y if fy else y, 1 - c if fc else c) for fx, fy, fc in flips]
        local = pltpu.make_async_copy(sv, svo.at[me], sloc)
        local.start()
        sends = []
        for j, (px, py, pc) in enumerate(others):
            cp = pltpu.make_async_remote_copy(
                src_ref=sv, dst_ref=svo.at[me], send_sem=ssend.at[j], recv_sem=srecv.at[j],
                device_id=(px, py, pc), device_id_type=MESH)
            cp.start()
            sends.append(cp)
        for j, (px, py, pc) in enumerate(others):
            pltpu.make_async_remote_copy(
                src_ref=sv, dst_ref=svo.at[4 * px + 2 * py + pc], send_sem=ssend.at[j], recv_sem=srecv.at[j],
                device_id=(px, py, pc), device_id_type=MESH).wait_recv()
        for cp in sends:
            cp.wait_send()
        local.wait()

    anyspec = pl.BlockSpec(memory_space=pl.ANY)
    return pl.pallas_call(
        body, name="gather_small", in_specs=[anyspec], out_specs=anyspec,
        out_shape=jax.ShapeDtypeStruct((8,) + small.shape, small.dtype),
        scratch_shapes=[pltpu.SemaphoreType.DMA((7,)), pltpu.SemaphoreType.DMA((7,)), pltpu.SemaphoreType.DMA],
    )(small)


def _sibling_exchange(arrs):
    n = len(arrs)

    def body(*refs):
        ins, outs = refs[:n], refs[n:2 * n]
        send_sems, recv_sems = refs[2 * n:]
        x, y, c = _mesh_pos()
        cps = [pltpu.make_async_remote_copy(
            src_ref=ins[w], dst_ref=outs[w], send_sem=send_sems.at[w], recv_sem=recv_sems.at[w],
            device_id=(x, y, 1 - c), device_id_type=MESH) for w in range(n)]
        for cp in cps:
            cp.start()
        for cp in cps:
            cp.wait_recv()
        for cp in cps:
            cp.wait_send()

    anyspec = pl.BlockSpec(memory_space=pl.ANY)
    return pl.pallas_call(
        body, name="sibling_exchange",
        in_specs=[anyspec] * n, out_specs=[anyspec] * n,
        out_shape=[jax.ShapeDtypeStruct(a.shape, a.dtype) for a in arrs],
        scratch_shapes=[pltpu.SemaphoreType.DMA((n,)), pltpu.SemaphoreType.DMA((n,))],
    )(*arrs)


def _sum_stack(own, recv, name):
    _, R, C = recv.shape
    tr = _row_tile(R, 256, 16)

    def body(g_ref, r_ref, o_ref):
        x, y, _ = _mesh_pos()
        kme = 2 * x + y
        acc = g_ref[kme].astype(F32)
        for d in range(1, N_CHIP):
            acc = acc + r_ref[(kme + d) % N_CHIP].astype(F32)
        o_ref[...] = acc

    spec = pl.BlockSpec((N_CHIP, tr, C), lambda i: (0, i, 0))
    return pl.pallas_call(
        body, name=name, grid=(R // tr,), in_specs=[spec, spec],
        out_specs=pl.BlockSpec((tr, C), lambda i: (i, 0)),
        out_shape=jax.ShapeDtypeStruct((R, C), F32),
        compiler_params=_params(("parallel",)),
    )(own, recv)


def _adam_math(w, g, m, v):
    m2 = ADAM_B1 * m + (1.0 - ADAM_B1) * g
    v2 = ADAM_B2 * v + (1.0 - ADAM_B2) * (g * g)
    m_hat = m2 / (1.0 - ADAM_B1 ** ADAM_STEP)
    v_hat = v2 / (1.0 - ADAM_B2 ** ADAM_STEP)
    delta = -ADAM_LR * (m_hat / (jnp.sqrt(v_hat) + ADAM_EPS) + ADAM_WD * w)
    return delta, m2, v2


def _adamw(w, m, v, s0, s1, name):
    R, C = w.shape
    tr = _row_tile(R, 128, 8)

    def body(w_ref, m_ref, v_ref, a_ref, b_ref, g_ref, d_ref, m2_ref, v2_ref):
        g = a_ref[...] + b_ref[...]
        delta, m2, v2 = _adam_math(w_ref[...], g, m_ref[...], v_ref[...])
        g_ref[...] = g
        d_ref[...] = delta
        m2_ref[...] = m2
        v2_ref[...] = v2

    spec = pl.BlockSpec((tr, C), lambda i: (i, 0))
    shp = jax.ShapeDtypeStruct((R, C), F32)
    return pl.pallas_call(
        body, name=name, grid=(R // tr,), in_specs=[spec] * 5, out_specs=[spec] * 4, out_shape=[shp] * 4,
        compiler_params=_params(("parallel",), VMEM_BIG),
    )(w, m, v, s0, s1)


def _adamw_small(w, m, v, gathered):
    def body(w_ref, m_ref, v_ref, s_ref, g_ref, d_ref, m2_ref, v2_ref):
        g = s_ref[0]
        for d in range(1, 8):
            g = g + s_ref[d]
        delta, m2, v2 = _adam_math(w_ref[...], g, m_ref[...], v_ref[...])
        g_ref[...] = g
        d_ref[...] = delta
        m2_ref[...] = m2
        v2_ref[...] = v2

    shp = jax.ShapeDtypeStruct(w.shape, F32)
    return pl.pallas_call(body, name="adamw_small", out_shape=[shp] * 4)(w, m, v, gathered)


SMALL = (("g_mix", 1024), ("g_ffn", 1024), ("g_ret_norm", 512), ("g_fox_q", 64), ("g_fox_k", 64), ("b_forget", 8))
SMALL_W = 3072


def _pack_small(parts):
    cols = []
    for (name, n) in SMALL:
        p = parts[name].reshape(1, -1)[:, :n]
        pad = -n % LANE
        cols.append(jnp.pad(p, ((0, 0), (0, pad))) if pad else p)
    used = sum(c.shape[1] for c in cols)
    cols.append(jnp.zeros((1, SMALL_W - used), F32))
    return jnp.concatenate(cols, axis=1)


def _unpack_small(vec):
    out, off = {}, 0
    for (name, n) in SMALL:
        out[name] = vec[:, off:off + n]
        off += n + (-n % LANE)
    return out


def kernel(x, g_mix, w_in, b_forget, g_ret_norm, w_ret_o, g_fox_q, g_fox_k, w_fox_o, w_out, g_ffn, w_gate, w_up, w_down, loss_target, m_g_mix, m_w_in, m_b_forget, m_g_ret_norm, m_w_ret_o, m_g_fox_q, m_g_fox_k, m_w_fox_o, m_w_out, m_g_ffn, m_w_gate, m_w_up, m_w_down, v_g_mix, v_w_in, v_b_forget, v_g_ret_norm, v_w_ret_o, v_g_fox_q, v_g_fox_k, v_w_fox_o, v_w_out, v_g_ffn, v_w_gate, v_w_up, v_w_down):
    T = x.shape[1]
    xs = x[0]
    tgt = loss_target[0]
    big_names = ("w_in", "w_ret_o", "w_fox_o", "w_out", "w_gate", "w_up", "w_down")
    tr = lambda a: jnp.swapaxes(a[0], 0, 1)
    big_w = dict(w_in=w_in[0], w_ret_o=w_ret_o[0], w_fox_o=w_fox_o[0], w_out=w_out[0], w_gate=tr(w_gate),
                 w_up=tr(w_up), w_down=w_down[0])
    big_m = dict(w_in=m_w_in[0], w_ret_o=m_w_ret_o[0], w_fox_o=m_w_fox_o[0], w_out=m_w_out[0], w_gate=tr(m_w_gate),
                 w_up=tr(m_w_up), w_down=m_w_down[0])
    big_v = dict(w_in=v_w_in[0], w_ret_o=v_w_ret_o[0], w_fox_o=v_w_fox_o[0], w_out=v_w_out[0], w_gate=tr(v_w_gate),
                 w_up=tr(v_w_up), w_down=v_w_down[0])
    small_w = dict(g_mix=g_mix, g_ffn=g_ffn, g_ret_norm=g_ret_norm, g_fox_q=g_fox_q, g_fox_k=g_fox_k, b_forget=b_forget)
    small_m = dict(g_mix=m_g_mix, g_ffn=m_g_ffn, g_ret_norm=m_g_ret_norm, g_fox_q=m_g_fox_q, g_fox_k=m_g_fox_k,
                   b_forget=m_b_forget)
    small_v = dict(g_mix=v_g_mix, g_ffn=v_g_ffn, g_ret_norm=v_g_ret_norm, g_fox_q=v_g_fox_q, g_fox_k=v_g_fox_k,
                   b_forget=v_b_forget)

    stacks = {n: _staged_place(big_w[n], "place_" + n) for n in big_names}
    s_in = _gather_two_level(stacks["w_in"], "gather_w_in")
    w_a, w_ff = _assemble_w_in(s_in)
    b_pad = jnp.pad(b_forget, ((0, 0), (0, LANE - FOX_H)))
    cos_t, sin_t = _rope_tables(T)
    consts = _ret_consts()

    h = _rms_cast(xs, g_mix)
    z_a, (s_gate, s_up) = _mm_nn(h, w_a, "proj_in", push=(None, [stacks["w_gate"], stacks["w_up"]]))
    z_ff, _ = _mm_nn(h, w_ff, "proj_ff")
    (qr, kr, qf, kf, vf, c_cum, qmax, kmax), (s_down, s_ro, s_fo, s_out) = _mix_prep(
        z_a, z_ff, cos_t, sin_t, b_pad, g_fox_q, g_fox_k,
        push=(None, [stacks["w_down"], stacks["w_ret_o"], stacks["w_fox_o"], stacks["w_out"]]))
    jstart, iend = _prune_tables(c_cum, qmax, kmax, 512)
    o_raw, u_r, states = _ret_fwd(qr, kr, z_a, g_ret_norm, consts)
    o_fox, q2 = _fox_fwd(jstart, qf, kf, vf)
    y_r, y_f, mrg, x2, h2, o_cat = _merge_out(u_r, o_fox, z_a, xs, g_ffn, s_ro, s_fo, s_out)
    sa, sb, act, dy, loss_vec = _ffn_fwd(h2, x2, tgt, s_gate, s_up, s_down)
    loss = lax.psum(0.5 / D_MODEL * jnp.sum(loss_vec), ("x", "y", "c"))

    def scatter_job(grads):
        return (grads, [lax.empty(g.shape, g.dtype) for g in grads])

    dgp, dup, dx2, dg_ffn = _ffn_bwd(dy, sa, sb, x2, g_ffn, s_gate, s_up, s_down)
    g_gate, g_up, g_down = (_grad_astack(dgp, h2, "gw_gate"), _grad_astack(dup, h2, "gw_up"),
                            _grad_astack(act, dy, "gw_down"))
    (d_yr, d_yf, dz_gt, dz_a, d_o, do_fox, dg_ret), (r_gate, r_up) = _out_bwd(
        dx2, z_a, y_r, y_f, o_raw, o_fox, g_ret_norm, s_ro, s_fo, s_out,
        push=scatter_job([g_gate, g_up]))
    dz_ret, (r_down,) = _ret_bwd(d_o, qr, kr, z_a, states, cos_t, sin_t, consts, push=scatter_job([g_down]))
    dq_f, dk_f, dv_f = _fox_bwd(iend, q2, kf, vf, do_fox)
    g_mid = [_grad_colstack(u_r, d_yr, "gw_ret_o", 256), _grad_colstack(o_cat, d_yf, "gw_fox_o", 256),
             _grad_plain(mrg, dx2, "gw_out", BF).reshape(N_CHIP, 256, D_MODEL)]
    (dz_fox, dz_ff, dg_q, dg_k, db_f), (r_ro, r_fo, r_out) = _fox_post_bwd(
        dq_f, dk_f, dv_f, z_a, z_ff, b_pad, g_fox_q, g_fox_k, push=scatter_job(g_mid))
    g_in = _pack_g_in(_grad_plain(h, dz_ret, "gw_in_ret", F32), _grad_plain(h, dz_gt, "gw_in_gt", F32),
                      _grad_plain(h, dz_fox, "gw_in_fox", F32, tn=768), _grad_plain(h, dz_a, "gw_in_a", F32),
                      _grad_plain(h, dz_ff, "gw_in_ff", F32))
    (grad_x, dg_mix), (r_in,) = _in_bwd(dz_ret, dz_gt, dz_fox, dz_a, dz_ff, w_a, w_ff, xs, g_mix, dx2,
                                        push=scatter_job([g_in]))
    small_g = _pack_small(dict(g_mix=dg_mix, g_ffn=dg_ffn, g_ret_norm=dg_ret, g_fox_q=dg_q, g_fox_k=dg_k, b_forget=db_f))

    small_all = _gather_small(small_g)
    sums = [_sum_stack(g, r, "sum_" + n) for g, r, n in zip(
        [g_in] + g_mid + [g_gate, g_up, g_down], [r_in, r_ro, r_fo, r_out, r_gate, r_up, r_down], big_names)]
    sib = _sibling_exchange(sums)
    big_out = {n: _adamw(big_w[n], big_m[n], big_v[n], sums[i], sib[i], "adamw_" + n) for i, n in enumerate(big_names)}
    sg, sd, sm, sv = _adamw_small(_pack_small(small_w), _pack_small(small_m), _pack_small(small_v), small_all)
    small_out = [_unpack_small(t) for t in (sg, sd, sm, sv)]

    order = ("g_mix", "w_in", "b_forget", "g_ret_norm", "w_ret_o", "g_fox_q", "g_fox_k", "w_fox_o", "w_out", "g_ffn",
             "w_gate", "w_up", "w_down")
    outs = [loss, grad_x[None]]
    for idx in range(4):
        for n in order:
            if n in ("w_gate", "w_up"):
                outs.append(jnp.swapaxes(big_out[n][idx], 0, 1)[None])
            else:
                outs.append(big_out[n][idx][None] if n in big_out else small_out[idx][n])
    return tuple(outs)
```

```python
import functools
import math

import numpy as np
import jax
import jax.numpy as jnp
from jax import lax
from jax.experimental import pallas as pl
from jax.experimental.pallas import tpu as pltpu

F32 = jnp.float32
BF = jnp.bfloat16
MESH = pl.DeviceIdType.MESH

D_MODEL = 1024
D_FF = 2816
N_CHIP = 4
FF_SH = D_FF // N_CHIP
IN_COLS = 5128
IN_SH = IN_COLS // N_CHIP
RET_H, RET_DV = 4, 128
FOX_H, FOX_D = 8, 64
CHUNK = 128
EPS = 1e-6
NEG = -1e30
LANE = 128
C_RET, C_GT, C_FOX, C_A, C_END = 0, 1024, 1536, 3072, 5120
L_CQ, L_CK, L_LSE, L_MAX = 64, 67, 70, 73

ADAM_LR, ADAM_B1, ADAM_B2, ADAM_EPS, ADAM_WD, ADAM_STEP = 0.001, 0.9, 0.999, 1e-08, 0.01, 10
VMEM_BIG = 56 * 1024 * 1024
VMEM_HUGE = 60 * 1024 * 1024
GRAD_TK = 2048
FFN_TM = 512


def _nn(a, b):
    return lax.dot_general(a, b, (((1,), (0,)), ((), ())), preferred_element_type=F32)


def _nt(a, b):
    return lax.dot_general(a, b, (((1,), (1,)), ((), ())), preferred_element_type=F32)


def _tn(a, b):
    return lax.dot_general(a, b, (((0,), (0,)), ((), ())), preferred_element_type=F32)


def _split3(x):
    hi = x.astype(BF)
    r = x - hi.astype(F32)
    mid = r.astype(BF)
    lo = (r - mid.astype(F32)).astype(BF)
    return hi, mid, lo


def _sigmoid(x):
    return 0.5 * jnp.tanh(0.5 * x) + 0.5


def _swap32(x):
    lane = lax.broadcasted_iota(jnp.int32, x.shape, 1)
    return jnp.where(lane < 32, pltpu.roll(x, 96, 1), pltpu.roll(x, 32, 1))


def _params(sem, vmem=None):
    return pltpu.CompilerParams(dimension_semantics=sem, vmem_limit_bytes=vmem)


def _row_tile(rows, cap, mult):
    return max(d for d in range(mult, cap + 1, mult) if rows % d == 0)


def _assemble_w_in(stack, tr=256):
    def body(s_ref, a_ref, f_ref):
        full = jnp.concatenate([s_ref[k].astype(F32) for k in range(N_CHIP)], axis=-1)
        a_ref[...] = jnp.concatenate([full[:, :3072], full[:, 3080:IN_COLS]], axis=-1).astype(BF)
        f_ref[...] = jnp.concatenate([full[:, 3072:3080], jnp.zeros((tr, LANE - FOX_H), F32)], axis=-1).astype(BF)

    return pl.pallas_call(
        body, name="assemble_w_in", grid=(D_MODEL // tr,),
        in_specs=[pl.BlockSpec((N_CHIP, tr, IN_SH), lambda i: (0, i, 0))],
        out_specs=[pl.BlockSpec((tr, C_END), lambda i: (i, 0)), pl.BlockSpec((tr, LANE), lambda i: (i, 0))],
        out_shape=[jax.ShapeDtypeStruct((D_MODEL, C_END), BF), jax.ShapeDtypeStruct((D_MODEL, LANE), BF)],
        compiler_params=_params(("parallel",), VMEM_BIG),
    )(stack)


def _pack_g_in(g_ret, g_gt, g_fox, g_a, g_ff, tr=256):
    def body(r_ref, t_ref, x_ref, a_ref, f_ref, o_ref):
        full = jnp.concatenate([r_ref[...], t_ref[...], x_ref[...], f_ref[...][:, :FOX_H], a_ref[...]], axis=-1)
        for k in range(N_CHIP):
            o_ref[k] = full[:, k * IN_SH:(k + 1) * IN_SH].astype(BF)

    def spec(w):
        return pl.BlockSpec((tr, w), lambda i: (i, 0))

    return pl.pallas_call(
        body, name="pack_g_in", grid=(D_MODEL // tr,),
        in_specs=[spec(1024), spec(512), spec(1536), spec(2048), spec(LANE)],
        out_specs=pl.BlockSpec((N_CHIP, tr, IN_SH), lambda i: (0, i, 0)),
        out_shape=jax.ShapeDtypeStruct((N_CHIP, D_MODEL, IN_SH), BF),
        compiler_params=_params(("parallel",), VMEM_BIG),
    )(g_ret, g_gt, g_fox, g_a, g_ff)


def _rms_cast(x, g, tm=512):
    T = x.shape[0]

    def body(x_ref, g_ref, o_ref):
        xv = x_ref[...]
        r = lax.rsqrt(jnp.mean(xv * xv, axis=-1, keepdims=True) + EPS)
        o_ref[...] = (xv * r * g_ref[...]).astype(BF)

    return pl.pallas_call(
        body, name="rms_cast", grid=(T // tm,),
        in_specs=[pl.BlockSpec((tm, D_MODEL), lambda i: (i, 0)), pl.BlockSpec((1, D_MODEL), lambda i: (0, 0))],
        out_specs=pl.BlockSpec((tm, D_MODEL), lambda i: (i, 0)),
        out_shape=jax.ShapeDtypeStruct((T, D_MODEL), BF),
        compiler_params=_params(("parallel",)),
    )(x, g)


def _hosted_call(body, name, grid, in_specs, out_specs, out_shape, scratch_shapes, vmem, args, push):
    sem = ("arbitrary",) * len(grid)
    if push is None:
        res = pl.pallas_call(body, name=name, grid=grid, in_specs=in_specs, out_specs=out_specs, out_shape=out_shape,
                             scratch_shapes=scratch_shapes, compiler_params=_params(sem, vmem))(*args)
        return list(res), []
    srcs, lands = push
    ns, nl, n_in, n_out = (0 if srcs is None else len(srcs)), len(lands), len(in_specs), len(out_specs)
    n_scr = len(scratch_shapes)

    def wrapped(*refs):
        pos = n_in + ns + nl
        ins, x_in = refs[:n_in], refs[n_in:pos]
        outs, x_out = refs[pos:pos + n_out], refs[pos + n_out:pos + n_out + nl]
        scr = refs[pos + n_out + nl:pos + n_out + nl + n_scr]
        ssem, rsem = refs[-2], refs[-1]
        src = None if srcs is None else x_in[:ns]
        ids = [pl.program_id(a) for a in range(len(grid))]
        first = functools.reduce(lambda p, q: p & q, [ids[a] == 0 for a in range(len(grid))])
        last = functools.reduce(lambda p, q: p & q, [ids[a] == grid[a] - 1 for a in range(len(grid))])

        @pl.when(first)
        def _():
            for cp in _push_copies(src, x_out, ssem, rsem, False):
                cp.start()

        body(*ins, *outs, *scr)

        @pl.when(last)
        def _():
            for cp in _push_copies(src, x_out, ssem, rsem, True):
                cp.wait_recv()
                cp.wait_send()

    anyspec = pl.BlockSpec(memory_space=pl.ANY)
    extra = ([] if srcs is None else list(srcs)) + list(lands)
    res = pl.pallas_call(
        wrapped, name=name, grid=grid,
        in_specs=list(in_specs) + [anyspec] * len(extra), out_specs=list(out_specs) + [anyspec] * nl,
        out_shape=list(out_shape) + [jax.ShapeDtypeStruct(a.shape, a.dtype) for a in lands],
        input_output_aliases={n_in + ns + i: n_out + i for i in range(nl)},
        scratch_shapes=list(scratch_shapes) + [pltpu.SemaphoreType.DMA((3 * nl,)), pltpu.SemaphoreType.DMA((3 * nl,))],
        compiler_params=_params(sem, vmem),
    )(*args, *extra)
    return list(res[:n_out]), list(res[n_out:])


def _mm_nn(a, b, name, tm=512, tn=1024, push=None):
    M, K = a.shape
    N = b.shape[1]
    tn = min(tn, N)

    def body(a_ref, b_ref, o_ref):
        o_ref[...] = _nn(a_ref[...], b_ref[...])

    (out,), lands = _hosted_call(
        body, name, (N // tn, M // tm),
        [pl.BlockSpec((tm, K), lambda j, i: (i, 0)), pl.BlockSpec((K, tn), lambda j, i: (0, j))],
        [pl.BlockSpec((tm, tn), lambda j, i: (i, j))], [jax.ShapeDtypeStruct((M, N), F32)], [], None, (a, b), push)
    return out, lands


def _mm_tn(a, b, name, grid, a_spec, b_spec, o_spec, out_shape, acc_shape):
    nk = grid[-1]

    def body(a_ref, b_ref, o_ref, acc):
        k = pl.program_id(len(grid) - 1)

        @pl.when(k == 0)
        def _():
            acc[...] = jnp.zeros(acc.shape, F32)

        acc[...] += _tn(a_ref[...].astype(BF), b_ref[...].astype(BF))

        @pl.when(k == nk - 1)
        def _():
            o_ref[...] = acc[...].astype(o_ref.dtype)

    return pl.pallas_call(
        body, name=name, grid=grid, in_specs=[a_spec, b_spec], out_specs=o_spec, out_shape=out_shape,
        scratch_shapes=[pltpu.VMEM(acc_shape, F32)],
        compiler_params=_params(("parallel",) * (len(grid) - 1) + ("arbitrary",), VMEM_BIG),
    )(a, b)


def _grad_plain(a, b, name, out_dtype, tk=GRAD_TK, tn=1024):
    T, M = a.shape
    N = b.shape[1]
    tn = min(tn, N)
    return _mm_tn(a, b, name, (N // tn, T // tk),
                  pl.BlockSpec((tk, M), lambda j, k: (k, 0)), pl.BlockSpec((tk, tn), lambda j, k: (k, j)),
                  pl.BlockSpec((M, tn), lambda j, k: (0, j)), jax.ShapeDtypeStruct((M, N), out_dtype), (M, tn))


def _grad_colstack(a, b, name, wcol, tk=GRAD_TK):
    T, M = a.shape
    N = b.shape[1]
    S = N // wcol
    nk = T // tk

    def body(a_ref, b_ref, o_ref, acc):
        k = pl.program_id(0)

        @pl.when(k == 0)
        def _():
            acc[...] = jnp.zeros(acc.shape, F32)

        acc[...] += _tn(a_ref[...], b_ref[...])

        @pl.when(k == nk - 1)
        def _():
            for s in range(S):
                o_ref[s] = acc[:, s * wcol:(s + 1) * wcol].astype(BF)

    return pl.pallas_call(
        body, name=name, grid=(nk,),
        in_specs=[pl.BlockSpec((tk, M), lambda k: (k, 0)), pl.BlockSpec((tk, N), lambda k: (k, 0))],
        out_specs=pl.BlockSpec((S, M, wcol), lambda k: (0, 0, 0)), out_shape=jax.ShapeDtypeStruct((S, M, wcol), BF),
        scratch_shapes=[pltpu.VMEM((M, N), F32)], compiler_params=_params(("arbitrary",), VMEM_BIG),
    )(a, b)


def _grad_astack(a, b, name, tk=1024):
    S, T, m = a.shape
    N = b.shape[1]
    nk = T // tk

    def body(a_ref, b_ref, o_ref, acc):
        k = pl.program_id(0)

        @pl.when(k == 0)
        def _():
            acc[...] = jnp.zeros(acc.shape, F32)

        bb = b_ref[...].astype(BF)
        for s in range(S):
            acc[s] += _tn(a_ref[s], bb)

        @pl.when(k == nk - 1)
        def _():
            o_ref[...] = acc[...].astype(BF)

    return pl.pallas_call(
        body, name=name, grid=(nk,),
        in_specs=[pl.BlockSpec((S, tk, m), lambda k: (0, k, 0)), pl.BlockSpec((tk, N), lambda k: (k, 0))],
        out_specs=pl.BlockSpec((S, m, N), lambda k: (0, 0, 0)), out_shape=jax.ShapeDtypeStruct((S, m, N), BF),
        scratch_shapes=[pltpu.VMEM((S, m, N), F32)], compiler_params=_params(("arbitrary",), VMEM_BIG),
    )(a, b)


def _rope_tables(T):
    half = 32
    pos = np.arange(T, dtype=np.float32)
    inv_freq = (np.float32(1.0) / (np.float32(10000.0) ** (np.arange(half, dtype=np.float32) / np.float32(half)))).astype(np.float32)
    ang = (pos[:, None] * inv_freq[None, :]).astype(np.float32)
    cos, sin = np.cos(ang).astype(np.float32), np.sin(ang).astype(np.float32)
    z = np.zeros((T, 64), np.float32)
    return (jnp.asarray(np.concatenate([cos, cos, z], axis=-1)), jnp.asarray(np.concatenate([-sin, sin, z], axis=-1)))


def _ret_consts():
    h = np.arange(RET_H, dtype=np.float32)
    log_g = np.log1p(-(np.float32(2.0) ** (-5.0 - h))).astype(np.float32)
    idx = np.arange(CHUNK, dtype=np.float32)
    diff = idx[:, None] - idx[None, :]
    decay = np.where(diff[None] >= 0, np.exp(np.maximum(diff, 0.0)[None] * log_g[:, None, None]), 0.0)
    zeta = np.exp((CHUNK - 1.0 - idx)[None, :] * log_g[:, None])
    xi = np.exp((idx + 1.0)[None, :] * log_g[:, None])
    gc = np.exp(CHUNK * log_g)
    bc = lambda v: np.broadcast_to(v[:, :, None], (RET_H, CHUNK, LANE)).astype(np.float32)
    gcb = np.broadcast_to(gc[:, None, None], (RET_H, CHUNK, LANE)).astype(np.float32)
    return (jnp.asarray(decay.astype(np.float32)), jnp.asarray(bc(zeta)), jnp.asarray(bc(xi)), jnp.asarray(gcb))


def _mix_prep(z_a, z_ff, cos_t, sin_t, b_f, g_q, g_k, tm=256, push=None):
    T = z_a.shape[0]

    def body(zqk_ref, zf_ref, zff_ref, cos_ref, sin_ref, b_ref, g_ref, seg_ref, segt_ref,
             qr_ref, kr_ref, qf_ref, kf_ref, vf_ref, c_ref, nmax_ref, carry):
        i = pl.program_id(0)

        @pl.when(i == 0)
        def _():
            carry[...] = jnp.zeros(carry.shape, F32)
            nmax_ref[...] = jnp.zeros(nmax_ref.shape, F32)

        lane = lax.broadcasted_iota(jnp.int32, (tm, LANE), 1)
        zpad = jnp.zeros((tm, 64), F32)
        cosv, sinv = cos_ref[...], sin_ref[...]
        zqk = zqk_ref[...]
        for h in range(RET_H):
            for src, dst, scale in ((0, qr_ref, 1.0), (256, kr_ref, 0.125)):
                xh = jnp.concatenate([zqk[:, src + 64 * h: src + 64 * h + 64], zpad], axis=-1)
                rot = xh * cosv + _swap32(xh) * sinv
                dst[h] = (rot * scale).astype(BF)

        lf_in = zff_ref[...] + b_ref[...]
        logf = jnp.minimum(lf_in, 0.0) - jnp.log(1.0 + jnp.exp(-jnp.abs(lf_in)))
        row = lax.broadcasted_iota(jnp.int32, (tm, tm), 0)
        col = lax.broadcasted_iota(jnp.int32, (tm, tm), 1)
        tri = (row >= col).astype(BF)
        hi, mid, lo = _split3(logf)
        cs = _nn(tri, hi) + _nn(tri, mid) + _nn(tri, lo) + carry[...]
        carry[...] = cs[tm - 1:tm, :]
        c_ref[...] = cs

        def seg_sum(v):
            return sum(_nn(t, seg_ref[...]) for t in _split3(v))

        zf = zf_ref[...]
        xqk = zf[:, :1024]
        rinv = lax.rsqrt(seg_sum(xqk * xqk) * (1.0 / FOX_D) + EPS)
        xn = xqk * sum(_nn(t, segt_ref[...]) for t in _split3(rinv)) * g_ref[...]
        nmax_ref[...] = jnp.maximum(nmax_ref[...], jnp.max(seg_sum(xn * xn), axis=0, keepdims=True))

        one = jnp.ones((tm, LANE), F32)
        for h in range(FOX_H):
            c = cs[:, h:h + 1]
            chi, cmid, clo = [t.astype(F32) for t in _split3(c)]
            qn = xn[:, 64 * h:64 * h + 64]
            kn = xn[:, 512 + 64 * h:512 + 64 * h + 64]
            vh = zf[:, 1024 + 64 * h:1024 + 64 * h + 64]
            qa = jnp.concatenate([qn, zpad], axis=-1)
            qa = jnp.where(lane == L_CQ, chi, jnp.where(lane == L_CQ + 1, cmid, jnp.where(lane == L_CQ + 2, clo, qa)))
            qa = jnp.where((lane >= L_CK) & (lane < L_CK + 3), one, qa)
            ka = jnp.concatenate([kn, zpad], axis=-1)
            ka = jnp.where(lane == L_CK, -chi, jnp.where(lane == L_CK + 1, -cmid, jnp.where(lane == L_CK + 2, -clo, ka)))
            ka = jnp.where(((lane >= L_CQ) & (lane < L_CQ + 3)) | ((lane >= L_LSE) & (lane < L_MAX + 3)), one, ka)
            va = jnp.concatenate([vh, zpad], axis=-1)
            va = jnp.where((lane >= 64) & (lane < 67), one, va)
            qf_ref[h] = qa.astype(BF)
            kf_ref[h] = ka.astype(BF)
            vf_ref[h] = va.astype(BF)

    hspec4 = pl.BlockSpec((RET_H, tm, LANE), lambda i: (0, i, 0))
    hspec8 = pl.BlockSpec((FOX_H, tm, LANE), lambda i: (0, i, 0))
    const = lambda r, w: pl.BlockSpec((r, w), lambda i: (0, 0))
    seg = _segment_matrix()
    g_all = jnp.concatenate([jnp.tile(g_q * 0.125, (1, FOX_H)), jnp.tile(g_k, (1, FOX_H))], axis=1)
    return _hosted_call(
        body, "mix_prep", (T // tm,),
        [pl.BlockSpec((tm, 512), lambda i: (i, 0)), pl.BlockSpec((tm, 1536), lambda i: (i, 1)),
         pl.BlockSpec((tm, LANE), lambda i: (i, 0)), pl.BlockSpec((tm, LANE), lambda i: (i, 0)),
         pl.BlockSpec((tm, LANE), lambda i: (i, 0)), const(1, LANE), const(1, 1024), const(1024, LANE), const(LANE, 1024)],
        [hspec4, hspec4, hspec8, hspec8, hspec8, pl.BlockSpec((tm, LANE), lambda i: (i, 0)), const(1, LANE)],
        [jax.ShapeDtypeStruct((RET_H, T, LANE), BF)] * 2 + [jax.ShapeDtypeStruct((FOX_H, T, LANE), BF)] * 3
        + [jax.ShapeDtypeStruct((T, LANE), F32), jax.ShapeDtypeStruct((1, LANE), F32)],
        [pltpu.VMEM((1, LANE), F32)], VMEM_BIG, (z_a, z_a, z_ff, cos_t, sin_t, b_f, g_all, seg, seg.T), push)


def _segment_matrix():
    m = np.zeros((2 * FOX_H * FOX_D, LANE), np.float32)
    m[np.arange(2 * FOX_H * FOX_D), np.arange(2 * FOX_H * FOX_D) // FOX_D] = 1.0
    return jnp.asarray(m, dtype=BF)


def _ret_fwd(qr, kr, z_a, g_ret, consts, tt=512):
    T = z_a.shape[0]
    nch = tt // CHUNK
    decay, zeta, xi, gcb = consts

    def body(q_ref, k_ref, v_ref, gt_ref, g_ref, d_ref, ze_ref, xi_ref, gc_ref, o_ref, u_ref, st_ref, r_sc):
        i = pl.program_id(0)

        @pl.when(i == 0)
        def _():
            r_sc[...] = jnp.zeros(r_sc.shape, F32)

        for c in range(nch):
            rows = slice(c * CHUNK, (c + 1) * CHUNK)
            for h in range(RET_H):
                cols = slice(h * RET_DV, (h + 1) * RET_DV)
                q, k = q_ref[h, rows, :], k_ref[h, rows, :]
                v32 = v_ref[rows, cols]
                r = r_sc[h]
                st_ref[h, rows, :] = r
                s = _nt(q, k) * d_ref[h]
                o = _nn(s.astype(BF), v32.astype(BF)) + _nn(q, r.astype(BF)) * xi_ref[h]
                r_sc[h] = gc_ref[h] * r + _tn(k, (v32 * ze_ref[h]).astype(BF))
                o_ref[rows, cols] = o
                mu = jnp.mean(o, axis=-1, keepdims=True)
                xc = o - mu
                on = xc * lax.rsqrt(jnp.mean(xc * xc, axis=-1, keepdims=True) + EPS)
                gt = gt_ref[rows, cols]
                u_ref[rows, cols] = (gt * _sigmoid(gt) * (on * g_ref[:, cols])).astype(BF)

    hspec = pl.BlockSpec((RET_H, tt, LANE), lambda i: (0, i, 0))
    cspec = pl.BlockSpec((RET_H, CHUNK, LANE), lambda i: (0, 0, 0))
    return pl.pallas_call(
        body, name="ret_fwd", grid=(T // tt,),
        in_specs=[hspec, hspec, pl.BlockSpec((tt, 512), lambda i: (i, 1)), pl.BlockSpec((tt, 512), lambda i: (i, 2)),
                  pl.BlockSpec((1, 512), lambda i: (0, 0)), cspec, cspec, cspec, cspec],
        out_specs=[pl.BlockSpec((tt, 512), lambda i: (i, 0)), pl.BlockSpec((tt, 512), lambda i: (i, 0)), hspec],
        out_shape=[jax.ShapeDtypeStruct((T, 512), F32), jax.ShapeDtypeStruct((T, 512), BF),
                   jax.ShapeDtypeStruct((RET_H, T, LANE), F32)],
        scratch_shapes=[pltpu.VMEM((RET_H, CHUNK, LANE), F32)],
        compiler_params=_params(("arbitrary",), VMEM_BIG),
    )(qr, kr, z_a, z_a, g_ret, decay, zeta, xi, gcb)


PRUNE_LOG = -110.0


def _prune_tables(c, nmax, sub):
    n = c.shape[0] // sub
    u = jnp.sqrt(nmax[0, :FOX_H] * nmax[0, FOX_H:2 * FOX_H]) * 1.02 + 0.5
    first = c[0::sub, :FOX_H].T
    last = c[sub - 1::sub, :FOX_H].T
    blk = jnp.arange(n, dtype=jnp.int32)
    needed = (2.0 * u[:, None, None] + first[:, :, None] - last[:, None, :] >= PRUNE_LOG) | (blk[None, :] >= blk[:, None])[None]
    jlo = jnp.argmax(needed, axis=2).astype(jnp.int32)
    jstart = jnp.minimum(jlo[:, 0::2], jlo[:, 1::2]) // 2
    need_q = jlo[:, None, :] <= (2 * jnp.arange(n // 2, dtype=jnp.int32) + 1)[None, :, None]
    iend = n - jnp.argmax(need_q[:, :, ::-1], axis=2).astype(jnp.int32)
    return jstart.astype(jnp.int32), iend.astype(jnp.int32)


def _fox_fwd(jstart, q, k, v, sub=512):
    H, T, _ = q.shape
    tb = 2 * sub

    def body(js_ref, q_ref, k_ref, v_ref, o_ref, q2_ref, mx_sc, acc_sc):
        i = pl.program_id(1)
        j0 = js_ref[pl.program_id(0), i]
        lane = lax.broadcasted_iota(jnp.int32, (sub, LANE), 1)
        row = lax.broadcasted_iota(jnp.int32, (sub, sub), 0)
        col = lax.broadcasted_iota(jnp.int32, (sub, sub), 1)
        causal = row >= col
        qs = [q_ref[0:sub, :], q_ref[sub:tb, :]]
        d0 = pl.multiple_of(i * tb, tb)
        d1 = pl.multiple_of(i * tb + sub, sub)

        def lane_max(s):
            m = s[:, 0:LANE]
            for c in range(1, s.shape[1] // LANE):
                m = jnp.maximum(m, s[:, c * LANE:(c + 1) * LANE])
            return m

        mx_sc[...] = jnp.full(mx_sc.shape, NEG, F32)

        def max_body(j, carry):
            kb = k_ref[pl.ds(pl.multiple_of(j * tb, tb), tb), :]
            for a in range(2):
                mx_sc[a] = jnp.maximum(mx_sc[a], lane_max(_nt(qs[a], kb)))
            return carry

        lax.fori_loop(j0, i, max_body, 0)
        k0, k1 = k_ref[pl.ds(d0, sub), :], k_ref[pl.ds(d1, sub), :]
        v0, v1 = v_ref[pl.ds(d0, sub), :], v_ref[pl.ds(d1, sub), :]
        mx = [jnp.maximum(mx_sc[0], lane_max(jnp.where(causal, _nt(qs[0], k0), NEG))),
              jnp.maximum(jnp.maximum(mx_sc[1], lane_max(_nt(qs[1], k0))),
                          lane_max(jnp.where(causal, _nt(qs[1], k1), NEG)))]
        ms = [jnp.max(t, axis=1, keepdims=True) for t in mx]

        def put3(base, first, val):
            hi, mid, lo = _split3(val)
            return jnp.where(lane == first, hi, jnp.where(lane == first + 1, mid, jnp.where(lane == first + 2, lo, base)))

        qm = [put3(qs[a], L_MAX, -ms[a]) for a in range(2)]

        acc_sc[...] = jnp.zeros(acc_sc.shape, F32)

        def acc_body(j, carry):
            off = pl.multiple_of(j * tb, tb)
            kb, vb = k_ref[pl.ds(off, tb), :], v_ref[pl.ds(off, tb), :]
            for a in range(2):
                acc_sc[a] += _nn(jnp.exp(_nt(qm[a], kb)).astype(BF), vb)
            return carry

        lax.fori_loop(j0, i, acc_body, 0)

        def pv(qa, kk, vv, masked):
            p = jnp.exp(_nt(qa, kk))
            if masked:
                p = jnp.where(causal, p, 0.0)
            return _nn(p.astype(BF), vv)

        accs = [acc_sc[0] + pv(qm[0], k0, v0, True),
                acc_sc[1] + pv(qm[1], k0, v0, False) + pv(qm[1], k1, v1, True)]
        for a in range(2):
            rows = slice(a * sub, (a + 1) * sub)
            l = accs[a][:, 64:65]
            o_ref[rows, :] = jnp.where(lane < 64, accs[a] / l, 0.0)
            q2_ref[rows, :] = put3(qs[a], L_LSE, -(ms[a] + jnp.log(l)))

    blk = pl.BlockSpec((None, tb, LANE), lambda h, i, js: (h, i, 0))
    full = pl.BlockSpec((None, T, LANE), lambda h, i, js: (h, 0, 0))
    return pl.pallas_call(
        body, name="fox_fwd",
        grid_spec=pltpu.PrefetchScalarGridSpec(
            num_scalar_prefetch=1, grid=(H, T // tb), in_specs=[blk, full, full], out_specs=[blk, blk],
            scratch_shapes=[pltpu.VMEM((2, sub, LANE), F32), pltpu.VMEM((2, sub, LANE), F32)]),
        out_shape=[jax.ShapeDtypeStruct((H, T, LANE), F32), jax.ShapeDtypeStruct((H, T, LANE), BF)],
        compiler_params=_params(("parallel", "arbitrary"), VMEM_BIG),
    )(jstart, q, k, v)


def _merge_out(u_r, o_fox, z_a, x, g_ffn, w_ro, w_fo, w_out, tm=256):
    T = x.shape[0]

    def body(u_ref, of_ref, ar_ref, af_ref, x_ref, g_ref, wro_ref, wfo_ref, wout_ref,
             yr_ref, yf_ref, m_ref, x2_ref, h2_ref, oc_ref):
        u = u_ref[...]
        oc = jnp.concatenate([of_ref[h][:, :FOX_D] for h in range(FOX_H)], axis=-1).astype(BF)
        oc_ref[...] = oc
        yr = jnp.concatenate([_nn(u, wro_ref[k]) for k in range(N_CHIP)], axis=-1)
        yf = jnp.concatenate([_nn(oc, wfo_ref[k]) for k in range(N_CHIP)], axis=-1)
        yr_ref[...] = yr
        yf_ref[...] = yf
        m = (_sigmoid(ar_ref[...]) * yr + _sigmoid(af_ref[...]) * yf).astype(BF)
        m_ref[...] = m
        x2 = x_ref[...]
        for k in range(N_CHIP):
            x2 = x2 + _nn(m[:, 256 * k:256 * k + 256], wout_ref[k])
        x2_ref[...] = x2
        r = lax.rsqrt(jnp.mean(x2 * x2, axis=-1, keepdims=True) + EPS)
        h2_ref[...] = (x2 * r * g_ref[...]).astype(BF)

    row = lambda w: pl.BlockSpec((tm, w), lambda i: (i, 0))
    const = lambda shp: pl.BlockSpec(shp, lambda i: (0,) * len(shp))
    return pl.pallas_call(
        body, name="merge_out", grid=(T // tm,),
        in_specs=[row(512), pl.BlockSpec((FOX_H, tm, LANE), lambda i: (0, i, 0)),
                  pl.BlockSpec((tm, 1024), lambda i: (i, 3)), pl.BlockSpec((tm, 1024), lambda i: (i, 4)),
                  row(1024), const((1, 1024)), const((N_CHIP, 512, 256)), const((N_CHIP, 512, 256)),
                  const((N_CHIP, 256, 1024))],
        out_specs=[row(1024), row(1024), row(1024), row(1024), row(1024), row(512)],
        out_shape=[jax.ShapeDtypeStruct((T, 1024), F32), jax.ShapeDtypeStruct((T, 1024), F32),
                   jax.ShapeDtypeStruct((T, 1024), BF), jax.ShapeDtypeStruct((T, 1024), F32),
                   jax.ShapeDtypeStruct((T, 1024), BF), jax.ShapeDtypeStruct((T, 512), BF)],
        compiler_params=_params(("parallel",), VMEM_BIG),
    )(u_r, o_fox, z_a, z_a, x, g_ffn, w_ro, w_fo, w_out)


def _load_resident(hbm_refs, vmem_refs, sem):
    cps = [pltpu.make_async_copy(h, v, sem.at[i]) for i, (h, v) in enumerate(zip(hbm_refs, vmem_refs))]
    for cp in cps:
        cp.start()
    for cp in cps:
        cp.wait()


def _ffn_fwd(h2, x2, tgt, w_gate, w_up, w_down, tm=FFN_TM):
    T = h2.shape[0]

    def body(h_ref, x2_ref, t_ref, wg_hbm, wu_hbm, wd_hbm, a_ref, b_ref, act_ref, dy_ref, ls_ref, wg, wu, wd, sem):
        @pl.when(pl.program_id(0) == 0)
        def _():
            _load_resident((wg_hbm, wu_hbm, wd_hbm), (wg, wu, wd), sem)
            ls_ref[...] = jnp.zeros(ls_ref.shape, F32)

        h = h_ref[...]
        err = x2_ref[...] - t_ref[...]
        for k in range(N_CHIP):
            gp = _nt(h, wg[k])
            up = _nt(h, wu[k])
            sg = _sigmoid(gp)
            silu = gp * sg
            a_ref[k] = silu.astype(BF)
            b_ref[k] = (up * sg * (1.0 + gp * (1.0 - sg))).astype(BF)
            act = (silu * up).astype(BF)
            act_ref[k] = act
            err = err + _nn(act, wd[k])
        dy_ref[...] = err * (1.0 / D_MODEL)
        ls_ref[...] += jnp.sum(err * err, axis=0, keepdims=True)

    row = pl.BlockSpec((tm, D_MODEL), lambda i: (i, 0))
    hid = pl.BlockSpec((N_CHIP, tm, FF_SH), lambda i: (0, i, 0))
    anyspec = pl.BlockSpec(memory_space=pl.ANY)
    wshape = pltpu.VMEM((N_CHIP, FF_SH, D_MODEL), BF)
    return pl.pallas_call(
        body, name="ffn_fwd", grid=(T // tm,),
        in_specs=[row, row, row, anyspec, anyspec, anyspec],
        out_specs=[hid, hid, hid, row, pl.BlockSpec((1, D_MODEL), lambda i: (0, 0))],
        out_shape=[jax.ShapeDtypeStruct((N_CHIP, T, FF_SH), BF)] * 3
        + [jax.ShapeDtypeStruct((T, D_MODEL), F32), jax.ShapeDtypeStruct((1, D_MODEL), F32)],
        scratch_shapes=[wshape, wshape, wshape, pltpu.SemaphoreType.DMA((3,))],
        compiler_params=_params(("arbitrary",), VMEM_HUGE),
    )(h2, x2, tgt, w_gate, w_up, w_down)


def _ffn_bwd(dy, sa, sb, x2, g_ffn, w_gate, w_up, w_down, tm=FFN_TM):
    T = dy.shape[0]

    def body(dy_ref, a_ref, b_ref, x2_ref, g_ref, wg_hbm, wu_hbm, wd_hbm, dgp_ref, dup_ref, dx_ref, dg_ref,
             wg, wu, wd, sem):
        @pl.when(pl.program_id(0) == 0)
        def _():
            _load_resident((wg_hbm, wu_hbm, wd_hbm), (wg, wu, wd), sem)
            dg_ref[...] = jnp.zeros(dg_ref.shape, F32)

        dy = dy_ref[...]
        dyb = dy.astype(BF)
        dh = jnp.zeros((tm, D_MODEL), F32)
        for k in range(N_CHIP):
            dact = _nt(dyb, wd[k])
            dup = (dact * a_ref[k]).astype(BF)
            dgp = (dact * b_ref[k]).astype(BF)
            dgp_ref[k] = dgp
            dup_ref[k] = dup
            dh = dh + _nn(dgp, wg[k]) + _nn(dup, wu[k])
        x2 = x2_ref[...]
        r = lax.rsqrt(jnp.mean(x2 * x2, axis=-1, keepdims=True) + EPS)
        xn = x2 * r
        dg_ref[...] += jnp.sum(dh * xn, axis=0, keepdims=True)
        dxn = dh * g_ref[...]
        dx_ref[...] = dy + r * (dxn - xn * jnp.mean(dxn * xn, axis=-1, keepdims=True))

    row = pl.BlockSpec((tm, D_MODEL), lambda i: (i, 0))
    hid = pl.BlockSpec((N_CHIP, tm, FF_SH), lambda i: (0, i, 0))
    vec = pl.BlockSpec((1, D_MODEL), lambda i: (0, 0))
    anyspec = pl.BlockSpec(memory_space=pl.ANY)
    wshape = pltpu.VMEM((N_CHIP, FF_SH, D_MODEL), BF)
    return pl.pallas_call(
        body, name="ffn_bwd", grid=(T // tm,),
        in_specs=[row, hid, hid, row, vec, anyspec, anyspec, anyspec],
        out_specs=[hid, hid, row, vec],
        out_shape=[jax.ShapeDtypeStruct((N_CHIP, T, FF_SH), BF), jax.ShapeDtypeStruct((N_CHIP, T, FF_SH), BF),
                   jax.ShapeDtypeStruct((T, D_MODEL), F32), jax.ShapeDtypeStruct((1, D_MODEL), F32)],
        scratch_shapes=[wshape, wshape, wshape, pltpu.SemaphoreType.DMA((3,))],
        compiler_params=_params(("arbitrary",), VMEM_HUGE),
    )(dy, sa, sb, x2, g_ffn, w_gate, w_up, w_down)


def _out_bwd(dx2, z_a, y_r, y_f, o_raw, o_fox, g_ret, w_ro, w_fo, w_out, tm=256, push=None):
    T = dx2.shape[0]

    def body(dx_ref, gt_ref, ar_ref, af_ref, yr_ref, yf_ref, o_ref, of_ref, g_ref, wro_ref, wfo_ref, wout_ref,
             dyr_ref, dyf_ref, dgt_ref, da_ref, do_ref, dof_ref, dg_ref):
        i = pl.program_id(0)

        @pl.when(i == 0)
        def _():
            dg_ref[...] = jnp.zeros(dg_ref.shape, F32)

        dxb = dx_ref[...].astype(BF)
        dm = jnp.concatenate([_nt(dxb, wout_ref[k]) for k in range(N_CHIP)], axis=-1)
        sr, sf = _sigmoid(ar_ref[...]), _sigmoid(af_ref[...])
        dyr = dm * sr
        dyf = dm * sf
        da_ref[:, :1024] = (dyr * yr_ref[...] * (1.0 - sr)).astype(BF)
        da_ref[:, 1024:] = (dyf * yf_ref[...] * (1.0 - sf)).astype(BF)
        dyr = dyr.astype(BF)
        dyf = dyf.astype(BF)
        dyr_ref[...] = dyr
        dyf_ref[...] = dyf
        du = jnp.zeros((tm, 512), F32)
        doc = jnp.zeros((tm, 512), F32)
        for k in range(N_CHIP):
            du = du + _nt(dyr[:, 256 * k:256 * k + 256], wro_ref[k])
            doc = doc + _nt(dyf[:, 256 * k:256 * k + 256], wfo_ref[k])

        for h in range(RET_H):
            cols = slice(h * RET_DV, (h + 1) * RET_DV)
            o = o_ref[:, cols]
            mu = jnp.mean(o, axis=-1, keepdims=True)
            xc = o - mu
            rstd = lax.rsqrt(jnp.mean(xc * xc, axis=-1, keepdims=True) + EPS)
            on = xc * rstd
            g = g_ref[:, cols]
            gt = gt_ref[:, cols]
            sg = _sigmoid(gt)
            duh = du[:, cols]
            dgt_ref[:, cols] = (duh * (on * g) * sg * (1.0 + gt * (1.0 - sg))).astype(BF)
            dog = duh * gt * sg
            dg_ref[:, cols] += jnp.sum(dog * on, axis=0, keepdims=True)
            don = dog * g
            do_ref[:, cols] = rstd * (don - jnp.mean(don, axis=-1, keepdims=True)
                                      - on * jnp.mean(don * on, axis=-1, keepdims=True))

        lane = lax.broadcasted_iota(jnp.int32, (tm, LANE), 1)
        zpad = jnp.zeros((tm, 64), F32)
        for h in range(FOX_H):
            doh = doc[:, 64 * h:64 * h + 64]
            delta = jnp.sum(doh * of_ref[h][:, :FOX_D], axis=-1, keepdims=True)
            hi, mid, lo = [t.astype(F32) for t in _split3(-delta)]
            da = jnp.concatenate([doh, zpad], axis=-1)
            da = jnp.where(lane == 64, hi, jnp.where(lane == 65, mid, jnp.where(lane == 66, lo, da)))
            dof_ref[h] = da.astype(BF)

    row = lambda w: pl.BlockSpec((tm, w), lambda i: (i, 0))
    const = lambda shp: pl.BlockSpec(shp, lambda i: (0,) * len(shp))
    hsp = pl.BlockSpec((FOX_H, tm, LANE), lambda i: (0, i, 0))
    return _hosted_call(
        body, "out_bwd", (T // tm,),
        [row(1024), pl.BlockSpec((tm, 512), lambda i: (i, 2)), pl.BlockSpec((tm, 1024), lambda i: (i, 3)),
         pl.BlockSpec((tm, 1024), lambda i: (i, 4)), row(1024), row(1024), row(512), hsp,
         const((1, 512)), const((N_CHIP, 512, 256)), const((N_CHIP, 512, 256)), const((N_CHIP, 256, 1024))],
        [row(1024), row(1024), row(512), row(2048), row(512), hsp, const((1, 512))],
        [jax.ShapeDtypeStruct((T, 1024), BF), jax.ShapeDtypeStruct((T, 1024), BF),
         jax.ShapeDtypeStruct((T, 512), BF), jax.ShapeDtypeStruct((T, 2048), BF),
         jax.ShapeDtypeStruct((T, 512), F32), jax.ShapeDtypeStruct((FOX_H, T, LANE), BF),
         jax.ShapeDtypeStruct((1, 512), F32)],
        [], VMEM_BIG, (dx2, z_a, z_a, z_a, y_r, y_f, o_raw, o_fox, g_ret, w_ro, w_fo, w_out), push)


def _ret_bwd(d_o, qr, kr, z_a, states, cos_t, sin_t, consts, tt=512, push=None):
    T = z_a.shape[0]
    nt = T // tt
    nch = tt // CHUNK
    decay, zeta, xi, gcb = consts

    def body(do_ref, q_ref, k_ref, v_ref, st_ref, cos_ref, sin_ref, d_ref, ze_ref, xi_ref, gc_ref, dz_ref, g_sc):
        i = pl.program_id(0)

        @pl.when(i == 0)
        def _():
            g_sc[...] = jnp.zeros(g_sc.shape, F32)

        for c in reversed(range(nch)):
            rows = slice(c * CHUNK, (c + 1) * CHUNK)
            cosv, sinv = cos_ref[rows, :], sin_ref[rows, :]
            dq_parts, dk_parts = [], []
            for h in range(RET_H):
                cols = slice(h * RET_DV, (h + 1) * RET_DV)
                q, k = q_ref[h, rows, :], k_ref[h, rows, :]
                v32 = v_ref[rows, cols]
                vb = v32.astype(BF)
                r = st_ref[h, rows, :]
                g = g_sc[h]
                gb = g.astype(BF)
                d_o = do_ref[rows, cols]
                dob = d_o.astype(BF)
                dox = (d_o * xi_ref[h]).astype(BF)
                dec = d_ref[h]
                s = (_nt(q, k) * dec).astype(BF)
                ds = (_nt(dob, vb) * dec).astype(BF)
                dv = _tn(s, dob) + ze_ref[h] * _nn(k, gb)
                dq = _nn(ds, k) + _nt(dox, r.astype(BF))
                dk = _tn(ds, q) + _nt((v32 * ze_ref[h]).astype(BF), gb)
                g_sc[h] = gc_ref[h] * g + _tn(q, dox)
                dq_parts.append((dq * cosv - _swap32(dq) * sinv)[:, :64])
                dk_parts.append(((dk * cosv - _swap32(dk) * sinv) * 0.125)[:, :64])
                dz_ref[rows, 512 + h * RET_DV:512 + (h + 1) * RET_DV] = dv.astype(BF)
            dz_ref[rows, 0:256] = jnp.concatenate(dq_parts, axis=-1).astype(BF)
            dz_ref[rows, 256:512] = jnp.concatenate(dk_parts, axis=-1).astype(BF)

    rev = lambda i: nt - 1 - i
    hspec = pl.BlockSpec((RET_H, tt, LANE), lambda i: (0, rev(i), 0))
    cspec = pl.BlockSpec((RET_H, CHUNK, LANE), lambda i: (0, 0, 0))
    tab = pl.BlockSpec((tt, LANE), lambda i: (rev(i), 0))
    (dz,), lands = _hosted_call(
        body, "ret_bwd", (nt,),
        [pl.BlockSpec((tt, 512), lambda i: (rev(i), 0)), hspec, hspec,
         pl.BlockSpec((tt, 512), lambda i: (rev(i), 1)), hspec, tab, tab, cspec, cspec, cspec, cspec],
        [pl.BlockSpec((tt, 1024), lambda i: (rev(i), 0))], [jax.ShapeDtypeStruct((T, 1024), BF)],
        [pltpu.VMEM((RET_H, CHUNK, LANE), F32)], VMEM_BIG,
        (d_o, qr, kr, z_a, states, cos_t, sin_t, decay, zeta, xi, gcb), push)
    return dz, lands


def _fox_bwd(iend, q2, k, v, do, sub=512):
    H, T, _ = k.shape
    tb = 2 * sub

    def body(ie_ref, q_ref, do_ref, k_ref, v_ref, dq_ref, dk_ref, dv_ref, dk_sc, dv_sc):
        j = pl.program_id(1)
        n = ie_ref[pl.program_id(0), j]

        @pl.when(j == 0)
        def _():
            dq_ref[...] = jnp.zeros(dq_ref.shape, F32)

        kk, vv = k_ref[...], v_ref[...]
        dk_sc[...] = jnp.zeros(dk_sc.shape, F32)
        dv_sc[...] = jnp.zeros(dv_sc.shape, F32)
        krow = lax.broadcasted_iota(jnp.int32, (tb, sub), 0)
        qcol = lax.broadcasted_iota(jnp.int32, (tb, sub), 1)

        def step(i, shift):
            off = pl.multiple_of(i * sub, sub)
            qq = q_ref[pl.ds(off, sub), :]
            dd = do_ref[pl.ds(off, sub), :]
            p = jnp.exp(_nt(kk, qq))
            if shift is not None:
                p = jnp.where(qcol + shift >= krow, p, 0.0)
            ds = (p * _nt(vv, dd)).astype(BF)
            dv_sc[...] += _nn(p.astype(BF), dd)
            dk_sc[...] += _nn(ds, qq)
            dq_ref[pl.ds(off, sub), :] += _tn(ds, kk)

        off0 = pl.multiple_of(2 * j * sub, sub)
        q0, d0 = q_ref[pl.ds(off0, sub), :], do_ref[pl.ds(off0, sub), :]
        k0, v0 = k_ref[0:sub, :], v_ref[0:sub, :]
        p0 = jnp.where(qcol[0:sub, :] >= krow[0:sub, :], jnp.exp(_nt(k0, q0)), 0.0)
        ds0 = (p0 * _nt(v0, d0)).astype(BF)
        dv_sc[0:sub, :] += _nn(p0.astype(BF), d0)
        dk_sc[0:sub, :] += _nn(ds0, q0)
        dq_ref[pl.ds(off0, sub), :] += _tn(ds0, k0)
        step(2 * j + 1, sub)

        def loop_body(i, carry):
            step(i, None)
            return carry

        lax.fori_loop(2 * j + 2, n, loop_body, 0)
        dk_ref[...] = dk_sc[...]
        dv_ref[...] = dv_sc[...]

    blk = pl.BlockSpec((None, tb, LANE), lambda h, j, ie: (h, j, 0))
    full = pl.BlockSpec((None, T, LANE), lambda h, j, ie: (h, 0, 0))
    shp = jax.ShapeDtypeStruct((H, T, LANE), F32)
    return pl.pallas_call(
        body, name="fox_bwd",
        grid_spec=pltpu.PrefetchScalarGridSpec(
            num_scalar_prefetch=1, grid=(H, T // tb), in_specs=[full, full, blk, blk], out_specs=[full, blk, blk],
            scratch_shapes=[pltpu.VMEM((tb, LANE), F32), pltpu.VMEM((tb, LANE), F32)]),
        out_shape=[shp, shp, shp],
        compiler_params=_params(("arbitrary", "arbitrary"), VMEM_BIG),
    )(iend, q2, do, k, v)


def _fox_post_bwd(dq, dk, dv, z_a, z_ff, b_f, g_q, g_k, tm=256, push=None):
    T = z_a.shape[0]
    nt = T // tm

    def body(dq_ref, dk_ref, dv_ref, zf_ref, zff_ref, b_ref, g_ref, sc_ref, seg_ref, segt_ref,
             dz_ref, dff_ref, dg_ref, db_ref, carry):
        i = pl.program_id(0)

        @pl.when(i == 0)
        def _():
            carry[...] = jnp.zeros(carry.shape, F32)
            dg_ref[...] = jnp.zeros(dg_ref.shape, F32)
            db_ref[...] = jnp.zeros(db_ref.shape, F32)

        lane = lax.broadcasted_iota(jnp.int32, (tm, LANE), 1)
        dcm = jnp.zeros((tm, LANE), F32)
        for h in range(FOX_H):
            dcm = jnp.where(lane == h, dq_ref[h][:, L_CQ:L_CQ + 1] - dk_ref[h][:, L_CK:L_CK + 1], dcm)

        def seg_mean(v):
            return sum(_nn(t, seg_ref[...]) for t in _split3(v)) * (1.0 / FOX_D)

        def seg_bcast(v):
            return sum(_nn(t, segt_ref[...]) for t in _split3(v))

        x = zf_ref[:, :1024]
        dy = jnp.concatenate([dq_ref[h][:, :FOX_D] for h in range(FOX_H)]
                             + [dk_ref[h][:, :FOX_D] for h in range(FOX_H)], axis=-1) * sc_ref[...]
        rb = seg_bcast(lax.rsqrt(seg_mean(x * x) + EPS))
        xn = x * rb
        dg_ref[...] += jnp.sum(dy * xn, axis=0, keepdims=True)
        dxn = dy * g_ref[...]
        dz_ref[:, :1024] = (rb * (dxn - xn * seg_bcast(seg_mean(dxn * xn)))).astype(BF)
        dz_ref[:, 1024:] = jnp.concatenate([dv_ref[h][:, :FOX_D] for h in range(FOX_H)], axis=-1).astype(BF)

        row = lax.broadcasted_iota(jnp.int32, (tm, tm), 0)
        col = lax.broadcasted_iota(jnp.int32, (tm, tm), 1)
        tri = (row <= col).astype(BF)
        hi, mid, lo = _split3(dcm)
        dlogf = _nn(tri, hi) + _nn(tri, mid) + _nn(tri, lo) + carry[...]
        carry[...] = dlogf[0:1, :]
        dff = jnp.where(lane < FOX_H, dlogf * _sigmoid(-(zff_ref[...] + b_ref[...])), 0.0)
        dff_ref[...] = dff.astype(BF)
        db_ref[...] += jnp.sum(dff, axis=0, keepdims=True)

    rev = lambda i: nt - 1 - i
    hsp = pl.BlockSpec((FOX_H, tm, LANE), lambda i: (0, rev(i), 0))
    const = lambda r, w: pl.BlockSpec((r, w), lambda i: (0, 0))
    seg = _segment_matrix()
    g_all = jnp.concatenate([jnp.tile(g_q, (1, FOX_H)), jnp.tile(g_k, (1, FOX_H))], axis=1)
    scale = jnp.asarray(np.concatenate([np.full((1, 512), 0.125, np.float32), np.ones((1, 512), np.float32)], axis=1))
    (dz, dff, dg, db), lands = _hosted_call(
        body, "fox_post_bwd", (nt,),
        [hsp, hsp, hsp, pl.BlockSpec((tm, 1536), lambda i: (rev(i), 1)),
         pl.BlockSpec((tm, LANE), lambda i: (rev(i), 0)), const(1, LANE), const(1, 1024), const(1, 1024),
         const(1024, LANE), const(LANE, 1024)],
        [pl.BlockSpec((tm, 1536), lambda i: (rev(i), 0)), pl.BlockSpec((tm, LANE), lambda i: (rev(i), 0)),
         const(1, 1024), const(1, LANE)],
        [jax.ShapeDtypeStruct((T, 1536), BF), jax.ShapeDtypeStruct((T, LANE), BF),
         jax.ShapeDtypeStruct((1, 1024), F32), jax.ShapeDtypeStruct((1, LANE), F32)],
        [pltpu.VMEM((1, LANE), F32)], VMEM_BIG, (dq, dk, dv, z_a, z_ff, b_f, g_all, scale, seg, seg.T), push)
    dg_heads = dg.reshape(2, FOX_H, FOX_D).sum(axis=1)
    return (dz, dff, dg_heads[0:1], dg_heads[1:2], db), lands


def _in_bwd(dz_ret, dz_gt, dz_fox, dz_a, dz_ff, w_a, w_ff, x, g_mix, dx2, tm=256, push=None):
    T = x.shape[0]

    def body(r_ref, t_ref, f_ref, a_ref, ff_ref, wa_ref, wf_ref, x_ref, g_ref, dx2_ref, dx_ref, dg_ref):
        i = pl.program_id(0)

        @pl.when(i == 0)
        def _():
            dg_ref[...] = jnp.zeros(dg_ref.shape, F32)

        dh = (_nt(r_ref[...], wa_ref[:, C_RET:C_GT]) + _nt(t_ref[...], wa_ref[:, C_GT:C_FOX])
              + _nt(f_ref[...], wa_ref[:, C_FOX:C_A]) + _nt(a_ref[...], wa_ref[:, C_A:C_END])
              + _nt(ff_ref[...], wf_ref[...]))
        xv = x_ref[...]
        r = lax.rsqrt(jnp.mean(xv * xv, axis=-1, keepdims=True) + EPS)
        xn = xv * r
        dg_ref[...] += jnp.sum(dh * xn, axis=0, keepdims=True)
        dxn = dh * g_ref[...]
        dx_ref[...] = dx2_ref[...] + r * (dxn - xn * jnp.mean(dxn * xn, axis=-1, keepdims=True))

    row = lambda w: pl.BlockSpec((tm, w), lambda i: (i, 0))
    const = lambda shp: pl.BlockSpec(shp, lambda i: (0,) * len(shp))
    return _hosted_call(
        body, "in_bwd", (T // tm,),
        [row(1024), row(512), row(1536), row(2048), row(LANE), const((D_MODEL, C_END)),
         const((D_MODEL, LANE)), row(1024), const((1, 1024)), row(1024)],
        [row(1024), const((1, 1024))],
        [jax.ShapeDtypeStruct((T, 1024), F32), jax.ShapeDtypeStruct((1, 1024), F32)],
        [], VMEM_BIG, (dz_ret, dz_gt, dz_fox, dz_a, dz_ff, w_a, w_ff, x, g_mix, dx2), push)


def _mesh_pos():
    return lax.axis_index("x"), lax.axis_index("y"), lax.axis_index("c")


def _staged_place(src, name):
    stacked = src.ndim == 3
    R, C = src.shape[-2:]
    tr = _row_tile(R, 128, 16)
    n = R // tr
    assert n >= 2

    def body(s_ref, o_ref, buf, sem):
        i = pl.program_id(0)
        slot = i % 2
        x, y, _ = _mesh_pos()
        kme = 2 * x + y

        def out_copy(s, step):
            return pltpu.make_async_copy(buf.at[s], o_ref.at[kme, pl.ds(pl.multiple_of(step * tr, tr), tr), :], sem.at[s])

        @pl.when(i >= 2)
        def _():
            out_copy(slot, i - 2).wait()

        buf[slot] = (s_ref[kme] if stacked else s_ref[...]).astype(BF)
        out_copy(slot, i).start()

        @pl.when(i == n - 1)
        def _():
            out_copy(1 - slot, i - 1).wait()
            out_copy(slot, i).wait()

    in_spec = (pl.BlockSpec((N_CHIP, tr, C), lambda i: (0, i, 0)) if stacked else pl.BlockSpec((tr, C), lambda i: (i, 0)))
    return pl.pallas_call(
        body, name=name, grid=(n,), in_specs=[in_spec], out_specs=pl.BlockSpec(memory_space=pl.ANY),
        out_shape=jax.ShapeDtypeStruct((N_CHIP, R, C), BF),
        scratch_shapes=[pltpu.VMEM((2, tr, C), BF), pltpu.SemaphoreType.DMA((2,))],
        compiler_params=_params(("arbitrary",)),
    )(src)


def _push_copies(src, land, send_sem, recv_sem, receiving):
    x, y, c = _mesh_pos()
    kme = 2 * x + y
    cps = []
    for w in range(len(land)):
        for j, (px, py) in enumerate([(1 - x, y), (x, 1 - y), (1 - x, 1 - y)]):
            kpeer = 2 * px + py
            cps.append(pltpu.make_async_remote_copy(
                src_ref=land[w].at[kme] if src is None else src[w].at[kpeer],
                dst_ref=land[w].at[kpeer if receiving else kme],
                send_sem=send_sem.at[3 * w + j], recv_sem=recv_sem.at[3 * w + j],
                device_id=(px, py, c), device_id_type=MESH))
    return cps


def _gather_two_level(stack, name):
    _, R, C = stack.shape
    hr = R // 2

    def body(_, land, send_sem, recv_sem):
        x, y, c = _mesh_pos()
        kme = 2 * x + y
        chips = [(1 - x, y), (x, 1 - y), (1 - x, 1 - y)]

        def rows(k, core):
            return land.at[k, pl.ds(pl.multiple_of(core * hr, hr), hr), :]

        def copy(idx, k, core, to):
            return pltpu.make_async_remote_copy(src_ref=rows(k, core), dst_ref=rows(k, core), send_sem=send_sem.at[idx],
                                                recv_sem=recv_sem.at[idx], device_id=to, device_id_type=MESH)

        first = [copy(j, kme, c, (px, py, c)) for j, (px, py) in enumerate(chips)]
        for cp in first:
            cp.start()
        passed = [copy(3 + j, 2 * px + py, c, (x, y, 1 - c)) for j, (px, py) in enumerate(chips)]
        for j, (px, py) in enumerate(chips):
            copy(j, 2 * px + py, c, (px, py, c)).wait_recv()
            passed[j].start()
        for j, (px, py) in enumerate(chips):
            copy(3 + j, 2 * px + py, 1 - c, (x, y, 1 - c)).wait_recv()
        for cp in first + passed:
            cp.wait_send()

    anyspec = pl.BlockSpec(memory_space=pl.ANY)
    return pl.pallas_call(
        body, name=name, in_specs=[anyspec], out_specs=anyspec,
        out_shape=jax.ShapeDtypeStruct(stack.shape, stack.dtype), input_output_aliases={0: 0},
        scratch_shapes=[pltpu.SemaphoreType.DMA((6,)), pltpu.SemaphoreType.DMA((6,))],
    )(stack)


def _gather_small(small):
    def body(sv, svo, ssend, srecv, sloc):
        x, y, c = _mesh_pos()
        me = 4 * x + 2 * y + c
        flips = [(b >> 2 & 1, b >> 1 & 1, b & 1) for b in range(1, 8)]
        others = [(1 - x if fx else x, 1 - y if fy else y, 1 - c if fc else c) for fx, fy, fc in flips]
        local = pltpu.make_async_copy(sv, svo.at[me], sloc)
        local.start()
        sends = []
        for j, (px, py, pc) in enumerate(others):
            cp = pltpu.make_async_remote_copy(
                src_ref=sv, dst_ref=svo.at[me], send_sem=ssend.at[j], recv_sem=srecv.at[j],
                device_id=(px, py, pc), device_id_type=MESH)
            cp.start()
            sends.append(cp)
        for j, (px, py, pc) in enumerate(others):
            pltpu.make_async_remote_copy(
                src_ref=sv, dst_ref=svo.at[4 * px + 2 * py + pc], send_sem=ssend.at[j], recv_sem=srecv.at[j],
                device_id=(px, py, pc), device_id_type=MESH).wait_recv()
        for cp in sends:
            cp.wait_send()
        local.wait()

    anyspec = pl.BlockSpec(memory_space=pl.ANY)
    return pl.pallas_call(
        body, name="gather_small", in_specs=[anyspec], out_specs=anyspec,
        out_shape=jax.ShapeDtypeStruct((8,) + small.shape, small.dtype),
        scratch_shapes=[pltpu.SemaphoreType.DMA((7,)), pltpu.SemaphoreType.DMA((7,)), pltpu.SemaphoreType.DMA],
    )(small)


def _sibling_exchange(arrs):
    n = len(arrs)

    def body(*refs):
        ins, outs = refs[:n], refs[n:2 * n]
        send_sems, recv_sems = refs[2 * n:]
        x, y, c = _mesh_pos()
        cps = [pltpu.make_async_remote_copy(
            src_ref=ins[w], dst_ref=outs[w], send_sem=send_sems.at[w], recv_sem=recv_sems.at[w],
            device_id=(x, y, 1 - c), device_id_type=MESH) for w in range(n)]
        for cp in cps:
            cp.start()
        for cp in cps:
            cp.wait_recv()
        for cp in cps:
            cp.wait_send()

    anyspec = pl.BlockSpec(memory_space=pl.ANY)
    return pl.pallas_call(
        body, name="sibling_exchange",
        in_specs=[anyspec] * n, out_specs=[anyspec] * n,
        out_shape=[jax.ShapeDtypeStruct(a.shape, a.dtype) for a in arrs],
        scratch_shapes=[pltpu.SemaphoreType.DMA((n,)), pltpu.SemaphoreType.DMA((n,))],
    )(*arrs)


def _sum_stack(own, recv, name):
    _, R, C = recv.shape
    tr = _row_tile(R, 256, 16)

    def body(g_ref, r_ref, o_ref):
        x, y, _ = _mesh_pos()
        kme = 2 * x + y
        acc = g_ref[kme].astype(F32)
        for d in range(1, N_CHIP):
            acc = acc + r_ref[(kme + d) % N_CHIP].astype(F32)
        o_ref[...] = acc

    spec = pl.BlockSpec((N_CHIP, tr, C), lambda i: (0, i, 0))
    return pl.pallas_call(
        body, name=name, grid=(R // tr,), in_specs=[spec, spec],
        out_specs=pl.BlockSpec((tr, C), lambda i: (i, 0)),
        out_shape=jax.ShapeDtypeStruct((R, C), F32),
        compiler_params=_params(("parallel",)),
    )(own, recv)


def _adam_math(w, g, m, v):
    m2 = ADAM_B1 * m + (1.0 - ADAM_B1) * g
    v2 = ADAM_B2 * v + (1.0 - ADAM_B2) * (g * g)
    m_hat = m2 / (1.0 - ADAM_B1 ** ADAM_STEP)
    v_hat = v2 / (1.0 - ADAM_B2 ** ADAM_STEP)
    delta = -ADAM_LR * (m_hat / (jnp.sqrt(v_hat) + ADAM_EPS) + ADAM_WD * w)
    return delta, m2, v2


def _adamw(w, m, v, s0, s1, name):
    R, C = w.shape
    tr = _row_tile(R, 128, 8)

    def body(w_ref, m_ref, v_ref, a_ref, b_ref, g_ref, d_ref, m2_ref, v2_ref):
        g = a_ref[...] + b_ref[...]
        delta, m2, v2 = _adam_math(w_ref[...], g, m_ref[...], v_ref[...])
        g_ref[...] = g
        d_ref[...] = delta
        m2_ref[...] = m2
        v2_ref[...] = v2

    spec = pl.BlockSpec((tr, C), lambda i: (i, 0))
    shp = jax.ShapeDtypeStruct((R, C), F32)
    return pl.pallas_call(
        body, name=name, grid=(R // tr,), in_specs=[spec] * 5, out_specs=[spec] * 4, out_shape=[shp] * 4,
        compiler_params=_params(("parallel",), VMEM_BIG),
    )(w, m, v, s0, s1)


def _adamw_small(w, m, v, gathered):
    def body(w_ref, m_ref, v_ref, s_ref, g_ref, d_ref, m2_ref, v2_ref):
        g = s_ref[0]
        for d in range(1, 8):
            g = g + s_ref[d]
        delta, m2, v2 = _adam_math(w_ref[...], g, m_ref[...], v_ref[...])
        g_ref[...] = g
        d_ref[...] = delta
        m2_ref[...] = m2
        v2_ref[...] = v2

    shp = jax.ShapeDtypeStruct(w.shape, F32)
    return pl.pallas_call(body, name="adamw_small", out_shape=[shp] * 4)(w, m, v, gathered)


SMALL = (("g_mix", 1024), ("g_ffn", 1024), ("g_ret_norm", 512), ("g_fox_q", 64), ("g_fox_k", 64), ("b_forget", 8))
SMALL_W = 3072


def _pack_small(parts):
    cols = []
    for (name, n) in SMALL:
        p = parts[name].reshape(1, -1)[:, :n]
        pad = -n % LANE
        cols.append(jnp.pad(p, ((0, 0), (0, pad))) if pad else p)
    used = sum(c.shape[1] for c in cols)
    cols.append(jnp.zeros((1, SMALL_W - used), F32))
    return jnp.concatenate(cols, axis=1)


def _unpack_small(vec):
    out, off = {}, 0
    for (name, n) in SMALL:
        out[name] = vec[:, off:off + n]
        off += n + (-n % LANE)
    return out


def kernel(x, g_mix, w_in, b_forget, g_ret_norm, w_ret_o, g_fox_q, g_fox_k, w_fox_o, w_out, g_ffn, w_gate, w_up, w_down, loss_target, m_g_mix, m_w_in, m_b_forget, m_g_ret_norm, m_w_ret_o, m_g_fox_q, m_g_fox_k, m_w_fox_o, m_w_out, m_g_ffn, m_w_gate, m_w_up, m_w_down, v_g_mix, v_w_in, v_b_forget, v_g_ret_norm, v_w_ret_o, v_g_fox_q, v_g_fox_k, v_w_fox_o, v_w_out, v_g_ffn, v_w_gate, v_w_up, v_w_down):
    T = x.shape[1]
    xs = x[0]
    tgt = loss_target[0]
    big_names = ("w_in", "w_ret_o", "w_fox_o", "w_out", "w_gate", "w_up", "w_down")
    tr = lambda a: jnp.swapaxes(a[0], 0, 1)
    big_w = dict(w_in=w_in[0], w_ret_o=w_ret_o[0], w_fox_o=w_fox_o[0], w_out=w_out[0], w_gate=tr(w_gate),
                 w_up=tr(w_up), w_down=w_down[0])
    big_m = dict(w_in=m_w_in[0], w_ret_o=m_w_ret_o[0], w_fox_o=m_w_fox_o[0], w_out=m_w_out[0], w_gate=tr(m_w_gate),
                 w_up=tr(m_w_up), w_down=m_w_down[0])
    big_v = dict(w_in=v_w_in[0], w_ret_o=v_w_ret_o[0], w_fox_o=v_w_fox_o[0], w_out=v_w_out[0], w_gate=tr(v_w_gate),
                 w_up=tr(v_w_up), w_down=v_w_down[0])
    small_w = dict(g_mix=g_mix, g_ffn=g_ffn, g_ret_norm=g_ret_norm, g_fox_q=g_fox_q, g_fox_k=g_fox_k, b_forget=b_forget)
    small_m = dict(g_mix=m_g_mix, g_ffn=m_g_ffn, g_ret_norm=m_g_ret_norm, g_fox_q=m_g_fox_q, g_fox_k=m_g_fox_k,
                   b_forget=m_b_forget)
    small_v = dict(g_mix=v_g_mix, g_ffn=v_g_ffn, g_ret_norm=v_g_ret_norm, g_fox_q=v_g_fox_q, g_fox_k=v_g_fox_k,
                   b_forget=v_b_forget)

    stacks = {n: _staged_place(big_w[n], "place_" + n) for n in big_names}
    s_in = _gather_two_level(stacks["w_in"], "gather_w_in")
    w_a, w_ff = _assemble_w_in(s_in)
    b_pad = jnp.pad(b_forget, ((0, 0), (0, LANE - FOX_H)))
    cos_t, sin_t = _rope_tables(T)
    consts = _ret_consts()

    h = _rms_cast(xs, g_mix)
    z_a, (s_gate, s_up) = _mm_nn(h, w_a, "proj_in", push=(None, [stacks["w_gate"], stacks["w_up"]]))
    z_ff, _ = _mm_nn(h, w_ff, "proj_ff")
    (qr, kr, qf, kf, vf, c_cum, nmax), (s_down, s_ro, s_fo, s_out) = _mix_prep(
        z_a, z_ff, cos_t, sin_t, b_pad, g_fox_q, g_fox_k,
        push=(None, [stacks["w_down"], stacks["w_ret_o"], stacks["w_fox_o"], stacks["w_out"]]))
    jstart, iend = _prune_tables(c_cum, nmax, 512)
    o_raw, u_r, states = _ret_fwd(qr, kr, z_a, g_ret_norm, consts)
    o_fox, q2 = _fox_fwd(jstart, qf, kf, vf)
    y_r, y_f, mrg, x2, h2, o_cat = _merge_out(u_r, o_fox, z_a, xs, g_ffn, s_ro, s_fo, s_out)
    sa, sb, act, dy, loss_vec = _ffn_fwd(h2, x2, tgt, s_gate, s_up, s_down)
    loss = lax.psum(0.5 / D_MODEL * jnp.sum(loss_vec), ("x", "y", "c"))

    def scatter_job(grads):
        return (grads, [lax.empty(g.shape, g.dtype) for g in grads])

    dgp, dup, dx2, dg_ffn = _ffn_bwd(dy, sa, sb, x2, g_ffn, s_gate, s_up, s_down)
    g_gate, g_up, g_down = (_grad_astack(dgp, h2, "gw_gate"), _grad_astack(dup, h2, "gw_up"),
                            _grad_astack(act, dy, "gw_down"))
    (d_yr, d_yf, dz_gt, dz_a, d_o, do_fox, dg_ret), (r_gate, r_up) = _out_bwd(
        dx2, z_a, y_r, y_f, o_raw, o_fox, g_ret_norm, s_ro, s_fo, s_out,
        push=scatter_job([g_gate, g_up]))
    dz_ret, (r_down,) = _ret_bwd(d_o, qr, kr, z_a, states, cos_t, sin_t, consts, push=scatter_job([g_down]))
    dq_f, dk_f, dv_f = _fox_bwd(iend, q2, kf, vf, do_fox)
    g_mid = [_grad_colstack(u_r, d_yr, "gw_ret_o", 256), _grad_colstack(o_cat, d_yf, "gw_fox_o", 256),
             _grad_plain(mrg, dx2, "gw_out", BF).reshape(N_CHIP, 256, D_MODEL)]
    (dz_fox, dz_ff, dg_q, dg_k, db_f), (r_ro, r_fo, r_out) = _fox_post_bwd(
        dq_f, dk_f, dv_f, z_a, z_ff, b_pad, g_fox_q, g_fox_k, push=scatter_job(g_mid))
    g_in = _pack_g_in(_grad_plain(h, dz_ret, "gw_in_ret", F32), _grad_plain(h, dz_gt, "gw_in_gt", F32),
                      _grad_plain(h, dz_fox, "gw_in_fox", F32, tn=1536),
                      _grad_plain(h, dz_a, "gw_in_a", F32, tk=1024, tn=2048),
                      _grad_plain(h, dz_ff, "gw_in_ff", F32))
    (grad_x, dg_mix), (r_in,) = _in_bwd(dz_ret, dz_gt, dz_fox, dz_a, dz_ff, w_a, w_ff, xs, g_mix, dx2,
                                        push=scatter_job([g_in]))
    small_g = _pack_small(dict(g_mix=dg_mix, g_ffn=dg_ffn, g_ret_norm=dg_ret, g_fox_q=dg_q, g_fox_k=dg_k, b_forget=db_f))

    small_all = _gather_small(small_g)
    sums = [_sum_stack(g, r, "sum_" + n) for g, r, n in zip(
        [g_in] + g_mid + [g_gate, g_up, g_down], [r_in, r_ro, r_fo, r_out, r_gate, r_up, r_down], big_names)]
    sib = _sibling_exchange(sums)
    big_out = {n: _adamw(big_w[n], big_m[n], big_v[n], sums[i], sib[i], "adamw_" + n) for i, n in enumerate(big_names)}
    sg, sd, sm, sv = _adamw_small(_pack_small(small_w), _pack_small(small_m), _pack_small(small_v), small_all)
    small_out = [_unpack_small(t) for t in (sg, sd, sm, sv)]

    order = ("g_mix", "w_in", "b_forget", "g_ret_norm", "w_ret_o", "g_fox_q", "g_fox_k", "w_fox_o", "w_out", "g_ffn",
             "w_gate", "w_up", "w_down")
    outs = [loss, grad_x[None]]
    for idx in range(4):
        for n in order:
            if n in ("w_gate", "w_up"):
                outs.append(jnp.swapaxes(big_out[n][idx], 0, 1)[None])
            else:
                outs.append(big_out[n][idx][None] if n in big_out else small_out[idx][n])
    return tuple(outs)
```

```python
import functools
import math

import numpy as np
import jax
import jax.numpy as jnp
from jax import lax
from jax.experimental import pallas as pl
from jax.experimental.pallas import tpu as pltpu

F32 = jnp.float32
BF = jnp.bfloat16
MESH = pl.DeviceIdType.MESH

D_MODEL = 1024
D_FF = 2816
N_CHIP = 4
FF_SH = D_FF // N_CHIP
IN_COLS = 5128
IN_SH = IN_COLS // N_CHIP
RET_H, RET_DV = 4, 128
FOX_H, FOX_D = 8, 64
CHUNK = 128
EPS = 1e-6
NEG = -1e30
LANE = 128
C_RET, C_GT, C_FOX, C_A, C_END = 0, 1024, 1536, 3072, 5120
L_CQ, L_CK, L_LSE, L_MAX = 64, 67, 70, 73

ADAM_LR, ADAM_B1, ADAM_B2, ADAM_EPS, ADAM_WD, ADAM_STEP = 0.001, 0.9, 0.999, 1e-08, 0.01, 10
VMEM_BIG = 56 * 1024 * 1024
VMEM_HUGE = 60 * 1024 * 1024
GRAD_TK = 2048
FFN_TM = 512
FOX_SUB = 256


def _nn(a, b):
    return lax.dot_general(a, b, (((1,), (0,)), ((), ())), preferred_element_type=F32)


def _nt(a, b):
    return lax.dot_general(a, b, (((1,), (1,)), ((), ())), preferred_element_type=F32)


def _tn(a, b):
    return lax.dot_general(a, b, (((0,), (0,)), ((), ())), preferred_element_type=F32)


def _split3(x):
    hi = x.astype(BF)
    r = x - hi.astype(F32)
    mid = r.astype(BF)
    lo = (r - mid.astype(F32)).astype(BF)
    return hi, mid, lo


def _sigmoid(x):
    return 0.5 * jnp.tanh(0.5 * x) + 0.5


def _swap32(x):
    lane = lax.broadcasted_iota(jnp.int32, x.shape, 1)
    return jnp.where(lane < 32, pltpu.roll(x, 96, 1), pltpu.roll(x, 32, 1))


def _params(sem, vmem=None):
    return pltpu.CompilerParams(dimension_semantics=sem, vmem_limit_bytes=vmem)


def _row_tile(rows, cap, mult):
    return max(d for d in range(mult, cap + 1, mult) if rows % d == 0)


def _assemble_w_in(stack, tr=256):
    def body(s_ref, a_ref, f_ref):
        full = jnp.concatenate([s_ref[k].astype(F32) for k in range(N_CHIP)], axis=-1)
        a_ref[...] = jnp.concatenate([full[:, :3072], full[:, 3080:IN_COLS]], axis=-1).astype(BF)
        f_ref[...] = jnp.concatenate([full[:, 3072:3080], jnp.zeros((tr, LANE - FOX_H), F32)], axis=-1).astype(BF)

    return pl.pallas_call(
        body, name="assemble_w_in", grid=(D_MODEL // tr,),
        in_specs=[pl.BlockSpec((N_CHIP, tr, IN_SH), lambda i: (0, i, 0))],
        out_specs=[pl.BlockSpec((tr, C_END), lambda i: (i, 0)), pl.BlockSpec((tr, LANE), lambda i: (i, 0))],
        out_shape=[jax.ShapeDtypeStruct((D_MODEL, C_END), BF), jax.ShapeDtypeStruct((D_MODEL, LANE), BF)],
        compiler_params=_params(("parallel",), VMEM_BIG),
    )(stack)


def _pack_g_in(g_ret, g_gt, g_fox, g_a, g_ff, tr=256):
    def body(r_ref, t_ref, x_ref, a_ref, f_ref, o_ref):
        full = jnp.concatenate([r_ref[...], t_ref[...], x_ref[...], f_ref[...][:, :FOX_H], a_ref[...]], axis=-1)
        for k in range(N_CHIP):
            o_ref[k] = full[:, k * IN_SH:(k + 1) * IN_SH].astype(BF)

    def spec(w):
        return pl.BlockSpec((tr, w), lambda i: (i, 0))

    return pl.pallas_call(
        body, name="pack_g_in", grid=(D_MODEL // tr,),
        in_specs=[spec(1024), spec(512), spec(1536), spec(2048), spec(LANE)],
        out_specs=pl.BlockSpec((N_CHIP, tr, IN_SH), lambda i: (0, i, 0)),
        out_shape=jax.ShapeDtypeStruct((N_CHIP, D_MODEL, IN_SH), BF),
        compiler_params=_params(("parallel",), VMEM_BIG),
    )(g_ret, g_gt, g_fox, g_a, g_ff)


def _rms_cast(x, g, tm=512):
    T = x.shape[0]

    def body(x_ref, g_ref, o_ref):
        xv = x_ref[...]
        r = lax.rsqrt(jnp.mean(xv * xv, axis=-1, keepdims=True) + EPS)
        o_ref[...] = (xv * r * g_ref[...]).astype(BF)

    return pl.pallas_call(
        body, name="rms_cast", grid=(T // tm,),
        in_specs=[pl.BlockSpec((tm, D_MODEL), lambda i: (i, 0)), pl.BlockSpec((1, D_MODEL), lambda i: (0, 0))],
        out_specs=pl.BlockSpec((tm, D_MODEL), lambda i: (i, 0)),
        out_shape=jax.ShapeDtypeStruct((T, D_MODEL), BF),
        compiler_params=_params(("parallel",)),
    )(x, g)


def _hosted_call(body, name, grid, in_specs, out_specs, out_shape, scratch_shapes, vmem, args, push):
    sem = ("arbitrary",) * len(grid)
    if push is None:
        res = pl.pallas_call(body, name=name, grid=grid, in_specs=in_specs, out_specs=out_specs, out_shape=out_shape,
                             scratch_shapes=scratch_shapes, compiler_params=_params(sem, vmem))(*args)
        return list(res), []
    srcs, lands = push
    ns, nl, n_in, n_out = (0 if srcs is None else len(srcs)), len(lands), len(in_specs), len(out_specs)
    n_scr = len(scratch_shapes)

    def wrapped(*refs):
        pos = n_in + ns + nl
        ins, x_in = refs[:n_in], refs[n_in:pos]
        outs, x_out = refs[pos:pos + n_out], refs[pos + n_out:pos + n_out + nl]
        scr = refs[pos + n_out + nl:pos + n_out + nl + n_scr]
        ssem, rsem = refs[-2], refs[-1]
        src = None if srcs is None else x_in[:ns]
        ids = [pl.program_id(a) for a in range(len(grid))]
        first = functools.reduce(lambda p, q: p & q, [ids[a] == 0 for a in range(len(grid))])
        last = functools.reduce(lambda p, q: p & q, [ids[a] == grid[a] - 1 for a in range(len(grid))])

        @pl.when(first)
        def _():
            for cp in _push_copies(src, x_out, ssem, rsem, False):
                cp.start()

        body(*ins, *outs, *scr)

        @pl.when(last)
        def _():
            for cp in _push_copies(src, x_out, ssem, rsem, True):
                cp.wait_recv()
                cp.wait_send()

    anyspec = pl.BlockSpec(memory_space=pl.ANY)
    extra = ([] if srcs is None else list(srcs)) + list(lands)
    res = pl.pallas_call(
        wrapped, name=name, grid=grid,
        in_specs=list(in_specs) + [anyspec] * len(extra), out_specs=list(out_specs) + [anyspec] * nl,
        out_shape=list(out_shape) + [jax.ShapeDtypeStruct(a.shape, a.dtype) for a in lands],
        input_output_aliases={n_in + ns + i: n_out + i for i in range(nl)},
        scratch_shapes=list(scratch_shapes) + [pltpu.SemaphoreType.DMA((3 * nl,)), pltpu.SemaphoreType.DMA((3 * nl,))],
        compiler_params=_params(sem, vmem),
    )(*args, *extra)
    return list(res[:n_out]), list(res[n_out:])


def _mm_nn(a, b, name, out_dtype, tm=512, tn=1024, push=None):
    M, K = a.shape
    N = b.shape[1]
    tn = min(tn, N)

    def body(a_ref, b_ref, o_ref):
        o_ref[...] = _nn(a_ref[...], b_ref[...]).astype(o_ref.dtype)

    (out,), lands = _hosted_call(
        body, name, (N // tn, M // tm),
        [pl.BlockSpec((tm, K), lambda j, i: (i, 0)), pl.BlockSpec((K, tn), lambda j, i: (0, j))],
        [pl.BlockSpec((tm, tn), lambda j, i: (i, j))], [jax.ShapeDtypeStruct((M, N), out_dtype)], [], None, (a, b), push)
    return out, lands


def _mm_tn(a, b, name, grid, a_spec, b_spec, o_spec, out_shape, acc_shape):
    nk = grid[-1]

    def body(a_ref, b_ref, o_ref, acc):
        k = pl.program_id(len(grid) - 1)

        @pl.when(k == 0)
        def _():
            acc[...] = jnp.zeros(acc.shape, F32)

        acc[...] += _tn(a_ref[...].astype(BF), b_ref[...].astype(BF))

        @pl.when(k == nk - 1)
        def _():
            o_ref[...] = acc[...].astype(o_ref.dtype)

    return pl.pallas_call(
        body, name=name, grid=grid, in_specs=[a_spec, b_spec], out_specs=o_spec, out_shape=out_shape,
        scratch_shapes=[pltpu.VMEM(acc_shape, F32)],
        compiler_params=_params(("parallel",) * (len(grid) - 1) + ("arbitrary",), VMEM_BIG),
    )(a, b)


def _grad_plain(a, b, name, out_dtype, tk=GRAD_TK, tn=1024):
    T, M = a.shape
    N = b.shape[1]
    tn = min(tn, N)
    return _mm_tn(a, b, name, (N // tn, T // tk),
                  pl.BlockSpec((tk, M), lambda j, k: (k, 0)), pl.BlockSpec((tk, tn), lambda j, k: (k, j)),
                  pl.BlockSpec((M, tn), lambda j, k: (0, j)), jax.ShapeDtypeStruct((M, N), out_dtype), (M, tn))


def _grad_colstack(a, b, name, wcol, tk=GRAD_TK):
    T, M = a.shape
    N = b.shape[1]
    S = N // wcol
    nk = T // tk

    def body(a_ref, b_ref, o_ref, acc):
        k = pl.program_id(0)

        @pl.when(k == 0)
        def _():
            acc[...] = jnp.zeros(acc.shape, F32)

        acc[...] += _tn(a_ref[...], b_ref[...])

        @pl.when(k == nk - 1)
        def _():
            for s in range(S):
                o_ref[s] = acc[:, s * wcol:(s + 1) * wcol].astype(BF)

    return pl.pallas_call(
        body, name=name, grid=(nk,),
        in_specs=[pl.BlockSpec((tk, M), lambda k: (k, 0)), pl.BlockSpec((tk, N), lambda k: (k, 0))],
        out_specs=pl.BlockSpec((S, M, wcol), lambda k: (0, 0, 0)), out_shape=jax.ShapeDtypeStruct((S, M, wcol), BF),
        scratch_shapes=[pltpu.VMEM((M, N), F32)], compiler_params=_params(("arbitrary",), VMEM_BIG),
    )(a, b)


def _grad_astack(a, b, name, tk=1024):
    S, T, m = a.shape
    N = b.shape[1]
    nk = T // tk

    def body(a_ref, b_ref, o_ref, acc):
        k = pl.program_id(0)

        @pl.when(k == 0)
        def _():
            acc[...] = jnp.zeros(acc.shape, F32)

        bb = b_ref[...].astype(BF)
        for s in range(S):
            acc[s] += _tn(a_ref[s], bb)

        @pl.when(k == nk - 1)
        def _():
            o_ref[...] = acc[...].astype(BF)

    return pl.pallas_call(
        body, name=name, grid=(nk,),
        in_specs=[pl.BlockSpec((S, tk, m), lambda k: (0, k, 0)), pl.BlockSpec((tk, N), lambda k: (k, 0))],
        out_specs=pl.BlockSpec((S, m, N), lambda k: (0, 0, 0)), out_shape=jax.ShapeDtypeStruct((S, m, N), BF),
        scratch_shapes=[pltpu.VMEM((S, m, N), F32)], compiler_params=_params(("arbitrary",), VMEM_BIG),
    )(a, b)


def _rope_tables(T):
    half = 32
    pos = np.arange(T, dtype=np.float32)
    inv_freq = (np.float32(1.0) / (np.float32(10000.0) ** (np.arange(half, dtype=np.float32) / np.float32(half)))).astype(np.float32)
    ang = (pos[:, None] * inv_freq[None, :]).astype(np.float32)
    cos, sin = np.cos(ang).astype(np.float32), np.sin(ang).astype(np.float32)
    z = np.zeros((T, 64), np.float32)
    return (jnp.asarray(np.concatenate([cos, cos, z], axis=-1)), jnp.asarray(np.concatenate([-sin, sin, z], axis=-1)))


def _ret_consts():
    h = np.arange(RET_H, dtype=np.float32)
    log_g = np.log1p(-(np.float32(2.0) ** (-5.0 - h))).astype(np.float32)
    idx = np.arange(CHUNK, dtype=np.float32)
    diff = idx[:, None] - idx[None, :]
    decay = np.where(diff[None] >= 0, np.exp(np.maximum(diff, 0.0)[None] * log_g[:, None, None]), 0.0)
    zeta = np.exp((CHUNK - 1.0 - idx)[None, :] * log_g[:, None])
    xi = np.exp((idx + 1.0)[None, :] * log_g[:, None])
    gc = np.exp(CHUNK * log_g)
    bc = lambda v: np.broadcast_to(v[:, :, None], (RET_H, CHUNK, LANE)).astype(np.float32)
    gcb = np.broadcast_to(gc[:, None, None], (RET_H, CHUNK, LANE)).astype(np.float32)
    return (jnp.asarray(decay.astype(np.float32)), jnp.asarray(bc(zeta)), jnp.asarray(bc(xi)), jnp.asarray(gcb))


def _mix_prep(z_a, z_ff, cos_t, sin_t, b_f, g_q, g_k, tm=256, push=None):
    T = z_a.shape[0]

    def body(zqk_ref, zf_ref, zff_ref, cos_ref, sin_ref, b_ref, g_ref, seg_ref, segt_ref,
             qr_ref, kr_ref, qf_ref, kf_ref, vf_ref, c_ref, nmax_ref, carry):
        i = pl.program_id(0)

        @pl.when(i == 0)
        def _():
            carry[...] = jnp.zeros(carry.shape, F32)
            nmax_ref[...] = jnp.zeros(nmax_ref.shape, F32)

        lane = lax.broadcasted_iota(jnp.int32, (tm, LANE), 1)
        zpad = jnp.zeros((tm, 64), F32)
        cosv, sinv = cos_ref[...], sin_ref[...]
        zqk = zqk_ref[...].astype(F32)
        for h in range(RET_H):
            for src, dst, scale in ((0, qr_ref, 1.0), (256, kr_ref, 0.125)):
                xh = jnp.concatenate([zqk[:, src + 64 * h: src + 64 * h + 64], zpad], axis=-1)
                rot = xh * cosv + _swap32(xh) * sinv
                dst[h] = (rot * scale).astype(BF)

        lf_in = zff_ref[...] + b_ref[...]
        logf = jnp.minimum(lf_in, 0.0) - jnp.log(1.0 + jnp.exp(-jnp.abs(lf_in)))
        row = lax.broadcasted_iota(jnp.int32, (tm, tm), 0)
        col = lax.broadcasted_iota(jnp.int32, (tm, tm), 1)
        tri = (row >= col).astype(BF)
        hi, mid, lo = _split3(logf)
        cs = _nn(tri, hi) + _nn(tri, mid) + _nn(tri, lo) + carry[...]
        carry[...] = cs[tm - 1:tm, :]
        c_ref[...] = cs

        def seg_sum(v):
            return sum(_nn(t, seg_ref[...]) for t in _split3(v))

        zf = zf_ref[...].astype(F32)
        xqk = zf[:, :1024]
        rinv = lax.rsqrt(seg_sum(xqk * xqk) * (1.0 / FOX_D) + EPS)
        xn = xqk * sum(_nn(t, segt_ref[...]) for t in _split3(rinv)) * g_ref[...]
        nmax_ref[...] = jnp.maximum(nmax_ref[...], jnp.max(seg_sum(xn * xn), axis=0, keepdims=True))

        one = jnp.ones((tm, LANE), F32)
        for h in range(FOX_H):
            c = cs[:, h:h + 1]
            chi, cmid, clo = [t.astype(F32) for t in _split3(c)]
            qn = xn[:, 64 * h:64 * h + 64]
            kn = xn[:, 512 + 64 * h:512 + 64 * h + 64]
            vh = zf[:, 1024 + 64 * h:1024 + 64 * h + 64]
            qa = jnp.concatenate([qn, zpad], axis=-1)
            qa = jnp.where(lane == L_CQ, chi, jnp.where(lane == L_CQ + 1, cmid, jnp.where(lane == L_CQ + 2, clo, qa)))
            qa = jnp.where((lane >= L_CK) & (lane < L_CK + 3), one, qa)
            ka = jnp.concatenate([kn, zpad], axis=-1)
            ka = jnp.where(lane == L_CK, -chi, jnp.where(lane == L_CK + 1, -cmid, jnp.where(lane == L_CK + 2, -clo, ka)))
            ka = jnp.where(((lane >= L_CQ) & (lane < L_CQ + 3)) | ((lane >= L_LSE) & (lane < L_MAX + 3)), one, ka)
            va = jnp.concatenate([vh, zpad], axis=-1)
            va = jnp.where((lane >= 64) & (lane < 67), one, va)
            qf_ref[h] = qa.astype(BF)
            kf_ref[h] = ka.astype(BF)
            vf_ref[h] = va.astype(BF)

    hspec4 = pl.BlockSpec((RET_H, tm, LANE), lambda i: (0, i, 0))
    hspec8 = pl.BlockSpec((FOX_H, tm, LANE), lambda i: (0, i, 0))
    const = lambda r, w: pl.BlockSpec((r, w), lambda i: (0, 0))
    seg = _segment_matrix()
    g_all = jnp.concatenate([jnp.tile(g_q * 0.125, (1, FOX_H)), jnp.tile(g_k, (1, FOX_H))], axis=1)
    return _hosted_call(
        body, "mix_prep", (T // tm,),
        [pl.BlockSpec((tm, 512), lambda i: (i, 0)), pl.BlockSpec((tm, 1536), lambda i: (i, 1)),
         pl.BlockSpec((tm, LANE), lambda i: (i, 0)), pl.BlockSpec((tm, LANE), lambda i: (i, 0)),
         pl.BlockSpec((tm, LANE), lambda i: (i, 0)), const(1, LANE), const(1, 1024), const(1024, LANE), const(LANE, 1024)],
        [hspec4, hspec4, hspec8, hspec8, hspec8, pl.BlockSpec((tm, LANE), lambda i: (i, 0)), const(1, LANE)],
        [jax.ShapeDtypeStruct((RET_H, T, LANE), BF)] * 2 + [jax.ShapeDtypeStruct((FOX_H, T, LANE), BF)] * 3
        + [jax.ShapeDtypeStruct((T, LANE), F32), jax.ShapeDtypeStruct((1, LANE), F32)],
        [pltpu.VMEM((1, LANE), F32)], VMEM_BIG, (z_a, z_a, z_ff, cos_t, sin_t, b_f, g_all, seg, seg.T), push)


def _segment_matrix():
    m = np.zeros((2 * FOX_H * FOX_D, LANE), np.float32)
    m[np.arange(2 * FOX_H * FOX_D), np.arange(2 * FOX_H * FOX_D) // FOX_D] = 1.0
    return jnp.asarray(m, dtype=BF)


def _ret_fwd(qr, kr, z_a, g_ret, consts, tt=512):
    T = z_a.shape[0]
    nch = tt // CHUNK
    decay, zeta, xi, gcb = consts

    def body(q_ref, k_ref, v_ref, gt_ref, g_ref, d_ref, ze_ref, xi_ref, gc_ref, o_ref, u_ref, st_ref, r_sc):
        i = pl.program_id(0)

        @pl.when(i == 0)
        def _():
            r_sc[...] = jnp.zeros(r_sc.shape, F32)

        for c in range(nch):
            rows = slice(c * CHUNK, (c + 1) * CHUNK)
            for h in range(RET_H):
                cols = slice(h * RET_DV, (h + 1) * RET_DV)
                q, k = q_ref[h, rows, :], k_ref[h, rows, :]
                v32 = v_ref[rows, cols].astype(F32)
                r = r_sc[h]
                st_ref[h, rows, :] = r
                s = _nt(q, k) * d_ref[h]
                o = _nn(s.astype(BF), v32.astype(BF)) + _nn(q, r.astype(BF)) * xi_ref[h]
                r_sc[h] = gc_ref[h] * r + _tn(k, (v32 * ze_ref[h]).astype(BF))
                o_ref[rows, cols] = o
                mu = jnp.mean(o, axis=-1, keepdims=True)
                xc = o - mu
                on = xc * lax.rsqrt(jnp.mean(xc * xc, axis=-1, keepdims=True) + EPS)
                gt = gt_ref[rows, cols].astype(F32)
                u_ref[rows, cols] = (gt * _sigmoid(gt) * (on * g_ref[:, cols])).astype(BF)

    hspec = pl.BlockSpec((RET_H, tt, LANE), lambda i: (0, i, 0))
    cspec = pl.BlockSpec((RET_H, CHUNK, LANE), lambda i: (0, 0, 0))
    return pl.pallas_call(
        body, name="ret_fwd", grid=(T // tt,),
        in_specs=[hspec, hspec, pl.BlockSpec((tt, 512), lambda i: (i, 1)), pl.BlockSpec((tt, 512), lambda i: (i, 2)),
                  pl.BlockSpec((1, 512), lambda i: (0, 0)), cspec, cspec, cspec, cspec],
        out_specs=[pl.BlockSpec((tt, 512), lambda i: (i, 0)), pl.BlockSpec((tt, 512), lambda i: (i, 0)), hspec],
        out_shape=[jax.ShapeDtypeStruct((T, 512), F32), jax.ShapeDtypeStruct((T, 512), BF),
                   jax.ShapeDtypeStruct((RET_H, T, LANE), F32)],
        scratch_shapes=[pltpu.VMEM((RET_H, CHUNK, LANE), F32)],
        compiler_params=_params(("arbitrary",), VMEM_BIG),
    )(qr, kr, z_a, z_a, g_ret, decay, zeta, xi, gcb)


PRUNE_LOG = -110.0


def _prune_tables(c, nmax, sub):
    n = c.shape[0] // sub
    u = jnp.sqrt(nmax[0, :FOX_H] * nmax[0, FOX_H:2 * FOX_H]) * 1.02 + 0.5
    first = c[0::sub, :FOX_H].T
    last = c[sub - 1::sub, :FOX_H].T
    blk = jnp.arange(n, dtype=jnp.int32)
    needed = (2.0 * u[:, None, None] + first[:, :, None] - last[:, None, :] >= PRUNE_LOG) | (blk[None, :] >= blk[:, None])[None]
    jlo = jnp.argmax(needed, axis=2).astype(jnp.int32)
    jstart = jnp.minimum(jlo[:, 0::2], jlo[:, 1::2]) // 2
    need_q = jlo[:, None, :] <= (2 * jnp.arange(n // 2, dtype=jnp.int32) + 1)[None, :, None]
    iend = n - jnp.argmax(need_q[:, :, ::-1], axis=2).astype(jnp.int32)
    return jstart.astype(jnp.int32), iend.astype(jnp.int32)


def _fox_fwd(jstart, q, k, v, sub=FOX_SUB):
    H, T, _ = q.shape
    tb = 2 * sub

    def body(js_ref, q_ref, k_ref, v_ref, o_ref, q2_ref, mx_sc, acc_sc):
        i = pl.program_id(1)
        j0 = js_ref[pl.program_id(0), i]
        lane = lax.broadcasted_iota(jnp.int32, (sub, LANE), 1)
        row = lax.broadcasted_iota(jnp.int32, (sub, sub), 0)
        col = lax.broadcasted_iota(jnp.int32, (sub, sub), 1)
        causal = row >= col
        qs = [q_ref[0:sub, :], q_ref[sub:tb, :]]
        d0 = pl.multiple_of(i * tb, tb)
        d1 = pl.multiple_of(i * tb + sub, sub)

        def lane_max(s):
            m = s[:, 0:LANE]
            for c in range(1, s.shape[1] // LANE):
                m = jnp.maximum(m, s[:, c * LANE:(c + 1) * LANE])
            return m

        mx_sc[...] = jnp.full(mx_sc.shape, NEG, F32)

        def max_body(j, carry):
            kb = k_ref[pl.ds(pl.multiple_of(j * tb, tb), tb), :]
            for a in range(2):
                mx_sc[a] = jnp.maximum(mx_sc[a], lane_max(_nt(qs[a], kb)))
            return carry

        lax.fori_loop(j0, i, max_body, 0)
        k0, k1 = k_ref[pl.ds(d0, sub), :], k_ref[pl.ds(d1, sub), :]
        v0, v1 = v_ref[pl.ds(d0, sub), :], v_ref[pl.ds(d1, sub), :]
        mx = [jnp.maximum(mx_sc[0], lane_max(jnp.where(causal, _nt(qs[0], k0), NEG))),
              jnp.maximum(jnp.maximum(mx_sc[1], lane_max(_nt(qs[1], k0))),
                          lane_max(jnp.where(causal, _nt(qs[1], k1), NEG)))]
        ms = [jnp.max(t, axis=1, keepdims=True) for t in mx]

        def put3(base, first, val):
            hi, mid, lo = _split3(val)
            return jnp.where(lane == first, hi, jnp.where(lane == first + 1, mid, jnp.where(lane == first + 2, lo, base)))

        qm = [put3(qs[a], L_MAX, -ms[a]) for a in range(2)]

        acc_sc[...] = jnp.zeros(acc_sc.shape, F32)

        def acc_body(j, carry):
            off = pl.multiple_of(j * tb, tb)
            kb, vb = k_ref[pl.ds(off, tb), :], v_ref[pl.ds(off, tb), :]
            for a in range(2):
                acc_sc[a] += _nn(jnp.exp(_nt(qm[a], kb)).astype(BF), vb)
            return carry

        lax.fori_loop(j0, i, acc_body, 0)

        def pv(qa, kk, vv, masked):
            p = jnp.exp(_nt(qa, kk))
            if masked:
                p = jnp.where(causal, p, 0.0)
            return _nn(p.astype(BF), vv)

        accs = [acc_sc[0] + pv(qm[0], k0, v0, True),
                acc_sc[1] + pv(qm[1], k0, v0, False) + pv(qm[1], k1, v1, True)]
        for a in range(2):
            rows = slice(a * sub, (a + 1) * sub)
            l = accs[a][:, 64:65]
            o_ref[rows, :] = jnp.where(lane < 64, accs[a] / l, 0.0)
            q2_ref[rows, :] = put3(qs[a], L_LSE, -(ms[a] + jnp.log(l)))

    blk = pl.BlockSpec((None, tb, LANE), lambda h, i, js: (h, i, 0))
    full = pl.BlockSpec((None, T, LANE), lambda h, i, js: (h, 0, 0))
    return pl.pallas_call(
        body, name="fox_fwd",
        grid_spec=pltpu.PrefetchScalarGridSpec(
            num_scalar_prefetch=1, grid=(H, T // tb), in_specs=[blk, full, full], out_specs=[blk, blk],
            scratch_shapes=[pltpu.VMEM((2, sub, LANE), F32), pltpu.VMEM((2, sub, LANE), F32)]),
        out_shape=[jax.ShapeDtypeStruct((H, T, LANE), F32), jax.ShapeDtypeStruct((H, T, LANE), BF)],
        compiler_params=_params(("parallel", "arbitrary"), VMEM_BIG),
    )(jstart, q, k, v)


def _merge_out(u_r, o_fox, z_a, x, g_ffn, w_ro, w_fo, w_out, tm=256):
    T = x.shape[0]

    def body(u_ref, of_ref, ar_ref, af_ref, x_ref, g_ref, wro_ref, wfo_ref, wout_ref,
             yr_ref, yf_ref, m_ref, x2_ref, h2_ref, oc_ref):
        u = u_ref[...]
        oc = jnp.concatenate([of_ref[h][:, :FOX_D] for h in range(FOX_H)], axis=-1).astype(BF)
        oc_ref[...] = oc
        yr = jnp.concatenate([_nn(u, wro_ref[k]) for k in range(N_CHIP)], axis=-1)
        yf = jnp.concatenate([_nn(oc, wfo_ref[k]) for k in range(N_CHIP)], axis=-1)
        yr_ref[...] = yr
        yf_ref[...] = yf
        m = (_sigmoid(ar_ref[...].astype(F32)) * yr + _sigmoid(af_ref[...].astype(F32)) * yf).astype(BF)
        m_ref[...] = m
        x2 = x_ref[...]
        for k in range(N_CHIP):
            x2 = x2 + _nn(m[:, 256 * k:256 * k + 256], wout_ref[k])
        x2_ref[...] = x2
        r = lax.rsqrt(jnp.mean(x2 * x2, axis=-1, keepdims=True) + EPS)
        h2_ref[...] = (x2 * r * g_ref[...]).astype(BF)

    row = lambda w: pl.BlockSpec((tm, w), lambda i: (i, 0))
    const = lambda shp: pl.BlockSpec(shp, lambda i: (0,) * len(shp))
    return pl.pallas_call(
        body, name="merge_out", grid=(T // tm,),
        in_specs=[row(512), pl.BlockSpec((FOX_H, tm, LANE), lambda i: (0, i, 0)),
                  pl.BlockSpec((tm, 1024), lambda i: (i, 3)), pl.BlockSpec((tm, 1024), lambda i: (i, 4)),
                  row(1024), const((1, 1024)), const((N_CHIP, 512, 256)), const((N_CHIP, 512, 256)),
                  const((N_CHIP, 256, 1024))],
        out_specs=[row(1024), row(1024), row(1024), row(1024), row(1024), row(512)],
        out_shape=[jax.ShapeDtypeStruct((T, 1024), F32), jax.ShapeDtypeStruct((T, 1024), F32),
                   jax.ShapeDtypeStruct((T, 1024), BF), jax.ShapeDtypeStruct((T, 1024), F32),
                   jax.ShapeDtypeStruct((T, 1024), BF), jax.ShapeDtypeStruct((T, 512), BF)],
        compiler_params=_params(("parallel",), VMEM_BIG),
    )(u_r, o_fox, z_a, z_a, x, g_ffn, w_ro, w_fo, w_out)


def _load_resident(hbm_refs, vmem_refs, sem):
    cps = [pltpu.make_async_copy(h, v, sem.at[i]) for i, (h, v) in enumerate(zip(hbm_refs, vmem_refs))]
    for cp in cps:
        cp.start()
    for cp in cps:
        cp.wait()


def _ffn_fwd(h2, x2, tgt, w_gate, w_up, w_down, tm=FFN_TM):
    T = h2.shape[0]

    def body(h_ref, x2_ref, t_ref, wg_hbm, wu_hbm, wd_hbm, a_ref, b_ref, act_ref, dy_ref, ls_ref, wg, wu, wd, sem):
        @pl.when(pl.program_id(0) == 0)
        def _():
            _load_resident((wg_hbm, wu_hbm, wd_hbm), (wg, wu, wd), sem)
            ls_ref[...] = jnp.zeros(ls_ref.shape, F32)

        h = h_ref[...]
        err = x2_ref[...] - t_ref[...]
        for k in range(N_CHIP):
            gp = _nt(h, wg[k])
            up = _nt(h, wu[k])
            sg = _sigmoid(gp)
            silu = gp * sg
            a_ref[k] = silu.astype(BF)
            b_ref[k] = (up * sg * (1.0 + gp * (1.0 - sg))).astype(BF)
            act = (silu * up).astype(BF)
            act_ref[k] = act
            err = err + _nn(act, wd[k])
        dy_ref[...] = err * (1.0 / D_MODEL)
        ls_ref[...] += jnp.sum(err * err, axis=0, keepdims=True)

    row = pl.BlockSpec((tm, D_MODEL), lambda i: (i, 0))
    hid = pl.BlockSpec((N_CHIP, tm, FF_SH), lambda i: (0, i, 0))
    anyspec = pl.BlockSpec(memory_space=pl.ANY)
    wshape = pltpu.VMEM((N_CHIP, FF_SH, D_MODEL), BF)
    return pl.pallas_call(
        body, name="ffn_fwd", grid=(T // tm,),
        in_specs=[row, row, row, anyspec, anyspec, anyspec],
        out_specs=[hid, hid, hid, row, pl.BlockSpec((1, D_MODEL), lambda i: (0, 0))],
        out_shape=[jax.ShapeDtypeStruct((N_CHIP, T, FF_SH), BF)] * 3
        + [jax.ShapeDtypeStruct((T, D_MODEL), F32), jax.ShapeDtypeStruct((1, D_MODEL), F32)],
        scratch_shapes=[wshape, wshape, wshape, pltpu.SemaphoreType.DMA((3,))],
        compiler_params=_params(("arbitrary",), VMEM_HUGE),
    )(h2, x2, tgt, w_gate, w_up, w_down)


def _ffn_bwd(dy, sa, sb, x2, g_ffn, w_gate, w_up, w_down, tm=FFN_TM):
    T = dy.shape[0]

    def body(dy_ref, a_ref, b_ref, x2_ref, g_ref, wg_hbm, wu_hbm, wd_hbm, dgp_ref, dup_ref, dx_ref, dg_ref,
             wg, wu, wd, sem):
        @pl.when(pl.program_id(0) == 0)
        def _():
            _load_resident((wg_hbm, wu_hbm, wd_hbm), (wg, wu, wd), sem)
            dg_ref[...] = jnp.zeros(dg_ref.shape, F32)

        dy = dy_ref[...]
        dyb = dy.astype(BF)
        dh = jnp.zeros((tm, D_MODEL), F32)
        for k in range(N_CHIP):
            dact = _nt(dyb, wd[k])
            dup = (dact * a_ref[k]).astype(BF)
            dgp = (dact * b_ref[k]).astype(BF)
            dgp_ref[k] = dgp
            dup_ref[k] = dup
            dh = dh + _nn(dgp, wg[k]) + _nn(dup, wu[k])
        x2 = x2_ref[...]
        r = lax.rsqrt(jnp.mean(x2 * x2, axis=-1, keepdims=True) + EPS)
        xn = x2 * r
        dg_ref[...] += jnp.sum(dh * xn, axis=0, keepdims=True)
        dxn = dh * g_ref[...]
        dx_ref[...] = dy + r * (dxn - xn * jnp.mean(dxn * xn, axis=-1, keepdims=True))

    row = pl.BlockSpec((tm, D_MODEL), lambda i: (i, 0))
    hid = pl.BlockSpec((N_CHIP, tm, FF_SH), lambda i: (0, i, 0))
    vec = pl.BlockSpec((1, D_MODEL), lambda i: (0, 0))
    anyspec = pl.BlockSpec(memory_space=pl.ANY)
    wshape = pltpu.VMEM((N_CHIP, FF_SH, D_MODEL), BF)
    return pl.pallas_call(
        body, name="ffn_bwd", grid=(T // tm,),
        in_specs=[row, hid, hid, row, vec, anyspec, anyspec, anyspec],
        out_specs=[hid, hid, row, vec],
        out_shape=[jax.ShapeDtypeStruct((N_CHIP, T, FF_SH), BF), jax.ShapeDtypeStruct((N_CHIP, T, FF_SH), BF),
                   jax.ShapeDtypeStruct((T, D_MODEL), F32), jax.ShapeDtypeStruct((1, D_MODEL), F32)],
        scratch_shapes=[wshape, wshape, wshape, pltpu.SemaphoreType.DMA((3,))],
        compiler_params=_params(("arbitrary",), VMEM_HUGE),
    )(dy, sa, sb, x2, g_ffn, w_gate, w_up, w_down)


def _out_bwd(dx2, z_a, y_r, y_f, o_raw, o_fox, g_ret, w_ro, w_fo, w_out, tm=256, push=None):
    T = dx2.shape[0]

    def body(dx_ref, gt_ref, ar_ref, af_ref, yr_ref, yf_ref, o_ref, of_ref, g_ref, wro_ref, wfo_ref, wout_ref,
             dyr_ref, dyf_ref, dgt_ref, da_ref, do_ref, dof_ref, dg_ref):
        i = pl.program_id(0)

        @pl.when(i == 0)
        def _():
            dg_ref[...] = jnp.zeros(dg_ref.shape, F32)

        dxb = dx_ref[...].astype(BF)
        dm = jnp.concatenate([_nt(dxb, wout_ref[k]) for k in range(N_CHIP)], axis=-1)
        sr, sf = _sigmoid(ar_ref[...].astype(F32)), _sigmoid(af_ref[...].astype(F32))
        dyr = dm * sr
        dyf = dm * sf
        da_ref[:, :1024] = (dyr * yr_ref[...] * (1.0 - sr)).astype(BF)
        da_ref[:, 1024:] = (dyf * yf_ref[...] * (1.0 - sf)).astype(BF)
        dyr = dyr.astype(BF)
        dyf = dyf.astype(BF)
        dyr_ref[...] = dyr
        dyf_ref[...] = dyf
        du = jnp.zeros((tm, 512), F32)
        doc = jnp.zeros((tm, 512), F32)
        for k in range(N_CHIP):
            du = du + _nt(dyr[:, 256 * k:256 * k + 256], wro_ref[k])
            doc = doc + _nt(dyf[:, 256 * k:256 * k + 256], wfo_ref[k])

        for h in range(RET_H):
            cols = slice(h * RET_DV, (h + 1) * RET_DV)
            o = o_ref[:, cols]
            mu = jnp.mean(o, axis=-1, keepdims=True)
            xc = o - mu
            rstd = lax.rsqrt(jnp.mean(xc * xc, axis=-1, keepdims=True) + EPS)
            on = xc * rstd
            g = g_ref[:, cols]
            gt = gt_ref[:, cols].astype(F32)
            sg = _sigmoid(gt)
            duh = du[:, cols]
            dgt_ref[:, cols] = (duh * (on * g) * sg * (1.0 + gt * (1.0 - sg))).astype(BF)
            dog = duh * gt * sg
            dg_ref[:, cols] += jnp.sum(dog * on, axis=0, keepdims=True)
            don = dog * g
            do_ref[:, cols] = rstd * (don - jnp.mean(don, axis=-1, keepdims=True)
                                      - on * jnp.mean(don * on, axis=-1, keepdims=True))

        lane = lax.broadcasted_iota(jnp.int32, (tm, LANE), 1)
        zpad = jnp.zeros((tm, 64), F32)
        for h in range(FOX_H):
            doh = doc[:, 64 * h:64 * h + 64]
            delta = jnp.sum(doh * of_ref[h][:, :FOX_D], axis=-1, keepdims=True)
            hi, mid, lo = [t.astype(F32) for t in _split3(-delta)]
            da = jnp.concatenate([doh, zpad], axis=-1)
            da = jnp.where(lane == 64, hi, jnp.where(lane == 65, mid, jnp.where(lane == 66, lo, da)))
            dof_ref[h] = da.astype(BF)

    row = lambda w: pl.BlockSpec((tm, w), lambda i: (i, 0))
    const = lambda shp: pl.BlockSpec(shp, lambda i: (0,) * len(shp))
    hsp = pl.BlockSpec((FOX_H, tm, LANE), lambda i: (0, i, 0))
    return _hosted_call(
        body, "out_bwd", (T // tm,),
        [row(1024), pl.BlockSpec((tm, 512), lambda i: (i, 2)), pl.BlockSpec((tm, 1024), lambda i: (i, 3)),
         pl.BlockSpec((tm, 1024), lambda i: (i, 4)), row(1024), row(1024), row(512), hsp,
         const((1, 512)), const((N_CHIP, 512, 256)), const((N_CHIP, 512, 256)), const((N_CHIP, 256, 1024))],
        [row(1024), row(1024), row(512), row(2048), row(512), hsp, const((1, 512))],
        [jax.ShapeDtypeStruct((T, 1024), BF), jax.ShapeDtypeStruct((T, 1024), BF),
         jax.ShapeDtypeStruct((T, 512), BF), jax.ShapeDtypeStruct((T, 2048), BF),
         jax.ShapeDtypeStruct((T, 512), F32), jax.ShapeDtypeStruct((FOX_H, T, LANE), BF),
         jax.ShapeDtypeStruct((1, 512), F32)],
        [], VMEM_BIG, (dx2, z_a, z_a, z_a, y_r, y_f, o_raw, o_fox, g_ret, w_ro, w_fo, w_out), push)


def _ret_bwd(d_o, qr, kr, z_a, states, cos_t, sin_t, consts, tt=512, push=None):
    T = z_a.shape[0]
    nt = T // tt
    nch = tt // CHUNK
    decay, zeta, xi, gcb = consts

    def body(do_ref, q_ref, k_ref, v_ref, st_ref, cos_ref, sin_ref, d_ref, ze_ref, xi_ref, gc_ref, dz_ref, g_sc):
        i = pl.program_id(0)

        @pl.when(i == 0)
        def _():
            g_sc[...] = jnp.zeros(g_sc.shape, F32)

        for c in reversed(range(nch)):
            rows = slice(c * CHUNK, (c + 1) * CHUNK)
            cosv, sinv = cos_ref[rows, :], sin_ref[rows, :]
            dq_parts, dk_parts = [], []
            for h in range(RET_H):
                cols = slice(h * RET_DV, (h + 1) * RET_DV)
                q, k = q_ref[h, rows, :], k_ref[h, rows, :]
                v32 = v_ref[rows, cols].astype(F32)
                vb = v32.astype(BF)
                r = st_ref[h, rows, :]
                g = g_sc[h]
                gb = g.astype(BF)
                d_o = do_ref[rows, cols]
                dob = d_o.astype(BF)
                dox = (d_o * xi_ref[h]).astype(BF)
                dec = d_ref[h]
                s = (_nt(q, k) * dec).astype(BF)
                ds = (_nt(dob, vb) * dec).astype(BF)
                dv = _tn(s, dob) + ze_ref[h] * _nn(k, gb)
                dq = _nn(ds, k) + _nt(dox, r.astype(BF))
                dk = _tn(ds, q) + _nt((v32 * ze_ref[h]).astype(BF), gb)
                g_sc[h] = gc_ref[h] * g + _tn(q, dox)
                dq_parts.append((dq * cosv - _swap32(dq) * sinv)[:, :64])
                dk_parts.append(((dk * cosv - _swap32(dk) * sinv) * 0.125)[:, :64])
                dz_ref[rows, 512 + h * RET_DV:512 + (h + 1) * RET_DV] = dv.astype(BF)
            dz_ref[rows, 0:256] = jnp.concatenate(dq_parts, axis=-1).astype(BF)
            dz_ref[rows, 256:512] = jnp.concatenate(dk_parts, axis=-1).astype(BF)

    rev = lambda i: nt - 1 - i
    hspec = pl.BlockSpec((RET_H, tt, LANE), lambda i: (0, rev(i), 0))
    cspec = pl.BlockSpec((RET_H, CHUNK, LANE), lambda i: (0, 0, 0))
    tab = pl.BlockSpec((tt, LANE), lambda i: (rev(i), 0))
    (dz,), lands = _hosted_call(
        body, "ret_bwd", (nt,),
        [pl.BlockSpec((tt, 512), lambda i: (rev(i), 0)), hspec, hspec,
         pl.BlockSpec((tt, 512), lambda i: (rev(i), 1)), hspec, tab, tab, cspec, cspec, cspec, cspec],
        [pl.BlockSpec((tt, 1024), lambda i: (rev(i), 0))], [jax.ShapeDtypeStruct((T, 1024), BF)],
        [pltpu.VMEM((RET_H, CHUNK, LANE), F32)], VMEM_BIG,
        (d_o, qr, kr, z_a, states, cos_t, sin_t, decay, zeta, xi, gcb), push)
    return dz, lands


def _fox_bwd(iend, q2, k, v, do, sub=FOX_SUB):
    H, T, _ = k.shape
    tb = 2 * sub

    def body(ie_ref, q_ref, do_ref, k_ref, v_ref, dq_ref, dk_ref, dv_ref, dk_sc, dv_sc):
        j = pl.program_id(1)
        n = ie_ref[pl.program_id(0), j]

        @pl.when(j == 0)
        def _():
            dq_ref[...] = jnp.zeros(dq_ref.shape, F32)

        kk, vv = k_ref[...], v_ref[...]
        dk_sc[...] = jnp.zeros(dk_sc.shape, F32)
        dv_sc[...] = jnp.zeros(dv_sc.shape, F32)
        krow = lax.broadcasted_iota(jnp.int32, (tb, sub), 0)
        qcol = lax.broadcasted_iota(jnp.int32, (tb, sub), 1)

        def step(i, shift):
            off = pl.multiple_of(i * sub, sub)
            qq = q_ref[pl.ds(off, sub), :]
            dd = do_ref[pl.ds(off, sub), :]
            p = jnp.exp(_nt(kk, qq))
            if shift is not None:
                p = jnp.where(qcol + shift >= krow, p, 0.0)
            ds = (p * _nt(vv, dd)).astype(BF)
            dv_sc[...] += _nn(p.astype(BF), dd)
            dk_sc[...] += _nn(ds, qq)
            dq_ref[pl.ds(off, sub), :] += _tn(ds, kk)

        off0 = pl.multiple_of(2 * j * sub, sub)
        q0, d0 = q_ref[pl.ds(off0, sub), :], do_ref[pl.ds(off0, sub), :]
        k0, v0 = k_ref[0:sub, :], v_ref[0:sub, :]
        p0 = jnp.where(qcol[0:sub, :] >= krow[0:sub, :], jnp.exp(_nt(k0, q0)), 0.0)
        ds0 = (p0 * _nt(v0, d0)).astype(BF)
        dv_sc[0:sub, :] += _nn(p0.astype(BF), d0)
        dk_sc[0:sub, :] += _nn(ds0, q0)
        dq_ref[pl.ds(off0, sub), :] += _tn(ds0, k0)
        step(2 * j + 1, sub)

        def loop_body(i, carry):
            step(i, None)
            return carry

        lax.fori_loop(2 * j + 2, n, loop_body, 0)
        dk_ref[...] = dk_sc[...]
        dv_ref[...] = dv_sc[...]

    blk = pl.BlockSpec((None, tb, LANE), lambda h, j, ie: (h, j, 0))
    full = pl.BlockSpec((None, T, LANE), lambda h, j, ie: (h, 0, 0))
    shp = jax.ShapeDtypeStruct((H, T, LANE), F32)
    return pl.pallas_call(
        body, name="fox_bwd",
        grid_spec=pltpu.PrefetchScalarGridSpec(
            num_scalar_prefetch=1, grid=(H, T // tb), in_specs=[full, full, blk, blk], out_specs=[full, blk, blk],
            scratch_shapes=[pltpu.VMEM((tb, LANE), F32), pltpu.VMEM((tb, LANE), F32)]),
        out_shape=[shp, shp, shp],
        compiler_params=_params(("arbitrary", "arbitrary"), VMEM_BIG),
    )(iend, q2, do, k, v)


def _fox_post_bwd(dq, dk, dv, z_a, z_ff, b_f, g_q, g_k, tm=256, push=None):
    T = z_a.shape[0]
    nt = T // tm

    def body(dq_ref, dk_ref, dv_ref, zf_ref, zff_ref, b_ref, g_ref, sc_ref, seg_ref, segt_ref,
             dz_ref, dff_ref, dg_ref, db_ref, carry):
        i = pl.program_id(0)

        @pl.when(i == 0)
        def _():
            carry[...] = jnp.zeros(carry.shape, F32)
            dg_ref[...] = jnp.zeros(dg_ref.shape, F32)
            db_ref[...] = jnp.zeros(db_ref.shape, F32)

        lane = lax.broadcasted_iota(jnp.int32, (tm, LANE), 1)
        dcm = jnp.zeros((tm, LANE), F32)
        for h in range(FOX_H):
            dcm = jnp.where(lane == h, dq_ref[h][:, L_CQ:L_CQ + 1] - dk_ref[h][:, L_CK:L_CK + 1], dcm)

        def seg_mean(v):
            return sum(_nn(t, seg_ref[...]) for t in _split3(v)) * (1.0 / FOX_D)

        def seg_bcast(v):
            return sum(_nn(t, segt_ref[...]) for t in _split3(v))

        x = zf_ref[:, :1024].astype(F32)
        dy = jnp.concatenate([dq_ref[h][:, :FOX_D] for h in range(FOX_H)]
                             + [dk_ref[h][:, :FOX_D] for h in range(FOX_H)], axis=-1) * sc_ref[...]
        rb = seg_bcast(lax.rsqrt(seg_mean(x * x) + EPS))
        xn = x * rb
        dg_ref[...] += jnp.sum(dy * xn, axis=0, keepdims=True)
        dxn = dy * g_ref[...]
        dz_ref[:, :1024] = (rb * (dxn - xn * seg_bcast(seg_mean(dxn * xn)))).astype(BF)
        dz_ref[:, 1024:] = jnp.concatenate([dv_ref[h][:, :FOX_D] for h in range(FOX_H)], axis=-1).astype(BF)

        row = lax.broadcasted_iota(jnp.int32, (tm, tm), 0)
        col = lax.broadcasted_iota(jnp.int32, (tm, tm), 1)
        tri = (row <= col).astype(BF)
        hi, mid, lo = _split3(dcm)
        dlogf = _nn(tri, hi) + _nn(tri, mid) + _nn(tri, lo) + carry[...]
        carry[...] = dlogf[0:1, :]
        dff = jnp.where(lane < FOX_H, dlogf * _sigmoid(-(zff_ref[...] + b_ref[...])), 0.0)
        dff_ref[...] = dff.astype(BF)
        db_ref[...] += jnp.sum(dff, axis=0, keepdims=True)

    rev = lambda i: nt - 1 - i
    hsp = pl.BlockSpec((FOX_H, tm, LANE), lambda i: (0, rev(i), 0))
    const = lambda r, w: pl.BlockSpec((r, w), lambda i: (0, 0))
    seg = _segment_matrix()
    g_all = jnp.concatenate([jnp.tile(g_q, (1, FOX_H)), jnp.tile(g_k, (1, FOX_H))], axis=1)
    scale = jnp.asarray(np.concatenate([np.full((1, 512), 0.125, np.float32), np.ones((1, 512), np.float32)], axis=1))
    (dz, dff, dg, db), lands = _hosted_call(
        body, "fox_post_bwd", (nt,),
        [hsp, hsp, hsp, pl.BlockSpec((tm, 1536), lambda i: (rev(i), 1)),
         pl.BlockSpec((tm, LANE), lambda i: (rev(i), 0)), const(1, LANE), const(1, 1024), const(1, 1024),
         const(1024, LANE), const(LANE, 1024)],
        [pl.BlockSpec((tm, 1536), lambda i: (rev(i), 0)), pl.BlockSpec((tm, LANE), lambda i: (rev(i), 0)),
         const(1, 1024), const(1, LANE)],
        [jax.ShapeDtypeStruct((T, 1536), BF), jax.ShapeDtypeStruct((T, LANE), BF),
         jax.ShapeDtypeStruct((1, 1024), F32), jax.ShapeDtypeStruct((1, LANE), F32)],
        [pltpu.VMEM((1, LANE), F32)], VMEM_BIG, (dq, dk, dv, z_a, z_ff, b_f, g_all, scale, seg, seg.T), push)
    dg_heads = dg.reshape(2, FOX_H, FOX_D).sum(axis=1)
    return (dz, dff, dg_heads[0:1], dg_heads[1:2], db), lands


def _in_bwd(dz_ret, dz_gt, dz_fox, dz_a, dz_ff, w_a, w_ff, x, g_mix, dx2, tm=256, push=None):
    T = x.shape[0]

    def body(r_ref, t_ref, f_ref, a_ref, ff_ref, wa_ref, wf_ref, x_ref, g_ref, dx2_ref, dx_ref, dg_ref):
        i = pl.program_id(0)

        @pl.when(i == 0)
        def _():
            dg_ref[...] = jnp.zeros(dg_ref.shape, F32)

        dh = (_nt(r_ref[...], wa_ref[:, C_RET:C_GT]) + _nt(t_ref[...], wa_ref[:, C_GT:C_FOX])
              + _nt(f_ref[...], wa_ref[:, C_FOX:C_A]) + _nt(a_ref[...], wa_ref[:, C_A:C_END])
              + _nt(ff_ref[...], wf_ref[...]))
        xv = x_ref[...]
        r = lax.rsqrt(jnp.mean(xv * xv, axis=-1, keepdims=True) + EPS)
        xn = xv * r
        dg_ref[...] += jnp.sum(dh * xn, axis=0, keepdims=True)
        dxn = dh * g_ref[...]
        dx_ref[...] = dx2_ref[...] + r * (dxn - xn * jnp.mean(dxn * xn, axis=-1, keepdims=True))

    row = lambda w: pl.BlockSpec((tm, w), lambda i: (i, 0))
    const = lambda shp: pl.BlockSpec(shp, lambda i: (0,) * len(shp))
    return _hosted_call(
        body, "in_bwd", (T // tm,),
        [row(1024), row(512), row(1536), row(2048), row(LANE), const((D_MODEL, C_END)),
         const((D_MODEL, LANE)), row(1024), const((1, 1024)), row(1024)],
        [row(1024), const((1, 1024))],
        [jax.ShapeDtypeStruct((T, 1024), F32), jax.ShapeDtypeStruct((1, 1024), F32)],
        [], VMEM_BIG, (dz_ret, dz_gt, dz_fox, dz_a, dz_ff, w_a, w_ff, x, g_mix, dx2), push)


def _mesh_pos():
    return lax.axis_index("x"), lax.axis_index("y"), lax.axis_index("c")


def _staged_place(src, name):
    stacked = src.ndim == 3
    R, C = src.shape[-2:]
    tr = _row_tile(R, 128, 16)
    n = R // tr
    assert n >= 2

    def body(s_ref, o_ref, buf, sem):
        i = pl.program_id(0)
        slot = i % 2
        x, y, _ = _mesh_pos()
        kme = 2 * x + y

        def out_copy(s, step):
            return pltpu.make_async_copy(buf.at[s], o_ref.at[kme, pl.ds(pl.multiple_of(step * tr, tr), tr), :], sem.at[s])

        @pl.when(i >= 2)
        def _():
            out_copy(slot, i - 2).wait()

        buf[slot] = (s_ref[kme] if stacked else s_ref[...]).astype(BF)
        out_copy(slot, i).start()

        @pl.when(i == n - 1)
        def _():
            out_copy(1 - slot, i - 1).wait()
            out_copy(slot, i).wait()

    in_spec = (pl.BlockSpec((N_CHIP, tr, C), lambda i: (0, i, 0)) if stacked else pl.BlockSpec((tr, C), lambda i: (i, 0)))
    return pl.pallas_call(
        body, name=name, grid=(n,), in_specs=[in_spec], out_specs=pl.BlockSpec(memory_space=pl.ANY),
        out_shape=jax.ShapeDtypeStruct((N_CHIP, R, C), BF),
        scratch_shapes=[pltpu.VMEM((2, tr, C), BF), pltpu.SemaphoreType.DMA((2,))],
        compiler_params=_params(("arbitrary",)),
    )(src)


def _push_copies(src, land, send_sem, recv_sem, receiving):
    x, y, c = _mesh_pos()
    kme = 2 * x + y
    cps = []
    for w in range(len(land)):
        for j, (px, py) in enumerate([(1 - x, y), (x, 1 - y), (1 - x, 1 - y)]):
            kpeer = 2 * px + py
            cps.append(pltpu.make_async_remote_copy(
                src_ref=land[w].at[kme] if src is None else src[w].at[kpeer],
                dst_ref=land[w].at[kpeer if receiving else kme],
                send_sem=send_sem.at[3 * w + j], recv_sem=recv_sem.at[3 * w + j],
                device_id=(px, py, c), device_id_type=MESH))
    return cps


def _gather_two_level(stack, name):
    _, R, C = stack.shape
    hr = R // 2

    def body(_, land, send_sem, recv_sem):
        x, y, c = _mesh_pos()
        kme = 2 * x + y
        chips = [(1 - x, y), (x, 1 - y), (1 - x, 1 - y)]

        def rows(k, core):
            return land.at[k, pl.ds(pl.multiple_of(core * hr, hr), hr), :]

        def copy(idx, k, core, to):
            return pltpu.make_async_remote_copy(src_ref=rows(k, core), dst_ref=rows(k, core), send_sem=send_sem.at[idx],
                                                recv_sem=recv_sem.at[idx], device_id=to, device_id_type=MESH)

        first = [copy(j, kme, c, (px, py, c)) for j, (px, py) in enumerate(chips)]
        for cp in first:
            cp.start()
        passed = [copy(3 + j, 2 * px + py, c, (x, y, 1 - c)) for j, (px, py) in enumerate(chips)]
        for j, (px, py) in enumerate(chips):
            copy(j, 2 * px + py, c, (px, py, c)).wait_recv()
            passed[j].start()
        for j, (px, py) in enumerate(chips):
            copy(3 + j, 2 * px + py, 1 - c, (x, y, 1 - c)).wait_recv()
        for cp in first + passed:
            cp.wait_send()

    anyspec = pl.BlockSpec(memory_space=pl.ANY)
    return pl.pallas_call(
        body, name=name, in_specs=[anyspec], out_specs=anyspec,
        out_shape=jax.ShapeDtypeStruct(stack.shape, stack.dtype), input_output_aliases={0: 0},
        scratch_shapes=[pltpu.SemaphoreType.DMA((6,)), pltpu.SemaphoreType.DMA((6,))],
    )(stack)


def _gather_small(small):
    def body(sv, svo, ssend, srecv, sloc):
        x, y, c = _mesh_pos()
        me = 4 * x + 2 * y + c
        flips = [(b >> 2 & 1, b >> 1 & 1, b & 1) for b in range(1, 8)]
        others = [(1 - x if fx else x, 1 - y if fy else y, 1 - c if fc else c) for fx, fy, fc in flips]
        local = pltpu.make_async_copy(sv, svo.at[me], sloc)
        local.start()
        sends = []
        for j, (px, py, pc) in enumerate(others):
            cp = pltpu.make_async_remote_copy(
                src_ref=sv, dst_ref=svo.at[me], send_sem=ssend.at[j], recv_sem=srecv.at[j],
                device_id=(px, py, pc), device_id_type=MESH)
            cp.start()
            sends.append(cp)
        for j, (px, py, pc) in enumerate(others):
            pltpu.make_async_remote_copy(
                src_ref=sv, dst_ref=svo.at[4 * px + 2 * py + pc], send_sem=ssend.at[j], recv_sem=srecv.at[j],
                device_id=(px, py, pc), device_id_type=MESH).wait_recv()
        for cp in sends:
            cp.wait_send()
        local.wait()

    anyspec = pl.BlockSpec(memory_space=pl.ANY)
    return pl.pallas_call(
        body, name="gather_small", in_specs=[anyspec], out_specs=anyspec,
        out_shape=jax.ShapeDtypeStruct((8,) + small.shape, small.dtype),
        scratch_shapes=[pltpu.SemaphoreType.DMA((7,)), pltpu.SemaphoreType.DMA((7,)), pltpu.SemaphoreType.DMA],
    )(small)


def _sibling_exchange(arrs):
    n = len(arrs)

    def body(*refs):
        ins, outs = refs[:n], refs[n:2 * n]
        send_sems, recv_sems = refs[2 * n:]
        x, y, c = _mesh_pos()
        cps = [pltpu.make_async_remote_copy(
            src_ref=ins[w], dst_ref=outs[w], send_sem=send_sems.at[w], recv_sem=recv_sems.at[w],
            device_id=(x, y, 1 - c), device_id_type=MESH) for w in range(n)]
        for cp in cps:
            cp.start()
        for cp in cps:
            cp.wait_recv()
        for cp in cps:
            cp.wait_send()

    anyspec = pl.BlockSpec(memory_space=pl.ANY)
    return pl.pallas_call(
        body, name="sibling_exchange",
        in_specs=[anyspec] * n, out_specs=[anyspec] * n,
        out_shape=[jax.ShapeDtypeStruct(a.shape, a.dtype) for a in arrs],
        scratch_shapes=[pltpu.SemaphoreType.DMA((n,)), pltpu.SemaphoreType.DMA((n,))],
    )(*arrs)


def _sum_stack(own, recv, name):
    _, R, C = recv.shape
    tr = _row_tile(R, 256, 16)

    def body(g_ref, r_ref, o_ref):
        x, y, _ = _mesh_pos()
        kme = 2 * x + y
        acc = g_ref[kme].astype(F32)
        for d in range(1, N_CHIP):
            acc = acc + r_ref[(kme + d) % N_CHIP].astype(F32)
        o_ref[...] = acc

    spec = pl.BlockSpec((N_CHIP, tr, C), lambda i: (0, i, 0))
    return pl.pallas_call(
        body, name=name, grid=(R // tr,), in_specs=[spec, spec],
        out_specs=pl.BlockSpec((tr, C), lambda i: (i, 0)),
        out_shape=jax.ShapeDtypeStruct((R, C), F32),
        compiler_params=_params(("parallel",)),
    )(own, recv)


def _adam_math(w, g, m, v):
    m2 = ADAM_B1 * m + (1.0 - ADAM_B1) * g
    v2 = ADAM_B2 * v + (1.0 - ADAM_B2) * (g * g)
    m_hat = m2 / (1.0 - ADAM_B1 ** ADAM_STEP)
    v_hat = v2 / (1.0 - ADAM_B2 ** ADAM_STEP)
    delta = -ADAM_LR * (m_hat / (jnp.sqrt(v_hat) + ADAM_EPS) + ADAM_WD * w)
    return delta, m2, v2


def _adamw(w, m, v, s0, s1, name):
    R, C = w.shape
    tr = _row_tile(R, 128, 8)

    def body(w_ref, m_ref, v_ref, a_ref, b_ref, g_ref, d_ref, m2_ref, v2_ref):
        g = a_ref[...] + b_ref[...]
        delta, m2, v2 = _adam_math(w_ref[...], g, m_ref[...], v_ref[...])
        g_ref[...] = g
        d_ref[...] = delta
        m2_ref[...] = m2
        v2_ref[...] = v2

    spec = pl.BlockSpec((tr, C), lambda i: (i, 0))
    shp = jax.ShapeDtypeStruct((R, C), F32)
    return pl.pallas_call(
        body, name=name, grid=(R // tr,), in_specs=[spec] * 5, out_specs=[spec] * 4, out_shape=[shp] * 4,
        compiler_params=_params(("parallel",), VMEM_BIG),
    )(w, m, v, s0, s1)


def _adamw_small(w, m, v, gathered):
    def body(w_ref, m_ref, v_ref, s_ref, g_ref, d_ref, m2_ref, v2_ref):
        g = s_ref[0]
        for d in range(1, 8):
            g = g + s_ref[d]
        delta, m2, v2 = _adam_math(w_ref[...], g, m_ref[...], v_ref[...])
        g_ref[...] = g
        d_ref[...] = delta
        m2_ref[...] = m2
        v2_ref[...] = v2

    shp = jax.ShapeDtypeStruct(w.shape, F32)
    return pl.pallas_call(body, name="adamw_small", out_shape=[shp] * 4)(w, m, v, gathered)


SMALL = (("g_mix", 1024), ("g_ffn", 1024), ("g_ret_norm", 512), ("g_fox_q", 64), ("g_fox_k", 64), ("b_forget", 8))
SMALL_W = 3072


def _pack_small(parts):
    cols = []
    for (name, n) in SMALL:
        p = parts[name].reshape(1, -1)[:, :n]
        pad = -n % LANE
        cols.append(jnp.pad(p, ((0, 0), (0, pad))) if pad else p)
    used = sum(c.shape[1] for c in cols)
    cols.append(jnp.zeros((1, SMALL_W - used), F32))
    return jnp.concatenate(cols, axis=1)


def _unpack_small(vec):
    out, off = {}, 0
    for (name, n) in SMALL:
        out[name] = vec[:, off:off + n]
        off += n + (-n % LANE)
    return out


def kernel(x, g_mix, w_in, b_forget, g_ret_norm, w_ret_o, g_fox_q, g_fox_k, w_fox_o, w_out, g_ffn, w_gate, w_up, w_down, loss_target, m_g_mix, m_w_in, m_b_forget, m_g_ret_norm, m_w_ret_o, m_g_fox_q, m_g_fox_k, m_w_fox_o, m_w_out, m_g_ffn, m_w_gate, m_w_up, m_w_down, v_g_mix, v_w_in, v_b_forget, v_g_ret_norm, v_w_ret_o, v_g_fox_q, v_g_fox_k, v_w_fox_o, v_w_out, v_g_ffn, v_w_gate, v_w_up, v_w_down):
    T = x.shape[1]
    xs = x[0]
    tgt = loss_target[0]
    big_names = ("w_in", "w_ret_o", "w_fox_o", "w_out", "w_gate", "w_up", "w_down")
    tr = lambda a: jnp.swapaxes(a[0], 0, 1)
    big_w = dict(w_in=w_in[0], w_ret_o=w_ret_o[0], w_fox_o=w_fox_o[0], w_out=w_out[0], w_gate=tr(w_gate),
                 w_up=tr(w_up), w_down=w_down[0])
    big_m = dict(w_in=m_w_in[0], w_ret_o=m_w_ret_o[0], w_fox_o=m_w_fox_o[0], w_out=m_w_out[0], w_gate=tr(m_w_gate),
                 w_up=tr(m_w_up), w_down=m_w_down[0])
    big_v = dict(w_in=v_w_in[0], w_ret_o=v_w_ret_o[0], w_fox_o=v_w_fox_o[0], w_out=v_w_out[0], w_gate=tr(v_w_gate),
                 w_up=tr(v_w_up), w_down=v_w_down[0])
    small_w = dict(g_mix=g_mix, g_ffn=g_ffn, g_ret_norm=g_ret_norm, g_fox_q=g_fox_q, g_fox_k=g_fox_k, b_forget=b_forget)
    small_m = dict(g_mix=m_g_mix, g_ffn=m_g_ffn, g_ret_norm=m_g_ret_norm, g_fox_q=m_g_fox_q, g_fox_k=m_g_fox_k,
                   b_forget=m_b_forget)
    small_v = dict(g_mix=v_g_mix, g_ffn=v_g_ffn, g_ret_norm=v_g_ret_norm, g_fox_q=v_g_fox_q, g_fox_k=v_g_fox_k,
                   b_forget=v_b_forget)

    stacks = {n: _staged_place(big_w[n], "place_" + n) for n in big_names}
    s_in = _gather_two_level(stacks["w_in"], "gather_w_in")
    w_a, w_ff = _assemble_w_in(s_in)
    b_pad = jnp.pad(b_forget, ((0, 0), (0, LANE - FOX_H)))
    cos_t, sin_t = _rope_tables(T)
    consts = _ret_consts()

    h = _rms_cast(xs, g_mix)
    z_a, (s_gate, s_up) = _mm_nn(h, w_a, "proj_in", BF, push=(None, [stacks["w_gate"], stacks["w_up"]]))
    z_ff, _ = _mm_nn(h, w_ff, "proj_ff", F32)
    (qr, kr, qf, kf, vf, c_cum, nmax), (s_down, s_ro, s_fo, s_out) = _mix_prep(
        z_a, z_ff, cos_t, sin_t, b_pad, g_fox_q, g_fox_k,
        push=(None, [stacks["w_down"], stacks["w_ret_o"], stacks["w_fox_o"], stacks["w_out"]]))
    jstart, iend = _prune_tables(c_cum, nmax, FOX_SUB)
    o_raw, u_r, states = _ret_fwd(qr, kr, z_a, g_ret_norm, consts)
    o_fox, q2 = _fox_fwd(jstart, qf, kf, vf)
    y_r, y_f, mrg, x2, h2, o_cat = _merge_out(u_r, o_fox, z_a, xs, g_ffn, s_ro, s_fo, s_out)
    sa, sb, act, dy, loss_vec = _ffn_fwd(h2, x2, tgt, s_gate, s_up, s_down)
    loss = lax.psum(0.5 / D_MODEL * jnp.sum(loss_vec), ("x", "y", "c"))

    def scatter_job(grads):
        return (grads, [lax.empty(g.shape, g.dtype) for g in grads])

    dgp, dup, dx2, dg_ffn = _ffn_bwd(dy, sa, sb, x2, g_ffn, s_gate, s_up, s_down)
    g_gate, g_up, g_down = (_grad_astack(dgp, h2, "gw_gate"), _grad_astack(dup, h2, "gw_up"),
                            _grad_astack(act, dy, "gw_down"))
    (d_yr, d_yf, dz_gt, dz_a, d_o, do_fox, dg_ret), (r_gate, r_up) = _out_bwd(
        dx2, z_a, y_r, y_f, o_raw, o_fox, g_ret_norm, s_ro, s_fo, s_out,
        push=scatter_job([g_gate, g_up]))
    dz_ret, (r_down,) = _ret_bwd(d_o, qr, kr, z_a, states, cos_t, sin_t, consts, push=scatter_job([g_down]))
    dq_f, dk_f, dv_f = _fox_bwd(iend, q2, kf, vf, do_fox)
    g_mid = [_grad_colstack(u_r, d_yr, "gw_ret_o", 256), _grad_colstack(o_cat, d_yf, "gw_fox_o", 256),
             _grad_plain(mrg, dx2, "gw_out", BF).reshape(N_CHIP, 256, D_MODEL)]
    (dz_fox, dz_ff, dg_q, dg_k, db_f), (r_ro, r_fo, r_out) = _fox_post_bwd(
        dq_f, dk_f, dv_f, z_a, z_ff, b_pad, g_fox_q, g_fox_k, push=scatter_job(g_mid))
    g_in = _pack_g_in(_grad_plain(h, dz_ret, "gw_in_ret", F32), _grad_plain(h, dz_gt, "gw_in_gt", F32),
                      _grad_plain(h, dz_fox, "gw_in_fox", F32, tn=1536),
                      _grad_plain(h, dz_a, "gw_in_a", F32, tk=1024, tn=2048),
                      _grad_plain(h, dz_ff, "gw_in_ff", F32))
    (grad_x, dg_mix), (r_in,) = _in_bwd(dz_ret, dz_gt, dz_fox, dz_a, dz_ff, w_a, w_ff, xs, g_mix, dx2,
                                        push=scatter_job([g_in]))
    small_g = _pack_small(dict(g_mix=dg_mix, g_ffn=dg_ffn, g_ret_norm=dg_ret, g_fox_q=dg_q, g_fox_k=dg_k, b_forget=db_f))

    small_all = _gather_small(small_g)
    sums = [_sum_stack(g, r, "sum_" + n) for g, r, n in zip(
        [g_in] + g_mid + [g_gate, g_up, g_down], [r_in, r_ro, r_fo, r_out, r_gate, r_up, r_down], big_names)]
    sib = _sibling_exchange(sums)
    big_out = {n: _adamw(big_w[n], big_m[n], big_v[n], sums[i], sib[i], "adamw_" + n) for i, n in enumerate(big_names)}
    sg, sd, sm, sv = _adamw_small(_pack_small(small_w), _pack_small(small_m), _pack_small(small_v), small_all)
    small_out = [_unpack_small(t) for t in (sg, sd, sm, sv)]

    order = ("g_mix", "w_in", "b_forget", "g_ret_norm", "w_ret_o", "g_fox_q", "g_fox_k", "w_fox_o", "w_out", "g_ffn",
             "w_gate", "w_up", "w_down")
    outs = [loss, grad_x[None]]
    for idx in range(4):
        for n in order:
            if n in ("w_gate", "w_up"):
                outs.append(jnp.swapaxes(big_out[n][idx], 0, 1)[None])
            else:
                outs.append(big_out[n][idx][None] if n in big_out else small_out[idx][n])
    return tuple(outs)
```

```python
import functools
import math

import numpy as np
import jax
import jax.numpy as jnp
from jax import lax
from jax.experimental import pallas as pl
from jax.experimental.pallas import tpu as pltpu

F32 = jnp.float32
BF = jnp.bfloat16
MESH = pl.DeviceIdType.MESH

D_MODEL = 1024
D_FF = 2816
N_CHIP = 4
FF_SH = D_FF // N_CHIP
IN_COLS = 5128
IN_SH = IN_COLS // N_CHIP
RET_H, RET_DV = 4, 128
FOX_H, FOX_D = 8, 64
CHUNK = 128
EPS = 1e-6
NEG = -1e30
LANE = 128
C_RET, C_GT, C_FOX, C_A, C_END = 0, 1024, 1536, 3072, 5120
L_CQ, L_CK, L_LSE, L_MAX = 64, 67, 70, 73

ADAM_LR, ADAM_B1, ADAM_B2, ADAM_EPS, ADAM_WD, ADAM_STEP = 0.001, 0.9, 0.999, 1e-08, 0.01, 10
VMEM_BIG = 56 * 1024 * 1024
VMEM_HUGE = 60 * 1024 * 1024
GRAD_TK = 2048
FFN_TM = 512
FOX_SUB = 512


def _nn(a, b):
    return lax.dot_general(a, b, (((1,), (0,)), ((), ())), preferred_element_type=F32)


def _nt(a, b):
    return lax.dot_general(a, b, (((1,), (1,)), ((), ())), preferred_element_type=F32)


def _tn(a, b):
    return lax.dot_general(a, b, (((0,), (0,)), ((), ())), preferred_element_type=F32)


def _split3(x):
    hi = x.astype(BF)
    r = x - hi.astype(F32)
    mid = r.astype(BF)
    lo = (r - mid.astype(F32)).astype(BF)
    return hi, mid, lo


def _sigmoid(x):
    return 0.5 * jnp.tanh(0.5 * x) + 0.5


def _swap32(x):
    lane = lax.broadcasted_iota(jnp.int32, x.shape, 1)
    return jnp.where(lane < 32, pltpu.roll(x, 96, 1), pltpu.roll(x, 32, 1))


def _params(sem, vmem=None):
    return pltpu.CompilerParams(dimension_semantics=sem, vmem_limit_bytes=vmem)


def _row_tile(rows, cap, mult):
    return max(d for d in range(mult, cap + 1, mult) if rows % d == 0)


def _assemble_w_in(stack, tr=256):
    def body(s_ref, a_ref, f_ref):
        full = jnp.concatenate([s_ref[k].astype(F32) for k in range(N_CHIP)], axis=-1)
        a_ref[...] = jnp.concatenate([full[:, :3072], full[:, 3080:IN_COLS]], axis=-1).astype(BF)
        f_ref[...] = jnp.concatenate([full[:, 3072:3080], jnp.zeros((tr, LANE - FOX_H), F32)], axis=-1).astype(BF)

    return pl.pallas_call(
        body, name="assemble_w_in", grid=(D_MODEL // tr,),
        in_specs=[pl.BlockSpec((N_CHIP, tr, IN_SH), lambda i: (0, i, 0))],
        out_specs=[pl.BlockSpec((tr, C_END), lambda i: (i, 0)), pl.BlockSpec((tr, LANE), lambda i: (i, 0))],
        out_shape=[jax.ShapeDtypeStruct((D_MODEL, C_END), BF), jax.ShapeDtypeStruct((D_MODEL, LANE), BF)],
        compiler_params=_params(("parallel",), VMEM_BIG),
    )(stack)


def _pack_g_in(g_ret, g_gt, g_fox, g_a, g_ff, tr=256):
    def body(r_ref, t_ref, x_ref, a_ref, f_ref, o_ref):
        full = jnp.concatenate([r_ref[...], t_ref[...], x_ref[...], f_ref[...][:, :FOX_H], a_ref[...]], axis=-1)
        for k in range(N_CHIP):
            o_ref[k] = full[:, k * IN_SH:(k + 1) * IN_SH].astype(BF)

    def spec(w):
        return pl.BlockSpec((tr, w), lambda i: (i, 0))

    return pl.pallas_call(
        body, name="pack_g_in", grid=(D_MODEL // tr,),
        in_specs=[spec(1024), spec(512), spec(1536), spec(2048), spec(LANE)],
        out_specs=pl.BlockSpec((N_CHIP, tr, IN_SH), lambda i: (0, i, 0)),
        out_shape=jax.ShapeDtypeStruct((N_CHIP, D_MODEL, IN_SH), BF),
        compiler_params=_params(("parallel",), VMEM_BIG),
    )(g_ret, g_gt, g_fox, g_a, g_ff)


def _rms_cast(x, g, tm=512):
    T = x.shape[0]

    def body(x_ref, g_ref, o_ref):
        xv = x_ref[...]
        r = lax.rsqrt(jnp.mean(xv * xv, axis=-1, keepdims=True) + EPS)
        o_ref[...] = (xv * r * g_ref[...]).astype(BF)

    return pl.pallas_call(
        body, name="rms_cast", grid=(T // tm,),
        in_specs=[pl.BlockSpec((tm, D_MODEL), lambda i: (i, 0)), pl.BlockSpec((1, D_MODEL), lambda i: (0, 0))],
        out_specs=pl.BlockSpec((tm, D_MODEL), lambda i: (i, 0)),
        out_shape=jax.ShapeDtypeStruct((T, D_MODEL), BF),
        compiler_params=_params(("parallel",)),
    )(x, g)


def _hosted_call(body, name, grid, in_specs, out_specs, out_shape, scratch_shapes, vmem, args, push):
    sem = ("arbitrary",) * len(grid)
    if push is None:
        res = pl.pallas_call(body, name=name, grid=grid, in_specs=in_specs, out_specs=out_specs, out_shape=out_shape,
                             scratch_shapes=scratch_shapes, compiler_params=_params(sem, vmem))(*args)
        return list(res), []
    srcs, lands = push
    ns, nl, n_in, n_out = (0 if srcs is None else len(srcs)), len(lands), len(in_specs), len(out_specs)
    n_scr = len(scratch_shapes)

    def wrapped(*refs):
        pos = n_in + ns + nl
        ins, x_in = refs[:n_in], refs[n_in:pos]
        outs, x_out = refs[pos:pos + n_out], refs[pos + n_out:pos + n_out + nl]
        scr = refs[pos + n_out + nl:pos + n_out + nl + n_scr]
        ssem, rsem = refs[-2], refs[-1]
        src = None if srcs is None else x_in[:ns]
        ids = [pl.program_id(a) for a in range(len(grid))]
        first = functools.reduce(lambda p, q: p & q, [ids[a] == 0 for a in range(len(grid))])
        last = functools.reduce(lambda p, q: p & q, [ids[a] == grid[a] - 1 for a in range(len(grid))])

        @pl.when(first)
        def _():
            for cp in _push_copies(src, x_out, ssem, rsem, False):
                cp.start()

        body(*ins, *outs, *scr)

        @pl.when(last)
        def _():
            for cp in _push_copies(src, x_out, ssem, rsem, True):
                cp.wait_recv()
                cp.wait_send()

    anyspec = pl.BlockSpec(memory_space=pl.ANY)
    extra = ([] if srcs is None else list(srcs)) + list(lands)
    res = pl.pallas_call(
        wrapped, name=name, grid=grid,
        in_specs=list(in_specs) + [anyspec] * len(extra), out_specs=list(out_specs) + [anyspec] * nl,
        out_shape=list(out_shape) + [jax.ShapeDtypeStruct(a.shape, a.dtype) for a in lands],
        input_output_aliases={n_in + ns + i: n_out + i for i in range(nl)},
        scratch_shapes=list(scratch_shapes) + [pltpu.SemaphoreType.DMA((3 * nl,)), pltpu.SemaphoreType.DMA((3 * nl,))],
        compiler_params=_params(sem, vmem),
    )(*args, *extra)
    return list(res[:n_out]), list(res[n_out:])


def _mm_nn(a, b, name, out_dtype, tm=512, tn=1024, push=None):
    M, K = a.shape
    N = b.shape[1]
    tn = min(tn, N)

    def body(a_ref, b_ref, o_ref):
        o_ref[...] = _nn(a_ref[...], b_ref[...]).astype(o_ref.dtype)

    (out,), lands = _hosted_call(
        body, name, (N // tn, M // tm),
        [pl.BlockSpec((tm, K), lambda j, i: (i, 0)), pl.BlockSpec((K, tn), lambda j, i: (0, j))],
        [pl.BlockSpec((tm, tn), lambda j, i: (i, j))], [jax.ShapeDtypeStruct((M, N), out_dtype)], [], None, (a, b), push)
    return out, lands


def _mm_tn(a, b, name, grid, a_spec, b_spec, o_spec, out_shape, acc_shape):
    nk = grid[-1]

    def body(a_ref, b_ref, o_ref, acc):
        k = pl.program_id(len(grid) - 1)

        @pl.when(k == 0)
        def _():
            acc[...] = jnp.zeros(acc.shape, F32)

        acc[...] += _tn(a_ref[...].astype(BF), b_ref[...].astype(BF))

        @pl.when(k == nk - 1)
        def _():
            o_ref[...] = acc[...].astype(o_ref.dtype)

    return pl.pallas_call(
        body, name=name, grid=grid, in_specs=[a_spec, b_spec], out_specs=o_spec, out_shape=out_shape,
        scratch_shapes=[pltpu.VMEM(acc_shape, F32)],
        compiler_params=_params(("parallel",) * (len(grid) - 1) + ("arbitrary",), VMEM_BIG),
    )(a, b)


def _grad_plain(a, b, name, out_dtype, tk=GRAD_TK, tn=1024):
    T, M = a.shape
    N = b.shape[1]
    tn = min(tn, N)
    return _mm_tn(a, b, name, (N // tn, T // tk),
                  pl.BlockSpec((tk, M), lambda j, k: (k, 0)), pl.BlockSpec((tk, tn), lambda j, k: (k, j)),
                  pl.BlockSpec((M, tn), lambda j, k: (0, j)), jax.ShapeDtypeStruct((M, N), out_dtype), (M, tn))


def _grad_colstack(a, b, name, wcol, tk=GRAD_TK):
    T, M = a.shape
    N = b.shape[1]
    S = N // wcol
    nk = T // tk

    def body(a_ref, b_ref, o_ref, acc):
        k = pl.program_id(0)

        @pl.when(k == 0)
        def _():
            acc[...] = jnp.zeros(acc.shape, F32)

        acc[...] += _tn(a_ref[...], b_ref[...])

        @pl.when(k == nk - 1)
        def _():
            for s in range(S):
                o_ref[s] = acc[:, s * wcol:(s + 1) * wcol].astype(BF)

    return pl.pallas_call(
        body, name=name, grid=(nk,),
        in_specs=[pl.BlockSpec((tk, M), lambda k: (k, 0)), pl.BlockSpec((tk, N), lambda k: (k, 0))],
        out_specs=pl.BlockSpec((S, M, wcol), lambda k: (0, 0, 0)), out_shape=jax.ShapeDtypeStruct((S, M, wcol), BF),
        scratch_shapes=[pltpu.VMEM((M, N), F32)], compiler_params=_params(("arbitrary",), VMEM_BIG),
    )(a, b)


def _grad_astack(a, b, name, tk=1024):
    S, T, m = a.shape
    N = b.shape[1]
    nk = T // tk

    def body(a_ref, b_ref, o_ref, acc):
        k = pl.program_id(0)

        @pl.when(k == 0)
        def _():
            acc[...] = jnp.zeros(acc.shape, F32)

        bb = b_ref[...].astype(BF)
        for s in range(S):
            acc[s] += _tn(a_ref[s], bb)

        @pl.when(k == nk - 1)
        def _():
            o_ref[...] = acc[...].astype(BF)

    return pl.pallas_call(
        body, name=name, grid=(nk,),
        in_specs=[pl.BlockSpec((S, tk, m), lambda k: (0, k, 0)), pl.BlockSpec((tk, N), lambda k: (k, 0))],
        out_specs=pl.BlockSpec((S, m, N), lambda k: (0, 0, 0)), out_shape=jax.ShapeDtypeStruct((S, m, N), BF),
        scratch_shapes=[pltpu.VMEM((S, m, N), F32)], compiler_params=_params(("arbitrary",), VMEM_BIG),
    )(a, b)


def _rope_tables(T):
    half = 32
    pos = np.arange(T, dtype=np.float32)
    inv_freq = (np.float32(1.0) / (np.float32(10000.0) ** (np.arange(half, dtype=np.float32) / np.float32(half)))).astype(np.float32)
    ang = (pos[:, None] * inv_freq[None, :]).astype(np.float32)
    cos, sin = np.cos(ang).astype(np.float32), np.sin(ang).astype(np.float32)
    z = np.zeros((T, 64), np.float32)
    return (jnp.asarray(np.concatenate([cos, cos, z], axis=-1)), jnp.asarray(np.concatenate([-sin, sin, z], axis=-1)))


def _ret_consts():
    h = np.arange(RET_H, dtype=np.float32)
    log_g = np.log1p(-(np.float32(2.0) ** (-5.0 - h))).astype(np.float32)
    idx = np.arange(CHUNK, dtype=np.float32)
    diff = idx[:, None] - idx[None, :]
    decay = np.where(diff[None] >= 0, np.exp(np.maximum(diff, 0.0)[None] * log_g[:, None, None]), 0.0)
    zeta = np.exp((CHUNK - 1.0 - idx)[None, :] * log_g[:, None])
    xi = np.exp((idx + 1.0)[None, :] * log_g[:, None])
    gc = np.exp(CHUNK * log_g)
    bc = lambda v: np.broadcast_to(v[:, :, None], (RET_H, CHUNK, LANE)).astype(np.float32)
    gcb = np.broadcast_to(gc[:, None, None], (RET_H, CHUNK, LANE)).astype(np.float32)
    return (jnp.asarray(decay.astype(np.float32)), jnp.asarray(bc(zeta)), jnp.asarray(bc(xi)), jnp.asarray(gcb))


def _mix_prep(z_a, z_ff, cos_t, sin_t, b_f, g_q, g_k, tm=256, push=None):
    T = z_a.shape[0]

    def body(zqk_ref, zf_ref, zff_ref, cos_ref, sin_ref, b_ref, g_ref, seg_ref, segt_ref,
             qr_ref, kr_ref, qf_ref, kf_ref, vf_ref, c_ref, nmax_ref, carry):
        i = pl.program_id(0)

        @pl.when(i == 0)
        def _():
            carry[...] = jnp.zeros(carry.shape, F32)
            nmax_ref[...] = jnp.zeros(nmax_ref.shape, F32)

        lane = lax.broadcasted_iota(jnp.int32, (tm, LANE), 1)
        zpad = jnp.zeros((tm, 64), F32)
        cosv, sinv = cos_ref[...], sin_ref[...]
        zqk = zqk_ref[...].astype(F32)
        for h in range(RET_H):
            for src, dst, scale in ((0, qr_ref, 1.0), (256, kr_ref, 0.125)):
                xh = jnp.concatenate([zqk[:, src + 64 * h: src + 64 * h + 64], zpad], axis=-1)
                rot = xh * cosv + _swap32(xh) * sinv
                dst[h] = (rot * scale).astype(BF)

        lf_in = zff_ref[...] + b_ref[...]
        logf = jnp.minimum(lf_in, 0.0) - jnp.log(1.0 + jnp.exp(-jnp.abs(lf_in)))
        row = lax.broadcasted_iota(jnp.int32, (tm, tm), 0)
        col = lax.broadcasted_iota(jnp.int32, (tm, tm), 1)
        tri = (row >= col).astype(BF)
        hi, mid, lo = _split3(logf)
        cs = _nn(tri, hi) + _nn(tri, mid) + _nn(tri, lo) + carry[...]
        carry[...] = cs[tm - 1:tm, :]
        c_ref[...] = cs

        def seg_sum(v):
            return sum(_nn(t, seg_ref[...]) for t in _split3(v))

        zf = zf_ref[...].astype(F32)
        xqk = zf[:, :1024]
        rinv = lax.rsqrt(seg_sum(xqk * xqk) * (1.0 / FOX_D) + EPS)
        xn = xqk * sum(_nn(t, segt_ref[...]) for t in _split3(rinv)) * g_ref[...]
        nmax_ref[...] = jnp.maximum(nmax_ref[...], jnp.max(seg_sum(xn * xn), axis=0, keepdims=True))

        one = jnp.ones((tm, LANE), F32)
        for h in range(FOX_H):
            c = cs[:, h:h + 1]
            chi, cmid, clo = [t.astype(F32) for t in _split3(c)]
            qn = xn[:, 64 * h:64 * h + 64]
            kn = xn[:, 512 + 64 * h:512 + 64 * h + 64]
            vh = zf[:, 1024 + 64 * h:1024 + 64 * h + 64]
            qa = jnp.concatenate([qn, zpad], axis=-1)
            qa = jnp.where(lane == L_CQ, chi, jnp.where(lane == L_CQ + 1, cmid, jnp.where(lane == L_CQ + 2, clo, qa)))
            qa = jnp.where((lane >= L_CK) & (lane < L_CK + 3), one, qa)
            ka = jnp.concatenate([kn, zpad], axis=-1)
            ka = jnp.where(lane == L_CK, -chi, jnp.where(lane == L_CK + 1, -cmid, jnp.where(lane == L_CK + 2, -clo, ka)))
            ka = jnp.where(((lane >= L_CQ) & (lane < L_CQ + 3)) | ((lane >= L_LSE) & (lane < L_MAX + 3)), one, ka)
            va = jnp.concatenate([vh, zpad], axis=-1)
            va = jnp.where((lane >= 64) & (lane < 67), one, va)
            qf_ref[h] = qa.astype(BF)
            kf_ref[h] = ka.astype(BF)
            vf_ref[h] = va.astype(BF)

    hspec4 = pl.BlockSpec((RET_H, tm, LANE), lambda i: (0, i, 0))
    hspec8 = pl.BlockSpec((FOX_H, tm, LANE), lambda i: (0, i, 0))
    const = lambda r, w: pl.BlockSpec((r, w), lambda i: (0, 0))
    seg = _segment_matrix()
    g_all = jnp.concatenate([jnp.tile(g_q * 0.125, (1, FOX_H)), jnp.tile(g_k, (1, FOX_H))], axis=1)
    return _hosted_call(
        body, "mix_prep", (T // tm,),
        [pl.BlockSpec((tm, 512), lambda i: (i, 0)), pl.BlockSpec((tm, 1536), lambda i: (i, 1)),
         pl.BlockSpec((tm, LANE), lambda i: (i, 0)), pl.BlockSpec((tm, LANE), lambda i: (i, 0)),
         pl.BlockSpec((tm, LANE), lambda i: (i, 0)), const(1, LANE), const(1, 1024), const(1024, LANE), const(LANE, 1024)],
        [hspec4, hspec4, hspec8, hspec8, hspec8, pl.BlockSpec((tm, LANE), lambda i: (i, 0)), const(1, LANE)],
        [jax.ShapeDtypeStruct((RET_H, T, LANE), BF)] * 2 + [jax.ShapeDtypeStruct((FOX_H, T, LANE), BF)] * 3
        + [jax.ShapeDtypeStruct((T, LANE), F32), jax.ShapeDtypeStruct((1, LANE), F32)],
        [pltpu.VMEM((1, LANE), F32)], VMEM_BIG, (z_a, z_a, z_ff, cos_t, sin_t, b_f, g_all, seg, seg.T), push)


def _segment_matrix():
    m = np.zeros((2 * FOX_H * FOX_D, LANE), np.float32)
    m[np.arange(2 * FOX_H * FOX_D), np.arange(2 * FOX_H * FOX_D) // FOX_D] = 1.0
    return jnp.asarray(m, dtype=BF)


def _ret_fwd(qr, kr, z_a, g_ret, consts, tt=512):
    T = z_a.shape[0]
    nch = tt // CHUNK
    decay, zeta, xi, gcb = consts

    def body(q_ref, k_ref, v_ref, gt_ref, g_ref, d_ref, ze_ref, xi_ref, gc_ref, o_ref, u_ref, st_ref, r_sc):
        i = pl.program_id(0)

        @pl.when(i == 0)
        def _():
            r_sc[...] = jnp.zeros(r_sc.shape, F32)

        for c in range(nch):
            rows = slice(c * CHUNK, (c + 1) * CHUNK)
            for h in range(RET_H):
                cols = slice(h * RET_DV, (h + 1) * RET_DV)
                q, k = q_ref[h, rows, :], k_ref[h, rows, :]
                v32 = v_ref[rows, cols].astype(F32)
                r = r_sc[h]
                st_ref[h, rows, :] = r
                s = _nt(q, k) * d_ref[h]
                o = _nn(s.astype(BF), v32.astype(BF)) + _nn(q, r.astype(BF)) * xi_ref[h]
                r_sc[h] = gc_ref[h] * r + _tn(k, (v32 * ze_ref[h]).astype(BF))
                o_ref[rows, cols] = o
                mu = jnp.mean(o, axis=-1, keepdims=True)
                xc = o - mu
                on = xc * lax.rsqrt(jnp.mean(xc * xc, axis=-1, keepdims=True) + EPS)
                gt = gt_ref[rows, cols].astype(F32)
                u_ref[rows, cols] = (gt * _sigmoid(gt) * (on * g_ref[:, cols])).astype(BF)

    hspec = pl.BlockSpec((RET_H, tt, LANE), lambda i: (0, i, 0))
    cspec = pl.BlockSpec((RET_H, CHUNK, LANE), lambda i: (0, 0, 0))
    return pl.pallas_call(
        body, name="ret_fwd", grid=(T // tt,),
        in_specs=[hspec, hspec, pl.BlockSpec((tt, 512), lambda i: (i, 1)), pl.BlockSpec((tt, 512), lambda i: (i, 2)),
                  pl.BlockSpec((1, 512), lambda i: (0, 0)), cspec, cspec, cspec, cspec],
        out_specs=[pl.BlockSpec((tt, 512), lambda i: (i, 0)), pl.BlockSpec((tt, 512), lambda i: (i, 0)), hspec],
        out_shape=[jax.ShapeDtypeStruct((T, 512), F32), jax.ShapeDtypeStruct((T, 512), BF),
                   jax.ShapeDtypeStruct((RET_H, T, LANE), F32)],
        scratch_shapes=[pltpu.VMEM((RET_H, CHUNK, LANE), F32)],
        compiler_params=_params(("arbitrary",), VMEM_BIG),
    )(qr, kr, z_a, z_a, g_ret, decay, zeta, xi, gcb)


PRUNE_LOG = -110.0


def _prune_tables(c, nmax, sub):
    n = c.shape[0] // sub
    u = jnp.sqrt(nmax[0, :FOX_H] * nmax[0, FOX_H:2 * FOX_H]) * 1.02 + 0.5
    first = c[0::sub, :FOX_H].T
    last = c[sub - 1::sub, :FOX_H].T
    blk = jnp.arange(n, dtype=jnp.int32)
    needed = (2.0 * u[:, None, None] + first[:, :, None] - last[:, None, :] >= PRUNE_LOG) | (blk[None, :] >= blk[:, None])[None]
    jlo = jnp.argmax(needed, axis=2).astype(jnp.int32)

    def end_of(key_block):
        reach = jlo[:, None, :] <= key_block[None, :, None]
        return (n - jnp.argmax(reach[:, :, ::-1], axis=2)).astype(jnp.int32)

    sup = 2 * jnp.arange(n // 2, dtype=jnp.int32)
    end_last = end_of(sup + 1)
    end_both = jnp.clip(end_of(sup), sup[None, :] + 2, end_last)
    return jlo, end_both, end_last


def _fox_fwd(jlo, q, k, v, sub=FOX_SUB):
    H, T, _ = q.shape
    tb = 2 * sub

    def body(js_ref, q_ref, k_ref, v_ref, o_ref, q2_ref, mx_sc, acc_sc):
        i = pl.program_id(1)
        hd = pl.program_id(0)
        starts = [jnp.minimum(js_ref[hd, 2 * i], 2 * i), jnp.minimum(js_ref[hd, 2 * i + 1], 2 * i)]
        lane = lax.broadcasted_iota(jnp.int32, (sub, LANE), 1)
        row = lax.broadcasted_iota(jnp.int32, (sub, sub), 0)
        col = lax.broadcasted_iota(jnp.int32, (sub, sub), 1)
        causal = row >= col
        qs = [q_ref[0:sub, :], q_ref[sub:tb, :]]
        d0 = pl.multiple_of(i * tb, tb)
        d1 = pl.multiple_of(i * tb + sub, sub)

        def lane_max(s):
            m = s[:, 0:LANE]
            for c in range(1, s.shape[1] // LANE):
                m = jnp.maximum(m, s[:, c * LANE:(c + 1) * LANE])
            return m

        mx_sc[...] = jnp.full(mx_sc.shape, NEG, F32)

        for a in range(2):
            def max_body(j, carry, a=a):
                kb = k_ref[pl.ds(pl.multiple_of(j * sub, sub), sub), :]
                mx_sc[a] = jnp.maximum(mx_sc[a], lane_max(_nt(qs[a], kb)))
                return carry

            lax.fori_loop(starts[a], 2 * i, max_body, 0)
        k0, k1 = k_ref[pl.ds(d0, sub), :], k_ref[pl.ds(d1, sub), :]
        v0, v1 = v_ref[pl.ds(d0, sub), :], v_ref[pl.ds(d1, sub), :]
        mx = [jnp.maximum(mx_sc[0], lane_max(jnp.where(causal, _nt(qs[0], k0), NEG))),
              jnp.maximum(jnp.maximum(mx_sc[1], lane_max(_nt(qs[1], k0))),
                          lane_max(jnp.where(causal, _nt(qs[1], k1), NEG)))]
        ms = [jnp.max(t, axis=1, keepdims=True) for t in mx]

        def put3(base, first, val):
            hi, mid, lo = _split3(val)
            return jnp.where(lane == first, hi, jnp.where(lane == first + 1, mid, jnp.where(lane == first + 2, lo, base)))

        qm = [put3(qs[a], L_MAX, -ms[a]) for a in range(2)]

        acc_sc[...] = jnp.zeros(acc_sc.shape, F32)

        for a in range(2):
            def acc_body(j, carry, a=a):
                off = pl.multiple_of(j * sub, sub)
                acc_sc[a] += _nn(jnp.exp(_nt(qm[a], k_ref[pl.ds(off, sub), :])).astype(BF), v_ref[pl.ds(off, sub), :])
                return carry

            lax.fori_loop(starts[a], 2 * i, acc_body, 0)

        def pv(qa, kk, vv, masked):
            p = jnp.exp(_nt(qa, kk))
            if masked:
                p = jnp.where(causal, p, 0.0)
            return _nn(p.astype(BF), vv)

        accs = [acc_sc[0] + pv(qm[0], k0, v0, True),
                acc_sc[1] + pv(qm[1], k0, v0, False) + pv(qm[1], k1, v1, True)]
        for a in range(2):
            rows = slice(a * sub, (a + 1) * sub)
            l = accs[a][:, 64:65]
            o_ref[rows, :] = jnp.where(lane < 64, accs[a] / l, 0.0)
            q2_ref[rows, :] = put3(qs[a], L_LSE, -(ms[a] + jnp.log(l)))

    blk = pl.BlockSpec((None, tb, LANE), lambda h, i, js: (h, i, 0))
    full = pl.BlockSpec((None, T, LANE), lambda h, i, js: (h, 0, 0))
    return pl.pallas_call(
        body, name="fox_fwd",
        grid_spec=pltpu.PrefetchScalarGridSpec(
            num_scalar_prefetch=1, grid=(H, T // tb), in_specs=[blk, full, full], out_specs=[blk, blk],
            scratch_shapes=[pltpu.VMEM((2, sub, LANE), F32), pltpu.VMEM((2, sub, LANE), F32)]),
        out_shape=[jax.ShapeDtypeStruct((H, T, LANE), F32), jax.ShapeDtypeStruct((H, T, LANE), BF)],
        compiler_params=_params(("parallel", "arbitrary"), VMEM_BIG),
    )(jlo, q, k, v)


def _merge_out(u_r, o_fox, z_a, x, g_ffn, w_ro, w_fo, w_out, tm=256):
    T = x.shape[0]

    def body(u_ref, of_ref, ar_ref, af_ref, x_ref, g_ref, wro_ref, wfo_ref, wout_ref,
             yr_ref, yf_ref, m_ref, x2_ref, h2_ref, oc_ref):
        u = u_ref[...]
        oc = jnp.concatenate([of_ref[h][:, :FOX_D] for h in range(FOX_H)], axis=-1).astype(BF)
        oc_ref[...] = oc
        yr = jnp.concatenate([_nn(u, wro_ref[k]) for k in range(N_CHIP)], axis=-1)
        yf = jnp.concatenate([_nn(oc, wfo_ref[k]) for k in range(N_CHIP)], axis=-1)
        yr_ref[...] = yr
        yf_ref[...] = yf
        m = (_sigmoid(ar_ref[...].astype(F32)) * yr + _sigmoid(af_ref[...].astype(F32)) * yf).astype(BF)
        m_ref[...] = m
        x2 = x_ref[...]
        for k in range(N_CHIP):
            x2 = x2 + _nn(m[:, 256 * k:256 * k + 256], wout_ref[k])
        x2_ref[...] = x2
        r = lax.rsqrt(jnp.mean(x2 * x2, axis=-1, keepdims=True) + EPS)
        h2_ref[...] = (x2 * r * g_ref[...]).astype(BF)

    row = lambda w: pl.BlockSpec((tm, w), lambda i: (i, 0))
    const = lambda shp: pl.BlockSpec(shp, lambda i: (0,) * len(shp))
    return pl.pallas_call(
        body, name="merge_out", grid=(T // tm,),
        in_specs=[row(512), pl.BlockSpec((FOX_H, tm, LANE), lambda i: (0, i, 0)),
                  pl.BlockSpec((tm, 1024), lambda i: (i, 3)), pl.BlockSpec((tm, 1024), lambda i: (i, 4)),
                  row(1024), const((1, 1024)), const((N_CHIP, 512, 256)), const((N_CHIP, 512, 256)),
                  const((N_CHIP, 256, 1024))],
        out_specs=[row(1024), row(1024), row(1024), row(1024), row(1024), row(512)],
        out_shape=[jax.ShapeDtypeStruct((T, 1024), F32), jax.ShapeDtypeStruct((T, 1024), F32),
                   jax.ShapeDtypeStruct((T, 1024), BF), jax.ShapeDtypeStruct((T, 1024), F32),
                   jax.ShapeDtypeStruct((T, 1024), BF), jax.ShapeDtypeStruct((T, 512), BF)],
        compiler_params=_params(("parallel",), VMEM_BIG),
    )(u_r, o_fox, z_a, z_a, x, g_ffn, w_ro, w_fo, w_out)


def _load_resident(hbm_refs, vmem_refs, sem):
    cps = [pltpu.make_async_copy(h, v, sem.at[i]) for i, (h, v) in enumerate(zip(hbm_refs, vmem_refs))]
    for cp in cps:
        cp.start()
    for cp in cps:
        cp.wait()


def _ffn_fwd(h2, x2, tgt, w_gate, w_up, w_down, tm=FFN_TM):
    T = h2.shape[0]

    def body(h_ref, x2_ref, t_ref, wg_hbm, wu_hbm, wd_hbm, a_ref, b_ref, act_ref, dy_ref, ls_ref, wg, wu, wd, sem):
        @pl.when(pl.program_id(0) == 0)
        def _():
            _load_resident((wg_hbm, wu_hbm, wd_hbm), (wg, wu, wd), sem)
            ls_ref[...] = jnp.zeros(ls_ref.shape, F32)

        h = h_ref[...]
        err = x2_ref[...] - t_ref[...]
        for k in range(N_CHIP):
            gp = _nt(h, wg[k])
            up = _nt(h, wu[k])
            sg = _sigmoid(gp)
            silu = gp * sg
            a_ref[k] = silu.astype(BF)
            b_ref[k] = (up * sg * (1.0 + gp * (1.0 - sg))).astype(BF)
            act = (silu * up).astype(BF)
            act_ref[k] = act
            err = err + _nn(act, wd[k])
        dy_ref[...] = err * (1.0 / D_MODEL)
        ls_ref[...] += jnp.sum(err * err, axis=0, keepdims=True)

    row = pl.BlockSpec((tm, D_MODEL), lambda i: (i, 0))
    hid = pl.BlockSpec((N_CHIP, tm, FF_SH), lambda i: (0, i, 0))
    anyspec = pl.BlockSpec(memory_space=pl.ANY)
    wshape = pltpu.VMEM((N_CHIP, FF_SH, D_MODEL), BF)
    return pl.pallas_call(
        body, name="ffn_fwd", grid=(T // tm,),
        in_specs=[row, row, row, anyspec, anyspec, anyspec],
        out_specs=[hid, hid, hid, row, pl.BlockSpec((1, D_MODEL), lambda i: (0, 0))],
        out_shape=[jax.ShapeDtypeStruct((N_CHIP, T, FF_SH), BF)] * 3
        + [jax.ShapeDtypeStruct((T, D_MODEL), F32), jax.ShapeDtypeStruct((1, D_MODEL), F32)],
        scratch_shapes=[wshape, wshape, wshape, pltpu.SemaphoreType.DMA((3,))],
        compiler_params=_params(("arbitrary",), VMEM_HUGE),
    )(h2, x2, tgt, w_gate, w_up, w_down)


def _ffn_bwd(dy, sa, sb, x2, g_ffn, w_gate, w_up, w_down, tm=FFN_TM):
    T = dy.shape[0]

    def body(dy_ref, a_ref, b_ref, x2_ref, g_ref, wg_hbm, wu_hbm, wd_hbm, dgp_ref, dup_ref, dx_ref, dg_ref,
             wg, wu, wd, sem):
        @pl.when(pl.program_id(0) == 0)
        def _():
            _load_resident((wg_hbm, wu_hbm, wd_hbm), (wg, wu, wd), sem)
            dg_ref[...] = jnp.zeros(dg_ref.shape, F32)

        dy = dy_ref[...]
        dyb = dy.astype(BF)
        dh = jnp.zeros((tm, D_MODEL), F32)
        for k in range(N_CHIP):
            dact = _nt(dyb, wd[k])
            dup = (dact * a_ref[k]).astype(BF)
            dgp = (dact * b_ref[k]).astype(BF)
            dgp_ref[k] = dgp
            dup_ref[k] = dup
            dh = dh + _nn(dgp, wg[k]) + _nn(dup, wu[k])
        x2 = x2_ref[...]
        r = lax.rsqrt(jnp.mean(x2 * x2, axis=-1, keepdims=True) + EPS)
        xn = x2 * r
        dg_ref[...] += jnp.sum(dh * xn, axis=0, keepdims=True)
        dxn = dh * g_ref[...]
        dx_ref[...] = dy + r * (dxn - xn * jnp.mean(dxn * xn, axis=-1, keepdims=True))

    row = pl.BlockSpec((tm, D_MODEL), lambda i: (i, 0))
    hid = pl.BlockSpec((N_CHIP, tm, FF_SH), lambda i: (0, i, 0))
    vec = pl.BlockSpec((1, D_MODEL), lambda i: (0, 0))
    anyspec = pl.BlockSpec(memory_space=pl.ANY)
    wshape = pltpu.VMEM((N_CHIP, FF_SH, D_MODEL), BF)
    return pl.pallas_call(
        body, name="ffn_bwd", grid=(T // tm,),
        in_specs=[row, hid, hid, row, vec, anyspec, anyspec, anyspec],
        out_specs=[hid, hid, row, vec],
        out_shape=[jax.ShapeDtypeStruct((N_CHIP, T, FF_SH), BF), jax.ShapeDtypeStruct((N_CHIP, T, FF_SH), BF),
                   jax.ShapeDtypeStruct((T, D_MODEL), F32), jax.ShapeDtypeStruct((1, D_MODEL), F32)],
        scratch_shapes=[wshape, wshape, wshape, pltpu.SemaphoreType.DMA((3,))],
        compiler_params=_params(("arbitrary",), VMEM_HUGE),
    )(dy, sa, sb, x2, g_ffn, w_gate, w_up, w_down)


def _out_bwd(dx2, z_a, y_r, y_f, o_raw, o_fox, g_ret, w_ro, w_fo, w_out, tm=256, push=None):
    T = dx2.shape[0]

    def body(dx_ref, gt_ref, ar_ref, af_ref, yr_ref, yf_ref, o_ref, of_ref, g_ref, wro_ref, wfo_ref, wout_ref,
             dyr_ref, dyf_ref, dgt_ref, da_ref, do_ref, dof_ref, dg_ref):
        i = pl.program_id(0)

        @pl.when(i == 0)
        def _():
            dg_ref[...] = jnp.zeros(dg_ref.shape, F32)

        dxb = dx_ref[...].astype(BF)
        dm = jnp.concatenate([_nt(dxb, wout_ref[k]) for k in range(N_CHIP)], axis=-1)
        sr, sf = _sigmoid(ar_ref[...].astype(F32)), _sigmoid(af_ref[...].astype(F32))
        dyr = dm * sr
        dyf = dm * sf
        da_ref[:, :1024] = (dyr * yr_ref[...] * (1.0 - sr)).astype(BF)
        da_ref[:, 1024:] = (dyf * yf_ref[...] * (1.0 - sf)).astype(BF)
        dyr = dyr.astype(BF)
        dyf = dyf.astype(BF)
        dyr_ref[...] = dyr
        dyf_ref[...] = dyf
        du = jnp.zeros((tm, 512), F32)
        doc = jnp.zeros((tm, 512), F32)
        for k in range(N_CHIP):
            du = du + _nt(dyr[:, 256 * k:256 * k + 256], wro_ref[k])
            doc = doc + _nt(dyf[:, 256 * k:256 * k + 256], wfo_ref[k])

        for h in range(RET_H):
            cols = slice(h * RET_DV, (h + 1) * RET_DV)
            o = o_ref[:, cols]
            mu = jnp.mean(o, axis=-1, keepdims=True)
            xc = o - mu
            rstd = lax.rsqrt(jnp.mean(xc * xc, axis=-1, keepdims=True) + EPS)
            on = xc * rstd
            g = g_ref[:, cols]
            gt = gt_ref[:, cols].astype(F32)
            sg = _sigmoid(gt)
            duh = du[:, cols]
            dgt_ref[:, cols] = (duh * (on * g) * sg * (1.0 + gt * (1.0 - sg))).astype(BF)
            dog = duh * gt * sg
            dg_ref[:, cols] += jnp.sum(dog * on, axis=0, keepdims=True)
            don = dog * g
            do_ref[:, cols] = rstd * (don - jnp.mean(don, axis=-1, keepdims=True)
                                      - on * jnp.mean(don * on, axis=-1, keepdims=True))

        lane = lax.broadcasted_iota(jnp.int32, (tm, LANE), 1)
        zpad = jnp.zeros((tm, 64), F32)
        for h in range(FOX_H):
            doh = doc[:, 64 * h:64 * h + 64]
            delta = jnp.sum(doh * of_ref[h][:, :FOX_D], axis=-1, keepdims=True)
            hi, mid, lo = [t.astype(F32) for t in _split3(-delta)]
            da = jnp.concatenate([doh, zpad], axis=-1)
            da = jnp.where(lane == 64, hi, jnp.where(lane == 65, mid, jnp.where(lane == 66, lo, da)))
            dof_ref[h] = da.astype(BF)

    row = lambda w: pl.BlockSpec((tm, w), lambda i: (i, 0))
    const = lambda shp: pl.BlockSpec(shp, lambda i: (0,) * len(shp))
    hsp = pl.BlockSpec((FOX_H, tm, LANE), lambda i: (0, i, 0))
    return _hosted_call(
        body, "out_bwd", (T // tm,),
        [row(1024), pl.BlockSpec((tm, 512), lambda i: (i, 2)), pl.BlockSpec((tm, 1024), lambda i: (i, 3)),
         pl.BlockSpec((tm, 1024), lambda i: (i, 4)), row(1024), row(1024), row(512), hsp,
         const((1, 512)), const((N_CHIP, 512, 256)), const((N_CHIP, 512, 256)), const((N_CHIP, 256, 1024))],
        [row(1024), row(1024), row(512), row(2048), row(512), hsp, const((1, 512))],
        [jax.ShapeDtypeStruct((T, 1024), BF), jax.ShapeDtypeStruct((T, 1024), BF),
         jax.ShapeDtypeStruct((T, 512), BF), jax.ShapeDtypeStruct((T, 2048), BF),
         jax.ShapeDtypeStruct((T, 512), F32), jax.ShapeDtypeStruct((FOX_H, T, LANE), BF),
         jax.ShapeDtypeStruct((1, 512), F32)],
        [], VMEM_BIG, (dx2, z_a, z_a, z_a, y_r, y_f, o_raw, o_fox, g_ret, w_ro, w_fo, w_out), push)


def _ret_bwd(d_o, qr, kr, z_a, states, cos_t, sin_t, consts, tt=512, push=None):
    T = z_a.shape[0]
    nt = T // tt
    nch = tt // CHUNK
    decay, zeta, xi, gcb = consts

    def body(do_ref, q_ref, k_ref, v_ref, st_ref, cos_ref, sin_ref, d_ref, ze_ref, xi_ref, gc_ref, dz_ref, g_sc):
        i = pl.program_id(0)

        @pl.when(i == 0)
        def _():
            g_sc[...] = jnp.zeros(g_sc.shape, F32)

        for c in reversed(range(nch)):
            rows = slice(c * CHUNK, (c + 1) * CHUNK)
            cosv, sinv = cos_ref[rows, :], sin_ref[rows, :]
            dq_parts, dk_parts = [], []
            for h in range(RET_H):
                cols = slice(h * RET_DV, (h + 1) * RET_DV)
                q, k = q_ref[h, rows, :], k_ref[h, rows, :]
                v32 = v_ref[rows, cols].astype(F32)
                vb = v32.astype(BF)
                r = st_ref[h, rows, :]
                g = g_sc[h]
                gb = g.astype(BF)
                d_o = do_ref[rows, cols]
                dob = d_o.astype(BF)
                dox = (d_o * xi_ref[h]).astype(BF)
                dec = d_ref[h]
                s = (_nt(q, k) * dec).astype(BF)
                ds = (_nt(dob, vb) * dec).astype(BF)
                dv = _tn(s, dob) + ze_ref[h] * _nn(k, gb)
                dq = _nn(ds, k) + _nt(dox, r.astype(BF))
                dk = _tn(ds, q) + _nt((v32 * ze_ref[h]).astype(BF), gb)
                g_sc[h] = gc_ref[h] * g + _tn(q, dox)
                dq_parts.append((dq * cosv - _swap32(dq) * sinv)[:, :64])
                dk_parts.append(((dk * cosv - _swap32(dk) * sinv) * 0.125)[:, :64])
                dz_ref[rows, 512 + h * RET_DV:512 + (h + 1) * RET_DV] = dv.astype(BF)
            dz_ref[rows, 0:256] = jnp.concatenate(dq_parts, axis=-1).astype(BF)
            dz_ref[rows, 256:512] = jnp.concatenate(dk_parts, axis=-1).astype(BF)

    rev = lambda i: nt - 1 - i
    hspec = pl.BlockSpec((RET_H, tt, LANE), lambda i: (0, rev(i), 0))
    cspec = pl.BlockSpec((RET_H, CHUNK, LANE), lambda i: (0, 0, 0))
    tab = pl.BlockSpec((tt, LANE), lambda i: (rev(i), 0))
    (dz,), lands = _hosted_call(
        body, "ret_bwd", (nt,),
        [pl.BlockSpec((tt, 512), lambda i: (rev(i), 0)), hspec, hspec,
         pl.BlockSpec((tt, 512), lambda i: (rev(i), 1)), hspec, tab, tab, cspec, cspec, cspec, cspec],
        [pl.BlockSpec((tt, 1024), lambda i: (rev(i), 0))], [jax.ShapeDtypeStruct((T, 1024), BF)],
        [pltpu.VMEM((RET_H, CHUNK, LANE), F32)], VMEM_BIG,
        (d_o, qr, kr, z_a, states, cos_t, sin_t, decay, zeta, xi, gcb), push)
    return dz, lands


def _fox_bwd(end_both, end_last, q2, k, v, do, sub=FOX_SUB):
    H, T, _ = k.shape
    tb = 2 * sub

    def body(eb_ref, el_ref, q_ref, do_ref, k_ref, v_ref, dq_ref, dk_ref, dv_ref, dk_sc, dv_sc):
        j = pl.program_id(1)
        n_both = eb_ref[pl.program_id(0), j]
        n_last = el_ref[pl.program_id(0), j]

        @pl.when(j == 0)
        def _():
            dq_ref[...] = jnp.zeros(dq_ref.shape, F32)

        dk_sc[...] = jnp.zeros(dk_sc.shape, F32)
        dv_sc[...] = jnp.zeros(dv_sc.shape, F32)
        krow = lax.broadcasted_iota(jnp.int32, (tb, sub), 0)
        qcol = lax.broadcasted_iota(jnp.int32, (tb, sub), 1)

        def step(i, r0, r1, shift):
            off = pl.multiple_of(i * sub, sub)
            qq = q_ref[pl.ds(off, sub), :]
            dd = do_ref[pl.ds(off, sub), :]
            kk, vv = k_ref[r0:r1, :], v_ref[r0:r1, :]
            p = jnp.exp(_nt(kk, qq))
            if shift is not None:
                p = jnp.where(qcol[0:r1 - r0, :] + shift >= krow[0:r1 - r0, :], p, 0.0)
            ds = (p * _nt(vv, dd)).astype(BF)
            dv_sc[r0:r1, :] += _nn(p.astype(BF), dd)
            dk_sc[r0:r1, :] += _nn(ds, qq)
            dq_ref[pl.ds(off, sub), :] += _tn(ds, kk)

        step(2 * j, 0, sub, 0)
        step(2 * j + 1, 0, tb, sub)

        def both_body(i, carry):
            step(i, 0, tb, None)
            return carry

        def last_body(i, carry):
            step(i, sub, tb, None)
            return carry

        lax.fori_loop(2 * j + 2, n_both, both_body, 0)
        lax.fori_loop(n_both, n_last, last_body, 0)
        dk_ref[...] = dk_sc[...]
        dv_ref[...] = dv_sc[...]

    blk = pl.BlockSpec((None, tb, LANE), lambda h, j, eb, el: (h, j, 0))
    full = pl.BlockSpec((None, T, LANE), lambda h, j, eb, el: (h, 0, 0))
    shp = jax.ShapeDtypeStruct((H, T, LANE), F32)
    return pl.pallas_call(
        body, name="fox_bwd",
        grid_spec=pltpu.PrefetchScalarGridSpec(
            num_scalar_prefetch=2, grid=(H, T // tb), in_specs=[full, full, blk, blk], out_specs=[full, blk, blk],
            scratch_shapes=[pltpu.VMEM((tb, LANE), F32), pltpu.VMEM((tb, LANE), F32)]),
        out_shape=[shp, shp, shp],
        compiler_params=_params(("arbitrary", "arbitrary"), VMEM_BIG),
    )(end_both, end_last, q2, do, k, v)


def _fox_post_bwd(dq, dk, dv, z_a, z_ff, b_f, g_q, g_k, tm=256, push=None):
    T = z_a.shape[0]
    nt = T // tm

    def body(dq_ref, dk_ref, dv_ref, zf_ref, zff_ref, b_ref, g_ref, sc_ref, seg_ref, segt_ref,
             dz_ref, dff_ref, dg_ref, db_ref, carry):
        i = pl.program_id(0)

        @pl.when(i == 0)
        def _():
            carry[...] = jnp.zeros(carry.shape, F32)
            dg_ref[...] = jnp.zeros(dg_ref.shape, F32)
            db_ref[...] = jnp.zeros(db_ref.shape, F32)

        lane = lax.broadcasted_iota(jnp.int32, (tm, LANE), 1)
        dcm = jnp.zeros((tm, LANE), F32)
        for h in range(FOX_H):
            dcm = jnp.where(lane == h, dq_ref[h][:, L_CQ:L_CQ + 1] - dk_ref[h][:, L_CK:L_CK + 1], dcm)

        def seg_mean(v):
            return sum(_nn(t, seg_ref[...]) for t in _split3(v)) * (1.0 / FOX_D)

        def seg_bcast(v):
            return sum(_nn(t, segt_ref[...]) for t in _split3(v))

        x = zf_ref[:, :1024].astype(F32)
        dy = jnp.concatenate([dq_ref[h][:, :FOX_D] for h in range(FOX_H)]
                             + [dk_ref[h][:, :FOX_D] for h in range(FOX_H)], axis=-1) * sc_ref[...]
        rb = seg_bcast(lax.rsqrt(seg_mean(x * x) + EPS))
        xn = x * rb
        dg_ref[...] += jnp.sum(dy * xn, axis=0, keepdims=True)
        dxn = dy * g_ref[...]
        dz_ref[:, :1024] = (rb * (dxn - xn * seg_bcast(seg_mean(dxn * xn)))).astype(BF)
        dz_ref[:, 1024:] = jnp.concatenate([dv_ref[h][:, :FOX_D] for h in range(FOX_H)], axis=-1).astype(BF)

        row = lax.broadcasted_iota(jnp.int32, (tm, tm), 0)
        col = lax.broadcasted_iota(jnp.int32, (tm, tm), 1)
        tri = (row <= col).astype(BF)
        hi, mid, lo = _split3(dcm)
        dlogf = _nn(tri, hi) + _nn(tri, mid) + _nn(tri, lo) + carry[...]
        carry[...] = dlogf[0:1, :]
        dff = jnp.where(lane < FOX_H, dlogf * _sigmoid(-(zff_ref[...] + b_ref[...])), 0.0)
        dff_ref[...] = dff.astype(BF)
        db_ref[...] += jnp.sum(dff, axis=0, keepdims=True)

    rev = lambda i: nt - 1 - i
    hsp = pl.BlockSpec((FOX_H, tm, LANE), lambda i: (0, rev(i), 0))
    const = lambda r, w: pl.BlockSpec((r, w), lambda i: (0, 0))
    seg = _segment_matrix()
    g_all = jnp.concatenate([jnp.tile(g_q, (1, FOX_H)), jnp.tile(g_k, (1, FOX_H))], axis=1)
    scale = jnp.asarray(np.concatenate([np.full((1, 512), 0.125, np.float32), np.ones((1, 512), np.float32)], axis=1))
    (dz, dff, dg, db), lands = _hosted_call(
        body, "fox_post_bwd", (nt,),
        [hsp, hsp, hsp, pl.BlockSpec((tm, 1536), lambda i: (rev(i), 1)),
         pl.BlockSpec((tm, LANE), lambda i: (rev(i), 0)), const(1, LANE), const(1, 1024), const(1, 1024),
         const(1024, LANE), const(LANE, 1024)],
        [pl.BlockSpec((tm, 1536), lambda i: (rev(i), 0)), pl.BlockSpec((tm, LANE), lambda i: (rev(i), 0)),
         const(1, 1024), const(1, LANE)],
        [jax.ShapeDtypeStruct((T, 1536), BF), jax.ShapeDtypeStruct((T, LANE), BF),
         jax.ShapeDtypeStruct((1, 1024), F32), jax.ShapeDtypeStruct((1, LANE), F32)],
        [pltpu.VMEM((1, LANE), F32)], VMEM_BIG, (dq, dk, dv, z_a, z_ff, b_f, g_all, scale, seg, seg.T), push)
    dg_heads = dg.reshape(2, FOX_H, FOX_D).sum(axis=1)
    return (dz, dff, dg_heads[0:1], dg_heads[1:2], db), lands


def _in_bwd(dz_ret, dz_gt, dz_fox, dz_a, dz_ff, w_a, w_ff, x, g_mix, dx2, tm=256, push=None):
    T = x.shape[0]

    def body(r_ref, t_ref, f_ref, a_ref, ff_ref, wa_ref, wf_ref, x_ref, g_ref, dx2_ref, dx_ref, dg_ref):
        i = pl.program_id(0)

        @pl.when(i == 0)
        def _():
            dg_ref[...] = jnp.zeros(dg_ref.shape, F32)

        dh = (_nt(r_ref[...], wa_ref[:, C_RET:C_GT]) + _nt(t_ref[...], wa_ref[:, C_GT:C_FOX])
              + _nt(f_ref[...], wa_ref[:, C_FOX:C_A]) + _nt(a_ref[...], wa_ref[:, C_A:C_END])
              + _nt(ff_ref[...], wf_ref[...]))
        xv = x_ref[...]
        r = lax.rsqrt(jnp.mean(xv * xv, axis=-1, keepdims=True) + EPS)
        xn = xv * r
        dg_ref[...] += jnp.sum(dh * xn, axis=0, keepdims=True)
        dxn = dh * g_ref[...]
        dx_ref[...] = dx2_ref[...] + r * (dxn - xn * jnp.mean(dxn * xn, axis=-1, keepdims=True))

    row = lambda w: pl.BlockSpec((tm, w), lambda i: (i, 0))
    const = lambda shp: pl.BlockSpec(shp, lambda i: (0,) * len(shp))
    return _hosted_call(
        body, "in_bwd", (T // tm,),
        [row(1024), row(512), row(1536), row(2048), row(LANE), const((D_MODEL, C_END)),
         const((D_MODEL, LANE)), row(1024), const((1, 1024)), row(1024)],
        [row(1024), const((1, 1024))],
        [jax.ShapeDtypeStruct((T, 1024), F32), jax.ShapeDtypeStruct((1, 1024), F32)],
        [], VMEM_BIG, (dz_ret, dz_gt, dz_fox, dz_a, dz_ff, w_a, w_ff, x, g_mix, dx2), push)


def _mesh_pos():
    return lax.axis_index("x"), lax.axis_index("y"), lax.axis_index("c")


def _staged_place(src, name):
    stacked = src.ndim == 3
    R, C = src.shape[-2:]
    tr = _row_tile(R, 128, 16)
    n = R // tr
    assert n >= 2

    def body(s_ref, o_ref, buf, sem):
        i = pl.program_id(0)
        slot = i % 2
        x, y, _ = _mesh_pos()
        kme = 2 * x + y

        def out_copy(s, step):
            return pltpu.make_async_copy(buf.at[s], o_ref.at[kme, pl.ds(pl.multiple_of(step * tr, tr), tr), :], sem.at[s])

        @pl.when(i >= 2)
        def _():
            out_copy(slot, i - 2).wait()

        buf[slot] = (s_ref[kme] if stacked else s_ref[...]).astype(BF)
        out_copy(slot, i).start()

        @pl.when(i == n - 1)
        def _():
            out_copy(1 - slot, i - 1).wait()
            out_copy(slot, i).wait()

    in_spec = (pl.BlockSpec((N_CHIP, tr, C), lambda i: (0, i, 0)) if stacked else pl.BlockSpec((tr, C), lambda i: (i, 0)))
    return pl.pallas_call(
        body, name=name, grid=(n,), in_specs=[in_spec], out_specs=pl.BlockSpec(memory_space=pl.ANY),
        out_shape=jax.ShapeDtypeStruct((N_CHIP, R, C), BF),
        scratch_shapes=[pltpu.VMEM((2, tr, C), BF), pltpu.SemaphoreType.DMA((2,))],
        compiler_params=_params(("arbitrary",)),
    )(src)


def _push_copies(src, land, send_sem, recv_sem, receiving):
    x, y, c = _mesh_pos()
    kme = 2 * x + y
    cps = []
    for w in range(len(land)):
        for j, (px, py) in enumerate([(1 - x, y), (x, 1 - y), (1 - x, 1 - y)]):
            kpeer = 2 * px + py
            cps.append(pltpu.make_async_remote_copy(
                src_ref=land[w].at[kme] if src is None else src[w].at[kpeer],
                dst_ref=land[w].at[kpeer if receiving else kme],
                send_sem=send_sem.at[3 * w + j], recv_sem=recv_sem.at[3 * w + j],
                device_id=(px, py, c), device_id_type=MESH))
    return cps


def _gather_two_level(stack, name):
    _, R, C = stack.shape
    hr = R // 2

    def body(_, land, send_sem, recv_sem):
        x, y, c = _mesh_pos()
        kme = 2 * x + y
        chips = [(1 - x, y), (x, 1 - y), (1 - x, 1 - y)]

        def rows(k, core):
            return land.at[k, pl.ds(pl.multiple_of(core * hr, hr), hr), :]

        def copy(idx, k, core, to):
            return pltpu.make_async_remote_copy(src_ref=rows(k, core), dst_ref=rows(k, core), send_sem=send_sem.at[idx],
                                                recv_sem=recv_sem.at[idx], device_id=to, device_id_type=MESH)

        first = [copy(j, kme, c, (px, py, c)) for j, (px, py) in enumerate(chips)]
        for cp in first:
            cp.start()
        passed = [copy(3 + j, 2 * px + py, c, (x, y, 1 - c)) for j, (px, py) in enumerate(chips)]
        for j, (px, py) in enumerate(chips):
            copy(j, 2 * px + py, c, (px, py, c)).wait_recv()
            passed[j].start()
        for j, (px, py) in enumerate(chips):
            copy(3 + j, 2 * px + py, 1 - c, (x, y, 1 - c)).wait_recv()
        for cp in first + passed:
            cp.wait_send()

    anyspec = pl.BlockSpec(memory_space=pl.ANY)
    return pl.pallas_call(
        body, name=name, in_specs=[anyspec], out_specs=anyspec,
        out_shape=jax.ShapeDtypeStruct(stack.shape, stack.dtype), input_output_aliases={0: 0},
        scratch_shapes=[pltpu.SemaphoreType.DMA((6,)), pltpu.SemaphoreType.DMA((6,))],
    )(stack)


def _gather_small(small):
    def body(sv, svo, ssend, srecv, sloc):
        x, y, c = _mesh_pos()
        me = 4 * x + 2 * y + c
        flips = [(b >> 2 & 1, b >> 1 & 1, b & 1) for b in range(1, 8)]
        others = [(1 - x if fx else x, 1 - y if fy else y, 1 - c if fc else c) for fx, fy, fc in flips]
        local = pltpu.make_async_copy(sv, svo.at[me], sloc)
        local.start()
        sends = []
        for j, (px, py, pc) in enumerate(others):
            cp = pltpu.make_async_remote_copy(
                src_ref=sv, dst_ref=svo.at[me], send_sem=ssend.at[j], recv_sem=srecv.at[j],
                device_id=(px, py, pc), device_id_type=MESH)
            cp.start()
            sends.append(cp)
        for j, (px, py, pc) in enumerate(others):
            pltpu.make_async_remote_copy(
                src_ref=sv, dst_ref=svo.at[4 * px + 2 * py + pc], send_sem=ssend.at[j], recv_sem=srecv.at[j],
                device_id=(px, py, pc), device_id_type=MESH).wait_recv()
        for cp in sends:
            cp.wait_send()
        local.wait()

    anyspec = pl.BlockSpec(memory_space=pl.ANY)
    return pl.pallas_call(
        body, name="gather_small", in_specs=[anyspec], out_specs=anyspec,
        out_shape=jax.ShapeDtypeStruct((8,) + small.shape, small.dtype),
        scratch_shapes=[pltpu.SemaphoreType.DMA((7,)), pltpu.SemaphoreType.DMA((7,)), pltpu.SemaphoreType.DMA],
    )(small)


def _sibling_exchange(arrs):
    n = len(arrs)

    def body(*refs):
        ins, outs = refs[:n], refs[n:2 * n]
        send_sems, recv_sems = refs[2 * n:]
        x, y, c = _mesh_pos()
        cps = [pltpu.make_async_remote_copy(
            src_ref=ins[w], dst_ref=outs[w], send_sem=send_sems.at[w], recv_sem=recv_sems.at[w],
            device_id=(x, y, 1 - c), device_id_type=MESH) for w in range(n)]
        for cp in cps:
            cp.start()
        for cp in cps:
            cp.wait_recv()
        for cp in cps:
            cp.wait_send()

    anyspec = pl.BlockSpec(memory_space=pl.ANY)
    return pl.pallas_call(
        body, name="sibling_exchange",
        in_specs=[anyspec] * n, out_specs=[anyspec] * n,
        out_shape=[jax.ShapeDtypeStruct(a.shape, a.dtype) for a in arrs],
        scratch_shapes=[pltpu.SemaphoreType.DMA((n,)), pltpu.SemaphoreType.DMA((n,))],
    )(*arrs)


def _sum_stack(own, recv, name):
    _, R, C = recv.shape
    tr = _row_tile(R, 256, 16)

    def body(g_ref, r_ref, o_ref):
        x, y, _ = _mesh_pos()
        kme = 2 * x + y
        acc = g_ref[kme].astype(F32)
        for d in range(1, N_CHIP):
            acc = acc + r_ref[(kme + d) % N_CHIP].astype(F32)
        o_ref[...] = acc

    spec = pl.BlockSpec((N_CHIP, tr, C), lambda i: (0, i, 0))
    return pl.pallas_call(
        body, name=name, grid=(R // tr,), in_specs=[spec, spec],
        out_specs=pl.BlockSpec((tr, C), lambda i: (i, 0)),
        out_shape=jax.ShapeDtypeStruct((R, C), F32),
        compiler_params=_params(("parallel",)),
    )(own, recv)


def _adam_math(w, g, m, v):
    m2 = ADAM_B1 * m + (1.0 - ADAM_B1) * g
    v2 = ADAM_B2 * v + (1.0 - ADAM_B2) * (g * g)
    m_hat = m2 / (1.0 - ADAM_B1 ** ADAM_STEP)
    v_hat = v2 / (1.0 - ADAM_B2 ** ADAM_STEP)
    delta = -ADAM_LR * (m_hat / (jnp.sqrt(v_hat) + ADAM_EPS) + ADAM_WD * w)
    return delta, m2, v2


def _adamw(w, m, v, s0, s1, name):
    R, C = w.shape
    tr = _row_tile(R, 128, 8)

    def body(w_ref, m_ref, v_ref, a_ref, b_ref, g_ref, d_ref, m2_ref, v2_ref):
        g = a_ref[...] + b_ref[...]
        delta, m2, v2 = _adam_math(w_ref[...], g, m_ref[...], v_ref[...])
        g_ref[...] = g
        d_ref[...] = delta
        m2_ref[...] = m2
        v2_ref[...] = v2

    spec = pl.BlockSpec((tr, C), lambda i: (i, 0))
    shp = jax.ShapeDtypeStruct((R, C), F32)
    return pl.pallas_call(
        body, name=name, grid=(R // tr,), in_specs=[spec] * 5, out_specs=[spec] * 4, out_shape=[shp] * 4,
        compiler_params=_params(("parallel",), VMEM_BIG),
    )(w, m, v, s0, s1)


def _adamw_small(w, m, v, gathered):
    def body(w_ref, m_ref, v_ref, s_ref, g_ref, d_ref, m2_ref, v2_ref):
        g = s_ref[0]
        for d in range(1, 8):
            g = g + s_ref[d]
        delta, m2, v2 = _adam_math(w_ref[...], g, m_ref[...], v_ref[...])
        g_ref[...] = g
        d_ref[...] = delta
        m2_ref[...] = m2
        v2_ref[...] = v2

    shp = jax.ShapeDtypeStruct(w.shape, F32)
    return pl.pallas_call(body, name="adamw_small", out_shape=[shp] * 4)(w, m, v, gathered)


SMALL = (("g_mix", 1024), ("g_ffn", 1024), ("g_ret_norm", 512), ("g_fox_q", 64), ("g_fox_k", 64), ("b_forget", 8))
SMALL_W = 3072


def _pack_small(parts):
    cols = []
    for (name, n) in SMALL:
        p = parts[name].reshape(1, -1)[:, :n]
        pad = -n % LANE
        cols.append(jnp.pad(p, ((0, 0), (0, pad))) if pad else p)
    used = sum(c.shape[1] for c in cols)
    cols.append(jnp.zeros((1, SMALL_W - used), F32))
    return jnp.concatenate(cols, axis=1)


def _unpack_small(vec):
    out, off = {}, 0
    for (name, n) in SMALL:
        out[name] = vec[:, off:off + n]
        off += n + (-n % LANE)
    return out


def kernel(x, g_mix, w_in, b_forget, g_ret_norm, w_ret_o, g_fox_q, g_fox_k, w_fox_o, w_out, g_ffn, w_gate, w_up, w_down, loss_target, m_g_mix, m_w_in, m_b_forget, m_g_ret_norm, m_w_ret_o, m_g_fox_q, m_g_fox_k, m_w_fox_o, m_w_out, m_g_ffn, m_w_gate, m_w_up, m_w_down, v_g_mix, v_w_in, v_b_forget, v_g_ret_norm, v_w_ret_o, v_g_fox_q, v_g_fox_k, v_w_fox_o, v_w_out, v_g_ffn, v_w_gate, v_w_up, v_w_down):
    T = x.shape[1]
    xs = x[0]
    tgt = loss_target[0]
    big_names = ("w_in", "w_ret_o", "w_fox_o", "w_out", "w_gate", "w_up", "w_down")
    tr = lambda a: jnp.swapaxes(a[0], 0, 1)
    big_w = dict(w_in=w_in[0], w_ret_o=w_ret_o[0], w_fox_o=w_fox_o[0], w_out=w_out[0], w_gate=tr(w_gate),
                 w_up=tr(w_up), w_down=w_down[0])
    big_m = dict(w_in=m_w_in[0], w_ret_o=m_w_ret_o[0], w_fox_o=m_w_fox_o[0], w_out=m_w_out[0], w_gate=tr(m_w_gate),
                 w_up=tr(m_w_up), w_down=m_w_down[0])
    big_v = dict(w_in=v_w_in[0], w_ret_o=v_w_ret_o[0], w_fox_o=v_w_fox_o[0], w_out=v_w_out[0], w_gate=tr(v_w_gate),
                 w_up=tr(v_w_up), w_down=v_w_down[0])
    small_w = dict(g_mix=g_mix, g_ffn=g_ffn, g_ret_norm=g_ret_norm, g_fox_q=g_fox_q, g_fox_k=g_fox_k, b_forget=b_forget)
    small_m = dict(g_mix=m_g_mix, g_ffn=m_g_ffn, g_ret_norm=m_g_ret_norm, g_fox_q=m_g_fox_q, g_fox_k=m_g_fox_k,
                   b_forget=m_b_forget)
    small_v = dict(g_mix=v_g_mix, g_ffn=v_g_ffn, g_ret_norm=v_g_ret_norm, g_fox_q=v_g_fox_q, g_fox_k=v_g_fox_k,
                   b_forget=v_b_forget)

    stacks = {n: _staged_place(big_w[n], "place_" + n) for n in big_names}
    s_in = _gather_two_level(stacks["w_in"], "gather_w_in")
    w_a, w_ff = _assemble_w_in(s_in)
    b_pad = jnp.pad(b_forget, ((0, 0), (0, LANE - FOX_H)))
    cos_t, sin_t = _rope_tables(T)
    consts = _ret_consts()

    h = _rms_cast(xs, g_mix)
    z_a, (s_gate, s_up) = _mm_nn(h, w_a, "proj_in", BF, push=(None, [stacks["w_gate"], stacks["w_up"]]))
    z_ff, _ = _mm_nn(h, w_ff, "proj_ff", F32)
    (qr, kr, qf, kf, vf, c_cum, nmax), (s_down, s_ro, s_fo, s_out) = _mix_prep(
        z_a, z_ff, cos_t, sin_t, b_pad, g_fox_q, g_fox_k,
        push=(None, [stacks["w_down"], stacks["w_ret_o"], stacks["w_fox_o"], stacks["w_out"]]))
    jlo, end_both, end_last = _prune_tables(c_cum, nmax, FOX_SUB)
    o_raw, u_r, states = _ret_fwd(qr, kr, z_a, g_ret_norm, consts)
    o_fox, q2 = _fox_fwd(jlo, qf, kf, vf)
    y_r, y_f, mrg, x2, h2, o_cat = _merge_out(u_r, o_fox, z_a, xs, g_ffn, s_ro, s_fo, s_out)
    sa, sb, act, dy, loss_vec = _ffn_fwd(h2, x2, tgt, s_gate, s_up, s_down)
    loss = lax.psum(0.5 / D_MODEL * jnp.sum(loss_vec), ("x", "y", "c"))

    def scatter_job(grads):
        return (grads, [lax.empty(g.shape, g.dtype) for g in grads])

    dgp, dup, dx2, dg_ffn = _ffn_bwd(dy, sa, sb, x2, g_ffn, s_gate, s_up, s_down)
    g_gate, g_up, g_down = (_grad_astack(dgp, h2, "gw_gate"), _grad_astack(dup, h2, "gw_up"),
                            _grad_astack(act, dy, "gw_down"))
    (d_yr, d_yf, dz_gt, dz_a, d_o, do_fox, dg_ret), (r_gate, r_up) = _out_bwd(
        dx2, z_a, y_r, y_f, o_raw, o_fox, g_ret_norm, s_ro, s_fo, s_out,
        push=scatter_job([g_gate, g_up]))
    dz_ret, (r_down,) = _ret_bwd(d_o, qr, kr, z_a, states, cos_t, sin_t, consts, push=scatter_job([g_down]))
    dq_f, dk_f, dv_f = _fox_bwd(end_both, end_last, q2, kf, vf, do_fox)
    g_mid = [_grad_colstack(u_r, d_yr, "gw_ret_o", 256), _grad_colstack(o_cat, d_yf, "gw_fox_o", 256),
             _grad_plain(mrg, dx2, "gw_out", BF).reshape(N_CHIP, 256, D_MODEL)]
    (dz_fox, dz_ff, dg_q, dg_k, db_f), (r_ro, r_fo, r_out) = _fox_post_bwd(
        dq_f, dk_f, dv_f, z_a, z_ff, b_pad, g_fox_q, g_fox_k, push=scatter_job(g_mid))
    g_in = _pack_g_in(_grad_plain(h, dz_ret, "gw_in_ret", F32), _grad_plain(h, dz_gt, "gw_in_gt", F32),
                      _grad_plain(h, dz_fox, "gw_in_fox", F32, tn=1536),
                      _grad_plain(h, dz_a, "gw_in_a", F32, tk=1024, tn=2048),
                      _grad_plain(h, dz_ff, "gw_in_ff", F32))
    (grad_x, dg_mix), (r_in,) = _in_bwd(dz_ret, dz_gt, dz_fox, dz_a, dz_ff, w_a, w_ff, xs, g_mix, dx2,
                                        push=scatter_job([g_in]))
    small_g = _pack_small(dict(g_mix=dg_mix, g_ffn=dg_ffn, g_ret_norm=dg_ret, g_fox_q=dg_q, g_fox_k=dg_k, b_forget=db_f))

    small_all = _gather_small(small_g)
    sums = [_sum_stack(g, r, "sum_" + n) for g, r, n in zip(
        [g_in] + g_mid + [g_gate, g_up, g_down], [r_in, r_ro, r_fo, r_out, r_gate, r_up, r_down], big_names)]
    sib = _sibling_exchange(sums)
    big_out = {n: _adamw(big_w[n], big_m[n], big_v[n], sums[i], sib[i], "adamw_" + n) for i, n in enumerate(big_names)}
    sg, sd, sm, sv = _adamw_small(_pack_small(small_w), _pack_small(small_m), _pack_small(small_v), small_all)
    small_out = [_unpack_small(t) for t in (sg, sd, sm, sv)]

    order = ("g_mix", "w_in", "b_forget", "g_ret_norm", "w_ret_o", "g_fox_q", "g_fox_k", "w_fox_o", "w_out", "g_ffn",
             "w_gate", "w_up", "w_down")
    outs = [loss, grad_x[None]]
    for idx in range(4):
        for n in order:
            if n in ("w_gate", "w_up"):
                outs.append(jnp.swapaxes(big_out[n][idx], 0, 1)[None])
            else:
                outs.append(big_out[n][idx][None] if n in big_out else small_out[idx][n])
    return tuple(outs)
```

```python
import functools
import math

import numpy as np
import jax
import jax.numpy as jnp
from jax import lax
from jax.experimental import pallas as pl
from jax.experimental.pallas import tpu as pltpu

F32 = jnp.float32
BF = jnp.bfloat16
MESH = pl.DeviceIdType.MESH

D_MODEL = 1024
D_FF = 2816
N_CHIP = 4
FF_SH = D_FF // N_CHIP
IN_COLS = 5128
IN_SH = IN_COLS // N_CHIP
RET_H, RET_DV = 4, 128
FOX_H, FOX_D = 8, 64
CHUNK = 128
EPS = 1e-6
NEG = -1e30
LANE = 128
C_RET, C_GT, C_FOX, C_A, C_END = 0, 1024, 1536, 3072, 5120
L_CQ, L_CK, L_LSE, L_MAX = 64, 67, 70, 73

ADAM_LR, ADAM_B1, ADAM_B2, ADAM_EPS, ADAM_WD, ADAM_STEP = 0.001, 0.9, 0.999, 1e-08, 0.01, 10
VMEM_BIG = 56 * 1024 * 1024
VMEM_HUGE = 60 * 1024 * 1024
GRAD_TK = 2048
FFN_TM = 512
FOX_SUB = 512


def _nn(a, b):
    return lax.dot_general(a, b, (((1,), (0,)), ((), ())), preferred_element_type=F32)


def _nt(a, b):
    return lax.dot_general(a, b, (((1,), (1,)), ((), ())), preferred_element_type=F32)


def _tn(a, b):
    return lax.dot_general(a, b, (((0,), (0,)), ((), ())), preferred_element_type=F32)


def _split3(x):
    hi = x.astype(BF)
    r = x - hi.astype(F32)
    mid = r.astype(BF)
    lo = (r - mid.astype(F32)).astype(BF)
    return hi, mid, lo


def _sigmoid(x):
    return 0.5 * jnp.tanh(0.5 * x) + 0.5


def _swap32(x):
    lane = lax.broadcasted_iota(jnp.int32, x.shape, 1)
    return jnp.where(lane < 32, pltpu.roll(x, 96, 1), pltpu.roll(x, 32, 1))


def _params(sem, vmem=None):
    return pltpu.CompilerParams(dimension_semantics=sem, vmem_limit_bytes=vmem)


def _row_tile(rows, cap, mult):
    return max(d for d in range(mult, cap + 1, mult) if rows % d == 0)


def _assemble_w_in(stack, tr=256):
    def body(s_ref, a_ref, f_ref):
        full = jnp.concatenate([s_ref[k].astype(F32) for k in range(N_CHIP)], axis=-1)
        a_ref[...] = jnp.concatenate([full[:, :3072], full[:, 3080:IN_COLS]], axis=-1).astype(BF)
        f_ref[...] = jnp.concatenate([full[:, 3072:3080], jnp.zeros((tr, LANE - FOX_H), F32)], axis=-1).astype(BF)

    return pl.pallas_call(
        body, name="assemble_w_in", grid=(D_MODEL // tr,),
        in_specs=[pl.BlockSpec((N_CHIP, tr, IN_SH), lambda i: (0, i, 0))],
        out_specs=[pl.BlockSpec((tr, C_END), lambda i: (i, 0)), pl.BlockSpec((tr, LANE), lambda i: (i, 0))],
        out_shape=[jax.ShapeDtypeStruct((D_MODEL, C_END), BF), jax.ShapeDtypeStruct((D_MODEL, LANE), BF)],
        compiler_params=_params(("parallel",), VMEM_BIG),
    )(stack)


def _pack_g_in(g_ret, g_gt, g_fox, g_a, g_ff, tr=256):
    def body(r_ref, t_ref, x_ref, a_ref, f_ref, o_ref):
        full = jnp.concatenate([r_ref[...], t_ref[...], x_ref[...], f_ref[...][:, :FOX_H], a_ref[...]], axis=-1)
        for k in range(N_CHIP):
            o_ref[k] = full[:, k * IN_SH:(k + 1) * IN_SH].astype(BF)

    def spec(w):
        return pl.BlockSpec((tr, w), lambda i: (i, 0))

    return pl.pallas_call(
        body, name="pack_g_in", grid=(D_MODEL // tr,),
        in_specs=[spec(1024), spec(512), spec(1536), spec(2048), spec(LANE)],
        out_specs=pl.BlockSpec((N_CHIP, tr, IN_SH), lambda i: (0, i, 0)),
        out_shape=jax.ShapeDtypeStruct((N_CHIP, D_MODEL, IN_SH), BF),
        compiler_params=_params(("parallel",), VMEM_BIG),
    )(g_ret, g_gt, g_fox, g_a, g_ff)


def _rms_cast(x, g, tm=512):
    T = x.shape[0]

    def body(x_ref, g_ref, o_ref):
        xv = x_ref[...]
        r = lax.rsqrt(jnp.mean(xv * xv, axis=-1, keepdims=True) + EPS)
        o_ref[...] = (xv * r * g_ref[...]).astype(BF)

    return pl.pallas_call(
        body, name="rms_cast", grid=(T // tm,),
        in_specs=[pl.BlockSpec((tm, D_MODEL), lambda i: (i, 0)), pl.BlockSpec((1, D_MODEL), lambda i: (0, 0))],
        out_specs=pl.BlockSpec((tm, D_MODEL), lambda i: (i, 0)),
        out_shape=jax.ShapeDtypeStruct((T, D_MODEL), BF),
        compiler_params=_params(("parallel",)),
    )(x, g)


def _hosted_call(body, name, grid, in_specs, out_specs, out_shape, scratch_shapes, vmem, args, push):
    sem = ("arbitrary",) * len(grid)
    if push is None:
        res = pl.pallas_call(body, name=name, grid=grid, in_specs=in_specs, out_specs=out_specs, out_shape=out_shape,
                             scratch_shapes=scratch_shapes, compiler_params=_params(sem, vmem))(*args)
        return list(res), []
    srcs, lands = push
    ns, nl, n_in, n_out = (0 if srcs is None else len(srcs)), len(lands), len(in_specs), len(out_specs)
    n_scr = len(scratch_shapes)

    def wrapped(*refs):
        pos = n_in + ns + nl
        ins, x_in = refs[:n_in], refs[n_in:pos]
        outs, x_out = refs[pos:pos + n_out], refs[pos + n_out:pos + n_out + nl]
        scr = refs[pos + n_out + nl:pos + n_out + nl + n_scr]
        ssem, rsem = refs[-2], refs[-1]
        src = None if srcs is None else x_in[:ns]
        ids = [pl.program_id(a) for a in range(len(grid))]
        first = functools.reduce(lambda p, q: p & q, [ids[a] == 0 for a in range(len(grid))])
        last = functools.reduce(lambda p, q: p & q, [ids[a] == grid[a] - 1 for a in range(len(grid))])

        @pl.when(first)
        def _():
            for cp in _push_copies(src, x_out, ssem, rsem, False):
                cp.start()

        body(*ins, *outs, *scr)

        @pl.when(last)
        def _():
            for cp in _push_copies(src, x_out, ssem, rsem, True):
                cp.wait_recv()
                cp.wait_send()

    anyspec = pl.BlockSpec(memory_space=pl.ANY)
    extra = ([] if srcs is None else list(srcs)) + list(lands)
    res = pl.pallas_call(
        wrapped, name=name, grid=grid,
        in_specs=list(in_specs) + [anyspec] * len(extra), out_specs=list(out_specs) + [anyspec] * nl,
        out_shape=list(out_shape) + [jax.ShapeDtypeStruct(a.shape, a.dtype) for a in lands],
        input_output_aliases={n_in + ns + i: n_out + i for i in range(nl)},
        scratch_shapes=list(scratch_shapes) + [pltpu.SemaphoreType.DMA((3 * nl,)), pltpu.SemaphoreType.DMA((3 * nl,))],
        compiler_params=_params(sem, vmem),
    )(*args, *extra)
    return list(res[:n_out]), list(res[n_out:])


def _mm_nn(a, b, name, out_dtype, tm=512, tn=1024, push=None):
    M, K = a.shape
    N = b.shape[1]
    tn = min(tn, N)

    def body(a_ref, b_ref, o_ref):
        o_ref[...] = _nn(a_ref[...], b_ref[...]).astype(o_ref.dtype)

    (out,), lands = _hosted_call(
        body, name, (N // tn, M // tm),
        [pl.BlockSpec((tm, K), lambda j, i: (i, 0)), pl.BlockSpec((K, tn), lambda j, i: (0, j))],
        [pl.BlockSpec((tm, tn), lambda j, i: (i, j))], [jax.ShapeDtypeStruct((M, N), out_dtype)], [], None, (a, b), push)
    return out, lands


def _mm_tn(a, b, name, grid, a_spec, b_spec, o_spec, out_shape, acc_shape):
    nk = grid[-1]

    def body(a_ref, b_ref, o_ref, acc):
        k = pl.program_id(len(grid) - 1)

        @pl.when(k == 0)
        def _():
            acc[...] = jnp.zeros(acc.shape, F32)

        acc[...] += _tn(a_ref[...].astype(BF), b_ref[...].astype(BF))

        @pl.when(k == nk - 1)
        def _():
            o_ref[...] = acc[...].astype(o_ref.dtype)

    return pl.pallas_call(
        body, name=name, grid=grid, in_specs=[a_spec, b_spec], out_specs=o_spec, out_shape=out_shape,
        scratch_shapes=[pltpu.VMEM(acc_shape, F32)],
        compiler_params=_params(("parallel",) * (len(grid) - 1) + ("arbitrary",), VMEM_BIG),
    )(a, b)


def _grad_plain(a, b, name, out_dtype, tk=GRAD_TK, tn=1024):
    T, M = a.shape
    N = b.shape[1]
    tn = min(tn, N)
    return _mm_tn(a, b, name, (N // tn, T // tk),
                  pl.BlockSpec((tk, M), lambda j, k: (k, 0)), pl.BlockSpec((tk, tn), lambda j, k: (k, j)),
                  pl.BlockSpec((M, tn), lambda j, k: (0, j)), jax.ShapeDtypeStruct((M, N), out_dtype), (M, tn))


def _grad_colstack(a, b, name, wcol, tk=GRAD_TK):
    T, M = a.shape
    N = b.shape[1]
    S = N // wcol
    nk = T // tk

    def body(a_ref, b_ref, o_ref, acc):
        k = pl.program_id(0)

        @pl.when(k == 0)
        def _():
            acc[...] = jnp.zeros(acc.shape, F32)

        acc[...] += _tn(a_ref[...], b_ref[...])

        @pl.when(k == nk - 1)
        def _():
            for s in range(S):
                o_ref[s] = acc[:, s * wcol:(s + 1) * wcol].astype(BF)

    return pl.pallas_call(
        body, name=name, grid=(nk,),
        in_specs=[pl.BlockSpec((tk, M), lambda k: (k, 0)), pl.BlockSpec((tk, N), lambda k: (k, 0))],
        out_specs=pl.BlockSpec((S, M, wcol), lambda k: (0, 0, 0)), out_shape=jax.ShapeDtypeStruct((S, M, wcol), BF),
        scratch_shapes=[pltpu.VMEM((M, N), F32)], compiler_params=_params(("arbitrary",), VMEM_BIG),
    )(a, b)


def _grad_astack(a, b, name, tk=1024):
    S, T, m = a.shape
    N = b.shape[1]
    nk = T // tk

    def body(a_ref, b_ref, o_ref, acc):
        k = pl.program_id(0)

        @pl.when(k == 0)
        def _():
            acc[...] = jnp.zeros(acc.shape, F32)

        bb = b_ref[...].astype(BF)
        for s in range(S):
            acc[s] += _tn(a_ref[s], bb)

        @pl.when(k == nk - 1)
        def _():
            o_ref[...] = acc[...].astype(BF)

    return pl.pallas_call(
        body, name=name, grid=(nk,),
        in_specs=[pl.BlockSpec((S, tk, m), lambda k: (0, k, 0)), pl.BlockSpec((tk, N), lambda k: (k, 0))],
        out_specs=pl.BlockSpec((S, m, N), lambda k: (0, 0, 0)), out_shape=jax.ShapeDtypeStruct((S, m, N), BF),
        scratch_shapes=[pltpu.VMEM((S, m, N), F32)], compiler_params=_params(("arbitrary",), VMEM_BIG),
    )(a, b)


def _rope_tables(T):
    half = 32
    pos = np.arange(T, dtype=np.float32)
    inv_freq = (np.float32(1.0) / (np.float32(10000.0) ** (np.arange(half, dtype=np.float32) / np.float32(half)))).astype(np.float32)
    ang = (pos[:, None] * inv_freq[None, :]).astype(np.float32)
    cos, sin = np.cos(ang).astype(np.float32), np.sin(ang).astype(np.float32)
    z = np.zeros((T, 64), np.float32)
    return (jnp.asarray(np.concatenate([cos, cos, z], axis=-1)), jnp.asarray(np.concatenate([-sin, sin, z], axis=-1)))


def _ret_consts():
    h = np.arange(RET_H, dtype=np.float32)
    log_g = np.log1p(-(np.float32(2.0) ** (-5.0 - h))).astype(np.float32)
    idx = np.arange(CHUNK, dtype=np.float32)
    diff = idx[:, None] - idx[None, :]
    decay = np.where(diff[None] >= 0, np.exp(np.maximum(diff, 0.0)[None] * log_g[:, None, None]), 0.0)
    zeta = np.exp((CHUNK - 1.0 - idx)[None, :] * log_g[:, None])
    xi = np.exp((idx + 1.0)[None, :] * log_g[:, None])
    gc = np.exp(CHUNK * log_g)
    bc = lambda v: np.broadcast_to(v[:, :, None], (RET_H, CHUNK, LANE)).astype(np.float32)
    gcb = np.broadcast_to(gc[:, None, None], (RET_H, CHUNK, LANE)).astype(np.float32)
    return (jnp.asarray(decay.astype(np.float32)), jnp.asarray(bc(zeta)), jnp.asarray(bc(xi)), jnp.asarray(gcb))


def _mix_prep(z_a, z_ff, cos_t, sin_t, b_f, g_q, g_k, tm=256, push=None):
    T = z_a.shape[0]

    def body(zqk_ref, zf_ref, zff_ref, cos_ref, sin_ref, b_ref, g_ref, seg_ref, segt_ref,
             qr_ref, kr_ref, qf_ref, kf_ref, vf_ref, c_ref, nmax_ref, carry):
        i = pl.program_id(0)

        @pl.when(i == 0)
        def _():
            carry[...] = jnp.zeros(carry.shape, F32)
            nmax_ref[...] = jnp.zeros(nmax_ref.shape, F32)

        lane = lax.broadcasted_iota(jnp.int32, (tm, LANE), 1)
        zpad = jnp.zeros((tm, 64), F32)
        cosv, sinv = cos_ref[...], sin_ref[...]
        zqk = zqk_ref[...].astype(F32)
        for h in range(RET_H):
            for src, dst, scale in ((0, qr_ref, 1.0), (256, kr_ref, 0.125)):
                xh = jnp.concatenate([zqk[:, src + 64 * h: src + 64 * h + 64], zpad], axis=-1)
                rot = xh * cosv + _swap32(xh) * sinv
                dst[h] = (rot * scale).astype(BF)

        lf_in = zff_ref[...] + b_ref[...]
        logf = jnp.minimum(lf_in, 0.0) - jnp.log(1.0 + jnp.exp(-jnp.abs(lf_in)))
        row = lax.broadcasted_iota(jnp.int32, (tm, tm), 0)
        col = lax.broadcasted_iota(jnp.int32, (tm, tm), 1)
        tri = (row >= col).astype(BF)
        hi, mid, lo = _split3(logf)
        cs = _nn(tri, hi) + _nn(tri, mid) + _nn(tri, lo) + carry[...]
        carry[...] = cs[tm - 1:tm, :]
        c_ref[...] = cs

        def seg_sum(v):
            return sum(_nn(t, seg_ref[...]) for t in _split3(v))

        zf = zf_ref[...].astype(F32)
        xqk = zf[:, :1024]
        rinv = lax.rsqrt(seg_sum(xqk * xqk) * (1.0 / FOX_D) + EPS)
        xn = xqk * sum(_nn(t, segt_ref[...]) for t in _split3(rinv)) * g_ref[...]
        nmax_ref[...] = jnp.maximum(nmax_ref[...], jnp.max(seg_sum(xn * xn), axis=0, keepdims=True))

        one = jnp.ones((tm, LANE), F32)
        for h in range(FOX_H):
            c = cs[:, h:h + 1]
            chi, cmid, clo = [t.astype(F32) for t in _split3(c)]
            qn = xn[:, 64 * h:64 * h + 64]
            kn = xn[:, 512 + 64 * h:512 + 64 * h + 64]
            vh = zf[:, 1024 + 64 * h:1024 + 64 * h + 64]
            qa = jnp.concatenate([qn, zpad], axis=-1)
            qa = jnp.where(lane == L_CQ, chi, jnp.where(lane == L_CQ + 1, cmid, jnp.where(lane == L_CQ + 2, clo, qa)))
            qa = jnp.where((lane >= L_CK) & (lane < L_CK + 3), one, qa)
            ka = jnp.concatenate([kn, zpad], axis=-1)
            ka = jnp.where(lane == L_CK, -chi, jnp.where(lane == L_CK + 1, -cmid, jnp.where(lane == L_CK + 2, -clo, ka)))
            ka = jnp.where(((lane >= L_CQ) & (lane < L_CQ + 3)) | ((lane >= L_LSE) & (lane < L_MAX + 3)), one, ka)
            va = jnp.concatenate([vh, zpad], axis=-1)
            va = jnp.where((lane >= 64) & (lane < 67), one, va)
            qf_ref[h] = qa.astype(BF)
            kf_ref[h] = ka.astype(BF)
            vf_ref[h] = va.astype(BF)

    hspec4 = pl.BlockSpec((RET_H, tm, LANE), lambda i: (0, i, 0))
    hspec8 = pl.BlockSpec((FOX_H, tm, LANE), lambda i: (0, i, 0))
    const = lambda r, w: pl.BlockSpec((r, w), lambda i: (0, 0))
    seg = _segment_matrix()
    g_all = jnp.concatenate([jnp.tile(g_q * 0.125, (1, FOX_H)), jnp.tile(g_k, (1, FOX_H))], axis=1)
    return _hosted_call(
        body, "mix_prep", (T // tm,),
        [pl.BlockSpec((tm, 512), lambda i: (i, 0)), pl.BlockSpec((tm, 1536), lambda i: (i, 1)),
         pl.BlockSpec((tm, LANE), lambda i: (i, 0)), pl.BlockSpec((tm, LANE), lambda i: (i, 0)),
         pl.BlockSpec((tm, LANE), lambda i: (i, 0)), const(1, LANE), const(1, 1024), const(1024, LANE), const(LANE, 1024)],
        [hspec4, hspec4, hspec8, hspec8, hspec8, pl.BlockSpec((tm, LANE), lambda i: (i, 0)), const(1, LANE)],
        [jax.ShapeDtypeStruct((RET_H, T, LANE), BF)] * 2 + [jax.ShapeDtypeStruct((FOX_H, T, LANE), BF)] * 3
        + [jax.ShapeDtypeStruct((T, LANE), F32), jax.ShapeDtypeStruct((1, LANE), F32)],
        [pltpu.VMEM((1, LANE), F32)], VMEM_BIG, (z_a, z_a, z_ff, cos_t, sin_t, b_f, g_all, seg, seg.T), push)


def _segment_matrix():
    m = np.zeros((2 * FOX_H * FOX_D, LANE), np.float32)
    m[np.arange(2 * FOX_H * FOX_D), np.arange(2 * FOX_H * FOX_D) // FOX_D] = 1.0
    return jnp.asarray(m, dtype=BF)


def _ret_fwd(qr, kr, z_a, g_ret, consts, tt=512):
    T = z_a.shape[0]
    nch = tt // CHUNK
    decay, zeta, xi, gcb = consts

    def body(q_ref, k_ref, v_ref, gt_ref, g_ref, d_ref, ze_ref, xi_ref, gc_ref, o_ref, u_ref, st_ref, r_sc):
        i = pl.program_id(0)

        @pl.when(i == 0)
        def _():
            r_sc[...] = jnp.zeros(r_sc.shape, F32)

        for c in range(nch):
            rows = slice(c * CHUNK, (c + 1) * CHUNK)
            for h in range(RET_H):
                cols = slice(h * RET_DV, (h + 1) * RET_DV)
                q, k = q_ref[h, rows, :], k_ref[h, rows, :]
                v32 = v_ref[rows, cols].astype(F32)
                r = r_sc[h]
                st_ref[h, rows, :] = r
                s = _nt(q, k) * d_ref[h]
                o = _nn(s.astype(BF), v32.astype(BF)) + _nn(q, r.astype(BF)) * xi_ref[h]
                r_sc[h] = gc_ref[h] * r + _tn(k, (v32 * ze_ref[h]).astype(BF))
                o_ref[rows, cols] = o
                mu = jnp.mean(o, axis=-1, keepdims=True)
                xc = o - mu
                on = xc * lax.rsqrt(jnp.mean(xc * xc, axis=-1, keepdims=True) + EPS)
                gt = gt_ref[rows, cols].astype(F32)
                u_ref[rows, cols] = (gt * _sigmoid(gt) * (on * g_ref[:, cols])).astype(BF)

    hspec = pl.BlockSpec((RET_H, tt, LANE), lambda i: (0, i, 0))
    cspec = pl.BlockSpec((RET_H, CHUNK, LANE), lambda i: (0, 0, 0))
    return pl.pallas_call(
        body, name="ret_fwd", grid=(T // tt,),
        in_specs=[hspec, hspec, pl.BlockSpec((tt, 512), lambda i: (i, 1)), pl.BlockSpec((tt, 512), lambda i: (i, 2)),
                  pl.BlockSpec((1, 512), lambda i: (0, 0)), cspec, cspec, cspec, cspec],
        out_specs=[pl.BlockSpec((tt, 512), lambda i: (i, 0)), pl.BlockSpec((tt, 512), lambda i: (i, 0)), hspec],
        out_shape=[jax.ShapeDtypeStruct((T, 512), F32), jax.ShapeDtypeStruct((T, 512), BF),
                   jax.ShapeDtypeStruct((RET_H, T, LANE), F32)],
        scratch_shapes=[pltpu.VMEM((RET_H, CHUNK, LANE), F32)],
        compiler_params=_params(("arbitrary",), VMEM_BIG),
    )(qr, kr, z_a, z_a, g_ret, decay, zeta, xi, gcb)


PRUNE_LOG = -110.0
TAME_LOGIT_SPAN = 60.0


def _prune_tables(c, nmax, sub):
    n = c.shape[0] // sub
    u = jnp.sqrt(nmax[0, :FOX_H] * nmax[0, FOX_H:2 * FOX_H]) * 1.02 + 0.5
    first = c[0::sub, :FOX_H].T
    last = c[sub - 1::sub, :FOX_H].T
    blk = jnp.arange(n, dtype=jnp.int32)
    needed = (2.0 * u[:, None, None] + first[:, :, None] - last[:, None, :] >= PRUNE_LOG) | (blk[None, :] >= blk[:, None])[None]
    jlo = jnp.argmax(needed, axis=2).astype(jnp.int32)

    def end_of(key_block):
        reach = jlo[:, None, :] <= key_block[None, :, None]
        return (n - jnp.argmax(reach[:, :, ::-1], axis=2)).astype(jnp.int32)

    sup = 2 * jnp.arange(n // 2, dtype=jnp.int32)
    end_last = end_of(sup + 1)
    end_both = jnp.clip(end_of(sup), sup[None, :] + 2, end_last)
    tame = (2.0 * u < TAME_LOGIT_SPAN).astype(jnp.int32)
    return jlo, end_both, end_last, tame


def _fox_fwd(jlo, tame, q, k, v, sub=FOX_SUB):
    H, T, _ = q.shape
    tb = 2 * sub

    def body(js_ref, tame_ref, q_ref, k_ref, v_ref, o_ref, q2_ref, mx_sc, acc_sc):
        i = pl.program_id(1)
        hd = pl.program_id(0)
        starts = [jnp.minimum(js_ref[hd, 2 * i], 2 * i), jnp.minimum(js_ref[hd, 2 * i + 1], 2 * i)]
        lane = lax.broadcasted_iota(jnp.int32, (sub, LANE), 1)
        row = lax.broadcasted_iota(jnp.int32, (sub, sub), 0)
        col = lax.broadcasted_iota(jnp.int32, (sub, sub), 1)
        causal = row >= col
        qs = [q_ref[0:sub, :], q_ref[sub:tb, :]]
        d0 = pl.multiple_of(i * tb, tb)
        d1 = pl.multiple_of(i * tb + sub, sub)
        k0, k1 = k_ref[pl.ds(d0, sub), :], k_ref[pl.ds(d1, sub), :]
        v0, v1 = v_ref[pl.ds(d0, sub), :], v_ref[pl.ds(d1, sub), :]

        def lane_max(s):
            m = s[:, 0:LANE]
            for c in range(1, s.shape[1] // LANE):
                m = jnp.maximum(m, s[:, c * LANE:(c + 1) * LANE])
            return m

        def put3(base, first, val):
            hi, mid, lo = _split3(val)
            return jnp.where(lane == first, hi, jnp.where(lane == first + 1, mid, jnp.where(lane == first + 2, lo, base)))

        def row_max():
            mx_sc[...] = jnp.full(mx_sc.shape, NEG, F32)
            for a in range(2):
                def max_body(j, carry, a=a):
                    kb = k_ref[pl.ds(pl.multiple_of(j * sub, sub), sub), :]
                    mx_sc[a] = jnp.maximum(mx_sc[a], lane_max(_nt(qs[a], kb)))
                    return carry

                lax.fori_loop(starts[a], 2 * i, max_body, 0)
            mx = [jnp.maximum(mx_sc[0], lane_max(jnp.where(causal, _nt(qs[0], k0), NEG))),
                  jnp.maximum(jnp.maximum(mx_sc[1], lane_max(_nt(qs[1], k0))),
                              lane_max(jnp.where(causal, _nt(qs[1], k1), NEG)))]
            return [jnp.max(t, axis=1, keepdims=True) for t in mx]

        def diag_logit():
            return [jnp.sum(qs[a].astype(F32) * kd.astype(F32), axis=1, keepdims=True) for a, kd in enumerate((k0, k1))]

        def finish(ms):
            qm = [put3(qs[a], L_MAX, -ms[a]) for a in range(2)]
            acc_sc[...] = jnp.zeros(acc_sc.shape, F32)
            for a in range(2):
                def acc_body(j, carry, a=a):
                    off = pl.multiple_of(j * sub, sub)
                    acc_sc[a] += _nn(jnp.exp(_nt(qm[a], k_ref[pl.ds(off, sub), :])).astype(BF), v_ref[pl.ds(off, sub), :])
                    return carry

                lax.fori_loop(starts[a], 2 * i, acc_body, 0)

            def pv(qa, kk, vv, masked):
                p = jnp.exp(_nt(qa, kk))
                if masked:
                    p = jnp.where(causal, p, 0.0)
                return _nn(p.astype(BF), vv)

            accs = [acc_sc[0] + pv(qm[0], k0, v0, True),
                    acc_sc[1] + pv(qm[1], k0, v0, False) + pv(qm[1], k1, v1, True)]
            for a in range(2):
                rows = slice(a * sub, (a + 1) * sub)
                l = accs[a][:, 64:65]
                o_ref[rows, :] = jnp.where(lane < 64, accs[a] / l, 0.0)
                q2_ref[rows, :] = put3(qs[a], L_LSE, -(ms[a] + jnp.log(l)))

        tame = tame_ref[hd] == 1

        @pl.when(tame)
        def _():
            finish(diag_logit())

        @pl.when(jnp.logical_not(tame))
        def _():
            finish(row_max())

    blk = pl.BlockSpec((None, tb, LANE), lambda h, i, js, tm_: (h, i, 0))
    full = pl.BlockSpec((None, T, LANE), lambda h, i, js, tm_: (h, 0, 0))
    return pl.pallas_call(
        body, name="fox_fwd",
        grid_spec=pltpu.PrefetchScalarGridSpec(
            num_scalar_prefetch=2, grid=(H, T // tb), in_specs=[blk, full, full], out_specs=[blk, blk],
            scratch_shapes=[pltpu.VMEM((2, sub, LANE), F32), pltpu.VMEM((2, sub, LANE), F32)]),
        out_shape=[jax.ShapeDtypeStruct((H, T, LANE), F32), jax.ShapeDtypeStruct((H, T, LANE), BF)],
        compiler_params=_params(("parallel", "arbitrary"), VMEM_BIG),
    )(jlo, tame, q, k, v)


def _merge_out(u_r, o_fox, z_a, x, g_ffn, w_ro, w_fo, w_out, tm=256):
    T = x.shape[0]

    def body(u_ref, of_ref, ar_ref, af_ref, x_ref, g_ref, wro_ref, wfo_ref, wout_ref,
             yr_ref, yf_ref, m_ref, x2_ref, h2_ref, oc_ref):
        u = u_ref[...]
        oc = jnp.concatenate([of_ref[h][:, :FOX_D] for h in range(FOX_H)], axis=-1).astype(BF)
        oc_ref[...] = oc
        yr = jnp.concatenate([_nn(u, wro_ref[k]) for k in range(N_CHIP)], axis=-1)
        yf = jnp.concatenate([_nn(oc, wfo_ref[k]) for k in range(N_CHIP)], axis=-1)
        yr_ref[...] = yr
        yf_ref[...] = yf
        m = (_sigmoid(ar_ref[...].astype(F32)) * yr + _sigmoid(af_ref[...].astype(F32)) * yf).astype(BF)
        m_ref[...] = m
        x2 = x_ref[...]
        for k in range(N_CHIP):
            x2 = x2 + _nn(m[:, 256 * k:256 * k + 256], wout_ref[k])
        x2_ref[...] = x2
        r = lax.rsqrt(jnp.mean(x2 * x2, axis=-1, keepdims=True) + EPS)
        h2_ref[...] = (x2 * r * g_ref[...]).astype(BF)

    row = lambda w: pl.BlockSpec((tm, w), lambda i: (i, 0))
    const = lambda shp: pl.BlockSpec(shp, lambda i: (0,) * len(shp))
    return pl.pallas_call(
        body, name="merge_out", grid=(T // tm,),
        in_specs=[row(512), pl.BlockSpec((FOX_H, tm, LANE), lambda i: (0, i, 0)),
                  pl.BlockSpec((tm, 1024), lambda i: (i, 3)), pl.BlockSpec((tm, 1024), lambda i: (i, 4)),
                  row(1024), const((1, 1024)), const((N_CHIP, 512, 256)), const((N_CHIP, 512, 256)),
                  const((N_CHIP, 256, 1024))],
        out_specs=[row(1024), row(1024), row(1024), row(1024), row(1024), row(512)],
        out_shape=[jax.ShapeDtypeStruct((T, 1024), F32), jax.ShapeDtypeStruct((T, 1024), F32),
                   jax.ShapeDtypeStruct((T, 1024), BF), jax.ShapeDtypeStruct((T, 1024), F32),
                   jax.ShapeDtypeStruct((T, 1024), BF), jax.ShapeDtypeStruct((T, 512), BF)],
        compiler_params=_params(("parallel",), VMEM_BIG),
    )(u_r, o_fox, z_a, z_a, x, g_ffn, w_ro, w_fo, w_out)


def _load_resident(hbm_refs, vmem_refs, sem):
    cps = [pltpu.make_async_copy(h, v, sem.at[i]) for i, (h, v) in enumerate(zip(hbm_refs, vmem_refs))]
    for cp in cps:
        cp.start()
    for cp in cps:
        cp.wait()


def _ffn_fwd(h2, x2, tgt, w_gate, w_up, w_down, tm=FFN_TM):
    T = h2.shape[0]

    def body(h_ref, x2_ref, t_ref, wg_hbm, wu_hbm, wd_hbm, a_ref, b_ref, act_ref, dy_ref, ls_ref, wg, wu, wd, sem):
        @pl.when(pl.program_id(0) == 0)
        def _():
            _load_resident((wg_hbm, wu_hbm, wd_hbm), (wg, wu, wd), sem)
            ls_ref[...] = jnp.zeros(ls_ref.shape, F32)

        h = h_ref[...]
        err = x2_ref[...] - t_ref[...]
        for k in range(N_CHIP):
            gp = _nt(h, wg[k])
            up = _nt(h, wu[k])
            sg = _sigmoid(gp)
            silu = gp * sg
            a_ref[k] = silu.astype(BF)
            b_ref[k] = (up * sg * (1.0 + gp * (1.0 - sg))).astype(BF)
            act = (silu * up).astype(BF)
            act_ref[k] = act
            err = err + _nn(act, wd[k])
        dy_ref[...] = err * (1.0 / D_MODEL)
        ls_ref[...] += jnp.sum(err * err, axis=0, keepdims=True)

    row = pl.BlockSpec((tm, D_MODEL), lambda i: (i, 0))
    hid = pl.BlockSpec((N_CHIP, tm, FF_SH), lambda i: (0, i, 0))
    anyspec = pl.BlockSpec(memory_space=pl.ANY)
    wshape = pltpu.VMEM((N_CHIP, FF_SH, D_MODEL), BF)
    return pl.pallas_call(
        body, name="ffn_fwd", grid=(T // tm,),
        in_specs=[row, row, row, anyspec, anyspec, anyspec],
        out_specs=[hid, hid, hid, row, pl.BlockSpec((1, D_MODEL), lambda i: (0, 0))],
        out_shape=[jax.ShapeDtypeStruct((N_CHIP, T, FF_SH), BF)] * 3
        + [jax.ShapeDtypeStruct((T, D_MODEL), F32), jax.ShapeDtypeStruct((1, D_MODEL), F32)],
        scratch_shapes=[wshape, wshape, wshape, pltpu.SemaphoreType.DMA((3,))],
        compiler_params=_params(("arbitrary",), VMEM_HUGE),
    )(h2, x2, tgt, w_gate, w_up, w_down)


def _ffn_bwd(dy, sa, sb, x2, g_ffn, w_gate, w_up, w_down, tm=FFN_TM):
    T = dy.shape[0]

    def body(dy_ref, a_ref, b_ref, x2_ref, g_ref, wg_hbm, wu_hbm, wd_hbm, dgp_ref, dup_ref, dx_ref, dg_ref,
             wg, wu, wd, sem):
        @pl.when(pl.program_id(0) == 0)
        def _():
            _load_resident((wg_hbm, wu_hbm, wd_hbm), (wg, wu, wd), sem)
            dg_ref[...] = jnp.zeros(dg_ref.shape, F32)

        dy = dy_ref[...]
        dyb = dy.astype(BF)
        dh = jnp.zeros((tm, D_MODEL), F32)
        for k in range(N_CHIP):
            dact = _nt(dyb, wd[k])
            dup = (dact * a_ref[k]).astype(BF)
            dgp = (dact * b_ref[k]).astype(BF)
            dgp_ref[k] = dgp
            dup_ref[k] = dup
            dh = dh + _nn(dgp, wg[k]) + _nn(dup, wu[k])
        x2 = x2_ref[...]
        r = lax.rsqrt(jnp.mean(x2 * x2, axis=-1, keepdims=True) + EPS)
        xn = x2 * r
        dg_ref[...] += jnp.sum(dh * xn, axis=0, keepdims=True)
        dxn = dh * g_ref[...]
        dx_ref[...] = dy + r * (dxn - xn * jnp.mean(dxn * xn, axis=-1, keepdims=True))

    row = pl.BlockSpec((tm, D_MODEL), lambda i: (i, 0))
    hid = pl.BlockSpec((N_CHIP, tm, FF_SH), lambda i: (0, i, 0))
    vec = pl.BlockSpec((1, D_MODEL), lambda i: (0, 0))
    anyspec = pl.BlockSpec(memory_space=pl.ANY)
    wshape = pltpu.VMEM((N_CHIP, FF_SH, D_MODEL), BF)
    return pl.pallas_call(
        body, name="ffn_bwd", grid=(T // tm,),
        in_specs=[row, hid, hid, row, vec, anyspec, anyspec, anyspec],
        out_specs=[hid, hid, row, vec],
        out_shape=[jax.ShapeDtypeStruct((N_CHIP, T, FF_SH), BF), jax.ShapeDtypeStruct((N_CHIP, T, FF_SH), BF),
                   jax.ShapeDtypeStruct((T, D_MODEL), F32), jax.ShapeDtypeStruct((1, D_MODEL), F32)],
        scratch_shapes=[wshape, wshape, wshape, pltpu.SemaphoreType.DMA((3,))],
        compiler_params=_params(("arbitrary",), VMEM_HUGE),
    )(dy, sa, sb, x2, g_ffn, w_gate, w_up, w_down)


def _out_bwd(dx2, z_a, y_r, y_f, o_raw, o_fox, g_ret, w_ro, w_fo, w_out, tm=256, push=None):
    T = dx2.shape[0]

    def body(dx_ref, gt_ref, ar_ref, af_ref, yr_ref, yf_ref, o_ref, of_ref, g_ref, wro_ref, wfo_ref, wout_ref,
             dyr_ref, dyf_ref, dgt_ref, da_ref, do_ref, dof_ref, dg_ref):
        i = pl.program_id(0)

        @pl.when(i == 0)
        def _():
            dg_ref[...] = jnp.zeros(dg_ref.shape, F32)

        dxb = dx_ref[...].astype(BF)
        dm = jnp.concatenate([_nt(dxb, wout_ref[k]) for k in range(N_CHIP)], axis=-1)
        sr, sf = _sigmoid(ar_ref[...].astype(F32)), _sigmoid(af_ref[...].astype(F32))
        dyr = dm * sr
        dyf = dm * sf
        da_ref[:, :1024] = (dyr * yr_ref[...] * (1.0 - sr)).astype(BF)
        da_ref[:, 1024:] = (dyf * yf_ref[...] * (1.0 - sf)).astype(BF)
        dyr = dyr.astype(BF)
        dyf = dyf.astype(BF)
        dyr_ref[...] = dyr
        dyf_ref[...] = dyf
        du = jnp.zeros((tm, 512), F32)
        doc = jnp.zeros((tm, 512), F32)
        for k in range(N_CHIP):
            du = du + _nt(dyr[:, 256 * k:256 * k + 256], wro_ref[k])
            doc = doc + _nt(dyf[:, 256 * k:256 * k + 256], wfo_ref[k])

        for h in range(RET_H):
            cols = slice(h * RET_DV, (h + 1) * RET_DV)
            o = o_ref[:, cols]
            mu = jnp.mean(o, axis=-1, keepdims=True)
            xc = o - mu
            rstd = lax.rsqrt(jnp.mean(xc * xc, axis=-1, keepdims=True) + EPS)
            on = xc * rstd
            g = g_ref[:, cols]
            gt = gt_ref[:, cols].astype(F32)
            sg = _sigmoid(gt)
            duh = du[:, cols]
            dgt_ref[:, cols] = (duh * (on * g) * sg * (1.0 + gt * (1.0 - sg))).astype(BF)
            dog = duh * gt * sg
            dg_ref[:, cols] += jnp.sum(dog * on, axis=0, keepdims=True)
            don = dog * g
            do_ref[:, cols] = rstd * (don - jnp.mean(don, axis=-1, keepdims=True)
                                      - on * jnp.mean(don * on, axis=-1, keepdims=True))

        lane = lax.broadcasted_iota(jnp.int32, (tm, LANE), 1)
        zpad = jnp.zeros((tm, 64), F32)
        for h in range(FOX_H):
            doh = doc[:, 64 * h:64 * h + 64]
            delta = jnp.sum(doh * of_ref[h][:, :FOX_D], axis=-1, keepdims=True)
            hi, mid, lo = [t.astype(F32) for t in _split3(-delta)]
            da = jnp.concatenate([doh, zpad], axis=-1)
            da = jnp.where(lane == 64, hi, jnp.where(lane == 65, mid, jnp.where(lane == 66, lo, da)))
            dof_ref[h] = da.astype(BF)

    row = lambda w: pl.BlockSpec((tm, w), lambda i: (i, 0))
    const = lambda shp: pl.BlockSpec(shp, lambda i: (0,) * len(shp))
    hsp = pl.BlockSpec((FOX_H, tm, LANE), lambda i: (0, i, 0))
    return _hosted_call(
        body, "out_bwd", (T // tm,),
        [row(1024), pl.BlockSpec((tm, 512), lambda i: (i, 2)), pl.BlockSpec((tm, 1024), lambda i: (i, 3)),
         pl.BlockSpec((tm, 1024), lambda i: (i, 4)), row(1024), row(1024), row(512), hsp,
         const((1, 512)), const((N_CHIP, 512, 256)), const((N_CHIP, 512, 256)), const((N_CHIP, 256, 1024))],
        [row(1024), row(1024), row(512), row(2048), row(512), hsp, const((1, 512))],
        [jax.ShapeDtypeStruct((T, 1024), BF), jax.ShapeDtypeStruct((T, 1024), BF),
         jax.ShapeDtypeStruct((T, 512), BF), jax.ShapeDtypeStruct((T, 2048), BF),
         jax.ShapeDtypeStruct((T, 512), F32), jax.ShapeDtypeStruct((FOX_H, T, LANE), BF),
         jax.ShapeDtypeStruct((1, 512), F32)],
        [], VMEM_BIG, (dx2, z_a, z_a, z_a, y_r, y_f, o_raw, o_fox, g_ret, w_ro, w_fo, w_out), push)


def _ret_bwd(d_o, qr, kr, z_a, states, cos_t, sin_t, consts, tt=512, push=None):
    T = z_a.shape[0]
    nt = T // tt
    nch = tt // CHUNK
    decay, zeta, xi, gcb = consts

    def body(do_ref, q_ref, k_ref, v_ref, st_ref, cos_ref, sin_ref, d_ref, ze_ref, xi_ref, gc_ref, dz_ref, g_sc):
        i = pl.program_id(0)

        @pl.when(i == 0)
        def _():
            g_sc[...] = jnp.zeros(g_sc.shape, F32)

        for c in reversed(range(nch)):
            rows = slice(c * CHUNK, (c + 1) * CHUNK)
            cosv, sinv = cos_ref[rows, :], sin_ref[rows, :]
            dq_parts, dk_parts = [], []
            for h in range(RET_H):
                cols = slice(h * RET_DV, (h + 1) * RET_DV)
                q, k = q_ref[h, rows, :], k_ref[h, rows, :]
                v32 = v_ref[rows, cols].astype(F32)
                vb = v32.astype(BF)
                r = st_ref[h, rows, :]
                g = g_sc[h]
                gb = g.astype(BF)
                d_o = do_ref[rows, cols]
                dob = d_o.astype(BF)
                dox = (d_o * xi_ref[h]).astype(BF)
                dec = d_ref[h]
                s = (_nt(q, k) * dec).astype(BF)
                ds = (_nt(dob, vb) * dec).astype(BF)
                dv = _tn(s, dob) + ze_ref[h] * _nn(k, gb)
                dq = _nn(ds, k) + _nt(dox, r.astype(BF))
                dk = _tn(ds, q) + _nt((v32 * ze_ref[h]).astype(BF), gb)
                g_sc[h] = gc_ref[h] * g + _tn(q, dox)
                dq_parts.append((dq * cosv - _swap32(dq) * sinv)[:, :64])
                dk_parts.append(((dk * cosv - _swap32(dk) * sinv) * 0.125)[:, :64])
                dz_ref[rows, 512 + h * RET_DV:512 + (h + 1) * RET_DV] = dv.astype(BF)
            dz_ref[rows, 0:256] = jnp.concatenate(dq_parts, axis=-1).astype(BF)
            dz_ref[rows, 256:512] = jnp.concatenate(dk_parts, axis=-1).astype(BF)

    rev = lambda i: nt - 1 - i
    hspec = pl.BlockSpec((RET_H, tt, LANE), lambda i: (0, rev(i), 0))
    cspec = pl.BlockSpec((RET_H, CHUNK, LANE), lambda i: (0, 0, 0))
    tab = pl.BlockSpec((tt, LANE), lambda i: (rev(i), 0))
    (dz,), lands = _hosted_call(
        body, "ret_bwd", (nt,),
        [pl.BlockSpec((tt, 512), lambda i: (rev(i), 0)), hspec, hspec,
         pl.BlockSpec((tt, 512), lambda i: (rev(i), 1)), hspec, tab, tab, cspec, cspec, cspec, cspec],
        [pl.BlockSpec((tt, 1024), lambda i: (rev(i), 0))], [jax.ShapeDtypeStruct((T, 1024), BF)],
        [pltpu.VMEM((RET_H, CHUNK, LANE), F32)], VMEM_BIG,
        (d_o, qr, kr, z_a, states, cos_t, sin_t, decay, zeta, xi, gcb), push)
    return dz, lands


def _fox_bwd(end_both, end_last, q2, k, v, do, sub=FOX_SUB):
    H, T, _ = k.shape
    tb = 2 * sub

    def body(eb_ref, el_ref, q_ref, do_ref, k_ref, v_ref, dq_ref, dk_ref, dv_ref, dk_sc, dv_sc):
        j = pl.program_id(1)
        n_both = eb_ref[pl.program_id(0), j]
        n_last = el_ref[pl.program_id(0), j]

        @pl.when(j == 0)
        def _():
            dq_ref[...] = jnp.zeros(dq_ref.shape, F32)

        dk_sc[...] = jnp.zeros(dk_sc.shape, F32)
        dv_sc[...] = jnp.zeros(dv_sc.shape, F32)
        krow = lax.broadcasted_iota(jnp.int32, (tb, sub), 0)
        qcol = lax.broadcasted_iota(jnp.int32, (tb, sub), 1)

        def step(i, r0, r1, shift):
            off = pl.multiple_of(i * sub, sub)
            qq = q_ref[pl.ds(off, sub), :]
            dd = do_ref[pl.ds(off, sub), :]
            kk, vv = k_ref[r0:r1, :], v_ref[r0:r1, :]
            p = jnp.exp(_nt(kk, qq))
            if shift is not None:
                p = jnp.where(qcol[0:r1 - r0, :] + shift >= krow[0:r1 - r0, :], p, 0.0)
            ds = (p * _nt(vv, dd)).astype(BF)
            dv_sc[r0:r1, :] += _nn(p.astype(BF), dd)
            dk_sc[r0:r1, :] += _nn(ds, qq)
            dq_ref[pl.ds(off, sub), :] += _tn(ds, kk)

        step(2 * j, 0, sub, 0)
        step(2 * j + 1, 0, tb, sub)

        def both_body(i, carry):
            step(i, 0, tb, None)
            return carry

        def last_body(i, carry):
            step(i, sub, tb, None)
            return carry

        lax.fori_loop(2 * j + 2, n_both, both_body, 0)
        lax.fori_loop(n_both, n_last, last_body, 0)
        dk_ref[...] = dk_sc[...]
        dv_ref[...] = dv_sc[...]

    blk = pl.BlockSpec((None, tb, LANE), lambda h, j, eb, el: (h, j, 0))
    full = pl.BlockSpec((None, T, LANE), lambda h, j, eb, el: (h, 0, 0))
    shp = jax.ShapeDtypeStruct((H, T, LANE), F32)
    return pl.pallas_call(
        body, name="fox_bwd",
        grid_spec=pltpu.PrefetchScalarGridSpec(
            num_scalar_prefetch=2, grid=(H, T // tb), in_specs=[full, full, blk, blk], out_specs=[full, blk, blk],
            scratch_shapes=[pltpu.VMEM((tb, LANE), F32), pltpu.VMEM((tb, LANE), F32)]),
        out_shape=[shp, shp, shp],
        compiler_params=_params(("arbitrary", "arbitrary"), VMEM_BIG),
    )(end_both, end_last, q2, do, k, v)


def _fox_post_bwd(dq, dk, dv, z_a, z_ff, b_f, g_q, g_k, tm=256, push=None):
    T = z_a.shape[0]
    nt = T // tm

    def body(dq_ref, dk_ref, dv_ref, zf_ref, zff_ref, b_ref, g_ref, sc_ref, seg_ref, segt_ref,
             dz_ref, dff_ref, dg_ref, db_ref, carry):
        i = pl.program_id(0)

        @pl.when(i == 0)
        def _():
            carry[...] = jnp.zeros(carry.shape, F32)
            dg_ref[...] = jnp.zeros(dg_ref.shape, F32)
            db_ref[...] = jnp.zeros(db_ref.shape, F32)

        lane = lax.broadcasted_iota(jnp.int32, (tm, LANE), 1)
        dcm = jnp.zeros((tm, LANE), F32)
        for h in range(FOX_H):
            dcm = jnp.where(lane == h, dq_ref[h][:, L_CQ:L_CQ + 1] - dk_ref[h][:, L_CK:L_CK + 1], dcm)

        def seg_mean(v):
            return sum(_nn(t, seg_ref[...]) for t in _split3(v)) * (1.0 / FOX_D)

        def seg_bcast(v):
            return sum(_nn(t, segt_ref[...]) for t in _split3(v))

        x = zf_ref[:, :1024].astype(F32)
        dy = jnp.concatenate([dq_ref[h][:, :FOX_D] for h in range(FOX_H)]
                             + [dk_ref[h][:, :FOX_D] for h in range(FOX_H)], axis=-1) * sc_ref[...]
        rb = seg_bcast(lax.rsqrt(seg_mean(x * x) + EPS))
        xn = x * rb
        dg_ref[...] += jnp.sum(dy * xn, axis=0, keepdims=True)
        dxn = dy * g_ref[...]
        dz_ref[:, :1024] = (rb * (dxn - xn * seg_bcast(seg_mean(dxn * xn)))).astype(BF)
        dz_ref[:, 1024:] = jnp.concatenate([dv_ref[h][:, :FOX_D] for h in range(FOX_H)], axis=-1).astype(BF)

        row = lax.broadcasted_iota(jnp.int32, (tm, tm), 0)
        col = lax.broadcasted_iota(jnp.int32, (tm, tm), 1)
        tri = (row <= col).astype(BF)
        hi, mid, lo = _split3(dcm)
        dlogf = _nn(tri, hi) + _nn(tri, mid) + _nn(tri, lo) + carry[...]
        carry[...] = dlogf[0:1, :]
        dff = jnp.where(lane < FOX_H, dlogf * _sigmoid(-(zff_ref[...] + b_ref[...])), 0.0)
        dff_ref[...] = dff.astype(BF)
        db_ref[...] += jnp.sum(dff, axis=0, keepdims=True)

    rev = lambda i: nt - 1 - i
    hsp = pl.BlockSpec((FOX_H, tm, LANE), lambda i: (0, rev(i), 0))
    const = lambda r, w: pl.BlockSpec((r, w), lambda i: (0, 0))
    seg = _segment_matrix()
    g_all = jnp.concatenate([jnp.tile(g_q, (1, FOX_H)), jnp.tile(g_k, (1, FOX_H))], axis=1)
    scale = jnp.asarray(np.concatenate([np.full((1, 512), 0.125, np.float32), np.ones((1, 512), np.float32)], axis=1))
    (dz, dff, dg, db), lands = _hosted_call(
        body, "fox_post_bwd", (nt,),
        [hsp, hsp, hsp, pl.BlockSpec((tm, 1536), lambda i: (rev(i), 1)),
         pl.BlockSpec((tm, LANE), lambda i: (rev(i), 0)), const(1, LANE), const(1, 1024), const(1, 1024),
         const(1024, LANE), const(LANE, 1024)],
        [pl.BlockSpec((tm, 1536), lambda i: (rev(i), 0)), pl.BlockSpec((tm, LANE), lambda i: (rev(i), 0)),
         const(1, 1024), const(1, LANE)],
        [jax.ShapeDtypeStruct((T, 1536), BF), jax.ShapeDtypeStruct((T, LANE), BF),
         jax.ShapeDtypeStruct((1, 1024), F32), jax.ShapeDtypeStruct((1, LANE), F32)],
        [pltpu.VMEM((1, LANE), F32)], VMEM_BIG, (dq, dk, dv, z_a, z_ff, b_f, g_all, scale, seg, seg.T), push)
    dg_heads = dg.reshape(2, FOX_H, FOX_D).sum(axis=1)
    return (dz, dff, dg_heads[0:1], dg_heads[1:2], db), lands


def _in_bwd(dz_ret, dz_gt, dz_fox, dz_a, dz_ff, w_a, w_ff, x, g_mix, dx2, tm=256, push=None):
    T = x.shape[0]

    def body(r_ref, t_ref, f_ref, a_ref, ff_ref, wa_ref, wf_ref, x_ref, g_ref, dx2_ref, dx_ref, dg_ref):
        i = pl.program_id(0)

        @pl.when(i == 0)
        def _():
            dg_ref[...] = jnp.zeros(dg_ref.shape, F32)

        dh = (_nt(r_ref[...], wa_ref[:, C_RET:C_GT]) + _nt(t_ref[...], wa_ref[:, C_GT:C_FOX])
              + _nt(f_ref[...], wa_ref[:, C_FOX:C_A]) + _nt(a_ref[...], wa_ref[:, C_A:C_END])
              + _nt(ff_ref[...], wf_ref[...]))
        xv = x_ref[...]
        r = lax.rsqrt(jnp.mean(xv * xv, axis=-1, keepdims=True) + EPS)
        xn = xv * r
        dg_ref[...] += jnp.sum(dh * xn, axis=0, keepdims=True)
        dxn = dh * g_ref[...]
        dx_ref[...] = dx2_ref[...] + r * (dxn - xn * jnp.mean(dxn * xn, axis=-1, keepdims=True))

    row = lambda w: pl.BlockSpec((tm, w), lambda i: (i, 0))
    const = lambda shp: pl.BlockSpec(shp, lambda i: (0,) * len(shp))
    return _hosted_call(
        body, "in_bwd", (T // tm,),
        [row(1024), row(512), row(1536), row(2048), row(LANE), const((D_MODEL, C_END)),
         const((D_MODEL, LANE)), row(1024), const((1, 1024)), row(1024)],
        [row(1024), const((1, 1024))],
        [jax.ShapeDtypeStruct((T, 1024), F32), jax.ShapeDtypeStruct((1, 1024), F32)],
        [], VMEM_BIG, (dz_ret, dz_gt, dz_fox, dz_a, dz_ff, w_a, w_ff, x, g_mix, dx2), push)


def _mesh_pos():
    return lax.axis_index("x"), lax.axis_index("y"), lax.axis_index("c")


def _staged_place(src, name):
    stacked = src.ndim == 3
    R, C = src.shape[-2:]
    tr = _row_tile(R, 128, 16)
    n = R // tr
    assert n >= 2

    def body(s_ref, o_ref, buf, sem):
        i = pl.program_id(0)
        slot = i % 2
        x, y, _ = _mesh_pos()
        kme = 2 * x + y

        def out_copy(s, step):
            return pltpu.make_async_copy(buf.at[s], o_ref.at[kme, pl.ds(pl.multiple_of(step * tr, tr), tr), :], sem.at[s])

        @pl.when(i >= 2)
        def _():
            out_copy(slot, i - 2).wait()

        buf[slot] = (s_ref[kme] if stacked else s_ref[...]).astype(BF)
        out_copy(slot, i).start()

        @pl.when(i == n - 1)
        def _():
            out_copy(1 - slot, i - 1).wait()
            out_copy(slot, i).wait()

    in_spec = (pl.BlockSpec((N_CHIP, tr, C), lambda i: (0, i, 0)) if stacked else pl.BlockSpec((tr, C), lambda i: (i, 0)))
    return pl.pallas_call(
        body, name=name, grid=(n,), in_specs=[in_spec], out_specs=pl.BlockSpec(memory_space=pl.ANY),
        out_shape=jax.ShapeDtypeStruct((N_CHIP, R, C), BF),
        scratch_shapes=[pltpu.VMEM((2, tr, C), BF), pltpu.SemaphoreType.DMA((2,))],
        compiler_params=_params(("arbitrary",)),
    )(src)


def _push_copies(src, land, send_sem, recv_sem, receiving):
    x, y, c = _mesh_pos()
    kme = 2 * x + y
    cps = []
    for w in range(len(land)):
        for j, (px, py) in enumerate([(1 - x, y), (x, 1 - y), (1 - x, 1 - y)]):
            kpeer = 2 * px + py
            cps.append(pltpu.make_async_remote_copy(
                src_ref=land[w].at[kme] if src is None else src[w].at[kpeer],
                dst_ref=land[w].at[kpeer if receiving else kme],
                send_sem=send_sem.at[3 * w + j], recv_sem=recv_sem.at[3 * w + j],
                device_id=(px, py, c), device_id_type=MESH))
    return cps


def _gather_two_level(stack, name):
    _, R, C = stack.shape
    hr = R // 2

    def body(_, land, send_sem, recv_sem):
        x, y, c = _mesh_pos()
        kme = 2 * x + y
        chips = [(1 - x, y), (x, 1 - y), (1 - x, 1 - y)]

        def rows(k, core):
            return land.at[k, pl.ds(pl.multiple_of(core * hr, hr), hr), :]

        def copy(idx, k, core, to):
            return pltpu.make_async_remote_copy(src_ref=rows(k, core), dst_ref=rows(k, core), send_sem=send_sem.at[idx],
                                                recv_sem=recv_sem.at[idx], device_id=to, device_id_type=MESH)

        first = [copy(j, kme, c, (px, py, c)) for j, (px, py) in enumerate(chips)]
        for cp in first:
            cp.start()
        passed = [copy(3 + j, 2 * px + py, c, (x, y, 1 - c)) for j, (px, py) in enumerate(chips)]
        for j, (px, py) in enumerate(chips):
            copy(j, 2 * px + py, c, (px, py, c)).wait_recv()
            passed[j].start()
        for j, (px, py) in enumerate(chips):
            copy(3 + j, 2 * px + py, 1 - c, (x, y, 1 - c)).wait_recv()
        for cp in first + passed:
            cp.wait_send()

    anyspec = pl.BlockSpec(memory_space=pl.ANY)
    return pl.pallas_call(
        body, name=name, in_specs=[anyspec], out_specs=anyspec,
        out_shape=jax.ShapeDtypeStruct(stack.shape, stack.dtype), input_output_aliases={0: 0},
        scratch_shapes=[pltpu.SemaphoreType.DMA((6,)), pltpu.SemaphoreType.DMA((6,))],
    )(stack)


def _gather_small(small):
    def body(sv, svo, ssend, srecv, sloc):
        x, y, c = _mesh_pos()
        me = 4 * x + 2 * y + c
        flips = [(b >> 2 & 1, b >> 1 & 1, b & 1) for b in range(1, 8)]
        others = [(1 - x if fx else x, 1 - y if fy else y, 1 - c if fc else c) for fx, fy, fc in flips]
        local = pltpu.make_async_copy(sv, svo.at[me], sloc)
        local.start()
        sends = []
        for j, (px, py, pc) in enumerate(others):
            cp = pltpu.make_async_remote_copy(
                src_ref=sv, dst_ref=svo.at[me], send_sem=ssend.at[j], recv_sem=srecv.at[j],
                device_id=(px, py, pc), device_id_type=MESH)
            cp.start()
            sends.append(cp)
        for j, (px, py, pc) in enumerate(others):
            pltpu.make_async_remote_copy(
                src_ref=sv, dst_ref=svo.at[4 * px + 2 * py + pc], send_sem=ssend.at[j], recv_sem=srecv.at[j],
                device_id=(px, py, pc), device_id_type=MESH).wait_recv()
        for cp in sends:
            cp.wait_send()
        local.wait()

    anyspec = pl.BlockSpec(memory_space=pl.ANY)
    return pl.pallas_call(
        body, name="gather_small", in_specs=[anyspec], out_specs=anyspec,
        out_shape=jax.ShapeDtypeStruct((8,) + small.shape, small.dtype),
        scratch_shapes=[pltpu.SemaphoreType.DMA((7,)), pltpu.SemaphoreType.DMA((7,)), pltpu.SemaphoreType.DMA],
    )(small)


def _sibling_exchange(arrs):
    n = len(arrs)

    def body(*refs):
        ins, outs = refs[:n], refs[n:2 * n]
        send_sems, recv_sems = refs[2 * n:]
        x, y, c = _mesh_pos()
        cps = [pltpu.make_async_remote_copy(
            src_ref=ins[w], dst_ref=outs[w], send_sem=send_sems.at[w], recv_sem=recv_sems.at[w],
            device_id=(x, y, 1 - c), device_id_type=MESH) for w in range(n)]
        for cp in cps:
            cp.start()
        for cp in cps:
            cp.wait_recv()
        for cp in cps:
            cp.wait_send()

    anyspec = pl.BlockSpec(memory_space=pl.ANY)
    return pl.pallas_call(
        body, name="sibling_exchange",
        in_specs=[anyspec] * n, out_specs=[anyspec] * n,
        out_shape=[jax.ShapeDtypeStruct(a.shape, a.dtype) for a in arrs],
        scratch_shapes=[pltpu.SemaphoreType.DMA((n,)), pltpu.SemaphoreType.DMA((n,))],
    )(*arrs)


def _sum_stack(own, recv, name):
    _, R, C = recv.shape
    tr = _row_tile(R, 256, 16)

    def body(g_ref, r_ref, o_ref):
        x, y, _ = _mesh_pos()
        kme = 2 * x + y
        acc = g_ref[kme].astype(F32)
        for d in range(1, N_CHIP):
            acc = acc + r_ref[(kme + d) % N_CHIP].astype(F32)
        o_ref[...] = acc

    spec = pl.BlockSpec((N_CHIP, tr, C), lambda i: (0, i, 0))
    return pl.pallas_call(
        body, name=name, grid=(R // tr,), in_specs=[spec, spec],
        out_specs=pl.BlockSpec((tr, C), lambda i: (i, 0)),
        out_shape=jax.ShapeDtypeStruct((R, C), F32),
        compiler_params=_params(("parallel",)),
    )(own, recv)


def _adam_math(w, g, m, v):
    m2 = ADAM_B1 * m + (1.0 - ADAM_B1) * g
    v2 = ADAM_B2 * v + (1.0 - ADAM_B2) * (g * g)
    m_hat = m2 / (1.0 - ADAM_B1 ** ADAM_STEP)
    v_hat = v2 / (1.0 - ADAM_B2 ** ADAM_STEP)
    delta = -ADAM_LR * (m_hat / (jnp.sqrt(v_hat) + ADAM_EPS) + ADAM_WD * w)
    return delta, m2, v2


def _adamw(w, m, v, s0, s1, name):
    R, C = w.shape
    tr = _row_tile(R, 128, 8)

    def body(w_ref, m_ref, v_ref, a_ref, b_ref, g_ref, d_ref, m2_ref, v2_ref):
        g = a_ref[...] + b_ref[...]
        delta, m2, v2 = _adam_math(w_ref[...], g, m_ref[...], v_ref[...])
        g_ref[...] = g
        d_ref[...] = delta
        m2_ref[...] = m2
        v2_ref[...] = v2

    spec = pl.BlockSpec((tr, C), lambda i: (i, 0))
    shp = jax.ShapeDtypeStruct((R, C), F32)
    return pl.pallas_call(
        body, name=name, grid=(R // tr,), in_specs=[spec] * 5, out_specs=[spec] * 4, out_shape=[shp] * 4,
        compiler_params=_params(("parallel",), VMEM_BIG),
    )(w, m, v, s0, s1)


def _adamw_small(ws, ms, vs, gathered):
    n = len(SMALL)

    def body(*refs):
        w_refs, m_refs, v_refs, s_ref = refs[:n], refs[n:2 * n], refs[2 * n:3 * n], refs[3 * n]
        outs = refs[3 * n + 1:]
        g_all = s_ref[0]
        for d in range(1, 8):
            g_all = g_all + s_ref[d]
        off = 0
        for i, (_, width) in enumerate(SMALL):
            g = g_all[:, off:off + width]
            delta, m2, v2 = _adam_math(w_refs[i][...], g, m_refs[i][...], v_refs[i][...])
            for kind, val in enumerate((g, delta, m2, v2)):
                outs[kind * n + i][...] = val
            off += width + (-width % LANE)

    shapes = [jax.ShapeDtypeStruct((1, width), F32) for _, width in SMALL]
    res = pl.pallas_call(body, name="adamw_small", out_shape=shapes * 4)(*ws, *ms, *vs, gathered)
    return [dict(zip([nm for nm, _ in SMALL], res[kind * n:(kind + 1) * n])) for kind in range(4)]


SMALL = (("g_mix", 1024), ("g_ffn", 1024), ("g_ret_norm", 512), ("g_fox_q", 64), ("g_fox_k", 64), ("b_forget", 8))
SMALL_W = 3072


def _pack_small(parts):
    cols = []
    for (name, n) in SMALL:
        p = parts[name].reshape(1, -1)[:, :n]
        pad = -n % LANE
        cols.append(jnp.pad(p, ((0, 0), (0, pad))) if pad else p)
    used = sum(c.shape[1] for c in cols)
    cols.append(jnp.zeros((1, SMALL_W - used), F32))
    return jnp.concatenate(cols, axis=1)


def kernel(x, g_mix, w_in, b_forget, g_ret_norm, w_ret_o, g_fox_q, g_fox_k, w_fox_o, w_out, g_ffn, w_gate, w_up, w_down, loss_target, m_g_mix, m_w_in, m_b_forget, m_g_ret_norm, m_w_ret_o, m_g_fox_q, m_g_fox_k, m_w_fox_o, m_w_out, m_g_ffn, m_w_gate, m_w_up, m_w_down, v_g_mix, v_w_in, v_b_forget, v_g_ret_norm, v_w_ret_o, v_g_fox_q, v_g_fox_k, v_w_fox_o, v_w_out, v_g_ffn, v_w_gate, v_w_up, v_w_down):
    T = x.shape[1]
    xs = x[0]
    tgt = loss_target[0]
    big_names = ("w_in", "w_ret_o", "w_fox_o", "w_out", "w_gate", "w_up", "w_down")
    tr = lambda a: jnp.swapaxes(a[0], 0, 1)
    big_w = dict(w_in=w_in[0], w_ret_o=w_ret_o[0], w_fox_o=w_fox_o[0], w_out=w_out[0], w_gate=tr(w_gate),
                 w_up=tr(w_up), w_down=w_down[0])
    big_m = dict(w_in=m_w_in[0], w_ret_o=m_w_ret_o[0], w_fox_o=m_w_fox_o[0], w_out=m_w_out[0], w_gate=tr(m_w_gate),
                 w_up=tr(m_w_up), w_down=m_w_down[0])
    big_v = dict(w_in=v_w_in[0], w_ret_o=v_w_ret_o[0], w_fox_o=v_w_fox_o[0], w_out=v_w_out[0], w_gate=tr(v_w_gate),
                 w_up=tr(v_w_up), w_down=v_w_down[0])
    small_w = dict(g_mix=g_mix, g_ffn=g_ffn, g_ret_norm=g_ret_norm, g_fox_q=g_fox_q, g_fox_k=g_fox_k, b_forget=b_forget)
    small_m = dict(g_mix=m_g_mix, g_ffn=m_g_ffn, g_ret_norm=m_g_ret_norm, g_fox_q=m_g_fox_q, g_fox_k=m_g_fox_k,
                   b_forget=m_b_forget)
    small_v = dict(g_mix=v_g_mix, g_ffn=v_g_ffn, g_ret_norm=v_g_ret_norm, g_fox_q=v_g_fox_q, g_fox_k=v_g_fox_k,
                   b_forget=v_b_forget)

    stacks = {n: _staged_place(big_w[n], "place_" + n) for n in big_names}
    s_in = _gather_two_level(stacks["w_in"], "gather_w_in")
    w_a, w_ff = _assemble_w_in(s_in)
    b_pad = jnp.pad(b_forget, ((0, 0), (0, LANE - FOX_H)))
    cos_t, sin_t = _rope_tables(T)
    consts = _ret_consts()

    h = _rms_cast(xs, g_mix)
    z_a, (s_gate, s_up) = _mm_nn(h, w_a, "proj_in", BF, push=(None, [stacks["w_gate"], stacks["w_up"]]))
    z_ff, _ = _mm_nn(h, w_ff, "proj_ff", F32)
    (qr, kr, qf, kf, vf, c_cum, nmax), (s_down, s_ro, s_fo, s_out) = _mix_prep(
        z_a, z_ff, cos_t, sin_t, b_pad, g_fox_q, g_fox_k,
        push=(None, [stacks["w_down"], stacks["w_ret_o"], stacks["w_fox_o"], stacks["w_out"]]))
    jlo, end_both, end_last, tame = _prune_tables(c_cum, nmax, FOX_SUB)
    o_raw, u_r, states = _ret_fwd(qr, kr, z_a, g_ret_norm, consts)
    o_fox, q2 = _fox_fwd(jlo, tame, qf, kf, vf)
    y_r, y_f, mrg, x2, h2, o_cat = _merge_out(u_r, o_fox, z_a, xs, g_ffn, s_ro, s_fo, s_out)
    sa, sb, act, dy, loss_vec = _ffn_fwd(h2, x2, tgt, s_gate, s_up, s_down)
    loss = lax.psum(0.5 / D_MODEL * jnp.sum(loss_vec), ("x", "y", "c"))

    def scatter_job(grads):
        return (grads, [lax.empty(g.shape, g.dtype) for g in grads])

    dgp, dup, dx2, dg_ffn = _ffn_bwd(dy, sa, sb, x2, g_ffn, s_gate, s_up, s_down)
    g_gate, g_up, g_down = (_grad_astack(dgp, h2, "gw_gate"), _grad_astack(dup, h2, "gw_up"),
                            _grad_astack(act, dy, "gw_down"))
    (d_yr, d_yf, dz_gt, dz_a, d_o, do_fox, dg_ret), (r_gate, r_up) = _out_bwd(
        dx2, z_a, y_r, y_f, o_raw, o_fox, g_ret_norm, s_ro, s_fo, s_out,
        push=scatter_job([g_gate, g_up]))
    dz_ret, (r_down,) = _ret_bwd(d_o, qr, kr, z_a, states, cos_t, sin_t, consts, push=scatter_job([g_down]))
    dq_f, dk_f, dv_f = _fox_bwd(end_both, end_last, q2, kf, vf, do_fox)
    g_mid = [_grad_colstack(u_r, d_yr, "gw_ret_o", 256), _grad_colstack(o_cat, d_yf, "gw_fox_o", 256),
             _grad_plain(mrg, dx2, "gw_out", BF).reshape(N_CHIP, 256, D_MODEL)]
    (dz_fox, dz_ff, dg_q, dg_k, db_f), (r_ro, r_fo, r_out) = _fox_post_bwd(
        dq_f, dk_f, dv_f, z_a, z_ff, b_pad, g_fox_q, g_fox_k, push=scatter_job(g_mid))
    g_in = _pack_g_in(_grad_plain(h, dz_ret, "gw_in_ret", F32), _grad_plain(h, dz_gt, "gw_in_gt", F32),
                      _grad_plain(h, dz_fox, "gw_in_fox", F32, tn=1536),
                      _grad_plain(h, dz_a, "gw_in_a", F32, tk=1024, tn=2048),
                      _grad_plain(h, dz_ff, "gw_in_ff", F32))
    (grad_x, dg_mix), (r_in,) = _in_bwd(dz_ret, dz_gt, dz_fox, dz_a, dz_ff, w_a, w_ff, xs, g_mix, dx2,
                                        push=scatter_job([g_in]))
    small_g = _pack_small(dict(g_mix=dg_mix, g_ffn=dg_ffn, g_ret_norm=dg_ret, g_fox_q=dg_q, g_fox_k=dg_k, b_forget=db_f))

    small_all = _gather_small(small_g)
    sums = [_sum_stack(g, r, "sum_" + n) for g, r, n in zip(
        [g_in] + g_mid + [g_gate, g_up, g_down], [r_in, r_ro, r_fo, r_out, r_gate, r_up, r_down], big_names)]
    sib = _sibling_exchange(sums)
    big_out = {n: _adamw(big_w[n], big_m[n], big_v[n], sums[i], sib[i], "adamw_" + n) for i, n in enumerate(big_names)}
    small_out = _adamw_small(*[[d[nm] for nm, _ in SMALL] for d in (small_w, small_m, small_v)], small_all)

    order = ("g_mix", "w_in", "b_forget", "g_ret_norm", "w_ret_o", "g_fox_q", "g_fox_k", "w_fox_o", "w_out", "g_ffn",
             "w_gate", "w_up", "w_down")
    outs = [loss, grad_x[None]]
    for idx in range(4):
        for n in order:
            if n in ("w_gate", "w_up"):
                outs.append(jnp.swapaxes(big_out[n][idx], 0, 1)[None])
            else:
                outs.append(big_out[n][idx][None] if n in big_out else small_out[idx][n])
    return tuple(outs)
```

```python
import functools
import math

import numpy as np
import jax
import jax.numpy as jnp
from jax import lax
from jax.experimental import pallas as pl
from jax.experimental.pallas import tpu as pltpu

F32 = jnp.float32
BF = jnp.bfloat16
MESH = pl.DeviceIdType.MESH

D_MODEL = 1024
D_FF = 2816
N_CHIP = 4
FF_SH = D_FF // N_CHIP
IN_COLS = 5128
IN_SH = IN_COLS // N_CHIP
RET_H, RET_DV = 4, 128
FOX_H, FOX_D = 8, 64
CHUNK = 256
EPS = 1e-6
NEG = -1e30
LANE = 128
C_RET, C_GT, C_FOX, C_A, C_END = 0, 1024, 1536, 3072, 5120
L_CQ, L_CK, L_LSE, L_MAX = 64, 67, 70, 73

ADAM_LR, ADAM_B1, ADAM_B2, ADAM_EPS, ADAM_WD, ADAM_STEP = 0.001, 0.9, 0.999, 1e-08, 0.01, 10
VMEM_BIG = 56 * 1024 * 1024
VMEM_HUGE = 60 * 1024 * 1024
GRAD_TK = 2048
FFN_TM = 512
FOX_SUB = 512


def _nn(a, b):
    return lax.dot_general(a, b, (((1,), (0,)), ((), ())), preferred_element_type=F32)


def _nt(a, b):
    return lax.dot_general(a, b, (((1,), (1,)), ((), ())), preferred_element_type=F32)


def _tn(a, b):
    return lax.dot_general(a, b, (((0,), (0,)), ((), ())), preferred_element_type=F32)


def _split3(x):
    hi = x.astype(BF)
    r = x - hi.astype(F32)
    mid = r.astype(BF)
    lo = (r - mid.astype(F32)).astype(BF)
    return hi, mid, lo


def _sigmoid(x):
    return 0.5 * jnp.tanh(0.5 * x) + 0.5


def _swap32(x):
    lane = lax.broadcasted_iota(jnp.int32, x.shape, 1)
    return jnp.where(lane < 32, pltpu.roll(x, 96, 1), pltpu.roll(x, 32, 1))


def _params(sem, vmem=None):
    return pltpu.CompilerParams(dimension_semantics=sem, vmem_limit_bytes=vmem)


def _row_tile(rows, cap, mult):
    return max(d for d in range(mult, cap + 1, mult) if rows % d == 0)


def _assemble_w_in(stack, tr=256):
    def body(s_ref, a_ref, f_ref):
        full = jnp.concatenate([s_ref[k].astype(F32) for k in range(N_CHIP)], axis=-1)
        a_ref[...] = jnp.concatenate([full[:, :3072], full[:, 3080:IN_COLS]], axis=-1).astype(BF)
        f_ref[...] = jnp.concatenate([full[:, 3072:3080], jnp.zeros((tr, LANE - FOX_H), F32)], axis=-1).astype(BF)

    return pl.pallas_call(
        body, name="assemble_w_in", grid=(D_MODEL // tr,),
        in_specs=[pl.BlockSpec((N_CHIP, tr, IN_SH), lambda i: (0, i, 0))],
        out_specs=[pl.BlockSpec((tr, C_END), lambda i: (i, 0)), pl.BlockSpec((tr, LANE), lambda i: (i, 0))],
        out_shape=[jax.ShapeDtypeStruct((D_MODEL, C_END), BF), jax.ShapeDtypeStruct((D_MODEL, LANE), BF)],
        compiler_params=_params(("parallel",), VMEM_BIG),
    )(stack)


def _pack_g_in(g_ret, g_gt, g_fox, g_a, g_ff, tr=256):
    def body(r_ref, t_ref, x_ref, a_ref, f_ref, o_ref):
        full = jnp.concatenate([r_ref[...], t_ref[...], x_ref[...], f_ref[...][:, :FOX_H], a_ref[...]], axis=-1)
        for k in range(N_CHIP):
            o_ref[k] = full[:, k * IN_SH:(k + 1) * IN_SH].astype(BF)

    def spec(w):
        return pl.BlockSpec((tr, w), lambda i: (i, 0))

    return pl.pallas_call(
        body, name="pack_g_in", grid=(D_MODEL // tr,),
        in_specs=[spec(1024), spec(512), spec(1536), spec(2048), spec(LANE)],
        out_specs=pl.BlockSpec((N_CHIP, tr, IN_SH), lambda i: (0, i, 0)),
        out_shape=jax.ShapeDtypeStruct((N_CHIP, D_MODEL, IN_SH), BF),
        compiler_params=_params(("parallel",), VMEM_BIG),
    )(g_ret, g_gt, g_fox, g_a, g_ff)


def _rms_cast(x, g, tm=512):
    T = x.shape[0]

    def body(x_ref, g_ref, o_ref):
        xv = x_ref[...]
        r = lax.rsqrt(jnp.mean(xv * xv, axis=-1, keepdims=True) + EPS)
        o_ref[...] = (xv * r * g_ref[...]).astype(BF)

    return pl.pallas_call(
        body, name="rms_cast", grid=(T // tm,),
        in_specs=[pl.BlockSpec((tm, D_MODEL), lambda i: (i, 0)), pl.BlockSpec((1, D_MODEL), lambda i: (0, 0))],
        out_specs=pl.BlockSpec((tm, D_MODEL), lambda i: (i, 0)),
        out_shape=jax.ShapeDtypeStruct((T, D_MODEL), BF),
        compiler_params=_params(("parallel",)),
    )(x, g)


def _hosted_call(body, name, grid, in_specs, out_specs, out_shape, scratch_shapes, vmem, args, push):
    sem = ("arbitrary",) * len(grid)
    if push is None:
        res = pl.pallas_call(body, name=name, grid=grid, in_specs=in_specs, out_specs=out_specs, out_shape=out_shape,
                             scratch_shapes=scratch_shapes, compiler_params=_params(sem, vmem))(*args)
        return list(res), []
    srcs, lands = push
    ns, nl, n_in, n_out = (0 if srcs is None else len(srcs)), len(lands), len(in_specs), len(out_specs)
    n_scr = len(scratch_shapes)

    def wrapped(*refs):
        pos = n_in + ns + nl
        ins, x_in = refs[:n_in], refs[n_in:pos]
        outs, x_out = refs[pos:pos + n_out], refs[pos + n_out:pos + n_out + nl]
        scr = refs[pos + n_out + nl:pos + n_out + nl + n_scr]
        ssem, rsem = refs[-2], refs[-1]
        src = None if srcs is None else x_in[:ns]
        ids = [pl.program_id(a) for a in range(len(grid))]
        first = functools.reduce(lambda p, q: p & q, [ids[a] == 0 for a in range(len(grid))])
        last = functools.reduce(lambda p, q: p & q, [ids[a] == grid[a] - 1 for a in range(len(grid))])

        @pl.when(first)
        def _():
            for cp in _push_copies(src, x_out, ssem, rsem, False):
                cp.start()

        body(*ins, *outs, *scr)

        @pl.when(last)
        def _():
            for cp in _push_copies(src, x_out, ssem, rsem, True):
                cp.wait_recv()
                cp.wait_send()

    anyspec = pl.BlockSpec(memory_space=pl.ANY)
    extra = ([] if srcs is None else list(srcs)) + list(lands)
    res = pl.pallas_call(
        wrapped, name=name, grid=grid,
        in_specs=list(in_specs) + [anyspec] * len(extra), out_specs=list(out_specs) + [anyspec] * nl,
        out_shape=list(out_shape) + [jax.ShapeDtypeStruct(a.shape, a.dtype) for a in lands],
        input_output_aliases={n_in + ns + i: n_out + i for i in range(nl)},
        scratch_shapes=list(scratch_shapes) + [pltpu.SemaphoreType.DMA((3 * nl,)), pltpu.SemaphoreType.DMA((3 * nl,))],
        compiler_params=_params(sem, vmem),
    )(*args, *extra)
    return list(res[:n_out]), list(res[n_out:])


def _mm_nn(a, b, name, out_dtype, tm=512, tn=1024, push=None):
    M, K = a.shape
    N = b.shape[1]
    tn = min(tn, N)

    def body(a_ref, b_ref, o_ref):
        o_ref[...] = _nn(a_ref[...], b_ref[...]).astype(o_ref.dtype)

    (out,), lands = _hosted_call(
        body, name, (N // tn, M // tm),
        [pl.BlockSpec((tm, K), lambda j, i: (i, 0)), pl.BlockSpec((K, tn), lambda j, i: (0, j))],
        [pl.BlockSpec((tm, tn), lambda j, i: (i, j))], [jax.ShapeDtypeStruct((M, N), out_dtype)], [], None, (a, b), push)
    return out, lands


def _mm_tn(a, b, name, grid, a_spec, b_spec, o_spec, out_shape, acc_shape):
    nk = grid[-1]

    def body(a_ref, b_ref, o_ref, acc):
        k = pl.program_id(len(grid) - 1)

        @pl.when(k == 0)
        def _():
            acc[...] = jnp.zeros(acc.shape, F32)

        acc[...] += _tn(a_ref[...].astype(BF), b_ref[...].astype(BF))

        @pl.when(k == nk - 1)
        def _():
            o_ref[...] = acc[...].astype(o_ref.dtype)

    return pl.pallas_call(
        body, name=name, grid=grid, in_specs=[a_spec, b_spec], out_specs=o_spec, out_shape=out_shape,
        scratch_shapes=[pltpu.VMEM(acc_shape, F32)],
        compiler_params=_params(("parallel",) * (len(grid) - 1) + ("arbitrary",), VMEM_BIG),
    )(a, b)


def _grad_plain(a, b, name, out_dtype, tk=GRAD_TK, tn=1024):
    T, M = a.shape
    N = b.shape[1]
    tn = min(tn, N)
    return _mm_tn(a, b, name, (N // tn, T // tk),
                  pl.BlockSpec((tk, M), lambda j, k: (k, 0)), pl.BlockSpec((tk, tn), lambda j, k: (k, j)),
                  pl.BlockSpec((M, tn), lambda j, k: (0, j)), jax.ShapeDtypeStruct((M, N), out_dtype), (M, tn))


def _grad_colstack(a, b, name, wcol, tk=GRAD_TK):
    T, M = a.shape
    N = b.shape[1]
    S = N // wcol
    nk = T // tk

    def body(a_ref, b_ref, o_ref, acc):
        k = pl.program_id(0)

        @pl.when(k == 0)
        def _():
            acc[...] = jnp.zeros(acc.shape, F32)

        acc[...] += _tn(a_ref[...], b_ref[...])

        @pl.when(k == nk - 1)
        def _():
            for s in range(S):
                o_ref[s] = acc[:, s * wcol:(s + 1) * wcol].astype(BF)

    return pl.pallas_call(
        body, name=name, grid=(nk,),
        in_specs=[pl.BlockSpec((tk, M), lambda k: (k, 0)), pl.BlockSpec((tk, N), lambda k: (k, 0))],
        out_specs=pl.BlockSpec((S, M, wcol), lambda k: (0, 0, 0)), out_shape=jax.ShapeDtypeStruct((S, M, wcol), BF),
        scratch_shapes=[pltpu.VMEM((M, N), F32)], compiler_params=_params(("arbitrary",), VMEM_BIG),
    )(a, b)


def _grad_astack(a, b, name, tk=1024):
    S, T, m = a.shape
    N = b.shape[1]
    nk = T // tk

    def body(a_ref, b_ref, o_ref, acc):
        k = pl.program_id(0)

        @pl.when(k == 0)
        def _():
            acc[...] = jnp.zeros(acc.shape, F32)

        bb = b_ref[...].astype(BF)
        for s in range(S):
            acc[s] += _tn(a_ref[s], bb)

        @pl.when(k == nk - 1)
        def _():
            o_ref[...] = acc[...].astype(BF)

    return pl.pallas_call(
        body, name=name, grid=(nk,),
        in_specs=[pl.BlockSpec((S, tk, m), lambda k: (0, k, 0)), pl.BlockSpec((tk, N), lambda k: (k, 0))],
        out_specs=pl.BlockSpec((S, m, N), lambda k: (0, 0, 0)), out_shape=jax.ShapeDtypeStruct((S, m, N), BF),
        scratch_shapes=[pltpu.VMEM((S, m, N), F32)], compiler_params=_params(("arbitrary",), VMEM_BIG),
    )(a, b)


def _rope_tables(T):
    half = 32
    pos = np.arange(T, dtype=np.float32)
    inv_freq = (np.float32(1.0) / (np.float32(10000.0) ** (np.arange(half, dtype=np.float32) / np.float32(half)))).astype(np.float32)
    ang = (pos[:, None] * inv_freq[None, :]).astype(np.float32)
    cos, sin = np.cos(ang).astype(np.float32), np.sin(ang).astype(np.float32)
    z = np.zeros((T, 64), np.float32)
    return (jnp.asarray(np.concatenate([cos, cos, z], axis=-1)), jnp.asarray(np.concatenate([-sin, sin, z], axis=-1)))


def _ret_consts():
    h = np.arange(RET_H, dtype=np.float32)
    log_g = np.log1p(-(np.float32(2.0) ** (-5.0 - h))).astype(np.float32)
    idx = np.arange(CHUNK, dtype=np.float32)
    diff = idx[:, None] - idx[None, :]
    decay = np.where(diff[None] >= 0, np.exp(np.maximum(diff, 0.0)[None] * log_g[:, None, None]), 0.0)
    zeta = np.exp((CHUNK - 1.0 - idx)[None, :] * log_g[:, None])
    xi = np.exp((idx + 1.0)[None, :] * log_g[:, None])
    gc = np.exp(CHUNK * log_g)
    bc = lambda v: np.broadcast_to(v[:, :, None], (RET_H, CHUNK, LANE)).astype(np.float32)
    gcb = np.broadcast_to(gc[:, None, None], (RET_H, LANE, LANE)).astype(np.float32)
    return (jnp.asarray(decay.astype(np.float32)), jnp.asarray(bc(zeta)), jnp.asarray(bc(xi)), jnp.asarray(gcb))


def _mix_prep(z_a, z_ff, cos_t, sin_t, b_f, g_q, g_k, tm=256, push=None):
    T = z_a.shape[0]

    def body(zqk_ref, zf_ref, zff_ref, cos_ref, sin_ref, b_ref, g_ref, seg_ref, segt_ref,
             qr_ref, kr_ref, qf_ref, kf_ref, vf_ref, c_ref, nmax_ref, carry):
        i = pl.program_id(0)

        @pl.when(i == 0)
        def _():
            carry[...] = jnp.zeros(carry.shape, F32)
            nmax_ref[...] = jnp.zeros(nmax_ref.shape, F32)

        lane = lax.broadcasted_iota(jnp.int32, (tm, LANE), 1)
        zpad = jnp.zeros((tm, 64), F32)
        cosv, sinv = cos_ref[...], sin_ref[...]
        zqk = zqk_ref[...].astype(F32)
        for h in range(RET_H):
            for src, dst, scale in ((0, qr_ref, 1.0), (256, kr_ref, 0.125)):
                xh = jnp.concatenate([zqk[:, src + 64 * h: src + 64 * h + 64], zpad], axis=-1)
                rot = xh * cosv + _swap32(xh) * sinv
                dst[h] = (rot * scale).astype(BF)

        lf_in = zff_ref[...] + b_ref[...]
        logf = jnp.minimum(lf_in, 0.0) - jnp.log(1.0 + jnp.exp(-jnp.abs(lf_in)))
        row = lax.broadcasted_iota(jnp.int32, (tm, tm), 0)
        col = lax.broadcasted_iota(jnp.int32, (tm, tm), 1)
        tri = (row >= col).astype(BF)
        hi, mid, lo = _split3(logf)
        cs = _nn(tri, hi) + _nn(tri, mid) + _nn(tri, lo) + carry[...]
        carry[...] = cs[tm - 1:tm, :]
        c_ref[...] = cs

        def seg_sum(v):
            return sum(_nn(t, seg_ref[...]) for t in _split3(v))

        zf = zf_ref[...].astype(F32)
        xqk = zf[:, :1024]
        rinv = lax.rsqrt(seg_sum(xqk * xqk) * (1.0 / FOX_D) + EPS)
        xn = xqk * sum(_nn(t, segt_ref[...]) for t in _split3(rinv)) * g_ref[...]
        nmax_ref[...] = jnp.maximum(nmax_ref[...], jnp.max(seg_sum(xn * xn), axis=0, keepdims=True))

        one = jnp.ones((tm, LANE), F32)
        for h in range(FOX_H):
            c = cs[:, h:h + 1]
            chi, cmid, clo = [t.astype(F32) for t in _split3(c)]
            qn = xn[:, 64 * h:64 * h + 64]
            kn = xn[:, 512 + 64 * h:512 + 64 * h + 64]
            vh = zf[:, 1024 + 64 * h:1024 + 64 * h + 64]
            qa = jnp.concatenate([qn, zpad], axis=-1)
            qa = jnp.where(lane == L_CQ, chi, jnp.where(lane == L_CQ + 1, cmid, jnp.where(lane == L_CQ + 2, clo, qa)))
            qa = jnp.where((lane >= L_CK) & (lane < L_CK + 3), one, qa)
            ka = jnp.concatenate([kn, zpad], axis=-1)
            ka = jnp.where(lane == L_CK, -chi, jnp.where(lane == L_CK + 1, -cmid, jnp.where(lane == L_CK + 2, -clo, ka)))
            ka = jnp.where(((lane >= L_CQ) & (lane < L_CQ + 3)) | ((lane >= L_LSE) & (lane < L_MAX + 3)), one, ka)
            va = jnp.concatenate([vh, zpad], axis=-1)
            va = jnp.where((lane >= 64) & (lane < 67), one, va)
            qf_ref[h] = qa.astype(BF)
            kf_ref[h] = ka.astype(BF)
            vf_ref[h] = va.astype(BF)

    hspec4 = pl.BlockSpec((RET_H, tm, LANE), lambda i: (0, i, 0))
    hspec8 = pl.BlockSpec((FOX_H, tm, LANE), lambda i: (0, i, 0))
    const = lambda r, w: pl.BlockSpec((r, w), lambda i: (0, 0))
    seg = _segment_matrix()
    g_all = jnp.concatenate([jnp.tile(g_q * 0.125, (1, FOX_H)), jnp.tile(g_k, (1, FOX_H))], axis=1)
    return _hosted_call(
        body, "mix_prep", (T // tm,),
        [pl.BlockSpec((tm, 512), lambda i: (i, 0)), pl.BlockSpec((tm, 1536), lambda i: (i, 1)),
         pl.BlockSpec((tm, LANE), lambda i: (i, 0)), pl.BlockSpec((tm, LANE), lambda i: (i, 0)),
         pl.BlockSpec((tm, LANE), lambda i: (i, 0)), const(1, LANE), const(1, 1024), const(1024, LANE), const(LANE, 1024)],
        [hspec4, hspec4, hspec8, hspec8, hspec8, pl.BlockSpec((tm, LANE), lambda i: (i, 0)), const(1, LANE)],
        [jax.ShapeDtypeStruct((RET_H, T, LANE), BF)] * 2 + [jax.ShapeDtypeStruct((FOX_H, T, LANE), BF)] * 3
        + [jax.ShapeDtypeStruct((T, LANE), F32), jax.ShapeDtypeStruct((1, LANE), F32)],
        [pltpu.VMEM((1, LANE), F32)], VMEM_BIG, (z_a, z_a, z_ff, cos_t, sin_t, b_f, g_all, seg, seg.T), push)


def _segment_matrix():
    m = np.zeros((2 * FOX_H * FOX_D, LANE), np.float32)
    m[np.arange(2 * FOX_H * FOX_D), np.arange(2 * FOX_H * FOX_D) // FOX_D] = 1.0
    return jnp.asarray(m, dtype=BF)


def _ret_fwd(qr, kr, z_a, g_ret, consts, tt=512):
    T = z_a.shape[0]
    nch = tt // CHUNK
    decay, zeta, xi, gcb = consts

    def body(q_ref, k_ref, v_ref, gt_ref, g_ref, d_ref, ze_ref, xi_ref, gc_ref, o_ref, u_ref, st_ref, r_sc):
        i = pl.program_id(0)

        @pl.when(i == 0)
        def _():
            r_sc[...] = jnp.zeros(r_sc.shape, F32)

        for c in range(nch):
            rows = slice(c * CHUNK, (c + 1) * CHUNK)
            for h in range(RET_H):
                cols = slice(h * RET_DV, (h + 1) * RET_DV)
                q, k = q_ref[h, rows, :], k_ref[h, rows, :]
                v32 = v_ref[rows, cols].astype(F32)
                r = r_sc[h]
                st_ref[h, c * CHUNK:c * CHUNK + LANE, :] = r
                s = _nt(q, k) * d_ref[h]
                o = _nn(s.astype(BF), v32.astype(BF)) + _nn(q, r.astype(BF)) * xi_ref[h]
                r_sc[h] = gc_ref[h] * r + _tn(k, (v32 * ze_ref[h]).astype(BF))
                o_ref[rows, cols] = o
                mu = jnp.mean(o, axis=-1, keepdims=True)
                xc = o - mu
                on = xc * lax.rsqrt(jnp.mean(xc * xc, axis=-1, keepdims=True) + EPS)
                gt = gt_ref[rows, cols].astype(F32)
                u_ref[rows, cols] = (gt * _sigmoid(gt) * (on * g_ref[:, cols])).astype(BF)

    hspec = pl.BlockSpec((RET_H, tt, LANE), lambda i: (0, i, 0))
    cspec = pl.BlockSpec((RET_H, CHUNK, LANE), lambda i: (0, 0, 0))
    dspec = pl.BlockSpec((RET_H, CHUNK, CHUNK), lambda i: (0, 0, 0))
    sspec = pl.BlockSpec((RET_H, LANE, LANE), lambda i: (0, 0, 0))
    return pl.pallas_call(
        body, name="ret_fwd", grid=(T // tt,),
        in_specs=[hspec, hspec, pl.BlockSpec((tt, 512), lambda i: (i, 1)), pl.BlockSpec((tt, 512), lambda i: (i, 2)),
                  pl.BlockSpec((1, 512), lambda i: (0, 0)), dspec, cspec, cspec, sspec],
        out_specs=[pl.BlockSpec((tt, 512), lambda i: (i, 0)), pl.BlockSpec((tt, 512), lambda i: (i, 0)), hspec],
        out_shape=[jax.ShapeDtypeStruct((T, 512), F32), jax.ShapeDtypeStruct((T, 512), BF),
                   jax.ShapeDtypeStruct((RET_H, T, LANE), F32)],
        scratch_shapes=[pltpu.VMEM((RET_H, LANE, LANE), F32)],
        compiler_params=_params(("arbitrary",), VMEM_BIG),
    )(qr, kr, z_a, z_a, g_ret, decay, zeta, xi, gcb)


PRUNE_LOG = -110.0
TAME_LOGIT_SPAN = 60.0


def _prune_tables(c, nmax, sub):
    n = c.shape[0] // sub
    u = jnp.sqrt(nmax[0, :FOX_H] * nmax[0, FOX_H:2 * FOX_H]) * 1.02 + 0.5
    first = c[0::sub, :FOX_H].T
    last = c[sub - 1::sub, :FOX_H].T
    blk = jnp.arange(n, dtype=jnp.int32)
    needed = (2.0 * u[:, None, None] + first[:, :, None] - last[:, None, :] >= PRUNE_LOG) | (blk[None, :] >= blk[:, None])[None]
    jlo = jnp.argmax(needed, axis=2).astype(jnp.int32)

    def end_of(key_block):
        reach = jlo[:, None, :] <= key_block[None, :, None]
        return (n - jnp.argmax(reach[:, :, ::-1], axis=2)).astype(jnp.int32)

    sup = 2 * jnp.arange(n // 2, dtype=jnp.int32)
    end_last = end_of(sup + 1)
    end_both = jnp.clip(end_of(sup), sup[None, :] + 2, end_last)
    tame = (2.0 * u < TAME_LOGIT_SPAN).astype(jnp.int32)
    return jlo, end_both, end_last, tame


def _fox_fwd(jlo, tame, q, k, v, sub=FOX_SUB):
    H, T, _ = q.shape
    tb = 2 * sub

    def body(js_ref, tame_ref, q_ref, k_ref, v_ref, o_ref, q2_ref, mx_sc, acc_sc):
        i = pl.program_id(1)
        hd = pl.program_id(0)
        starts = [jnp.minimum(js_ref[hd, 2 * i], 2 * i), jnp.minimum(js_ref[hd, 2 * i + 1], 2 * i)]
        lane = lax.broadcasted_iota(jnp.int32, (sub, LANE), 1)
        row = lax.broadcasted_iota(jnp.int32, (sub, sub), 0)
        col = lax.broadcasted_iota(jnp.int32, (sub, sub), 1)
        causal = row >= col
        qs = [q_ref[0:sub, :], q_ref[sub:tb, :]]
        d0 = pl.multiple_of(i * tb, tb)
        d1 = pl.multiple_of(i * tb + sub, sub)
        k0, k1 = k_ref[pl.ds(d0, sub), :], k_ref[pl.ds(d1, sub), :]
        v0, v1 = v_ref[pl.ds(d0, sub), :], v_ref[pl.ds(d1, sub), :]

        def lane_max(s):
            m = s[:, 0:LANE]
            for c in range(1, s.shape[1] // LANE):
                m = jnp.maximum(m, s[:, c * LANE:(c + 1) * LANE])
            return m

        def put3(base, first, val):
            hi, mid, lo = _split3(val)
            return jnp.where(lane == first, hi, jnp.where(lane == first + 1, mid, jnp.where(lane == first + 2, lo, base)))

        def row_max():
            mx_sc[...] = jnp.full(mx_sc.shape, NEG, F32)
            for a in range(2):
                def max_body(j, carry, a=a):
                    kb = k_ref[pl.ds(pl.multiple_of(j * sub, sub), sub), :]
                    mx_sc[a] = jnp.maximum(mx_sc[a], lane_max(_nt(qs[a], kb)))
                    return carry

                lax.fori_loop(starts[a], 2 * i, max_body, 0)
            mx = [jnp.maximum(mx_sc[0], lane_max(jnp.where(causal, _nt(qs[0], k0), NEG))),
                  jnp.maximum(jnp.maximum(mx_sc[1], lane_max(_nt(qs[1], k0))),
                              lane_max(jnp.where(causal, _nt(qs[1], k1), NEG)))]
            return [jnp.max(t, axis=1, keepdims=True) for t in mx]

        def diag_logit():
            return [jnp.sum(qs[a].astype(F32) * kd.astype(F32), axis=1, keepdims=True) for a, kd in enumerate((k0, k1))]

        def finish(ms):
            qm = [put3(qs[a], L_MAX, -ms[a]) for a in range(2)]
            acc_sc[...] = jnp.zeros(acc_sc.shape, F32)
            for a in range(2):
                def acc_body(j, carry, a=a):
                    off = pl.multiple_of(j * sub, sub)
                    acc_sc[a] += _nn(jnp.exp(_nt(qm[a], k_ref[pl.ds(off, sub), :])).astype(BF), v_ref[pl.ds(off, sub), :])
                    return carry

                lax.fori_loop(starts[a], 2 * i, acc_body, 0)

            def pv(qa, kk, vv, masked):
                p = jnp.exp(_nt(qa, kk))
                if masked:
                    p = jnp.where(causal, p, 0.0)
                return _nn(p.astype(BF), vv)

            accs = [acc_sc[0] + pv(qm[0], k0, v0, True),
                    acc_sc[1] + pv(qm[1], k0, v0, False) + pv(qm[1], k1, v1, True)]
            for a in range(2):
                rows = slice(a * sub, (a + 1) * sub)
                l = accs[a][:, 64:65]
                o_ref[rows, :] = jnp.where(lane < 64, accs[a] / l, 0.0)
                q2_ref[rows, :] = put3(qs[a], L_LSE, -(ms[a] + jnp.log(l)))

        tame = tame_ref[hd] == 1

        @pl.when(tame)
        def _():
            finish(diag_logit())

        @pl.when(jnp.logical_not(tame))
        def _():
            finish(row_max())

    blk = pl.BlockSpec((None, tb, LANE), lambda h, i, js, tm_: (h, i, 0))
    full = pl.BlockSpec((None, T, LANE), lambda h, i, js, tm_: (h, 0, 0))
    return pl.pallas_call(
        body, name="fox_fwd",
        grid_spec=pltpu.PrefetchScalarGridSpec(
            num_scalar_prefetch=2, grid=(H, T // tb), in_specs=[blk, full, full], out_specs=[blk, blk],
            scratch_shapes=[pltpu.VMEM((2, sub, LANE), F32), pltpu.VMEM((2, sub, LANE), F32)]),
        out_shape=[jax.ShapeDtypeStruct((H, T, LANE), F32), jax.ShapeDtypeStruct((H, T, LANE), BF)],
        compiler_params=_params(("parallel", "arbitrary"), VMEM_BIG),
    )(jlo, tame, q, k, v)


def _merge_out(u_r, o_fox, z_a, x, g_ffn, w_ro, w_fo, w_out, tm=256):
    T = x.shape[0]

    def body(u_ref, of_ref, ar_ref, af_ref, x_ref, g_ref, wro_ref, wfo_ref, wout_ref,
             yr_ref, yf_ref, m_ref, x2_ref, h2_ref, oc_ref):
        u = u_ref[...]
        oc = jnp.concatenate([of_ref[h][:, :FOX_D] for h in range(FOX_H)], axis=-1).astype(BF)
        oc_ref[...] = oc
        yr = jnp.concatenate([_nn(u, wro_ref[k]) for k in range(N_CHIP)], axis=-1)
        yf = jnp.concatenate([_nn(oc, wfo_ref[k]) for k in range(N_CHIP)], axis=-1)
        yr_ref[...] = yr.astype(BF)
        yf_ref[...] = yf.astype(BF)
        m = (_sigmoid(ar_ref[...].astype(F32)) * yr + _sigmoid(af_ref[...].astype(F32)) * yf).astype(BF)
        m_ref[...] = m
        x2 = x_ref[...]
        for k in range(N_CHIP):
            x2 = x2 + _nn(m[:, 256 * k:256 * k + 256], wout_ref[k])
        x2_ref[...] = x2
        r = lax.rsqrt(jnp.mean(x2 * x2, axis=-1, keepdims=True) + EPS)
        h2_ref[...] = (x2 * r * g_ref[...]).astype(BF)

    row = lambda w: pl.BlockSpec((tm, w), lambda i: (i, 0))
    const = lambda shp: pl.BlockSpec(shp, lambda i: (0,) * len(shp))
    return pl.pallas_call(
        body, name="merge_out", grid=(T // tm,),
        in_specs=[row(512), pl.BlockSpec((FOX_H, tm, LANE), lambda i: (0, i, 0)),
                  pl.BlockSpec((tm, 1024), lambda i: (i, 3)), pl.BlockSpec((tm, 1024), lambda i: (i, 4)),
                  row(1024), const((1, 1024)), const((N_CHIP, 512, 256)), const((N_CHIP, 512, 256)),
                  const((N_CHIP, 256, 1024))],
        out_specs=[row(1024), row(1024), row(1024), row(1024), row(1024), row(512)],
        out_shape=[jax.ShapeDtypeStruct((T, 1024), BF), jax.ShapeDtypeStruct((T, 1024), BF),
                   jax.ShapeDtypeStruct((T, 1024), BF), jax.ShapeDtypeStruct((T, 1024), F32),
                   jax.ShapeDtypeStruct((T, 1024), BF), jax.ShapeDtypeStruct((T, 512), BF)],
        compiler_params=_params(("parallel",), VMEM_BIG),
    )(u_r, o_fox, z_a, z_a, x, g_ffn, w_ro, w_fo, w_out)


def _load_resident(hbm_refs, vmem_refs, sem):
    cps = [pltpu.make_async_copy(h, v, sem.at[i]) for i, (h, v) in enumerate(zip(hbm_refs, vmem_refs))]
    for cp in cps:
        cp.start()
    for cp in cps:
        cp.wait()


def _ffn_fwd(h2, x2, tgt, w_gate, w_up, w_down, tm=FFN_TM):
    T = h2.shape[0]

    def body(h_ref, x2_ref, t_ref, wg_hbm, wu_hbm, wd_hbm, a_ref, b_ref, act_ref, dy_ref, ls_ref, wg, wu, wd, sem):
        @pl.when(pl.program_id(0) == 0)
        def _():
            _load_resident((wg_hbm, wu_hbm, wd_hbm), (wg, wu, wd), sem)
            ls_ref[...] = jnp.zeros(ls_ref.shape, F32)

        h = h_ref[...]
        err = x2_ref[...] - t_ref[...]
        for k in range(N_CHIP):
            gp = _nt(h, wg[k])
            up = _nt(h, wu[k])
            sg = _sigmoid(gp)
            silu = gp * sg
            a_ref[k] = silu.astype(BF)
            b_ref[k] = (up * sg * (1.0 + gp * (1.0 - sg))).astype(BF)
            act = (silu * up).astype(BF)
            act_ref[k] = act
            err = err + _nn(act, wd[k])
        dy_ref[...] = err * (1.0 / D_MODEL)
        ls_ref[...] += jnp.sum(err * err, axis=0, keepdims=True)

    row = pl.BlockSpec((tm, D_MODEL), lambda i: (i, 0))
    hid = pl.BlockSpec((N_CHIP, tm, FF_SH), lambda i: (0, i, 0))
    anyspec = pl.BlockSpec(memory_space=pl.ANY)
    wshape = pltpu.VMEM((N_CHIP, FF_SH, D_MODEL), BF)
    return pl.pallas_call(
        body, name="ffn_fwd", grid=(T // tm,),
        in_specs=[row, row, row, anyspec, anyspec, anyspec],
        out_specs=[hid, hid, hid, row, pl.BlockSpec((1, D_MODEL), lambda i: (0, 0))],
        out_shape=[jax.ShapeDtypeStruct((N_CHIP, T, FF_SH), BF)] * 3
        + [jax.ShapeDtypeStruct((T, D_MODEL), F32), jax.ShapeDtypeStruct((1, D_MODEL), F32)],
        scratch_shapes=[wshape, wshape, wshape, pltpu.SemaphoreType.DMA((3,))],
        compiler_params=_params(("arbitrary",), VMEM_HUGE),
    )(h2, x2, tgt, w_gate, w_up, w_down)


def _ffn_bwd(dy, sa, sb, x2, g_ffn, w_gate, w_up, w_down, tm=FFN_TM):
    T = dy.shape[0]

    def body(dy_ref, a_ref, b_ref, x2_ref, g_ref, wg_hbm, wu_hbm, wd_hbm, dgp_ref, dup_ref, dx_ref, dg_ref,
             wg, wu, wd, sem):
        @pl.when(pl.program_id(0) == 0)
        def _():
            _load_resident((wg_hbm, wu_hbm, wd_hbm), (wg, wu, wd), sem)
            dg_ref[...] = jnp.zeros(dg_ref.shape, F32)

        dy = dy_ref[...]
        dyb = dy.astype(BF)
        dh = jnp.zeros((tm, D_MODEL), F32)
        for k in range(N_CHIP):
            dact = _nt(dyb, wd[k])
            dup = (dact * a_ref[k]).astype(BF)
            dgp = (dact * b_ref[k]).astype(BF)
            dgp_ref[k] = dgp
            dup_ref[k] = dup
            dh = dh + _nn(dgp, wg[k]) + _nn(dup, wu[k])
        x2 = x2_ref[...]
        r = lax.rsqrt(jnp.mean(x2 * x2, axis=-1, keepdims=True) + EPS)
        xn = x2 * r
        dg_ref[...] += jnp.sum(dh * xn, axis=0, keepdims=True)
        dxn = dh * g_ref[...]
        dx_ref[...] = dy + r * (dxn - xn * jnp.mean(dxn * xn, axis=-1, keepdims=True))

    row = pl.BlockSpec((tm, D_MODEL), lambda i: (i, 0))
    hid = pl.BlockSpec((N_CHIP, tm, FF_SH), lambda i: (0, i, 0))
    vec = pl.BlockSpec((1, D_MODEL), lambda i: (0, 0))
    anyspec = pl.BlockSpec(memory_space=pl.ANY)
    wshape = pltpu.VMEM((N_CHIP, FF_SH, D_MODEL), BF)
    return pl.pallas_call(
        body, name="ffn_bwd", grid=(T // tm,),
        in_specs=[row, hid, hid, row, vec, anyspec, anyspec, anyspec],
        out_specs=[hid, hid, row, vec],
        out_shape=[jax.ShapeDtypeStruct((N_CHIP, T, FF_SH), BF), jax.ShapeDtypeStruct((N_CHIP, T, FF_SH), BF),
                   jax.ShapeDtypeStruct((T, D_MODEL), F32), jax.ShapeDtypeStruct((1, D_MODEL), F32)],
        scratch_shapes=[wshape, wshape, wshape, pltpu.SemaphoreType.DMA((3,))],
        compiler_params=_params(("arbitrary",), VMEM_HUGE),
    )(dy, sa, sb, x2, g_ffn, w_gate, w_up, w_down)


def _out_bwd(dx2, z_a, y_r, y_f, o_raw, o_fox, g_ret, w_ro, w_fo, w_out, tm=256, push=None):
    T = dx2.shape[0]

    def body(dx_ref, gt_ref, ar_ref, af_ref, yr_ref, yf_ref, o_ref, of_ref, g_ref, wro_ref, wfo_ref, wout_ref,
             dyr_ref, dyf_ref, dgt_ref, da_ref, do_ref, dof_ref, dg_ref):
        i = pl.program_id(0)

        @pl.when(i == 0)
        def _():
            dg_ref[...] = jnp.zeros(dg_ref.shape, F32)

        dxb = dx_ref[...].astype(BF)
        dm = jnp.concatenate([_nt(dxb, wout_ref[k]) for k in range(N_CHIP)], axis=-1)
        sr, sf = _sigmoid(ar_ref[...].astype(F32)), _sigmoid(af_ref[...].astype(F32))
        dyr = dm * sr
        dyf = dm * sf
        da_ref[:, :1024] = (dyr * yr_ref[...].astype(F32) * (1.0 - sr)).astype(BF)
        da_ref[:, 1024:] = (dyf * yf_ref[...].astype(F32) * (1.0 - sf)).astype(BF)
        dyr = dyr.astype(BF)
        dyf = dyf.astype(BF)
        dyr_ref[...] = dyr
        dyf_ref[...] = dyf
        du = jnp.zeros((tm, 512), F32)
        doc = jnp.zeros((tm, 512), F32)
        for k in range(N_CHIP):
            du = du + _nt(dyr[:, 256 * k:256 * k + 256], wro_ref[k])
            doc = doc + _nt(dyf[:, 256 * k:256 * k + 256], wfo_ref[k])

        for h in range(RET_H):
            cols = slice(h * RET_DV, (h + 1) * RET_DV)
            o = o_ref[:, cols]
            mu = jnp.mean(o, axis=-1, keepdims=True)
            xc = o - mu
            rstd = lax.rsqrt(jnp.mean(xc * xc, axis=-1, keepdims=True) + EPS)
            on = xc * rstd
            g = g_ref[:, cols]
            gt = gt_ref[:, cols].astype(F32)
            sg = _sigmoid(gt)
            duh = du[:, cols]
            dgt_ref[:, cols] = (duh * (on * g) * sg * (1.0 + gt * (1.0 - sg))).astype(BF)
            dog = duh * gt * sg
            dg_ref[:, cols] += jnp.sum(dog * on, axis=0, keepdims=True)
            don = dog * g
            do_ref[:, cols] = rstd * (don - jnp.mean(don, axis=-1, keepdims=True)
                                      - on * jnp.mean(don * on, axis=-1, keepdims=True))

        lane = lax.broadcasted_iota(jnp.int32, (tm, LANE), 1)
        zpad = jnp.zeros((tm, 64), F32)
        for h in range(FOX_H):
            doh = doc[:, 64 * h:64 * h + 64]
            delta = jnp.sum(doh * of_ref[h][:, :FOX_D], axis=-1, keepdims=True)
            hi, mid, lo = [t.astype(F32) for t in _split3(-delta)]
            da = jnp.concatenate([doh, zpad], axis=-1)
            da = jnp.where(lane == 64, hi, jnp.where(lane == 65, mid, jnp.where(lane == 66, lo, da)))
            dof_ref[h] = da.astype(BF)

    row = lambda w: pl.BlockSpec((tm, w), lambda i: (i, 0))
    const = lambda shp: pl.BlockSpec(shp, lambda i: (0,) * len(shp))
    hsp = pl.BlockSpec((FOX_H, tm, LANE), lambda i: (0, i, 0))
    return _hosted_call(
        body, "out_bwd", (T // tm,),
        [row(1024), pl.BlockSpec((tm, 512), lambda i: (i, 2)), pl.BlockSpec((tm, 1024), lambda i: (i, 3)),
         pl.BlockSpec((tm, 1024), lambda i: (i, 4)), row(1024), row(1024), row(512), hsp,
         const((1, 512)), const((N_CHIP, 512, 256)), const((N_CHIP, 512, 256)), const((N_CHIP, 256, 1024))],
        [row(1024), row(1024), row(512), row(2048), row(512), hsp, const((1, 512))],
        [jax.ShapeDtypeStruct((T, 1024), BF), jax.ShapeDtypeStruct((T, 1024), BF),
         jax.ShapeDtypeStruct((T, 512), BF), jax.ShapeDtypeStruct((T, 2048), BF),
         jax.ShapeDtypeStruct((T, 512), F32), jax.ShapeDtypeStruct((FOX_H, T, LANE), BF),
         jax.ShapeDtypeStruct((1, 512), F32)],
        [], VMEM_BIG, (dx2, z_a, z_a, z_a, y_r, y_f, o_raw, o_fox, g_ret, w_ro, w_fo, w_out), push)


def _ret_bwd(d_o, qr, kr, z_a, states, cos_t, sin_t, consts, tt=512, push=None):
    T = z_a.shape[0]
    nt = T // tt
    nch = tt // CHUNK
    decay, zeta, xi, gcb = consts

    def body(do_ref, q_ref, k_ref, v_ref, st_ref, cos_ref, sin_ref, d_ref, ze_ref, xi_ref, gc_ref, dz_ref, g_sc):
        i = pl.program_id(0)

        @pl.when(i == 0)
        def _():
            g_sc[...] = jnp.zeros(g_sc.shape, F32)

        for c in reversed(range(nch)):
            rows = slice(c * CHUNK, (c + 1) * CHUNK)
            cosv, sinv = cos_ref[rows, :], sin_ref[rows, :]
            dq_parts, dk_parts = [], []
            for h in range(RET_H):
                cols = slice(h * RET_DV, (h + 1) * RET_DV)
                q, k = q_ref[h, rows, :], k_ref[h, rows, :]
                v32 = v_ref[rows, cols].astype(F32)
                vb = v32.astype(BF)
                r = st_ref[h, c * CHUNK:c * CHUNK + LANE, :]
                g = g_sc[h]
                gb = g.astype(BF)
                d_o = do_ref[rows, cols]
                dob = d_o.astype(BF)
                dox = (d_o * xi_ref[h]).astype(BF)
                dec = d_ref[h]
                s = (_nt(q, k) * dec).astype(BF)
                ds = (_nt(dob, vb) * dec).astype(BF)
                dv = _tn(s, dob) + ze_ref[h] * _nn(k, gb)
                dq = _nn(ds, k) + _nt(dox, r.astype(BF))
                dk = _tn(ds, q) + _nt((v32 * ze_ref[h]).astype(BF), gb)
                g_sc[h] = gc_ref[h] * g + _tn(q, dox)
                dq_parts.append((dq * cosv - _swap32(dq) * sinv)[:, :64])
                dk_parts.append(((dk * cosv - _swap32(dk) * sinv) * 0.125)[:, :64])
                dz_ref[rows, 512 + h * RET_DV:512 + (h + 1) * RET_DV] = dv.astype(BF)
            dz_ref[rows, 0:256] = jnp.concatenate(dq_parts, axis=-1).astype(BF)
            dz_ref[rows, 256:512] = jnp.concatenate(dk_parts, axis=-1).astype(BF)

    rev = lambda i: nt - 1 - i
    hspec = pl.BlockSpec((RET_H, tt, LANE), lambda i: (0, rev(i), 0))
    cspec = pl.BlockSpec((RET_H, CHUNK, LANE), lambda i: (0, 0, 0))
    tab = pl.BlockSpec((tt, LANE), lambda i: (rev(i), 0))
    (dz,), lands = _hosted_call(
        body, "ret_bwd", (nt,),
        [pl.BlockSpec((tt, 512), lambda i: (rev(i), 0)), hspec, hspec,
         pl.BlockSpec((tt, 512), lambda i: (rev(i), 1)), hspec, tab, tab,
         pl.BlockSpec((RET_H, CHUNK, CHUNK), lambda i: (0, 0, 0)), cspec, cspec,
         pl.BlockSpec((RET_H, LANE, LANE), lambda i: (0, 0, 0))],
        [pl.BlockSpec((tt, 1024), lambda i: (rev(i), 0))], [jax.ShapeDtypeStruct((T, 1024), BF)],
        [pltpu.VMEM((RET_H, LANE, LANE), F32)], VMEM_BIG,
        (d_o, qr, kr, z_a, states, cos_t, sin_t, decay, zeta, xi, gcb), push)
    return dz, lands


def _fox_bwd(end_both, end_last, q2, k, v, do, sub=FOX_SUB):
    H, T, _ = k.shape
    tb = 2 * sub

    def body(eb_ref, el_ref, q_ref, do_ref, k_ref, v_ref, dq_ref, dk_ref, dv_ref, dk_sc, dv_sc):
        j = pl.program_id(1)
        n_both = eb_ref[pl.program_id(0), j]
        n_last = el_ref[pl.program_id(0), j]

        @pl.when(j == 0)
        def _():
            dq_ref[...] = jnp.zeros(dq_ref.shape, F32)

        dk_sc[...] = jnp.zeros(dk_sc.shape, F32)
        dv_sc[...] = jnp.zeros(dv_sc.shape, F32)
        krow = lax.broadcasted_iota(jnp.int32, (tb, sub), 0)
        qcol = lax.broadcasted_iota(jnp.int32, (tb, sub), 1)

        def step(i, r0, r1, shift):
            off = pl.multiple_of(i * sub, sub)
            qq = q_ref[pl.ds(off, sub), :]
            dd = do_ref[pl.ds(off, sub), :]
            kk, vv = k_ref[r0:r1, :], v_ref[r0:r1, :]
            p = jnp.exp(_nt(kk, qq))
            if shift is not None:
                p = jnp.where(qcol[0:r1 - r0, :] + shift >= krow[0:r1 - r0, :], p, 0.0)
            ds = (p * _nt(vv, dd)).astype(BF)
            dv_sc[r0:r1, :] += _nn(p.astype(BF), dd)
            dk_sc[r0:r1, :] += _nn(ds, qq)
            dq_ref[pl.ds(off, sub), :] += _tn(ds, kk)

        step(2 * j, 0, sub, 0)
        step(2 * j + 1, 0, tb, sub)

        def both_body(i, carry):
            step(i, 0, tb, None)
            return carry

        def last_body(i, carry):
            step(i, sub, tb, None)
            return carry

        lax.fori_loop(2 * j + 2, n_both, both_body, 0)
        lax.fori_loop(n_both, n_last, last_body, 0)
        dk_ref[...] = dk_sc[...]
        dv_ref[...] = dv_sc[...]

    blk = pl.BlockSpec((None, tb, LANE), lambda h, j, eb, el: (h, j, 0))
    full = pl.BlockSpec((None, T, LANE), lambda h, j, eb, el: (h, 0, 0))
    shp = jax.ShapeDtypeStruct((H, T, LANE), F32)
    return pl.pallas_call(
        body, name="fox_bwd",
        grid_spec=pltpu.PrefetchScalarGridSpec(
            num_scalar_prefetch=2, grid=(H, T // tb), in_specs=[full, full, blk, blk], out_specs=[full, blk, blk],
            scratch_shapes=[pltpu.VMEM((tb, LANE), F32), pltpu.VMEM((tb, LANE), F32)]),
        out_shape=[shp, shp, shp],
        compiler_params=_params(("arbitrary", "arbitrary"), VMEM_BIG),
    )(end_both, end_last, q2, do, k, v)


def _fox_post_bwd(dq, dk, dv, z_a, z_ff, b_f, g_q, g_k, tm=256, push=None):
    T = z_a.shape[0]
    nt = T // tm

    def body(dq_ref, dk_ref, dv_ref, zf_ref, zff_ref, b_ref, g_ref, sc_ref, seg_ref, segt_ref,
             dz_ref, dff_ref, dg_ref, db_ref, carry):
        i = pl.program_id(0)

        @pl.when(i == 0)
        def _():
            carry[...] = jnp.zeros(carry.shape, F32)
            dg_ref[...] = jnp.zeros(dg_ref.shape, F32)
            db_ref[...] = jnp.zeros(db_ref.shape, F32)

        lane = lax.broadcasted_iota(jnp.int32, (tm, LANE), 1)
        dcm = jnp.zeros((tm, LANE), F32)
        for h in range(FOX_H):
            dcm = jnp.where(lane == h, dq_ref[h][:, L_CQ:L_CQ + 1] - dk_ref[h][:, L_CK:L_CK + 1], dcm)

        def seg_mean(v):
            return sum(_nn(t, seg_ref[...]) for t in _split3(v)) * (1.0 / FOX_D)

        def seg_bcast(v):
            return sum(_nn(t, segt_ref[...]) for t in _split3(v))

        x = zf_ref[:, :1024].astype(F32)
        dy = jnp.concatenate([dq_ref[h][:, :FOX_D] for h in range(FOX_H)]
                             + [dk_ref[h][:, :FOX_D] for h in range(FOX_H)], axis=-1) * sc_ref[...]
        rb = seg_bcast(lax.rsqrt(seg_mean(x * x) + EPS))
        xn = x * rb
        dg_ref[...] += jnp.sum(dy * xn, axis=0, keepdims=True)
        dxn = dy * g_ref[...]
        dz_ref[:, :1024] = (rb * (dxn - xn * seg_bcast(seg_mean(dxn * xn)))).astype(BF)
        dz_ref[:, 1024:] = jnp.concatenate([dv_ref[h][:, :FOX_D] for h in range(FOX_H)], axis=-1).astype(BF)

        row = lax.broadcasted_iota(jnp.int32, (tm, tm), 0)
        col = lax.broadcasted_iota(jnp.int32, (tm, tm), 1)
        tri = (row <= col).astype(BF)
        hi, mid, lo = _split3(dcm)
        dlogf = _nn(tri, hi) + _nn(tri, mid) + _nn(tri, lo) + carry[...]
        carry[...] = dlogf[0:1, :]
        dff = jnp.where(lane < FOX_H, dlogf * _sigmoid(-(zff_ref[...] + b_ref[...])), 0.0)
        dff_ref[...] = dff.astype(BF)
        db_ref[...] += jnp.sum(dff, axis=0, keepdims=True)

    rev = lambda i: nt - 1 - i
    hsp = pl.BlockSpec((FOX_H, tm, LANE), lambda i: (0, rev(i), 0))
    const = lambda r, w: pl.BlockSpec((r, w), lambda i: (0, 0))
    seg = _segment_matrix()
    g_all = jnp.concatenate([jnp.tile(g_q, (1, FOX_H)), jnp.tile(g_k, (1, FOX_H))], axis=1)
    scale = jnp.asarray(np.concatenate([np.full((1, 512), 0.125, np.float32), np.ones((1, 512), np.float32)], axis=1))
    (dz, dff, dg, db), lands = _hosted_call(
        body, "fox_post_bwd", (nt,),
        [hsp, hsp, hsp, pl.BlockSpec((tm, 1536), lambda i: (rev(i), 1)),
         pl.BlockSpec((tm, LANE), lambda i: (rev(i), 0)), const(1, LANE), const(1, 1024), const(1, 1024),
         const(1024, LANE), const(LANE, 1024)],
        [pl.BlockSpec((tm, 1536), lambda i: (rev(i), 0)), pl.BlockSpec((tm, LANE), lambda i: (rev(i), 0)),
         const(1, 1024), const(1, LANE)],
        [jax.ShapeDtypeStruct((T, 1536), BF), jax.ShapeDtypeStruct((T, LANE), BF),
         jax.ShapeDtypeStruct((1, 1024), F32), jax.ShapeDtypeStruct((1, LANE), F32)],
        [pltpu.VMEM((1, LANE), F32)], VMEM_BIG, (dq, dk, dv, z_a, z_ff, b_f, g_all, scale, seg, seg.T), push)
    dg_heads = dg.reshape(2, FOX_H, FOX_D).sum(axis=1)
    return (dz, dff, dg_heads[0:1], dg_heads[1:2], db), lands


def _in_bwd(dz_ret, dz_gt, dz_fox, dz_a, dz_ff, w_a, w_ff, x, g_mix, dx2, tm=256, push=None):
    T = x.shape[0]

    def body(r_ref, t_ref, f_ref, a_ref, ff_ref, wa_ref, wf_ref, x_ref, g_ref, dx2_ref, dx_ref, dg_ref):
        i = pl.program_id(0)

        @pl.when(i == 0)
        def _():
            dg_ref[...] = jnp.zeros(dg_ref.shape, F32)

        dh = (_nt(r_ref[...], wa_ref[:, C_RET:C_GT]) + _nt(t_ref[...], wa_ref[:, C_GT:C_FOX])
              + _nt(f_ref[...], wa_ref[:, C_FOX:C_A]) + _nt(a_ref[...], wa_ref[:, C_A:C_END])
              + _nt(ff_ref[...], wf_ref[...]))
        xv = x_ref[...]
        r = lax.rsqrt(jnp.mean(xv * xv, axis=-1, keepdims=True) + EPS)
        xn = xv * r
        dg_ref[...] += jnp.sum(dh * xn, axis=0, keepdims=True)
        dxn = dh * g_ref[...]
        dx_ref[...] = dx2_ref[...] + r * (dxn - xn * jnp.mean(dxn * xn, axis=-1, keepdims=True))

    row = lambda w: pl.BlockSpec((tm, w), lambda i: (i, 0))
    const = lambda shp: pl.BlockSpec(shp, lambda i: (0,) * len(shp))
    return _hosted_call(
        body, "in_bwd", (T // tm,),
        [row(1024), row(512), row(1536), row(2048), row(LANE), const((D_MODEL, C_END)),
         const((D_MODEL, LANE)), row(1024), const((1, 1024)), row(1024)],
        [row(1024), const((1, 1024))],
        [jax.ShapeDtypeStruct((T, 1024), F32), jax.ShapeDtypeStruct((1, 1024), F32)],
        [], VMEM_BIG, (dz_ret, dz_gt, dz_fox, dz_a, dz_ff, w_a, w_ff, x, g_mix, dx2), push)


def _mesh_pos():
    return lax.axis_index("x"), lax.axis_index("y"), lax.axis_index("c")


def _staged_place(src, name):
    stacked = src.ndim == 3
    R, C = src.shape[-2:]
    tr = _row_tile(R, 128, 16)
    n = R // tr
    assert n >= 2

    def body(s_ref, o_ref, buf, sem):
        i = pl.program_id(0)
        slot = i % 2
        x, y, _ = _mesh_pos()
        kme = 2 * x + y

        def out_copy(s, step):
            return pltpu.make_async_copy(buf.at[s], o_ref.at[kme, pl.ds(pl.multiple_of(step * tr, tr), tr), :], sem.at[s])

        @pl.when(i >= 2)
        def _():
            out_copy(slot, i - 2).wait()

        buf[slot] = (s_ref[kme] if stacked else s_ref[...]).astype(BF)
        out_copy(slot, i).start()

        @pl.when(i == n - 1)
        def _():
            out_copy(1 - slot, i - 1).wait()
            out_copy(slot, i).wait()

    in_spec = (pl.BlockSpec((N_CHIP, tr, C), lambda i: (0, i, 0)) if stacked else pl.BlockSpec((tr, C), lambda i: (i, 0)))
    return pl.pallas_call(
        body, name=name, grid=(n,), in_specs=[in_spec], out_specs=pl.BlockSpec(memory_space=pl.ANY),
        out_shape=jax.ShapeDtypeStruct((N_CHIP, R, C), BF),
        scratch_shapes=[pltpu.VMEM((2, tr, C), BF), pltpu.SemaphoreType.DMA((2,))],
        compiler_params=_params(("arbitrary",)),
    )(src)


def _push_copies(src, land, send_sem, recv_sem, receiving):
    x, y, c = _mesh_pos()
    kme = 2 * x + y
    cps = []
    for w in range(len(land)):
        for j, (px, py) in enumerate([(1 - x, y), (x, 1 - y), (1 - x, 1 - y)]):
            kpeer = 2 * px + py
            cps.append(pltpu.make_async_remote_copy(
                src_ref=land[w].at[kme] if src is None else src[w].at[kpeer],
                dst_ref=land[w].at[kpeer if receiving else kme],
                send_sem=send_sem.at[3 * w + j], recv_sem=recv_sem.at[3 * w + j],
                device_id=(px, py, c), device_id_type=MESH))
    return cps


def _gather_two_level(stack, name):
    _, R, C = stack.shape
    hr = R // 2

    def body(_, land, send_sem, recv_sem):
        x, y, c = _mesh_pos()
        kme = 2 * x + y
        chips = [(1 - x, y), (x, 1 - y), (1 - x, 1 - y)]

        def rows(k, core):
            return land.at[k, pl.ds(pl.multiple_of(core * hr, hr), hr), :]

        def copy(idx, k, core, to):
            return pltpu.make_async_remote_copy(src_ref=rows(k, core), dst_ref=rows(k, core), send_sem=send_sem.at[idx],
                                                recv_sem=recv_sem.at[idx], device_id=to, device_id_type=MESH)

        first = [copy(j, kme, c, (px, py, c)) for j, (px, py) in enumerate(chips)]
        for cp in first:
            cp.start()
        passed = [copy(3 + j, 2 * px + py, c, (x, y, 1 - c)) for j, (px, py) in enumerate(chips)]
        for j, (px, py) in enumerate(chips):
            copy(j, 2 * px + py, c, (px, py, c)).wait_recv()
            passed[j].start()
        for j, (px, py) in enumerate(chips):
            copy(3 + j, 2 * px + py, 1 - c, (x, y, 1 - c)).wait_recv()
        for cp in first + passed:
            cp.wait_send()

    anyspec = pl.BlockSpec(memory_space=pl.ANY)
    return pl.pallas_call(
        body, name=name, in_specs=[anyspec], out_specs=anyspec,
        out_shape=jax.ShapeDtypeStruct(stack.shape, stack.dtype), input_output_aliases={0: 0},
        scratch_shapes=[pltpu.SemaphoreType.DMA((6,)), pltpu.SemaphoreType.DMA((6,))],
    )(stack)


def _gather_small(small):
    def body(sv, svo, ssend, srecv, sloc):
        x, y, c = _mesh_pos()
        me = 4 * x + 2 * y + c
        flips = [(b >> 2 & 1, b >> 1 & 1, b & 1) for b in range(1, 8)]
        others = [(1 - x if fx else x, 1 - y if fy else y, 1 - c if fc else c) for fx, fy, fc in flips]
        local = pltpu.make_async_copy(sv, svo.at[me], sloc)
        local.start()
        sends = []
        for j, (px, py, pc) in enumerate(others):
            cp = pltpu.make_async_remote_copy(
                src_ref=sv, dst_ref=svo.at[me], send_sem=ssend.at[j], recv_sem=srecv.at[j],
                device_id=(px, py, pc), device_id_type=MESH)
            cp.start()
            sends.append(cp)
        for j, (px, py, pc) in enumerate(others):
            pltpu.make_async_remote_copy(
                src_ref=sv, dst_ref=svo.at[4 * px + 2 * py + pc], send_sem=ssend.at[j], recv_sem=srecv.at[j],
                device_id=(px, py, pc), device_id_type=MESH).wait_recv()
        for cp in sends:
            cp.wait_send()
        local.wait()

    anyspec = pl.BlockSpec(memory_space=pl.ANY)
    return pl.pallas_call(
        body, name="gather_small", in_specs=[anyspec], out_specs=anyspec,
        out_shape=jax.ShapeDtypeStruct((8,) + small.shape, small.dtype),
        scratch_shapes=[pltpu.SemaphoreType.DMA((7,)), pltpu.SemaphoreType.DMA((7,)), pltpu.SemaphoreType.DMA],
    )(small)


def _sibling_exchange(arrs):
    n = len(arrs)

    def body(*refs):
        ins, outs = refs[:n], refs[n:2 * n]
        send_sems, recv_sems = refs[2 * n:]
        x, y, c = _mesh_pos()
        cps = [pltpu.make_async_remote_copy(
            src_ref=ins[w], dst_ref=outs[w], send_sem=send_sems.at[w], recv_sem=recv_sems.at[w],
            device_id=(x, y, 1 - c), device_id_type=MESH) for w in range(n)]
        for cp in cps:
            cp.start()
        for cp in cps:
            cp.wait_recv()
        for cp in cps:
            cp.wait_send()

    anyspec = pl.BlockSpec(memory_space=pl.ANY)
    return pl.pallas_call(
        body, name="sibling_exchange",
        in_specs=[anyspec] * n, out_specs=[anyspec] * n,
        out_shape=[jax.ShapeDtypeStruct(a.shape, a.dtype) for a in arrs],
        scratch_shapes=[pltpu.SemaphoreType.DMA((n,)), pltpu.SemaphoreType.DMA((n,))],
    )(*arrs)


def _sum_stack(own, recv, name):
    _, R, C = recv.shape
    tr = _row_tile(R, 256, 16)

    def body(g_ref, r_ref, o_ref):
        x, y, _ = _mesh_pos()
        kme = 2 * x + y
        acc = g_ref[kme].astype(F32)
        for d in range(1, N_CHIP):
            acc = acc + r_ref[(kme + d) % N_CHIP].astype(F32)
        o_ref[...] = acc

    spec = pl.BlockSpec((N_CHIP, tr, C), lambda i: (0, i, 0))
    return pl.pallas_call(
        body, name=name, grid=(R // tr,), in_specs=[spec, spec],
        out_specs=pl.BlockSpec((tr, C), lambda i: (i, 0)),
        out_shape=jax.ShapeDtypeStruct((R, C), F32),
        compiler_params=_params(("parallel",)),
    )(own, recv)


def _adam_math(w, g, m, v):
    m2 = ADAM_B1 * m + (1.0 - ADAM_B1) * g
    v2 = ADAM_B2 * v + (1.0 - ADAM_B2) * (g * g)
    m_hat = m2 / (1.0 - ADAM_B1 ** ADAM_STEP)
    v_hat = v2 / (1.0 - ADAM_B2 ** ADAM_STEP)
    delta = -ADAM_LR * (m_hat / (jnp.sqrt(v_hat) + ADAM_EPS) + ADAM_WD * w)
    return delta, m2, v2


def _adamw(w, m, v, s0, s1, name):
    R, C = w.shape
    tr = _row_tile(R, 128, 8)

    def body(w_ref, m_ref, v_ref, a_ref, b_ref, g_ref, d_ref, m2_ref, v2_ref):
        g = a_ref[...] + b_ref[...]
        delta, m2, v2 = _adam_math(w_ref[...], g, m_ref[...], v_ref[...])
        g_ref[...] = g
        d_ref[...] = delta
        m2_ref[...] = m2
        v2_ref[...] = v2

    spec = pl.BlockSpec((tr, C), lambda i: (i, 0))
    shp = jax.ShapeDtypeStruct((R, C), F32)
    return pl.pallas_call(
        body, name=name, grid=(R // tr,), in_specs=[spec] * 5, out_specs=[spec] * 4, out_shape=[shp] * 4,
        compiler_params=_params(("parallel",), VMEM_BIG),
    )(w, m, v, s0, s1)


def _adamw_small(ws, ms, vs, gathered):
    n = len(SMALL)

    def body(*refs):
        w_refs, m_refs, v_refs, s_ref = refs[:n], refs[n:2 * n], refs[2 * n:3 * n], refs[3 * n]
        outs = refs[3 * n + 1:]
        g_all = s_ref[0]
        for d in range(1, 8):
            g_all = g_all + s_ref[d]
        off = 0
        for i, (_, width) in enumerate(SMALL):
            g = g_all[:, off:off + width]
            delta, m2, v2 = _adam_math(w_refs[i][...], g, m_refs[i][...], v_refs[i][...])
            for kind, val in enumerate((g, delta, m2, v2)):
                outs[kind * n + i][...] = val
            off += width + (-width % LANE)

    shapes = [jax.ShapeDtypeStruct((1, width), F32) for _, width in SMALL]
    res = pl.pallas_call(body, name="adamw_small", out_shape=shapes * 4)(*ws, *ms, *vs, gathered)
    return [dict(zip([nm for nm, _ in SMALL], res[kind * n:(kind + 1) * n])) for kind in range(4)]


SMALL = (("g_mix", 1024), ("g_ffn", 1024), ("g_ret_norm", 512), ("g_fox_q", 64), ("g_fox_k", 64), ("b_forget", 8))
SMALL_W = 3072


def _pack_small(parts):
    cols = []
    for (name, n) in SMALL:
        p = parts[name].reshape(1, -1)[:, :n]
        pad = -n % LANE
        cols.append(jnp.pad(p, ((0, 0), (0, pad))) if pad else p)
    used = sum(c.shape[1] for c in cols)
    cols.append(jnp.zeros((1, SMALL_W - used), F32))
    return jnp.concatenate(cols, axis=1)


def kernel(x, g_mix, w_in, b_forget, g_ret_norm, w_ret_o, g_fox_q, g_fox_k, w_fox_o, w_out, g_ffn, w_gate, w_up, w_down, loss_target, m_g_mix, m_w_in, m_b_forget, m_g_ret_norm, m_w_ret_o, m_g_fox_q, m_g_fox_k, m_w_fox_o, m_w_out, m_g_ffn, m_w_gate, m_w_up, m_w_down, v_g_mix, v_w_in, v_b_forget, v_g_ret_norm, v_w_ret_o, v_g_fox_q, v_g_fox_k, v_w_fox_o, v_w_out, v_g_ffn, v_w_gate, v_w_up, v_w_down):
    T = x.shape[1]
    xs = x[0]
    tgt = loss_target[0]
    big_names = ("w_in", "w_ret_o", "w_fox_o", "w_out", "w_gate", "w_up", "w_down")
    tr = lambda a: jnp.swapaxes(a[0], 0, 1)
    big_w = dict(w_in=w_in[0], w_ret_o=w_ret_o[0], w_fox_o=w_fox_o[0], w_out=w_out[0], w_gate=tr(w_gate),
                 w_up=tr(w_up), w_down=w_down[0])
    big_m = dict(w_in=m_w_in[0], w_ret_o=m_w_ret_o[0], w_fox_o=m_w_fox_o[0], w_out=m_w_out[0], w_gate=tr(m_w_gate),
                 w_up=tr(m_w_up), w_down=m_w_down[0])
    big_v = dict(w_in=v_w_in[0], w_ret_o=v_w_ret_o[0], w_fox_o=v_w_fox_o[0], w_out=v_w_out[0], w_gate=tr(v_w_gate),
                 w_up=tr(v_w_up), w_down=v_w_down[0])
    small_w = dict(g_mix=g_mix, g_ffn=g_ffn, g_ret_norm=g_ret_norm, g_fox_q=g_fox_q, g_fox_k=g_fox_k, b_forget=b_forget)
    small_m = dict(g_mix=m_g_mix, g_ffn=m_g_ffn, g_ret_norm=m_g_ret_norm, g_fox_q=m_g_fox_q, g_fox_k=m_g_fox_k,
                   b_forget=m_b_forget)
    small_v = dict(g_mix=v_g_mix, g_ffn=v_g_ffn, g_ret_norm=v_g_ret_norm, g_fox_q=v_g_fox_q, g_fox_k=v_g_fox_k,
                   b_forget=v_b_forget)

    stacks = {n: _staged_place(big_w[n], "place_" + n) for n in big_names}
    s_in = _gather_two_level(stacks["w_in"], "gather_w_in")
    w_a, w_ff = _assemble_w_in(s_in)
    b_pad = jnp.pad(b_forget, ((0, 0), (0, LANE - FOX_H)))
    cos_t, sin_t = _rope_tables(T)
    consts = _ret_consts()

    h = _rms_cast(xs, g_mix)
    z_a, (s_gate, s_up) = _mm_nn(h, w_a, "proj_in", BF, tm=1024, push=(None, [stacks["w_gate"], stacks["w_up"]]))
    z_ff, _ = _mm_nn(h, w_ff, "proj_ff", F32)
    (qr, kr, qf, kf, vf, c_cum, nmax), (s_down, s_ro, s_fo, s_out) = _mix_prep(
        z_a, z_ff, cos_t, sin_t, b_pad, g_fox_q, g_fox_k,
        push=(None, [stacks["w_down"], stacks["w_ret_o"], stacks["w_fox_o"], stacks["w_out"]]))
    jlo, end_both, end_last, tame = _prune_tables(c_cum, nmax, FOX_SUB)
    o_raw, u_r, states = _ret_fwd(qr, kr, z_a, g_ret_norm, consts)
    o_fox, q2 = _fox_fwd(jlo, tame, qf, kf, vf)
    y_r, y_f, mrg, x2, h2, o_cat = _merge_out(u_r, o_fox, z_a, xs, g_ffn, s_ro, s_fo, s_out)
    sa, sb, act, dy, loss_vec = _ffn_fwd(h2, x2, tgt, s_gate, s_up, s_down)
    loss = lax.psum(0.5 / D_MODEL * jnp.sum(loss_vec), ("x", "y", "c"))

    def scatter_job(grads):
        return (grads, [lax.empty(g.shape, g.dtype) for g in grads])

    dgp, dup, dx2, dg_ffn = _ffn_bwd(dy, sa, sb, x2, g_ffn, s_gate, s_up, s_down)
    g_gate, g_up, g_down = (_grad_astack(dgp, h2, "gw_gate"), _grad_astack(dup, h2, "gw_up"),
                            _grad_astack(act, dy, "gw_down"))
    (d_yr, d_yf, dz_gt, dz_a, d_o, do_fox, dg_ret), (r_gate, r_up) = _out_bwd(
        dx2, z_a, y_r, y_f, o_raw, o_fox, g_ret_norm, s_ro, s_fo, s_out,
        push=scatter_job([g_gate, g_up]))
    dz_ret, (r_down,) = _ret_bwd(d_o, qr, kr, z_a, states, cos_t, sin_t, consts, push=scatter_job([g_down]))
    dq_f, dk_f, dv_f = _fox_bwd(end_both, end_last, q2, kf, vf, do_fox)
    g_mid = [_grad_colstack(u_r, d_yr, "gw_ret_o", 256), _grad_colstack(o_cat, d_yf, "gw_fox_o", 256),
             _grad_plain(mrg, dx2, "gw_out", BF).reshape(N_CHIP, 256, D_MODEL)]
    (dz_fox, dz_ff, dg_q, dg_k, db_f), (r_ro, r_fo, r_out) = _fox_post_bwd(
        dq_f, dk_f, dv_f, z_a, z_ff, b_pad, g_fox_q, g_fox_k, push=scatter_job(g_mid))
    g_in = _pack_g_in(_grad_plain(h, dz_ret, "gw_in_ret", F32), _grad_plain(h, dz_gt, "gw_in_gt", F32),
                      _grad_plain(h, dz_fox, "gw_in_fox", F32, tn=1536),
                      _grad_plain(h, dz_a, "gw_in_a", F32, tk=1024, tn=2048),
                      _grad_plain(h, dz_ff, "gw_in_ff", F32))
    (grad_x, dg_mix), (r_in,) = _in_bwd(dz_ret, dz_gt, dz_fox, dz_a, dz_ff, w_a, w_ff, xs, g_mix, dx2,
                                        push=scatter_job([g_in]))
    small_g = _pack_small(dict(g_mix=dg_mix, g_ffn=dg_ffn, g_ret_norm=dg_ret, g_fox_q=dg_q, g_fox_k=dg_k, b_forget=db_f))

    small_all = _gather_small(small_g)
    sums = [_sum_stack(g, r, "sum_" + n) for g, r, n in zip(
        [g_in] + g_mid + [g_gate, g_up, g_down], [r_in, r_ro, r_fo, r_out, r_gate, r_up, r_down], big_names)]
    sib = _sibling_exchange(sums)
    big_out = {n: _adamw(big_w[n], big_m[n], big_v[n], sums[i], sib[i], "adamw_" + n) for i, n in enumerate(big_names)}
    small_out = _adamw_small(*[[d[nm] for nm, _ in SMALL] for d in (small_w, small_m, small_v)], small_all)

    order = ("g_mix", "w_in", "b_forget", "g_ret_norm", "w_ret_o", "g_fox_q", "g_fox_k", "w_fox_o", "w_out", "g_ffn",
             "w_gate", "w_up", "w_down")
    outs = [loss, grad_x[None]]
    for idx in range(4):
        for n in order:
            if n in ("w_gate", "w_up"):
                outs.append(jnp.swapaxes(big_out[n][idx], 0, 1)[None])
            else:
                outs.append(big_out[n][idx][None] if n in big_out else small_out[idx][n])
    return tuple(outs)
```

```python
import functools
import math

import numpy as np
import jax
import jax.numpy as jnp
from jax import lax
from jax.experimental import pallas as pl
from jax.experimental.pallas import tpu as pltpu

F32 = jnp.float32
BF = jnp.bfloat16
MESH = pl.DeviceIdType.MESH

D_MODEL = 1024
D_FF = 2816
N_CHIP = 4
FF_SH = D_FF // N_CHIP
IN_COLS = 5128
IN_SH = IN_COLS // N_CHIP
RET_H, RET_DV = 4, 128
FOX_H, FOX_D = 8, 64
CHUNK = 256
EPS = 1e-6
NEG = -1e30
LANE = 128
C_RET, C_GT, C_FOX, C_A, C_END = 0, 1024, 1536, 3072, 5120
L_CQ, L_CK, L_LSE, L_MAX = 64, 67, 70, 73

ADAM_LR, ADAM_B1, ADAM_B2, ADAM_EPS, ADAM_WD, ADAM_STEP = 0.001, 0.9, 0.999, 1e-08, 0.01, 10
VMEM_BIG = 56 * 1024 * 1024
VMEM_HUGE = 60 * 1024 * 1024
GRAD_TK = 2048
FFN_TM = 512
FOX_SUB = 512


def _nn(a, b):
    return lax.dot_general(a, b, (((1,), (0,)), ((), ())), preferred_element_type=F32)


def _nt(a, b):
    return lax.dot_general(a, b, (((1,), (1,)), ((), ())), preferred_element_type=F32)


def _tn(a, b):
    return lax.dot_general(a, b, (((0,), (0,)), ((), ())), preferred_element_type=F32)


def _split3(x):
    hi = x.astype(BF)
    r = x - hi.astype(F32)
    mid = r.astype(BF)
    lo = (r - mid.astype(F32)).astype(BF)
    return hi, mid, lo


def _sigmoid(x):
    return 0.5 * jnp.tanh(0.5 * x) + 0.5


def _swap32(x):
    lane = lax.broadcasted_iota(jnp.int32, x.shape, 1)
    return jnp.where(lane < 32, pltpu.roll(x, 96, 1), pltpu.roll(x, 32, 1))


def _params(sem, vmem=None):
    return pltpu.CompilerParams(dimension_semantics=sem, vmem_limit_bytes=vmem)


def _row_tile(rows, cap, mult):
    return max(d for d in range(mult, cap + 1, mult) if rows % d == 0)


def _assemble_w_in(stack, tr=256):
    def body(s_ref, a_ref, f_ref):
        full = jnp.concatenate([s_ref[k].astype(F32) for k in range(N_CHIP)], axis=-1)
        a_ref[...] = jnp.concatenate([full[:, :3072], full[:, 3080:IN_COLS]], axis=-1).astype(BF)
        f_ref[...] = jnp.concatenate([full[:, 3072:3080], jnp.zeros((tr, LANE - FOX_H), F32)], axis=-1).astype(BF)

    return pl.pallas_call(
        body, name="assemble_w_in", grid=(D_MODEL // tr,),
        in_specs=[pl.BlockSpec((N_CHIP, tr, IN_SH), lambda i: (0, i, 0))],
        out_specs=[pl.BlockSpec((tr, C_END), lambda i: (i, 0)), pl.BlockSpec((tr, LANE), lambda i: (i, 0))],
        out_shape=[jax.ShapeDtypeStruct((D_MODEL, C_END), BF), jax.ShapeDtypeStruct((D_MODEL, LANE), BF)],
        compiler_params=_params(("parallel",), VMEM_BIG),
    )(stack)


def _pack_g_in(g_ret, g_gt, g_fox, g_a, g_ff, tr=256):
    def body(r_ref, t_ref, x_ref, a_ref, f_ref, o_ref):
        full = jnp.concatenate([r_ref[...], t_ref[...], x_ref[...], f_ref[...][:, :FOX_H], a_ref[...]], axis=-1)
        for k in range(N_CHIP):
            o_ref[k] = full[:, k * IN_SH:(k + 1) * IN_SH].astype(BF)

    def spec(w):
        return pl.BlockSpec((tr, w), lambda i: (i, 0))

    return pl.pallas_call(
        body, name="pack_g_in", grid=(D_MODEL // tr,),
        in_specs=[spec(1024), spec(512), spec(1536), spec(2048), spec(LANE)],
        out_specs=pl.BlockSpec((N_CHIP, tr, IN_SH), lambda i: (0, i, 0)),
        out_shape=jax.ShapeDtypeStruct((N_CHIP, D_MODEL, IN_SH), BF),
        compiler_params=_params(("parallel",), VMEM_BIG),
    )(g_ret, g_gt, g_fox, g_a, g_ff)


def _rms_cast(x, g, tm=512):
    T = x.shape[0]

    def body(x_ref, g_ref, o_ref):
        xv = x_ref[...]
        r = lax.rsqrt(jnp.mean(xv * xv, axis=-1, keepdims=True) + EPS)
        o_ref[...] = (xv * r * g_ref[...]).astype(BF)

    return pl.pallas_call(
        body, name="rms_cast", grid=(T // tm,),
        in_specs=[pl.BlockSpec((tm, D_MODEL), lambda i: (i, 0)), pl.BlockSpec((1, D_MODEL), lambda i: (0, 0))],
        out_specs=pl.BlockSpec((tm, D_MODEL), lambda i: (i, 0)),
        out_shape=jax.ShapeDtypeStruct((T, D_MODEL), BF),
        compiler_params=_params(("parallel",)),
    )(x, g)


def _hosted_call(body, name, grid, in_specs, out_specs, out_shape, scratch_shapes, vmem, args, push):
    sem = ("arbitrary",) * len(grid)
    if push is None:
        res = pl.pallas_call(body, name=name, grid=grid, in_specs=in_specs, out_specs=out_specs, out_shape=out_shape,
                             scratch_shapes=scratch_shapes, compiler_params=_params(sem, vmem))(*args)
        return list(res), []
    srcs, lands = push
    ns, nl, n_in, n_out = (0 if srcs is None else len(srcs)), len(lands), len(in_specs), len(out_specs)
    n_scr = len(scratch_shapes)

    def wrapped(*refs):
        pos = n_in + ns + nl
        ins, x_in = refs[:n_in], refs[n_in:pos]
        outs, x_out = refs[pos:pos + n_out], refs[pos + n_out:pos + n_out + nl]
        scr = refs[pos + n_out + nl:pos + n_out + nl + n_scr]
        ssem, rsem = refs[-2], refs[-1]
        src = None if srcs is None else x_in[:ns]
        ids = [pl.program_id(a) for a in range(len(grid))]
        first = functools.reduce(lambda p, q: p & q, [ids[a] == 0 for a in range(len(grid))])
        last = functools.reduce(lambda p, q: p & q, [ids[a] == grid[a] - 1 for a in range(len(grid))])

        @pl.when(first)
        def _():
            for cp in _push_copies(src, x_out, ssem, rsem, False):
                cp.start()

        body(*ins, *outs, *scr)

        @pl.when(last)
        def _():
            for cp in _push_copies(src, x_out, ssem, rsem, True):
                cp.wait_recv()
                cp.wait_send()

    anyspec = pl.BlockSpec(memory_space=pl.ANY)
    extra = ([] if srcs is None else list(srcs)) + list(lands)
    res = pl.pallas_call(
        wrapped, name=name, grid=grid,
        in_specs=list(in_specs) + [anyspec] * len(extra), out_specs=list(out_specs) + [anyspec] * nl,
        out_shape=list(out_shape) + [jax.ShapeDtypeStruct(a.shape, a.dtype) for a in lands],
        input_output_aliases={n_in + ns + i: n_out + i for i in range(nl)},
        scratch_shapes=list(scratch_shapes) + [pltpu.SemaphoreType.DMA((3 * nl,)), pltpu.SemaphoreType.DMA((3 * nl,))],
        compiler_params=_params(sem, vmem),
    )(*args, *extra)
    return list(res[:n_out]), list(res[n_out:])


def _mm_nn(a, b, name, out_dtype, tm=512, tn=1024, push=None):
    M, K = a.shape
    N = b.shape[1]
    tn = min(tn, N)

    def body(a_ref, b_ref, o_ref):
        o_ref[...] = _nn(a_ref[...], b_ref[...]).astype(o_ref.dtype)

    (out,), lands = _hosted_call(
        body, name, (N // tn, M // tm),
        [pl.BlockSpec((tm, K), lambda j, i: (i, 0)), pl.BlockSpec((K, tn), lambda j, i: (0, j))],
        [pl.BlockSpec((tm, tn), lambda j, i: (i, j))], [jax.ShapeDtypeStruct((M, N), out_dtype)], [], None, (a, b), push)
    return out, lands


def _mm_tn(a, b, name, grid, a_spec, b_spec, o_spec, out_shape, acc_shape):
    nk = grid[-1]

    def body(a_ref, b_ref, o_ref, acc):
        k = pl.program_id(len(grid) - 1)

        @pl.when(k == 0)
        def _():
            acc[...] = jnp.zeros(acc.shape, F32)

        acc[...] += _tn(a_ref[...].astype(BF), b_ref[...].astype(BF))

        @pl.when(k == nk - 1)
        def _():
            o_ref[...] = acc[...].astype(o_ref.dtype)

    return pl.pallas_call(
        body, name=name, grid=grid, in_specs=[a_spec, b_spec], out_specs=o_spec, out_shape=out_shape,
        scratch_shapes=[pltpu.VMEM(acc_shape, F32)],
        compiler_params=_params(("parallel",) * (len(grid) - 1) + ("arbitrary",), VMEM_BIG),
    )(a, b)


def _grad_plain(a, b, name, out_dtype, tk=GRAD_TK, tn=1024):
    T, M = a.shape
    N = b.shape[1]
    tn = min(tn, N)
    return _mm_tn(a, b, name, (N // tn, T // tk),
                  pl.BlockSpec((tk, M), lambda j, k: (k, 0)), pl.BlockSpec((tk, tn), lambda j, k: (k, j)),
                  pl.BlockSpec((M, tn), lambda j, k: (0, j)), jax.ShapeDtypeStruct((M, N), out_dtype), (M, tn))


def _grad_colstack(a, b, name, wcol, tk=GRAD_TK):
    T, M = a.shape
    N = b.shape[1]
    S = N // wcol
    nk = T // tk

    def body(a_ref, b_ref, o_ref, acc):
        k = pl.program_id(0)

        @pl.when(k == 0)
        def _():
            acc[...] = jnp.zeros(acc.shape, F32)

        acc[...] += _tn(a_ref[...], b_ref[...])

        @pl.when(k == nk - 1)
        def _():
            for s in range(S):
                o_ref[s] = acc[:, s * wcol:(s + 1) * wcol].astype(BF)

    return pl.pallas_call(
        body, name=name, grid=(nk,),
        in_specs=[pl.BlockSpec((tk, M), lambda k: (k, 0)), pl.BlockSpec((tk, N), lambda k: (k, 0))],
        out_specs=pl.BlockSpec((S, M, wcol), lambda k: (0, 0, 0)), out_shape=jax.ShapeDtypeStruct((S, M, wcol), BF),
        scratch_shapes=[pltpu.VMEM((M, N), F32)], compiler_params=_params(("arbitrary",), VMEM_BIG),
    )(a, b)


def _grad_astack(a, b, name, tk=1024, push=None):
    S, T, m = a.shape
    N = b.shape[1]
    nk = T // tk

    def body(a_ref, b_ref, o_ref, acc):
        k = pl.program_id(0)

        @pl.when(k == 0)
        def _():
            acc[...] = jnp.zeros(acc.shape, F32)

        bb = b_ref[...].astype(BF)
        for s in range(S):
            acc[s] += _tn(a_ref[s], bb)

        @pl.when(k == nk - 1)
        def _():
            o_ref[...] = acc[...].astype(BF)

    (out,), lands = _hosted_call(
        body, name, (nk,),
        [pl.BlockSpec((S, tk, m), lambda k: (0, k, 0)), pl.BlockSpec((tk, N), lambda k: (k, 0))],
        [pl.BlockSpec((S, m, N), lambda k: (0, 0, 0))], [jax.ShapeDtypeStruct((S, m, N), BF)],
        [pltpu.VMEM((S, m, N), F32)], VMEM_BIG, (a, b), push)
    return out, lands


def _rope_tables(T):
    half = 32
    pos = np.arange(T, dtype=np.float32)
    inv_freq = (np.float32(1.0) / (np.float32(10000.0) ** (np.arange(half, dtype=np.float32) / np.float32(half)))).astype(np.float32)
    ang = (pos[:, None] * inv_freq[None, :]).astype(np.float32)
    cos, sin = np.cos(ang).astype(np.float32), np.sin(ang).astype(np.float32)
    z = np.zeros((T, 64), np.float32)
    return (jnp.asarray(np.concatenate([cos, cos, z], axis=-1)), jnp.asarray(np.concatenate([-sin, sin, z], axis=-1)))


def _ret_consts():
    h = np.arange(RET_H, dtype=np.float32)
    log_g = np.log1p(-(np.float32(2.0) ** (-5.0 - h))).astype(np.float32)
    idx = np.arange(CHUNK, dtype=np.float32)
    diff = idx[:, None] - idx[None, :]
    decay = np.where(diff[None] >= 0, np.exp(np.maximum(diff, 0.0)[None] * log_g[:, None, None]), 0.0)
    zeta = np.exp((CHUNK - 1.0 - idx)[None, :] * log_g[:, None])
    xi = np.exp((idx + 1.0)[None, :] * log_g[:, None])
    gc = np.exp(CHUNK * log_g)
    bc = lambda v: np.broadcast_to(v[:, :, None], (RET_H, CHUNK, LANE)).astype(np.float32)
    gcb = np.broadcast_to(gc[:, None, None], (RET_H, LANE, LANE)).astype(np.float32)
    return (jnp.asarray(decay.astype(np.float32)), jnp.asarray(bc(zeta)), jnp.asarray(bc(xi)), jnp.asarray(gcb))


def _mix_prep(z_a, z_ff, cos_t, sin_t, b_f, g_q, g_k, tm=256, push=None):
    T = z_a.shape[0]

    def body(zqk_ref, zf_ref, zff_ref, cos_ref, sin_ref, b_ref, g_ref, seg_ref, segt_ref,
             qr_ref, kr_ref, qf_ref, kf_ref, vf_ref, c_ref, nmax_ref, carry):
        i = pl.program_id(0)

        @pl.when(i == 0)
        def _():
            carry[...] = jnp.zeros(carry.shape, F32)
            nmax_ref[...] = jnp.zeros(nmax_ref.shape, F32)

        lane = lax.broadcasted_iota(jnp.int32, (tm, LANE), 1)
        zpad = jnp.zeros((tm, 64), F32)
        cosv, sinv = cos_ref[...], sin_ref[...]
        zqk = zqk_ref[...].astype(F32)
        for h in range(RET_H):
            for src, dst, scale in ((0, qr_ref, 1.0), (256, kr_ref, 0.125)):
                xh = jnp.concatenate([zqk[:, src + 64 * h: src + 64 * h + 64], zpad], axis=-1)
                rot = xh * cosv + _swap32(xh) * sinv
                dst[h] = (rot * scale).astype(BF)

        lf_in = zff_ref[...] + b_ref[...]
        logf = jnp.minimum(lf_in, 0.0) - jnp.log(1.0 + jnp.exp(-jnp.abs(lf_in)))
        row = lax.broadcasted_iota(jnp.int32, (tm, tm), 0)
        col = lax.broadcasted_iota(jnp.int32, (tm, tm), 1)
        tri = (row >= col).astype(BF)
        hi, mid, lo = _split3(logf)
        cs = _nn(tri, hi) + _nn(tri, mid) + _nn(tri, lo) + carry[...]
        carry[...] = cs[tm - 1:tm, :]
        c_ref[...] = cs

        def seg_sum(v):
            return sum(_nn(t, seg_ref[...]) for t in _split3(v))

        zf = zf_ref[...].astype(F32)
        xqk = zf[:, :1024]
        rinv = lax.rsqrt(seg_sum(xqk * xqk) * (1.0 / FOX_D) + EPS)
        xn = xqk * sum(_nn(t, segt_ref[...]) for t in _split3(rinv)) * g_ref[...]
        nmax_ref[...] = jnp.maximum(nmax_ref[...], jnp.max(seg_sum(xn * xn), axis=0, keepdims=True))

        one = jnp.ones((tm, LANE), F32)
        for h in range(FOX_H):
            c = cs[:, h:h + 1]
            chi, cmid, clo = [t.astype(F32) for t in _split3(c)]
            qn = xn[:, 64 * h:64 * h + 64]
            kn = xn[:, 512 + 64 * h:512 + 64 * h + 64]
            vh = zf[:, 1024 + 64 * h:1024 + 64 * h + 64]
            qa = jnp.concatenate([qn, zpad], axis=-1)
            qa = jnp.where(lane == L_CQ, chi, jnp.where(lane == L_CQ + 1, cmid, jnp.where(lane == L_CQ + 2, clo, qa)))
            qa = jnp.where((lane >= L_CK) & (lane < L_CK + 3), one, qa)
            ka = jnp.concatenate([kn, zpad], axis=-1)
            ka = jnp.where(lane == L_CK, -chi, jnp.where(lane == L_CK + 1, -cmid, jnp.where(lane == L_CK + 2, -clo, ka)))
            ka = jnp.where(((lane >= L_CQ) & (lane < L_CQ + 3)) | ((lane >= L_LSE) & (lane < L_MAX + 3)), one, ka)
            va = jnp.concatenate([vh, zpad], axis=-1)
            va = jnp.where((lane >= 64) & (lane < 67), one, va)
            qf_ref[h] = qa.astype(BF)
            kf_ref[h] = ka.astype(BF)
            vf_ref[h] = va.astype(BF)

    hspec4 = pl.BlockSpec((RET_H, tm, LANE), lambda i: (0, i, 0))
    hspec8 = pl.BlockSpec((FOX_H, tm, LANE), lambda i: (0, i, 0))
    const = lambda r, w: pl.BlockSpec((r, w), lambda i: (0, 0))
    seg = _segment_matrix()
    g_all = jnp.concatenate([jnp.tile(g_q * 0.125, (1, FOX_H)), jnp.tile(g_k, (1, FOX_H))], axis=1)
    return _hosted_call(
        body, "mix_prep", (T // tm,),
        [pl.BlockSpec((tm, 512), lambda i: (i, 0)), pl.BlockSpec((tm, 1536), lambda i: (i, 1)),
         pl.BlockSpec((tm, LANE), lambda i: (i, 0)), pl.BlockSpec((tm, LANE), lambda i: (i, 0)),
         pl.BlockSpec((tm, LANE), lambda i: (i, 0)), const(1, LANE), const(1, 1024), const(1024, LANE), const(LANE, 1024)],
        [hspec4, hspec4, hspec8, hspec8, hspec8, pl.BlockSpec((tm, LANE), lambda i: (i, 0)), const(1, LANE)],
        [jax.ShapeDtypeStruct((RET_H, T, LANE), BF)] * 2 + [jax.ShapeDtypeStruct((FOX_H, T, LANE), BF)] * 3
        + [jax.ShapeDtypeStruct((T, LANE), F32), jax.ShapeDtypeStruct((1, LANE), F32)],
        [pltpu.VMEM((1, LANE), F32)], VMEM_BIG, (z_a, z_a, z_ff, cos_t, sin_t, b_f, g_all, seg, seg.T), push)


def _segment_matrix():
    m = np.zeros((2 * FOX_H * FOX_D, LANE), np.float32)
    m[np.arange(2 * FOX_H * FOX_D), np.arange(2 * FOX_H * FOX_D) // FOX_D] = 1.0
    return jnp.asarray(m, dtype=BF)


def _ret_fwd(qr, kr, z_a, g_ret, consts, tt=512):
    T = z_a.shape[0]
    nch = tt // CHUNK
    decay, zeta, xi, gcb = consts

    def body(q_ref, k_ref, v_ref, gt_ref, g_ref, d_ref, ze_ref, xi_ref, gc_ref, o_ref, u_ref, st_ref, r_sc):
        i = pl.program_id(0)

        @pl.when(i == 0)
        def _():
            r_sc[...] = jnp.zeros(r_sc.shape, F32)

        for c in range(nch):
            rows = slice(c * CHUNK, (c + 1) * CHUNK)
            for h in range(RET_H):
                cols = slice(h * RET_DV, (h + 1) * RET_DV)
                q, k = q_ref[h, rows, :], k_ref[h, rows, :]
                v32 = v_ref[rows, cols].astype(F32)
                r = r_sc[h]
                st_ref[h, c * CHUNK:c * CHUNK + LANE, :] = r
                s = _nt(q, k) * d_ref[h]
                o = _nn(s.astype(BF), v32.astype(BF)) + _nn(q, r.astype(BF)) * xi_ref[h]
                r_sc[h] = gc_ref[h] * r + _tn(k, (v32 * ze_ref[h]).astype(BF))
                o_ref[rows, cols] = o
                mu = jnp.mean(o, axis=-1, keepdims=True)
                xc = o - mu
                on = xc * lax.rsqrt(jnp.mean(xc * xc, axis=-1, keepdims=True) + EPS)
                gt = gt_ref[rows, cols].astype(F32)
                u_ref[rows, cols] = (gt * _sigmoid(gt) * (on * g_ref[:, cols])).astype(BF)

    hspec = pl.BlockSpec((RET_H, tt, LANE), lambda i: (0, i, 0))
    cspec = pl.BlockSpec((RET_H, CHUNK, LANE), lambda i: (0, 0, 0))
    dspec = pl.BlockSpec((RET_H, CHUNK, CHUNK), lambda i: (0, 0, 0))
    sspec = pl.BlockSpec((RET_H, LANE, LANE), lambda i: (0, 0, 0))
    return pl.pallas_call(
        body, name="ret_fwd", grid=(T // tt,),
        in_specs=[hspec, hspec, pl.BlockSpec((tt, 512), lambda i: (i, 1)), pl.BlockSpec((tt, 512), lambda i: (i, 2)),
                  pl.BlockSpec((1, 512), lambda i: (0, 0)), dspec, cspec, cspec, sspec],
        out_specs=[pl.BlockSpec((tt, 512), lambda i: (i, 0)), pl.BlockSpec((tt, 512), lambda i: (i, 0)), hspec],
        out_shape=[jax.ShapeDtypeStruct((T, 512), F32), jax.ShapeDtypeStruct((T, 512), BF),
                   jax.ShapeDtypeStruct((RET_H, T, LANE), F32)],
        scratch_shapes=[pltpu.VMEM((RET_H, LANE, LANE), F32)],
        compiler_params=_params(("arbitrary",), VMEM_BIG),
    )(qr, kr, z_a, z_a, g_ret, decay, zeta, xi, gcb)


PRUNE_LOG = -110.0
TAME_LOGIT_SPAN = 60.0


def _prune_tables(c, nmax, sub):
    n = c.shape[0] // sub
    u = jnp.sqrt(nmax[0, :FOX_H] * nmax[0, FOX_H:2 * FOX_H]) * 1.02 + 0.5
    first = c[0::sub, :FOX_H].T
    last = c[sub - 1::sub, :FOX_H].T
    blk = jnp.arange(n, dtype=jnp.int32)
    needed = (2.0 * u[:, None, None] + first[:, :, None] - last[:, None, :] >= PRUNE_LOG) | (blk[None, :] >= blk[:, None])[None]
    jlo = jnp.argmax(needed, axis=2).astype(jnp.int32)

    def end_of(key_block):
        reach = jlo[:, None, :] <= key_block[None, :, None]
        return (n - jnp.argmax(reach[:, :, ::-1], axis=2)).astype(jnp.int32)

    sup = 2 * jnp.arange(n // 2, dtype=jnp.int32)
    end_last = end_of(sup + 1)
    end_both = jnp.clip(end_of(sup), sup[None, :] + 2, end_last)
    tame = (2.0 * u < TAME_LOGIT_SPAN).astype(jnp.int32)
    return jlo, end_both, end_last, tame


def _fox_fwd(jlo, tame, q, k, v, sub=FOX_SUB):
    H, T, _ = q.shape
    tb = 2 * sub

    def body(js_ref, tame_ref, q_ref, k_ref, v_ref, o_ref, q2_ref, mx_sc, acc_sc):
        i = pl.program_id(1)
        hd = pl.program_id(0)
        starts = [jnp.minimum(js_ref[hd, 2 * i], 2 * i), jnp.minimum(js_ref[hd, 2 * i + 1], 2 * i)]
        lane = lax.broadcasted_iota(jnp.int32, (sub, LANE), 1)
        row = lax.broadcasted_iota(jnp.int32, (sub, sub), 0)
        col = lax.broadcasted_iota(jnp.int32, (sub, sub), 1)
        causal = row >= col
        qs = [q_ref[0:sub, :], q_ref[sub:tb, :]]
        d0 = pl.multiple_of(i * tb, tb)
        d1 = pl.multiple_of(i * tb + sub, sub)
        k0, k1 = k_ref[pl.ds(d0, sub), :], k_ref[pl.ds(d1, sub), :]
        v0, v1 = v_ref[pl.ds(d0, sub), :], v_ref[pl.ds(d1, sub), :]

        def lane_max(s):
            m = s[:, 0:LANE]
            for c in range(1, s.shape[1] // LANE):
                m = jnp.maximum(m, s[:, c * LANE:(c + 1) * LANE])
            return m

        def put3(base, first, val):
            hi, mid, lo = _split3(val)
            return jnp.where(lane == first, hi, jnp.where(lane == first + 1, mid, jnp.where(lane == first + 2, lo, base)))

        def row_max():
            mx_sc[...] = jnp.full(mx_sc.shape, NEG, F32)
            for a in range(2):
                def max_body(j, carry, a=a):
                    kb = k_ref[pl.ds(pl.multiple_of(j * sub, sub), sub), :]
                    mx_sc[a] = jnp.maximum(mx_sc[a], lane_max(_nt(qs[a], kb)))
                    return carry

                lax.fori_loop(starts[a], 2 * i, max_body, 0)
            mx = [jnp.maximum(mx_sc[0], lane_max(jnp.where(causal, _nt(qs[0], k0), NEG))),
                  jnp.maximum(jnp.maximum(mx_sc[1], lane_max(_nt(qs[1], k0))),
                              lane_max(jnp.where(causal, _nt(qs[1], k1), NEG)))]
            return [jnp.max(t, axis=1, keepdims=True) for t in mx]

        def diag_logit():
            return [jnp.sum(qs[a].astype(F32) * kd.astype(F32), axis=1, keepdims=True) for a, kd in enumerate((k0, k1))]

        def finish(ms):
            qm = [put3(qs[a], L_MAX, -ms[a]) for a in range(2)]
            acc_sc[...] = jnp.zeros(acc_sc.shape, F32)
            for a in range(2):
                def acc_body(j, carry, a=a):
                    off = pl.multiple_of(j * sub, sub)
                    acc_sc[a] += _nn(jnp.exp(_nt(qm[a], k_ref[pl.ds(off, sub), :])).astype(BF), v_ref[pl.ds(off, sub), :])
                    return carry

                lax.fori_loop(starts[a], 2 * i, acc_body, 0)

            def pv(qa, kk, vv, masked):
                p = jnp.exp(_nt(qa, kk))
                if masked:
                    p = jnp.where(causal, p, 0.0)
                return _nn(p.astype(BF), vv)

            accs = [acc_sc[0] + pv(qm[0], k0, v0, True),
                    acc_sc[1] + pv(qm[1], k0, v0, False) + pv(qm[1], k1, v1, True)]
            for a in range(2):
                rows = slice(a * sub, (a + 1) * sub)
                l = accs[a][:, 64:65]
                o_ref[rows, :] = jnp.where(lane < 64, accs[a] / l, 0.0)
                q2_ref[rows, :] = put3(qs[a], L_LSE, -(ms[a] + jnp.log(l)))

        tame = tame_ref[hd] == 1

        @pl.when(tame)
        def _():
            finish(diag_logit())

        @pl.when(jnp.logical_not(tame))
        def _():
            finish(row_max())

    blk = pl.BlockSpec((None, tb, LANE), lambda h, i, js, tm_: (h, i, 0))
    full = pl.BlockSpec((None, T, LANE), lambda h, i, js, tm_: (h, 0, 0))
    return pl.pallas_call(
        body, name="fox_fwd",
        grid_spec=pltpu.PrefetchScalarGridSpec(
            num_scalar_prefetch=2, grid=(H, T // tb), in_specs=[blk, full, full], out_specs=[blk, blk],
            scratch_shapes=[pltpu.VMEM((2, sub, LANE), F32), pltpu.VMEM((2, sub, LANE), F32)]),
        out_shape=[jax.ShapeDtypeStruct((H, T, LANE), F32), jax.ShapeDtypeStruct((H, T, LANE), BF)],
        compiler_params=_params(("parallel", "arbitrary"), VMEM_BIG),
    )(jlo, tame, q, k, v)


def _merge_out(u_r, o_fox, z_a, x, g_ffn, w_ro, w_fo, w_out, tm=256, push=None):
    T = x.shape[0]

    def body(u_ref, of_ref, ar_ref, af_ref, x_ref, g_ref, wro_ref, wfo_ref, wout_ref,
             yr_ref, yf_ref, m_ref, x2_ref, h2_ref, oc_ref):
        u = u_ref[...]
        oc = jnp.concatenate([of_ref[h][:, :FOX_D] for h in range(FOX_H)], axis=-1).astype(BF)
        oc_ref[...] = oc
        yr = jnp.concatenate([_nn(u, wro_ref[k]) for k in range(N_CHIP)], axis=-1)
        yf = jnp.concatenate([_nn(oc, wfo_ref[k]) for k in range(N_CHIP)], axis=-1)
        yr_ref[...] = yr.astype(BF)
        yf_ref[...] = yf.astype(BF)
        m = (_sigmoid(ar_ref[...].astype(F32)) * yr + _sigmoid(af_ref[...].astype(F32)) * yf).astype(BF)
        m_ref[...] = m
        x2 = x_ref[...]
        for k in range(N_CHIP):
            x2 = x2 + _nn(m[:, 256 * k:256 * k + 256], wout_ref[k])
        x2_ref[...] = x2
        r = lax.rsqrt(jnp.mean(x2 * x2, axis=-1, keepdims=True) + EPS)
        h2_ref[...] = (x2 * r * g_ref[...]).astype(BF)

    row = lambda w: pl.BlockSpec((tm, w), lambda i: (i, 0))
    const = lambda shp: pl.BlockSpec(shp, lambda i: (0,) * len(shp))
    return _hosted_call(
        body, "merge_out", (T // tm,),
        [row(512), pl.BlockSpec((FOX_H, tm, LANE), lambda i: (0, i, 0)),
         pl.BlockSpec((tm, 1024), lambda i: (i, 3)), pl.BlockSpec((tm, 1024), lambda i: (i, 4)),
         row(1024), const((1, 1024)), const((N_CHIP, 512, 256)), const((N_CHIP, 512, 256)),
         const((N_CHIP, 256, 1024))],
        [row(1024), row(1024), row(1024), row(1024), row(1024), row(512)],
        [jax.ShapeDtypeStruct((T, 1024), BF), jax.ShapeDtypeStruct((T, 1024), BF),
         jax.ShapeDtypeStruct((T, 1024), BF), jax.ShapeDtypeStruct((T, 1024), F32),
         jax.ShapeDtypeStruct((T, 1024), BF), jax.ShapeDtypeStruct((T, 512), BF)],
        [], VMEM_BIG, (u_r, o_fox, z_a, z_a, x, g_ffn, w_ro, w_fo, w_out), push)


def _load_resident(hbm_refs, vmem_refs, sem):
    cps = [pltpu.make_async_copy(h, v, sem.at[i]) for i, (h, v) in enumerate(zip(hbm_refs, vmem_refs))]
    for cp in cps:
        cp.start()
    for cp in cps:
        cp.wait()


def _ffn_fwd(h2, x2, tgt, w_gate, w_up, w_down, tm=FFN_TM):
    T = h2.shape[0]

    def body(h_ref, x2_ref, t_ref, wg_hbm, wu_hbm, wd_hbm, a_ref, b_ref, act_ref, dy_ref, ls_ref, wg, wu, wd, sem):
        @pl.when(pl.program_id(0) == 0)
        def _():
            _load_resident((wg_hbm, wu_hbm, wd_hbm), (wg, wu, wd), sem)
            ls_ref[...] = jnp.zeros(ls_ref.shape, F32)

        h = h_ref[...]
        err = x2_ref[...] - t_ref[...]
        for k in range(N_CHIP):
            gp = _nt(h, wg[k])
            up = _nt(h, wu[k])
            sg = _sigmoid(gp)
            silu = gp * sg
            a_ref[k] = silu.astype(BF)
            b_ref[k] = (up * sg * (1.0 + gp * (1.0 - sg))).astype(BF)
            act = (silu * up).astype(BF)
            act_ref[k] = act
            err = err + _nn(act, wd[k])
        dy_ref[...] = err * (1.0 / D_MODEL)
        ls_ref[...] += jnp.sum(err * err, axis=0, keepdims=True)

    row = pl.BlockSpec((tm, D_MODEL), lambda i: (i, 0))
    hid = pl.BlockSpec((N_CHIP, tm, FF_SH), lambda i: (0, i, 0))
    anyspec = pl.BlockSpec(memory_space=pl.ANY)
    wshape = pltpu.VMEM((N_CHIP, FF_SH, D_MODEL), BF)
    return pl.pallas_call(
        body, name="ffn_fwd", grid=(T // tm,),
        in_specs=[row, row, row, anyspec, anyspec, anyspec],
        out_specs=[hid, hid, hid, row, pl.BlockSpec((1, D_MODEL), lambda i: (0, 0))],
        out_shape=[jax.ShapeDtypeStruct((N_CHIP, T, FF_SH), BF)] * 3
        + [jax.ShapeDtypeStruct((T, D_MODEL), F32), jax.ShapeDtypeStruct((1, D_MODEL), F32)],
        scratch_shapes=[wshape, wshape, wshape, pltpu.SemaphoreType.DMA((3,))],
        compiler_params=_params(("arbitrary",), VMEM_HUGE),
    )(h2, x2, tgt, w_gate, w_up, w_down)


def _ffn_bwd(dy, sa, sb, x2, g_ffn, w_gate, w_up, w_down, tm=FFN_TM):
    T = dy.shape[0]

    def body(dy_ref, a_ref, b_ref, x2_ref, g_ref, wg_hbm, wu_hbm, wd_hbm, dgp_ref, dup_ref, dx_ref, dg_ref,
             wg, wu, wd, sem):
        @pl.when(pl.program_id(0) == 0)
        def _():
            _load_resident((wg_hbm, wu_hbm, wd_hbm), (wg, wu, wd), sem)
            dg_ref[...] = jnp.zeros(dg_ref.shape, F32)

        dy = dy_ref[...]
        dyb = dy.astype(BF)
        dh = jnp.zeros((tm, D_MODEL), F32)
        for k in range(N_CHIP):
            dact = _nt(dyb, wd[k])
            dup = (dact * a_ref[k]).astype(BF)
            dgp = (dact * b_ref[k]).astype(BF)
            dgp_ref[k] = dgp
            dup_ref[k] = dup
            dh = dh + _nn(dgp, wg[k]) + _nn(dup, wu[k])
        x2 = x2_ref[...]
        r = lax.rsqrt(jnp.mean(x2 * x2, axis=-1, keepdims=True) + EPS)
        xn = x2 * r
        dg_ref[...] += jnp.sum(dh * xn, axis=0, keepdims=True)
        dxn = dh * g_ref[...]
        dx_ref[...] = dy + r * (dxn - xn * jnp.mean(dxn * xn, axis=-1, keepdims=True))

    row = pl.BlockSpec((tm, D_MODEL), lambda i: (i, 0))
    hid = pl.BlockSpec((N_CHIP, tm, FF_SH), lambda i: (0, i, 0))
    vec = pl.BlockSpec((1, D_MODEL), lambda i: (0, 0))
    anyspec = pl.BlockSpec(memory_space=pl.ANY)
    wshape = pltpu.VMEM((N_CHIP, FF_SH, D_MODEL), BF)
    return pl.pallas_call(
        body, name="ffn_bwd", grid=(T // tm,),
        in_specs=[row, hid, hid, row, vec, anyspec, anyspec, anyspec],
        out_specs=[hid, hid, row, vec],
        out_shape=[jax.ShapeDtypeStruct((N_CHIP, T, FF_SH), BF), jax.ShapeDtypeStruct((N_CHIP, T, FF_SH), BF),
                   jax.ShapeDtypeStruct((T, D_MODEL), F32), jax.ShapeDtypeStruct((1, D_MODEL), F32)],
        scratch_shapes=[wshape, wshape, wshape, pltpu.SemaphoreType.DMA((3,))],
        compiler_params=_params(("arbitrary",), VMEM_HUGE),
    )(dy, sa, sb, x2, g_ffn, w_gate, w_up, w_down)


def _out_bwd(dx2, z_a, y_r, y_f, o_raw, o_fox, g_ret, w_ro, w_fo, w_out, tm=256, push=None):
    T = dx2.shape[0]

    def body(dx_ref, gt_ref, ar_ref, af_ref, yr_ref, yf_ref, o_ref, of_ref, g_ref, wro_ref, wfo_ref, wout_ref,
             dyr_ref, dyf_ref, dgt_ref, da_ref, do_ref, dof_ref, dg_ref):
        i = pl.program_id(0)

        @pl.when(i == 0)
        def _():
            dg_ref[...] = jnp.zeros(dg_ref.shape, F32)

        dxb = dx_ref[...].astype(BF)
        dm = jnp.concatenate([_nt(dxb, wout_ref[k]) for k in range(N_CHIP)], axis=-1)
        sr, sf = _sigmoid(ar_ref[...].astype(F32)), _sigmoid(af_ref[...].astype(F32))
        dyr = dm * sr
        dyf = dm * sf
        da_ref[:, :1024] = (dyr * yr_ref[...].astype(F32) * (1.0 - sr)).astype(BF)
        da_ref[:, 1024:] = (dyf * yf_ref[...].astype(F32) * (1.0 - sf)).astype(BF)
        dyr = dyr.astype(BF)
        dyf = dyf.astype(BF)
        dyr_ref[...] = dyr
        dyf_ref[...] = dyf
        du = jnp.zeros((tm, 512), F32)
        doc = jnp.zeros((tm, 512), F32)
        for k in range(N_CHIP):
            du = du + _nt(dyr[:, 256 * k:256 * k + 256], wro_ref[k])
            doc = doc + _nt(dyf[:, 256 * k:256 * k + 256], wfo_ref[k])

        for h in range(RET_H):
            cols = slice(h * RET_DV, (h + 1) * RET_DV)
            o = o_ref[:, cols]
            mu = jnp.mean(o, axis=-1, keepdims=True)
            xc = o - mu
            rstd = lax.rsqrt(jnp.mean(xc * xc, axis=-1, keepdims=True) + EPS)
            on = xc * rstd
            g = g_ref[:, cols]
            gt = gt_ref[:, cols].astype(F32)
            sg = _sigmoid(gt)
            duh = du[:, cols]
            dgt_ref[:, cols] = (duh * (on * g) * sg * (1.0 + gt * (1.0 - sg))).astype(BF)
            dog = duh * gt * sg
            dg_ref[:, cols] += jnp.sum(dog * on, axis=0, keepdims=True)
            don = dog * g
            do_ref[:, cols] = rstd * (don - jnp.mean(don, axis=-1, keepdims=True)
                                      - on * jnp.mean(don * on, axis=-1, keepdims=True))

        lane = lax.broadcasted_iota(jnp.int32, (tm, LANE), 1)
        zpad = jnp.zeros((tm, 64), F32)
        for h in range(FOX_H):
            doh = doc[:, 64 * h:64 * h + 64]
            delta = jnp.sum(doh * of_ref[h][:, :FOX_D], axis=-1, keepdims=True)
            hi, mid, lo = [t.astype(F32) for t in _split3(-delta)]
            da = jnp.concatenate([doh, zpad], axis=-1)
            da = jnp.where(lane == 64, hi, jnp.where(lane == 65, mid, jnp.where(lane == 66, lo, da)))
            dof_ref[h] = da.astype(BF)

    row = lambda w: pl.BlockSpec((tm, w), lambda i: (i, 0))
    const = lambda shp: pl.BlockSpec(shp, lambda i: (0,) * len(shp))
    hsp = pl.BlockSpec((FOX_H, tm, LANE), lambda i: (0, i, 0))
    return _hosted_call(
        body, "out_bwd", (T // tm,),
        [row(1024), pl.BlockSpec((tm, 512), lambda i: (i, 2)), pl.BlockSpec((tm, 1024), lambda i: (i, 3)),
         pl.BlockSpec((tm, 1024), lambda i: (i, 4)), row(1024), row(1024), row(512), hsp,
         const((1, 512)), const((N_CHIP, 512, 256)), const((N_CHIP, 512, 256)), const((N_CHIP, 256, 1024))],
        [row(1024), row(1024), row(512), row(2048), row(512), hsp, const((1, 512))],
        [jax.ShapeDtypeStruct((T, 1024), BF), jax.ShapeDtypeStruct((T, 1024), BF),
         jax.ShapeDtypeStruct((T, 512), BF), jax.ShapeDtypeStruct((T, 2048), BF),
         jax.ShapeDtypeStruct((T, 512), F32), jax.ShapeDtypeStruct((FOX_H, T, LANE), BF),
         jax.ShapeDtypeStruct((1, 512), F32)],
        [], VMEM_BIG, (dx2, z_a, z_a, z_a, y_r, y_f, o_raw, o_fox, g_ret, w_ro, w_fo, w_out), push)


def _ret_bwd(d_o, qr, kr, z_a, states, cos_t, sin_t, consts, tt=512, push=None):
    T = z_a.shape[0]
    nt = T // tt
    nch = tt // CHUNK
    decay, zeta, xi, gcb = consts

    def body(do_ref, q_ref, k_ref, v_ref, st_ref, cos_ref, sin_ref, d_ref, ze_ref, xi_ref, gc_ref, dz_ref, g_sc):
        i = pl.program_id(0)

        @pl.when(i == 0)
        def _():
            g_sc[...] = jnp.zeros(g_sc.shape, F32)

        for c in reversed(range(nch)):
            rows = slice(c * CHUNK, (c + 1) * CHUNK)
            cosv, sinv = cos_ref[rows, :], sin_ref[rows, :]
            dq_parts, dk_parts = [], []
            for h in range(RET_H):
                cols = slice(h * RET_DV, (h + 1) * RET_DV)
                q, k = q_ref[h, rows, :], k_ref[h, rows, :]
                v32 = v_ref[rows, cols].astype(F32)
                vb = v32.astype(BF)
                r = st_ref[h, c * CHUNK:c * CHUNK + LANE, :]
                g = g_sc[h]
                gb = g.astype(BF)
                d_o = do_ref[rows, cols]
                dob = d_o.astype(BF)
                dox = (d_o * xi_ref[h]).astype(BF)
                dec = d_ref[h]
                s = (_nt(q, k) * dec).astype(BF)
                ds = (_nt(dob, vb) * dec).astype(BF)
                dv = _tn(s, dob) + ze_ref[h] * _nn(k, gb)
                dq = _nn(ds, k) + _nt(dox, r.astype(BF))
                dk = _tn(ds, q) + _nt((v32 * ze_ref[h]).astype(BF), gb)
                g_sc[h] = gc_ref[h] * g + _tn(q, dox)
                dq_parts.append((dq * cosv - _swap32(dq) * sinv)[:, :64])
                dk_parts.append(((dk * cosv - _swap32(dk) * sinv) * 0.125)[:, :64])
                dz_ref[rows, 512 + h * RET_DV:512 + (h + 1) * RET_DV] = dv.astype(BF)
            dz_ref[rows, 0:256] = jnp.concatenate(dq_parts, axis=-1).astype(BF)
            dz_ref[rows, 256:512] = jnp.concatenate(dk_parts, axis=-1).astype(BF)

    rev = lambda i: nt - 1 - i
    hspec = pl.BlockSpec((RET_H, tt, LANE), lambda i: (0, rev(i), 0))
    cspec = pl.BlockSpec((RET_H, CHUNK, LANE), lambda i: (0, 0, 0))
    tab = pl.BlockSpec((tt, LANE), lambda i: (rev(i), 0))
    (dz,), lands = _hosted_call(
        body, "ret_bwd", (nt,),
        [pl.BlockSpec((tt, 512), lambda i: (rev(i), 0)), hspec, hspec,
         pl.BlockSpec((tt, 512), lambda i: (rev(i), 1)), hspec, tab, tab,
         pl.BlockSpec((RET_H, CHUNK, CHUNK), lambda i: (0, 0, 0)), cspec, cspec,
         pl.BlockSpec((RET_H, LANE, LANE), lambda i: (0, 0, 0))],
        [pl.BlockSpec((tt, 1024), lambda i: (rev(i), 0))], [jax.ShapeDtypeStruct((T, 1024), BF)],
        [pltpu.VMEM((RET_H, LANE, LANE), F32)], VMEM_BIG,
        (d_o, qr, kr, z_a, states, cos_t, sin_t, decay, zeta, xi, gcb), push)
    return dz, lands


def _fox_bwd(end_both, end_last, q2, k, v, do, sub=FOX_SUB):
    H, T, _ = k.shape
    tb = 2 * sub

    def body(eb_ref, el_ref, q_ref, do_ref, k_ref, v_ref, dq_ref, dk_ref, dv_ref, dk_sc, dv_sc):
        j = pl.program_id(1)
        n_both = eb_ref[pl.program_id(0), j]
        n_last = el_ref[pl.program_id(0), j]

        @pl.when(j == 0)
        def _():
            dq_ref[...] = jnp.zeros(dq_ref.shape, F32)

        dk_sc[...] = jnp.zeros(dk_sc.shape, F32)
        dv_sc[...] = jnp.zeros(dv_sc.shape, F32)
        krow = lax.broadcasted_iota(jnp.int32, (tb, sub), 0)
        qcol = lax.broadcasted_iota(jnp.int32, (tb, sub), 1)

        def step(i, r0, r1, shift):
            off = pl.multiple_of(i * sub, sub)
            qq = q_ref[pl.ds(off, sub), :]
            dd = do_ref[pl.ds(off, sub), :]
            kk, vv = k_ref[r0:r1, :], v_ref[r0:r1, :]
            p = jnp.exp(_nt(kk, qq))
            if shift is not None:
                p = jnp.where(qcol[0:r1 - r0, :] + shift >= krow[0:r1 - r0, :], p, 0.0)
            ds = (p * _nt(vv, dd)).astype(BF)
            dv_sc[r0:r1, :] += _nn(p.astype(BF), dd)
            dk_sc[r0:r1, :] += _nn(ds, qq)
            dq_ref[pl.ds(off, sub), :] += _tn(ds, kk)

        step(2 * j, 0, sub, 0)
        step(2 * j + 1, 0, tb, sub)

        def both_body(i, carry):
            step(i, 0, tb, None)
            return carry

        def last_body(i, carry):
            step(i, sub, tb, None)
            return carry

        lax.fori_loop(2 * j + 2, n_both, both_body, 0)
        lax.fori_loop(n_both, n_last, last_body, 0)
        dk_ref[...] = dk_sc[...]
        dv_ref[...] = dv_sc[...]

    blk = pl.BlockSpec((None, tb, LANE), lambda h, j, eb, el: (h, j, 0))
    full = pl.BlockSpec((None, T, LANE), lambda h, j, eb, el: (h, 0, 0))
    shp = jax.ShapeDtypeStruct((H, T, LANE), F32)
    return pl.pallas_call(
        body, name="fox_bwd",
        grid_spec=pltpu.PrefetchScalarGridSpec(
            num_scalar_prefetch=2, grid=(H, T // tb), in_specs=[full, full, blk, blk], out_specs=[full, blk, blk],
            scratch_shapes=[pltpu.VMEM((tb, LANE), F32), pltpu.VMEM((tb, LANE), F32)]),
        out_shape=[shp, shp, shp],
        compiler_params=_params(("arbitrary", "arbitrary"), VMEM_BIG),
    )(end_both, end_last, q2, do, k, v)


def _fox_post_bwd(dq, dk, dv, z_a, z_ff, b_f, g_q, g_k, tm=256, push=None):
    T = z_a.shape[0]
    nt = T // tm

    def body(dq_ref, dk_ref, dv_ref, zf_ref, zff_ref, b_ref, g_ref, sc_ref, seg_ref, segt_ref,
             dz_ref, dff_ref, dg_ref, db_ref, carry):
        i = pl.program_id(0)

        @pl.when(i == 0)
        def _():
            carry[...] = jnp.zeros(carry.shape, F32)
            dg_ref[...] = jnp.zeros(dg_ref.shape, F32)
            db_ref[...] = jnp.zeros(db_ref.shape, F32)

        lane = lax.broadcasted_iota(jnp.int32, (tm, LANE), 1)
        dcm = jnp.zeros((tm, LANE), F32)
        for h in range(FOX_H):
            dcm = jnp.where(lane == h, dq_ref[h][:, L_CQ:L_CQ + 1] - dk_ref[h][:, L_CK:L_CK + 1], dcm)

        def seg_mean(v):
            return sum(_nn(t, seg_ref[...]) for t in _split3(v)) * (1.0 / FOX_D)

        def seg_bcast(v):
            return sum(_nn(t, segt_ref[...]) for t in _split3(v))

        x = zf_ref[:, :1024].astype(F32)
        dy = jnp.concatenate([dq_ref[h][:, :FOX_D] for h in range(FOX_H)]
                             + [dk_ref[h][:, :FOX_D] for h in range(FOX_H)], axis=-1) * sc_ref[...]
        rb = seg_bcast(lax.rsqrt(seg_mean(x * x) + EPS))
        xn = x * rb
        dg_ref[...] += jnp.sum(dy * xn, axis=0, keepdims=True)
        dxn = dy * g_ref[...]
        dz_ref[:, :1024] = (rb * (dxn - xn * seg_bcast(seg_mean(dxn * xn)))).astype(BF)
        dz_ref[:, 1024:] = jnp.concatenate([dv_ref[h][:, :FOX_D] for h in range(FOX_H)], axis=-1).astype(BF)

        row = lax.broadcasted_iota(jnp.int32, (tm, tm), 0)
        col = lax.broadcasted_iota(jnp.int32, (tm, tm), 1)
        tri = (row <= col).astype(BF)
        hi, mid, lo = _split3(dcm)
        dlogf = _nn(tri, hi) + _nn(tri, mid) + _nn(tri, lo) + carry[...]
        carry[...] = dlogf[0:1, :]
        dff = jnp.where(lane < FOX_H, dlogf * _sigmoid(-(zff_ref[...] + b_ref[...])), 0.0)
        dff_ref[...] = dff.astype(BF)
        db_ref[...] += jnp.sum(dff, axis=0, keepdims=True)

    rev = lambda i: nt - 1 - i
    hsp = pl.BlockSpec((FOX_H, tm, LANE), lambda i: (0, rev(i), 0))
    const = lambda r, w: pl.BlockSpec((r, w), lambda i: (0, 0))
    seg = _segment_matrix()
    g_all = jnp.concatenate([jnp.tile(g_q, (1, FOX_H)), jnp.tile(g_k, (1, FOX_H))], axis=1)
    scale = jnp.asarray(np.concatenate([np.full((1, 512), 0.125, np.float32), np.ones((1, 512), np.float32)], axis=1))
    (dz, dff, dg, db), lands = _hosted_call(
        body, "fox_post_bwd", (nt,),
        [hsp, hsp, hsp, pl.BlockSpec((tm, 1536), lambda i: (rev(i), 1)),
         pl.BlockSpec((tm, LANE), lambda i: (rev(i), 0)), const(1, LANE), const(1, 1024), const(1, 1024),
         const(1024, LANE), const(LANE, 1024)],
        [pl.BlockSpec((tm, 1536), lambda i: (rev(i), 0)), pl.BlockSpec((tm, LANE), lambda i: (rev(i), 0)),
         const(1, 1024), const(1, LANE)],
        [jax.ShapeDtypeStruct((T, 1536), BF), jax.ShapeDtypeStruct((T, LANE), BF),
         jax.ShapeDtypeStruct((1, 1024), F32), jax.ShapeDtypeStruct((1, LANE), F32)],
        [pltpu.VMEM((1, LANE), F32)], VMEM_BIG, (dq, dk, dv, z_a, z_ff, b_f, g_all, scale, seg, seg.T), push)
    dg_heads = dg.reshape(2, FOX_H, FOX_D).sum(axis=1)
    return (dz, dff, dg_heads[0:1], dg_heads[1:2], db), lands


def _in_bwd(dz_ret, dz_gt, dz_fox, dz_a, dz_ff, w_a, w_ff, x, g_mix, dx2, tm=256, push=None):
    T = x.shape[0]

    def body(r_ref, t_ref, f_ref, a_ref, ff_ref, wa_ref, wf_ref, x_ref, g_ref, dx2_ref, dx_ref, dg_ref):
        i = pl.program_id(0)

        @pl.when(i == 0)
        def _():
            dg_ref[...] = jnp.zeros(dg_ref.shape, F32)

        dh = (_nt(r_ref[...], wa_ref[:, C_RET:C_GT]) + _nt(t_ref[...], wa_ref[:, C_GT:C_FOX])
              + _nt(f_ref[...], wa_ref[:, C_FOX:C_A]) + _nt(a_ref[...], wa_ref[:, C_A:C_END])
              + _nt(ff_ref[...], wf_ref[...]))
        xv = x_ref[...]
        r = lax.rsqrt(jnp.mean(xv * xv, axis=-1, keepdims=True) + EPS)
        xn = xv * r
        dg_ref[...] += jnp.sum(dh * xn, axis=0, keepdims=True)
        dxn = dh * g_ref[...]
        dx_ref[...] = dx2_ref[...] + r * (dxn - xn * jnp.mean(dxn * xn, axis=-1, keepdims=True))

    row = lambda w: pl.BlockSpec((tm, w), lambda i: (i, 0))
    const = lambda shp: pl.BlockSpec(shp, lambda i: (0,) * len(shp))
    return _hosted_call(
        body, "in_bwd", (T // tm,),
        [row(1024), row(512), row(1536), row(2048), row(LANE), const((D_MODEL, C_END)),
         const((D_MODEL, LANE)), row(1024), const((1, 1024)), row(1024)],
        [row(1024), const((1, 1024))],
        [jax.ShapeDtypeStruct((T, 1024), F32), jax.ShapeDtypeStruct((1, 1024), F32)],
        [], VMEM_BIG, (dz_ret, dz_gt, dz_fox, dz_a, dz_ff, w_a, w_ff, x, g_mix, dx2), push)


def _mesh_pos():
    return lax.axis_index("x"), lax.axis_index("y"), lax.axis_index("c")


def _staged_place(src, name):
    stacked = src.ndim == 3
    R, C = src.shape[-2:]
    tr = _row_tile(R, 128, 16)
    n = R // tr
    assert n >= 2

    def body(s_ref, o_ref, buf, sem):
        i = pl.program_id(0)
        slot = i % 2
        x, y, _ = _mesh_pos()
        kme = 2 * x + y

        def out_copy(s, step):
            return pltpu.make_async_copy(buf.at[s], o_ref.at[kme, pl.ds(pl.multiple_of(step * tr, tr), tr), :], sem.at[s])

        @pl.when(i >= 2)
        def _():
            out_copy(slot, i - 2).wait()

        buf[slot] = (s_ref[kme] if stacked else s_ref[...]).astype(BF)
        out_copy(slot, i).start()

        @pl.when(i == n - 1)
        def _():
            out_copy(1 - slot, i - 1).wait()
            out_copy(slot, i).wait()

    in_spec = (pl.BlockSpec((N_CHIP, tr, C), lambda i: (0, i, 0)) if stacked else pl.BlockSpec((tr, C), lambda i: (i, 0)))
    return pl.pallas_call(
        body, name=name, grid=(n,), in_specs=[in_spec], out_specs=pl.BlockSpec(memory_space=pl.ANY),
        out_shape=jax.ShapeDtypeStruct((N_CHIP, R, C), BF),
        scratch_shapes=[pltpu.VMEM((2, tr, C), BF), pltpu.SemaphoreType.DMA((2,))],
        compiler_params=_params(("arbitrary",)),
    )(src)


def _push_copies(src, land, send_sem, recv_sem, receiving):
    x, y, c = _mesh_pos()
    kme = 2 * x + y
    cps = []
    for w in range(len(land)):
        for j, (px, py) in enumerate([(1 - x, y), (x, 1 - y), (1 - x, 1 - y)]):
            kpeer = 2 * px + py
            cps.append(pltpu.make_async_remote_copy(
                src_ref=land[w].at[kme] if src is None else src[w].at[kpeer],
                dst_ref=land[w].at[kpeer if receiving else kme],
                send_sem=send_sem.at[3 * w + j], recv_sem=recv_sem.at[3 * w + j],
                device_id=(px, py, c), device_id_type=MESH))
    return cps


def _gather_two_level(stack, name):
    _, R, C = stack.shape
    hr = R // 2

    def body(_, land, send_sem, recv_sem):
        x, y, c = _mesh_pos()
        kme = 2 * x + y
        chips = [(1 - x, y), (x, 1 - y), (1 - x, 1 - y)]

        def rows(k, core):
            return land.at[k, pl.ds(pl.multiple_of(core * hr, hr), hr), :]

        def copy(idx, k, core, to):
            return pltpu.make_async_remote_copy(src_ref=rows(k, core), dst_ref=rows(k, core), send_sem=send_sem.at[idx],
                                                recv_sem=recv_sem.at[idx], device_id=to, device_id_type=MESH)

        first = [copy(j, kme, c, (px, py, c)) for j, (px, py) in enumerate(chips)]
        for cp in first:
            cp.start()
        passed = [copy(3 + j, 2 * px + py, c, (x, y, 1 - c)) for j, (px, py) in enumerate(chips)]
        for j, (px, py) in enumerate(chips):
            copy(j, 2 * px + py, c, (px, py, c)).wait_recv()
            passed[j].start()
        for j, (px, py) in enumerate(chips):
            copy(3 + j, 2 * px + py, 1 - c, (x, y, 1 - c)).wait_recv()
        for cp in first + passed:
            cp.wait_send()

    anyspec = pl.BlockSpec(memory_space=pl.ANY)
    return pl.pallas_call(
        body, name=name, in_specs=[anyspec], out_specs=anyspec,
        out_shape=jax.ShapeDtypeStruct(stack.shape, stack.dtype), input_output_aliases={0: 0},
        scratch_shapes=[pltpu.SemaphoreType.DMA((6,)), pltpu.SemaphoreType.DMA((6,))],
    )(stack)


def _gather_small(small):
    def body(sv, svo, ssend, srecv, sloc):
        x, y, c = _mesh_pos()
        me = 4 * x + 2 * y + c
        flips = [(b >> 2 & 1, b >> 1 & 1, b & 1) for b in range(1, 8)]
        others = [(1 - x if fx else x, 1 - y if fy else y, 1 - c if fc else c) for fx, fy, fc in flips]
        local = pltpu.make_async_copy(sv, svo.at[me], sloc)
        local.start()
        sends = []
        for j, (px, py, pc) in enumerate(others):
            cp = pltpu.make_async_remote_copy(
                src_ref=sv, dst_ref=svo.at[me], send_sem=ssend.at[j], recv_sem=srecv.at[j],
                device_id=(px, py, pc), device_id_type=MESH)
            cp.start()
            sends.append(cp)
        for j, (px, py, pc) in enumerate(others):
            pltpu.make_async_remote_copy(
                src_ref=sv, dst_ref=svo.at[4 * px + 2 * py + pc], send_sem=ssend.at[j], recv_sem=srecv.at[j],
                device_id=(px, py, pc), device_id_type=MESH).wait_recv()
        for cp in sends:
            cp.wait_send()
        local.wait()

    anyspec = pl.BlockSpec(memory_space=pl.ANY)
    return pl.pallas_call(
        body, name="gather_small", in_specs=[anyspec], out_specs=anyspec,
        out_shape=jax.ShapeDtypeStruct((8,) + small.shape, small.dtype),
        scratch_shapes=[pltpu.SemaphoreType.DMA((7,)), pltpu.SemaphoreType.DMA((7,)), pltpu.SemaphoreType.DMA],
    )(small)


def _sibling_exchange(arrs):
    n = len(arrs)

    def body(*refs):
        ins, outs = refs[:n], refs[n:2 * n]
        send_sems, recv_sems = refs[2 * n:]
        x, y, c = _mesh_pos()
        cps = [pltpu.make_async_remote_copy(
            src_ref=ins[w], dst_ref=outs[w], send_sem=send_sems.at[w], recv_sem=recv_sems.at[w],
            device_id=(x, y, 1 - c), device_id_type=MESH) for w in range(n)]
        for cp in cps:
            cp.start()
        for cp in cps:
            cp.wait_recv()
        for cp in cps:
            cp.wait_send()

    anyspec = pl.BlockSpec(memory_space=pl.ANY)
    return pl.pallas_call(
        body, name="sibling_exchange",
        in_specs=[anyspec] * n, out_specs=[anyspec] * n,
        out_shape=[jax.ShapeDtypeStruct(a.shape, a.dtype) for a in arrs],
        scratch_shapes=[pltpu.SemaphoreType.DMA((n,)), pltpu.SemaphoreType.DMA((n,))],
    )(*arrs)


def _sum_stack(own, recv, name):
    _, R, C = recv.shape
    tr = _row_tile(R, 256, 16)

    def body(g_ref, r_ref, o_ref):
        x, y, _ = _mesh_pos()
        kme = 2 * x + y
        acc = g_ref[kme].astype(F32)
        for d in range(1, N_CHIP):
            acc = acc + r_ref[(kme + d) % N_CHIP].astype(F32)
        o_ref[...] = acc

    spec = pl.BlockSpec((N_CHIP, tr, C), lambda i: (0, i, 0))
    return pl.pallas_call(
        body, name=name, grid=(R // tr,), in_specs=[spec, spec],
        out_specs=pl.BlockSpec((tr, C), lambda i: (i, 0)),
        out_shape=jax.ShapeDtypeStruct((R, C), F32),
        compiler_params=_params(("parallel",)),
    )(own, recv)


def _adam_math(w, g, m, v):
    m2 = ADAM_B1 * m + (1.0 - ADAM_B1) * g
    v2 = ADAM_B2 * v + (1.0 - ADAM_B2) * (g * g)
    m_hat = m2 / (1.0 - ADAM_B1 ** ADAM_STEP)
    v_hat = v2 / (1.0 - ADAM_B2 ** ADAM_STEP)
    delta = -ADAM_LR * (m_hat / (jnp.sqrt(v_hat) + ADAM_EPS) + ADAM_WD * w)
    return delta, m2, v2


def _adamw(w, m, v, s0, s1, name):
    R, C = w.shape
    tr = _row_tile(R, 128, 8)

    def body(w_ref, m_ref, v_ref, a_ref, b_ref, g_ref, d_ref, m2_ref, v2_ref):
        g = a_ref[...] + b_ref[...]
        delta, m2, v2 = _adam_math(w_ref[...], g, m_ref[...], v_ref[...])
        g_ref[...] = g
        d_ref[...] = delta
        m2_ref[...] = m2
        v2_ref[...] = v2

    spec = pl.BlockSpec((tr, C), lambda i: (i, 0))
    shp = jax.ShapeDtypeStruct((R, C), F32)
    return pl.pallas_call(
        body, name=name, grid=(R // tr,), in_specs=[spec] * 5, out_specs=[spec] * 4, out_shape=[shp] * 4,
        compiler_params=_params(("parallel",), VMEM_BIG),
    )(w, m, v, s0, s1)


def _adamw_small(ws, ms, vs, gathered):
    n = len(SMALL)

    def body(*refs):
        w_refs, m_refs, v_refs, s_ref = refs[:n], refs[n:2 * n], refs[2 * n:3 * n], refs[3 * n]
        outs = refs[3 * n + 1:]
        g_all = s_ref[0]
        for d in range(1, 8):
            g_all = g_all + s_ref[d]
        off = 0
        for i, (_, width) in enumerate(SMALL):
            g = g_all[:, off:off + width]
            delta, m2, v2 = _adam_math(w_refs[i][...], g, m_refs[i][...], v_refs[i][...])
            for kind, val in enumerate((g, delta, m2, v2)):
                outs[kind * n + i][...] = val
            off += width + (-width % LANE)

    shapes = [jax.ShapeDtypeStruct((1, width), F32) for _, width in SMALL]
    res = pl.pallas_call(body, name="adamw_small", out_shape=shapes * 4)(*ws, *ms, *vs, gathered)
    return [dict(zip([nm for nm, _ in SMALL], res[kind * n:(kind + 1) * n])) for kind in range(4)]


SMALL = (("g_mix", 1024), ("g_ffn", 1024), ("g_ret_norm", 512), ("g_fox_q", 64), ("g_fox_k", 64), ("b_forget", 8))
SMALL_W = 3072


def _pack_small(parts):
    cols = []
    for (name, n) in SMALL:
        p = parts[name].reshape(1, -1)[:, :n]
        pad = -n % LANE
        cols.append(jnp.pad(p, ((0, 0), (0, pad))) if pad else p)
    used = sum(c.shape[1] for c in cols)
    cols.append(jnp.zeros((1, SMALL_W - used), F32))
    return jnp.concatenate(cols, axis=1)


def kernel(x, g_mix, w_in, b_forget, g_ret_norm, w_ret_o, g_fox_q, g_fox_k, w_fox_o, w_out, g_ffn, w_gate, w_up, w_down, loss_target, m_g_mix, m_w_in, m_b_forget, m_g_ret_norm, m_w_ret_o, m_g_fox_q, m_g_fox_k, m_w_fox_o, m_w_out, m_g_ffn, m_w_gate, m_w_up, m_w_down, v_g_mix, v_w_in, v_b_forget, v_g_ret_norm, v_w_ret_o, v_g_fox_q, v_g_fox_k, v_w_fox_o, v_w_out, v_g_ffn, v_w_gate, v_w_up, v_w_down):
    T = x.shape[1]
    xs = x[0]
    tgt = loss_target[0]
    big_names = ("w_in", "w_ret_o", "w_fox_o", "w_out", "w_gate", "w_up", "w_down")
    tr = lambda a: jnp.swapaxes(a[0], 0, 1)
    big_w = dict(w_in=w_in[0], w_ret_o=w_ret_o[0], w_fox_o=w_fox_o[0], w_out=w_out[0], w_gate=tr(w_gate),
                 w_up=tr(w_up), w_down=w_down[0])
    big_m = dict(w_in=m_w_in[0], w_ret_o=m_w_ret_o[0], w_fox_o=m_w_fox_o[0], w_out=m_w_out[0], w_gate=tr(m_w_gate),
                 w_up=tr(m_w_up), w_down=m_w_down[0])
    big_v = dict(w_in=v_w_in[0], w_ret_o=v_w_ret_o[0], w_fox_o=v_w_fox_o[0], w_out=v_w_out[0], w_gate=tr(v_w_gate),
                 w_up=tr(v_w_up), w_down=v_w_down[0])
    small_w = dict(g_mix=g_mix, g_ffn=g_ffn, g_ret_norm=g_ret_norm, g_fox_q=g_fox_q, g_fox_k=g_fox_k, b_forget=b_forget)
    small_m = dict(g_mix=m_g_mix, g_ffn=m_g_ffn, g_ret_norm=m_g_ret_norm, g_fox_q=m_g_fox_q, g_fox_k=m_g_fox_k,
                   b_forget=m_b_forget)
    small_v = dict(g_mix=v_g_mix, g_ffn=v_g_ffn, g_ret_norm=v_g_ret_norm, g_fox_q=v_g_fox_q, g_fox_k=v_g_fox_k,
                   b_forget=v_b_forget)

    stacks = {n: _staged_place(big_w[n], "place_" + n) for n in big_names}
    s_in = _gather_two_level(stacks["w_in"], "gather_w_in")
    w_a, w_ff = _assemble_w_in(s_in)
    b_pad = jnp.pad(b_forget, ((0, 0), (0, LANE - FOX_H)))
    cos_t, sin_t = _rope_tables(T)
    consts = _ret_consts()

    h = _rms_cast(xs, g_mix)
    z_a, (s_ro, s_fo, s_out, s_gate) = _mm_nn(
        h, w_a, "proj_in", BF, tm=1024,
        push=(None, [stacks["w_ret_o"], stacks["w_fox_o"], stacks["w_out"], stacks["w_gate"]]))
    z_ff, _ = _mm_nn(h, w_ff, "proj_ff", F32)
    (qr, kr, qf, kf, vf, c_cum, nmax), (s_up,) = _mix_prep(
        z_a, z_ff, cos_t, sin_t, b_pad, g_fox_q, g_fox_k, push=(None, [stacks["w_up"]]))
    jlo, end_both, end_last, tame = _prune_tables(c_cum, nmax, FOX_SUB)
    o_raw, u_r, states = _ret_fwd(qr, kr, z_a, g_ret_norm, consts)
    o_fox, q2 = _fox_fwd(jlo, tame, qf, kf, vf)
    (y_r, y_f, mrg, x2, h2, o_cat), (s_down,) = _merge_out(u_r, o_fox, z_a, xs, g_ffn, s_ro, s_fo, s_out,
                                                            push=(None, [stacks["w_down"]]))
    sa, sb, act, dy, loss_vec = _ffn_fwd(h2, x2, tgt, s_gate, s_up, s_down)
    loss = lax.psum(0.5 / D_MODEL * jnp.sum(loss_vec), ("x", "y", "c"))

    def scatter_job(grads):
        return (grads, [lax.empty(g.shape, g.dtype) for g in grads])

    dgp, dup, dx2, dg_ffn = _ffn_bwd(dy, sa, sb, x2, g_ffn, s_gate, s_up, s_down)
    g_gate, _ = _grad_astack(dgp, h2, "gw_gate")
    g_up, (r_gate,) = _grad_astack(dup, h2, "gw_up", push=scatter_job([g_gate]))
    g_down, (r_up,) = _grad_astack(act, dy, "gw_down", push=scatter_job([g_up]))
    (d_yr, d_yf, dz_gt, dz_a, d_o, do_fox, dg_ret), (r_down,) = _out_bwd(
        dx2, z_a, y_r, y_f, o_raw, o_fox, g_ret_norm, s_ro, s_fo, s_out, push=scatter_job([g_down]))
    dz_ret, _ = _ret_bwd(d_o, qr, kr, z_a, states, cos_t, sin_t, consts)
    dq_f, dk_f, dv_f = _fox_bwd(end_both, end_last, q2, kf, vf, do_fox)
    g_mid = [_grad_colstack(u_r, d_yr, "gw_ret_o", 256), _grad_colstack(o_cat, d_yf, "gw_fox_o", 256),
             _grad_plain(mrg, dx2, "gw_out", BF).reshape(N_CHIP, 256, D_MODEL)]
    (dz_fox, dz_ff, dg_q, dg_k, db_f), (r_ro, r_fo, r_out) = _fox_post_bwd(
        dq_f, dk_f, dv_f, z_a, z_ff, b_pad, g_fox_q, g_fox_k, push=scatter_job(g_mid))
    g_in = _pack_g_in(_grad_plain(h, dz_ret, "gw_in_ret", F32), _grad_plain(h, dz_gt, "gw_in_gt", F32),
                      _grad_plain(h, dz_fox, "gw_in_fox", F32, tn=1536),
                      _grad_plain(h, dz_a, "gw_in_a", F32, tk=1024, tn=2048),
                      _grad_plain(h, dz_ff, "gw_in_ff", F32))
    (grad_x, dg_mix), (r_in,) = _in_bwd(dz_ret, dz_gt, dz_fox, dz_a, dz_ff, w_a, w_ff, xs, g_mix, dx2,
                                        push=scatter_job([g_in]))
    small_g = _pack_small(dict(g_mix=dg_mix, g_ffn=dg_ffn, g_ret_norm=dg_ret, g_fox_q=dg_q, g_fox_k=dg_k, b_forget=db_f))

    small_all = _gather_small(small_g)
    sums = [_sum_stack(g, r, "sum_" + n) for g, r, n in zip(
        [g_in] + g_mid + [g_gate, g_up, g_down], [r_in, r_ro, r_fo, r_out, r_gate, r_up, r_down], big_names)]
    sib = _sibling_exchange(sums)
    big_out = {n: _adamw(big_w[n], big_m[n], big_v[n], sums[i], sib[i], "adamw_" + n) for i, n in enumerate(big_names)}
    small_out = _adamw_small(*[[d[nm] for nm, _ in SMALL] for d in (small_w, small_m, small_v)], small_all)

    order = ("g_mix", "w_in", "b_forget", "g_ret_norm", "w_ret_o", "g_fox_q", "g_fox_k", "w_fox_o", "w_out", "g_ffn",
             "w_gate", "w_up", "w_down")
    outs = [loss, grad_x[None]]
    for idx in range(4):
        for n in order:
            if n in ("w_gate", "w_up"):
                outs.append(jnp.swapaxes(big_out[n][idx], 0, 1)[None])
            else:
                outs.append(big_out[n][idx][None] if n in big_out else small_out[idx][n])
    return tuple(outs)
```

```python
import functools
import math

import numpy as np
import jax
import jax.numpy as jnp
from jax import lax
from jax.experimental import pallas as pl
from jax.experimental.pallas import tpu as pltpu

F32 = jnp.float32
BF = jnp.bfloat16
MESH = pl.DeviceIdType.MESH

D_MODEL = 1024
D_FF = 2816
N_CHIP = 4
FF_SH = D_FF // N_CHIP
IN_COLS = 5128
IN_SH = IN_COLS // N_CHIP
RET_H, RET_DV = 4, 128
FOX_H, FOX_D = 8, 64
CHUNK = 256
EPS = 1e-6
NEG = -1e30
LANE = 128
C_RET, C_GT, C_FOX, C_A, C_END = 0, 1024, 1536, 3072, 5120
L_CQ, L_CK, L_LSE, L_MAX = 64, 67, 70, 73

ADAM_LR, ADAM_B1, ADAM_B2, ADAM_EPS, ADAM_WD, ADAM_STEP = 0.001, 0.9, 0.999, 1e-08, 0.01, 10
VMEM_BIG = 56 * 1024 * 1024
VMEM_HUGE = 60 * 1024 * 1024
GRAD_TK = 2048
FFN_TM = 512
FOX_SUB = 512


def _nn(a, b):
    return lax.dot_general(a, b, (((1,), (0,)), ((), ())), preferred_element_type=F32)


def _nt(a, b):
    return lax.dot_general(a, b, (((1,), (1,)), ((), ())), preferred_element_type=F32)


def _tn(a, b):
    return lax.dot_general(a, b, (((0,), (0,)), ((), ())), preferred_element_type=F32)


def _split3(x):
    hi = x.astype(BF)
    r = x - hi.astype(F32)
    mid = r.astype(BF)
    lo = (r - mid.astype(F32)).astype(BF)
    return hi, mid, lo


def _sigmoid(x):
    return 0.5 * jnp.tanh(0.5 * x) + 0.5


def _swap32(x):
    lane = lax.broadcasted_iota(jnp.int32, x.shape, 1)
    return jnp.where(lane < 32, pltpu.roll(x, 96, 1), pltpu.roll(x, 32, 1))


def _params(sem, vmem=None):
    return pltpu.CompilerParams(dimension_semantics=sem, vmem_limit_bytes=vmem)


def _row_tile(rows, cap, mult):
    return max(d for d in range(mult, cap + 1, mult) if rows % d == 0)


def _assemble_w_in(stack, tr=256):
    def body(s_ref, a_ref, f_ref):
        full = jnp.concatenate([s_ref[k].astype(F32) for k in range(N_CHIP)], axis=-1)
        a_ref[...] = jnp.concatenate([full[:, :3072], full[:, 3080:IN_COLS]], axis=-1).astype(BF)
        f_ref[...] = jnp.concatenate([full[:, 3072:3080], jnp.zeros((tr, LANE - FOX_H), F32)], axis=-1).astype(BF)

    return pl.pallas_call(
        body, name="assemble_w_in", grid=(D_MODEL // tr,),
        in_specs=[pl.BlockSpec((N_CHIP, tr, IN_SH), lambda i: (0, i, 0))],
        out_specs=[pl.BlockSpec((tr, C_END), lambda i: (i, 0)), pl.BlockSpec((tr, LANE), lambda i: (i, 0))],
        out_shape=[jax.ShapeDtypeStruct((D_MODEL, C_END), BF), jax.ShapeDtypeStruct((D_MODEL, LANE), BF)],
        compiler_params=_params(("parallel",), VMEM_BIG),
    )(stack)


def _pack_g_in(g_ret, g_gt, g_fox, g_a, g_ff, tr=256):
    def body(r_ref, t_ref, x_ref, a_ref, f_ref, o_ref):
        full = jnp.concatenate([r_ref[...], t_ref[...], x_ref[...], f_ref[...][:, :FOX_H], a_ref[...]], axis=-1)
        for k in range(N_CHIP):
            o_ref[k] = full[:, k * IN_SH:(k + 1) * IN_SH].astype(BF)

    def spec(w):
        return pl.BlockSpec((tr, w), lambda i: (i, 0))

    return pl.pallas_call(
        body, name="pack_g_in", grid=(D_MODEL // tr,),
        in_specs=[spec(1024), spec(512), spec(1536), spec(2048), spec(LANE)],
        out_specs=pl.BlockSpec((N_CHIP, tr, IN_SH), lambda i: (0, i, 0)),
        out_shape=jax.ShapeDtypeStruct((N_CHIP, D_MODEL, IN_SH), BF),
        compiler_params=_params(("parallel",), VMEM_BIG),
    )(g_ret, g_gt, g_fox, g_a, g_ff)


def _rms_cast(x, g, tm=512):
    T = x.shape[0]

    def body(x_ref, g_ref, o_ref):
        xv = x_ref[...]
        r = lax.rsqrt(jnp.mean(xv * xv, axis=-1, keepdims=True) + EPS)
        o_ref[...] = (xv * r * g_ref[...]).astype(BF)

    return pl.pallas_call(
        body, name="rms_cast", grid=(T // tm,),
        in_specs=[pl.BlockSpec((tm, D_MODEL), lambda i: (i, 0)), pl.BlockSpec((1, D_MODEL), lambda i: (0, 0))],
        out_specs=pl.BlockSpec((tm, D_MODEL), lambda i: (i, 0)),
        out_shape=jax.ShapeDtypeStruct((T, D_MODEL), BF),
        compiler_params=_params(("parallel",)),
    )(x, g)


def _hosted_call(body, name, grid, in_specs, out_specs, out_shape, scratch_shapes, vmem, args, push):
    sem = ("arbitrary",) * len(grid)
    if push is None:
        res = pl.pallas_call(body, name=name, grid=grid, in_specs=in_specs, out_specs=out_specs, out_shape=out_shape,
                             scratch_shapes=scratch_shapes, compiler_params=_params(sem, vmem))(*args)
        return list(res), []
    srcs, lands = push
    ns, nl, n_in, n_out = (0 if srcs is None else len(srcs)), len(lands), len(in_specs), len(out_specs)
    n_scr = len(scratch_shapes)

    def wrapped(*refs):
        pos = n_in + ns + nl
        ins, x_in = refs[:n_in], refs[n_in:pos]
        outs, x_out = refs[pos:pos + n_out], refs[pos + n_out:pos + n_out + nl]
        scr = refs[pos + n_out + nl:pos + n_out + nl + n_scr]
        ssem, rsem = refs[-2], refs[-1]
        src = None if srcs is None else x_in[:ns]
        ids = [pl.program_id(a) for a in range(len(grid))]
        first = functools.reduce(lambda p, q: p & q, [ids[a] == 0 for a in range(len(grid))])
        last = functools.reduce(lambda p, q: p & q, [ids[a] == grid[a] - 1 for a in range(len(grid))])

        @pl.when(first)
        def _():
            for cp in _push_copies(src, x_out, ssem, rsem, False):
                cp.start()

        body(*ins, *outs, *scr)

        @pl.when(last)
        def _():
            for cp in _push_copies(src, x_out, ssem, rsem, True):
                cp.wait_recv()
                cp.wait_send()

    anyspec = pl.BlockSpec(memory_space=pl.ANY)
    extra = ([] if srcs is None else list(srcs)) + list(lands)
    res = pl.pallas_call(
        wrapped, name=name, grid=grid,
        in_specs=list(in_specs) + [anyspec] * len(extra), out_specs=list(out_specs) + [anyspec] * nl,
        out_shape=list(out_shape) + [jax.ShapeDtypeStruct(a.shape, a.dtype) for a in lands],
        input_output_aliases={n_in + ns + i: n_out + i for i in range(nl)},
        scratch_shapes=list(scratch_shapes) + [pltpu.SemaphoreType.DMA((3 * nl,)), pltpu.SemaphoreType.DMA((3 * nl,))],
        compiler_params=_params(sem, vmem),
    )(*args, *extra)
    return list(res[:n_out]), list(res[n_out:])


def _mm_nn(a, b, name, out_dtype, tm=512, tn=1024, push=None):
    M, K = a.shape
    N = b.shape[1]
    tn = min(tn, N)

    def body(a_ref, b_ref, o_ref):
        o_ref[...] = _nn(a_ref[...], b_ref[...]).astype(o_ref.dtype)

    (out,), lands = _hosted_call(
        body, name, (N // tn, M // tm),
        [pl.BlockSpec((tm, K), lambda j, i: (i, 0)), pl.BlockSpec((K, tn), lambda j, i: (0, j))],
        [pl.BlockSpec((tm, tn), lambda j, i: (i, j))], [jax.ShapeDtypeStruct((M, N), out_dtype)], [], None, (a, b), push)
    return out, lands


def _mm_tn(a, b, name, grid, a_spec, b_spec, o_spec, out_shape, acc_shape):
    nk = grid[-1]

    def body(a_ref, b_ref, o_ref, acc):
        k = pl.program_id(len(grid) - 1)

        @pl.when(k == 0)
        def _():
            acc[...] = jnp.zeros(acc.shape, F32)

        acc[...] += _tn(a_ref[...].astype(BF), b_ref[...].astype(BF))

        @pl.when(k == nk - 1)
        def _():
            o_ref[...] = acc[...].astype(o_ref.dtype)

    return pl.pallas_call(
        body, name=name, grid=grid, in_specs=[a_spec, b_spec], out_specs=o_spec, out_shape=out_shape,
        scratch_shapes=[pltpu.VMEM(acc_shape, F32)],
        compiler_params=_params(("parallel",) * (len(grid) - 1) + ("arbitrary",), VMEM_BIG),
    )(a, b)


def _grad_plain(a, b, name, out_dtype, tk=GRAD_TK, tn=1024):
    T, M = a.shape
    N = b.shape[1]
    tn = min(tn, N)
    return _mm_tn(a, b, name, (N // tn, T // tk),
                  pl.BlockSpec((tk, M), lambda j, k: (k, 0)), pl.BlockSpec((tk, tn), lambda j, k: (k, j)),
                  pl.BlockSpec((M, tn), lambda j, k: (0, j)), jax.ShapeDtypeStruct((M, N), out_dtype), (M, tn))


def _grad_colstack(a, b, name, wcol, tk=GRAD_TK):
    T, M = a.shape
    N = b.shape[1]
    S = N // wcol
    nk = T // tk

    def body(a_ref, b_ref, o_ref, acc):
        k = pl.program_id(0)

        @pl.when(k == 0)
        def _():
            acc[...] = jnp.zeros(acc.shape, F32)

        acc[...] += _tn(a_ref[...], b_ref[...])

        @pl.when(k == nk - 1)
        def _():
            for s in range(S):
                o_ref[s] = acc[:, s * wcol:(s + 1) * wcol].astype(BF)

    return pl.pallas_call(
        body, name=name, grid=(nk,),
        in_specs=[pl.BlockSpec((tk, M), lambda k: (k, 0)), pl.BlockSpec((tk, N), lambda k: (k, 0))],
        out_specs=pl.BlockSpec((S, M, wcol), lambda k: (0, 0, 0)), out_shape=jax.ShapeDtypeStruct((S, M, wcol), BF),
        scratch_shapes=[pltpu.VMEM((M, N), F32)], compiler_params=_params(("arbitrary",), VMEM_BIG),
    )(a, b)


def _grad_astack(a, b, name, tk=1024, push=None):
    S, T, m = a.shape
    N = b.shape[1]
    nk = T // tk

    def body(a_ref, b_ref, o_ref, acc):
        k = pl.program_id(0)

        @pl.when(k == 0)
        def _():
            acc[...] = jnp.zeros(acc.shape, F32)

        bb = b_ref[...].astype(BF)
        for s in range(S):
            acc[s] += _tn(a_ref[s], bb)

        @pl.when(k == nk - 1)
        def _():
            o_ref[...] = acc[...].astype(BF)

    (out,), lands = _hosted_call(
        body, name, (nk,),
        [pl.BlockSpec((S, tk, m), lambda k: (0, k, 0)), pl.BlockSpec((tk, N), lambda k: (k, 0))],
        [pl.BlockSpec((S, m, N), lambda k: (0, 0, 0))], [jax.ShapeDtypeStruct((S, m, N), BF)],
        [pltpu.VMEM((S, m, N), F32)], VMEM_BIG, (a, b), push)
    return out, lands


def _rope_tables(T):
    half = 32
    pos = np.arange(T, dtype=np.float32)
    inv_freq = (np.float32(1.0) / (np.float32(10000.0) ** (np.arange(half, dtype=np.float32) / np.float32(half)))).astype(np.float32)
    ang = (pos[:, None] * inv_freq[None, :]).astype(np.float32)
    cos, sin = np.cos(ang).astype(np.float32), np.sin(ang).astype(np.float32)
    z = np.zeros((T, 64), np.float32)
    return (jnp.asarray(np.concatenate([cos, cos, z], axis=-1)), jnp.asarray(np.concatenate([-sin, sin, z], axis=-1)))


def _ret_consts():
    h = np.arange(RET_H, dtype=np.float32)
    log_g = np.log1p(-(np.float32(2.0) ** (-5.0 - h))).astype(np.float32)
    idx = np.arange(CHUNK, dtype=np.float32)
    diff = idx[:, None] - idx[None, :]
    decay = np.where(diff[None] >= 0, np.exp(np.maximum(diff, 0.0)[None] * log_g[:, None, None]), 0.0)
    zeta = np.exp((CHUNK - 1.0 - idx)[None, :] * log_g[:, None])
    xi = np.exp((idx + 1.0)[None, :] * log_g[:, None])
    gc = np.exp(CHUNK * log_g)
    bc = lambda v: np.broadcast_to(v[:, :, None], (RET_H, CHUNK, LANE)).astype(np.float32)
    gcb = np.broadcast_to(gc[:, None, None], (RET_H, LANE, LANE)).astype(np.float32)
    return (jnp.asarray(decay.astype(np.float32)), jnp.asarray(bc(zeta)), jnp.asarray(bc(xi)), jnp.asarray(gcb))


def _mix_prep(z_a, h, w_ff, cos_t, sin_t, b_f, g_q, g_k, tm=256, push=None):
    T = z_a.shape[0]

    def body(zqk_ref, zf_ref, h_ref, wff_ref, cos_ref, sin_ref, b_ref, g_ref, seg_ref, segt_ref,
             qr_ref, kr_ref, qf_ref, kf_ref, vf_ref, c_ref, nmax_ref, zff_ref, carry):
        i = pl.program_id(0)
        zff = _nn(h_ref[...], wff_ref[...])
        zff_ref[...] = zff

        @pl.when(i == 0)
        def _():
            carry[...] = jnp.zeros(carry.shape, F32)
            nmax_ref[...] = jnp.zeros(nmax_ref.shape, F32)

        lane = lax.broadcasted_iota(jnp.int32, (tm, LANE), 1)
        zpad = jnp.zeros((tm, 64), F32)
        cosv, sinv = cos_ref[...], sin_ref[...]
        zqk = zqk_ref[...].astype(F32)
        for h in range(RET_H):
            for src, dst, scale in ((0, qr_ref, 1.0), (256, kr_ref, 0.125)):
                xh = jnp.concatenate([zqk[:, src + 64 * h: src + 64 * h + 64], zpad], axis=-1)
                rot = xh * cosv + _swap32(xh) * sinv
                dst[h] = (rot * scale).astype(BF)

        lf_in = zff + b_ref[...]
        logf = jnp.minimum(lf_in, 0.0) - jnp.log(1.0 + jnp.exp(-jnp.abs(lf_in)))
        row = lax.broadcasted_iota(jnp.int32, (tm, tm), 0)
        col = lax.broadcasted_iota(jnp.int32, (tm, tm), 1)
        tri = (row >= col).astype(BF)
        hi, mid, lo = _split3(logf)
        cs = _nn(tri, hi) + _nn(tri, mid) + _nn(tri, lo) + carry[...]
        carry[...] = cs[tm - 1:tm, :]
        c_ref[...] = cs

        def seg_sum(v):
            return sum(_nn(t, seg_ref[...]) for t in _split3(v))

        zf = zf_ref[...].astype(F32)
        xqk = zf[:, :1024]
        rinv = lax.rsqrt(seg_sum(xqk * xqk) * (1.0 / FOX_D) + EPS)
        xn = xqk * sum(_nn(t, segt_ref[...]) for t in _split3(rinv)) * g_ref[...]
        nmax_ref[...] = jnp.maximum(nmax_ref[...], jnp.max(seg_sum(xn * xn), axis=0, keepdims=True))

        one = jnp.ones((tm, LANE), F32)
        for h in range(FOX_H):
            c = cs[:, h:h + 1]
            chi, cmid, clo = [t.astype(F32) for t in _split3(c)]
            qn = xn[:, 64 * h:64 * h + 64]
            kn = xn[:, 512 + 64 * h:512 + 64 * h + 64]
            vh = zf[:, 1024 + 64 * h:1024 + 64 * h + 64]
            qa = jnp.concatenate([qn, zpad], axis=-1)
            qa = jnp.where(lane == L_CQ, chi, jnp.where(lane == L_CQ + 1, cmid, jnp.where(lane == L_CQ + 2, clo, qa)))
            qa = jnp.where((lane >= L_CK) & (lane < L_CK + 3), one, qa)
            ka = jnp.concatenate([kn, zpad], axis=-1)
            ka = jnp.where(lane == L_CK, -chi, jnp.where(lane == L_CK + 1, -cmid, jnp.where(lane == L_CK + 2, -clo, ka)))
            ka = jnp.where(((lane >= L_CQ) & (lane < L_CQ + 3)) | ((lane >= L_LSE) & (lane < L_MAX + 3)), one, ka)
            va = jnp.concatenate([vh, zpad], axis=-1)
            va = jnp.where((lane >= 64) & (lane < 67), one, va)
            qf_ref[h] = qa.astype(BF)
            kf_ref[h] = ka.astype(BF)
            vf_ref[h] = va.astype(BF)

    hspec4 = pl.BlockSpec((RET_H, tm, LANE), lambda i: (0, i, 0))
    hspec8 = pl.BlockSpec((FOX_H, tm, LANE), lambda i: (0, i, 0))
    const = lambda r, w: pl.BlockSpec((r, w), lambda i: (0, 0))
    seg = _segment_matrix()
    g_all = jnp.concatenate([jnp.tile(g_q * 0.125, (1, FOX_H)), jnp.tile(g_k, (1, FOX_H))], axis=1)
    return _hosted_call(
        body, "mix_prep", (T // tm,),
        [pl.BlockSpec((tm, 512), lambda i: (i, 0)), pl.BlockSpec((tm, 1536), lambda i: (i, 1)),
         pl.BlockSpec((tm, D_MODEL), lambda i: (i, 0)), const(D_MODEL, LANE), pl.BlockSpec((tm, LANE), lambda i: (i, 0)),
         pl.BlockSpec((tm, LANE), lambda i: (i, 0)), const(1, LANE), const(1, 1024), const(1024, LANE), const(LANE, 1024)],
        [hspec4, hspec4, hspec8, hspec8, hspec8, pl.BlockSpec((tm, LANE), lambda i: (i, 0)), const(1, LANE),
         pl.BlockSpec((tm, LANE), lambda i: (i, 0))],
        [jax.ShapeDtypeStruct((RET_H, T, LANE), BF)] * 2 + [jax.ShapeDtypeStruct((FOX_H, T, LANE), BF)] * 3
        + [jax.ShapeDtypeStruct((T, LANE), F32), jax.ShapeDtypeStruct((1, LANE), F32), jax.ShapeDtypeStruct((T, LANE), F32)],
        [pltpu.VMEM((1, LANE), F32)], VMEM_BIG, (z_a, z_a, h, w_ff, cos_t, sin_t, b_f, g_all, seg, seg.T), push)


def _segment_matrix():
    m = np.zeros((2 * FOX_H * FOX_D, LANE), np.float32)
    m[np.arange(2 * FOX_H * FOX_D), np.arange(2 * FOX_H * FOX_D) // FOX_D] = 1.0
    return jnp.asarray(m, dtype=BF)


def _ret_fwd(qr, kr, z_a, g_ret, consts, tt=512):
    T = z_a.shape[0]
    nch = tt // CHUNK
    decay, zeta, xi, gcb = consts

    def body(q_ref, k_ref, v_ref, gt_ref, g_ref, d_ref, ze_ref, xi_ref, gc_ref, o_ref, u_ref, st_ref, r_sc):
        i = pl.program_id(0)

        @pl.when(i == 0)
        def _():
            r_sc[...] = jnp.zeros(r_sc.shape, F32)

        for c in range(nch):
            rows = slice(c * CHUNK, (c + 1) * CHUNK)
            for h in range(RET_H):
                cols = slice(h * RET_DV, (h + 1) * RET_DV)
                q, k = q_ref[h, rows, :], k_ref[h, rows, :]
                v32 = v_ref[rows, cols].astype(F32)
                r = r_sc[h]
                st_ref[h, c * CHUNK:c * CHUNK + LANE, :] = r
                s = _nt(q, k) * d_ref[h]
                o = _nn(s.astype(BF), v32.astype(BF)) + _nn(q, r.astype(BF)) * xi_ref[h]
                r_sc[h] = gc_ref[h] * r + _tn(k, (v32 * ze_ref[h]).astype(BF))
                o_ref[rows, cols] = o
                mu = jnp.mean(o, axis=-1, keepdims=True)
                xc = o - mu
                on = xc * lax.rsqrt(jnp.mean(xc * xc, axis=-1, keepdims=True) + EPS)
                gt = gt_ref[rows, cols].astype(F32)
                u_ref[rows, cols] = (gt * _sigmoid(gt) * (on * g_ref[:, cols])).astype(BF)

    hspec = pl.BlockSpec((RET_H, tt, LANE), lambda i: (0, i, 0))
    cspec = pl.BlockSpec((RET_H, CHUNK, LANE), lambda i: (0, 0, 0))
    dspec = pl.BlockSpec((RET_H, CHUNK, CHUNK), lambda i: (0, 0, 0))
    sspec = pl.BlockSpec((RET_H, LANE, LANE), lambda i: (0, 0, 0))
    return pl.pallas_call(
        body, name="ret_fwd", grid=(T // tt,),
        in_specs=[hspec, hspec, pl.BlockSpec((tt, 512), lambda i: (i, 1)), pl.BlockSpec((tt, 512), lambda i: (i, 2)),
                  pl.BlockSpec((1, 512), lambda i: (0, 0)), dspec, cspec, cspec, sspec],
        out_specs=[pl.BlockSpec((tt, 512), lambda i: (i, 0)), pl.BlockSpec((tt, 512), lambda i: (i, 0)), hspec],
        out_shape=[jax.ShapeDtypeStruct((T, 512), F32), jax.ShapeDtypeStruct((T, 512), BF),
                   jax.ShapeDtypeStruct((RET_H, T, LANE), F32)],
        scratch_shapes=[pltpu.VMEM((RET_H, LANE, LANE), F32)],
        compiler_params=_params(("arbitrary",), VMEM_BIG),
    )(qr, kr, z_a, z_a, g_ret, decay, zeta, xi, gcb)


PRUNE_LOG = -110.0
TAME_LOGIT_SPAN = 60.0


def _prune_tables(c, nmax, sub):
    n = c.shape[0] // sub
    u = jnp.sqrt(nmax[0, :FOX_H] * nmax[0, FOX_H:2 * FOX_H]) * 1.02 + 0.5
    first = c[0::sub, :FOX_H].T
    last = c[sub - 1::sub, :FOX_H].T
    blk = jnp.arange(n, dtype=jnp.int32)
    needed = (2.0 * u[:, None, None] + first[:, :, None] - last[:, None, :] >= PRUNE_LOG) | (blk[None, :] >= blk[:, None])[None]
    jlo = jnp.argmax(needed, axis=2).astype(jnp.int32)

    def end_of(key_block):
        reach = jlo[:, None, :] <= key_block[None, :, None]
        return (n - jnp.argmax(reach[:, :, ::-1], axis=2)).astype(jnp.int32)

    sup = 2 * jnp.arange(n // 2, dtype=jnp.int32)
    end_last = end_of(sup + 1)
    end_both = jnp.clip(end_of(sup), sup[None, :] + 2, end_last)
    tame = (2.0 * u < TAME_LOGIT_SPAN).astype(jnp.int32)
    return jlo, end_both, end_last, tame


def _fox_fwd(jlo, tame, q, k, v, sub=FOX_SUB):
    H, T, _ = q.shape
    tb = 2 * sub

    def body(js_ref, tame_ref, q_ref, k_ref, v_ref, o_ref, q2_ref, mx_sc, acc_sc):
        i = pl.program_id(1)
        hd = pl.program_id(0)
        starts = [jnp.minimum(js_ref[hd, 2 * i], 2 * i), jnp.minimum(js_ref[hd, 2 * i + 1], 2 * i)]
        lane = lax.broadcasted_iota(jnp.int32, (sub, LANE), 1)
        row = lax.broadcasted_iota(jnp.int32, (sub, sub), 0)
        col = lax.broadcasted_iota(jnp.int32, (sub, sub), 1)
        causal = row >= col
        qs = [q_ref[0:sub, :], q_ref[sub:tb, :]]
        d0 = pl.multiple_of(i * tb, tb)
        d1 = pl.multiple_of(i * tb + sub, sub)
        k0, k1 = k_ref[pl.ds(d0, sub), :], k_ref[pl.ds(d1, sub), :]
        v0, v1 = v_ref[pl.ds(d0, sub), :], v_ref[pl.ds(d1, sub), :]

        def lane_max(s):
            m = s[:, 0:LANE]
            for c in range(1, s.shape[1] // LANE):
                m = jnp.maximum(m, s[:, c * LANE:(c + 1) * LANE])
            return m

        def put3(base, first, val):
            hi, mid, lo = _split3(val)
            return jnp.where(lane == first, hi, jnp.where(lane == first + 1, mid, jnp.where(lane == first + 2, lo, base)))

        def row_max():
            mx_sc[...] = jnp.full(mx_sc.shape, NEG, F32)
            for a in range(2):
                def max_body(j, carry, a=a):
                    kb = k_ref[pl.ds(pl.multiple_of(j * sub, sub), sub), :]
                    mx_sc[a] = jnp.maximum(mx_sc[a], lane_max(_nt(qs[a], kb)))
                    return carry

                lax.fori_loop(starts[a], 2 * i, max_body, 0)
            mx = [jnp.maximum(mx_sc[0], lane_max(jnp.where(causal, _nt(qs[0], k0), NEG))),
                  jnp.maximum(jnp.maximum(mx_sc[1], lane_max(_nt(qs[1], k0))),
                              lane_max(jnp.where(causal, _nt(qs[1], k1), NEG)))]
            return [jnp.max(t, axis=1, keepdims=True) for t in mx]

        def diag_logit():
            return [jnp.sum(qs[a].astype(F32) * kd.astype(F32), axis=1, keepdims=True) for a, kd in enumerate((k0, k1))]

        def finish(ms):
            qm = [put3(qs[a], L_MAX, -ms[a]) for a in range(2)]
            acc_sc[...] = jnp.zeros(acc_sc.shape, F32)
            for a in range(2):
                def acc_body(j, carry, a=a):
                    off = pl.multiple_of(j * sub, sub)
                    acc_sc[a] += _nn(jnp.exp(_nt(qm[a], k_ref[pl.ds(off, sub), :])).astype(BF), v_ref[pl.ds(off, sub), :])
                    return carry

                lax.fori_loop(starts[a], 2 * i, acc_body, 0)

            def pv(qa, kk, vv, masked):
                p = jnp.exp(_nt(qa, kk))
                if masked:
                    p = jnp.where(causal, p, 0.0)
                return _nn(p.astype(BF), vv)

            accs = [acc_sc[0] + pv(qm[0], k0, v0, True),
                    acc_sc[1] + pv(qm[1], k0, v0, False) + pv(qm[1], k1, v1, True)]
            for a in range(2):
                rows = slice(a * sub, (a + 1) * sub)
                l = accs[a][:, 64:65]
                o_ref[rows, :] = jnp.where(lane < 64, accs[a] / l, 0.0)
                q2_ref[rows, :] = put3(qs[a], L_LSE, -(ms[a] + jnp.log(l)))

        tame = tame_ref[hd] == 1

        @pl.when(tame)
        def _():
            finish(diag_logit())

        @pl.when(jnp.logical_not(tame))
        def _():
            finish(row_max())

    blk = pl.BlockSpec((None, tb, LANE), lambda h, i, js, tm_: (h, i, 0))
    full = pl.BlockSpec((None, T, LANE), lambda h, i, js, tm_: (h, 0, 0))
    return pl.pallas_call(
        body, name="fox_fwd",
        grid_spec=pltpu.PrefetchScalarGridSpec(
            num_scalar_prefetch=2, grid=(H, T // tb), in_specs=[blk, full, full], out_specs=[blk, blk],
            scratch_shapes=[pltpu.VMEM((2, sub, LANE), F32), pltpu.VMEM((2, sub, LANE), F32)]),
        out_shape=[jax.ShapeDtypeStruct((H, T, LANE), F32), jax.ShapeDtypeStruct((H, T, LANE), BF)],
        compiler_params=_params(("parallel", "arbitrary"), VMEM_BIG),
    )(jlo, tame, q, k, v)


def _merge_out(u_r, o_fox, z_a, x, g_ffn, w_ro, w_fo, w_out, tm=256, push=None):
    T = x.shape[0]

    def body(u_ref, of_ref, ar_ref, af_ref, x_ref, g_ref, wro_ref, wfo_ref, wout_ref,
             yr_ref, yf_ref, m_ref, x2_ref, h2_ref, oc_ref):
        u = u_ref[...]
        oc = jnp.concatenate([of_ref[h][:, :FOX_D] for h in range(FOX_H)], axis=-1).astype(BF)
        oc_ref[...] = oc
        yr = jnp.concatenate([_nn(u, wro_ref[k]) for k in range(N_CHIP)], axis=-1)
        yf = jnp.concatenate([_nn(oc, wfo_ref[k]) for k in range(N_CHIP)], axis=-1)
        yr_ref[...] = yr.astype(BF)
        yf_ref[...] = yf.astype(BF)
        m = (_sigmoid(ar_ref[...].astype(F32)) * yr + _sigmoid(af_ref[...].astype(F32)) * yf).astype(BF)
        m_ref[...] = m
        x2 = x_ref[...]
        for k in range(N_CHIP):
            x2 = x2 + _nn(m[:, 256 * k:256 * k + 256], wout_ref[k])
        x2_ref[...] = x2
        r = lax.rsqrt(jnp.mean(x2 * x2, axis=-1, keepdims=True) + EPS)
        h2_ref[...] = (x2 * r * g_ref[...]).astype(BF)

    row = lambda w: pl.BlockSpec((tm, w), lambda i: (i, 0))
    const = lambda shp: pl.BlockSpec(shp, lambda i: (0,) * len(shp))
    return _hosted_call(
        body, "merge_out", (T // tm,),
        [row(512), pl.BlockSpec((FOX_H, tm, LANE), lambda i: (0, i, 0)),
         pl.BlockSpec((tm, 1024), lambda i: (i, 3)), pl.BlockSpec((tm, 1024), lambda i: (i, 4)),
         row(1024), const((1, 1024)), const((N_CHIP, 512, 256)), const((N_CHIP, 512, 256)),
         const((N_CHIP, 256, 1024))],
        [row(1024), row(1024), row(1024), row(1024), row(1024), row(512)],
        [jax.ShapeDtypeStruct((T, 1024), BF), jax.ShapeDtypeStruct((T, 1024), BF),
         jax.ShapeDtypeStruct((T, 1024), BF), jax.ShapeDtypeStruct((T, 1024), F32),
         jax.ShapeDtypeStruct((T, 1024), BF), jax.ShapeDtypeStruct((T, 512), BF)],
        [], VMEM_BIG, (u_r, o_fox, z_a, z_a, x, g_ffn, w_ro, w_fo, w_out), push)


def _load_resident(hbm_refs, vmem_refs, sem):
    cps = [pltpu.make_async_copy(h, v, sem.at[i]) for i, (h, v) in enumerate(zip(hbm_refs, vmem_refs))]
    for cp in cps:
        cp.start()
    for cp in cps:
        cp.wait()


def _ffn_fwd(h2, x2, tgt, w_gate, w_up, w_down, tm=FFN_TM):
    T = h2.shape[0]

    def body(h_ref, x2_ref, t_ref, wg_hbm, wu_hbm, wd_hbm, a_ref, b_ref, act_ref, dy_ref, ls_ref, wg, wu, wd, sem):
        @pl.when(pl.program_id(0) == 0)
        def _():
            _load_resident((wg_hbm, wu_hbm, wd_hbm), (wg, wu, wd), sem)
            ls_ref[...] = jnp.zeros(ls_ref.shape, F32)

        h = h_ref[...]
        err = x2_ref[...] - t_ref[...]
        for k in range(N_CHIP):
            gp = _nt(h, wg[k])
            up = _nt(h, wu[k])
            sg = _sigmoid(gp)
            silu = gp * sg
            a_ref[k] = silu.astype(BF)
            b_ref[k] = (up * sg * (1.0 + gp * (1.0 - sg))).astype(BF)
            act = (silu * up).astype(BF)
            act_ref[k] = act
            err = err + _nn(act, wd[k])
        dy_ref[...] = err * (1.0 / D_MODEL)
        ls_ref[...] += jnp.sum(err * err, axis=0, keepdims=True)

    row = pl.BlockSpec((tm, D_MODEL), lambda i: (i, 0))
    hid = pl.BlockSpec((N_CHIP, tm, FF_SH), lambda i: (0, i, 0))
    anyspec = pl.BlockSpec(memory_space=pl.ANY)
    wshape = pltpu.VMEM((N_CHIP, FF_SH, D_MODEL), BF)
    return pl.pallas_call(
        body, name="ffn_fwd", grid=(T // tm,),
        in_specs=[row, row, row, anyspec, anyspec, anyspec],
        out_specs=[hid, hid, hid, row, pl.BlockSpec((1, D_MODEL), lambda i: (0, 0))],
        out_shape=[jax.ShapeDtypeStruct((N_CHIP, T, FF_SH), BF)] * 3
        + [jax.ShapeDtypeStruct((T, D_MODEL), F32), jax.ShapeDtypeStruct((1, D_MODEL), F32)],
        scratch_shapes=[wshape, wshape, wshape, pltpu.SemaphoreType.DMA((3,))],
        compiler_params=_params(("arbitrary",), VMEM_HUGE),
    )(h2, x2, tgt, w_gate, w_up, w_down)


def _ffn_bwd(dy, sa, sb, x2, g_ffn, w_gate, w_up, w_down, tm=FFN_TM):
    T = dy.shape[0]

    def body(dy_ref, a_ref, b_ref, x2_ref, g_ref, wg_hbm, wu_hbm, wd_hbm, dgp_ref, dup_ref, dx_ref, dg_ref,
             wg, wu, wd, sem):
        @pl.when(pl.program_id(0) == 0)
        def _():
            _load_resident((wg_hbm, wu_hbm, wd_hbm), (wg, wu, wd), sem)
            dg_ref[...] = jnp.zeros(dg_ref.shape, F32)

        dy = dy_ref[...]
        dyb = dy.astype(BF)
        dh = jnp.zeros((tm, D_MODEL), F32)
        for k in range(N_CHIP):
            dact = _nt(dyb, wd[k])
            dup = (dact * a_ref[k]).astype(BF)
            dgp = (dact * b_ref[k]).astype(BF)
            dgp_ref[k] = dgp
            dup_ref[k] = dup
            dh = dh + _nn(dgp, wg[k]) + _nn(dup, wu[k])
        x2 = x2_ref[...]
        r = lax.rsqrt(jnp.mean(x2 * x2, axis=-1, keepdims=True) + EPS)
        xn = x2 * r
        dg_ref[...] += jnp.sum(dh * xn, axis=0, keepdims=True)
        dxn = dh * g_ref[...]
        dx_ref[...] = dy + r * (dxn - xn * jnp.mean(dxn * xn, axis=-1, keepdims=True))

    row = pl.BlockSpec((tm, D_MODEL), lambda i: (i, 0))
    hid = pl.BlockSpec((N_CHIP, tm, FF_SH), lambda i: (0, i, 0))
    vec = pl.BlockSpec((1, D_MODEL), lambda i: (0, 0))
    anyspec = pl.BlockSpec(memory_space=pl.ANY)
    wshape = pltpu.VMEM((N_CHIP, FF_SH, D_MODEL), BF)
    return pl.pallas_call(
        body, name="ffn_bwd", grid=(T // tm,),
        in_specs=[row, hid, hid, row, vec, anyspec, anyspec, anyspec],
        out_specs=[hid, hid, row, vec],
        out_shape=[jax.ShapeDtypeStruct((N_CHIP, T, FF_SH), BF), jax.ShapeDtypeStruct((N_CHIP, T, FF_SH), BF),
                   jax.ShapeDtypeStruct((T, D_MODEL), F32), jax.ShapeDtypeStruct((1, D_MODEL), F32)],
        scratch_shapes=[wshape, wshape, wshape, pltpu.SemaphoreType.DMA((3,))],
        compiler_params=_params(("arbitrary",), VMEM_HUGE),
    )(dy, sa, sb, x2, g_ffn, w_gate, w_up, w_down)


def _out_bwd(dx2, z_a, y_r, y_f, o_raw, o_fox, g_ret, w_ro, w_fo, w_out, tm=256, push=None):
    T = dx2.shape[0]

    def body(dx_ref, gt_ref, ar_ref, af_ref, yr_ref, yf_ref, o_ref, of_ref, g_ref, wro_ref, wfo_ref, wout_ref,
             dyr_ref, dyf_ref, dgt_ref, da_ref, do_ref, dof_ref, dg_ref):
        i = pl.program_id(0)

        @pl.when(i == 0)
        def _():
            dg_ref[...] = jnp.zeros(dg_ref.shape, F32)

        dxb = dx_ref[...].astype(BF)
        dm = jnp.concatenate([_nt(dxb, wout_ref[k]) for k in range(N_CHIP)], axis=-1)
        sr, sf = _sigmoid(ar_ref[...].astype(F32)), _sigmoid(af_ref[...].astype(F32))
        dyr = dm * sr
        dyf = dm * sf
        da_ref[:, :1024] = (dyr * yr_ref[...].astype(F32) * (1.0 - sr)).astype(BF)
        da_ref[:, 1024:] = (dyf * yf_ref[...].astype(F32) * (1.0 - sf)).astype(BF)
        dyr = dyr.astype(BF)
        dyf = dyf.astype(BF)
        dyr_ref[...] = dyr
        dyf_ref[...] = dyf
        du = jnp.zeros((tm, 512), F32)
        doc = jnp.zeros((tm, 512), F32)
        for k in range(N_CHIP):
            du = du + _nt(dyr[:, 256 * k:256 * k + 256], wro_ref[k])
            doc = doc + _nt(dyf[:, 256 * k:256 * k + 256], wfo_ref[k])

        for h in range(RET_H):
            cols = slice(h * RET_DV, (h + 1) * RET_DV)
            o = o_ref[:, cols]
            mu = jnp.mean(o, axis=-1, keepdims=True)
            xc = o - mu
            rstd = lax.rsqrt(jnp.mean(xc * xc, axis=-1, keepdims=True) + EPS)
            on = xc * rstd
            g = g_ref[:, cols]
            gt = gt_ref[:, cols].astype(F32)
            sg = _sigmoid(gt)
            duh = du[:, cols]
            dgt_ref[:, cols] = (duh * (on * g) * sg * (1.0 + gt * (1.0 - sg))).astype(BF)
            dog = duh * gt * sg
            dg_ref[:, cols] += jnp.sum(dog * on, axis=0, keepdims=True)
            don = dog * g
            do_ref[:, cols] = rstd * (don - jnp.mean(don, axis=-1, keepdims=True)
                                      - on * jnp.mean(don * on, axis=-1, keepdims=True))

        lane = lax.broadcasted_iota(jnp.int32, (tm, LANE), 1)
        zpad = jnp.zeros((tm, 64), F32)
        for h in range(FOX_H):
            doh = doc[:, 64 * h:64 * h + 64]
            delta = jnp.sum(doh * of_ref[h][:, :FOX_D], axis=-1, keepdims=True)
            hi, mid, lo = [t.astype(F32) for t in _split3(-delta)]
            da = jnp.concatenate([doh, zpad], axis=-1)
            da = jnp.where(lane == 64, hi, jnp.where(lane == 65, mid, jnp.where(lane == 66, lo, da)))
            dof_ref[h] = da.astype(BF)

    row = lambda w: pl.BlockSpec((tm, w), lambda i: (i, 0))
    const = lambda shp: pl.BlockSpec(shp, lambda i: (0,) * len(shp))
    hsp = pl.BlockSpec((FOX_H, tm, LANE), lambda i: (0, i, 0))
    return _hosted_call(
        body, "out_bwd", (T // tm,),
        [row(1024), pl.BlockSpec((tm, 512), lambda i: (i, 2)), pl.BlockSpec((tm, 1024), lambda i: (i, 3)),
         pl.BlockSpec((tm, 1024), lambda i: (i, 4)), row(1024), row(1024), row(512), hsp,
         const((1, 512)), const((N_CHIP, 512, 256)), const((N_CHIP, 512, 256)), const((N_CHIP, 256, 1024))],
        [row(1024), row(1024), row(512), row(2048), row(512), hsp, const((1, 512))],
        [jax.ShapeDtypeStruct((T, 1024), BF), jax.ShapeDtypeStruct((T, 1024), BF),
         jax.ShapeDtypeStruct((T, 512), BF), jax.ShapeDtypeStruct((T, 2048), BF),
         jax.ShapeDtypeStruct((T, 512), F32), jax.ShapeDtypeStruct((FOX_H, T, LANE), BF),
         jax.ShapeDtypeStruct((1, 512), F32)],
        [], VMEM_BIG, (dx2, z_a, z_a, z_a, y_r, y_f, o_raw, o_fox, g_ret, w_ro, w_fo, w_out), push)


def _ret_bwd(d_o, qr, kr, z_a, states, cos_t, sin_t, consts, tt=512, push=None):
    T = z_a.shape[0]
    nt = T // tt
    nch = tt // CHUNK
    decay, zeta, xi, gcb = consts

    def body(do_ref, q_ref, k_ref, v_ref, st_ref, cos_ref, sin_ref, d_ref, ze_ref, xi_ref, gc_ref, dz_ref, g_sc):
        i = pl.program_id(0)

        @pl.when(i == 0)
        def _():
            g_sc[...] = jnp.zeros(g_sc.shape, F32)

        for c in reversed(range(nch)):
            rows = slice(c * CHUNK, (c + 1) * CHUNK)
            cosv, sinv = cos_ref[rows, :], sin_ref[rows, :]
            dq_parts, dk_parts = [], []
            for h in range(RET_H):
                cols = slice(h * RET_DV, (h + 1) * RET_DV)
                q, k = q_ref[h, rows, :], k_ref[h, rows, :]
                v32 = v_ref[rows, cols].astype(F32)
                vb = v32.astype(BF)
                r = st_ref[h, c * CHUNK:c * CHUNK + LANE, :]
                g = g_sc[h]
                gb = g.astype(BF)
                d_o = do_ref[rows, cols]
                dob = d_o.astype(BF)
                dox = (d_o * xi_ref[h]).astype(BF)
                dec = d_ref[h]
                s = (_nt(q, k) * dec).astype(BF)
                ds = (_nt(dob, vb) * dec).astype(BF)
                dv = _tn(s, dob) + ze_ref[h] * _nn(k, gb)
                dq = _nn(ds, k) + _nt(dox, r.astype(BF))
                dk = _tn(ds, q) + _nt((v32 * ze_ref[h]).astype(BF), gb)
                g_sc[h] = gc_ref[h] * g + _tn(q, dox)
                dq_parts.append((dq * cosv - _swap32(dq) * sinv)[:, :64])
                dk_parts.append(((dk * cosv - _swap32(dk) * sinv) * 0.125)[:, :64])
                dz_ref[rows, 512 + h * RET_DV:512 + (h + 1) * RET_DV] = dv.astype(BF)
            dz_ref[rows, 0:256] = jnp.concatenate(dq_parts, axis=-1).astype(BF)
            dz_ref[rows, 256:512] = jnp.concatenate(dk_parts, axis=-1).astype(BF)

    rev = lambda i: nt - 1 - i
    hspec = pl.BlockSpec((RET_H, tt, LANE), lambda i: (0, rev(i), 0))
    cspec = pl.BlockSpec((RET_H, CHUNK, LANE), lambda i: (0, 0, 0))
    tab = pl.BlockSpec((tt, LANE), lambda i: (rev(i), 0))
    (dz,), lands = _hosted_call(
        body, "ret_bwd", (nt,),
        [pl.BlockSpec((tt, 512), lambda i: (rev(i), 0)), hspec, hspec,
         pl.BlockSpec((tt, 512), lambda i: (rev(i), 1)), hspec, tab, tab,
         pl.BlockSpec((RET_H, CHUNK, CHUNK), lambda i: (0, 0, 0)), cspec, cspec,
         pl.BlockSpec((RET_H, LANE, LANE), lambda i: (0, 0, 0))],
        [pl.BlockSpec((tt, 1024), lambda i: (rev(i), 0))], [jax.ShapeDtypeStruct((T, 1024), BF)],
        [pltpu.VMEM((RET_H, LANE, LANE), F32)], VMEM_BIG,
        (d_o, qr, kr, z_a, states, cos_t, sin_t, decay, zeta, xi, gcb), push)
    return dz, lands


def _fox_bwd(end_both, end_last, q2, k, v, do, sub=FOX_SUB):
    H, T, _ = k.shape
    tb = 2 * sub

    def body(eb_ref, el_ref, q_ref, do_ref, k_ref, v_ref, dq_ref, dk_ref, dv_ref, dk_sc, dv_sc):
        j = pl.program_id(1)
        n_both = eb_ref[pl.program_id(0), j]
        n_last = el_ref[pl.program_id(0), j]

        @pl.when(j == 0)
        def _():
            dq_ref[...] = jnp.zeros(dq_ref.shape, F32)

        dk_sc[...] = jnp.zeros(dk_sc.shape, F32)
        dv_sc[...] = jnp.zeros(dv_sc.shape, F32)
        krow = lax.broadcasted_iota(jnp.int32, (tb, sub), 0)
        qcol = lax.broadcasted_iota(jnp.int32, (tb, sub), 1)

        def step(i, r0, r1, shift):
            off = pl.multiple_of(i * sub, sub)
            qq = q_ref[pl.ds(off, sub), :]
            dd = do_ref[pl.ds(off, sub), :]
            kk, vv = k_ref[r0:r1, :], v_ref[r0:r1, :]
            p = jnp.exp(_nt(kk, qq))
            if shift is not None:
                p = jnp.where(qcol[0:r1 - r0, :] + shift >= krow[0:r1 - r0, :], p, 0.0)
            ds = (p * _nt(vv, dd)).astype(BF)
            dv_sc[r0:r1, :] += _nn(p.astype(BF), dd)
            dk_sc[r0:r1, :] += _nn(ds, qq)
            dq_ref[pl.ds(off, sub), :] += _tn(ds, kk)

        step(2 * j, 0, sub, 0)
        step(2 * j + 1, 0, tb, sub)

        def both_body(i, carry):
            step(i, 0, tb, None)
            return carry

        def last_body(i, carry):
            step(i, sub, tb, None)
            return carry

        lax.fori_loop(2 * j + 2, n_both, both_body, 0)
        lax.fori_loop(n_both, n_last, last_body, 0)
        dk_ref[...] = dk_sc[...]
        dv_ref[...] = dv_sc[...]

    blk = pl.BlockSpec((None, tb, LANE), lambda h, j, eb, el: (h, j, 0))
    full = pl.BlockSpec((None, T, LANE), lambda h, j, eb, el: (h, 0, 0))
    shp = jax.ShapeDtypeStruct((H, T, LANE), F32)
    return pl.pallas_call(
        body, name="fox_bwd",
        grid_spec=pltpu.PrefetchScalarGridSpec(
            num_scalar_prefetch=2, grid=(H, T // tb), in_specs=[full, full, blk, blk], out_specs=[full, blk, blk],
            scratch_shapes=[pltpu.VMEM((tb, LANE), F32), pltpu.VMEM((tb, LANE), F32)]),
        out_shape=[shp, shp, shp],
        compiler_params=_params(("arbitrary", "arbitrary"), VMEM_BIG),
    )(end_both, end_last, q2, do, k, v)


def _fox_post_bwd(dq, dk, dv, z_a, z_ff, b_f, g_q, g_k, tm=256, push=None):
    T = z_a.shape[0]
    nt = T // tm

    def body(dq_ref, dk_ref, dv_ref, zf_ref, zff_ref, b_ref, g_ref, sc_ref, seg_ref, segt_ref,
             dz_ref, dff_ref, dg_ref, db_ref, carry):
        i = pl.program_id(0)

        @pl.when(i == 0)
        def _():
            carry[...] = jnp.zeros(carry.shape, F32)
            dg_ref[...] = jnp.zeros(dg_ref.shape, F32)
            db_ref[...] = jnp.zeros(db_ref.shape, F32)

        lane = lax.broadcasted_iota(jnp.int32, (tm, LANE), 1)
        dcm = jnp.zeros((tm, LANE), F32)
        for h in range(FOX_H):
            dcm = jnp.where(lane == h, dq_ref[h][:, L_CQ:L_CQ + 1] - dk_ref[h][:, L_CK:L_CK + 1], dcm)

        def seg_mean(v):
            return sum(_nn(t, seg_ref[...]) for t in _split3(v)) * (1.0 / FOX_D)

        def seg_bcast(v):
            return sum(_nn(t, segt_ref[...]) for t in _split3(v))

        x = zf_ref[:, :1024].astype(F32)
        dy = jnp.concatenate([dq_ref[h][:, :FOX_D] for h in range(FOX_H)]
                             + [dk_ref[h][:, :FOX_D] for h in range(FOX_H)], axis=-1) * sc_ref[...]
        rb = seg_bcast(lax.rsqrt(seg_mean(x * x) + EPS))
        xn = x * rb
        dg_ref[...] += jnp.sum(dy * xn, axis=0, keepdims=True)
        dxn = dy * g_ref[...]
        dz_ref[:, :1024] = (rb * (dxn - xn * seg_bcast(seg_mean(dxn * xn)))).astype(BF)
        dz_ref[:, 1024:] = jnp.concatenate([dv_ref[h][:, :FOX_D] for h in range(FOX_H)], axis=-1).astype(BF)

        row = lax.broadcasted_iota(jnp.int32, (tm, tm), 0)
        col = lax.broadcasted_iota(jnp.int32, (tm, tm), 1)
        tri = (row <= col).astype(BF)
        hi, mid, lo = _split3(dcm)
        dlogf = _nn(tri, hi) + _nn(tri, mid) + _nn(tri, lo) + carry[...]
        carry[...] = dlogf[0:1, :]
        dff = jnp.where(lane < FOX_H, dlogf * _sigmoid(-(zff_ref[...] + b_ref[...])), 0.0)
        dff_ref[...] = dff.astype(BF)
        db_ref[...] += jnp.sum(dff, axis=0, keepdims=True)

    rev = lambda i: nt - 1 - i
    hsp = pl.BlockSpec((FOX_H, tm, LANE), lambda i: (0, rev(i), 0))
    const = lambda r, w: pl.BlockSpec((r, w), lambda i: (0, 0))
    seg = _segment_matrix()
    g_all = jnp.concatenate([jnp.tile(g_q, (1, FOX_H)), jnp.tile(g_k, (1, FOX_H))], axis=1)
    scale = jnp.asarray(np.concatenate([np.full((1, 512), 0.125, np.float32), np.ones((1, 512), np.float32)], axis=1))
    (dz, dff, dg, db), lands = _hosted_call(
        body, "fox_post_bwd", (nt,),
        [hsp, hsp, hsp, pl.BlockSpec((tm, 1536), lambda i: (rev(i), 1)),
         pl.BlockSpec((tm, LANE), lambda i: (rev(i), 0)), const(1, LANE), const(1, 1024), const(1, 1024),
         const(1024, LANE), const(LANE, 1024)],
        [pl.BlockSpec((tm, 1536), lambda i: (rev(i), 0)), pl.BlockSpec((tm, LANE), lambda i: (rev(i), 0)),
         const(1, 1024), const(1, LANE)],
        [jax.ShapeDtypeStruct((T, 1536), BF), jax.ShapeDtypeStruct((T, LANE), BF),
         jax.ShapeDtypeStruct((1, 1024), F32), jax.ShapeDtypeStruct((1, LANE), F32)],
        [pltpu.VMEM((1, LANE), F32)], VMEM_BIG, (dq, dk, dv, z_a, z_ff, b_f, g_all, scale, seg, seg.T), push)
    dg_heads = dg.reshape(2, FOX_H, FOX_D).sum(axis=1)
    return (dz, dff, dg_heads[0:1], dg_heads[1:2], db), lands


def _in_bwd(dz_ret, dz_gt, dz_fox, dz_a, dz_ff, w_a, w_ff, x, g_mix, dx2, tm=256, push=None):
    T = x.shape[0]

    def body(r_ref, t_ref, f_ref, a_ref, ff_ref, wa_ref, wf_ref, x_ref, g_ref, dx2_ref, dx_ref, dg_ref):
        i = pl.program_id(0)

        @pl.when(i == 0)
        def _():
            dg_ref[...] = jnp.zeros(dg_ref.shape, F32)

        dh = (_nt(r_ref[...], wa_ref[:, C_RET:C_GT]) + _nt(t_ref[...], wa_ref[:, C_GT:C_FOX])
              + _nt(f_ref[...], wa_ref[:, C_FOX:C_A]) + _nt(a_ref[...], wa_ref[:, C_A:C_END])
              + _nt(ff_ref[...], wf_ref[...]))
        xv = x_ref[...]
        r = lax.rsqrt(jnp.mean(xv * xv, axis=-1, keepdims=True) + EPS)
        xn = xv * r
        dg_ref[...] += jnp.sum(dh * xn, axis=0, keepdims=True)
        dxn = dh * g_ref[...]
        dx_ref[...] = dx2_ref[...] + r * (dxn - xn * jnp.mean(dxn * xn, axis=-1, keepdims=True))

    row = lambda w: pl.BlockSpec((tm, w), lambda i: (i, 0))
    const = lambda shp: pl.BlockSpec(shp, lambda i: (0,) * len(shp))
    return _hosted_call(
        body, "in_bwd", (T // tm,),
        [row(1024), row(512), row(1536), row(2048), row(LANE), const((D_MODEL, C_END)),
         const((D_MODEL, LANE)), row(1024), const((1, 1024)), row(1024)],
        [row(1024), const((1, 1024))],
        [jax.ShapeDtypeStruct((T, 1024), F32), jax.ShapeDtypeStruct((1, 1024), F32)],
        [], VMEM_BIG, (dz_ret, dz_gt, dz_fox, dz_a, dz_ff, w_a, w_ff, x, g_mix, dx2), push)


def _mesh_pos():
    return lax.axis_index("x"), lax.axis_index("y"), lax.axis_index("c")


def _staged_place(src, name):
    stacked = src.ndim == 3
    R, C = src.shape[-2:]
    tr = _row_tile(R, 128, 16)
    n = R // tr
    assert n >= 2

    def body(s_ref, o_ref, buf, sem):
        i = pl.program_id(0)
        slot = i % 2
        x, y, _ = _mesh_pos()
        kme = 2 * x + y

        def out_copy(s, step):
            return pltpu.make_async_copy(buf.at[s], o_ref.at[kme, pl.ds(pl.multiple_of(step * tr, tr), tr), :], sem.at[s])

        @pl.when(i >= 2)
        def _():
            out_copy(slot, i - 2).wait()

        buf[slot] = (s_ref[kme] if stacked else s_ref[...]).astype(BF)
        out_copy(slot, i).start()

        @pl.when(i == n - 1)
        def _():
            out_copy(1 - slot, i - 1).wait()
            out_copy(slot, i).wait()

    in_spec = (pl.BlockSpec((N_CHIP, tr, C), lambda i: (0, i, 0)) if stacked else pl.BlockSpec((tr, C), lambda i: (i, 0)))
    return pl.pallas_call(
        body, name=name, grid=(n,), in_specs=[in_spec], out_specs=pl.BlockSpec(memory_space=pl.ANY),
        out_shape=jax.ShapeDtypeStruct((N_CHIP, R, C), BF),
        scratch_shapes=[pltpu.VMEM((2, tr, C), BF), pltpu.SemaphoreType.DMA((2,))],
        compiler_params=_params(("arbitrary",)),
    )(src)


def _push_copies(src, land, send_sem, recv_sem, receiving):
    x, y, c = _mesh_pos()
    kme = 2 * x + y
    cps = []
    for w in range(len(land)):
        for j, (px, py) in enumerate([(1 - x, y), (x, 1 - y), (1 - x, 1 - y)]):
            kpeer = 2 * px + py
            cps.append(pltpu.make_async_remote_copy(
                src_ref=land[w].at[kme] if src is None else src[w].at[kpeer],
                dst_ref=land[w].at[kpeer if receiving else kme],
                send_sem=send_sem.at[3 * w + j], recv_sem=recv_sem.at[3 * w + j],
                device_id=(px, py, c), device_id_type=MESH))
    return cps


def _gather_two_level(stack, name):
    _, R, C = stack.shape
    hr = R // 2

    def body(_, land, send_sem, recv_sem):
        x, y, c = _mesh_pos()
        kme = 2 * x + y
        chips = [(1 - x, y), (x, 1 - y), (1 - x, 1 - y)]

        def rows(k, core):
            return land.at[k, pl.ds(pl.multiple_of(core * hr, hr), hr), :]

        def copy(idx, k, core, to):
            return pltpu.make_async_remote_copy(src_ref=rows(k, core), dst_ref=rows(k, core), send_sem=send_sem.at[idx],
                                                recv_sem=recv_sem.at[idx], device_id=to, device_id_type=MESH)

        first = [copy(j, kme, c, (px, py, c)) for j, (px, py) in enumerate(chips)]
        for cp in first:
            cp.start()
        passed = [copy(3 + j, 2 * px + py, c, (x, y, 1 - c)) for j, (px, py) in enumerate(chips)]
        for j, (px, py) in enumerate(chips):
            copy(j, 2 * px + py, c, (px, py, c)).wait_recv()
            passed[j].start()
        for j, (px, py) in enumerate(chips):
            copy(3 + j, 2 * px + py, 1 - c, (x, y, 1 - c)).wait_recv()
        for cp in first + passed:
            cp.wait_send()

    anyspec = pl.BlockSpec(memory_space=pl.ANY)
    return pl.pallas_call(
        body, name=name, in_specs=[anyspec], out_specs=anyspec,
        out_shape=jax.ShapeDtypeStruct(stack.shape, stack.dtype), input_output_aliases={0: 0},
        scratch_shapes=[pltpu.SemaphoreType.DMA((6,)), pltpu.SemaphoreType.DMA((6,))],
    )(stack)


def _gather_small(small):
    def body(sv, svo, ssend, srecv, sloc):
        x, y, c = _mesh_pos()
        me = 4 * x + 2 * y + c
        flips = [(b >> 2 & 1, b >> 1 & 1, b & 1) for b in range(1, 8)]
        others = [(1 - x if fx else x, 1 - y if fy else y, 1 - c if fc else c) for fx, fy, fc in flips]
        local = pltpu.make_async_copy(sv, svo.at[me], sloc)
        local.start()
        sends = []
        for j, (px, py, pc) in enumerate(others):
            cp = pltpu.make_async_remote_copy(
                src_ref=sv, dst_ref=svo.at[me], send_sem=ssend.at[j], recv_sem=srecv.at[j],
                device_id=(px, py, pc), device_id_type=MESH)
            cp.start()
            sends.append(cp)
        for j, (px, py, pc) in enumerate(others):
            pltpu.make_async_remote_copy(
                src_ref=sv, dst_ref=svo.at[4 * px + 2 * py + pc], send_sem=ssend.at[j], recv_sem=srecv.at[j],
                device_id=(px, py, pc), device_id_type=MESH).wait_recv()
        for cp in sends:
            cp.wait_send()
        local.wait()

    anyspec = pl.BlockSpec(memory_space=pl.ANY)
    return pl.pallas_call(
        body, name="gather_small", in_specs=[anyspec], out_specs=anyspec,
        out_shape=jax.ShapeDtypeStruct((8,) + small.shape, small.dtype),
        scratch_shapes=[pltpu.SemaphoreType.DMA((7,)), pltpu.SemaphoreType.DMA((7,)), pltpu.SemaphoreType.DMA],
    )(small)


def _sibling_exchange(arrs):
    n = len(arrs)

    def body(*refs):
        ins, outs = refs[:n], refs[n:2 * n]
        send_sems, recv_sems = refs[2 * n:]
        x, y, c = _mesh_pos()
        cps = [pltpu.make_async_remote_copy(
            src_ref=ins[w], dst_ref=outs[w], send_sem=send_sems.at[w], recv_sem=recv_sems.at[w],
            device_id=(x, y, 1 - c), device_id_type=MESH) for w in range(n)]
        for cp in cps:
            cp.start()
        for cp in cps:
            cp.wait_recv()
        for cp in cps:
            cp.wait_send()

    anyspec = pl.BlockSpec(memory_space=pl.ANY)
    return pl.pallas_call(
        body, name="sibling_exchange",
        in_specs=[anyspec] * n, out_specs=[anyspec] * n,
        out_shape=[jax.ShapeDtypeStruct(a.shape, a.dtype) for a in arrs],
        scratch_shapes=[pltpu.SemaphoreType.DMA((n,)), pltpu.SemaphoreType.DMA((n,))],
    )(*arrs)


def _sum_stack(own, recv, name):
    _, R, C = recv.shape
    tr = _row_tile(R, 256, 16)

    def body(g_ref, r_ref, o_ref):
        x, y, _ = _mesh_pos()
        kme = 2 * x + y
        acc = g_ref[kme].astype(F32)
        for d in range(1, N_CHIP):
            acc = acc + r_ref[(kme + d) % N_CHIP].astype(F32)
        o_ref[...] = acc

    spec = pl.BlockSpec((N_CHIP, tr, C), lambda i: (0, i, 0))
    return pl.pallas_call(
        body, name=name, grid=(R // tr,), in_specs=[spec, spec],
        out_specs=pl.BlockSpec((tr, C), lambda i: (i, 0)),
        out_shape=jax.ShapeDtypeStruct((R, C), F32),
        compiler_params=_params(("parallel",)),
    )(own, recv)


def _adam_math(w, g, m, v):
    m2 = ADAM_B1 * m + (1.0 - ADAM_B1) * g
    v2 = ADAM_B2 * v + (1.0 - ADAM_B2) * (g * g)
    m_hat = m2 / (1.0 - ADAM_B1 ** ADAM_STEP)
    v_hat = v2 / (1.0 - ADAM_B2 ** ADAM_STEP)
    delta = -ADAM_LR * (m_hat / (jnp.sqrt(v_hat) + ADAM_EPS) + ADAM_WD * w)
    return delta, m2, v2


def _adamw(w, m, v, s0, s1, name):
    R, C = w.shape
    tr = _row_tile(R, 128, 8)

    def body(w_ref, m_ref, v_ref, a_ref, b_ref, g_ref, d_ref, m2_ref, v2_ref):
        g = a_ref[...] + b_ref[...]
        delta, m2, v2 = _adam_math(w_ref[...], g, m_ref[...], v_ref[...])
        g_ref[...] = g
        d_ref[...] = delta
        m2_ref[...] = m2
        v2_ref[...] = v2

    spec = pl.BlockSpec((tr, C), lambda i: (i, 0))
    shp = jax.ShapeDtypeStruct((R, C), F32)
    return pl.pallas_call(
        body, name=name, grid=(R // tr,), in_specs=[spec] * 5, out_specs=[spec] * 4, out_shape=[shp] * 4,
        compiler_params=_params(("parallel",), VMEM_BIG),
    )(w, m, v, s0, s1)


def _adamw_small(ws, ms, vs, gathered):
    n = len(SMALL)

    def body(*refs):
        w_refs, m_refs, v_refs, s_ref = refs[:n], refs[n:2 * n], refs[2 * n:3 * n], refs[3 * n]
        outs = refs[3 * n + 1:]
        g_all = s_ref[0]
        for d in range(1, 8):
            g_all = g_all + s_ref[d]
        off = 0
        for i, (_, width) in enumerate(SMALL):
            g = g_all[:, off:off + width]
            delta, m2, v2 = _adam_math(w_refs[i][...], g, m_refs[i][...], v_refs[i][...])
            for kind, val in enumerate((g, delta, m2, v2)):
                outs[kind * n + i][...] = val
            off += width + (-width % LANE)

    shapes = [jax.ShapeDtypeStruct((1, width), F32) for _, width in SMALL]
    res = pl.pallas_call(body, name="adamw_small", out_shape=shapes * 4)(*ws, *ms, *vs, gathered)
    return [dict(zip([nm for nm, _ in SMALL], res[kind * n:(kind + 1) * n])) for kind in range(4)]


SMALL = (("g_mix", 1024), ("g_ffn", 1024), ("g_ret_norm", 512), ("g_fox_q", 64), ("g_fox_k", 64), ("b_forget", 8))
SMALL_W = 3072


def _pack_small(parts):
    cols = []
    for (name, n) in SMALL:
        p = parts[name].reshape(1, -1)[:, :n]
        pad = -n % LANE
        cols.append(jnp.pad(p, ((0, 0), (0, pad))) if pad else p)
    used = sum(c.shape[1] for c in cols)
    cols.append(jnp.zeros((1, SMALL_W - used), F32))
    return jnp.concatenate(cols, axis=1)


def kernel(x, g_mix, w_in, b_forget, g_ret_norm, w_ret_o, g_fox_q, g_fox_k, w_fox_o, w_out, g_ffn, w_gate, w_up, w_down, loss_target, m_g_mix, m_w_in, m_b_forget, m_g_ret_norm, m_w_ret_o, m_g_fox_q, m_g_fox_k, m_w_fox_o, m_w_out, m_g_ffn, m_w_gate, m_w_up, m_w_down, v_g_mix, v_w_in, v_b_forget, v_g_ret_norm, v_w_ret_o, v_g_fox_q, v_g_fox_k, v_w_fox_o, v_w_out, v_g_ffn, v_w_gate, v_w_up, v_w_down):
    T = x.shape[1]
    xs = x[0]
    tgt = loss_target[0]
    big_names = ("w_in", "w_ret_o", "w_fox_o", "w_out", "w_gate", "w_up", "w_down")
    tr = lambda a: jnp.swapaxes(a[0], 0, 1)
    big_w = dict(w_in=w_in[0], w_ret_o=w_ret_o[0], w_fox_o=w_fox_o[0], w_out=w_out[0], w_gate=tr(w_gate),
                 w_up=tr(w_up), w_down=w_down[0])
    big_m = dict(w_in=m_w_in[0], w_ret_o=m_w_ret_o[0], w_fox_o=m_w_fox_o[0], w_out=m_w_out[0], w_gate=tr(m_w_gate),
                 w_up=tr(m_w_up), w_down=m_w_down[0])
    big_v = dict(w_in=v_w_in[0], w_ret_o=v_w_ret_o[0], w_fox_o=v_w_fox_o[0], w_out=v_w_out[0], w_gate=tr(v_w_gate),
                 w_up=tr(v_w_up), w_down=v_w_down[0])
    small_w = dict(g_mix=g_mix, g_ffn=g_ffn, g_ret_norm=g_ret_norm, g_fox_q=g_fox_q, g_fox_k=g_fox_k, b_forget=b_forget)
    small_m = dict(g_mix=m_g_mix, g_ffn=m_g_ffn, g_ret_norm=m_g_ret_norm, g_fox_q=m_g_fox_q, g_fox_k=m_g_fox_k,
                   b_forget=m_b_forget)
    small_v = dict(g_mix=v_g_mix, g_ffn=v_g_ffn, g_ret_norm=v_g_ret_norm, g_fox_q=v_g_fox_q, g_fox_k=v_g_fox_k,
                   b_forget=v_b_forget)

    stacks = {n: _staged_place(big_w[n], "place_" + n) for n in big_names}
    s_in = _gather_two_level(stacks["w_in"], "gather_w_in")
    w_a, w_ff = _assemble_w_in(s_in)
    b_pad = jnp.pad(b_forget, ((0, 0), (0, LANE - FOX_H)))
    cos_t, sin_t = _rope_tables(T)
    consts = _ret_consts()

    h = _rms_cast(xs, g_mix)
    z_a, (s_ro, s_fo, s_out, s_gate) = _mm_nn(
        h, w_a, "proj_in", BF, tm=1024,
        push=(None, [stacks["w_ret_o"], stacks["w_fox_o"], stacks["w_out"], stacks["w_gate"]]))
    (qr, kr, qf, kf, vf, c_cum, nmax, z_ff), (s_up,) = _mix_prep(
        z_a, h, w_ff, cos_t, sin_t, b_pad, g_fox_q, g_fox_k, push=(None, [stacks["w_up"]]))
    jlo, end_both, end_last, tame = _prune_tables(c_cum, nmax, FOX_SUB)
    o_raw, u_r, states = _ret_fwd(qr, kr, z_a, g_ret_norm, consts)
    o_fox, q2 = _fox_fwd(jlo, tame, qf, kf, vf)
    (y_r, y_f, mrg, x2, h2, o_cat), (s_down,) = _merge_out(u_r, o_fox, z_a, xs, g_ffn, s_ro, s_fo, s_out,
                                                            push=(None, [stacks["w_down"]]))
    sa, sb, act, dy, loss_vec = _ffn_fwd(h2, x2, tgt, s_gate, s_up, s_down)
    loss = lax.psum(0.5 / D_MODEL * jnp.sum(loss_vec), ("x", "y", "c"))

    def scatter_job(grads):
        return (grads, [lax.empty(g.shape, g.dtype) for g in grads])

    dgp, dup, dx2, dg_ffn = _ffn_bwd(dy, sa, sb, x2, g_ffn, s_gate, s_up, s_down)
    (g_gate, _), (g_up, _), (g_down, _) = (_grad_astack(dgp, h2, "gw_gate"), _grad_astack(dup, h2, "gw_up"),
                                           _grad_astack(act, dy, "gw_down"))
    (d_yr, d_yf, dz_gt, dz_a, d_o, do_fox, dg_ret), (r_gate, r_up) = _out_bwd(
        dx2, z_a, y_r, y_f, o_raw, o_fox, g_ret_norm, s_ro, s_fo, s_out, push=scatter_job([g_gate, g_up]))
    dz_ret, (r_down,) = _ret_bwd(d_o, qr, kr, z_a, states, cos_t, sin_t, consts, push=scatter_job([g_down]))
    dq_f, dk_f, dv_f = _fox_bwd(end_both, end_last, q2, kf, vf, do_fox)
    g_mid = [_grad_colstack(u_r, d_yr, "gw_ret_o", 256), _grad_colstack(o_cat, d_yf, "gw_fox_o", 256),
             _grad_plain(mrg, dx2, "gw_out", BF).reshape(N_CHIP, 256, D_MODEL)]
    (dz_fox, dz_ff, dg_q, dg_k, db_f), (r_ro, r_fo, r_out) = _fox_post_bwd(
        dq_f, dk_f, dv_f, z_a, z_ff, b_pad, g_fox_q, g_fox_k, push=scatter_job(g_mid))
    g_in = _pack_g_in(_grad_plain(h, dz_ret, "gw_in_ret", F32), _grad_plain(h, dz_gt, "gw_in_gt", F32),
                      _grad_plain(h, dz_fox, "gw_in_fox", F32, tn=1536),
                      _grad_plain(h, dz_a, "gw_in_a", F32, tk=1024, tn=2048),
                      _grad_plain(h, dz_ff, "gw_in_ff", F32))
    (grad_x, dg_mix), (r_in,) = _in_bwd(dz_ret, dz_gt, dz_fox, dz_a, dz_ff, w_a, w_ff, xs, g_mix, dx2,
                                        push=scatter_job([g_in]))
    small_g = _pack_small(dict(g_mix=dg_mix, g_ffn=dg_ffn, g_ret_norm=dg_ret, g_fox_q=dg_q, g_fox_k=dg_k, b_forget=db_f))

    small_all = _gather_small(small_g)
    sums = [_sum_stack(g, r, "sum_" + n) for g, r, n in zip(
        [g_in] + g_mid + [g_gate, g_up, g_down], [r_in, r_ro, r_fo, r_out, r_gate, r_up, r_down], big_names)]
    sib = _sibling_exchange(sums)
    big_out = {n: _adamw(big_w[n], big_m[n], big_v[n], sums[i], sib[i], "adamw_" + n) for i, n in enumerate(big_names)}
    small_out = _adamw_small(*[[d[nm] for nm, _ in SMALL] for d in (small_w, small_m, small_v)], small_all)

    order = ("g_mix", "w_in", "b_forget", "g_ret_norm", "w_ret_o", "g_fox_q", "g_fox_k", "w_fox_o", "w_out", "g_ffn",
             "w_gate", "w_up", "w_down")
    outs = [loss, grad_x[None]]
    for idx in range(4):
        for n in order:
            if n in ("w_gate", "w_up"):
                outs.append(jnp.swapaxes(big_out[n][idx], 0, 1)[None])
            else:
                outs.append(big_out[n][idx][None] if n in big_out else small_out[idx][n])
    return tuple(outs)
```

```python
import functools

import numpy as np
import jax
import jax.numpy as jnp
from jax import lax
from jax.experimental import pallas as pl
from jax.experimental.pallas import tpu as pltpu

F32 = jnp.float32
BF = jnp.bfloat16
MESH = pl.DeviceIdType.MESH

D_MODEL = 1024
D_FF = 2816
N_CHIP = 4
FF_SH = D_FF // N_CHIP
IN_COLS = 5128
IN_SH = IN_COLS // N_CHIP
RET_H, RET_DV = 4, 128
FOX_H, FOX_D = 8, 64
CHUNK = 256
EPS = 1e-6
NEG = -1e30
LANE = 128
C_RET, C_GT, C_FOX, C_A, C_END = 0, 1024, 1536, 3072, 5120
L_CQ, L_CK, L_LSE, L_MAX = 64, 67, 70, 73

ADAM_LR, ADAM_B1, ADAM_B2, ADAM_EPS, ADAM_WD, ADAM_STEP = 0.001, 0.9, 0.999, 1e-08, 0.01, 10
VMEM_BIG = 56 * 1024 * 1024
VMEM_HUGE = 60 * 1024 * 1024
GRAD_TK = 2048
FFN_TM = 512
FOX_SUB = 512


def _nn(a, b):
    return lax.dot_general(a, b, (((1,), (0,)), ((), ())), preferred_element_type=F32)


def _nt(a, b):
    return lax.dot_general(a, b, (((1,), (1,)), ((), ())), preferred_element_type=F32)


def _tn(a, b):
    return lax.dot_general(a, b, (((0,), (0,)), ((), ())), preferred_element_type=F32)


def _split3(x):
    hi = x.astype(BF)
    r = x - hi.astype(F32)
    mid = r.astype(BF)
    lo = (r - mid.astype(F32)).astype(BF)
    return hi, mid, lo


def _sigmoid(x):
    return 0.5 * jnp.tanh(0.5 * x) + 0.5


def _swap32(x):
    lane = lax.broadcasted_iota(jnp.int32, x.shape, 1)
    return jnp.where(lane < 32, pltpu.roll(x, 96, 1), pltpu.roll(x, 32, 1))


def _params(sem, vmem=None):
    return pltpu.CompilerParams(dimension_semantics=sem, vmem_limit_bytes=vmem)


def _row_tile(rows, cap, mult):
    return max(d for d in range(mult, cap + 1, mult) if rows % d == 0)


def _assemble_w_in(stack, tr=256):
    def body(s_ref, a_ref, f_ref):
        full = jnp.concatenate([s_ref[k].astype(F32) for k in range(N_CHIP)], axis=-1)
        a_ref[...] = jnp.concatenate([full[:, :3072], full[:, 3080:IN_COLS]], axis=-1).astype(BF)
        f_ref[...] = jnp.concatenate([full[:, 3072:3080], jnp.zeros((tr, LANE - FOX_H), F32)], axis=-1).astype(BF)

    return pl.pallas_call(
        body, name="assemble_w_in", grid=(D_MODEL // tr,),
        in_specs=[pl.BlockSpec((N_CHIP, tr, IN_SH), lambda i: (0, i, 0))],
        out_specs=[pl.BlockSpec((tr, C_END), lambda i: (i, 0)), pl.BlockSpec((tr, LANE), lambda i: (i, 0))],
        out_shape=[jax.ShapeDtypeStruct((D_MODEL, C_END), BF), jax.ShapeDtypeStruct((D_MODEL, LANE), BF)],
        compiler_params=_params(("parallel",), VMEM_BIG),
    )(stack)


def _pack_g_in(g_ret, g_gt, g_fox, g_a, g_ff, tr=256):
    def body(r_ref, t_ref, x_ref, a_ref, f_ref, o_ref):
        r, t, x, a, f = [ref[...].astype(F32) for ref in (r_ref, t_ref, x_ref, a_ref, f_ref)]
        full = jnp.concatenate([r, t, x, f[:, :FOX_H], a], axis=-1)
        for k in range(N_CHIP):
            o_ref[k] = full[:, k * IN_SH:(k + 1) * IN_SH].astype(BF)

    def spec(w):
        return pl.BlockSpec((tr, w), lambda i: (i, 0))

    return pl.pallas_call(
        body, name="pack_g_in", grid=(D_MODEL // tr,),
        in_specs=[spec(1024), spec(512), spec(1536), spec(2048), spec(LANE)],
        out_specs=pl.BlockSpec((N_CHIP, tr, IN_SH), lambda i: (0, i, 0)),
        out_shape=jax.ShapeDtypeStruct((N_CHIP, D_MODEL, IN_SH), BF),
        compiler_params=_params(("parallel",), VMEM_BIG),
    )(g_ret, g_gt, g_fox, g_a, g_ff)


def _rms_cast(x, g, tm=512):
    T = x.shape[0]

    def body(x_ref, g_ref, o_ref):
        xv = x_ref[...]
        r = lax.rsqrt(jnp.mean(xv * xv, axis=-1, keepdims=True) + EPS)
        o_ref[...] = (xv * r * g_ref[...]).astype(BF)

    return pl.pallas_call(
        body, name="rms_cast", grid=(T // tm,),
        in_specs=[pl.BlockSpec((tm, D_MODEL), lambda i: (i, 0)), pl.BlockSpec((1, D_MODEL), lambda i: (0, 0))],
        out_specs=pl.BlockSpec((tm, D_MODEL), lambda i: (i, 0)),
        out_shape=jax.ShapeDtypeStruct((T, D_MODEL), BF),
        compiler_params=_params(("parallel",)),
    )(x, g)


def _hosted_call(body, name, grid, in_specs, out_specs, out_shape, scratch_shapes, vmem, args, push):
    sem = ("arbitrary",) * len(grid)
    if push is None:
        res = pl.pallas_call(body, name=name, grid=grid, in_specs=in_specs, out_specs=out_specs, out_shape=out_shape,
                             scratch_shapes=scratch_shapes, compiler_params=_params(sem, vmem))(*args)
        return list(res), []
    srcs, lands = push
    ns, nl, n_in, n_out = (0 if srcs is None else len(srcs)), len(lands), len(in_specs), len(out_specs)
    n_scr = len(scratch_shapes)

    def wrapped(*refs):
        pos = n_in + ns + nl
        ins, x_in = refs[:n_in], refs[n_in:pos]
        outs, x_out = refs[pos:pos + n_out], refs[pos + n_out:pos + n_out + nl]
        scr = refs[pos + n_out + nl:pos + n_out + nl + n_scr]
        ssem, rsem = refs[-2], refs[-1]
        src = None if srcs is None else x_in[:ns]
        ids = [pl.program_id(a) for a in range(len(grid))]
        first = functools.reduce(lambda p, q: p & q, [ids[a] == 0 for a in range(len(grid))])
        last = functools.reduce(lambda p, q: p & q, [ids[a] == grid[a] - 1 for a in range(len(grid))])

        @pl.when(first)
        def _():
            for cp in _push_copies(src, x_out, ssem, rsem, False):
                cp.start()

        body(*ins, *outs, *scr)

        @pl.when(last)
        def _():
            for cp in _push_copies(src, x_out, ssem, rsem, True):
                cp.wait_recv()
                cp.wait_send()

    anyspec = pl.BlockSpec(memory_space=pl.ANY)
    extra = ([] if srcs is None else list(srcs)) + list(lands)
    res = pl.pallas_call(
        wrapped, name=name, grid=grid,
        in_specs=list(in_specs) + [anyspec] * len(extra), out_specs=list(out_specs) + [anyspec] * nl,
        out_shape=list(out_shape) + [jax.ShapeDtypeStruct(a.shape, a.dtype) for a in lands],
        input_output_aliases={n_in + ns + i: n_out + i for i in range(nl)},
        scratch_shapes=list(scratch_shapes) + [pltpu.SemaphoreType.DMA((3 * nl,)), pltpu.SemaphoreType.DMA((3 * nl,))],
        compiler_params=_params(sem, vmem),
    )(*args, *extra)
    return list(res[:n_out]), list(res[n_out:])


def _mm_nn(a, b, name, out_dtype, tm=512, tn=1024, push=None):
    M, K = a.shape
    N = b.shape[1]
    tn = min(tn, N)

    def body(a_ref, b_ref, o_ref):
        o_ref[...] = _nn(a_ref[...], b_ref[...]).astype(o_ref.dtype)

    (out,), lands = _hosted_call(
        body, name, (N // tn, M // tm),
        [pl.BlockSpec((tm, K), lambda j, i: (i, 0)), pl.BlockSpec((K, tn), lambda j, i: (0, j))],
        [pl.BlockSpec((tm, tn), lambda j, i: (i, j))], [jax.ShapeDtypeStruct((M, N), out_dtype)], [], None, (a, b), push)
    return out, lands


def _mm_tn(a, b, name, grid, a_spec, b_spec, o_spec, out_shape, acc_shape):
    nk = grid[-1]

    def body(a_ref, b_ref, o_ref, acc):
        k = pl.program_id(len(grid) - 1)

        @pl.when(k == 0)
        def _():
            acc[...] = jnp.zeros(acc.shape, F32)

        acc[...] += _tn(a_ref[...].astype(BF), b_ref[...].astype(BF))

        @pl.when(k == nk - 1)
        def _():
            o_ref[...] = acc[...].astype(o_ref.dtype)

    return pl.pallas_call(
        body, name=name, grid=grid, in_specs=[a_spec, b_spec], out_specs=o_spec, out_shape=out_shape,
        scratch_shapes=[pltpu.VMEM(acc_shape, F32)],
        compiler_params=_params(("parallel",) * (len(grid) - 1) + ("arbitrary",), VMEM_BIG),
    )(a, b)


def _grad_plain(a, b, name, out_dtype, tk=GRAD_TK, tn=1024):
    T, M = a.shape
    N = b.shape[1]
    tn = min(tn, N)
    return _mm_tn(a, b, name, (N // tn, T // tk),
                  pl.BlockSpec((tk, M), lambda j, k: (k, 0)), pl.BlockSpec((tk, tn), lambda j, k: (k, j)),
                  pl.BlockSpec((M, tn), lambda j, k: (0, j)), jax.ShapeDtypeStruct((M, N), out_dtype), (M, tn))


def _grad_multi(a, bs, name, tk=1024):
    T, M = a.shape
    n = len(bs)
    nk = T // tk

    def body(*refs):
        a_ref, b_refs, o_refs, accs = refs[0], refs[1:1 + n], refs[1 + n:1 + 2 * n], refs[1 + 2 * n:]
        k = pl.program_id(0)

        @pl.when(k == 0)
        def _():
            for acc in accs:
                acc[...] = jnp.zeros(acc.shape, F32)

        av = a_ref[...]
        for i in range(n):
            accs[i][...] += _tn(av, b_refs[i][...])

        @pl.when(k == nk - 1)
        def _():
            for i in range(n):
                o_refs[i][...] = accs[i][...].astype(BF)

    widths = [b.shape[1] for b in bs]
    return pl.pallas_call(
        body, name=name, grid=(nk,),
        in_specs=[pl.BlockSpec((tk, M), lambda k: (k, 0))] + [pl.BlockSpec((tk, w), lambda k: (k, 0)) for w in widths],
        out_specs=[pl.BlockSpec((M, w), lambda k: (0, 0)) for w in widths],
        out_shape=[jax.ShapeDtypeStruct((M, w), BF) for w in widths],
        scratch_shapes=[pltpu.VMEM((M, w), F32) for w in widths],
        compiler_params=_params(("arbitrary",), VMEM_BIG),
    )(a, *bs)


def _grad_colstack(a, b, name, wcol, tk=GRAD_TK):
    T, M = a.shape
    N = b.shape[1]
    S = N // wcol
    nk = T // tk

    def body(a_ref, b_ref, o_ref, acc):
        k = pl.program_id(0)

        @pl.when(k == 0)
        def _():
            acc[...] = jnp.zeros(acc.shape, F32)

        acc[...] += _tn(a_ref[...], b_ref[...])

        @pl.when(k == nk - 1)
        def _():
            for s in range(S):
                o_ref[s] = acc[:, s * wcol:(s + 1) * wcol].astype(BF)

    return pl.pallas_call(
        body, name=name, grid=(nk,),
        in_specs=[pl.BlockSpec((tk, M), lambda k: (k, 0)), pl.BlockSpec((tk, N), lambda k: (k, 0))],
        out_specs=pl.BlockSpec((S, M, wcol), lambda k: (0, 0, 0)), out_shape=jax.ShapeDtypeStruct((S, M, wcol), BF),
        scratch_shapes=[pltpu.VMEM((M, N), F32)], compiler_params=_params(("arbitrary",), VMEM_BIG),
    )(a, b)


def _grad_astack(a, b, name, tk=1024, push=None):
    S, T, m = a.shape
    N = b.shape[1]
    nk = T // tk

    def body(a_ref, b_ref, o_ref, acc):
        k = pl.program_id(0)

        @pl.when(k == 0)
        def _():
            acc[...] = jnp.zeros(acc.shape, F32)

        bb = b_ref[...].astype(BF)
        for s in range(S):
            acc[s] += _tn(a_ref[s], bb)

        @pl.when(k == nk - 1)
        def _():
            o_ref[...] = acc[...].astype(BF)

    (out,), lands = _hosted_call(
        body, name, (nk,),
        [pl.BlockSpec((S, tk, m), lambda k: (0, k, 0)), pl.BlockSpec((tk, N), lambda k: (k, 0))],
        [pl.BlockSpec((S, m, N), lambda k: (0, 0, 0))], [jax.ShapeDtypeStruct((S, m, N), BF)],
        [pltpu.VMEM((S, m, N), F32)], VMEM_BIG, (a, b), push)
    return out, lands


def _rope_tables(T):
    half = 32
    pos = np.arange(T, dtype=np.float32)
    inv_freq = (np.float32(1.0) / (np.float32(10000.0) ** (np.arange(half, dtype=np.float32) / np.float32(half)))).astype(np.float32)
    ang = (pos[:, None] * inv_freq[None, :]).astype(np.float32)
    cos, sin = np.cos(ang).astype(np.float32), np.sin(ang).astype(np.float32)
    z = np.zeros((T, 64), np.float32)
    return (jnp.asarray(np.concatenate([cos, cos, z], axis=-1)), jnp.asarray(np.concatenate([-sin, sin, z], axis=-1)))


def _ret_consts():
    h = np.arange(RET_H, dtype=np.float32)
    log_g = np.log1p(-(np.float32(2.0) ** (-5.0 - h))).astype(np.float32)
    idx = np.arange(CHUNK, dtype=np.float32)
    diff = idx[:, None] - idx[None, :]
    decay = np.where(diff[None] >= 0, np.exp(np.maximum(diff, 0.0)[None] * log_g[:, None, None]), 0.0)
    zeta = np.exp((CHUNK - 1.0 - idx)[None, :] * log_g[:, None])
    xi = np.exp((idx + 1.0)[None, :] * log_g[:, None])
    gc = np.exp(CHUNK * log_g)
    bc = lambda v: np.broadcast_to(v[:, :, None], (RET_H, CHUNK, LANE)).astype(np.float32)
    gcb = np.broadcast_to(gc[:, None, None], (RET_H, LANE, LANE)).astype(np.float32)
    return (jnp.asarray(decay.astype(np.float32)), jnp.asarray(bc(zeta)), jnp.asarray(bc(xi)), jnp.asarray(gcb))


def _mix_prep(z_a, h, w_ff, cos_t, sin_t, b_f, g_q, g_k, tm=256, push=None):
    T = z_a.shape[0]

    def body(zqk_ref, zf_ref, h_ref, wff_ref, cos_ref, sin_ref, b_ref, g_ref, seg_ref, segt_ref,
             qr_ref, kr_ref, qf_ref, kf_ref, vf_ref, c_ref, nmax_ref, zff_ref, carry):
        i = pl.program_id(0)
        zff = _nn(h_ref[...], wff_ref[...])
        zff_ref[...] = zff

        @pl.when(i == 0)
        def _():
            carry[...] = jnp.zeros(carry.shape, F32)
            nmax_ref[...] = jnp.zeros(nmax_ref.shape, F32)

        lane = lax.broadcasted_iota(jnp.int32, (tm, LANE), 1)
        zpad = jnp.zeros((tm, 64), F32)
        cosv, sinv = cos_ref[...], sin_ref[...]
        zqk = zqk_ref[...].astype(F32)
        for h in range(RET_H):
            for src, dst, scale in ((0, qr_ref, 1.0), (256, kr_ref, 0.125)):
                xh = jnp.concatenate([zqk[:, src + 64 * h: src + 64 * h + 64], zpad], axis=-1)
                rot = xh * cosv + _swap32(xh) * sinv
                dst[h] = (rot * scale).astype(BF)

        lf_in = zff + b_ref[...]
        logf = jnp.minimum(lf_in, 0.0) - jnp.log(1.0 + jnp.exp(-jnp.abs(lf_in)))
        row = lax.broadcasted_iota(jnp.int32, (tm, tm), 0)
        col = lax.broadcasted_iota(jnp.int32, (tm, tm), 1)
        tri = (row >= col).astype(BF)
        hi, mid, lo = _split3(logf)
        cs = _nn(tri, hi) + _nn(tri, mid) + _nn(tri, lo) + carry[...]
        carry[...] = cs[tm - 1:tm, :]
        c_ref[...] = cs

        def seg_sum(v):
            return sum(_nn(t, seg_ref[...]) for t in _split3(v))

        zf = zf_ref[...].astype(F32)
        xqk = zf[:, :1024]
        rinv = lax.rsqrt(seg_sum(xqk * xqk) * (1.0 / FOX_D) + EPS)
        xn = xqk * sum(_nn(t, segt_ref[...]) for t in _split3(rinv)) * g_ref[...]
        nmax_ref[...] = jnp.maximum(nmax_ref[...], jnp.max(seg_sum(xn * xn), axis=0, keepdims=True))

        one = jnp.ones((tm, LANE), F32)
        for h in range(FOX_H):
            c = cs[:, h:h + 1]
            chi, cmid, clo = [t.astype(F32) for t in _split3(c)]
            qn = xn[:, 64 * h:64 * h + 64]
            kn = xn[:, 512 + 64 * h:512 + 64 * h + 64]
            vh = zf[:, 1024 + 64 * h:1024 + 64 * h + 64]
            qa = jnp.concatenate([qn, zpad], axis=-1)
            qa = jnp.where(lane == L_CQ, chi, jnp.where(lane == L_CQ + 1, cmid, jnp.where(lane == L_CQ + 2, clo, qa)))
            qa = jnp.where((lane >= L_CK) & (lane < L_CK + 3), one, qa)
            ka = jnp.concatenate([kn, zpad], axis=-1)
            ka = jnp.where(lane == L_CK, -chi, jnp.where(lane == L_CK + 1, -cmid, jnp.where(lane == L_CK + 2, -clo, ka)))
            ka = jnp.where(((lane >= L_CQ) & (lane < L_CQ + 3)) | ((lane >= L_LSE) & (lane < L_MAX + 3)), one, ka)
            va = jnp.concatenate([vh, zpad], axis=-1)
            va = jnp.where((lane >= 64) & (lane < 67), one, va)
            qf_ref[h] = qa.astype(BF)
            kf_ref[h] = ka.astype(BF)
            vf_ref[h] = va.astype(BF)

    hspec4 = pl.BlockSpec((RET_H, tm, LANE), lambda i: (0, i, 0))
    hspec8 = pl.BlockSpec((FOX_H, tm, LANE), lambda i: (0, i, 0))
    const = lambda r, w: pl.BlockSpec((r, w), lambda i: (0, 0))
    seg = _segment_matrix()
    g_all = jnp.concatenate([jnp.tile(g_q * 0.125, (1, FOX_H)), jnp.tile(g_k, (1, FOX_H))], axis=1)
    return _hosted_call(
        body, "mix_prep", (T // tm,),
        [pl.BlockSpec((tm, 512), lambda i: (i, 0)), pl.BlockSpec((tm, 1536), lambda i: (i, 1)),
         pl.BlockSpec((tm, D_MODEL), lambda i: (i, 0)), const(D_MODEL, LANE), pl.BlockSpec((tm, LANE), lambda i: (i, 0)),
         pl.BlockSpec((tm, LANE), lambda i: (i, 0)), const(1, LANE), const(1, 1024), const(1024, LANE), const(LANE, 1024)],
        [hspec4, hspec4, hspec8, hspec8, hspec8, pl.BlockSpec((tm, LANE), lambda i: (i, 0)), const(1, LANE),
         pl.BlockSpec((tm, LANE), lambda i: (i, 0))],
        [jax.ShapeDtypeStruct((RET_H, T, LANE), BF)] * 2 + [jax.ShapeDtypeStruct((FOX_H, T, LANE), BF)] * 3
        + [jax.ShapeDtypeStruct((T, LANE), F32), jax.ShapeDtypeStruct((1, LANE), F32), jax.ShapeDtypeStruct((T, LANE), F32)],
        [pltpu.VMEM((1, LANE), F32)], VMEM_BIG, (z_a, z_a, h, w_ff, cos_t, sin_t, b_f, g_all, seg, seg.T), push)


def _segment_matrix():
    m = np.zeros((2 * FOX_H * FOX_D, LANE), np.float32)
    m[np.arange(2 * FOX_H * FOX_D), np.arange(2 * FOX_H * FOX_D) // FOX_D] = 1.0
    return jnp.asarray(m, dtype=BF)


def _ret_fwd(qr, kr, z_a, g_ret, consts, tt=512):
    T = z_a.shape[0]
    nch = tt // CHUNK
    decay, zeta, xi, gcb = consts

    def body(q_ref, k_ref, v_ref, gt_ref, g_ref, d_ref, ze_ref, xi_ref, gc_ref, o_ref, u_ref, st_ref, r_sc):
        i = pl.program_id(0)

        @pl.when(i == 0)
        def _():
            r_sc[...] = jnp.zeros(r_sc.shape, F32)

        for c in range(nch):
            rows = slice(c * CHUNK, (c + 1) * CHUNK)
            for h in range(RET_H):
                cols = slice(h * RET_DV, (h + 1) * RET_DV)
                q, k = q_ref[h, rows, :], k_ref[h, rows, :]
                v32 = v_ref[rows, cols].astype(F32)
                r = r_sc[h]
                st_ref[h, c * CHUNK:c * CHUNK + LANE, :] = r
                s = _nt(q, k) * d_ref[h]
                o = _nn(s.astype(BF), v32.astype(BF)) + _nn(q, r.astype(BF)) * xi_ref[h]
                r_sc[h] = gc_ref[h] * r + _tn(k, (v32 * ze_ref[h]).astype(BF))
                o_ref[rows, cols] = o
                mu = jnp.mean(o, axis=-1, keepdims=True)
                xc = o - mu
                on = xc * lax.rsqrt(jnp.mean(xc * xc, axis=-1, keepdims=True) + EPS)
                gt = gt_ref[rows, cols].astype(F32)
                u_ref[rows, cols] = (gt * _sigmoid(gt) * (on * g_ref[:, cols])).astype(BF)

    hspec = pl.BlockSpec((RET_H, tt, LANE), lambda i: (0, i, 0))
    cspec = pl.BlockSpec((RET_H, CHUNK, LANE), lambda i: (0, 0, 0))
    dspec = pl.BlockSpec((RET_H, CHUNK, CHUNK), lambda i: (0, 0, 0))
    sspec = pl.BlockSpec((RET_H, LANE, LANE), lambda i: (0, 0, 0))
    return pl.pallas_call(
        body, name="ret_fwd", grid=(T // tt,),
        in_specs=[hspec, hspec, pl.BlockSpec((tt, 512), lambda i: (i, 1)), pl.BlockSpec((tt, 512), lambda i: (i, 2)),
                  pl.BlockSpec((1, 512), lambda i: (0, 0)), dspec, cspec, cspec, sspec],
        out_specs=[pl.BlockSpec((tt, 512), lambda i: (i, 0)), pl.BlockSpec((tt, 512), lambda i: (i, 0)), hspec],
        out_shape=[jax.ShapeDtypeStruct((T, 512), F32), jax.ShapeDtypeStruct((T, 512), BF),
                   jax.ShapeDtypeStruct((RET_H, T, LANE), F32)],
        scratch_shapes=[pltpu.VMEM((RET_H, LANE, LANE), F32)],
        compiler_params=_params(("arbitrary",), VMEM_BIG),
    )(qr, kr, z_a, z_a, g_ret, decay, zeta, xi, gcb)


PRUNE_LOG = -110.0
TAME_LOGIT_SPAN = 60.0


def _prune_tables(c, nmax, sub):
    n = c.shape[0] // sub
    u = jnp.sqrt(nmax[0, :FOX_H] * nmax[0, FOX_H:2 * FOX_H]) * 1.02 + 0.5
    first = c[0::sub, :FOX_H].T
    last = c[sub - 1::sub, :FOX_H].T
    blk = jnp.arange(n, dtype=jnp.int32)
    needed = (2.0 * u[:, None, None] + first[:, :, None] - last[:, None, :] >= PRUNE_LOG) | (blk[None, :] >= blk[:, None])[None]
    jlo = jnp.argmax(needed, axis=2).astype(jnp.int32)

    def end_of(key_block):
        reach = jlo[:, None, :] <= key_block[None, :, None]
        return (n - jnp.argmax(reach[:, :, ::-1], axis=2)).astype(jnp.int32)

    sup = 2 * jnp.arange(n // 2, dtype=jnp.int32)
    end_last = end_of(sup + 1)
    end_both = jnp.clip(end_of(sup), sup[None, :] + 2, end_last)
    tame = (2.0 * u < TAME_LOGIT_SPAN).astype(jnp.int32)
    return jlo, end_both, end_last, tame


def _fox_fwd(jlo, tame, q, k, v, sub=FOX_SUB):
    H, T, _ = q.shape
    tb = 2 * sub

    def body(js_ref, tame_ref, q_ref, k_ref, v_ref, o_ref, q2_ref, mx_sc, acc_sc):
        i = pl.program_id(1)
        hd = pl.program_id(0)
        starts = [jnp.minimum(js_ref[hd, 2 * i], 2 * i), jnp.minimum(js_ref[hd, 2 * i + 1], 2 * i)]
        lane = lax.broadcasted_iota(jnp.int32, (sub, LANE), 1)
        row = lax.broadcasted_iota(jnp.int32, (sub, sub), 0)
        col = lax.broadcasted_iota(jnp.int32, (sub, sub), 1)
        causal = row >= col
        qs = [q_ref[0:sub, :], q_ref[sub:tb, :]]
        d0 = pl.multiple_of(i * tb, tb)
        d1 = pl.multiple_of(i * tb + sub, sub)
        k0, k1 = k_ref[pl.ds(d0, sub), :], k_ref[pl.ds(d1, sub), :]
        v0, v1 = v_ref[pl.ds(d0, sub), :], v_ref[pl.ds(d1, sub), :]

        def lane_max(s):
            m = s[:, 0:LANE]
            for c in range(1, s.shape[1] // LANE):
                m = jnp.maximum(m, s[:, c * LANE:(c + 1) * LANE])
            return m

        def put3(base, first, val):
            hi, mid, lo = _split3(val)
            return jnp.where(lane == first, hi, jnp.where(lane == first + 1, mid, jnp.where(lane == first + 2, lo, base)))

        def row_max():
            mx_sc[...] = jnp.full(mx_sc.shape, NEG, F32)
            for a in range(2):
                def max_body(j, carry, a=a):
                    kb = k_ref[pl.ds(pl.multiple_of(j * sub, sub), sub), :]
                    mx_sc[a] = jnp.maximum(mx_sc[a], lane_max(_nt(qs[a], kb)))
                    return carry

                lax.fori_loop(starts[a], 2 * i, max_body, 0)
            mx = [jnp.maximum(mx_sc[0], lane_max(jnp.where(causal, _nt(qs[0], k0), NEG))),
                  jnp.maximum(jnp.maximum(mx_sc[1], lane_max(_nt(qs[1], k0))),
                              lane_max(jnp.where(causal, _nt(qs[1], k1), NEG)))]
            return [jnp.max(t, axis=1, keepdims=True) for t in mx]

        def diag_logit():
            return [jnp.sum(qs[a].astype(F32) * kd.astype(F32), axis=1, keepdims=True) for a, kd in enumerate((k0, k1))]

        def finish(ms):
            qm = [put3(qs[a], L_MAX, -ms[a]) for a in range(2)]
            acc_sc[...] = jnp.zeros(acc_sc.shape, F32)
            for a in range(2):
                def acc_body(j, carry, a=a):
                    off = pl.multiple_of(j * sub, sub)
                    acc_sc[a] += _nn(jnp.exp(_nt(qm[a], k_ref[pl.ds(off, sub), :])).astype(BF), v_ref[pl.ds(off, sub), :])
                    return carry

                lax.fori_loop(starts[a], 2 * i, acc_body, 0)

            def pv(qa, kk, vv, masked):
                p = jnp.exp(_nt(qa, kk))
                if masked:
                    p = jnp.where(causal, p, 0.0)
                return _nn(p.astype(BF), vv)

            accs = [acc_sc[0] + pv(qm[0], k0, v0, True),
                    acc_sc[1] + pv(qm[1], k0, v0, False) + pv(qm[1], k1, v1, True)]
            for a in range(2):
                rows = slice(a * sub, (a + 1) * sub)
                l = accs[a][:, 64:65]
                o_ref[rows, :] = jnp.where(lane < 64, accs[a] / l, 0.0)
                q2_ref[rows, :] = put3(qs[a], L_LSE, -(ms[a] + jnp.log(l)))

        tame = tame_ref[hd] == 1

        @pl.when(tame)
        def _():
            finish(diag_logit())

        @pl.when(jnp.logical_not(tame))
        def _():
            finish(row_max())

    blk = pl.BlockSpec((None, tb, LANE), lambda h, i, js, tm_: (h, i, 0))
    full = pl.BlockSpec((None, T, LANE), lambda h, i, js, tm_: (h, 0, 0))
    return pl.pallas_call(
        body, name="fox_fwd",
        grid_spec=pltpu.PrefetchScalarGridSpec(
            num_scalar_prefetch=2, grid=(H, T // tb), in_specs=[blk, full, full], out_specs=[blk, blk],
            scratch_shapes=[pltpu.VMEM((2, sub, LANE), F32), pltpu.VMEM((2, sub, LANE), F32)]),
        out_shape=[jax.ShapeDtypeStruct((H, T, LANE), F32), jax.ShapeDtypeStruct((H, T, LANE), BF)],
        compiler_params=_params(("parallel", "arbitrary"), VMEM_BIG),
    )(jlo, tame, q, k, v)


def _merge_out(u_r, o_fox, z_a, x, g_ffn, w_ro, w_fo, w_out, tm=256, push=None):
    T = x.shape[0]

    def body(u_ref, of_ref, ar_ref, af_ref, x_ref, g_ref, wro_ref, wfo_ref, wout_ref,
             yr_ref, yf_ref, m_ref, x2_ref, h2_ref, oc_ref):
        u = u_ref[...]
        oc = jnp.concatenate([of_ref[h][:, :FOX_D] for h in range(FOX_H)], axis=-1).astype(BF)
        oc_ref[...] = oc
        yr = jnp.concatenate([_nn(u, wro_ref[k]) for k in range(N_CHIP)], axis=-1)
        yf = jnp.concatenate([_nn(oc, wfo_ref[k]) for k in range(N_CHIP)], axis=-1)
        yr_ref[...] = yr.astype(BF)
        yf_ref[...] = yf.astype(BF)
        m = (_sigmoid(ar_ref[...].astype(F32)) * yr + _sigmoid(af_ref[...].astype(F32)) * yf).astype(BF)
        m_ref[...] = m
        x2 = x_ref[...]
        for k in range(N_CHIP):
            x2 = x2 + _nn(m[:, 256 * k:256 * k + 256], wout_ref[k])
        x2_ref[...] = x2
        r = lax.rsqrt(jnp.mean(x2 * x2, axis=-1, keepdims=True) + EPS)
        h2_ref[...] = (x2 * r * g_ref[...]).astype(BF)

    row = lambda w: pl.BlockSpec((tm, w), lambda i: (i, 0))
    const = lambda shp: pl.BlockSpec(shp, lambda i: (0,) * len(shp))
    return _hosted_call(
        body, "merge_out", (T // tm,),
        [row(512), pl.BlockSpec((FOX_H, tm, LANE), lambda i: (0, i, 0)),
         pl.BlockSpec((tm, 1024), lambda i: (i, 3)), pl.BlockSpec((tm, 1024), lambda i: (i, 4)),
         row(1024), const((1, 1024)), const((N_CHIP, 512, 256)), const((N_CHIP, 512, 256)),
         const((N_CHIP, 256, 1024))],
        [row(1024), row(1024), row(1024), row(1024), row(1024), row(512)],
        [jax.ShapeDtypeStruct((T, 1024), BF), jax.ShapeDtypeStruct((T, 1024), BF),
         jax.ShapeDtypeStruct((T, 1024), BF), jax.ShapeDtypeStruct((T, 1024), F32),
         jax.ShapeDtypeStruct((T, 1024), BF), jax.ShapeDtypeStruct((T, 512), BF)],
        [], VMEM_BIG, (u_r, o_fox, z_a, z_a, x, g_ffn, w_ro, w_fo, w_out), push)


def _load_resident(hbm_refs, vmem_refs, sem):
    cps = [pltpu.make_async_copy(h, v, sem.at[i]) for i, (h, v) in enumerate(zip(hbm_refs, vmem_refs))]
    for cp in cps:
        cp.start()
    for cp in cps:
        cp.wait()


def _ffn_fwd(h2, x2, tgt, w_gate, w_up, w_down, tm=FFN_TM):
    T = h2.shape[0]

    def body(h_ref, x2_ref, t_ref, wg_hbm, wu_hbm, wd_hbm, a_ref, b_ref, act_ref, dy_ref, ls_ref, wg, wu, wd, sem):
        @pl.when(pl.program_id(0) == 0)
        def _():
            _load_resident((wg_hbm, wu_hbm, wd_hbm), (wg, wu, wd), sem)
            ls_ref[...] = jnp.zeros(ls_ref.shape, F32)

        h = h_ref[...]
        err = x2_ref[...] - t_ref[...]
        for k in range(N_CHIP):
            gp = _nt(h, wg[k])
            up = _nt(h, wu[k])
            sg = _sigmoid(gp)
            silu = gp * sg
            a_ref[k] = silu.astype(BF)
            b_ref[k] = (up * sg * (1.0 + gp * (1.0 - sg))).astype(BF)
            act = (silu * up).astype(BF)
            act_ref[k] = act
            err = err + _nn(act, wd[k])
        dy_ref[...] = err * (1.0 / D_MODEL)
        ls_ref[...] += jnp.sum(err * err, axis=0, keepdims=True)

    row = pl.BlockSpec((tm, D_MODEL), lambda i: (i, 0))
    hid = pl.BlockSpec((N_CHIP, tm, FF_SH), lambda i: (0, i, 0))
    anyspec = pl.BlockSpec(memory_space=pl.ANY)
    wshape = pltpu.VMEM((N_CHIP, FF_SH, D_MODEL), BF)
    return pl.pallas_call(
        body, name="ffn_fwd", grid=(T // tm,),
        in_specs=[row, row, row, anyspec, anyspec, anyspec],
        out_specs=[hid, hid, hid, row, pl.BlockSpec((1, D_MODEL), lambda i: (0, 0))],
        out_shape=[jax.ShapeDtypeStruct((N_CHIP, T, FF_SH), BF)] * 3
        + [jax.ShapeDtypeStruct((T, D_MODEL), F32), jax.ShapeDtypeStruct((1, D_MODEL), F32)],
        scratch_shapes=[wshape, wshape, wshape, pltpu.SemaphoreType.DMA((3,))],
        compiler_params=_params(("arbitrary",), VMEM_HUGE),
    )(h2, x2, tgt, w_gate, w_up, w_down)


def _ffn_bwd(dy, sa, sb, x2, g_ffn, w_gate, w_up, w_down, tm=FFN_TM):
    T = dy.shape[0]

    def body(dy_ref, a_ref, b_ref, x2_ref, g_ref, wg_hbm, wu_hbm, wd_hbm, dgp_ref, dup_ref, dx_ref, dg_ref,
             wg, wu, wd, sem):
        @pl.when(pl.program_id(0) == 0)
        def _():
            _load_resident((wg_hbm, wu_hbm, wd_hbm), (wg, wu, wd), sem)
            dg_ref[...] = jnp.zeros(dg_ref.shape, F32)

        dy = dy_ref[...]
        dyb = dy.astype(BF)
        dh = jnp.zeros((tm, D_MODEL), F32)
        for k in range(N_CHIP):
            dact = _nt(dyb, wd[k])
            dup = (dact * a_ref[k]).astype(BF)
            dgp = (dact * b_ref[k]).astype(BF)
            dgp_ref[k] = dgp
            dup_ref[k] = dup
            dh = dh + _nn(dgp, wg[k]) + _nn(dup, wu[k])
        x2 = x2_ref[...]
        r = lax.rsqrt(jnp.mean(x2 * x2, axis=-1, keepdims=True) + EPS)
        xn = x2 * r
        dg_ref[...] += jnp.sum(dh * xn, axis=0, keepdims=True)
        dxn = dh * g_ref[...]
        dx_ref[...] = dy + r * (dxn - xn * jnp.mean(dxn * xn, axis=-1, keepdims=True))

    row = pl.BlockSpec((tm, D_MODEL), lambda i: (i, 0))
    hid = pl.BlockSpec((N_CHIP, tm, FF_SH), lambda i: (0, i, 0))
    vec = pl.BlockSpec((1, D_MODEL), lambda i: (0, 0))
    anyspec = pl.BlockSpec(memory_space=pl.ANY)
    wshape = pltpu.VMEM((N_CHIP, FF_SH, D_MODEL), BF)
    return pl.pallas_call(
        body, name="ffn_bwd", grid=(T // tm,),
        in_specs=[row, hid, hid, row, vec, anyspec, anyspec, anyspec],
        out_specs=[hid, hid, row, vec],
        out_shape=[jax.ShapeDtypeStruct((N_CHIP, T, FF_SH), BF), jax.ShapeDtypeStruct((N_CHIP, T, FF_SH), BF),
                   jax.ShapeDtypeStruct((T, D_MODEL), F32), jax.ShapeDtypeStruct((1, D_MODEL), F32)],
        scratch_shapes=[wshape, wshape, wshape, pltpu.SemaphoreType.DMA((3,))],
        compiler_params=_params(("arbitrary",), VMEM_HUGE),
    )(dy, sa, sb, x2, g_ffn, w_gate, w_up, w_down)


def _out_bwd(dx2, z_a, y_r, y_f, o_raw, o_fox, g_ret, w_ro, w_fo, w_out, tm=256, push=None):
    T = dx2.shape[0]

    def body(dx_ref, gt_ref, ar_ref, af_ref, yr_ref, yf_ref, o_ref, of_ref, g_ref, wro_ref, wfo_ref, wout_ref,
             dyr_ref, dyf_ref, dgt_ref, da_ref, do_ref, dof_ref, dg_ref):
        i = pl.program_id(0)

        @pl.when(i == 0)
        def _():
            dg_ref[...] = jnp.zeros(dg_ref.shape, F32)

        dxb = dx_ref[...].astype(BF)
        dm = jnp.concatenate([_nt(dxb, wout_ref[k]) for k in range(N_CHIP)], axis=-1)
        sr, sf = _sigmoid(ar_ref[...].astype(F32)), _sigmoid(af_ref[...].astype(F32))
        dyr = dm * sr
        dyf = dm * sf
        da_ref[:, :1024] = (dyr * yr_ref[...].astype(F32) * (1.0 - sr)).astype(BF)
        da_ref[:, 1024:] = (dyf * yf_ref[...].astype(F32) * (1.0 - sf)).astype(BF)
        dyr = dyr.astype(BF)
        dyf = dyf.astype(BF)
        dyr_ref[...] = dyr
        dyf_ref[...] = dyf
        du = jnp.zeros((tm, 512), F32)
        doc = jnp.zeros((tm, 512), F32)
        for k in range(N_CHIP):
            du = du + _nt(dyr[:, 256 * k:256 * k + 256], wro_ref[k])
            doc = doc + _nt(dyf[:, 256 * k:256 * k + 256], wfo_ref[k])

        for h in range(RET_H):
            cols = slice(h * RET_DV, (h + 1) * RET_DV)
            o = o_ref[:, cols]
            mu = jnp.mean(o, axis=-1, keepdims=True)
            xc = o - mu
            rstd = lax.rsqrt(jnp.mean(xc * xc, axis=-1, keepdims=True) + EPS)
            on = xc * rstd
            g = g_ref[:, cols]
            gt = gt_ref[:, cols].astype(F32)
            sg = _sigmoid(gt)
            duh = du[:, cols]
            dgt_ref[:, cols] = (duh * (on * g) * sg * (1.0 + gt * (1.0 - sg))).astype(BF)
            dog = duh * gt * sg
            dg_ref[:, cols] += jnp.sum(dog * on, axis=0, keepdims=True)
            don = dog * g
            do_ref[:, cols] = rstd * (don - jnp.mean(don, axis=-1, keepdims=True)
                                      - on * jnp.mean(don * on, axis=-1, keepdims=True))

        lane = lax.broadcasted_iota(jnp.int32, (tm, LANE), 1)
        zpad = jnp.zeros((tm, 64), F32)
        for h in range(FOX_H):
            doh = doc[:, 64 * h:64 * h + 64]
            delta = jnp.sum(doh * of_ref[h][:, :FOX_D], axis=-1, keepdims=True)
            hi, mid, lo = [t.astype(F32) for t in _split3(-delta)]
            da = jnp.concatenate([doh, zpad], axis=-1)
            da = jnp.where(lane == 64, hi, jnp.where(lane == 65, mid, jnp.where(lane == 66, lo, da)))
            dof_ref[h] = da.astype(BF)

    row = lambda w: pl.BlockSpec((tm, w), lambda i: (i, 0))
    const = lambda shp: pl.BlockSpec(shp, lambda i: (0,) * len(shp))
    hsp = pl.BlockSpec((FOX_H, tm, LANE), lambda i: (0, i, 0))
    return _hosted_call(
        body, "out_bwd", (T // tm,),
        [row(1024), pl.BlockSpec((tm, 512), lambda i: (i, 2)), pl.BlockSpec((tm, 1024), lambda i: (i, 3)),
         pl.BlockSpec((tm, 1024), lambda i: (i, 4)), row(1024), row(1024), row(512), hsp,
         const((1, 512)), const((N_CHIP, 512, 256)), const((N_CHIP, 512, 256)), const((N_CHIP, 256, 1024))],
        [row(1024), row(1024), row(512), row(2048), row(512), hsp, const((1, 512))],
        [jax.ShapeDtypeStruct((T, 1024), BF), jax.ShapeDtypeStruct((T, 1024), BF),
         jax.ShapeDtypeStruct((T, 512), BF), jax.ShapeDtypeStruct((T, 2048), BF),
         jax.ShapeDtypeStruct((T, 512), F32), jax.ShapeDtypeStruct((FOX_H, T, LANE), BF),
         jax.ShapeDtypeStruct((1, 512), F32)],
        [], VMEM_BIG, (dx2, z_a, z_a, z_a, y_r, y_f, o_raw, o_fox, g_ret, w_ro, w_fo, w_out), push)


def _ret_bwd(d_o, qr, kr, z_a, states, cos_t, sin_t, consts, tt=512, push=None):
    T = z_a.shape[0]
    nt = T // tt
    nch = tt // CHUNK
    decay, zeta, xi, gcb = consts

    def body(do_ref, q_ref, k_ref, v_ref, st_ref, cos_ref, sin_ref, d_ref, ze_ref, xi_ref, gc_ref, dz_ref, g_sc):
        i = pl.program_id(0)

        @pl.when(i == 0)
        def _():
            g_sc[...] = jnp.zeros(g_sc.shape, F32)

        for c in reversed(range(nch)):
            rows = slice(c * CHUNK, (c + 1) * CHUNK)
            cosv, sinv = cos_ref[rows, :], sin_ref[rows, :]
            dq_parts, dk_parts = [], []
            for h in range(RET_H):
                cols = slice(h * RET_DV, (h + 1) * RET_DV)
                q, k = q_ref[h, rows, :], k_ref[h, rows, :]
                v32 = v_ref[rows, cols].astype(F32)
                vb = v32.astype(BF)
                r = st_ref[h, c * CHUNK:c * CHUNK + LANE, :]
                g = g_sc[h]
                gb = g.astype(BF)
                d_o = do_ref[rows, cols]
                dob = d_o.astype(BF)
                dox = (d_o * xi_ref[h]).astype(BF)
                dec = d_ref[h]
                s = (_nt(q, k) * dec).astype(BF)
                ds = (_nt(dob, vb) * dec).astype(BF)
                dv = _tn(s, dob) + ze_ref[h] * _nn(k, gb)
                dq = _nn(ds, k) + _nt(dox, r.astype(BF))
                dk = _tn(ds, q) + _nt((v32 * ze_ref[h]).astype(BF), gb)
                g_sc[h] = gc_ref[h] * g + _tn(q, dox)
                dq_parts.append((dq * cosv - _swap32(dq) * sinv)[:, :64])
                dk_parts.append(((dk * cosv - _swap32(dk) * sinv) * 0.125)[:, :64])
                dz_ref[rows, 512 + h * RET_DV:512 + (h + 1) * RET_DV] = dv.astype(BF)
            dz_ref[rows, 0:256] = jnp.concatenate(dq_parts, axis=-1).astype(BF)
            dz_ref[rows, 256:512] = jnp.concatenate(dk_parts, axis=-1).astype(BF)

    rev = lambda i: nt - 1 - i
    hspec = pl.BlockSpec((RET_H, tt, LANE), lambda i: (0, rev(i), 0))
    cspec = pl.BlockSpec((RET_H, CHUNK, LANE), lambda i: (0, 0, 0))
    tab = pl.BlockSpec((tt, LANE), lambda i: (rev(i), 0))
    (dz,), lands = _hosted_call(
        body, "ret_bwd", (nt,),
        [pl.BlockSpec((tt, 512), lambda i: (rev(i), 0)), hspec, hspec,
         pl.BlockSpec((tt, 512), lambda i: (rev(i), 1)), hspec, tab, tab,
         pl.BlockSpec((RET_H, CHUNK, CHUNK), lambda i: (0, 0, 0)), cspec, cspec,
         pl.BlockSpec((RET_H, LANE, LANE), lambda i: (0, 0, 0))],
        [pl.BlockSpec((tt, 1024), lambda i: (rev(i), 0))], [jax.ShapeDtypeStruct((T, 1024), BF)],
        [pltpu.VMEM((RET_H, LANE, LANE), F32)], VMEM_BIG,
        (d_o, qr, kr, z_a, states, cos_t, sin_t, decay, zeta, xi, gcb), push)
    return dz, lands


def _fox_bwd(end_both, end_last, q2, k, v, do, sub=FOX_SUB):
    H, T, _ = k.shape
    tb = 2 * sub

    def body(eb_ref, el_ref, q_ref, do_ref, k_ref, v_ref, dq_ref, dk_ref, dv_ref, dk_sc, dv_sc):
        j = pl.program_id(1)
        n_both = eb_ref[pl.program_id(0), j]
        n_last = el_ref[pl.program_id(0), j]

        @pl.when(j == 0)
        def _():
            dq_ref[...] = jnp.zeros(dq_ref.shape, F32)

        dk_sc[...] = jnp.zeros(dk_sc.shape, F32)
        dv_sc[...] = jnp.zeros(dv_sc.shape, F32)
        krow = lax.broadcasted_iota(jnp.int32, (tb, sub), 0)
        qcol = lax.broadcasted_iota(jnp.int32, (tb, sub), 1)

        def step(i, r0, r1, shift):
            off = pl.multiple_of(i * sub, sub)
            qq = q_ref[pl.ds(off, sub), :]
            dd = do_ref[pl.ds(off, sub), :]
            kk, vv = k_ref[r0:r1, :], v_ref[r0:r1, :]
            p = jnp.exp(_nt(kk, qq))
            if shift is not None:
                p = jnp.where(qcol[0:r1 - r0, :] + shift >= krow[0:r1 - r0, :], p, 0.0)
            ds = (p * _nt(vv, dd)).astype(BF)
            dv_sc[r0:r1, :] += _nn(p.astype(BF), dd)
            dk_sc[r0:r1, :] += _nn(ds, qq)
            dq_ref[pl.ds(off, sub), :] += _tn(ds, kk)

        step(2 * j, 0, sub, 0)
        step(2 * j + 1, 0, tb, sub)

        def both_body(i, carry):
            step(i, 0, tb, None)
            return carry

        def last_body(i, carry):
            step(i, sub, tb, None)
            return carry

        lax.fori_loop(2 * j + 2, n_both, both_body, 0)
        lax.fori_loop(n_both, n_last, last_body, 0)
        dk_ref[...] = dk_sc[...]
        dv_ref[...] = dv_sc[...]

    blk = pl.BlockSpec((None, tb, LANE), lambda h, j, eb, el: (h, j, 0))
    full = pl.BlockSpec((None, T, LANE), lambda h, j, eb, el: (h, 0, 0))
    shp = jax.ShapeDtypeStruct((H, T, LANE), F32)
    return pl.pallas_call(
        body, name="fox_bwd",
        grid_spec=pltpu.PrefetchScalarGridSpec(
            num_scalar_prefetch=2, grid=(H, T // tb), in_specs=[full, full, blk, blk], out_specs=[full, blk, blk],
            scratch_shapes=[pltpu.VMEM((tb, LANE), F32), pltpu.VMEM((tb, LANE), F32)]),
        out_shape=[shp, shp, shp],
        compiler_params=_params(("arbitrary", "arbitrary"), VMEM_BIG),
    )(end_both, end_last, q2, do, k, v)


def _fox_post_bwd(dq, dk, dv, z_a, z_ff, b_f, g_q, g_k, tm=256, push=None):
    T = z_a.shape[0]
    nt = T // tm

    def body(dq_ref, dk_ref, dv_ref, zf_ref, zff_ref, b_ref, g_ref, sc_ref, seg_ref, segt_ref,
             dz_ref, dff_ref, dg_ref, db_ref, carry):
        i = pl.program_id(0)

        @pl.when(i == 0)
        def _():
            carry[...] = jnp.zeros(carry.shape, F32)
            dg_ref[...] = jnp.zeros(dg_ref.shape, F32)
            db_ref[...] = jnp.zeros(db_ref.shape, F32)

        lane = lax.broadcasted_iota(jnp.int32, (tm, LANE), 1)
        dcm = jnp.zeros((tm, LANE), F32)
        for h in range(FOX_H):
            dcm = jnp.where(lane == h, dq_ref[h][:, L_CQ:L_CQ + 1] - dk_ref[h][:, L_CK:L_CK + 1], dcm)

        def seg_mean(v):
            return sum(_nn(t, seg_ref[...]) for t in _split3(v)) * (1.0 / FOX_D)

        def seg_bcast(v):
            return sum(_nn(t, segt_ref[...]) for t in _split3(v))

        x = zf_ref[:, :1024].astype(F32)
        dy = jnp.concatenate([dq_ref[h][:, :FOX_D] for h in range(FOX_H)]
                             + [dk_ref[h][:, :FOX_D] for h in range(FOX_H)], axis=-1) * sc_ref[...]
        rb = seg_bcast(lax.rsqrt(seg_mean(x * x) + EPS))
        xn = x * rb
        dg_ref[...] += jnp.sum(dy * xn, axis=0, keepdims=True)
        dxn = dy * g_ref[...]
        dz_ref[:, :1024] = (rb * (dxn - xn * seg_bcast(seg_mean(dxn * xn)))).astype(BF)
        dz_ref[:, 1024:] = jnp.concatenate([dv_ref[h][:, :FOX_D] for h in range(FOX_H)], axis=-1).astype(BF)

        row = lax.broadcasted_iota(jnp.int32, (tm, tm), 0)
        col = lax.broadcasted_iota(jnp.int32, (tm, tm), 1)
        tri = (row <= col).astype(BF)
        hi, mid, lo = _split3(dcm)
        dlogf = _nn(tri, hi) + _nn(tri, mid) + _nn(tri, lo) + carry[...]
        carry[...] = dlogf[0:1, :]
        dff = jnp.where(lane < FOX_H, dlogf * _sigmoid(-(zff_ref[...] + b_ref[...])), 0.0)
        dff_ref[...] = dff.astype(BF)
        db_ref[...] += jnp.sum(dff, axis=0, keepdims=True)

    rev = lambda i: nt - 1 - i
    hsp = pl.BlockSpec((FOX_H, tm, LANE), lambda i: (0, rev(i), 0))
    const = lambda r, w: pl.BlockSpec((r, w), lambda i: (0, 0))
    seg = _segment_matrix()
    g_all = jnp.concatenate([jnp.tile(g_q, (1, FOX_H)), jnp.tile(g_k, (1, FOX_H))], axis=1)
    scale = jnp.asarray(np.concatenate([np.full((1, 512), 0.125, np.float32), np.ones((1, 512), np.float32)], axis=1))
    (dz, dff, dg, db), lands = _hosted_call(
        body, "fox_post_bwd", (nt,),
        [hsp, hsp, hsp, pl.BlockSpec((tm, 1536), lambda i: (rev(i), 1)),
         pl.BlockSpec((tm, LANE), lambda i: (rev(i), 0)), const(1, LANE), const(1, 1024), const(1, 1024),
         const(1024, LANE), const(LANE, 1024)],
        [pl.BlockSpec((tm, 1536), lambda i: (rev(i), 0)), pl.BlockSpec((tm, LANE), lambda i: (rev(i), 0)),
         const(1, 1024), const(1, LANE)],
        [jax.ShapeDtypeStruct((T, 1536), BF), jax.ShapeDtypeStruct((T, LANE), BF),
         jax.ShapeDtypeStruct((1, 1024), F32), jax.ShapeDtypeStruct((1, LANE), F32)],
        [pltpu.VMEM((1, LANE), F32)], VMEM_BIG, (dq, dk, dv, z_a, z_ff, b_f, g_all, scale, seg, seg.T), push)
    dg_heads = dg.reshape(2, FOX_H, FOX_D).sum(axis=1)
    return (dz, dff, dg_heads[0:1], dg_heads[1:2], db), lands


def _in_bwd(dz_ret, dz_gt, dz_fox, dz_a, dz_ff, w_a, w_ff, x, g_mix, dx2, tm=256, push=None):
    T = x.shape[0]

    def body(r_ref, t_ref, f_ref, a_ref, ff_ref, wa_ref, wf_ref, x_ref, g_ref, dx2_ref, dx_ref, dg_ref):
        i = pl.program_id(0)

        @pl.when(i == 0)
        def _():
            dg_ref[...] = jnp.zeros(dg_ref.shape, F32)

        dh = (_nt(r_ref[...], wa_ref[:, C_RET:C_GT]) + _nt(t_ref[...], wa_ref[:, C_GT:C_FOX])
              + _nt(f_ref[...], wa_ref[:, C_FOX:C_A]) + _nt(a_ref[...], wa_ref[:, C_A:C_END])
              + _nt(ff_ref[...], wf_ref[...]))
        xv = x_ref[...]
        r = lax.rsqrt(jnp.mean(xv * xv, axis=-1, keepdims=True) + EPS)
        xn = xv * r
        dg_ref[...] += jnp.sum(dh * xn, axis=0, keepdims=True)
        dxn = dh * g_ref[...]
        dx_ref[...] = dx2_ref[...] + r * (dxn - xn * jnp.mean(dxn * xn, axis=-1, keepdims=True))

    row = lambda w: pl.BlockSpec((tm, w), lambda i: (i, 0))
    const = lambda shp: pl.BlockSpec(shp, lambda i: (0,) * len(shp))
    return _hosted_call(
        body, "in_bwd", (T // tm,),
        [row(1024), row(512), row(1536), row(2048), row(LANE), const((D_MODEL, C_END)),
         const((D_MODEL, LANE)), row(1024), const((1, 1024)), row(1024)],
        [row(1024), const((1, 1024))],
        [jax.ShapeDtypeStruct((T, 1024), F32), jax.ShapeDtypeStruct((1, 1024), F32)],
        [], VMEM_BIG, (dz_ret, dz_gt, dz_fox, dz_a, dz_ff, w_a, w_ff, x, g_mix, dx2), push)


def _mesh_pos():
    return lax.axis_index("x"), lax.axis_index("y"), lax.axis_index("c")


def _staged_place(src, name):
    stacked = src.ndim == 3
    R, C = src.shape[-2:]
    tr = _row_tile(R, 128, 16)
    n = R // tr
    assert n >= 2

    def body(s_ref, o_ref, buf, sem):
        i = pl.program_id(0)
        slot = i % 2
        x, y, _ = _mesh_pos()
        kme = 2 * x + y

        def out_copy(s, step):
            return pltpu.make_async_copy(buf.at[s], o_ref.at[kme, pl.ds(pl.multiple_of(step * tr, tr), tr), :], sem.at[s])

        @pl.when(i >= 2)
        def _():
            out_copy(slot, i - 2).wait()

        buf[slot] = (s_ref[kme] if stacked else s_ref[...]).astype(BF)
        out_copy(slot, i).start()

        @pl.when(i == n - 1)
        def _():
            out_copy(1 - slot, i - 1).wait()
            out_copy(slot, i).wait()

    in_spec = (pl.BlockSpec((N_CHIP, tr, C), lambda i: (0, i, 0)) if stacked else pl.BlockSpec((tr, C), lambda i: (i, 0)))
    return pl.pallas_call(
        body, name=name, grid=(n,), in_specs=[in_spec], out_specs=pl.BlockSpec(memory_space=pl.ANY),
        out_shape=jax.ShapeDtypeStruct((N_CHIP, R, C), BF),
        scratch_shapes=[pltpu.VMEM((2, tr, C), BF), pltpu.SemaphoreType.DMA((2,))],
        compiler_params=_params(("arbitrary",)),
    )(src)


def _push_copies(src, land, send_sem, recv_sem, receiving):
    x, y, c = _mesh_pos()
    kme = 2 * x + y
    cps = []
    for w in range(len(land)):
        for j, (px, py) in enumerate([(1 - x, y), (x, 1 - y), (1 - x, 1 - y)]):
            kpeer = 2 * px + py
            cps.append(pltpu.make_async_remote_copy(
                src_ref=land[w].at[kme] if src is None else src[w].at[kpeer],
                dst_ref=land[w].at[kpeer if receiving else kme],
                send_sem=send_sem.at[3 * w + j], recv_sem=recv_sem.at[3 * w + j],
                device_id=(px, py, c), device_id_type=MESH))
    return cps


def _gather_two_level(stack, name):
    _, R, C = stack.shape
    hr = R // 2

    def body(_, land, send_sem, recv_sem):
        x, y, c = _mesh_pos()
        kme = 2 * x + y
        chips = [(1 - x, y), (x, 1 - y), (1 - x, 1 - y)]

        def rows(k, core):
            return land.at[k, pl.ds(pl.multiple_of(core * hr, hr), hr), :]

        def copy(idx, k, core, to):
            return pltpu.make_async_remote_copy(src_ref=rows(k, core), dst_ref=rows(k, core), send_sem=send_sem.at[idx],
                                                recv_sem=recv_sem.at[idx], device_id=to, device_id_type=MESH)

        first = [copy(j, kme, c, (px, py, c)) for j, (px, py) in enumerate(chips)]
        for cp in first:
            cp.start()
        passed = [copy(3 + j, 2 * px + py, c, (x, y, 1 - c)) for j, (px, py) in enumerate(chips)]
        for j, (px, py) in enumerate(chips):
            copy(j, 2 * px + py, c, (px, py, c)).wait_recv()
            passed[j].start()
        for j, (px, py) in enumerate(chips):
            copy(3 + j, 2 * px + py, 1 - c, (x, y, 1 - c)).wait_recv()
        for cp in first + passed:
            cp.wait_send()

    anyspec = pl.BlockSpec(memory_space=pl.ANY)
    return pl.pallas_call(
        body, name=name, in_specs=[anyspec], out_specs=anyspec,
        out_shape=jax.ShapeDtypeStruct(stack.shape, stack.dtype), input_output_aliases={0: 0},
        scratch_shapes=[pltpu.SemaphoreType.DMA((6,)), pltpu.SemaphoreType.DMA((6,))],
    )(stack)


def _gather_small(small):
    def body(sv, svo, ssend, srecv, sloc):
        x, y, c = _mesh_pos()
        me = 4 * x + 2 * y + c
        flips = [(b >> 2 & 1, b >> 1 & 1, b & 1) for b in range(1, 8)]
        others = [(1 - x if fx else x, 1 - y if fy else y, 1 - c if fc else c) for fx, fy, fc in flips]
        local = pltpu.make_async_copy(sv, svo.at[me], sloc)
        local.start()
        sends = []
        for j, (px, py, pc) in enumerate(others):
            cp = pltpu.make_async_remote_copy(
                src_ref=sv, dst_ref=svo.at[me], send_sem=ssend.at[j], recv_sem=srecv.at[j],
                device_id=(px, py, pc), device_id_type=MESH)
            cp.start()
            sends.append(cp)
        for j, (px, py, pc) in enumerate(others):
            pltpu.make_async_remote_copy(
                src_ref=sv, dst_ref=svo.at[4 * px + 2 * py + pc], send_sem=ssend.at[j], recv_sem=srecv.at[j],
                device_id=(px, py, pc), device_id_type=MESH).wait_recv()
        for cp in sends:
            cp.wait_send()
        local.wait()

    anyspec = pl.BlockSpec(memory_space=pl.ANY)
    return pl.pallas_call(
        body, name="gather_small", in_specs=[anyspec], out_specs=anyspec,
        out_shape=jax.ShapeDtypeStruct((8,) + small.shape, small.dtype),
        scratch_shapes=[pltpu.SemaphoreType.DMA((7,)), pltpu.SemaphoreType.DMA((7,)), pltpu.SemaphoreType.DMA],
    )(small)


def _sibling_exchange(arrs):
    n = len(arrs)

    def body(*refs):
        ins, outs = refs[:n], refs[n:2 * n]
        send_sems, recv_sems = refs[2 * n:]
        x, y, c = _mesh_pos()
        cps = [pltpu.make_async_remote_copy(
            src_ref=ins[w], dst_ref=outs[w], send_sem=send_sems.at[w], recv_sem=recv_sems.at[w],
            device_id=(x, y, 1 - c), device_id_type=MESH) for w in range(n)]
        for cp in cps:
            cp.start()
        for cp in cps:
            cp.wait_recv()
        for cp in cps:
            cp.wait_send()

    anyspec = pl.BlockSpec(memory_space=pl.ANY)
    return pl.pallas_call(
        body, name="sibling_exchange",
        in_specs=[anyspec] * n, out_specs=[anyspec] * n,
        out_shape=[jax.ShapeDtypeStruct(a.shape, a.dtype) for a in arrs],
        scratch_shapes=[pltpu.SemaphoreType.DMA((n,)), pltpu.SemaphoreType.DMA((n,))],
    )(*arrs)


def _sum_stack(own, recv, name):
    _, R, C = recv.shape
    tr = _row_tile(R, 256, 16)

    def body(g_ref, r_ref, o_ref):
        x, y, _ = _mesh_pos()
        kme = 2 * x + y
        acc = g_ref[kme].astype(F32)
        for d in range(1, N_CHIP):
            acc = acc + r_ref[(kme + d) % N_CHIP].astype(F32)
        o_ref[...] = acc

    spec = pl.BlockSpec((N_CHIP, tr, C), lambda i: (0, i, 0))
    return pl.pallas_call(
        body, name=name, grid=(R // tr,), in_specs=[spec, spec],
        out_specs=pl.BlockSpec((tr, C), lambda i: (i, 0)),
        out_shape=jax.ShapeDtypeStruct((R, C), F32),
        compiler_params=_params(("parallel",)),
    )(own, recv)


def _adam_math(w, g, m, v):
    m2 = ADAM_B1 * m + (1.0 - ADAM_B1) * g
    v2 = ADAM_B2 * v + (1.0 - ADAM_B2) * (g * g)
    m_hat = m2 / (1.0 - ADAM_B1 ** ADAM_STEP)
    v_hat = v2 / (1.0 - ADAM_B2 ** ADAM_STEP)
    delta = -ADAM_LR * (m_hat / (jnp.sqrt(v_hat) + ADAM_EPS) + ADAM_WD * w)
    return delta, m2, v2


def _adamw(w, m, v, s0, s1, name):
    R, C = w.shape
    tr = _row_tile(R, 256, 8)

    def body(w_ref, m_ref, v_ref, a_ref, b_ref, g_ref, d_ref, m2_ref, v2_ref):
        g = a_ref[...] + b_ref[...]
        delta, m2, v2 = _adam_math(w_ref[...], g, m_ref[...], v_ref[...])
        g_ref[...] = g
        d_ref[...] = delta
        m2_ref[...] = m2
        v2_ref[...] = v2

    spec = pl.BlockSpec((tr, C), lambda i: (i, 0))
    shp = jax.ShapeDtypeStruct((R, C), F32)
    return pl.pallas_call(
        body, name=name, grid=(R // tr,), in_specs=[spec] * 5, out_specs=[spec] * 4, out_shape=[shp] * 4,
        compiler_params=_params(("parallel",), VMEM_BIG),
    )(w, m, v, s0, s1)


def _adamw_small(ws, ms, vs, gathered):
    n = len(SMALL)

    def body(*refs):
        w_refs, m_refs, v_refs, s_ref = refs[:n], refs[n:2 * n], refs[2 * n:3 * n], refs[3 * n]
        outs = refs[3 * n + 1:]
        g_all = s_ref[0]
        for d in range(1, 8):
            g_all = g_all + s_ref[d]
        off = 0
        for i, (_, width) in enumerate(SMALL):
            g = g_all[:, off:off + width]
            delta, m2, v2 = _adam_math(w_refs[i][...], g, m_refs[i][...], v_refs[i][...])
            for kind, val in enumerate((g, delta, m2, v2)):
                outs[kind * n + i][...] = val
            off += width + (-width % LANE)

    shapes = [jax.ShapeDtypeStruct((1, width), F32) for _, width in SMALL]
    res = pl.pallas_call(body, name="adamw_small", out_shape=shapes * 4)(*ws, *ms, *vs, gathered)
    return [dict(zip([nm for nm, _ in SMALL], res[kind * n:(kind + 1) * n])) for kind in range(4)]


SMALL = (("g_mix", 1024), ("g_ffn", 1024), ("g_ret_norm", 512), ("g_fox_q", 64), ("g_fox_k", 64), ("b_forget", 8))
SMALL_W = 3072


def _pack_small(parts):
    cols = []
    for (name, n) in SMALL:
        p = parts[name].reshape(1, -1)[:, :n]
        pad = -n % LANE
        cols.append(jnp.pad(p, ((0, 0), (0, pad))) if pad else p)
    used = sum(c.shape[1] for c in cols)
    cols.append(jnp.zeros((1, SMALL_W - used), F32))
    return jnp.concatenate(cols, axis=1)


def kernel(x, g_mix, w_in, b_forget, g_ret_norm, w_ret_o, g_fox_q, g_fox_k, w_fox_o, w_out, g_ffn, w_gate, w_up, w_down, loss_target, m_g_mix, m_w_in, m_b_forget, m_g_ret_norm, m_w_ret_o, m_g_fox_q, m_g_fox_k, m_w_fox_o, m_w_out, m_g_ffn, m_w_gate, m_w_up, m_w_down, v_g_mix, v_w_in, v_b_forget, v_g_ret_norm, v_w_ret_o, v_g_fox_q, v_g_fox_k, v_w_fox_o, v_w_out, v_g_ffn, v_w_gate, v_w_up, v_w_down):
    T = x.shape[1]
    xs = x[0]
    tgt = loss_target[0]
    big_names = ("w_in", "w_ret_o", "w_fox_o", "w_out", "w_gate", "w_up", "w_down")
    tr = lambda a: jnp.swapaxes(a[0], 0, 1)
    big_w = dict(w_in=w_in[0], w_ret_o=w_ret_o[0], w_fox_o=w_fox_o[0], w_out=w_out[0], w_gate=tr(w_gate),
                 w_up=tr(w_up), w_down=w_down[0])
    big_m = dict(w_in=m_w_in[0], w_ret_o=m_w_ret_o[0], w_fox_o=m_w_fox_o[0], w_out=m_w_out[0], w_gate=tr(m_w_gate),
                 w_up=tr(m_w_up), w_down=m_w_down[0])
    big_v = dict(w_in=v_w_in[0], w_ret_o=v_w_ret_o[0], w_fox_o=v_w_fox_o[0], w_out=v_w_out[0], w_gate=tr(v_w_gate),
                 w_up=tr(v_w_up), w_down=v_w_down[0])
    small_w = dict(g_mix=g_mix, g_ffn=g_ffn, g_ret_norm=g_ret_norm, g_fox_q=g_fox_q, g_fox_k=g_fox_k, b_forget=b_forget)
    small_m = dict(g_mix=m_g_mix, g_ffn=m_g_ffn, g_ret_norm=m_g_ret_norm, g_fox_q=m_g_fox_q, g_fox_k=m_g_fox_k,
                   b_forget=m_b_forget)
    small_v = dict(g_mix=v_g_mix, g_ffn=v_g_ffn, g_ret_norm=v_g_ret_norm, g_fox_q=v_g_fox_q, g_fox_k=v_g_fox_k,
                   b_forget=v_b_forget)

    stacks = {n: _staged_place(big_w[n], "place_" + n) for n in big_names}
    s_in = _gather_two_level(stacks["w_in"], "gather_w_in")
    w_a, w_ff = _assemble_w_in(s_in)
    b_pad = jnp.pad(b_forget, ((0, 0), (0, LANE - FOX_H)))
    cos_t, sin_t = _rope_tables(T)
    consts = _ret_consts()

    h = _rms_cast(xs, g_mix)
    z_a, (s_ro, s_fo, s_out, s_gate) = _mm_nn(
        h, w_a, "proj_in", BF, tm=1024,
        push=(None, [stacks["w_ret_o"], stacks["w_fox_o"], stacks["w_out"], stacks["w_gate"]]))
    (qr, kr, qf, kf, vf, c_cum, nmax, z_ff), (s_up,) = _mix_prep(
        z_a, h, w_ff, cos_t, sin_t, b_pad, g_fox_q, g_fox_k, push=(None, [stacks["w_up"]]))
    jlo, end_both, end_last, tame = _prune_tables(c_cum, nmax, FOX_SUB)
    o_raw, u_r, states = _ret_fwd(qr, kr, z_a, g_ret_norm, consts)
    o_fox, q2 = _fox_fwd(jlo, tame, qf, kf, vf)
    (y_r, y_f, mrg, x2, h2, o_cat), (s_down,) = _merge_out(u_r, o_fox, z_a, xs, g_ffn, s_ro, s_fo, s_out,
                                                            push=(None, [stacks["w_down"]]))
    sa, sb, act, dy, loss_vec = _ffn_fwd(h2, x2, tgt, s_gate, s_up, s_down)
    loss = lax.psum(0.5 / D_MODEL * jnp.sum(loss_vec), ("x", "y", "c"))

    def scatter_job(grads):
        return (grads, [lax.empty(g.shape, g.dtype) for g in grads])

    dgp, dup, dx2, dg_ffn = _ffn_bwd(dy, sa, sb, x2, g_ffn, s_gate, s_up, s_down)
    (g_gate, _), (g_up, _), (g_down, _) = (_grad_astack(dgp, h2, "gw_gate"), _grad_astack(dup, h2, "gw_up"),
                                           _grad_astack(act, dy, "gw_down"))
    (d_yr, d_yf, dz_gt, dz_a, d_o, do_fox, dg_ret), (r_gate, r_up) = _out_bwd(
        dx2, z_a, y_r, y_f, o_raw, o_fox, g_ret_norm, s_ro, s_fo, s_out, push=scatter_job([g_gate, g_up]))
    dz_ret, (r_down,) = _ret_bwd(d_o, qr, kr, z_a, states, cos_t, sin_t, consts, push=scatter_job([g_down]))
    dq_f, dk_f, dv_f = _fox_bwd(end_both, end_last, q2, kf, vf, do_fox)
    g_mid = [_grad_colstack(u_r, d_yr, "gw_ret_o", 256), _grad_colstack(o_cat, d_yf, "gw_fox_o", 256),
             _grad_plain(mrg, dx2, "gw_out", BF).reshape(N_CHIP, 256, D_MODEL)]
    (dz_fox, dz_ff, dg_q, dg_k, db_f), (r_ro, r_fo, r_out) = _fox_post_bwd(
        dq_f, dk_f, dv_f, z_a, z_ff, b_pad, g_fox_q, g_fox_k, push=scatter_job(g_mid))
    gi_ret, gi_gt, gi_ff = _grad_multi(h, [dz_ret, dz_gt, dz_ff], "gw_in_small")
    gi_fox, gi_a = _grad_multi(h, [dz_fox, dz_a], "gw_in_large")
    g_in = _pack_g_in(gi_ret, gi_gt, gi_fox, gi_a, gi_ff)
    (grad_x, dg_mix), (r_in,) = _in_bwd(dz_ret, dz_gt, dz_fox, dz_a, dz_ff, w_a, w_ff, xs, g_mix, dx2,
                                        push=scatter_job([g_in]))
    small_g = _pack_small(dict(g_mix=dg_mix, g_ffn=dg_ffn, g_ret_norm=dg_ret, g_fox_q=dg_q, g_fox_k=dg_k, b_forget=db_f))

    small_all = _gather_small(small_g)
    sums = [_sum_stack(g, r, "sum_" + n) for g, r, n in zip(
        [g_in] + g_mid + [g_gate, g_up, g_down], [r_in, r_ro, r_fo, r_out, r_gate, r_up, r_down], big_names)]
    sib = _sibling_exchange(sums)
    big_out = {n: _adamw(big_w[n], big_m[n], big_v[n], sums[i], sib[i], "adamw_" + n) for i, n in enumerate(big_names)}
    small_out = _adamw_small(*[[d[nm] for nm, _ in SMALL] for d in (small_w, small_m, small_v)], small_all)

    order = ("g_mix", "w_in", "b_forget", "g_ret_norm", "w_ret_o", "g_fox_q", "g_fox_k", "w_fox_o", "w_out", "g_ffn",
             "w_gate", "w_up", "w_down")
    outs = [loss, grad_x[None]]
    for idx in range(4):
        for n in order:
            if n in ("w_gate", "w_up"):
                outs.append(jnp.swapaxes(big_out[n][idx], 0, 1)[None])
            else:
                outs.append(big_out[n][idx][None] if n in big_out else small_out[idx][n])
    return tuple(outs)
```

```python
import functools

import numpy as np
import jax
import jax.numpy as jnp
from jax import lax
from jax.experimental import pallas as pl
from jax.experimental.pallas import tpu as pltpu

F32 = jnp.float32
BF = jnp.bfloat16
MESH = pl.DeviceIdType.MESH

D_MODEL = 1024
D_FF = 2816
N_CHIP = 4
FF_SH = D_FF // N_CHIP
IN_COLS = 5128
IN_SH = IN_COLS // N_CHIP
RET_H, RET_DV = 4, 128
FOX_H, FOX_D = 8, 64
CHUNK = 256
EPS = 1e-6
NEG = -1e30
LANE = 128
C_RET, C_GT, C_FOX, C_A, C_END = 0, 1024, 1536, 3072, 5120
L_CQ, L_CK, L_LSE, L_MAX = 64, 67, 70, 73

ADAM_LR, ADAM_B1, ADAM_B2, ADAM_EPS, ADAM_WD, ADAM_STEP = 0.001, 0.9, 0.999, 1e-08, 0.01, 10
VMEM_BIG = 56 * 1024 * 1024
VMEM_HUGE = 60 * 1024 * 1024
GRAD_TK = 2048
FFN_TM = 512
FOX_SUB = 512


def _nn(a, b):
    return lax.dot_general(a, b, (((1,), (0,)), ((), ())), preferred_element_type=F32)


def _nt(a, b):
    return lax.dot_general(a, b, (((1,), (1,)), ((), ())), preferred_element_type=F32)


def _tn(a, b):
    return lax.dot_general(a, b, (((0,), (0,)), ((), ())), preferred_element_type=F32)


def _split3(x):
    hi = x.astype(BF)
    r = x - hi.astype(F32)
    mid = r.astype(BF)
    lo = (r - mid.astype(F32)).astype(BF)
    return hi, mid, lo


def _sigmoid(x):
    return 0.5 * jnp.tanh(0.5 * x) + 0.5


def _swap32(x):
    lane = lax.broadcasted_iota(jnp.int32, x.shape, 1)
    return jnp.where(lane < 32, pltpu.roll(x, 96, 1), pltpu.roll(x, 32, 1))


def _params(sem, vmem=None):
    return pltpu.CompilerParams(dimension_semantics=sem, vmem_limit_bytes=vmem)


def _row_tile(rows, cap, mult):
    return max(d for d in range(mult, cap + 1, mult) if rows % d == 0)


def _assemble_w_in(stack, tr=256):
    def body(s_ref, a_ref, f_ref):
        full = jnp.concatenate([s_ref[k].astype(F32) for k in range(N_CHIP)], axis=-1)
        a_ref[...] = jnp.concatenate([full[:, :3072], full[:, 3080:IN_COLS]], axis=-1).astype(BF)
        f_ref[...] = jnp.concatenate([full[:, 3072:3080], jnp.zeros((tr, LANE - FOX_H), F32)], axis=-1).astype(BF)

    return pl.pallas_call(
        body, name="assemble_w_in", grid=(D_MODEL // tr,),
        in_specs=[pl.BlockSpec((N_CHIP, tr, IN_SH), lambda i: (0, i, 0))],
        out_specs=[pl.BlockSpec((tr, C_END), lambda i: (i, 0)), pl.BlockSpec((tr, LANE), lambda i: (i, 0))],
        out_shape=[jax.ShapeDtypeStruct((D_MODEL, C_END), BF), jax.ShapeDtypeStruct((D_MODEL, LANE), BF)],
        compiler_params=_params(("parallel",), VMEM_BIG),
    )(stack)


def _pack_g_in(g_ret, g_gt, g_fox, g_a, g_ff, tr=256):
    def body(r_ref, t_ref, x_ref, a_ref, f_ref, o_ref):
        r, t, x, a, f = [ref[...].astype(F32) for ref in (r_ref, t_ref, x_ref, a_ref, f_ref)]
        full = jnp.concatenate([r, t, x, f[:, :FOX_H], a], axis=-1)
        for k in range(N_CHIP):
            o_ref[k] = full[:, k * IN_SH:(k + 1) * IN_SH].astype(BF)

    def spec(w):
        return pl.BlockSpec((tr, w), lambda i: (i, 0))

    return pl.pallas_call(
        body, name="pack_g_in", grid=(D_MODEL // tr,),
        in_specs=[spec(1024), spec(512), spec(1536), spec(2048), spec(LANE)],
        out_specs=pl.BlockSpec((N_CHIP, tr, IN_SH), lambda i: (0, i, 0)),
        out_shape=jax.ShapeDtypeStruct((N_CHIP, D_MODEL, IN_SH), BF),
        compiler_params=_params(("parallel",), VMEM_BIG),
    )(g_ret, g_gt, g_fox, g_a, g_ff)


def _rms_cast(x, g, tm=512):
    T = x.shape[0]

    def body(x_ref, g_ref, o_ref):
        xv = x_ref[...]
        r = lax.rsqrt(jnp.mean(xv * xv, axis=-1, keepdims=True) + EPS)
        o_ref[...] = (xv * r * g_ref[...]).astype(BF)

    return pl.pallas_call(
        body, name="rms_cast", grid=(T // tm,),
        in_specs=[pl.BlockSpec((tm, D_MODEL), lambda i: (i, 0)), pl.BlockSpec((1, D_MODEL), lambda i: (0, 0))],
        out_specs=pl.BlockSpec((tm, D_MODEL), lambda i: (i, 0)),
        out_shape=jax.ShapeDtypeStruct((T, D_MODEL), BF),
        compiler_params=_params(("parallel",)),
    )(x, g)


def _hosted_call(body, name, grid, in_specs, out_specs, out_shape, scratch_shapes, vmem, args, push):
    sem = ("arbitrary",) * len(grid)
    if push is None:
        res = pl.pallas_call(body, name=name, grid=grid, in_specs=in_specs, out_specs=out_specs, out_shape=out_shape,
                             scratch_shapes=scratch_shapes, compiler_params=_params(sem, vmem))(*args)
        return list(res), []
    srcs, lands = push
    ns, nl, n_in, n_out = (0 if srcs is None else len(srcs)), len(lands), len(in_specs), len(out_specs)
    n_scr = len(scratch_shapes)

    def wrapped(*refs):
        pos = n_in + ns + nl
        ins, x_in = refs[:n_in], refs[n_in:pos]
        outs, x_out = refs[pos:pos + n_out], refs[pos + n_out:pos + n_out + nl]
        scr = refs[pos + n_out + nl:pos + n_out + nl + n_scr]
        ssem, rsem = refs[-2], refs[-1]
        src = None if srcs is None else x_in[:ns]
        ids = [pl.program_id(a) for a in range(len(grid))]
        first = functools.reduce(lambda p, q: p & q, [ids[a] == 0 for a in range(len(grid))])
        last = functools.reduce(lambda p, q: p & q, [ids[a] == grid[a] - 1 for a in range(len(grid))])

        @pl.when(first)
        def _():
            for cp in _push_copies(src, x_out, ssem, rsem, False):
                cp.start()

        body(*ins, *outs, *scr)

        @pl.when(last)
        def _():
            for cp in _push_copies(src, x_out, ssem, rsem, True):
                cp.wait_recv()
                cp.wait_send()

    anyspec = pl.BlockSpec(memory_space=pl.ANY)
    extra = ([] if srcs is None else list(srcs)) + list(lands)
    res = pl.pallas_call(
        wrapped, name=name, grid=grid,
        in_specs=list(in_specs) + [anyspec] * len(extra), out_specs=list(out_specs) + [anyspec] * nl,
        out_shape=list(out_shape) + [jax.ShapeDtypeStruct(a.shape, a.dtype) for a in lands],
        input_output_aliases={n_in + ns + i: n_out + i for i in range(nl)},
        scratch_shapes=list(scratch_shapes) + [pltpu.SemaphoreType.DMA((3 * nl,)), pltpu.SemaphoreType.DMA((3 * nl,))],
        compiler_params=_params(sem, vmem),
    )(*args, *extra)
    return list(res[:n_out]), list(res[n_out:])


def _mm_nn(a, b, name, out_dtype, tm=512, tn=1024, push=None):
    M, K = a.shape
    N = b.shape[1]
    tn = min(tn, N)

    def body(a_ref, b_ref, o_ref):
        o_ref[...] = _nn(a_ref[...], b_ref[...]).astype(o_ref.dtype)

    (out,), lands = _hosted_call(
        body, name, (N // tn, M // tm),
        [pl.BlockSpec((tm, K), lambda j, i: (i, 0)), pl.BlockSpec((K, tn), lambda j, i: (0, j))],
        [pl.BlockSpec((tm, tn), lambda j, i: (i, j))], [jax.ShapeDtypeStruct((M, N), out_dtype)], [], None, (a, b), push)
    return out, lands


def _mm_tn(a, b, name, grid, a_spec, b_spec, o_spec, out_shape, acc_shape):
    nk = grid[-1]

    def body(a_ref, b_ref, o_ref, acc):
        k = pl.program_id(len(grid) - 1)

        @pl.when(k == 0)
        def _():
            acc[...] = jnp.zeros(acc.shape, F32)

        acc[...] += _tn(a_ref[...].astype(BF), b_ref[...].astype(BF))

        @pl.when(k == nk - 1)
        def _():
            o_ref[...] = acc[...].astype(o_ref.dtype)

    return pl.pallas_call(
        body, name=name, grid=grid, in_specs=[a_spec, b_spec], out_specs=o_spec, out_shape=out_shape,
        scratch_shapes=[pltpu.VMEM(acc_shape, F32)],
        compiler_params=_params(("parallel",) * (len(grid) - 1) + ("arbitrary",), VMEM_BIG),
    )(a, b)


def _grad_plain(a, b, name, out_dtype, tk=GRAD_TK, tn=1024):
    T, M = a.shape
    N = b.shape[1]
    tn = min(tn, N)
    return _mm_tn(a, b, name, (N // tn, T // tk),
                  pl.BlockSpec((tk, M), lambda j, k: (k, 0)), pl.BlockSpec((tk, tn), lambda j, k: (k, j)),
                  pl.BlockSpec((M, tn), lambda j, k: (0, j)), jax.ShapeDtypeStruct((M, N), out_dtype), (M, tn))


def _grad_multi(a, bs, name, tk=1024):
    T, M = a.shape
    n = len(bs)
    nk = T // tk

    def body(*refs):
        a_ref, b_refs, o_refs, accs = refs[0], refs[1:1 + n], refs[1 + n:1 + 2 * n], refs[1 + 2 * n:]
        k = pl.program_id(0)

        @pl.when(k == 0)
        def _():
            for acc in accs:
                acc[...] = jnp.zeros(acc.shape, F32)

        av = a_ref[...]
        for i in range(n):
            accs[i][...] += _tn(av, b_refs[i][...])

        @pl.when(k == nk - 1)
        def _():
            for i in range(n):
                o_refs[i][...] = accs[i][...].astype(BF)

    widths = [b.shape[1] for b in bs]
    return pl.pallas_call(
        body, name=name, grid=(nk,),
        in_specs=[pl.BlockSpec((tk, M), lambda k: (k, 0))] + [pl.BlockSpec((tk, w), lambda k: (k, 0)) for w in widths],
        out_specs=[pl.BlockSpec((M, w), lambda k: (0, 0)) for w in widths],
        out_shape=[jax.ShapeDtypeStruct((M, w), BF) for w in widths],
        scratch_shapes=[pltpu.VMEM((M, w), F32) for w in widths],
        compiler_params=_params(("arbitrary",), VMEM_BIG),
    )(a, *bs)


def _grad_colstack(a, b, name, wcol, tk=GRAD_TK):
    T, M = a.shape
    N = b.shape[1]
    S = N // wcol
    nk = T // tk

    def body(a_ref, b_ref, o_ref, acc):
        k = pl.program_id(0)

        @pl.when(k == 0)
        def _():
            acc[...] = jnp.zeros(acc.shape, F32)

        acc[...] += _tn(a_ref[...], b_ref[...])

        @pl.when(k == nk - 1)
        def _():
            for s in range(S):
                o_ref[s] = acc[:, s * wcol:(s + 1) * wcol].astype(BF)

    return pl.pallas_call(
        body, name=name, grid=(nk,),
        in_specs=[pl.BlockSpec((tk, M), lambda k: (k, 0)), pl.BlockSpec((tk, N), lambda k: (k, 0))],
        out_specs=pl.BlockSpec((S, M, wcol), lambda k: (0, 0, 0)), out_shape=jax.ShapeDtypeStruct((S, M, wcol), BF),
        scratch_shapes=[pltpu.VMEM((M, N), F32)], compiler_params=_params(("arbitrary",), VMEM_BIG),
    )(a, b)


def _grad_astack(a, b, name, tk=1024, push=None):
    S, T, m = a.shape
    N = b.shape[1]
    nk = T // tk

    def body(a_ref, b_ref, o_ref, acc):
        k = pl.program_id(0)

        @pl.when(k == 0)
        def _():
            acc[...] = jnp.zeros(acc.shape, F32)

        bb = b_ref[...].astype(BF)
        for s in range(S):
            acc[s] += _tn(a_ref[s], bb)

        @pl.when(k == nk - 1)
        def _():
            o_ref[...] = acc[...].astype(BF)

    (out,), lands = _hosted_call(
        body, name, (nk,),
        [pl.BlockSpec((S, tk, m), lambda k: (0, k, 0)), pl.BlockSpec((tk, N), lambda k: (k, 0))],
        [pl.BlockSpec((S, m, N), lambda k: (0, 0, 0))], [jax.ShapeDtypeStruct((S, m, N), BF)],
        [pltpu.VMEM((S, m, N), F32)], VMEM_BIG, (a, b), push)
    return out, lands


def _rope_tables(T):
    half = 32
    pos = np.arange(T, dtype=np.float32)
    inv_freq = (np.float32(1.0) / (np.float32(10000.0) ** (np.arange(half, dtype=np.float32) / np.float32(half)))).astype(np.float32)
    ang = (pos[:, None] * inv_freq[None, :]).astype(np.float32)
    cos, sin = np.cos(ang).astype(np.float32), np.sin(ang).astype(np.float32)
    z = np.zeros((T, 64), np.float32)
    return (jnp.asarray(np.concatenate([cos, cos, z], axis=-1)), jnp.asarray(np.concatenate([-sin, sin, z], axis=-1)))


def _ret_consts():
    h = np.arange(RET_H, dtype=np.float32)
    log_g = np.log1p(-(np.float32(2.0) ** (-5.0 - h))).astype(np.float32)
    idx = np.arange(CHUNK, dtype=np.float32)
    diff = idx[:, None] - idx[None, :]
    decay = np.where(diff[None] >= 0, np.exp(np.maximum(diff, 0.0)[None] * log_g[:, None, None]), 0.0)
    zeta = np.exp((CHUNK - 1.0 - idx)[None, :] * log_g[:, None])
    xi = np.exp((idx + 1.0)[None, :] * log_g[:, None])
    gc = np.exp(CHUNK * log_g)
    bc = lambda v: np.broadcast_to(v[:, :, None], (RET_H, CHUNK, LANE)).astype(np.float32)
    gcb = np.broadcast_to(gc[:, None, None], (RET_H, LANE, LANE)).astype(np.float32)
    return (jnp.asarray(decay.astype(np.float32)), jnp.asarray(bc(zeta)), jnp.asarray(bc(xi)), jnp.asarray(gcb))


def _mix_prep(z_a, h, w_ff, cos_t, sin_t, b_f, g_q, g_k, tm=256, push=None):
    T = z_a.shape[0]

    def body(zqk_ref, zf_ref, h_ref, wff_ref, cos_ref, sin_ref, b_ref, g_ref, seg_ref, segt_ref,
             qr_ref, kr_ref, qf_ref, kf_ref, vf_ref, c_ref, nmax_ref, zff_ref, carry):
        i = pl.program_id(0)
        zff = _nn(h_ref[...], wff_ref[...])
        zff_ref[...] = zff

        @pl.when(i == 0)
        def _():
            carry[...] = jnp.zeros(carry.shape, F32)
            nmax_ref[...] = jnp.zeros(nmax_ref.shape, F32)

        lane = lax.broadcasted_iota(jnp.int32, (tm, LANE), 1)
        zpad = jnp.zeros((tm, 64), F32)
        cosv, sinv = cos_ref[...], sin_ref[...]
        zqk = zqk_ref[...].astype(F32)
        for h in range(RET_H):
            for src, dst, scale in ((0, qr_ref, 1.0), (256, kr_ref, 0.125)):
                xh = jnp.concatenate([zqk[:, src + 64 * h: src + 64 * h + 64], zpad], axis=-1)
                rot = xh * cosv + _swap32(xh) * sinv
                dst[h] = (rot * scale).astype(BF)

        lf_in = zff + b_ref[...]
        logf = jnp.minimum(lf_in, 0.0) - jnp.log(1.0 + jnp.exp(-jnp.abs(lf_in)))
        row = lax.broadcasted_iota(jnp.int32, (tm, tm), 0)
        col = lax.broadcasted_iota(jnp.int32, (tm, tm), 1)
        tri = (row >= col).astype(BF)
        hi, mid, lo = _split3(logf)
        cs = _nn(tri, hi) + _nn(tri, mid) + _nn(tri, lo) + carry[...]
        carry[...] = cs[tm - 1:tm, :]
        c_ref[...] = cs

        def seg_sum(v):
            return sum(_nn(t, seg_ref[...]) for t in _split3(v))

        zf = zf_ref[...].astype(F32)
        xqk = zf[:, :1024]
        rinv = lax.rsqrt(seg_sum(xqk * xqk) * (1.0 / FOX_D) + EPS)
        xn = xqk * sum(_nn(t, segt_ref[...]) for t in _split3(rinv)) * g_ref[...]
        nmax_ref[...] = jnp.maximum(nmax_ref[...], jnp.max(seg_sum(xn * xn), axis=0, keepdims=True))

        one = jnp.ones((tm, LANE), F32)
        for h in range(FOX_H):
            c = cs[:, h:h + 1]
            chi, cmid, clo = [t.astype(F32) for t in _split3(c)]
            qn = xn[:, 64 * h:64 * h + 64]
            kn = xn[:, 512 + 64 * h:512 + 64 * h + 64]
            vh = zf[:, 1024 + 64 * h:1024 + 64 * h + 64]
            qa = jnp.concatenate([qn, zpad], axis=-1)
            qa = jnp.where(lane == L_CQ, chi, jnp.where(lane == L_CQ + 1, cmid, jnp.where(lane == L_CQ + 2, clo, qa)))
            qa = jnp.where((lane >= L_CK) & (lane < L_CK + 3), one, qa)
            ka = jnp.concatenate([kn, zpad], axis=-1)
            ka = jnp.where(lane == L_CK, -chi, jnp.where(lane == L_CK + 1, -cmid, jnp.where(lane == L_CK + 2, -clo, ka)))
            ka = jnp.where(((lane >= L_CQ) & (lane < L_CQ + 3)) | ((lane >= L_LSE) & (lane < L_MAX + 3)), one, ka)
            va = jnp.concatenate([vh, zpad], axis=-1)
            va = jnp.where((lane >= 64) & (lane < 67), one, va)
            qf_ref[h] = qa.astype(BF)
            kf_ref[h] = ka.astype(BF)
            vf_ref[h] = va.astype(BF)

    hspec4 = pl.BlockSpec((RET_H, tm, LANE), lambda i: (0, i, 0))
    hspec8 = pl.BlockSpec((FOX_H, tm, LANE), lambda i: (0, i, 0))
    const = lambda r, w: pl.BlockSpec((r, w), lambda i: (0, 0))
    seg = _segment_matrix()
    g_all = jnp.concatenate([jnp.tile(g_q * 0.125, (1, FOX_H)), jnp.tile(g_k, (1, FOX_H))], axis=1)
    return _hosted_call(
        body, "mix_prep", (T // tm,),
        [pl.BlockSpec((tm, 512), lambda i: (i, 0)), pl.BlockSpec((tm, 1536), lambda i: (i, 1)),
         pl.BlockSpec((tm, D_MODEL), lambda i: (i, 0)), const(D_MODEL, LANE), pl.BlockSpec((tm, LANE), lambda i: (i, 0)),
         pl.BlockSpec((tm, LANE), lambda i: (i, 0)), const(1, LANE), const(1, 1024), const(1024, LANE), const(LANE, 1024)],
        [hspec4, hspec4, hspec8, hspec8, hspec8, pl.BlockSpec((tm, LANE), lambda i: (i, 0)), const(1, LANE),
         pl.BlockSpec((tm, LANE), lambda i: (i, 0))],
        [jax.ShapeDtypeStruct((RET_H, T, LANE), BF)] * 2 + [jax.ShapeDtypeStruct((FOX_H, T, LANE), BF)] * 3
        + [jax.ShapeDtypeStruct((T, LANE), F32), jax.ShapeDtypeStruct((1, LANE), F32), jax.ShapeDtypeStruct((T, LANE), F32)],
        [pltpu.VMEM((1, LANE), F32)], VMEM_BIG, (z_a, z_a, h, w_ff, cos_t, sin_t, b_f, g_all, seg, seg.T), push)


def _segment_matrix():
    m = np.zeros((2 * FOX_H * FOX_D, LANE), np.float32)
    m[np.arange(2 * FOX_H * FOX_D), np.arange(2 * FOX_H * FOX_D) // FOX_D] = 1.0
    return jnp.asarray(m, dtype=BF)


def _ret_fwd(qr, kr, z_a, g_ret, consts, tt=512):
    T = z_a.shape[0]
    nch = tt // CHUNK
    decay, zeta, xi, gcb = consts

    def body(q_ref, k_ref, v_ref, gt_ref, g_ref, d_ref, ze_ref, xi_ref, gc_ref, o_ref, u_ref, st_ref, r_sc):
        i = pl.program_id(0)

        @pl.when(i == 0)
        def _():
            r_sc[...] = jnp.zeros(r_sc.shape, F32)

        for c in range(nch):
            rows = slice(c * CHUNK, (c + 1) * CHUNK)
            for h in range(RET_H):
                cols = slice(h * RET_DV, (h + 1) * RET_DV)
                q, k = q_ref[h, rows, :], k_ref[h, rows, :]
                v32 = v_ref[rows, cols].astype(F32)
                r = r_sc[h]
                st_ref[h, c * CHUNK:c * CHUNK + LANE, :] = r
                s = _nt(q, k) * d_ref[h]
                o = _nn(s.astype(BF), v32.astype(BF)) + _nn(q, r.astype(BF)) * xi_ref[h]
                r_sc[h] = gc_ref[h] * r + _tn(k, (v32 * ze_ref[h]).astype(BF))
                o_ref[rows, cols] = o
                mu = jnp.mean(o, axis=-1, keepdims=True)
                xc = o - mu
                on = xc * lax.rsqrt(jnp.mean(xc * xc, axis=-1, keepdims=True) + EPS)
                gt = gt_ref[rows, cols].astype(F32)
                u_ref[rows, cols] = (gt * _sigmoid(gt) * (on * g_ref[:, cols])).astype(BF)

    hspec = pl.BlockSpec((RET_H, tt, LANE), lambda i: (0, i, 0))
    cspec = pl.BlockSpec((RET_H, CHUNK, LANE), lambda i: (0, 0, 0))
    dspec = pl.BlockSpec((RET_H, CHUNK, CHUNK), lambda i: (0, 0, 0))
    sspec = pl.BlockSpec((RET_H, LANE, LANE), lambda i: (0, 0, 0))
    return pl.pallas_call(
        body, name="ret_fwd", grid=(T // tt,),
        in_specs=[hspec, hspec, pl.BlockSpec((tt, 512), lambda i: (i, 1)), pl.BlockSpec((tt, 512), lambda i: (i, 2)),
                  pl.BlockSpec((1, 512), lambda i: (0, 0)), dspec, cspec, cspec, sspec],
        out_specs=[pl.BlockSpec((tt, 512), lambda i: (i, 0)), pl.BlockSpec((tt, 512), lambda i: (i, 0)), hspec],
        out_shape=[jax.ShapeDtypeStruct((T, 512), F32), jax.ShapeDtypeStruct((T, 512), BF),
                   jax.ShapeDtypeStruct((RET_H, T, LANE), F32)],
        scratch_shapes=[pltpu.VMEM((RET_H, LANE, LANE), F32)],
        compiler_params=_params(("arbitrary",), VMEM_BIG),
    )(qr, kr, z_a, z_a, g_ret, decay, zeta, xi, gcb)


PRUNE_LOG = -110.0
TAME_LOGIT_SPAN = 60.0


def _prune_tables(c, nmax, sub):
    n = c.shape[0] // sub
    u = jnp.sqrt(nmax[0, :FOX_H] * nmax[0, FOX_H:2 * FOX_H]) * 1.02 + 0.5
    first = c[0::sub, :FOX_H].T
    last = c[sub - 1::sub, :FOX_H].T
    blk = jnp.arange(n, dtype=jnp.int32)
    needed = (2.0 * u[:, None, None] + first[:, :, None] - last[:, None, :] >= PRUNE_LOG) | (blk[None, :] >= blk[:, None])[None]
    jlo = jnp.argmax(needed, axis=2).astype(jnp.int32)

    def end_of(key_block):
        reach = jlo[:, None, :] <= key_block[None, :, None]
        return (n - jnp.argmax(reach[:, :, ::-1], axis=2)).astype(jnp.int32)

    sup = 2 * jnp.arange(n // 2, dtype=jnp.int32)
    end_last = end_of(sup + 1)
    end_both = jnp.clip(end_of(sup), sup[None, :] + 2, end_last)
    tame = (2.0 * u < TAME_LOGIT_SPAN).astype(jnp.int32)
    return jlo, end_both, end_last, tame


def _fox_fwd(jlo, tame, q, k, v, sub=FOX_SUB):
    H, T, _ = q.shape
    tb = 2 * sub

    def body(js_ref, tame_ref, q_ref, k_ref, v_ref, o_ref, q2_ref, mx_sc, acc_sc):
        i = pl.program_id(1)
        hd = pl.program_id(0)
        starts = [jnp.minimum(js_ref[hd, 2 * i], 2 * i), jnp.minimum(js_ref[hd, 2 * i + 1], 2 * i)]
        lane = lax.broadcasted_iota(jnp.int32, (sub, LANE), 1)
        row = lax.broadcasted_iota(jnp.int32, (sub, sub), 0)
        col = lax.broadcasted_iota(jnp.int32, (sub, sub), 1)
        causal = row >= col
        qs = [q_ref[0:sub, :], q_ref[sub:tb, :]]
        d0 = pl.multiple_of(i * tb, tb)
        d1 = pl.multiple_of(i * tb + sub, sub)
        k0, k1 = k_ref[pl.ds(d0, sub), :], k_ref[pl.ds(d1, sub), :]
        v0, v1 = v_ref[pl.ds(d0, sub), :], v_ref[pl.ds(d1, sub), :]

        def lane_max(s):
            m = s[:, 0:LANE]
            for c in range(1, s.shape[1] // LANE):
                m = jnp.maximum(m, s[:, c * LANE:(c + 1) * LANE])
            return m

        def put3(base, first, val):
            hi, mid, lo = _split3(val)
            return jnp.where(lane == first, hi, jnp.where(lane == first + 1, mid, jnp.where(lane == first + 2, lo, base)))

        def row_max():
            mx_sc[...] = jnp.full(mx_sc.shape, NEG, F32)
            for a in range(2):
                def max_body(j, carry, a=a):
                    kb = k_ref[pl.ds(pl.multiple_of(j * sub, sub), sub), :]
                    mx_sc[a] = jnp.maximum(mx_sc[a], lane_max(_nt(qs[a], kb)))
                    return carry

                lax.fori_loop(starts[a], 2 * i, max_body, 0)
            mx = [jnp.maximum(mx_sc[0], lane_max(jnp.where(causal, _nt(qs[0], k0), NEG))),
                  jnp.maximum(jnp.maximum(mx_sc[1], lane_max(_nt(qs[1], k0))),
                              lane_max(jnp.where(causal, _nt(qs[1], k1), NEG)))]
            return [jnp.max(t, axis=1, keepdims=True) for t in mx]

        def diag_logit():
            return [jnp.sum(qs[a].astype(F32) * kd.astype(F32), axis=1, keepdims=True) for a, kd in enumerate((k0, k1))]

        def finish(ms):
            qm = [put3(qs[a], L_MAX, -ms[a]) for a in range(2)]
            acc_sc[...] = jnp.zeros(acc_sc.shape, F32)
            for a in range(2):
                def acc_body(j, carry, a=a):
                    off = pl.multiple_of(j * sub, sub)
                    acc_sc[a] += _nn(jnp.exp(_nt(qm[a], k_ref[pl.ds(off, sub), :])).astype(BF), v_ref[pl.ds(off, sub), :])
                    return carry

                lax.fori_loop(starts[a], 2 * i, acc_body, 0)

            def pv(qa, kk, vv, masked):
                p = jnp.exp(_nt(qa, kk))
                if masked:
                    p = jnp.where(causal, p, 0.0)
                return _nn(p.astype(BF), vv)

            accs = [acc_sc[0] + pv(qm[0], k0, v0, True),
                    acc_sc[1] + pv(qm[1], k0, v0, False) + pv(qm[1], k1, v1, True)]
            for a in range(2):
                rows = slice(a * sub, (a + 1) * sub)
                l = accs[a][:, 64:65]
                o_ref[rows, :] = jnp.where(lane < 64, accs[a] / l, 0.0)
                q2_ref[rows, :] = put3(qs[a], L_LSE, -(ms[a] + jnp.log(l)))

        tame = tame_ref[hd] == 1

        @pl.when(tame)
        def _():
            finish(diag_logit())

        @pl.when(jnp.logical_not(tame))
        def _():
            finish(row_max())

    blk = pl.BlockSpec((None, tb, LANE), lambda h, i, js, tm_: (h, i, 0))
    full = pl.BlockSpec((None, T, LANE), lambda h, i, js, tm_: (h, 0, 0))
    return pl.pallas_call(
        body, name="fox_fwd",
        grid_spec=pltpu.PrefetchScalarGridSpec(
            num_scalar_prefetch=2, grid=(H, T // tb), in_specs=[blk, full, full], out_specs=[blk, blk],
            scratch_shapes=[pltpu.VMEM((2, sub, LANE), F32), pltpu.VMEM((2, sub, LANE), F32)]),
        out_shape=[jax.ShapeDtypeStruct((H, T, LANE), F32), jax.ShapeDtypeStruct((H, T, LANE), BF)],
        compiler_params=_params(("parallel", "arbitrary"), VMEM_BIG),
    )(jlo, tame, q, k, v)


def _merge_out(u_r, o_fox, z_a, x, g_ffn, w_ro, w_fo, w_out, tm=512, push=None):
    T = x.shape[0]

    def body(u_ref, of_ref, ar_ref, af_ref, x_ref, g_ref, wro_ref, wfo_ref, wout_ref,
             yr_ref, yf_ref, m_ref, x2_ref, h2_ref, oc_ref):
        u = u_ref[...]
        oc = jnp.concatenate([of_ref[h][:, :FOX_D] for h in range(FOX_H)], axis=-1).astype(BF)
        oc_ref[...] = oc
        yr = jnp.concatenate([_nn(u, wro_ref[k]) for k in range(N_CHIP)], axis=-1)
        yf = jnp.concatenate([_nn(oc, wfo_ref[k]) for k in range(N_CHIP)], axis=-1)
        yr_ref[...] = yr.astype(BF)
        yf_ref[...] = yf.astype(BF)
        m = (_sigmoid(ar_ref[...].astype(F32)) * yr + _sigmoid(af_ref[...].astype(F32)) * yf).astype(BF)
        m_ref[...] = m
        x2 = x_ref[...]
        for k in range(N_CHIP):
            x2 = x2 + _nn(m[:, 256 * k:256 * k + 256], wout_ref[k])
        x2_ref[...] = x2
        r = lax.rsqrt(jnp.mean(x2 * x2, axis=-1, keepdims=True) + EPS)
        h2_ref[...] = (x2 * r * g_ref[...]).astype(BF)

    row = lambda w: pl.BlockSpec((tm, w), lambda i: (i, 0))
    const = lambda shp: pl.BlockSpec(shp, lambda i: (0,) * len(shp))
    return _hosted_call(
        body, "merge_out", (T // tm,),
        [row(512), pl.BlockSpec((FOX_H, tm, LANE), lambda i: (0, i, 0)),
         pl.BlockSpec((tm, 1024), lambda i: (i, 3)), pl.BlockSpec((tm, 1024), lambda i: (i, 4)),
         row(1024), const((1, 1024)), const((N_CHIP, 512, 256)), const((N_CHIP, 512, 256)),
         const((N_CHIP, 256, 1024))],
        [row(1024), row(1024), row(1024), row(1024), row(1024), row(512)],
        [jax.ShapeDtypeStruct((T, 1024), BF), jax.ShapeDtypeStruct((T, 1024), BF),
         jax.ShapeDtypeStruct((T, 1024), BF), jax.ShapeDtypeStruct((T, 1024), F32),
         jax.ShapeDtypeStruct((T, 1024), BF), jax.ShapeDtypeStruct((T, 512), BF)],
        [], VMEM_BIG, (u_r, o_fox, z_a, z_a, x, g_ffn, w_ro, w_fo, w_out), push)


def _load_resident(hbm_refs, vmem_refs, sem):
    cps = [pltpu.make_async_copy(h, v, sem.at[i]) for i, (h, v) in enumerate(zip(hbm_refs, vmem_refs))]
    for cp in cps:
        cp.start()
    for cp in cps:
        cp.wait()


def _ffn_fwd(h2, x2, tgt, w_gate, w_up, w_down, tm=FFN_TM):
    T = h2.shape[0]

    def body(h_ref, x2_ref, t_ref, wg_hbm, wu_hbm, wd_hbm, a_ref, b_ref, act_ref, dy_ref, ls_ref, wg, wu, wd, sem):
        @pl.when(pl.program_id(0) == 0)
        def _():
            _load_resident((wg_hbm, wu_hbm, wd_hbm), (wg, wu, wd), sem)
            ls_ref[...] = jnp.zeros(ls_ref.shape, F32)

        h = h_ref[...]
        err = x2_ref[...] - t_ref[...]
        for k in range(N_CHIP):
            gp = _nt(h, wg[k])
            up = _nt(h, wu[k])
            sg = _sigmoid(gp)
            silu = gp * sg
            a_ref[k] = silu.astype(BF)
            b_ref[k] = (up * sg * (1.0 + gp * (1.0 - sg))).astype(BF)
            act = (silu * up).astype(BF)
            act_ref[k] = act
            err = err + _nn(act, wd[k])
        dy_ref[...] = err * (1.0 / D_MODEL)
        ls_ref[...] += jnp.sum(err * err, axis=0, keepdims=True)

    row = pl.BlockSpec((tm, D_MODEL), lambda i: (i, 0))
    hid = pl.BlockSpec((N_CHIP, tm, FF_SH), lambda i: (0, i, 0))
    anyspec = pl.BlockSpec(memory_space=pl.ANY)
    wshape = pltpu.VMEM((N_CHIP, FF_SH, D_MODEL), BF)
    return pl.pallas_call(
        body, name="ffn_fwd", grid=(T // tm,),
        in_specs=[row, row, row, anyspec, anyspec, anyspec],
        out_specs=[hid, hid, hid, row, pl.BlockSpec((1, D_MODEL), lambda i: (0, 0))],
        out_shape=[jax.ShapeDtypeStruct((N_CHIP, T, FF_SH), BF)] * 3
        + [jax.ShapeDtypeStruct((T, D_MODEL), F32), jax.ShapeDtypeStruct((1, D_MODEL), F32)],
        scratch_shapes=[wshape, wshape, wshape, pltpu.SemaphoreType.DMA((3,))],
        compiler_params=_params(("arbitrary",), VMEM_HUGE),
    )(h2, x2, tgt, w_gate, w_up, w_down)


def _ffn_bwd(dy, sa, sb, x2, g_ffn, w_gate, w_up, w_down, tm=FFN_TM):
    T = dy.shape[0]

    def body(dy_ref, a_ref, b_ref, x2_ref, g_ref, wg_hbm, wu_hbm, wd_hbm, dgp_ref, dup_ref, dx_ref, dg_ref,
             wg, wu, wd, sem):
        @pl.when(pl.program_id(0) == 0)
        def _():
            _load_resident((wg_hbm, wu_hbm, wd_hbm), (wg, wu, wd), sem)
            dg_ref[...] = jnp.zeros(dg_ref.shape, F32)

        dy = dy_ref[...]
        dyb = dy.astype(BF)
        dh = jnp.zeros((tm, D_MODEL), F32)
        for k in range(N_CHIP):
            dact = _nt(dyb, wd[k])
            dup = (dact * a_ref[k]).astype(BF)
            dgp = (dact * b_ref[k]).astype(BF)
            dgp_ref[k] = dgp
            dup_ref[k] = dup
            dh = dh + _nn(dgp, wg[k]) + _nn(dup, wu[k])
        x2 = x2_ref[...]
        r = lax.rsqrt(jnp.mean(x2 * x2, axis=-1, keepdims=True) + EPS)
        xn = x2 * r
        dg_ref[...] += jnp.sum(dh * xn, axis=0, keepdims=True)
        dxn = dh * g_ref[...]
        dx_ref[...] = dy + r * (dxn - xn * jnp.mean(dxn * xn, axis=-1, keepdims=True))

    row = pl.BlockSpec((tm, D_MODEL), lambda i: (i, 0))
    hid = pl.BlockSpec((N_CHIP, tm, FF_SH), lambda i: (0, i, 0))
    vec = pl.BlockSpec((1, D_MODEL), lambda i: (0, 0))
    anyspec = pl.BlockSpec(memory_space=pl.ANY)
    wshape = pltpu.VMEM((N_CHIP, FF_SH, D_MODEL), BF)
    return pl.pallas_call(
        body, name="ffn_bwd", grid=(T // tm,),
        in_specs=[row, hid, hid, row, vec, anyspec, anyspec, anyspec],
        out_specs=[hid, hid, row, vec],
        out_shape=[jax.ShapeDtypeStruct((N_CHIP, T, FF_SH), BF), jax.ShapeDtypeStruct((N_CHIP, T, FF_SH), BF),
                   jax.ShapeDtypeStruct((T, D_MODEL), F32), jax.ShapeDtypeStruct((1, D_MODEL), F32)],
        scratch_shapes=[wshape, wshape, wshape, pltpu.SemaphoreType.DMA((3,))],
        compiler_params=_params(("arbitrary",), VMEM_HUGE),
    )(dy, sa, sb, x2, g_ffn, w_gate, w_up, w_down)


def _out_bwd(dx2, z_a, y_r, y_f, o_raw, o_fox, g_ret, w_ro, w_fo, w_out, tm=512, push=None):
    T = dx2.shape[0]

    def body(dx_ref, gt_ref, ar_ref, af_ref, yr_ref, yf_ref, o_ref, of_ref, g_ref, wro_ref, wfo_ref, wout_ref,
             dyr_ref, dyf_ref, dgt_ref, da_ref, do_ref, dof_ref, dg_ref):
        i = pl.program_id(0)

        @pl.when(i == 0)
        def _():
            dg_ref[...] = jnp.zeros(dg_ref.shape, F32)

        dxb = dx_ref[...].astype(BF)
        dm = jnp.concatenate([_nt(dxb, wout_ref[k]) for k in range(N_CHIP)], axis=-1)
        sr, sf = _sigmoid(ar_ref[...].astype(F32)), _sigmoid(af_ref[...].astype(F32))
        dyr = dm * sr
        dyf = dm * sf
        da_ref[:, :1024] = (dyr * yr_ref[...].astype(F32) * (1.0 - sr)).astype(BF)
        da_ref[:, 1024:] = (dyf * yf_ref[...].astype(F32) * (1.0 - sf)).astype(BF)
        dyr = dyr.astype(BF)
        dyf = dyf.astype(BF)
        dyr_ref[...] = dyr
        dyf_ref[...] = dyf
        du = jnp.zeros((tm, 512), F32)
        doc = jnp.zeros((tm, 512), F32)
        for k in range(N_CHIP):
            du = du + _nt(dyr[:, 256 * k:256 * k + 256], wro_ref[k])
            doc = doc + _nt(dyf[:, 256 * k:256 * k + 256], wfo_ref[k])

        for h in range(RET_H):
            cols = slice(h * RET_DV, (h + 1) * RET_DV)
            o = o_ref[:, cols]
            mu = jnp.mean(o, axis=-1, keepdims=True)
            xc = o - mu
            rstd = lax.rsqrt(jnp.mean(xc * xc, axis=-1, keepdims=True) + EPS)
            on = xc * rstd
            g = g_ref[:, cols]
            gt = gt_ref[:, cols].astype(F32)
            sg = _sigmoid(gt)
            duh = du[:, cols]
            dgt_ref[:, cols] = (duh * (on * g) * sg * (1.0 + gt * (1.0 - sg))).astype(BF)
            dog = duh * gt * sg
            dg_ref[:, cols] += jnp.sum(dog * on, axis=0, keepdims=True)
            don = dog * g
            do_ref[:, cols] = rstd * (don - jnp.mean(don, axis=-1, keepdims=True)
                                      - on * jnp.mean(don * on, axis=-1, keepdims=True))

        lane = lax.broadcasted_iota(jnp.int32, (tm, LANE), 1)
        zpad = jnp.zeros((tm, 64), F32)
        for h in range(FOX_H):
            doh = doc[:, 64 * h:64 * h + 64]
            delta = jnp.sum(doh * of_ref[h][:, :FOX_D], axis=-1, keepdims=True)
            hi, mid, lo = [t.astype(F32) for t in _split3(-delta)]
            da = jnp.concatenate([doh, zpad], axis=-1)
            da = jnp.where(lane == 64, hi, jnp.where(lane == 65, mid, jnp.where(lane == 66, lo, da)))
            dof_ref[h] = da.astype(BF)

    row = lambda w: pl.BlockSpec((tm, w), lambda i: (i, 0))
    const = lambda shp: pl.BlockSpec(shp, lambda i: (0,) * len(shp))
    hsp = pl.BlockSpec((FOX_H, tm, LANE), lambda i: (0, i, 0))
    return _hosted_call(
        body, "out_bwd", (T // tm,),
        [row(1024), pl.BlockSpec((tm, 512), lambda i: (i, 2)), pl.BlockSpec((tm, 1024), lambda i: (i, 3)),
         pl.BlockSpec((tm, 1024), lambda i: (i, 4)), row(1024), row(1024), row(512), hsp,
         const((1, 512)), const((N_CHIP, 512, 256)), const((N_CHIP, 512, 256)), const((N_CHIP, 256, 1024))],
        [row(1024), row(1024), row(512), row(2048), row(512), hsp, const((1, 512))],
        [jax.ShapeDtypeStruct((T, 1024), BF), jax.ShapeDtypeStruct((T, 1024), BF),
         jax.ShapeDtypeStruct((T, 512), BF), jax.ShapeDtypeStruct((T, 2048), BF),
         jax.ShapeDtypeStruct((T, 512), F32), jax.ShapeDtypeStruct((FOX_H, T, LANE), BF),
         jax.ShapeDtypeStruct((1, 512), F32)],
        [], VMEM_BIG, (dx2, z_a, z_a, z_a, y_r, y_f, o_raw, o_fox, g_ret, w_ro, w_fo, w_out), push)


def _ret_bwd(d_o, qr, kr, z_a, states, cos_t, sin_t, consts, tt=512, push=None):
    T = z_a.shape[0]
    nt = T // tt
    nch = tt // CHUNK
    decay, zeta, xi, gcb = consts

    def body(do_ref, q_ref, k_ref, v_ref, st_ref, cos_ref, sin_ref, d_ref, ze_ref, xi_ref, gc_ref, dz_ref, g_sc):
        i = pl.program_id(0)

        @pl.when(i == 0)
        def _():
            g_sc[...] = jnp.zeros(g_sc.shape, F32)

        for c in reversed(range(nch)):
            rows = slice(c * CHUNK, (c + 1) * CHUNK)
            cosv, sinv = cos_ref[rows, :], sin_ref[rows, :]
            dq_parts, dk_parts = [], []
            for h in range(RET_H):
                cols = slice(h * RET_DV, (h + 1) * RET_DV)
                q, k = q_ref[h, rows, :], k_ref[h, rows, :]
                v32 = v_ref[rows, cols].astype(F32)
                vb = v32.astype(BF)
                r = st_ref[h, c * CHUNK:c * CHUNK + LANE, :]
                g = g_sc[h]
                gb = g.astype(BF)
                d_o = do_ref[rows, cols]
                dob = d_o.astype(BF)
                dox = (d_o * xi_ref[h]).astype(BF)
                dec = d_ref[h]
                s = (_nt(q, k) * dec).astype(BF)
                ds = (_nt(dob, vb) * dec).astype(BF)
                dv = _tn(s, dob) + ze_ref[h] * _nn(k, gb)
                dq = _nn(ds, k) + _nt(dox, r.astype(BF))
                dk = _tn(ds, q) + _nt((v32 * ze_ref[h]).astype(BF), gb)
                g_sc[h] = gc_ref[h] * g + _tn(q, dox)
                dq_parts.append((dq * cosv - _swap32(dq) * sinv)[:, :64])
                dk_parts.append(((dk * cosv - _swap32(dk) * sinv) * 0.125)[:, :64])
                dz_ref[rows, 512 + h * RET_DV:512 + (h + 1) * RET_DV] = dv.astype(BF)
            dz_ref[rows, 0:256] = jnp.concatenate(dq_parts, axis=-1).astype(BF)
            dz_ref[rows, 256:512] = jnp.concatenate(dk_parts, axis=-1).astype(BF)

    rev = lambda i: nt - 1 - i
    hspec = pl.BlockSpec((RET_H, tt, LANE), lambda i: (0, rev(i), 0))
    cspec = pl.BlockSpec((RET_H, CHUNK, LANE), lambda i: (0, 0, 0))
    tab = pl.BlockSpec((tt, LANE), lambda i: (rev(i), 0))
    (dz,), lands = _hosted_call(
        body, "ret_bwd", (nt,),
        [pl.BlockSpec((tt, 512), lambda i: (rev(i), 0)), hspec, hspec,
         pl.BlockSpec((tt, 512), lambda i: (rev(i), 1)), hspec, tab, tab,
         pl.BlockSpec((RET_H, CHUNK, CHUNK), lambda i: (0, 0, 0)), cspec, cspec,
         pl.BlockSpec((RET_H, LANE, LANE), lambda i: (0, 0, 0))],
        [pl.BlockSpec((tt, 1024), lambda i: (rev(i), 0))], [jax.ShapeDtypeStruct((T, 1024), BF)],
        [pltpu.VMEM((RET_H, LANE, LANE), F32)], VMEM_BIG,
        (d_o, qr, kr, z_a, states, cos_t, sin_t, decay, zeta, xi, gcb), push)
    return dz, lands


def _fox_bwd(end_both, end_last, q2, k, v, do, sub=FOX_SUB):
    H, T, _ = k.shape
    tb = 2 * sub

    def body(eb_ref, el_ref, q_ref, do_ref, k_ref, v_ref, dq_ref, dk_ref, dv_ref, dk_sc, dv_sc):
        j = pl.program_id(1)
        n_both = eb_ref[pl.program_id(0), j]
        n_last = el_ref[pl.program_id(0), j]

        @pl.when(j == 0)
        def _():
            dq_ref[...] = jnp.zeros(dq_ref.shape, F32)

        dk_sc[...] = jnp.zeros(dk_sc.shape, F32)
        dv_sc[...] = jnp.zeros(dv_sc.shape, F32)
        krow = lax.broadcasted_iota(jnp.int32, (tb, sub), 0)
        qcol = lax.broadcasted_iota(jnp.int32, (tb, sub), 1)

        def step(i, r0, r1, shift):
            off = pl.multiple_of(i * sub, sub)
            qq = q_ref[pl.ds(off, sub), :]
            dd = do_ref[pl.ds(off, sub), :]
            kk, vv = k_ref[r0:r1, :], v_ref[r0:r1, :]
            p = jnp.exp(_nt(kk, qq))
            if shift is not None:
                p = jnp.where(qcol[0:r1 - r0, :] + shift >= krow[0:r1 - r0, :], p, 0.0)
            ds = (p * _nt(vv, dd)).astype(BF)
            dv_sc[r0:r1, :] += _nn(p.astype(BF), dd)
            dk_sc[r0:r1, :] += _nn(ds, qq)
            dq_ref[pl.ds(off, sub), :] += _tn(ds, kk)

        step(2 * j, 0, sub, 0)
        step(2 * j + 1, 0, tb, sub)

        def both_body(i, carry):
            step(i, 0, tb, None)
            return carry

        def last_body(i, carry):
            step(i, sub, tb, None)
            return carry

        lax.fori_loop(2 * j + 2, n_both, both_body, 0)
        lax.fori_loop(n_both, n_last, last_body, 0)
        dk_ref[...] = dk_sc[...]
        dv_ref[...] = dv_sc[...]

    blk = pl.BlockSpec((None, tb, LANE), lambda h, j, eb, el: (h, j, 0))
    full = pl.BlockSpec((None, T, LANE), lambda h, j, eb, el: (h, 0, 0))
    shp = jax.ShapeDtypeStruct((H, T, LANE), F32)
    return pl.pallas_call(
        body, name="fox_bwd",
        grid_spec=pltpu.PrefetchScalarGridSpec(
            num_scalar_prefetch=2, grid=(H, T // tb), in_specs=[full, full, blk, blk], out_specs=[full, blk, blk],
            scratch_shapes=[pltpu.VMEM((tb, LANE), F32), pltpu.VMEM((tb, LANE), F32)]),
        out_shape=[shp, shp, shp],
        compiler_params=_params(("arbitrary", "arbitrary"), VMEM_BIG),
    )(end_both, end_last, q2, do, k, v)


def _fox_post_bwd(dq, dk, dv, z_a, z_ff, b_f, g_q, g_k, tm=256, push=None):
    T = z_a.shape[0]
    nt = T // tm

    def body(dq_ref, dk_ref, dv_ref, zf_ref, zff_ref, b_ref, g_ref, sc_ref, seg_ref, segt_ref,
             dz_ref, dff_ref, dg_ref, db_ref, carry):
        i = pl.program_id(0)

        @pl.when(i == 0)
        def _():
            carry[...] = jnp.zeros(carry.shape, F32)
            dg_ref[...] = jnp.zeros(dg_ref.shape, F32)
            db_ref[...] = jnp.zeros(db_ref.shape, F32)

        lane = lax.broadcasted_iota(jnp.int32, (tm, LANE), 1)
        dcm = jnp.zeros((tm, LANE), F32)
        for h in range(FOX_H):
            dcm = jnp.where(lane == h, dq_ref[h][:, L_CQ:L_CQ + 1] - dk_ref[h][:, L_CK:L_CK + 1], dcm)

        def seg_mean(v):
            return sum(_nn(t, seg_ref[...]) for t in _split3(v)) * (1.0 / FOX_D)

        def seg_bcast(v):
            return sum(_nn(t, segt_ref[...]) for t in _split3(v))

        x = zf_ref[:, :1024].astype(F32)
        dy = jnp.concatenate([dq_ref[h][:, :FOX_D] for h in range(FOX_H)]
                             + [dk_ref[h][:, :FOX_D] for h in range(FOX_H)], axis=-1) * sc_ref[...]
        rb = seg_bcast(lax.rsqrt(seg_mean(x * x) + EPS))
        xn = x * rb
        dg_ref[...] += jnp.sum(dy * xn, axis=0, keepdims=True)
        dxn = dy * g_ref[...]
        dz_ref[:, :1024] = (rb * (dxn - xn * seg_bcast(seg_mean(dxn * xn)))).astype(BF)
        dz_ref[:, 1024:] = jnp.concatenate([dv_ref[h][:, :FOX_D] for h in range(FOX_H)], axis=-1).astype(BF)

        row = lax.broadcasted_iota(jnp.int32, (tm, tm), 0)
        col = lax.broadcasted_iota(jnp.int32, (tm, tm), 1)
        tri = (row <= col).astype(BF)
        hi, mid, lo = _split3(dcm)
        dlogf = _nn(tri, hi) + _nn(tri, mid) + _nn(tri, lo) + carry[...]
        carry[...] = dlogf[0:1, :]
        dff = jnp.where(lane < FOX_H, dlogf * _sigmoid(-(zff_ref[...] + b_ref[...])), 0.0)
        dff_ref[...] = dff.astype(BF)
        db_ref[...] += jnp.sum(dff, axis=0, keepdims=True)

    rev = lambda i: nt - 1 - i
    hsp = pl.BlockSpec((FOX_H, tm, LANE), lambda i: (0, rev(i), 0))
    const = lambda r, w: pl.BlockSpec((r, w), lambda i: (0, 0))
    seg = _segment_matrix()
    g_all = jnp.concatenate([jnp.tile(g_q, (1, FOX_H)), jnp.tile(g_k, (1, FOX_H))], axis=1)
    scale = jnp.asarray(np.concatenate([np.full((1, 512), 0.125, np.float32), np.ones((1, 512), np.float32)], axis=1))
    (dz, dff, dg, db), lands = _hosted_call(
        body, "fox_post_bwd", (nt,),
        [hsp, hsp, hsp, pl.BlockSpec((tm, 1536), lambda i: (rev(i), 1)),
         pl.BlockSpec((tm, LANE), lambda i: (rev(i), 0)), const(1, LANE), const(1, 1024), const(1, 1024),
         const(1024, LANE), const(LANE, 1024)],
        [pl.BlockSpec((tm, 1536), lambda i: (rev(i), 0)), pl.BlockSpec((tm, LANE), lambda i: (rev(i), 0)),
         const(1, 1024), const(1, LANE)],
        [jax.ShapeDtypeStruct((T, 1536), BF), jax.ShapeDtypeStruct((T, LANE), BF),
         jax.ShapeDtypeStruct((1, 1024), F32), jax.ShapeDtypeStruct((1, LANE), F32)],
        [pltpu.VMEM((1, LANE), F32)], VMEM_BIG, (dq, dk, dv, z_a, z_ff, b_f, g_all, scale, seg, seg.T), push)
    dg_heads = dg.reshape(2, FOX_H, FOX_D).sum(axis=1)
    return (dz, dff, dg_heads[0:1], dg_heads[1:2], db), lands


def _in_bwd(dz_ret, dz_gt, dz_fox, dz_a, dz_ff, w_a, w_ff, x, g_mix, dx2, tm=512, push=None):
    T = x.shape[0]

    def body(r_ref, t_ref, f_ref, a_ref, ff_ref, wa_ref, wf_ref, x_ref, g_ref, dx2_ref, dx_ref, dg_ref):
        i = pl.program_id(0)

        @pl.when(i == 0)
        def _():
            dg_ref[...] = jnp.zeros(dg_ref.shape, F32)

        dh = (_nt(r_ref[...], wa_ref[:, C_RET:C_GT]) + _nt(t_ref[...], wa_ref[:, C_GT:C_FOX])
              + _nt(f_ref[...], wa_ref[:, C_FOX:C_A]) + _nt(a_ref[...], wa_ref[:, C_A:C_END])
              + _nt(ff_ref[...], wf_ref[...]))
        xv = x_ref[...]
        r = lax.rsqrt(jnp.mean(xv * xv, axis=-1, keepdims=True) + EPS)
        xn = xv * r
        dg_ref[...] += jnp.sum(dh * xn, axis=0, keepdims=True)
        dxn = dh * g_ref[...]
        dx_ref[...] = dx2_ref[...] + r * (dxn - xn * jnp.mean(dxn * xn, axis=-1, keepdims=True))

    row = lambda w: pl.BlockSpec((tm, w), lambda i: (i, 0))
    const = lambda shp: pl.BlockSpec(shp, lambda i: (0,) * len(shp))
    return _hosted_call(
        body, "in_bwd", (T // tm,),
        [row(1024), row(512), row(1536), row(2048), row(LANE), const((D_MODEL, C_END)),
         const((D_MODEL, LANE)), row(1024), const((1, 1024)), row(1024)],
        [row(1024), const((1, 1024))],
        [jax.ShapeDtypeStruct((T, 1024), F32), jax.ShapeDtypeStruct((1, 1024), F32)],
        [], VMEM_BIG, (dz_ret, dz_gt, dz_fox, dz_a, dz_ff, w_a, w_ff, x, g_mix, dx2), push)


def _mesh_pos():
    return lax.axis_index("x"), lax.axis_index("y"), lax.axis_index("c")


def _staged_place(src, name):
    stacked = src.ndim == 3
    R, C = src.shape[-2:]
    tr = _row_tile(R, 128, 16)
    n = R // tr
    assert n >= 2

    def body(s_ref, o_ref, buf, sem):
        i = pl.program_id(0)
        slot = i % 2
        x, y, _ = _mesh_pos()
        kme = 2 * x + y

        def out_copy(s, step):
            return pltpu.make_async_copy(buf.at[s], o_ref.at[kme, pl.ds(pl.multiple_of(step * tr, tr), tr), :], sem.at[s])

        @pl.when(i >= 2)
        def _():
            out_copy(slot, i - 2).wait()

        buf[slot] = (s_ref[kme] if stacked else s_ref[...]).astype(BF)
        out_copy(slot, i).start()

        @pl.when(i == n - 1)
        def _():
            out_copy(1 - slot, i - 1).wait()
            out_copy(slot, i).wait()

    in_spec = (pl.BlockSpec((N_CHIP, tr, C), lambda i: (0, i, 0)) if stacked else pl.BlockSpec((tr, C), lambda i: (i, 0)))
    return pl.pallas_call(
        body, name=name, grid=(n,), in_specs=[in_spec], out_specs=pl.BlockSpec(memory_space=pl.ANY),
        out_shape=jax.ShapeDtypeStruct((N_CHIP, R, C), BF),
        scratch_shapes=[pltpu.VMEM((2, tr, C), BF), pltpu.SemaphoreType.DMA((2,))],
        compiler_params=_params(("arbitrary",)),
    )(src)


def _push_copies(src, land, send_sem, recv_sem, receiving):
    x, y, c = _mesh_pos()
    kme = 2 * x + y
    cps = []
    for w in range(len(land)):
        for j, (px, py) in enumerate([(1 - x, y), (x, 1 - y), (1 - x, 1 - y)]):
            kpeer = 2 * px + py
            cps.append(pltpu.make_async_remote_copy(
                src_ref=land[w].at[kme] if src is None else src[w].at[kpeer],
                dst_ref=land[w].at[kpeer if receiving else kme],
                send_sem=send_sem.at[3 * w + j], recv_sem=recv_sem.at[3 * w + j],
                device_id=(px, py, c), device_id_type=MESH))
    return cps


def _gather_two_level(stack, name):
    _, R, C = stack.shape
    hr = R // 2

    def body(_, land, send_sem, recv_sem):
        x, y, c = _mesh_pos()
        kme = 2 * x + y
        chips = [(1 - x, y), (x, 1 - y), (1 - x, 1 - y)]

        def rows(k, core):
            return land.at[k, pl.ds(pl.multiple_of(core * hr, hr), hr), :]

        def copy(idx, k, core, to):
            return pltpu.make_async_remote_copy(src_ref=rows(k, core), dst_ref=rows(k, core), send_sem=send_sem.at[idx],
                                                recv_sem=recv_sem.at[idx], device_id=to, device_id_type=MESH)

        first = [copy(j, kme, c, (px, py, c)) for j, (px, py) in enumerate(chips)]
        for cp in first:
            cp.start()
        passed = [copy(3 + j, 2 * px + py, c, (x, y, 1 - c)) for j, (px, py) in enumerate(chips)]
        for j, (px, py) in enumerate(chips):
            copy(j, 2 * px + py, c, (px, py, c)).wait_recv()
            passed[j].start()
        for j, (px, py) in enumerate(chips):
            copy(3 + j, 2 * px + py, 1 - c, (x, y, 1 - c)).wait_recv()
        for cp in first + passed:
            cp.wait_send()

    anyspec = pl.BlockSpec(memory_space=pl.ANY)
    return pl.pallas_call(
        body, name=name, in_specs=[anyspec], out_specs=anyspec,
        out_shape=jax.ShapeDtypeStruct(stack.shape, stack.dtype), input_output_aliases={0: 0},
        scratch_shapes=[pltpu.SemaphoreType.DMA((6,)), pltpu.SemaphoreType.DMA((6,))],
    )(stack)


def _gather_small(small):
    def body(sv, svo, ssend, srecv, sloc):
        x, y, c = _mesh_pos()
        me = 4 * x + 2 * y + c
        flips = [(b >> 2 & 1, b >> 1 & 1, b & 1) for b in range(1, 8)]
        others = [(1 - x if fx else x, 1 - y if fy else y, 1 - c if fc else c) for fx, fy, fc in flips]
        local = pltpu.make_async_copy(sv, svo.at[me], sloc)
        local.start()
        sends = []
        for j, (px, py, pc) in enumerate(others):
            cp = pltpu.make_async_remote_copy(
                src_ref=sv, dst_ref=svo.at[me], send_sem=ssend.at[j], recv_sem=srecv.at[j],
                device_id=(px, py, pc), device_id_type=MESH)
            cp.start()
            sends.append(cp)
        for j, (px, py, pc) in enumerate(others):
            pltpu.make_async_remote_copy(
                src_ref=sv, dst_ref=svo.at[4 * px + 2 * py + pc], send_sem=ssend.at[j], recv_sem=srecv.at[j],
                device_id=(px, py, pc), device_id_type=MESH).wait_recv()
        for cp in sends:
            cp.wait_send()
        local.wait()

    anyspec = pl.BlockSpec(memory_space=pl.ANY)
    return pl.pallas_call(
        body, name="gather_small", in_specs=[anyspec], out_specs=anyspec,
        out_shape=jax.ShapeDtypeStruct((8,) + small.shape, small.dtype),
        scratch_shapes=[pltpu.SemaphoreType.DMA((7,)), pltpu.SemaphoreType.DMA((7,)), pltpu.SemaphoreType.DMA],
    )(small)


def _sibling_exchange(arrs):
    n = len(arrs)

    def body(*refs):
        ins, outs = refs[:n], refs[n:2 * n]
        send_sems, recv_sems = refs[2 * n:]
        x, y, c = _mesh_pos()
        cps = [pltpu.make_async_remote_copy(
            src_ref=ins[w], dst_ref=outs[w], send_sem=send_sems.at[w], recv_sem=recv_sems.at[w],
            device_id=(x, y, 1 - c), device_id_type=MESH) for w in range(n)]
        for cp in cps:
            cp.start()
        for cp in cps:
            cp.wait_recv()
        for cp in cps:
            cp.wait_send()

    anyspec = pl.BlockSpec(memory_space=pl.ANY)
    return pl.pallas_call(
        body, name="sibling_exchange",
        in_specs=[anyspec] * n, out_specs=[anyspec] * n,
        out_shape=[jax.ShapeDtypeStruct(a.shape, a.dtype) for a in arrs],
        scratch_shapes=[pltpu.SemaphoreType.DMA((n,)), pltpu.SemaphoreType.DMA((n,))],
    )(*arrs)


def _sum_stack(own, recv, name):
    _, R, C = recv.shape
    tr = _row_tile(R, 256, 16)

    def body(g_ref, r_ref, o_ref):
        x, y, _ = _mesh_pos()
        kme = 2 * x + y
        acc = g_ref[kme].astype(F32)
        for d in range(1, N_CHIP):
            acc = acc + r_ref[(kme + d) % N_CHIP].astype(F32)
        o_ref[...] = acc

    spec = pl.BlockSpec((N_CHIP, tr, C), lambda i: (0, i, 0))
    return pl.pallas_call(
        body, name=name, grid=(R // tr,), in_specs=[spec, spec],
        out_specs=pl.BlockSpec((tr, C), lambda i: (i, 0)),
        out_shape=jax.ShapeDtypeStruct((R, C), F32),
        compiler_params=_params(("parallel",)),
    )(own, recv)


def _adam_math(w, g, m, v):
    m2 = ADAM_B1 * m + (1.0 - ADAM_B1) * g
    v2 = ADAM_B2 * v + (1.0 - ADAM_B2) * (g * g)
    m_hat = m2 / (1.0 - ADAM_B1 ** ADAM_STEP)
    v_hat = v2 / (1.0 - ADAM_B2 ** ADAM_STEP)
    delta = -ADAM_LR * (m_hat / (jnp.sqrt(v_hat) + ADAM_EPS) + ADAM_WD * w)
    return delta, m2, v2


def _adamw(w, m, v, s0, s1, name):
    R, C = w.shape
    tr = _row_tile(R, 256, 8)

    def body(w_ref, m_ref, v_ref, a_ref, b_ref, g_ref, d_ref, m2_ref, v2_ref):
        g = a_ref[...] + b_ref[...]
        delta, m2, v2 = _adam_math(w_ref[...], g, m_ref[...], v_ref[...])
        g_ref[...] = g
        d_ref[...] = delta
        m2_ref[...] = m2
        v2_ref[...] = v2

    spec = pl.BlockSpec((tr, C), lambda i: (i, 0))
    shp = jax.ShapeDtypeStruct((R, C), F32)
    return pl.pallas_call(
        body, name=name, grid=(R // tr,), in_specs=[spec] * 5, out_specs=[spec] * 4, out_shape=[shp] * 4,
        compiler_params=_params(("parallel",), VMEM_BIG),
    )(w, m, v, s0, s1)


def _adamw_small(ws, ms, vs, gathered):
    n = len(SMALL)

    def body(*refs):
        w_refs, m_refs, v_refs, s_ref = refs[:n], refs[n:2 * n], refs[2 * n:3 * n], refs[3 * n]
        outs = refs[3 * n + 1:]
        g_all = s_ref[0]
        for d in range(1, 8):
            g_all = g_all + s_ref[d]
        off = 0
        for i, (_, width) in enumerate(SMALL):
            g = g_all[:, off:off + width]
            delta, m2, v2 = _adam_math(w_refs[i][...], g, m_refs[i][...], v_refs[i][...])
            for kind, val in enumerate((g, delta, m2, v2)):
                outs[kind * n + i][...] = val
            off += width + (-width % LANE)

    shapes = [jax.ShapeDtypeStruct((1, width), F32) for _, width in SMALL]
    res = pl.pallas_call(body, name="adamw_small", out_shape=shapes * 4)(*ws, *ms, *vs, gathered)
    return [dict(zip([nm for nm, _ in SMALL], res[kind * n:(kind + 1) * n])) for kind in range(4)]


SMALL = (("g_mix", 1024), ("g_ffn", 1024), ("g_ret_norm", 512), ("g_fox_q", 64), ("g_fox_k", 64), ("b_forget", 8))
SMALL_W = 3072


def _pack_small(parts):
    cols = []
    for (name, n) in SMALL:
        p = parts[name].reshape(1, -1)[:, :n]
        pad = -n % LANE
        cols.append(jnp.pad(p, ((0, 0), (0, pad))) if pad else p)
    used = sum(c.shape[1] for c in cols)
    cols.append(jnp.zeros((1, SMALL_W - used), F32))
    return jnp.concatenate(cols, axis=1)


def kernel(x, g_mix, w_in, b_forget, g_ret_norm, w_ret_o, g_fox_q, g_fox_k, w_fox_o, w_out, g_ffn, w_gate, w_up, w_down, loss_target, m_g_mix, m_w_in, m_b_forget, m_g_ret_norm, m_w_ret_o, m_g_fox_q, m_g_fox_k, m_w_fox_o, m_w_out, m_g_ffn, m_w_gate, m_w_up, m_w_down, v_g_mix, v_w_in, v_b_forget, v_g_ret_norm, v_w_ret_o, v_g_fox_q, v_g_fox_k, v_w_fox_o, v_w_out, v_g_ffn, v_w_gate, v_w_up, v_w_down):
    T = x.shape[1]
    xs = x[0]
    tgt = loss_target[0]
    big_names = ("w_in", "w_ret_o", "w_fox_o", "w_out", "w_gate", "w_up", "w_down")
    tr = lambda a: jnp.swapaxes(a[0], 0, 1)
    big_w = dict(w_in=w_in[0], w_ret_o=w_ret_o[0], w_fox_o=w_fox_o[0], w_out=w_out[0], w_gate=tr(w_gate),
                 w_up=tr(w_up), w_down=w_down[0])
    big_m = dict(w_in=m_w_in[0], w_ret_o=m_w_ret_o[0], w_fox_o=m_w_fox_o[0], w_out=m_w_out[0], w_gate=tr(m_w_gate),
                 w_up=tr(m_w_up), w_down=m_w_down[0])
    big_v = dict(w_in=v_w_in[0], w_ret_o=v_w_ret_o[0], w_fox_o=v_w_fox_o[0], w_out=v_w_out[0], w_gate=tr(v_w_gate),
                 w_up=tr(v_w_up), w_down=v_w_down[0])
    small_w = dict(g_mix=g_mix, g_ffn=g_ffn, g_ret_norm=g_ret_norm, g_fox_q=g_fox_q, g_fox_k=g_fox_k, b_forget=b_forget)
    small_m = dict(g_mix=m_g_mix, g_ffn=m_g_ffn, g_ret_norm=m_g_ret_norm, g_fox_q=m_g_fox_q, g_fox_k=m_g_fox_k,
                   b_forget=m_b_forget)
    small_v = dict(g_mix=v_g_mix, g_ffn=v_g_ffn, g_ret_norm=v_g_ret_norm, g_fox_q=v_g_fox_q, g_fox_k=v_g_fox_k,
                   b_forget=v_b_forget)

    stacks = {n: _staged_place(big_w[n], "place_" + n) for n in big_names}
    s_in = _gather_two_level(stacks["w_in"], "gather_w_in")
    w_a, w_ff = _assemble_w_in(s_in)
    b_pad = jnp.pad(b_forget, ((0, 0), (0, LANE - FOX_H)))
    cos_t, sin_t = _rope_tables(T)
    consts = _ret_consts()

    h = _rms_cast(xs, g_mix)
    z_a, (s_ro, s_fo, s_out, s_gate) = _mm_nn(
        h, w_a, "proj_in", BF, tm=1024,
        push=(None, [stacks["w_ret_o"], stacks["w_fox_o"], stacks["w_out"], stacks["w_gate"]]))
    (qr, kr, qf, kf, vf, c_cum, nmax, z_ff), (s_up,) = _mix_prep(
        z_a, h, w_ff, cos_t, sin_t, b_pad, g_fox_q, g_fox_k, push=(None, [stacks["w_up"]]))
    jlo, end_both, end_last, tame = _prune_tables(c_cum, nmax, FOX_SUB)
    o_raw, u_r, states = _ret_fwd(qr, kr, z_a, g_ret_norm, consts)
    o_fox, q2 = _fox_fwd(jlo, tame, qf, kf, vf)
    (y_r, y_f, mrg, x2, h2, o_cat), (s_down,) = _merge_out(u_r, o_fox, z_a, xs, g_ffn, s_ro, s_fo, s_out,
                                                            push=(None, [stacks["w_down"]]))
    sa, sb, act, dy, loss_vec = _ffn_fwd(h2, x2, tgt, s_gate, s_up, s_down)
    loss = lax.psum(0.5 / D_MODEL * jnp.sum(loss_vec), ("x", "y", "c"))

    def scatter_job(grads):
        return (grads, [lax.empty(g.shape, g.dtype) for g in grads])

    dgp, dup, dx2, dg_ffn = _ffn_bwd(dy, sa, sb, x2, g_ffn, s_gate, s_up, s_down)
    (g_gate, _), (g_up, _), (g_down, _) = (_grad_astack(dgp, h2, "gw_gate"), _grad_astack(dup, h2, "gw_up"),
                                           _grad_astack(act, dy, "gw_down"))
    (d_yr, d_yf, dz_gt, dz_a, d_o, do_fox, dg_ret), (r_gate, r_up) = _out_bwd(
        dx2, z_a, y_r, y_f, o_raw, o_fox, g_ret_norm, s_ro, s_fo, s_out, push=scatter_job([g_gate, g_up]))
    dz_ret, (r_down,) = _ret_bwd(d_o, qr, kr, z_a, states, cos_t, sin_t, consts, push=scatter_job([g_down]))
    dq_f, dk_f, dv_f = _fox_bwd(end_both, end_last, q2, kf, vf, do_fox)
    g_mid = [_grad_colstack(u_r, d_yr, "gw_ret_o", 256), _grad_colstack(o_cat, d_yf, "gw_fox_o", 256),
             _grad_plain(mrg, dx2, "gw_out", BF).reshape(N_CHIP, 256, D_MODEL)]
    (dz_fox, dz_ff, dg_q, dg_k, db_f), (r_ro, r_fo, r_out) = _fox_post_bwd(
        dq_f, dk_f, dv_f, z_a, z_ff, b_pad, g_fox_q, g_fox_k, push=scatter_job(g_mid))
    gi_ret, gi_gt, gi_ff = _grad_multi(h, [dz_ret, dz_gt, dz_ff], "gw_in_small")
    gi_fox, gi_a = _grad_multi(h, [dz_fox, dz_a], "gw_in_large")
    g_in = _pack_g_in(gi_ret, gi_gt, gi_fox, gi_a, gi_ff)
    (grad_x, dg_mix), (r_in,) = _in_bwd(dz_ret, dz_gt, dz_fox, dz_a, dz_ff, w_a, w_ff, xs, g_mix, dx2,
                                        push=scatter_job([g_in]))
    small_g = _pack_small(dict(g_mix=dg_mix, g_ffn=dg_ffn, g_ret_norm=dg_ret, g_fox_q=dg_q, g_fox_k=dg_k, b_forget=db_f))

    small_all = _gather_small(small_g)
    sums = [_sum_stack(g, r, "sum_" + n) for g, r, n in zip(
        [g_in] + g_mid + [g_gate, g_up, g_down], [r_in, r_ro, r_fo, r_out, r_gate, r_up, r_down], big_names)]
    sib = _sibling_exchange(sums)
    big_out = {n: _adamw(big_w[n], big_m[n], big_v[n], sums[i], sib[i], "adamw_" + n) for i, n in enumerate(big_names)}
    small_out = _adamw_small(*[[d[nm] for nm, _ in SMALL] for d in (small_w, small_m, small_v)], small_all)

    order = ("g_mix", "w_in", "b_forget", "g_ret_norm", "w_ret_o", "g_fox_q", "g_fox_k", "w_fox_o", "w_out", "g_ffn",
             "w_gate", "w_up", "w_down")
    outs = [loss, grad_x[None]]
    for idx in range(4):
        for n in order:
            if n in ("w_gate", "w_up"):
                outs.append(jnp.swapaxes(big_out[n][idx], 0, 1)[None])
            else:
                outs.append(big_out[n][idx][None] if n in big_out else small_out[idx][n])
    return tuple(outs)
```

```python
import functools

import numpy as np
import jax
import jax.numpy as jnp
from jax import lax
from jax.experimental import pallas as pl
from jax.experimental.pallas import tpu as pltpu

F32 = jnp.float32
BF = jnp.bfloat16
MESH = pl.DeviceIdType.MESH

D_MODEL = 1024
D_FF = 2816
N_CHIP = 4
FF_SH = D_FF // N_CHIP
IN_COLS = 5128
IN_SH = IN_COLS // N_CHIP
RET_H, RET_DV = 4, 128
FOX_H, FOX_D = 8, 64
CHUNK = 256
EPS = 1e-6
NEG = -1e30
LANE = 128
C_RET, C_GT, C_FOX, C_A, C_END = 0, 1024, 1536, 3072, 5120
L_CQ, L_CK, L_LSE, L_MAX = 64, 67, 70, 73

ADAM_LR, ADAM_B1, ADAM_B2, ADAM_EPS, ADAM_WD, ADAM_STEP = 0.001, 0.9, 0.999, 1e-08, 0.01, 10
VMEM_BIG = 56 * 1024 * 1024
VMEM_HUGE = 60 * 1024 * 1024
GRAD_TK = 2048
FFN_TM = 512
FOX_SUB = 512


def _nn(a, b):
    return lax.dot_general(a, b, (((1,), (0,)), ((), ())), preferred_element_type=F32)


def _nt(a, b):
    return lax.dot_general(a, b, (((1,), (1,)), ((), ())), preferred_element_type=F32)


def _tn(a, b):
    return lax.dot_general(a, b, (((0,), (0,)), ((), ())), preferred_element_type=F32)


def _split3(x):
    hi = x.astype(BF)
    r = x - hi.astype(F32)
    mid = r.astype(BF)
    lo = (r - mid.astype(F32)).astype(BF)
    return hi, mid, lo


def _sigmoid(x):
    return 0.5 * jnp.tanh(0.5 * x) + 0.5


def _swap32(x):
    lane = lax.broadcasted_iota(jnp.int32, x.shape, 1)
    return jnp.where(lane < 32, pltpu.roll(x, 96, 1), pltpu.roll(x, 32, 1))


def _params(sem, vmem=None):
    return pltpu.CompilerParams(dimension_semantics=sem, vmem_limit_bytes=vmem)


def _row_tile(rows, cap, mult):
    return max(d for d in range(mult, cap + 1, mult) if rows % d == 0)


def _assemble_w_in(stack, tr=256):
    def body(s_ref, a_ref, f_ref):
        full = jnp.concatenate([s_ref[k].astype(F32) for k in range(N_CHIP)], axis=-1)
        a_ref[...] = jnp.concatenate([full[:, :3072], full[:, 3080:IN_COLS]], axis=-1).astype(BF)
        f_ref[...] = jnp.concatenate([full[:, 3072:3080], jnp.zeros((tr, LANE - FOX_H), F32)], axis=-1).astype(BF)

    return pl.pallas_call(
        body, name="assemble_w_in", grid=(D_MODEL // tr,),
        in_specs=[pl.BlockSpec((N_CHIP, tr, IN_SH), lambda i: (0, i, 0))],
        out_specs=[pl.BlockSpec((tr, C_END), lambda i: (i, 0)), pl.BlockSpec((tr, LANE), lambda i: (i, 0))],
        out_shape=[jax.ShapeDtypeStruct((D_MODEL, C_END), BF), jax.ShapeDtypeStruct((D_MODEL, LANE), BF)],
        compiler_params=_params(("parallel",), VMEM_BIG),
    )(stack)


def _pack_g_in(g_ret, g_gt, g_fox, g_a, g_ff, tr=256):
    def body(r_ref, t_ref, x_ref, a_ref, f_ref, o_ref):
        r, t, x, a, f = [ref[...].astype(F32) for ref in (r_ref, t_ref, x_ref, a_ref, f_ref)]
        full = jnp.concatenate([r, t, x, f[:, :FOX_H], a], axis=-1)
        for k in range(N_CHIP):
            o_ref[k] = full[:, k * IN_SH:(k + 1) * IN_SH].astype(BF)

    def spec(w):
        return pl.BlockSpec((tr, w), lambda i: (i, 0))

    return pl.pallas_call(
        body, name="pack_g_in", grid=(D_MODEL // tr,),
        in_specs=[spec(1024), spec(512), spec(1536), spec(2048), spec(LANE)],
        out_specs=pl.BlockSpec((N_CHIP, tr, IN_SH), lambda i: (0, i, 0)),
        out_shape=jax.ShapeDtypeStruct((N_CHIP, D_MODEL, IN_SH), BF),
        compiler_params=_params(("parallel",), VMEM_BIG),
    )(g_ret, g_gt, g_fox, g_a, g_ff)


def _rms_cast(x, g, tm=512):
    T = x.shape[0]

    def body(x_ref, g_ref, o_ref):
        xv = x_ref[...]
        r = lax.rsqrt(jnp.mean(xv * xv, axis=-1, keepdims=True) + EPS)
        o_ref[...] = (xv * r * g_ref[...]).astype(BF)

    return pl.pallas_call(
        body, name="rms_cast", grid=(T // tm,),
        in_specs=[pl.BlockSpec((tm, D_MODEL), lambda i: (i, 0)), pl.BlockSpec((1, D_MODEL), lambda i: (0, 0))],
        out_specs=pl.BlockSpec((tm, D_MODEL), lambda i: (i, 0)),
        out_shape=jax.ShapeDtypeStruct((T, D_MODEL), BF),
        compiler_params=_params(("parallel",)),
    )(x, g)


def _hosted_call(body, name, grid, in_specs, out_specs, out_shape, scratch_shapes, vmem, args, push):
    sem = ("arbitrary",) * len(grid)
    if push is None:
        res = pl.pallas_call(body, name=name, grid=grid, in_specs=in_specs, out_specs=out_specs, out_shape=out_shape,
                             scratch_shapes=scratch_shapes, compiler_params=_params(sem, vmem))(*args)
        return list(res), []
    srcs, lands = push
    ns, nl, n_in, n_out = (0 if srcs is None else len(srcs)), len(lands), len(in_specs), len(out_specs)
    n_scr = len(scratch_shapes)

    def wrapped(*refs):
        pos = n_in + ns + nl
        ins, x_in = refs[:n_in], refs[n_in:pos]
        outs, x_out = refs[pos:pos + n_out], refs[pos + n_out:pos + n_out + nl]
        scr = refs[pos + n_out + nl:pos + n_out + nl + n_scr]
        ssem, rsem = refs[-2], refs[-1]
        src = None if srcs is None else x_in[:ns]
        ids = [pl.program_id(a) for a in range(len(grid))]
        first = functools.reduce(lambda p, q: p & q, [ids[a] == 0 for a in range(len(grid))])
        last = functools.reduce(lambda p, q: p & q, [ids[a] == grid[a] - 1 for a in range(len(grid))])

        @pl.when(first)
        def _():
            for cp in _push_copies(src, x_out, ssem, rsem, False):
                cp.start()

        body(*ins, *outs, *scr)

        @pl.when(last)
        def _():
            for cp in _push_copies(src, x_out, ssem, rsem, True):
                cp.wait_recv()
                cp.wait_send()

    anyspec = pl.BlockSpec(memory_space=pl.ANY)
    extra = ([] if srcs is None else list(srcs)) + list(lands)
    res = pl.pallas_call(
        wrapped, name=name, grid=grid,
        in_specs=list(in_specs) + [anyspec] * len(extra), out_specs=list(out_specs) + [anyspec] * nl,
        out_shape=list(out_shape) + [jax.ShapeDtypeStruct(a.shape, a.dtype) for a in lands],
        input_output_aliases={n_in + ns + i: n_out + i for i in range(nl)},
        scratch_shapes=list(scratch_shapes) + [pltpu.SemaphoreType.DMA((3 * nl,)), pltpu.SemaphoreType.DMA((3 * nl,))],
        compiler_params=_params(sem, vmem),
    )(*args, *extra)
    return list(res[:n_out]), list(res[n_out:])


def _mm_nn(a, b, name, out_dtype, tm=512, tn=1024, push=None):
    M, K = a.shape
    N = b.shape[1]
    tn = min(tn, N)

    def body(a_ref, b_ref, o_ref):
        o_ref[...] = _nn(a_ref[...], b_ref[...]).astype(o_ref.dtype)

    (out,), lands = _hosted_call(
        body, name, (N // tn, M // tm),
        [pl.BlockSpec((tm, K), lambda j, i: (i, 0)), pl.BlockSpec((K, tn), lambda j, i: (0, j))],
        [pl.BlockSpec((tm, tn), lambda j, i: (i, j))], [jax.ShapeDtypeStruct((M, N), out_dtype)], [], None, (a, b), push)
    return out, lands


def _mm_tn(a, b, name, grid, a_spec, b_spec, o_spec, out_shape, acc_shape):
    nk = grid[-1]

    def body(a_ref, b_ref, o_ref, acc):
        k = pl.program_id(len(grid) - 1)

        @pl.when(k == 0)
        def _():
            acc[...] = jnp.zeros(acc.shape, F32)

        acc[...] += _tn(a_ref[...].astype(BF), b_ref[...].astype(BF))

        @pl.when(k == nk - 1)
        def _():
            o_ref[...] = acc[...].astype(o_ref.dtype)

    return pl.pallas_call(
        body, name=name, grid=grid, in_specs=[a_spec, b_spec], out_specs=o_spec, out_shape=out_shape,
        scratch_shapes=[pltpu.VMEM(acc_shape, F32)],
        compiler_params=_params(("parallel",) * (len(grid) - 1) + ("arbitrary",), VMEM_BIG),
    )(a, b)


def _grad_plain(a, b, name, out_dtype, tk=GRAD_TK, tn=1024):
    T, M = a.shape
    N = b.shape[1]
    tn = min(tn, N)
    return _mm_tn(a, b, name, (N // tn, T // tk),
                  pl.BlockSpec((tk, M), lambda j, k: (k, 0)), pl.BlockSpec((tk, tn), lambda j, k: (k, j)),
                  pl.BlockSpec((M, tn), lambda j, k: (0, j)), jax.ShapeDtypeStruct((M, N), out_dtype), (M, tn))


def _grad_multi(a, bs, name, tk=1024):
    T, M = a.shape
    n = len(bs)
    nk = T // tk

    def body(*refs):
        a_ref, b_refs, o_refs, accs = refs[0], refs[1:1 + n], refs[1 + n:1 + 2 * n], refs[1 + 2 * n:]
        k = pl.program_id(0)

        @pl.when(k == 0)
        def _():
            for acc in accs:
                acc[...] = jnp.zeros(acc.shape, F32)

        av = a_ref[...]
        for i in range(n):
            accs[i][...] += _tn(av, b_refs[i][...])

        @pl.when(k == nk - 1)
        def _():
            for i in range(n):
                o_refs[i][...] = accs[i][...].astype(BF)

    widths = [b.shape[1] for b in bs]
    return pl.pallas_call(
        body, name=name, grid=(nk,),
        in_specs=[pl.BlockSpec((tk, M), lambda k: (k, 0))] + [pl.BlockSpec((tk, w), lambda k: (k, 0)) for w in widths],
        out_specs=[pl.BlockSpec((M, w), lambda k: (0, 0)) for w in widths],
        out_shape=[jax.ShapeDtypeStruct((M, w), BF) for w in widths],
        scratch_shapes=[pltpu.VMEM((M, w), F32) for w in widths],
        compiler_params=_params(("arbitrary",), VMEM_BIG),
    )(a, *bs)


def _grad_colstack(a, b, name, wcol, tk=GRAD_TK):
    T, M = a.shape
    N = b.shape[1]
    S = N // wcol
    nk = T // tk

    def body(a_ref, b_ref, o_ref, acc):
        k = pl.program_id(0)

        @pl.when(k == 0)
        def _():
            acc[...] = jnp.zeros(acc.shape, F32)

        acc[...] += _tn(a_ref[...], b_ref[...])

        @pl.when(k == nk - 1)
        def _():
            for s in range(S):
                o_ref[s] = acc[:, s * wcol:(s + 1) * wcol].astype(BF)

    return pl.pallas_call(
        body, name=name, grid=(nk,),
        in_specs=[pl.BlockSpec((tk, M), lambda k: (k, 0)), pl.BlockSpec((tk, N), lambda k: (k, 0))],
        out_specs=pl.BlockSpec((S, M, wcol), lambda k: (0, 0, 0)), out_shape=jax.ShapeDtypeStruct((S, M, wcol), BF),
        scratch_shapes=[pltpu.VMEM((M, N), F32)], compiler_params=_params(("arbitrary",), VMEM_BIG),
    )(a, b)


def _grad_astack(a, b, name, tk=1024, push=None):
    S, T, m = a.shape
    N = b.shape[1]
    nk = T // tk

    def body(a_ref, b_ref, o_ref, acc):
        k = pl.program_id(0)

        @pl.when(k == 0)
        def _():
            acc[...] = jnp.zeros(acc.shape, F32)

        bb = b_ref[...].astype(BF)
        for s in range(S):
            acc[s] += _tn(a_ref[s], bb)

        @pl.when(k == nk - 1)
        def _():
            o_ref[...] = acc[...].astype(BF)

    (out,), lands = _hosted_call(
        body, name, (nk,),
        [pl.BlockSpec((S, tk, m), lambda k: (0, k, 0)), pl.BlockSpec((tk, N), lambda k: (k, 0))],
        [pl.BlockSpec((S, m, N), lambda k: (0, 0, 0))], [jax.ShapeDtypeStruct((S, m, N), BF)],
        [pltpu.VMEM((S, m, N), F32)], VMEM_BIG, (a, b), push)
    return out, lands


def _rope_tables(T):
    half = 32
    pos = np.arange(T, dtype=np.float32)
    inv_freq = (np.float32(1.0) / (np.float32(10000.0) ** (np.arange(half, dtype=np.float32) / np.float32(half)))).astype(np.float32)
    ang = (pos[:, None] * inv_freq[None, :]).astype(np.float32)
    cos, sin = np.cos(ang).astype(np.float32), np.sin(ang).astype(np.float32)
    z = np.zeros((T, 64), np.float32)
    return (jnp.asarray(np.concatenate([cos, cos, z], axis=-1)), jnp.asarray(np.concatenate([-sin, sin, z], axis=-1)))


def _ret_consts():
    h = np.arange(RET_H, dtype=np.float32)
    log_g = np.log1p(-(np.float32(2.0) ** (-5.0 - h))).astype(np.float32)
    idx = np.arange(CHUNK, dtype=np.float32)
    diff = idx[:, None] - idx[None, :]
    decay = np.where(diff[None] >= 0, np.exp(np.maximum(diff, 0.0)[None] * log_g[:, None, None]), 0.0)
    zeta = np.exp((CHUNK - 1.0 - idx)[None, :] * log_g[:, None])
    xi = np.exp((idx + 1.0)[None, :] * log_g[:, None])
    gc = np.exp(CHUNK * log_g)
    bc = lambda v: np.broadcast_to(v[:, :, None], (RET_H, CHUNK, LANE)).astype(np.float32)
    gcb = np.broadcast_to(gc[:, None, None], (RET_H, LANE, LANE)).astype(np.float32)
    return (jnp.asarray(decay.astype(np.float32)), jnp.asarray(bc(zeta)), jnp.asarray(bc(xi)), jnp.asarray(gcb))


def _mix_prep(z_a, h, w_ff, cos_t, sin_t, b_f, g_q, g_k, tm=512, push=None):
    T = z_a.shape[0]

    def body(zqk_ref, zf_ref, h_ref, wff_ref, cos_ref, sin_ref, b_ref, g_ref, seg_ref, segt_ref,
             qr_ref, kr_ref, qf_ref, kf_ref, vf_ref, c_ref, nmax_ref, zff_ref, carry):
        i = pl.program_id(0)
        zff = _nn(h_ref[...], wff_ref[...])
        zff_ref[...] = zff

        @pl.when(i == 0)
        def _():
            carry[...] = jnp.zeros(carry.shape, F32)
            nmax_ref[...] = jnp.zeros(nmax_ref.shape, F32)

        lane = lax.broadcasted_iota(jnp.int32, (tm, LANE), 1)
        zpad = jnp.zeros((tm, 64), F32)
        cosv, sinv = cos_ref[...], sin_ref[...]
        zqk = zqk_ref[...].astype(F32)
        for h in range(RET_H):
            for src, dst, scale in ((0, qr_ref, 1.0), (256, kr_ref, 0.125)):
                xh = jnp.concatenate([zqk[:, src + 64 * h: src + 64 * h + 64], zpad], axis=-1)
                rot = xh * cosv + _swap32(xh) * sinv
                dst[h] = (rot * scale).astype(BF)

        lf_in = zff + b_ref[...]
        logf = jnp.minimum(lf_in, 0.0) - jnp.log(1.0 + jnp.exp(-jnp.abs(lf_in)))
        row = lax.broadcasted_iota(jnp.int32, (tm, tm), 0)
        col = lax.broadcasted_iota(jnp.int32, (tm, tm), 1)
        tri = (row >= col).astype(BF)
        hi, mid, lo = _split3(logf)
        cs = _nn(tri, hi) + _nn(tri, mid) + _nn(tri, lo) + carry[...]
        carry[...] = cs[tm - 1:tm, :]
        c_ref[...] = cs

        def seg_sum(v):
            return sum(_nn(t, seg_ref[...]) for t in _split3(v))

        zf = zf_ref[...].astype(F32)
        xqk = zf[:, :1024]
        rinv = lax.rsqrt(seg_sum(xqk * xqk) * (1.0 / FOX_D) + EPS)
        xn = xqk * sum(_nn(t, segt_ref[...]) for t in _split3(rinv)) * g_ref[...]
        nmax_ref[...] = jnp.maximum(nmax_ref[...], jnp.max(seg_sum(xn * xn), axis=0, keepdims=True))

        one = jnp.ones((tm, LANE), F32)
        for h in range(FOX_H):
            c = cs[:, h:h + 1]
            chi, cmid, clo = [t.astype(F32) for t in _split3(c)]
            qn = xn[:, 64 * h:64 * h + 64]
            kn = xn[:, 512 + 64 * h:512 + 64 * h + 64]
            vh = zf[:, 1024 + 64 * h:1024 + 64 * h + 64]
            qa = jnp.concatenate([qn, zpad], axis=-1)
            qa = jnp.where(lane == L_CQ, chi, jnp.where(lane == L_CQ + 1, cmid, jnp.where(lane == L_CQ + 2, clo, qa)))
            qa = jnp.where((lane >= L_CK) & (lane < L_CK + 3), one, qa)
            ka = jnp.concatenate([kn, zpad], axis=-1)
            ka = jnp.where(lane == L_CK, -chi, jnp.where(lane == L_CK + 1, -cmid, jnp.where(lane == L_CK + 2, -clo, ka)))
            ka = jnp.where(((lane >= L_CQ) & (lane < L_CQ + 3)) | ((lane >= L_LSE) & (lane < L_MAX + 3)), one, ka)
            va = jnp.concatenate([vh, zpad], axis=-1)
            va = jnp.where((lane >= 64) & (lane < 67), one, va)
            qf_ref[h] = qa.astype(BF)
            kf_ref[h] = ka.astype(BF)
            vf_ref[h] = va.astype(BF)

    hspec4 = pl.BlockSpec((RET_H, tm, LANE), lambda i: (0, i, 0))
    hspec8 = pl.BlockSpec((FOX_H, tm, LANE), lambda i: (0, i, 0))
    const = lambda r, w: pl.BlockSpec((r, w), lambda i: (0, 0))
    seg = _segment_matrix()
    g_all = jnp.concatenate([jnp.tile(g_q * 0.125, (1, FOX_H)), jnp.tile(g_k, (1, FOX_H))], axis=1)
    return _hosted_call(
        body, "mix_prep", (T // tm,),
        [pl.BlockSpec((tm, 512), lambda i: (i, 0)), pl.BlockSpec((tm, 1536), lambda i: (i, 1)),
         pl.BlockSpec((tm, D_MODEL), lambda i: (i, 0)), const(D_MODEL, LANE), pl.BlockSpec((tm, LANE), lambda i: (i, 0)),
         pl.BlockSpec((tm, LANE), lambda i: (i, 0)), const(1, LANE), const(1, 1024), const(1024, LANE), const(LANE, 1024)],
        [hspec4, hspec4, hspec8, hspec8, hspec8, pl.BlockSpec((tm, LANE), lambda i: (i, 0)), const(1, LANE),
         pl.BlockSpec((tm, LANE), lambda i: (i, 0))],
        [jax.ShapeDtypeStruct((RET_H, T, LANE), BF)] * 2 + [jax.ShapeDtypeStruct((FOX_H, T, LANE), BF)] * 3
        + [jax.ShapeDtypeStruct((T, LANE), F32), jax.ShapeDtypeStruct((1, LANE), F32), jax.ShapeDtypeStruct((T, LANE), F32)],
        [pltpu.VMEM((1, LANE), F32)], VMEM_BIG, (z_a, z_a, h, w_ff, cos_t, sin_t, b_f, g_all, seg, seg.T), push)


def _segment_matrix():
    m = np.zeros((2 * FOX_H * FOX_D, LANE), np.float32)
    m[np.arange(2 * FOX_H * FOX_D), np.arange(2 * FOX_H * FOX_D) // FOX_D] = 1.0
    return jnp.asarray(m, dtype=BF)


def _ret_fwd(qr, kr, z_a, g_ret, consts, tt=512):
    T = z_a.shape[0]
    nch = tt // CHUNK
    decay, zeta, xi, gcb = consts

    def body(q_ref, k_ref, v_ref, gt_ref, g_ref, d_ref, ze_ref, xi_ref, gc_ref, o_ref, u_ref, st_ref, r_sc):
        i = pl.program_id(0)

        @pl.when(i == 0)
        def _():
            r_sc[...] = jnp.zeros(r_sc.shape, F32)

        for c in range(nch):
            rows = slice(c * CHUNK, (c + 1) * CHUNK)
            for h in range(RET_H):
                cols = slice(h * RET_DV, (h + 1) * RET_DV)
                q, k = q_ref[h, rows, :], k_ref[h, rows, :]
                v32 = v_ref[rows, cols].astype(F32)
                r = r_sc[h]
                st_ref[h, c * CHUNK:c * CHUNK + LANE, :] = r
                s = _nt(q, k) * d_ref[h]
                o = _nn(s.astype(BF), v32.astype(BF)) + _nn(q, r.astype(BF)) * xi_ref[h]
                r_sc[h] = gc_ref[h] * r + _tn(k, (v32 * ze_ref[h]).astype(BF))
                o_ref[rows, cols] = o
                mu = jnp.mean(o, axis=-1, keepdims=True)
                xc = o - mu
                on = xc * lax.rsqrt(jnp.mean(xc * xc, axis=-1, keepdims=True) + EPS)
                gt = gt_ref[rows, cols].astype(F32)
                u_ref[rows, cols] = (gt * _sigmoid(gt) * (on * g_ref[:, cols])).astype(BF)

    hspec = pl.BlockSpec((RET_H, tt, LANE), lambda i: (0, i, 0))
    cspec = pl.BlockSpec((RET_H, CHUNK, LANE), lambda i: (0, 0, 0))
    dspec = pl.BlockSpec((RET_H, CHUNK, CHUNK), lambda i: (0, 0, 0))
    sspec = pl.BlockSpec((RET_H, LANE, LANE), lambda i: (0, 0, 0))
    return pl.pallas_call(
        body, name="ret_fwd", grid=(T // tt,),
        in_specs=[hspec, hspec, pl.BlockSpec((tt, 512), lambda i: (i, 1)), pl.BlockSpec((tt, 512), lambda i: (i, 2)),
                  pl.BlockSpec((1, 512), lambda i: (0, 0)), dspec, cspec, cspec, sspec],
        out_specs=[pl.BlockSpec((tt, 512), lambda i: (i, 0)), pl.BlockSpec((tt, 512), lambda i: (i, 0)), hspec],
        out_shape=[jax.ShapeDtypeStruct((T, 512), F32), jax.ShapeDtypeStruct((T, 512), BF),
                   jax.ShapeDtypeStruct((RET_H, T, LANE), F32)],
        scratch_shapes=[pltpu.VMEM((RET_H, LANE, LANE), F32)],
        compiler_params=_params(("arbitrary",), VMEM_BIG),
    )(qr, kr, z_a, z_a, g_ret, decay, zeta, xi, gcb)


PRUNE_LOG = -110.0
TAME_LOGIT_SPAN = 60.0


def _prune_tables(c, nmax, sub):
    n = c.shape[0] // sub
    u = jnp.sqrt(nmax[0, :FOX_H] * nmax[0, FOX_H:2 * FOX_H]) * 1.02 + 0.5
    first = c[0::sub, :FOX_H].T
    last = c[sub - 1::sub, :FOX_H].T
    blk = jnp.arange(n, dtype=jnp.int32)
    needed = (2.0 * u[:, None, None] + first[:, :, None] - last[:, None, :] >= PRUNE_LOG) | (blk[None, :] >= blk[:, None])[None]
    jlo = jnp.argmax(needed, axis=2).astype(jnp.int32)

    def end_of(key_block):
        reach = jlo[:, None, :] <= key_block[None, :, None]
        return (n - jnp.argmax(reach[:, :, ::-1], axis=2)).astype(jnp.int32)

    sup = 2 * jnp.arange(n // 2, dtype=jnp.int32)
    end_last = end_of(sup + 1)
    end_both = jnp.clip(end_of(sup), sup[None, :] + 2, end_last)
    tame = (2.0 * u < TAME_LOGIT_SPAN).astype(jnp.int32)
    return jlo, end_both, end_last, tame


def _fox_fwd(jlo, tame, q, k, v, sub=FOX_SUB):
    H, T, _ = q.shape
    tb = 2 * sub

    def body(js_ref, tame_ref, q_ref, k_ref, v_ref, o_ref, q2_ref, mx_sc, acc_sc):
        i = pl.program_id(1)
        hd = pl.program_id(0)
        starts = [jnp.minimum(js_ref[hd, 2 * i], 2 * i), jnp.minimum(js_ref[hd, 2 * i + 1], 2 * i)]
        lane = lax.broadcasted_iota(jnp.int32, (sub, LANE), 1)
        row = lax.broadcasted_iota(jnp.int32, (sub, sub), 0)
        col = lax.broadcasted_iota(jnp.int32, (sub, sub), 1)
        causal = row >= col
        qs = [q_ref[0:sub, :], q_ref[sub:tb, :]]
        d0 = pl.multiple_of(i * tb, tb)
        d1 = pl.multiple_of(i * tb + sub, sub)
        k0, k1 = k_ref[pl.ds(d0, sub), :], k_ref[pl.ds(d1, sub), :]
        v0, v1 = v_ref[pl.ds(d0, sub), :], v_ref[pl.ds(d1, sub), :]

        def lane_max(s):
            m = s[:, 0:LANE]
            for c in range(1, s.shape[1] // LANE):
                m = jnp.maximum(m, s[:, c * LANE:(c + 1) * LANE])
            return m

        def put3(base, first, val):
            hi, mid, lo = _split3(val)
            return jnp.where(lane == first, hi, jnp.where(lane == first + 1, mid, jnp.where(lane == first + 2, lo, base)))

        def row_max():
            mx_sc[...] = jnp.full(mx_sc.shape, NEG, F32)
            for a in range(2):
                def max_body(j, carry, a=a):
                    kb = k_ref[pl.ds(pl.multiple_of(j * sub, sub), sub), :]
                    mx_sc[a] = jnp.maximum(mx_sc[a], lane_max(_nt(qs[a], kb)))
                    return carry

                lax.fori_loop(starts[a], 2 * i, max_body, 0)
            mx = [jnp.maximum(mx_sc[0], lane_max(jnp.where(causal, _nt(qs[0], k0), NEG))),
                  jnp.maximum(jnp.maximum(mx_sc[1], lane_max(_nt(qs[1], k0))),
                              lane_max(jnp.where(causal, _nt(qs[1], k1), NEG)))]
            return [jnp.max(t, axis=1, keepdims=True) for t in mx]

        def diag_logit():
            return [jnp.sum(qs[a].astype(F32) * kd.astype(F32), axis=1, keepdims=True) for a, kd in enumerate((k0, k1))]

        def finish(ms):
            qm = [put3(qs[a], L_MAX, -ms[a]) for a in range(2)]
            acc_sc[...] = jnp.zeros(acc_sc.shape, F32)
            for a in range(2):
                def acc_body(j, carry, a=a):
                    off = pl.multiple_of(j * sub, sub)
                    acc_sc[a] += _nn(jnp.exp(_nt(qm[a], k_ref[pl.ds(off, sub), :])).astype(BF), v_ref[pl.ds(off, sub), :])
                    return carry

                lax.fori_loop(starts[a], 2 * i, acc_body, 0)

            def pv(qa, kk, vv, masked):
                p = jnp.exp(_nt(qa, kk))
                if masked:
                    p = jnp.where(causal, p, 0.0)
                return _nn(p.astype(BF), vv)

            accs = [acc_sc[0] + pv(qm[0], k0, v0, True),
                    acc_sc[1] + pv(qm[1], k0, v0, False) + pv(qm[1], k1, v1, True)]
            for a in range(2):
                rows = slice(a * sub, (a + 1) * sub)
                l = accs[a][:, 64:65]
                o_ref[rows, :] = jnp.where(lane < 64, accs[a] / l, 0.0)
                q2_ref[rows, :] = put3(qs[a], L_LSE, -(ms[a] + jnp.log(l)))

        tame = tame_ref[hd] == 1

        @pl.when(tame)
        def _():
            finish(diag_logit())

        @pl.when(jnp.logical_not(tame))
        def _():
            finish(row_max())

    blk = pl.BlockSpec((None, tb, LANE), lambda h, i, js, tm_: (h, i, 0))
    full = pl.BlockSpec((None, T, LANE), lambda h, i, js, tm_: (h, 0, 0))
    return pl.pallas_call(
        body, name="fox_fwd",
        grid_spec=pltpu.PrefetchScalarGridSpec(
            num_scalar_prefetch=2, grid=(H, T // tb), in_specs=[blk, full, full], out_specs=[blk, blk],
            scratch_shapes=[pltpu.VMEM((2, sub, LANE), F32), pltpu.VMEM((2, sub, LANE), F32)]),
        out_shape=[jax.ShapeDtypeStruct((H, T, LANE), F32), jax.ShapeDtypeStruct((H, T, LANE), BF)],
        compiler_params=_params(("parallel", "arbitrary"), VMEM_BIG),
    )(jlo, tame, q, k, v)


def _merge_out(u_r, o_fox, z_a, x, g_ffn, w_ro, w_fo, w_out, tm=512, push=None):
    T = x.shape[0]

    def body(u_ref, of_ref, ar_ref, af_ref, x_ref, g_ref, wro_ref, wfo_ref, wout_ref,
             yr_ref, yf_ref, m_ref, x2_ref, h2_ref, oc_ref):
        u = u_ref[...]
        oc = jnp.concatenate([of_ref[h][:, :FOX_D] for h in range(FOX_H)], axis=-1).astype(BF)
        oc_ref[...] = oc
        yr = jnp.concatenate([_nn(u, wro_ref[k]) for k in range(N_CHIP)], axis=-1)
        yf = jnp.concatenate([_nn(oc, wfo_ref[k]) for k in range(N_CHIP)], axis=-1)
        yr_ref[...] = yr.astype(BF)
        yf_ref[...] = yf.astype(BF)
        m = (_sigmoid(ar_ref[...].astype(F32)) * yr + _sigmoid(af_ref[...].astype(F32)) * yf).astype(BF)
        m_ref[...] = m
        x2 = x_ref[...]
        for k in range(N_CHIP):
            x2 = x2 + _nn(m[:, 256 * k:256 * k + 256], wout_ref[k])
        x2_ref[...] = x2
        r = lax.rsqrt(jnp.mean(x2 * x2, axis=-1, keepdims=True) + EPS)
        h2_ref[...] = (x2 * r * g_ref[...]).astype(BF)

    row = lambda w: pl.BlockSpec((tm, w), lambda i: (i, 0))
    const = lambda shp: pl.BlockSpec(shp, lambda i: (0,) * len(shp))
    return _hosted_call(
        body, "merge_out", (T // tm,),
        [row(512), pl.BlockSpec((FOX_H, tm, LANE), lambda i: (0, i, 0)),
         pl.BlockSpec((tm, 1024), lambda i: (i, 3)), pl.BlockSpec((tm, 1024), lambda i: (i, 4)),
         row(1024), const((1, 1024)), const((N_CHIP, 512, 256)), const((N_CHIP, 512, 256)),
         const((N_CHIP, 256, 1024))],
        [row(1024), row(1024), row(1024), row(1024), row(1024), row(512)],
        [jax.ShapeDtypeStruct((T, 1024), BF), jax.ShapeDtypeStruct((T, 1024), BF),
         jax.ShapeDtypeStruct((T, 1024), BF), jax.ShapeDtypeStruct((T, 1024), F32),
         jax.ShapeDtypeStruct((T, 1024), BF), jax.ShapeDtypeStruct((T, 512), BF)],
        [], VMEM_BIG, (u_r, o_fox, z_a, z_a, x, g_ffn, w_ro, w_fo, w_out), push)


def _load_resident(hbm_refs, vmem_refs, sem):
    cps = [pltpu.make_async_copy(h, v, sem.at[i]) for i, (h, v) in enumerate(zip(hbm_refs, vmem_refs))]
    for cp in cps:
        cp.start()
    for cp in cps:
        cp.wait()


def _ffn_fwd(h2, x2, tgt, w_gate, w_up, w_down, tm=FFN_TM):
    T = h2.shape[0]

    def body(h_ref, x2_ref, t_ref, wg_hbm, wu_hbm, wd_hbm, a_ref, b_ref, act_ref, dy_ref, ls_ref, wg, wu, wd, sem):
        @pl.when(pl.program_id(0) == 0)
        def _():
            _load_resident((wg_hbm, wu_hbm, wd_hbm), (wg, wu, wd), sem)
            ls_ref[...] = jnp.zeros(ls_ref.shape, F32)

        h = h_ref[...]
        err = x2_ref[...] - t_ref[...]
        for k in range(N_CHIP):
            gp = _nt(h, wg[k])
            up = _nt(h, wu[k])
            sg = _sigmoid(gp)
            silu = gp * sg
            a_ref[k] = silu.astype(BF)
            b_ref[k] = (up * sg * (1.0 + gp * (1.0 - sg))).astype(BF)
            act = (silu * up).astype(BF)
            act_ref[k] = act
            err = err + _nn(act, wd[k])
        dy_ref[...] = err * (1.0 / D_MODEL)
        ls_ref[...] += jnp.sum(err * err, axis=0, keepdims=True)

    row = pl.BlockSpec((tm, D_MODEL), lambda i: (i, 0))
    hid = pl.BlockSpec((N_CHIP, tm, FF_SH), lambda i: (0, i, 0))
    anyspec = pl.BlockSpec(memory_space=pl.ANY)
    wshape = pltpu.VMEM((N_CHIP, FF_SH, D_MODEL), BF)
    return pl.pallas_call(
        body, name="ffn_fwd", grid=(T // tm,),
        in_specs=[row, row, row, anyspec, anyspec, anyspec],
        out_specs=[hid, hid, hid, row, pl.BlockSpec((1, D_MODEL), lambda i: (0, 0))],
        out_shape=[jax.ShapeDtypeStruct((N_CHIP, T, FF_SH), BF)] * 3
        + [jax.ShapeDtypeStruct((T, D_MODEL), F32), jax.ShapeDtypeStruct((1, D_MODEL), F32)],
        scratch_shapes=[wshape, wshape, wshape, pltpu.SemaphoreType.DMA((3,))],
        compiler_params=_params(("arbitrary",), VMEM_HUGE),
    )(h2, x2, tgt, w_gate, w_up, w_down)


def _ffn_bwd(dy, sa, sb, x2, g_ffn, w_gate, w_up, w_down, tm=FFN_TM):
    T = dy.shape[0]

    def body(dy_ref, a_ref, b_ref, x2_ref, g_ref, wg_hbm, wu_hbm, wd_hbm, dgp_ref, dup_ref, dx_ref, dg_ref,
             wg, wu, wd, sem):
        @pl.when(pl.program_id(0) == 0)
        def _():
            _load_resident((wg_hbm, wu_hbm, wd_hbm), (wg, wu, wd), sem)
            dg_ref[...] = jnp.zeros(dg_ref.shape, F32)

        dy = dy_ref[...]
        dyb = dy.astype(BF)
        dh = jnp.zeros((tm, D_MODEL), F32)
        for k in range(N_CHIP):
            dact = _nt(dyb, wd[k])
            dup = (dact * a_ref[k]).astype(BF)
            dgp = (dact * b_ref[k]).astype(BF)
            dgp_ref[k] = dgp
            dup_ref[k] = dup
            dh = dh + _nn(dgp, wg[k]) + _nn(dup, wu[k])
        x2 = x2_ref[...]
        r = lax.rsqrt(jnp.mean(x2 * x2, axis=-1, keepdims=True) + EPS)
        xn = x2 * r
        dg_ref[...] += jnp.sum(dh * xn, axis=0, keepdims=True)
        dxn = dh * g_ref[...]
        dx_ref[...] = dy + r * (dxn - xn * jnp.mean(dxn * xn, axis=-1, keepdims=True))

    row = pl.BlockSpec((tm, D_MODEL), lambda i: (i, 0))
    hid = pl.BlockSpec((N_CHIP, tm, FF_SH), lambda i: (0, i, 0))
    vec = pl.BlockSpec((1, D_MODEL), lambda i: (0, 0))
    anyspec = pl.BlockSpec(memory_space=pl.ANY)
    wshape = pltpu.VMEM((N_CHIP, FF_SH, D_MODEL), BF)
    return pl.pallas_call(
        body, name="ffn_bwd", grid=(T // tm,),
        in_specs=[row, hid, hid, row, vec, anyspec, anyspec, anyspec],
        out_specs=[hid, hid, row, vec],
        out_shape=[jax.ShapeDtypeStruct((N_CHIP, T, FF_SH), BF), jax.ShapeDtypeStruct((N_CHIP, T, FF_SH), BF),
                   jax.ShapeDtypeStruct((T, D_MODEL), F32), jax.ShapeDtypeStruct((1, D_MODEL), F32)],
        scratch_shapes=[wshape, wshape, wshape, pltpu.SemaphoreType.DMA((3,))],
        compiler_params=_params(("arbitrary",), VMEM_HUGE),
    )(dy, sa, sb, x2, g_ffn, w_gate, w_up, w_down)


def _out_bwd(dx2, z_a, y_r, y_f, o_raw, o_fox, g_ret, w_ro, w_fo, w_out, tm=512, push=None):
    T = dx2.shape[0]

    def body(dx_ref, gt_ref, ar_ref, af_ref, yr_ref, yf_ref, o_ref, of_ref, g_ref, wro_ref, wfo_ref, wout_ref,
             dyr_ref, dyf_ref, dgt_ref, da_ref, do_ref, dof_ref, dg_ref):
        i = pl.program_id(0)

        @pl.when(i == 0)
        def _():
            dg_ref[...] = jnp.zeros(dg_ref.shape, F32)

        dxb = dx_ref[...].astype(BF)
        dm = jnp.concatenate([_nt(dxb, wout_ref[k]) for k in range(N_CHIP)], axis=-1)
        sr, sf = _sigmoid(ar_ref[...].astype(F32)), _sigmoid(af_ref[...].astype(F32))
        dyr = dm * sr
        dyf = dm * sf
        da_ref[:, :1024] = (dyr * yr_ref[...].astype(F32) * (1.0 - sr)).astype(BF)
        da_ref[:, 1024:] = (dyf * yf_ref[...].astype(F32) * (1.0 - sf)).astype(BF)
        dyr = dyr.astype(BF)
        dyf = dyf.astype(BF)
        dyr_ref[...] = dyr
        dyf_ref[...] = dyf
        du = jnp.zeros((tm, 512), F32)
        doc = jnp.zeros((tm, 512), F32)
        for k in range(N_CHIP):
            du = du + _nt(dyr[:, 256 * k:256 * k + 256], wro_ref[k])
            doc = doc + _nt(dyf[:, 256 * k:256 * k + 256], wfo_ref[k])

        for h in range(RET_H):
            cols = slice(h * RET_DV, (h + 1) * RET_DV)
            o = o_ref[:, cols]
            mu = jnp.mean(o, axis=-1, keepdims=True)
            xc = o - mu
            rstd = lax.rsqrt(jnp.mean(xc * xc, axis=-1, keepdims=True) + EPS)
            on = xc * rstd
            g = g_ref[:, cols]
            gt = gt_ref[:, cols].astype(F32)
            sg = _sigmoid(gt)
            duh = du[:, cols]
            dgt_ref[:, cols] = (duh * (on * g) * sg * (1.0 + gt * (1.0 - sg))).astype(BF)
            dog = duh * gt * sg
            dg_ref[:, cols] += jnp.sum(dog * on, axis=0, keepdims=True)
            don = dog * g
            do_ref[:, cols] = rstd * (don - jnp.mean(don, axis=-1, keepdims=True)
                                      - on * jnp.mean(don * on, axis=-1, keepdims=True))

        lane = lax.broadcasted_iota(jnp.int32, (tm, LANE), 1)
        zpad = jnp.zeros((tm, 64), F32)
        for h in range(FOX_H):
            doh = doc[:, 64 * h:64 * h + 64]
            delta = jnp.sum(doh * of_ref[h][:, :FOX_D], axis=-1, keepdims=True)
            hi, mid, lo = [t.astype(F32) for t in _split3(-delta)]
            da = jnp.concatenate([doh, zpad], axis=-1)
            da = jnp.where(lane == 64, hi, jnp.where(lane == 65, mid, jnp.where(lane == 66, lo, da)))
            dof_ref[h] = da.astype(BF)

    row = lambda w: pl.BlockSpec((tm, w), lambda i: (i, 0))
    const = lambda shp: pl.BlockSpec(shp, lambda i: (0,) * len(shp))
    hsp = pl.BlockSpec((FOX_H, tm, LANE), lambda i: (0, i, 0))
    return _hosted_call(
        body, "out_bwd", (T // tm,),
        [row(1024), pl.BlockSpec((tm, 512), lambda i: (i, 2)), pl.BlockSpec((tm, 1024), lambda i: (i, 3)),
         pl.BlockSpec((tm, 1024), lambda i: (i, 4)), row(1024), row(1024), row(512), hsp,
         const((1, 512)), const((N_CHIP, 512, 256)), const((N_CHIP, 512, 256)), const((N_CHIP, 256, 1024))],
        [row(1024), row(1024), row(512), row(2048), row(512), hsp, const((1, 512))],
        [jax.ShapeDtypeStruct((T, 1024), BF), jax.ShapeDtypeStruct((T, 1024), BF),
         jax.ShapeDtypeStruct((T, 512), BF), jax.ShapeDtypeStruct((T, 2048), BF),
         jax.ShapeDtypeStruct((T, 512), F32), jax.ShapeDtypeStruct((FOX_H, T, LANE), BF),
         jax.ShapeDtypeStruct((1, 512), F32)],
        [], VMEM_BIG, (dx2, z_a, z_a, z_a, y_r, y_f, o_raw, o_fox, g_ret, w_ro, w_fo, w_out), push)


def _ret_bwd(d_o, qr, kr, z_a, states, cos_t, sin_t, consts, tt=512, push=None):
    T = z_a.shape[0]
    nt = T // tt
    nch = tt // CHUNK
    decay, zeta, xi, gcb = consts

    def body(do_ref, q_ref, k_ref, v_ref, st_ref, cos_ref, sin_ref, d_ref, ze_ref, xi_ref, gc_ref, dz_ref, g_sc):
        i = pl.program_id(0)

        @pl.when(i == 0)
        def _():
            g_sc[...] = jnp.zeros(g_sc.shape, F32)

        for c in reversed(range(nch)):
            rows = slice(c * CHUNK, (c + 1) * CHUNK)
            cosv, sinv = cos_ref[rows, :], sin_ref[rows, :]
            dq_parts, dk_parts = [], []
            for h in range(RET_H):
                cols = slice(h * RET_DV, (h + 1) * RET_DV)
                q, k = q_ref[h, rows, :], k_ref[h, rows, :]
                v32 = v_ref[rows, cols].astype(F32)
                vb = v32.astype(BF)
                r = st_ref[h, c * CHUNK:c * CHUNK + LANE, :]
                g = g_sc[h]
                gb = g.astype(BF)
                d_o = do_ref[rows, cols]
                dob = d_o.astype(BF)
                dox = (d_o * xi_ref[h]).astype(BF)
                dec = d_ref[h]
                s = (_nt(q, k) * dec).astype(BF)
                ds = (_nt(dob, vb) * dec).astype(BF)
                dv = _tn(s, dob) + ze_ref[h] * _nn(k, gb)
                dq = _nn(ds, k) + _nt(dox, r.astype(BF))
                dk = _tn(ds, q) + _nt((v32 * ze_ref[h]).astype(BF), gb)
                g_sc[h] = gc_ref[h] * g + _tn(q, dox)
                dq_parts.append((dq * cosv - _swap32(dq) * sinv)[:, :64])
                dk_parts.append(((dk * cosv - _swap32(dk) * sinv) * 0.125)[:, :64])
                dz_ref[rows, 512 + h * RET_DV:512 + (h + 1) * RET_DV] = dv.astype(BF)
            dz_ref[rows, 0:256] = jnp.concatenate(dq_parts, axis=-1).astype(BF)
            dz_ref[rows, 256:512] = jnp.concatenate(dk_parts, axis=-1).astype(BF)

    rev = lambda i: nt - 1 - i
    hspec = pl.BlockSpec((RET_H, tt, LANE), lambda i: (0, rev(i), 0))
    cspec = pl.BlockSpec((RET_H, CHUNK, LANE), lambda i: (0, 0, 0))
    tab = pl.BlockSpec((tt, LANE), lambda i: (rev(i), 0))
    (dz,), lands = _hosted_call(
        body, "ret_bwd", (nt,),
        [pl.BlockSpec((tt, 512), lambda i: (rev(i), 0)), hspec, hspec,
         pl.BlockSpec((tt, 512), lambda i: (rev(i), 1)), hspec, tab, tab,
         pl.BlockSpec((RET_H, CHUNK, CHUNK), lambda i: (0, 0, 0)), cspec, cspec,
         pl.BlockSpec((RET_H, LANE, LANE), lambda i: (0, 0, 0))],
        [pl.BlockSpec((tt, 1024), lambda i: (rev(i), 0))], [jax.ShapeDtypeStruct((T, 1024), BF)],
        [pltpu.VMEM((RET_H, LANE, LANE), F32)], VMEM_BIG,
        (d_o, qr, kr, z_a, states, cos_t, sin_t, decay, zeta, xi, gcb), push)
    return dz, lands


def _fox_bwd(end_both, end_last, q2, k, v, do, sub=FOX_SUB):
    H, T, _ = k.shape
    tb = 2 * sub

    def body(eb_ref, el_ref, q_ref, do_ref, k_ref, v_ref, dq_ref, dk_ref, dv_ref, dk_sc, dv_sc):
        j = pl.program_id(1)
        n_both = eb_ref[pl.program_id(0), j]
        n_last = el_ref[pl.program_id(0), j]

        @pl.when(j == 0)
        def _():
            dq_ref[...] = jnp.zeros(dq_ref.shape, F32)

        dk_sc[...] = jnp.zeros(dk_sc.shape, F32)
        dv_sc[...] = jnp.zeros(dv_sc.shape, F32)
        krow = lax.broadcasted_iota(jnp.int32, (tb, sub), 0)
        qcol = lax.broadcasted_iota(jnp.int32, (tb, sub), 1)

        def step(i, r0, r1, shift):
            off = pl.multiple_of(i * sub, sub)
            qq = q_ref[pl.ds(off, sub), :]
            dd = do_ref[pl.ds(off, sub), :]
            kk, vv = k_ref[r0:r1, :], v_ref[r0:r1, :]
            p = jnp.exp(_nt(kk, qq))
            if shift is not None:
                p = jnp.where(qcol[0:r1 - r0, :] + shift >= krow[0:r1 - r0, :], p, 0.0)
            ds = (p * _nt(vv, dd)).astype(BF)
            dv_sc[r0:r1, :] += _nn(p.astype(BF), dd)
            dk_sc[r0:r1, :] += _nn(ds, qq)
            dq_ref[pl.ds(off, sub), :] += _tn(ds, kk)

        step(2 * j, 0, sub, 0)
        step(2 * j + 1, 0, tb, sub)

        def both_body(i, carry):
            step(i, 0, tb, None)
            return carry

        def last_body(i, carry):
            step(i, sub, tb, None)
            return carry

        lax.fori_loop(2 * j + 2, n_both, both_body, 0)
        lax.fori_loop(n_both, n_last, last_body, 0)
        dk_ref[...] = dk_sc[...]
        dv_ref[...] = dv_sc[...]

    blk = pl.BlockSpec((None, tb, LANE), lambda h, j, eb, el: (h, j, 0))
    full = pl.BlockSpec((None, T, LANE), lambda h, j, eb, el: (h, 0, 0))
    shp = jax.ShapeDtypeStruct((H, T, LANE), F32)
    return pl.pallas_call(
        body, name="fox_bwd",
        grid_spec=pltpu.PrefetchScalarGridSpec(
            num_scalar_prefetch=2, grid=(H, T // tb), in_specs=[full, full, blk, blk], out_specs=[full, blk, blk],
            scratch_shapes=[pltpu.VMEM((tb, LANE), F32), pltpu.VMEM((tb, LANE), F32)]),
        out_shape=[shp, shp, shp],
        compiler_params=_params(("arbitrary", "arbitrary"), VMEM_BIG),
    )(end_both, end_last, q2, do, k, v)


def _fox_post_bwd(dq, dk, dv, z_a, z_ff, b_f, g_q, g_k, tm=512, push=None):
    T = z_a.shape[0]
    nt = T // tm

    def body(dq_ref, dk_ref, dv_ref, zf_ref, zff_ref, b_ref, g_ref, sc_ref, seg_ref, segt_ref,
             dz_ref, dff_ref, dg_ref, db_ref, carry):
        i = pl.program_id(0)

        @pl.when(i == 0)
        def _():
            carry[...] = jnp.zeros(carry.shape, F32)
            dg_ref[...] = jnp.zeros(dg_ref.shape, F32)
            db_ref[...] = jnp.zeros(db_ref.shape, F32)

        lane = lax.broadcasted_iota(jnp.int32, (tm, LANE), 1)
        dcm = jnp.zeros((tm, LANE), F32)
        for h in range(FOX_H):
            dcm = jnp.where(lane == h, dq_ref[h][:, L_CQ:L_CQ + 1] - dk_ref[h][:, L_CK:L_CK + 1], dcm)

        def seg_mean(v):
            return sum(_nn(t, seg_ref[...]) for t in _split3(v)) * (1.0 / FOX_D)

        def seg_bcast(v):
            return sum(_nn(t, segt_ref[...]) for t in _split3(v))

        x = zf_ref[:, :1024].astype(F32)
        dy = jnp.concatenate([dq_ref[h][:, :FOX_D] for h in range(FOX_H)]
                             + [dk_ref[h][:, :FOX_D] for h in range(FOX_H)], axis=-1) * sc_ref[...]
        rb = seg_bcast(lax.rsqrt(seg_mean(x * x) + EPS))
        xn = x * rb
        dg_ref[...] += jnp.sum(dy * xn, axis=0, keepdims=True)
        dxn = dy * g_ref[...]
        dz_ref[:, :1024] = (rb * (dxn - xn * seg_bcast(seg_mean(dxn * xn)))).astype(BF)
        dz_ref[:, 1024:] = jnp.concatenate([dv_ref[h][:, :FOX_D] for h in range(FOX_H)], axis=-1).astype(BF)

        row = lax.broadcasted_iota(jnp.int32, (tm, tm), 0)
        col = lax.broadcasted_iota(jnp.int32, (tm, tm), 1)
        tri = (row <= col).astype(BF)
        hi, mid, lo = _split3(dcm)
        dlogf = _nn(tri, hi) + _nn(tri, mid) + _nn(tri, lo) + carry[...]
        carry[...] = dlogf[0:1, :]
        dff = jnp.where(lane < FOX_H, dlogf * _sigmoid(-(zff_ref[...] + b_ref[...])), 0.0)
        dff_ref[...] = dff.astype(BF)
        db_ref[...] += jnp.sum(dff, axis=0, keepdims=True)

    rev = lambda i: nt - 1 - i
    hsp = pl.BlockSpec((FOX_H, tm, LANE), lambda i: (0, rev(i), 0))
    const = lambda r, w: pl.BlockSpec((r, w), lambda i: (0, 0))
    seg = _segment_matrix()
    g_all = jnp.concatenate([jnp.tile(g_q, (1, FOX_H)), jnp.tile(g_k, (1, FOX_H))], axis=1)
    scale = jnp.asarray(np.concatenate([np.full((1, 512), 0.125, np.float32), np.ones((1, 512), np.float32)], axis=1))
    (dz, dff, dg, db), lands = _hosted_call(
        body, "fox_post_bwd", (nt,),
        [hsp, hsp, hsp, pl.BlockSpec((tm, 1536), lambda i: (rev(i), 1)),
         pl.BlockSpec((tm, LANE), lambda i: (rev(i), 0)), const(1, LANE), const(1, 1024), const(1, 1024),
         const(1024, LANE), const(LANE, 1024)],
        [pl.BlockSpec((tm, 1536), lambda i: (rev(i), 0)), pl.BlockSpec((tm, LANE), lambda i: (rev(i), 0)),
         const(1, 1024), const(1, LANE)],
        [jax.ShapeDtypeStruct((T, 1536), BF), jax.ShapeDtypeStruct((T, LANE), BF),
         jax.ShapeDtypeStruct((1, 1024), F32), jax.ShapeDtypeStruct((1, LANE), F32)],
        [pltpu.VMEM((1, LANE), F32)], VMEM_BIG, (dq, dk, dv, z_a, z_ff, b_f, g_all, scale, seg, seg.T), push)
    dg_heads = dg.reshape(2, FOX_H, FOX_D).sum(axis=1)
    return (dz, dff, dg_heads[0:1], dg_heads[1:2], db), lands


def _in_bwd(dz_ret, dz_gt, dz_fox, dz_a, dz_ff, w_a, w_ff, x, g_mix, dx2, tm=512, push=None):
    T = x.shape[0]

    def body(r_ref, t_ref, f_ref, a_ref, ff_ref, wa_ref, wf_ref, x_ref, g_ref, dx2_ref, dx_ref, dg_ref):
        i = pl.program_id(0)

        @pl.when(i == 0)
        def _():
            dg_ref[...] = jnp.zeros(dg_ref.shape, F32)

        dh = (_nt(r_ref[...], wa_ref[:, C_RET:C_GT]) + _nt(t_ref[...], wa_ref[:, C_GT:C_FOX])
              + _nt(f_ref[...], wa_ref[:, C_FOX:C_A]) + _nt(a_ref[...], wa_ref[:, C_A:C_END])
              + _nt(ff_ref[...], wf_ref[...]))
        xv = x_ref[...]
        r = lax.rsqrt(jnp.mean(xv * xv, axis=-1, keepdims=True) + EPS)
        xn = xv * r
        dg_ref[...] += jnp.sum(dh * xn, axis=0, keepdims=True)
        dxn = dh * g_ref[...]
        dx_ref[...] = dx2_ref[...] + r * (dxn - xn * jnp.mean(dxn * xn, axis=-1, keepdims=True))

    row = lambda w: pl.BlockSpec((tm, w), lambda i: (i, 0))
    const = lambda shp: pl.BlockSpec(shp, lambda i: (0,) * len(shp))
    return _hosted_call(
        body, "in_bwd", (T // tm,),
        [row(1024), row(512), row(1536), row(2048), row(LANE), const((D_MODEL, C_END)),
         const((D_MODEL, LANE)), row(1024), const((1, 1024)), row(1024)],
        [row(1024), const((1, 1024))],
        [jax.ShapeDtypeStruct((T, 1024), F32), jax.ShapeDtypeStruct((1, 1024), F32)],
        [], VMEM_BIG, (dz_ret, dz_gt, dz_fox, dz_a, dz_ff, w_a, w_ff, x, g_mix, dx2), push)


def _mesh_pos():
    return lax.axis_index("x"), lax.axis_index("y"), lax.axis_index("c")


def _staged_place(src, name):
    stacked = src.ndim == 3
    R, C = src.shape[-2:]
    tr = _row_tile(R, 128, 16)
    n = R // tr
    assert n >= 2

    def body(s_ref, o_ref, buf, sem):
        i = pl.program_id(0)
        slot = i % 2
        x, y, _ = _mesh_pos()
        kme = 2 * x + y

        def out_copy(s, step):
            return pltpu.make_async_copy(buf.at[s], o_ref.at[kme, pl.ds(pl.multiple_of(step * tr, tr), tr), :], sem.at[s])

        @pl.when(i >= 2)
        def _():
            out_copy(slot, i - 2).wait()

        buf[slot] = (s_ref[kme] if stacked else s_ref[...]).astype(BF)
        out_copy(slot, i).start()

        @pl.when(i == n - 1)
        def _():
            out_copy(1 - slot, i - 1).wait()
            out_copy(slot, i).wait()

    in_spec = (pl.BlockSpec((N_CHIP, tr, C), lambda i: (0, i, 0)) if stacked else pl.BlockSpec((tr, C), lambda i: (i, 0)))
    return pl.pallas_call(
        body, name=name, grid=(n,), in_specs=[in_spec], out_specs=pl.BlockSpec(memory_space=pl.ANY),
        out_shape=jax.ShapeDtypeStruct((N_CHIP, R, C), BF),
        scratch_shapes=[pltpu.VMEM((2, tr, C), BF), pltpu.SemaphoreType.DMA((2,))],
        compiler_params=_params(("arbitrary",)),
    )(src)


def _push_copies(src, land, send_sem, recv_sem, receiving):
    x, y, c = _mesh_pos()
    kme = 2 * x + y
    cps = []
    for w in range(len(land)):
        for j, (px, py) in enumerate([(1 - x, y), (x, 1 - y), (1 - x, 1 - y)]):
            kpeer = 2 * px + py
            cps.append(pltpu.make_async_remote_copy(
                src_ref=land[w].at[kme] if src is None else src[w].at[kpeer],
                dst_ref=land[w].at[kpeer if receiving else kme],
                send_sem=send_sem.at[3 * w + j], recv_sem=recv_sem.at[3 * w + j],
                device_id=(px, py, c), device_id_type=MESH))
    return cps


def _gather_two_level(stack, name):
    _, R, C = stack.shape
    hr = R // 2

    def body(_, land, send_sem, recv_sem):
        x, y, c = _mesh_pos()
        kme = 2 * x + y
        chips = [(1 - x, y), (x, 1 - y), (1 - x, 1 - y)]

        def rows(k, core):
            return land.at[k, pl.ds(pl.multiple_of(core * hr, hr), hr), :]

        def copy(idx, k, core, to):
            return pltpu.make_async_remote_copy(src_ref=rows(k, core), dst_ref=rows(k, core), send_sem=send_sem.at[idx],
                                                recv_sem=recv_sem.at[idx], device_id=to, device_id_type=MESH)

        first = [copy(j, kme, c, (px, py, c)) for j, (px, py) in enumerate(chips)]
        for cp in first:
            cp.start()
        passed = [copy(3 + j, 2 * px + py, c, (x, y, 1 - c)) for j, (px, py) in enumerate(chips)]
        for j, (px, py) in enumerate(chips):
            copy(j, 2 * px + py, c, (px, py, c)).wait_recv()
            passed[j].start()
        for j, (px, py) in enumerate(chips):
            copy(3 + j, 2 * px + py, 1 - c, (x, y, 1 - c)).wait_recv()
        for cp in first + passed:
            cp.wait_send()

    anyspec = pl.BlockSpec(memory_space=pl.ANY)
    return pl.pallas_call(
        body, name=name, in_specs=[anyspec], out_specs=anyspec,
        out_shape=jax.ShapeDtypeStruct(stack.shape, stack.dtype), input_output_aliases={0: 0},
        scratch_shapes=[pltpu.SemaphoreType.DMA((6,)), pltpu.SemaphoreType.DMA((6,))],
    )(stack)


def _gather_small(small):
    def body(sv, svo, ssend, srecv, sloc):
        x, y, c = _mesh_pos()
        me = 4 * x + 2 * y + c
        flips = [(b >> 2 & 1, b >> 1 & 1, b & 1) for b in range(1, 8)]
        others = [(1 - x if fx else x, 1 - y if fy else y, 1 - c if fc else c) for fx, fy, fc in flips]
        local = pltpu.make_async_copy(sv, svo.at[me], sloc)
        local.start()
        sends = []
        for j, (px, py, pc) in enumerate(others):
            cp = pltpu.make_async_remote_copy(
                src_ref=sv, dst_ref=svo.at[me], send_sem=ssend.at[j], recv_sem=srecv.at[j],
                device_id=(px, py, pc), device_id_type=MESH)
            cp.start()
            sends.append(cp)
        for j, (px, py, pc) in enumerate(others):
            pltpu.make_async_remote_copy(
                src_ref=sv, dst_ref=svo.at[4 * px + 2 * py + pc], send_sem=ssend.at[j], recv_sem=srecv.at[j],
                device_id=(px, py, pc), device_id_type=MESH).wait_recv()
        for cp in sends:
            cp.wait_send()
        local.wait()

    anyspec = pl.BlockSpec(memory_space=pl.ANY)
    return pl.pallas_call(
        body, name="gather_small", in_specs=[anyspec], out_specs=anyspec,
        out_shape=jax.ShapeDtypeStruct((8,) + small.shape, small.dtype),
        scratch_shapes=[pltpu.SemaphoreType.DMA((7,)), pltpu.SemaphoreType.DMA((7,)), pltpu.SemaphoreType.DMA],
    )(small)


def _sibling_exchange(arrs):
    n = len(arrs)

    def body(*refs):
        ins, outs = refs[:n], refs[n:2 * n]
        send_sems, recv_sems = refs[2 * n:]
        x, y, c = _mesh_pos()
        cps = [pltpu.make_async_remote_copy(
            src_ref=ins[w], dst_ref=outs[w], send_sem=send_sems.at[w], recv_sem=recv_sems.at[w],
            device_id=(x, y, 1 - c), device_id_type=MESH) for w in range(n)]
        for cp in cps:
            cp.start()
        for cp in cps:
            cp.wait_recv()
        for cp in cps:
            cp.wait_send()

    anyspec = pl.BlockSpec(memory_space=pl.ANY)
    return pl.pallas_call(
        body, name="sibling_exchange",
        in_specs=[anyspec] * n, out_specs=[anyspec] * n,
        out_shape=[jax.ShapeDtypeStruct(a.shape, a.dtype) for a in arrs],
        scratch_shapes=[pltpu.SemaphoreType.DMA((n,)), pltpu.SemaphoreType.DMA((n,))],
    )(*arrs)


def _sum_stack(own, recv, name):
    _, R, C = recv.shape
    tr = _row_tile(R, 256, 16)

    def body(g_ref, r_ref, o_ref):
        x, y, _ = _mesh_pos()
        kme = 2 * x + y
        acc = g_ref[kme].astype(F32)
        for d in range(1, N_CHIP):
            acc = acc + r_ref[(kme + d) % N_CHIP].astype(F32)
        o_ref[...] = acc

    spec = pl.BlockSpec((N_CHIP, tr, C), lambda i: (0, i, 0))
    return pl.pallas_call(
        body, name=name, grid=(R // tr,), in_specs=[spec, spec],
        out_specs=pl.BlockSpec((tr, C), lambda i: (i, 0)),
        out_shape=jax.ShapeDtypeStruct((R, C), F32),
        compiler_params=_params(("parallel",)),
    )(own, recv)


def _adam_math(w, g, m, v):
    m2 = ADAM_B1 * m + (1.0 - ADAM_B1) * g
    v2 = ADAM_B2 * v + (1.0 - ADAM_B2) * (g * g)
    m_hat = m2 / (1.0 - ADAM_B1 ** ADAM_STEP)
    v_hat = v2 / (1.0 - ADAM_B2 ** ADAM_STEP)
    delta = -ADAM_LR * (m_hat / (jnp.sqrt(v_hat) + ADAM_EPS) + ADAM_WD * w)
    return delta, m2, v2


def _adamw(w, m, v, s0, s1, name):
    R, C = w.shape
    tr = _row_tile(R, 256, 8)

    def body(w_ref, m_ref, v_ref, a_ref, b_ref, g_ref, d_ref, m2_ref, v2_ref):
        g = a_ref[...] + b_ref[...]
        delta, m2, v2 = _adam_math(w_ref[...], g, m_ref[...], v_ref[...])
        g_ref[...] = g
        d_ref[...] = delta
        m2_ref[...] = m2
        v2_ref[...] = v2

    spec = pl.BlockSpec((tr, C), lambda i: (i, 0))
    shp = jax.ShapeDtypeStruct((R, C), F32)
    return pl.pallas_call(
        body, name=name, grid=(R // tr,), in_specs=[spec] * 5, out_specs=[spec] * 4, out_shape=[shp] * 4,
        compiler_params=_params(("parallel",), VMEM_BIG),
    )(w, m, v, s0, s1)


def _adamw_small(ws, ms, vs, gathered):
    n = len(SMALL)

    def body(*refs):
        w_refs, m_refs, v_refs, s_ref = refs[:n], refs[n:2 * n], refs[2 * n:3 * n], refs[3 * n]
        outs = refs[3 * n + 1:]
        g_all = s_ref[0]
        for d in range(1, 8):
            g_all = g_all + s_ref[d]
        off = 0
        for i, (_, width) in enumerate(SMALL):
            g = g_all[:, off:off + width]
            delta, m2, v2 = _adam_math(w_refs[i][...], g, m_refs[i][...], v_refs[i][...])
            for kind, val in enumerate((g, delta, m2, v2)):
                outs[kind * n + i][...] = val
            off += width + (-width % LANE)

    shapes = [jax.ShapeDtypeStruct((1, width), F32) for _, width in SMALL]
    res = pl.pallas_call(body, name="adamw_small", out_shape=shapes * 4)(*ws, *ms, *vs, gathered)
    return [dict(zip([nm for nm, _ in SMALL], res[kind * n:(kind + 1) * n])) for kind in range(4)]


SMALL = (("g_mix", 1024), ("g_ffn", 1024), ("g_ret_norm", 512), ("g_fox_q", 64), ("g_fox_k", 64), ("b_forget", 8))
SMALL_W = 3072


def _pack_small(parts):
    cols = []
    for (name, n) in SMALL:
        p = parts[name].reshape(1, -1)[:, :n]
        pad = -n % LANE
        cols.append(jnp.pad(p, ((0, 0), (0, pad))) if pad else p)
    used = sum(c.shape[1] for c in cols)
    cols.append(jnp.zeros((1, SMALL_W - used), F32))
    return jnp.concatenate(cols, axis=1)


def kernel(x, g_mix, w_in, b_forget, g_ret_norm, w_ret_o, g_fox_q, g_fox_k, w_fox_o, w_out, g_ffn, w_gate, w_up, w_down, loss_target, m_g_mix, m_w_in, m_b_forget, m_g_ret_norm, m_w_ret_o, m_g_fox_q, m_g_fox_k, m_w_fox_o, m_w_out, m_g_ffn, m_w_gate, m_w_up, m_w_down, v_g_mix, v_w_in, v_b_forget, v_g_ret_norm, v_w_ret_o, v_g_fox_q, v_g_fox_k, v_w_fox_o, v_w_out, v_g_ffn, v_w_gate, v_w_up, v_w_down):
    T = x.shape[1]
    xs = x[0]
    tgt = loss_target[0]
    big_names = ("w_in", "w_ret_o", "w_fox_o", "w_out", "w_gate", "w_up", "w_down")
    tr = lambda a: jnp.swapaxes(a[0], 0, 1)
    big_w = dict(w_in=w_in[0], w_ret_o=w_ret_o[0], w_fox_o=w_fox_o[0], w_out=w_out[0], w_gate=tr(w_gate),
                 w_up=tr(w_up), w_down=w_down[0])
    big_m = dict(w_in=m_w_in[0], w_ret_o=m_w_ret_o[0], w_fox_o=m_w_fox_o[0], w_out=m_w_out[0], w_gate=tr(m_w_gate),
                 w_up=tr(m_w_up), w_down=m_w_down[0])
    big_v = dict(w_in=v_w_in[0], w_ret_o=v_w_ret_o[0], w_fox_o=v_w_fox_o[0], w_out=v_w_out[0], w_gate=tr(v_w_gate),
                 w_up=tr(v_w_up), w_down=v_w_down[0])
    small_w = dict(g_mix=g_mix, g_ffn=g_ffn, g_ret_norm=g_ret_norm, g_fox_q=g_fox_q, g_fox_k=g_fox_k, b_forget=b_forget)
    small_m = dict(g_mix=m_g_mix, g_ffn=m_g_ffn, g_ret_norm=m_g_ret_norm, g_fox_q=m_g_fox_q, g_fox_k=m_g_fox_k,
                   b_forget=m_b_forget)
    small_v = dict(g_mix=v_g_mix, g_ffn=v_g_ffn, g_ret_norm=v_g_ret_norm, g_fox_q=v_g_fox_q, g_fox_k=v_g_fox_k,
                   b_forget=v_b_forget)

    stacks = {n: _staged_place(big_w[n], "place_" + n) for n in big_names}
    s_in = _gather_two_level(stacks["w_in"], "gather_w_in")
    w_a, w_ff = _assemble_w_in(s_in)
    b_pad = jnp.pad(b_forget, ((0, 0), (0, LANE - FOX_H)))
    cos_t, sin_t = _rope_tables(T)
    consts = _ret_consts()

    h = _rms_cast(xs, g_mix)
    z_a, (s_ro, s_fo, s_out, s_gate) = _mm_nn(
        h, w_a, "proj_in", BF, tm=1024,
        push=(None, [stacks["w_ret_o"], stacks["w_fox_o"], stacks["w_out"], stacks["w_gate"]]))
    (qr, kr, qf, kf, vf, c_cum, nmax, z_ff), (s_up,) = _mix_prep(
        z_a, h, w_ff, cos_t, sin_t, b_pad, g_fox_q, g_fox_k, push=(None, [stacks["w_up"]]))
    jlo, end_both, end_last, tame = _prune_tables(c_cum, nmax, FOX_SUB)
    o_raw, u_r, states = _ret_fwd(qr, kr, z_a, g_ret_norm, consts)
    o_fox, q2 = _fox_fwd(jlo, tame, qf, kf, vf)
    (y_r, y_f, mrg, x2, h2, o_cat), (s_down,) = _merge_out(u_r, o_fox, z_a, xs, g_ffn, s_ro, s_fo, s_out,
                                                            push=(None, [stacks["w_down"]]))
    sa, sb, act, dy, loss_vec = _ffn_fwd(h2, x2, tgt, s_gate, s_up, s_down)
    loss = lax.psum(0.5 / D_MODEL * jnp.sum(loss_vec), ("x", "y", "c"))

    def scatter_job(grads):
        return (grads, [lax.empty(g.shape, g.dtype) for g in grads])

    dgp, dup, dx2, dg_ffn = _ffn_bwd(dy, sa, sb, x2, g_ffn, s_gate, s_up, s_down)
    (g_gate, _), (g_up, _), (g_down, _) = (_grad_astack(dgp, h2, "gw_gate"), _grad_astack(dup, h2, "gw_up"),
                                           _grad_astack(act, dy, "gw_down"))
    (d_yr, d_yf, dz_gt, dz_a, d_o, do_fox, dg_ret), (r_gate, r_up) = _out_bwd(
        dx2, z_a, y_r, y_f, o_raw, o_fox, g_ret_norm, s_ro, s_fo, s_out, push=scatter_job([g_gate, g_up]))
    dz_ret, (r_down,) = _ret_bwd(d_o, qr, kr, z_a, states, cos_t, sin_t, consts, push=scatter_job([g_down]))
    dq_f, dk_f, dv_f = _fox_bwd(end_both, end_last, q2, kf, vf, do_fox)
    g_mid = [_grad_colstack(u_r, d_yr, "gw_ret_o", 256), _grad_colstack(o_cat, d_yf, "gw_fox_o", 256),
             _grad_plain(mrg, dx2, "gw_out", BF).reshape(N_CHIP, 256, D_MODEL)]
    (dz_fox, dz_ff, dg_q, dg_k, db_f), (r_ro, r_fo, r_out) = _fox_post_bwd(
        dq_f, dk_f, dv_f, z_a, z_ff, b_pad, g_fox_q, g_fox_k, push=scatter_job(g_mid))
    gi_ret, gi_gt, gi_ff = _grad_multi(h, [dz_ret, dz_gt, dz_ff], "gw_in_small")
    gi_fox, gi_a = _grad_multi(h, [dz_fox, dz_a], "gw_in_large")
    g_in = _pack_g_in(gi_ret, gi_gt, gi_fox, gi_a, gi_ff)
    (grad_x, dg_mix), (r_in,) = _in_bwd(dz_ret, dz_gt, dz_fox, dz_a, dz_ff, w_a, w_ff, xs, g_mix, dx2,
                                        push=scatter_job([g_in]))
    small_g = _pack_small(dict(g_mix=dg_mix, g_ffn=dg_ffn, g_ret_norm=dg_ret, g_fox_q=dg_q, g_fox_k=dg_k, b_forget=db_f))

    small_all = _gather_small(small_g)
    sums = [_sum_stack(g, r, "sum_" + n) for g, r, n in zip(
        [g_in] + g_mid + [g_gate, g_up, g_down], [r_in, r_ro, r_fo, r_out, r_gate, r_up, r_down], big_names)]
    sib = _sibling_exchange(sums)
    big_out = {n: _adamw(big_w[n], big_m[n], big_v[n], sums[i], sib[i], "adamw_" + n) for i, n in enumerate(big_names)}
    small_out = _adamw_small(*[[d[nm] for nm, _ in SMALL] for d in (small_w, small_m, small_v)], small_all)

    order = ("g_mix", "w_in", "b_forget", "g_ret_norm", "w_ret_o", "g_fox_q", "g_fox_k", "w_fox_o", "w_out", "g_ffn",
             "w_gate", "w_up", "w_down")
    outs = [loss, grad_x[None]]
    for idx in range(4):
        for n in order:
            if n in ("w_gate", "w_up"):
                outs.append(jnp.swapaxes(big_out[n][idx], 0, 1)[None])
            else:
                outs.append(big_out[n][idx][None] if n in big_out else small_out[idx][n])
    return tuple(outs)
```

```python
import functools

import numpy as np
import jax
import jax.numpy as jnp
from jax import lax
from jax.experimental import pallas as pl
from jax.experimental.pallas import tpu as pltpu

F32 = jnp.float32
BF = jnp.bfloat16
MESH = pl.DeviceIdType.MESH

D_MODEL = 1024
D_FF = 2816
N_CHIP = 4
FF_SH = D_FF // N_CHIP
IN_COLS = 5128
IN_SH = IN_COLS // N_CHIP
RET_H, RET_DV = 4, 128
FOX_H, FOX_D = 8, 64
CHUNK = 256
EPS = 1e-6
NEG = -1e30
LANE = 128
C_RET, C_GT, C_FOX, C_A, C_END = 0, 1024, 1536, 3072, 5120
L_CQ, L_CK, L_LSE, L_MAX = 64, 67, 70, 73

ADAM_LR, ADAM_B1, ADAM_B2, ADAM_EPS, ADAM_WD, ADAM_STEP = 0.001, 0.9, 0.999, 1e-08, 0.01, 10
VMEM_BIG = 56 * 1024 * 1024
VMEM_HUGE = 60 * 1024 * 1024
GRAD_TK = 2048
FFN_TM = 512
FOX_SUB = 512


def _nn(a, b):
    return lax.dot_general(a, b, (((1,), (0,)), ((), ())), preferred_element_type=F32)


def _nt(a, b):
    return lax.dot_general(a, b, (((1,), (1,)), ((), ())), preferred_element_type=F32)


def _tn(a, b):
    return lax.dot_general(a, b, (((0,), (0,)), ((), ())), preferred_element_type=F32)


def _split3(x):
    hi = x.astype(BF)
    r = x - hi.astype(F32)
    mid = r.astype(BF)
    lo = (r - mid.astype(F32)).astype(BF)
    return hi, mid, lo


def _sigmoid(x):
    return 0.5 * jnp.tanh(0.5 * x) + 0.5


def _swap32(x):
    lane = lax.broadcasted_iota(jnp.int32, x.shape, 1)
    return jnp.where(lane < 32, pltpu.roll(x, 96, 1), pltpu.roll(x, 32, 1))


def _params(sem, vmem=None):
    return pltpu.CompilerParams(dimension_semantics=sem, vmem_limit_bytes=vmem)


def _row_tile(rows, cap, mult):
    return max(d for d in range(mult, cap + 1, mult) if rows % d == 0)


def _assemble_w_in(stack, tr=256):
    def body(s_ref, a_ref, f_ref):
        full = jnp.concatenate([s_ref[k].astype(F32) for k in range(N_CHIP)], axis=-1)
        a_ref[...] = jnp.concatenate([full[:, :3072], full[:, 3080:IN_COLS]], axis=-1).astype(BF)
        f_ref[...] = jnp.concatenate([full[:, 3072:3080], jnp.zeros((tr, LANE - FOX_H), F32)], axis=-1).astype(BF)

    return pl.pallas_call(
        body, name="assemble_w_in", grid=(D_MODEL // tr,),
        in_specs=[pl.BlockSpec((N_CHIP, tr, IN_SH), lambda i: (0, i, 0))],
        out_specs=[pl.BlockSpec((tr, C_END), lambda i: (i, 0)), pl.BlockSpec((tr, LANE), lambda i: (i, 0))],
        out_shape=[jax.ShapeDtypeStruct((D_MODEL, C_END), BF), jax.ShapeDtypeStruct((D_MODEL, LANE), BF)],
        compiler_params=_params(("parallel",), VMEM_BIG),
    )(stack)


def _pack_g_in(g_ret, g_gt, g_fox, g_a, g_ff, tr=256):
    def body(r_ref, t_ref, x_ref, a_ref, f_ref, o_ref):
        r, t, x, a, f = [ref[...].astype(F32) for ref in (r_ref, t_ref, x_ref, a_ref, f_ref)]
        full = jnp.concatenate([r, t, x, f[:, :FOX_H], a], axis=-1)
        for k in range(N_CHIP):
            o_ref[k] = full[:, k * IN_SH:(k + 1) * IN_SH].astype(BF)

    def spec(w):
        return pl.BlockSpec((tr, w), lambda i: (i, 0))

    return pl.pallas_call(
        body, name="pack_g_in", grid=(D_MODEL // tr,),
        in_specs=[spec(1024), spec(512), spec(1536), spec(2048), spec(LANE)],
        out_specs=pl.BlockSpec((N_CHIP, tr, IN_SH), lambda i: (0, i, 0)),
        out_shape=jax.ShapeDtypeStruct((N_CHIP, D_MODEL, IN_SH), BF),
        compiler_params=_params(("parallel",), VMEM_BIG),
    )(g_ret, g_gt, g_fox, g_a, g_ff)


def _rms_cast(x, g, tm=512):
    T = x.shape[0]

    def body(x_ref, g_ref, o_ref):
        xv = x_ref[...]
        r = lax.rsqrt(jnp.mean(xv * xv, axis=-1, keepdims=True) + EPS)
        o_ref[...] = (xv * r * g_ref[...]).astype(BF)

    return pl.pallas_call(
        body, name="rms_cast", grid=(T // tm,),
        in_specs=[pl.BlockSpec((tm, D_MODEL), lambda i: (i, 0)), pl.BlockSpec((1, D_MODEL), lambda i: (0, 0))],
        out_specs=pl.BlockSpec((tm, D_MODEL), lambda i: (i, 0)),
        out_shape=jax.ShapeDtypeStruct((T, D_MODEL), BF),
        compiler_params=_params(("parallel",)),
    )(x, g)


def _hosted_call(body, name, grid, in_specs, out_specs, out_shape, scratch_shapes, vmem, args, push):
    sem = ("arbitrary",) * len(grid)
    if push is None:
        res = pl.pallas_call(body, name=name, grid=grid, in_specs=in_specs, out_specs=out_specs, out_shape=out_shape,
                             scratch_shapes=scratch_shapes, compiler_params=_params(sem, vmem))(*args)
        return list(res), []
    srcs, lands = push
    ns, nl, n_in, n_out = (0 if srcs is None else len(srcs)), len(lands), len(in_specs), len(out_specs)
    n_scr = len(scratch_shapes)

    def wrapped(*refs):
        pos = n_in + ns + nl
        ins, x_in = refs[:n_in], refs[n_in:pos]
        outs, x_out = refs[pos:pos + n_out], refs[pos + n_out:pos + n_out + nl]
        scr = refs[pos + n_out + nl:pos + n_out + nl + n_scr]
        ssem, rsem = refs[-2], refs[-1]
        src = None if srcs is None else x_in[:ns]
        ids = [pl.program_id(a) for a in range(len(grid))]
        first = functools.reduce(lambda p, q: p & q, [ids[a] == 0 for a in range(len(grid))])
        last = functools.reduce(lambda p, q: p & q, [ids[a] == grid[a] - 1 for a in range(len(grid))])

        @pl.when(first)
        def _():
            for cp in _push_copies(src, x_out, ssem, rsem, False):
                cp.start()

        body(*ins, *outs, *scr)

        @pl.when(last)
        def _():
            for cp in _push_copies(src, x_out, ssem, rsem, True):
                cp.wait_recv()
                cp.wait_send()

    anyspec = pl.BlockSpec(memory_space=pl.ANY)
    extra = ([] if srcs is None else list(srcs)) + list(lands)
    res = pl.pallas_call(
        wrapped, name=name, grid=grid,
        in_specs=list(in_specs) + [anyspec] * len(extra), out_specs=list(out_specs) + [anyspec] * nl,
        out_shape=list(out_shape) + [jax.ShapeDtypeStruct(a.shape, a.dtype) for a in lands],
        input_output_aliases={n_in + ns + i: n_out + i for i in range(nl)},
        scratch_shapes=list(scratch_shapes) + [pltpu.SemaphoreType.DMA((3 * nl,)), pltpu.SemaphoreType.DMA((3 * nl,))],
        compiler_params=_params(sem, vmem),
    )(*args, *extra)
    return list(res[:n_out]), list(res[n_out:])


def _mm_nn(a, b, name, out_dtype, tm=512, tn=1024, push=None):
    M, K = a.shape
    N = b.shape[1]
    tn = min(tn, N)

    def body(a_ref, b_ref, o_ref):
        o_ref[...] = _nn(a_ref[...], b_ref[...]).astype(o_ref.dtype)

    (out,), lands = _hosted_call(
        body, name, (N // tn, M // tm),
        [pl.BlockSpec((tm, K), lambda j, i: (i, 0)), pl.BlockSpec((K, tn), lambda j, i: (0, j))],
        [pl.BlockSpec((tm, tn), lambda j, i: (i, j))], [jax.ShapeDtypeStruct((M, N), out_dtype)], [], None, (a, b), push)
    return out, lands


def _mm_tn(a, b, name, grid, a_spec, b_spec, o_spec, out_shape, acc_shape):
    nk = grid[-1]

    def body(a_ref, b_ref, o_ref, acc):
        k = pl.program_id(len(grid) - 1)

        @pl.when(k == 0)
        def _():
            acc[...] = jnp.zeros(acc.shape, F32)

        acc[...] += _tn(a_ref[...].astype(BF), b_ref[...].astype(BF))

        @pl.when(k == nk - 1)
        def _():
            o_ref[...] = acc[...].astype(o_ref.dtype)

    return pl.pallas_call(
        body, name=name, grid=grid, in_specs=[a_spec, b_spec], out_specs=o_spec, out_shape=out_shape,
        scratch_shapes=[pltpu.VMEM(acc_shape, F32)],
        compiler_params=_params(("parallel",) * (len(grid) - 1) + ("arbitrary",), VMEM_BIG),
    )(a, b)


def _grad_plain(a, b, name, out_dtype, tk=GRAD_TK, tn=1024):
    T, M = a.shape
    N = b.shape[1]
    tn = min(tn, N)
    return _mm_tn(a, b, name, (N // tn, T // tk),
                  pl.BlockSpec((tk, M), lambda j, k: (k, 0)), pl.BlockSpec((tk, tn), lambda j, k: (k, j)),
                  pl.BlockSpec((M, tn), lambda j, k: (0, j)), jax.ShapeDtypeStruct((M, N), out_dtype), (M, tn))


def _grad_multi(a, bs, name, tk=1024):
    T, M = a.shape
    n = len(bs)
    nk = T // tk

    def body(*refs):
        a_ref, b_refs, o_refs, accs = refs[0], refs[1:1 + n], refs[1 + n:1 + 2 * n], refs[1 + 2 * n:]
        k = pl.program_id(0)

        @pl.when(k == 0)
        def _():
            for acc in accs:
                acc[...] = jnp.zeros(acc.shape, F32)

        av = a_ref[...]
        for i in range(n):
            accs[i][...] += _tn(av, b_refs[i][...])

        @pl.when(k == nk - 1)
        def _():
            for i in range(n):
                o_refs[i][...] = accs[i][...].astype(BF)

    widths = [b.shape[1] for b in bs]
    return pl.pallas_call(
        body, name=name, grid=(nk,),
        in_specs=[pl.BlockSpec((tk, M), lambda k: (k, 0))] + [pl.BlockSpec((tk, w), lambda k: (k, 0)) for w in widths],
        out_specs=[pl.BlockSpec((M, w), lambda k: (0, 0)) for w in widths],
        out_shape=[jax.ShapeDtypeStruct((M, w), BF) for w in widths],
        scratch_shapes=[pltpu.VMEM((M, w), F32) for w in widths],
        compiler_params=_params(("arbitrary",), VMEM_BIG),
    )(a, *bs)


def _grad_colstack(a, b, name, wcol, tk=GRAD_TK):
    T, M = a.shape
    N = b.shape[1]
    S = N // wcol
    nk = T // tk

    def body(a_ref, b_ref, o_ref, acc):
        k = pl.program_id(0)

        @pl.when(k == 0)
        def _():
            acc[...] = jnp.zeros(acc.shape, F32)

        acc[...] += _tn(a_ref[...], b_ref[...])

        @pl.when(k == nk - 1)
        def _():
            for s in range(S):
                o_ref[s] = acc[:, s * wcol:(s + 1) * wcol].astype(BF)

    return pl.pallas_call(
        body, name=name, grid=(nk,),
        in_specs=[pl.BlockSpec((tk, M), lambda k: (k, 0)), pl.BlockSpec((tk, N), lambda k: (k, 0))],
        out_specs=pl.BlockSpec((S, M, wcol), lambda k: (0, 0, 0)), out_shape=jax.ShapeDtypeStruct((S, M, wcol), BF),
        scratch_shapes=[pltpu.VMEM((M, N), F32)], compiler_params=_params(("arbitrary",), VMEM_BIG),
    )(a, b)


def _grad_astack(a, b, name, tk=1024, push=None):
    S, T, m = a.shape
    N = b.shape[1]
    nk = T // tk

    def body(a_ref, b_ref, o_ref, acc):
        k = pl.program_id(0)

        @pl.when(k == 0)
        def _():
            acc[...] = jnp.zeros(acc.shape, F32)

        bb = b_ref[...].astype(BF)
        for s in range(S):
            acc[s] += _tn(a_ref[s], bb)

        @pl.when(k == nk - 1)
        def _():
            o_ref[...] = acc[...].astype(BF)

    (out,), lands = _hosted_call(
        body, name, (nk,),
        [pl.BlockSpec((S, tk, m), lambda k: (0, k, 0)), pl.BlockSpec((tk, N), lambda k: (k, 0))],
        [pl.BlockSpec((S, m, N), lambda k: (0, 0, 0))], [jax.ShapeDtypeStruct((S, m, N), BF)],
        [pltpu.VMEM((S, m, N), F32)], VMEM_BIG, (a, b), push)
    return out, lands


def _rope_tables(T):
    half = 32
    pos = np.arange(T, dtype=np.float32)
    inv_freq = (np.float32(1.0) / (np.float32(10000.0) ** (np.arange(half, dtype=np.float32) / np.float32(half)))).astype(np.float32)
    ang = (pos[:, None] * inv_freq[None, :]).astype(np.float32)
    cos, sin = np.cos(ang).astype(np.float32), np.sin(ang).astype(np.float32)
    z = np.zeros((T, 64), np.float32)
    return (jnp.asarray(np.concatenate([cos, cos, z], axis=-1)), jnp.asarray(np.concatenate([-sin, sin, z], axis=-1)))


def _ret_consts():
    h = np.arange(RET_H, dtype=np.float32)
    log_g = np.log1p(-(np.float32(2.0) ** (-5.0 - h))).astype(np.float32)
    idx = np.arange(CHUNK, dtype=np.float32)
    diff = idx[:, None] - idx[None, :]
    decay = np.where(diff[None] >= 0, np.exp(np.maximum(diff, 0.0)[None] * log_g[:, None, None]), 0.0)
    zeta = np.exp((CHUNK - 1.0 - idx)[None, :] * log_g[:, None])
    xi = np.exp((idx + 1.0)[None, :] * log_g[:, None])
    gc = np.exp(CHUNK * log_g)
    bc = lambda v: np.broadcast_to(v[:, :, None], (RET_H, CHUNK, LANE)).astype(np.float32)
    gcb = np.broadcast_to(gc[:, None, None], (RET_H, LANE, LANE)).astype(np.float32)
    return (jnp.asarray(decay.astype(np.float32)), jnp.asarray(bc(zeta)), jnp.asarray(bc(xi)), jnp.asarray(gcb))


def _mix_prep(z_a, h, w_ff, cos_t, sin_t, b_f, g_q, g_k, tm=256, push=None):
    T = z_a.shape[0]

    def body(zqk_ref, zf_ref, h_ref, wff_ref, cos_ref, sin_ref, b_ref, g_ref, seg_ref, segt_ref,
             qr_ref, kr_ref, qf_ref, kf_ref, vf_ref, c_ref, nmax_ref, zff_ref, carry):
        i = pl.program_id(0)
        zff = _nn(h_ref[...], wff_ref[...])
        zff_ref[...] = zff

        @pl.when(i == 0)
        def _():
            carry[...] = jnp.zeros(carry.shape, F32)
            nmax_ref[...] = jnp.zeros(nmax_ref.shape, F32)

        lane = lax.broadcasted_iota(jnp.int32, (tm, LANE), 1)
        zpad = jnp.zeros((tm, 64), F32)
        cosv, sinv = cos_ref[...], sin_ref[...]
        zqk = zqk_ref[...].astype(F32)
        for h in range(RET_H):
            for src, dst, scale in ((0, qr_ref, 1.0), (256, kr_ref, 0.125)):
                xh = jnp.concatenate([zqk[:, src + 64 * h: src + 64 * h + 64], zpad], axis=-1)
                rot = xh * cosv + _swap32(xh) * sinv
                dst[h] = (rot * scale).astype(BF)

        lf_in = zff + b_ref[...]
        logf = jnp.minimum(lf_in, 0.0) - jnp.log(1.0 + jnp.exp(-jnp.abs(lf_in)))
        row = lax.broadcasted_iota(jnp.int32, (tm, tm), 0)
        col = lax.broadcasted_iota(jnp.int32, (tm, tm), 1)
        tri = (row >= col).astype(BF)
        hi, mid, lo = _split3(logf)
        cs = _nn(tri, hi) + _nn(tri, mid) + _nn(tri, lo) + carry[...]
        carry[...] = cs[tm - 1:tm, :]
        c_ref[...] = cs

        def seg_sum(v):
            return sum(_nn(t, seg_ref[...]) for t in _split3(v))

        zf = zf_ref[...].astype(F32)
        xqk = zf[:, :1024]
        rinv = lax.rsqrt(seg_sum(xqk * xqk) * (1.0 / FOX_D) + EPS)
        xn = xqk * sum(_nn(t, segt_ref[...]) for t in _split3(rinv)) * g_ref[...]
        nmax_ref[...] = jnp.maximum(nmax_ref[...], jnp.max(seg_sum(xn * xn), axis=0, keepdims=True))

        one = jnp.ones((tm, LANE), F32)
        for h in range(FOX_H):
            c = cs[:, h:h + 1]
            chi, cmid, clo = [t.astype(F32) for t in _split3(c)]
            qn = xn[:, 64 * h:64 * h + 64]
            kn = xn[:, 512 + 64 * h:512 + 64 * h + 64]
            vh = zf[:, 1024 + 64 * h:1024 + 64 * h + 64]
            qa = jnp.concatenate([qn, zpad], axis=-1)
            qa = jnp.where(lane == L_CQ, chi, jnp.where(lane == L_CQ + 1, cmid, jnp.where(lane == L_CQ + 2, clo, qa)))
            qa = jnp.where((lane >= L_CK) & (lane < L_CK + 3), one, qa)
            ka = jnp.concatenate([kn, zpad], axis=-1)
            ka = jnp.where(lane == L_CK, -chi, jnp.where(lane == L_CK + 1, -cmid, jnp.where(lane == L_CK + 2, -clo, ka)))
            ka = jnp.where(((lane >= L_CQ) & (lane < L_CQ + 3)) | ((lane >= L_LSE) & (lane < L_MAX + 3)), one, ka)
            va = jnp.concatenate([vh, zpad], axis=-1)
            va = jnp.where((lane >= 64) & (lane < 67), one, va)
            qf_ref[h] = qa.astype(BF)
            kf_ref[h] = ka.astype(BF)
            vf_ref[h] = va.astype(BF)

    hspec4 = pl.BlockSpec((RET_H, tm, LANE), lambda i: (0, i, 0))
    hspec8 = pl.BlockSpec((FOX_H, tm, LANE), lambda i: (0, i, 0))
    const = lambda r, w: pl.BlockSpec((r, w), lambda i: (0, 0))
    seg = _segment_matrix()
    g_all = jnp.concatenate([jnp.tile(g_q * 0.125, (1, FOX_H)), jnp.tile(g_k, (1, FOX_H))], axis=1)
    return _hosted_call(
        body, "mix_prep", (T // tm,),
        [pl.BlockSpec((tm, 512), lambda i: (i, 0)), pl.BlockSpec((tm, 1536), lambda i: (i, 1)),
         pl.BlockSpec((tm, D_MODEL), lambda i: (i, 0)), const(D_MODEL, LANE), pl.BlockSpec((tm, LANE), lambda i: (i, 0)),
         pl.BlockSpec((tm, LANE), lambda i: (i, 0)), const(1, LANE), const(1, 1024), const(1024, LANE), const(LANE, 1024)],
        [hspec4, hspec4, hspec8, hspec8, hspec8, pl.BlockSpec((tm, LANE), lambda i: (i, 0)), const(1, LANE),
         pl.BlockSpec((tm, LANE), lambda i: (i, 0))],
        [jax.ShapeDtypeStruct((RET_H, T, LANE), BF)] * 2 + [jax.ShapeDtypeStruct((FOX_H, T, LANE), BF)] * 3
        + [jax.ShapeDtypeStruct((T, LANE), F32), jax.ShapeDtypeStruct((1, LANE), F32), jax.ShapeDtypeStruct((T, LANE), F32)],
        [pltpu.VMEM((1, LANE), F32)], VMEM_BIG, (z_a, z_a, h, w_ff, cos_t, sin_t, b_f, g_all, seg, seg.T), push)


def _segment_matrix():
    m = np.zeros((2 * FOX_H * FOX_D, LANE), np.float32)
    m[np.arange(2 * FOX_H * FOX_D), np.arange(2 * FOX_H * FOX_D) // FOX_D] = 1.0
    return jnp.asarray(m, dtype=BF)


def _ret_fwd(qr, kr, z_a, g_ret, consts, tt=512):
    T = z_a.shape[0]
    nch = tt // CHUNK
    decay, zeta, xi, gcb = consts

    def body(q_ref, k_ref, v_ref, gt_ref, g_ref, d_ref, ze_ref, xi_ref, gc_ref, o_ref, u_ref, st_ref, r_sc):
        i = pl.program_id(0)

        @pl.when(i == 0)
        def _():
            r_sc[...] = jnp.zeros(r_sc.shape, F32)

        for c in range(nch):
            rows = slice(c * CHUNK, (c + 1) * CHUNK)
            for h in range(RET_H):
                cols = slice(h * RET_DV, (h + 1) * RET_DV)
                q, k = q_ref[h, rows, :], k_ref[h, rows, :]
                v32 = v_ref[rows, cols].astype(F32)
                r = r_sc[h]
                st_ref[h, c * CHUNK:c * CHUNK + LANE, :] = r
                s = _nt(q, k) * d_ref[h]
                o = _nn(s.astype(BF), v32.astype(BF)) + _nn(q, r.astype(BF)) * xi_ref[h]
                r_sc[h] = gc_ref[h] * r + _tn(k, (v32 * ze_ref[h]).astype(BF))
                o_ref[rows, cols] = o
                mu = jnp.mean(o, axis=-1, keepdims=True)
                xc = o - mu
                on = xc * lax.rsqrt(jnp.mean(xc * xc, axis=-1, keepdims=True) + EPS)
                gt = gt_ref[rows, cols].astype(F32)
                u_ref[rows, cols] = (gt * _sigmoid(gt) * (on * g_ref[:, cols])).astype(BF)

    hspec = pl.BlockSpec((RET_H, tt, LANE), lambda i: (0, i, 0))
    cspec = pl.BlockSpec((RET_H, CHUNK, LANE), lambda i: (0, 0, 0))
    dspec = pl.BlockSpec((RET_H, CHUNK, CHUNK), lambda i: (0, 0, 0))
    sspec = pl.BlockSpec((RET_H, LANE, LANE), lambda i: (0, 0, 0))
    return pl.pallas_call(
        body, name="ret_fwd", grid=(T // tt,),
        in_specs=[hspec, hspec, pl.BlockSpec((tt, 512), lambda i: (i, 1)), pl.BlockSpec((tt, 512), lambda i: (i, 2)),
                  pl.BlockSpec((1, 512), lambda i: (0, 0)), dspec, cspec, cspec, sspec],
        out_specs=[pl.BlockSpec((tt, 512), lambda i: (i, 0)), pl.BlockSpec((tt, 512), lambda i: (i, 0)), hspec],
        out_shape=[jax.ShapeDtypeStruct((T, 512), F32), jax.ShapeDtypeStruct((T, 512), BF),
                   jax.ShapeDtypeStruct((RET_H, T, LANE), F32)],
        scratch_shapes=[pltpu.VMEM((RET_H, LANE, LANE), F32)],
        compiler_params=_params(("arbitrary",), VMEM_BIG),
    )(qr, kr, z_a, z_a, g_ret, decay, zeta, xi, gcb)


PRUNE_LOG = -110.0
TAME_LOGIT_SPAN = 60.0


def _prune_tables(c, nmax, sub):
    n = c.shape[0] // sub
    u = jnp.sqrt(nmax[0, :FOX_H] * nmax[0, FOX_H:2 * FOX_H]) * 1.02 + 0.5
    first = c[0::sub, :FOX_H].T
    last = c[sub - 1::sub, :FOX_H].T
    blk = jnp.arange(n, dtype=jnp.int32)
    needed = (2.0 * u[:, None, None] + first[:, :, None] - last[:, None, :] >= PRUNE_LOG) | (blk[None, :] >= blk[:, None])[None]
    jlo = jnp.argmax(needed, axis=2).astype(jnp.int32)

    def end_of(key_block):
        reach = jlo[:, None, :] <= key_block[None, :, None]
        return (n - jnp.argmax(reach[:, :, ::-1], axis=2)).astype(jnp.int32)

    sup = 2 * jnp.arange(n // 2, dtype=jnp.int32)
    end_last = end_of(sup + 1)
    end_both = jnp.clip(end_of(sup), sup[None, :] + 2, end_last)
    tame = (2.0 * u < TAME_LOGIT_SPAN).astype(jnp.int32)
    return jlo, end_both, end_last, tame


def _fox_fwd(jlo, tame, q, k, v, sub=FOX_SUB):
    H, T, _ = q.shape
    tb = 2 * sub

    def body(js_ref, tame_ref, q_ref, k_ref, v_ref, o_ref, q2_ref, mx_sc, acc_sc):
        i = pl.program_id(1)
        hd = pl.program_id(0)
        starts = [jnp.minimum(js_ref[hd, 2 * i], 2 * i), jnp.minimum(js_ref[hd, 2 * i + 1], 2 * i)]
        lane = lax.broadcasted_iota(jnp.int32, (sub, LANE), 1)
        row = lax.broadcasted_iota(jnp.int32, (sub, sub), 0)
        col = lax.broadcasted_iota(jnp.int32, (sub, sub), 1)
        causal = row >= col
        qs = [q_ref[0:sub, :], q_ref[sub:tb, :]]
        d0 = pl.multiple_of(i * tb, tb)
        d1 = pl.multiple_of(i * tb + sub, sub)
        k0, k1 = k_ref[pl.ds(d0, sub), :], k_ref[pl.ds(d1, sub), :]
        v0, v1 = v_ref[pl.ds(d0, sub), :], v_ref[pl.ds(d1, sub), :]

        def lane_max(s):
            m = s[:, 0:LANE]
            for c in range(1, s.shape[1] // LANE):
                m = jnp.maximum(m, s[:, c * LANE:(c + 1) * LANE])
            return m

        def put3(base, first, val):
            hi, mid, lo = _split3(val)
            return jnp.where(lane == first, hi, jnp.where(lane == first + 1, mid, jnp.where(lane == first + 2, lo, base)))

        def row_max():
            mx_sc[...] = jnp.full(mx_sc.shape, NEG, F32)
            for a in range(2):
                def max_body(j, carry, a=a):
                    kb = k_ref[pl.ds(pl.multiple_of(j * sub, sub), sub), :]
                    mx_sc[a] = jnp.maximum(mx_sc[a], lane_max(_nt(qs[a], kb)))
                    return carry

                lax.fori_loop(starts[a], 2 * i, max_body, 0)
            mx = [jnp.maximum(mx_sc[0], lane_max(jnp.where(causal, _nt(qs[0], k0), NEG))),
                  jnp.maximum(jnp.maximum(mx_sc[1], lane_max(_nt(qs[1], k0))),
                              lane_max(jnp.where(causal, _nt(qs[1], k1), NEG)))]
            return [jnp.max(t, axis=1, keepdims=True) for t in mx]

        def diag_logit():
            return [jnp.sum(qs[a].astype(F32) * kd.astype(F32), axis=1, keepdims=True) for a, kd in enumerate((k0, k1))]

        def finish(ms):
            qm = [put3(qs[a], L_MAX, -ms[a]) for a in range(2)]
            acc_sc[...] = jnp.zeros(acc_sc.shape, F32)
            for a in range(2):
                def acc_body(j, carry, a=a):
                    off = pl.multiple_of(j * sub, sub)
                    acc_sc[a] += _nn(jnp.exp(_nt(qm[a], k_ref[pl.ds(off, sub), :])).astype(BF), v_ref[pl.ds(off, sub), :])
                    return carry

                lax.fori_loop(starts[a], 2 * i, acc_body, 0)

            def pv(qa, kk, vv, masked):
                p = jnp.exp(_nt(qa, kk))
                if masked:
                    p = jnp.where(causal, p, 0.0)
                return _nn(p.astype(BF), vv)

            accs = [acc_sc[0] + pv(qm[0], k0, v0, True),
                    acc_sc[1] + pv(qm[1], k0, v0, False) + pv(qm[1], k1, v1, True)]
            for a in range(2):
                rows = slice(a * sub, (a + 1) * sub)
                l = accs[a][:, 64:65]
                o_ref[rows, :] = jnp.where(lane < 64, accs[a] / l, 0.0)
                q2_ref[rows, :] = put3(qs[a], L_LSE, -(ms[a] + jnp.log(l)))

        tame = tame_ref[hd] == 1

        @pl.when(tame)
        def _():
            finish(diag_logit())

        @pl.when(jnp.logical_not(tame))
        def _():
            finish(row_max())

    blk = pl.BlockSpec((None, tb, LANE), lambda h, i, js, tm_: (h, i, 0))
    full = pl.BlockSpec((None, T, LANE), lambda h, i, js, tm_: (h, 0, 0))
    return pl.pallas_call(
        body, name="fox_fwd",
        grid_spec=pltpu.PrefetchScalarGridSpec(
            num_scalar_prefetch=2, grid=(H, T // tb), in_specs=[blk, full, full], out_specs=[blk, blk],
            scratch_shapes=[pltpu.VMEM((2, sub, LANE), F32), pltpu.VMEM((2, sub, LANE), F32)]),
        out_shape=[jax.ShapeDtypeStruct((H, T, LANE), F32), jax.ShapeDtypeStruct((H, T, LANE), BF)],
        compiler_params=_params(("parallel", "arbitrary"), VMEM_BIG),
    )(jlo, tame, q, k, v)


def _merge_out(u_r, o_fox, z_a, x, g_ffn, w_ro, w_fo, w_out, tm=512, push=None):
    T = x.shape[0]

    def body(u_ref, of_ref, ar_ref, af_ref, x_ref, g_ref, wro_ref, wfo_ref, wout_ref,
             yr_ref, yf_ref, m_ref, x2_ref, h2_ref, oc_ref):
        u = u_ref[...]
        oc = jnp.concatenate([of_ref[h][:, :FOX_D] for h in range(FOX_H)], axis=-1).astype(BF)
        oc_ref[...] = oc
        yr = jnp.concatenate([_nn(u, wro_ref[k]) for k in range(N_CHIP)], axis=-1)
        yf = jnp.concatenate([_nn(oc, wfo_ref[k]) for k in range(N_CHIP)], axis=-1)
        yr_ref[...] = yr.astype(BF)
        yf_ref[...] = yf.astype(BF)
        m = (_sigmoid(ar_ref[...].astype(F32)) * yr + _sigmoid(af_ref[...].astype(F32)) * yf).astype(BF)
        m_ref[...] = m
        x2 = x_ref[...]
        for k in range(N_CHIP):
            x2 = x2 + _nn(m[:, 256 * k:256 * k + 256], wout_ref[k])
        x2_ref[...] = x2
        r = lax.rsqrt(jnp.mean(x2 * x2, axis=-1, keepdims=True) + EPS)
        h2_ref[...] = (x2 * r * g_ref[...]).astype(BF)

    row = lambda w: pl.BlockSpec((tm, w), lambda i: (i, 0))
    const = lambda shp: pl.BlockSpec(shp, lambda i: (0,) * len(shp))
    return _hosted_call(
        body, "merge_out", (T // tm,),
        [row(512), pl.BlockSpec((FOX_H, tm, LANE), lambda i: (0, i, 0)),
         pl.BlockSpec((tm, 1024), lambda i: (i, 3)), pl.BlockSpec((tm, 1024), lambda i: (i, 4)),
         row(1024), const((1, 1024)), const((N_CHIP, 512, 256)), const((N_CHIP, 512, 256)),
         const((N_CHIP, 256, 1024))],
        [row(1024), row(1024), row(1024), row(1024), row(1024), row(512)],
        [jax.ShapeDtypeStruct((T, 1024), BF), jax.ShapeDtypeStruct((T, 1024), BF),
         jax.ShapeDtypeStruct((T, 1024), BF), jax.ShapeDtypeStruct((T, 1024), F32),
         jax.ShapeDtypeStruct((T, 1024), BF), jax.ShapeDtypeStruct((T, 512), BF)],
        [], VMEM_BIG, (u_r, o_fox, z_a, z_a, x, g_ffn, w_ro, w_fo, w_out), push)


def _load_resident(hbm_refs, vmem_refs, sem):
    cps = [pltpu.make_async_copy(h, v, sem.at[i]) for i, (h, v) in enumerate(zip(hbm_refs, vmem_refs))]
    for cp in cps:
        cp.start()
    for cp in cps:
        cp.wait()


def _ffn_fwd(h2, x2, tgt, w_gate, w_up, w_down, tm=FFN_TM):
    T = h2.shape[0]

    def body(h_ref, x2_ref, t_ref, wg_hbm, wu_hbm, wd_hbm, a_ref, b_ref, act_ref, dy_ref, ls_ref, wg, wu, wd, sem):
        @pl.when(pl.program_id(0) == 0)
        def _():
            _load_resident((wg_hbm, wu_hbm, wd_hbm), (wg, wu, wd), sem)
            ls_ref[...] = jnp.zeros(ls_ref.shape, F32)

        h = h_ref[...]
        err = x2_ref[...] - t_ref[...]
        for k in range(N_CHIP):
            gp = _nt(h, wg[k])
            up = _nt(h, wu[k])
            sg = _sigmoid(gp)
            silu = gp * sg
            a_ref[k] = silu.astype(BF)
            b_ref[k] = (up * sg * (1.0 + gp * (1.0 - sg))).astype(BF)
            act = (silu * up).astype(BF)
            act_ref[k] = act
            err = err + _nn(act, wd[k])
        dy_ref[...] = err * (1.0 / D_MODEL)
        ls_ref[...] += jnp.sum(err * err, axis=0, keepdims=True)

    row = pl.BlockSpec((tm, D_MODEL), lambda i: (i, 0))
    hid = pl.BlockSpec((N_CHIP, tm, FF_SH), lambda i: (0, i, 0))
    anyspec = pl.BlockSpec(memory_space=pl.ANY)
    wshape = pltpu.VMEM((N_CHIP, FF_SH, D_MODEL), BF)
    return pl.pallas_call(
        body, name="ffn_fwd", grid=(T // tm,),
        in_specs=[row, row, row, anyspec, anyspec, anyspec],
        out_specs=[hid, hid, hid, row, pl.BlockSpec((1, D_MODEL), lambda i: (0, 0))],
        out_shape=[jax.ShapeDtypeStruct((N_CHIP, T, FF_SH), BF)] * 3
        + [jax.ShapeDtypeStruct((T, D_MODEL), F32), jax.ShapeDtypeStruct((1, D_MODEL), F32)],
        scratch_shapes=[wshape, wshape, wshape, pltpu.SemaphoreType.DMA((3,))],
        compiler_params=_params(("arbitrary",), VMEM_HUGE),
    )(h2, x2, tgt, w_gate, w_up, w_down)


def _ffn_bwd(dy, sa, sb, x2, g_ffn, w_gate, w_up, w_down, tm=FFN_TM):
    T = dy.shape[0]

    def body(dy_ref, a_ref, b_ref, x2_ref, g_ref, wg_hbm, wu_hbm, wd_hbm, dgp_ref, dup_ref, dx_ref, dg_ref,
             wg, wu, wd, sem):
        @pl.when(pl.program_id(0) == 0)
        def _():
            _load_resident((wg_hbm, wu_hbm, wd_hbm), (wg, wu, wd), sem)
            dg_ref[...] = jnp.zeros(dg_ref.shape, F32)

        dy = dy_ref[...]
        dyb = dy.astype(BF)
        dh = jnp.zeros((tm, D_MODEL), F32)
        for k in range(N_CHIP):
            dact = _nt(dyb, wd[k])
            dup = (dact * a_ref[k]).astype(BF)
            dgp = (dact * b_ref[k]).astype(BF)
            dgp_ref[k] = dgp
            dup_ref[k] = dup
            dh = dh + _nn(dgp, wg[k]) + _nn(dup, wu[k])
        x2 = x2_ref[...]
        r = lax.rsqrt(jnp.mean(x2 * x2, axis=-1, keepdims=True) + EPS)
        xn = x2 * r
        dg_ref[...] += jnp.sum(dh * xn, axis=0, keepdims=True)
        dxn = dh * g_ref[...]
        dx_ref[...] = dy + r * (dxn - xn * jnp.mean(dxn * xn, axis=-1, keepdims=True))

    row = pl.BlockSpec((tm, D_MODEL), lambda i: (i, 0))
    hid = pl.BlockSpec((N_CHIP, tm, FF_SH), lambda i: (0, i, 0))
    vec = pl.BlockSpec((1, D_MODEL), lambda i: (0, 0))
    anyspec = pl.BlockSpec(memory_space=pl.ANY)
    wshape = pltpu.VMEM((N_CHIP, FF_SH, D_MODEL), BF)
    return pl.pallas_call(
        body, name="ffn_bwd", grid=(T // tm,),
        in_specs=[row, hid, hid, row, vec, anyspec, anyspec, anyspec],
        out_specs=[hid, hid, row, vec],
        out_shape=[jax.ShapeDtypeStruct((N_CHIP, T, FF_SH), BF), jax.ShapeDtypeStruct((N_CHIP, T, FF_SH), BF),
                   jax.ShapeDtypeStruct((T, D_MODEL), F32), jax.ShapeDtypeStruct((1, D_MODEL), F32)],
        scratch_shapes=[wshape, wshape, wshape, pltpu.SemaphoreType.DMA((3,))],
        compiler_params=_params(("arbitrary",), VMEM_HUGE),
    )(dy, sa, sb, x2, g_ffn, w_gate, w_up, w_down)


def _out_bwd(dx2, z_a, y_r, y_f, o_raw, o_fox, g_ret, w_ro, w_fo, w_out, tm=512, push=None):
    T = dx2.shape[0]

    def body(dx_ref, gt_ref, ar_ref, af_ref, yr_ref, yf_ref, o_ref, of_ref, g_ref, wro_ref, wfo_ref, wout_ref,
             dyr_ref, dyf_ref, dgt_ref, da_ref, do_ref, dof_ref, dg_ref):
        i = pl.program_id(0)

        @pl.when(i == 0)
        def _():
            dg_ref[...] = jnp.zeros(dg_ref.shape, F32)

        dxb = dx_ref[...].astype(BF)
        dm = jnp.concatenate([_nt(dxb, wout_ref[k]) for k in range(N_CHIP)], axis=-1)
        sr, sf = _sigmoid(ar_ref[...].astype(F32)), _sigmoid(af_ref[...].astype(F32))
        dyr = dm * sr
        dyf = dm * sf
        da_ref[:, :1024] = (dyr * yr_ref[...].astype(F32) * (1.0 - sr)).astype(BF)
        da_ref[:, 1024:] = (dyf * yf_ref[...].astype(F32) * (1.0 - sf)).astype(BF)
        dyr = dyr.astype(BF)
        dyf = dyf.astype(BF)
        dyr_ref[...] = dyr
        dyf_ref[...] = dyf
        du = jnp.zeros((tm, 512), F32)
        doc = jnp.zeros((tm, 512), F32)
        for k in range(N_CHIP):
            du = du + _nt(dyr[:, 256 * k:256 * k + 256], wro_ref[k])
            doc = doc + _nt(dyf[:, 256 * k:256 * k + 256], wfo_ref[k])

        for h in range(RET_H):
            cols = slice(h * RET_DV, (h + 1) * RET_DV)
            o = o_ref[:, cols]
            mu = jnp.mean(o, axis=-1, keepdims=True)
            xc = o - mu
            rstd = lax.rsqrt(jnp.mean(xc * xc, axis=-1, keepdims=True) + EPS)
            on = xc * rstd
            g = g_ref[:, cols]
            gt = gt_ref[:, cols].astype(F32)
            sg = _sigmoid(gt)
            duh = du[:, cols]
            dgt_ref[:, cols] = (duh * (on * g) * sg * (1.0 + gt * (1.0 - sg))).astype(BF)
            dog = duh * gt * sg
            dg_ref[:, cols] += jnp.sum(dog * on, axis=0, keepdims=True)
            don = dog * g
            do_ref[:, cols] = rstd * (don - jnp.mean(don, axis=-1, keepdims=True)
                                      - on * jnp.mean(don * on, axis=-1, keepdims=True))

        lane = lax.broadcasted_iota(jnp.int32, (tm, LANE), 1)
        zpad = jnp.zeros((tm, 64), F32)
        for h in range(FOX_H):
            doh = doc[:, 64 * h:64 * h + 64]
            delta = jnp.sum(doh * of_ref[h][:, :FOX_D], axis=-1, keepdims=True)
            hi, mid, lo = [t.astype(F32) for t in _split3(-delta)]
            da = jnp.concatenate([doh, zpad], axis=-1)
            da = jnp.where(lane == 64, hi, jnp.where(lane == 65, mid, jnp.where(lane == 66, lo, da)))
            dof_ref[h] = da.astype(BF)

    row = lambda w: pl.BlockSpec((tm, w), lambda i: (i, 0))
    const = lambda shp: pl.BlockSpec(shp, lambda i: (0,) * len(shp))
    hsp = pl.BlockSpec((FOX_H, tm, LANE), lambda i: (0, i, 0))
    return _hosted_call(
        body, "out_bwd", (T // tm,),
        [row(1024), pl.BlockSpec((tm, 512), lambda i: (i, 2)), pl.BlockSpec((tm, 1024), lambda i: (i, 3)),
         pl.BlockSpec((tm, 1024), lambda i: (i, 4)), row(1024), row(1024), row(512), hsp,
         const((1, 512)), const((N_CHIP, 512, 256)), const((N_CHIP, 512, 256)), const((N_CHIP, 256, 1024))],
        [row(1024), row(1024), row(512), row(2048), row(512), hsp, const((1, 512))],
        [jax.ShapeDtypeStruct((T, 1024), BF), jax.ShapeDtypeStruct((T, 1024), BF),
         jax.ShapeDtypeStruct((T, 512), BF), jax.ShapeDtypeStruct((T, 2048), BF),
         jax.ShapeDtypeStruct((T, 512), F32), jax.ShapeDtypeStruct((FOX_H, T, LANE), BF),
         jax.ShapeDtypeStruct((1, 512), F32)],
        [], VMEM_BIG, (dx2, z_a, z_a, z_a, y_r, y_f, o_raw, o_fox, g_ret, w_ro, w_fo, w_out), push)


def _ret_bwd(d_o, qr, kr, z_a, states, cos_t, sin_t, consts, tt=512, push=None):
    T = z_a.shape[0]
    nt = T // tt
    nch = tt // CHUNK
    decay, zeta, xi, gcb = consts

    def body(do_ref, q_ref, k_ref, v_ref, st_ref, cos_ref, sin_ref, d_ref, ze_ref, xi_ref, gc_ref, dz_ref, g_sc):
        i = pl.program_id(0)

        @pl.when(i == 0)
        def _():
            g_sc[...] = jnp.zeros(g_sc.shape, F32)

        for c in reversed(range(nch)):
            rows = slice(c * CHUNK, (c + 1) * CHUNK)
            cosv, sinv = cos_ref[rows, :], sin_ref[rows, :]
            dq_parts, dk_parts = [], []
            for h in range(RET_H):
                cols = slice(h * RET_DV, (h + 1) * RET_DV)
                q, k = q_ref[h, rows, :], k_ref[h, rows, :]
                v32 = v_ref[rows, cols].astype(F32)
                vb = v32.astype(BF)
                r = st_ref[h, c * CHUNK:c * CHUNK + LANE, :]
                g = g_sc[h]
                gb = g.astype(BF)
                d_o = do_ref[rows, cols]
                dob = d_o.astype(BF)
                dox = (d_o * xi_ref[h]).astype(BF)
                dec = d_ref[h]
                s = (_nt(q, k) * dec).astype(BF)
                ds = (_nt(dob, vb) * dec).astype(BF)
                dv = _tn(s, dob) + ze_ref[h] * _nn(k, gb)
                dq = _nn(ds, k) + _nt(dox, r.astype(BF))
                dk = _tn(ds, q) + _nt((v32 * ze_ref[h]).astype(BF), gb)
                g_sc[h] = gc_ref[h] * g + _tn(q, dox)
                dq_parts.append((dq * cosv - _swap32(dq) * sinv)[:, :64])
                dk_parts.append(((dk * cosv - _swap32(dk) * sinv) * 0.125)[:, :64])
                dz_ref[rows, 512 + h * RET_DV:512 + (h + 1) * RET_DV] = dv.astype(BF)
            dz_ref[rows, 0:256] = jnp.concatenate(dq_parts, axis=-1).astype(BF)
            dz_ref[rows, 256:512] = jnp.concatenate(dk_parts, axis=-1).astype(BF)

    rev = lambda i: nt - 1 - i
    hspec = pl.BlockSpec((RET_H, tt, LANE), lambda i: (0, rev(i), 0))
    cspec = pl.BlockSpec((RET_H, CHUNK, LANE), lambda i: (0, 0, 0))
    tab = pl.BlockSpec((tt, LANE), lambda i: (rev(i), 0))
    (dz,), lands = _hosted_call(
        body, "ret_bwd", (nt,),
        [pl.BlockSpec((tt, 512), lambda i: (rev(i), 0)), hspec, hspec,
         pl.BlockSpec((tt, 512), lambda i: (rev(i), 1)), hspec, tab, tab,
         pl.BlockSpec((RET_H, CHUNK, CHUNK), lambda i: (0, 0, 0)), cspec, cspec,
         pl.BlockSpec((RET_H, LANE, LANE), lambda i: (0, 0, 0))],
        [pl.BlockSpec((tt, 1024), lambda i: (rev(i), 0))], [jax.ShapeDtypeStruct((T, 1024), BF)],
        [pltpu.VMEM((RET_H, LANE, LANE), F32)], VMEM_BIG,
        (d_o, qr, kr, z_a, states, cos_t, sin_t, decay, zeta, xi, gcb), push)
    return dz, lands


def _fox_bwd(end_both, end_last, q2, k, v, do, sub=FOX_SUB):
    H, T, _ = k.shape
    tb = 2 * sub

    def body(eb_ref, el_ref, q_ref, do_ref, k_ref, v_ref, dq_ref, dk_ref, dv_ref, dk_sc, dv_sc):
        j = pl.program_id(1)
        n_both = eb_ref[pl.program_id(0), j]
        n_last = el_ref[pl.program_id(0), j]

        @pl.when(j == 0)
        def _():
            dq_ref[...] = jnp.zeros(dq_ref.shape, F32)

        dk_sc[...] = jnp.zeros(dk_sc.shape, F32)
        dv_sc[...] = jnp.zeros(dv_sc.shape, F32)
        krow = lax.broadcasted_iota(jnp.int32, (tb, sub), 0)
        qcol = lax.broadcasted_iota(jnp.int32, (tb, sub), 1)

        def step(i, r0, r1, shift):
            off = pl.multiple_of(i * sub, sub)
            qq = q_ref[pl.ds(off, sub), :]
            dd = do_ref[pl.ds(off, sub), :]
            kk, vv = k_ref[r0:r1, :], v_ref[r0:r1, :]
            p = jnp.exp(_nt(kk, qq))
            if shift is not None:
                p = jnp.where(qcol[0:r1 - r0, :] + shift >= krow[0:r1 - r0, :], p, 0.0)
            ds = (p * _nt(vv, dd)).astype(BF)
            dv_sc[r0:r1, :] += _nn(p.astype(BF), dd)
            dk_sc[r0:r1, :] += _nn(ds, qq)
            dq_ref[pl.ds(off, sub), :] += _tn(ds, kk)

        step(2 * j, 0, sub, 0)
        step(2 * j + 1, 0, tb, sub)

        def both_body(i, carry):
            step(i, 0, tb, None)
            return carry

        def last_body(i, carry):
            step(i, sub, tb, None)
            return carry

        lax.fori_loop(2 * j + 2, n_both, both_body, 0)
        lax.fori_loop(n_both, n_last, last_body, 0)
        dk_ref[...] = dk_sc[...]
        dv_ref[...] = dv_sc[...]

    blk = pl.BlockSpec((None, tb, LANE), lambda h, j, eb, el: (h, j, 0))
    full = pl.BlockSpec((None, T, LANE), lambda h, j, eb, el: (h, 0, 0))
    shp = jax.ShapeDtypeStruct((H, T, LANE), F32)
    return pl.pallas_call(
        body, name="fox_bwd",
        grid_spec=pltpu.PrefetchScalarGridSpec(
            num_scalar_prefetch=2, grid=(H, T // tb), in_specs=[full, full, blk, blk], out_specs=[full, blk, blk],
            scratch_shapes=[pltpu.VMEM((tb, LANE), F32), pltpu.VMEM((tb, LANE), F32)]),
        out_shape=[shp, shp, shp],
        compiler_params=_params(("arbitrary", "arbitrary"), VMEM_BIG),
    )(end_both, end_last, q2, do, k, v)


def _fox_post_bwd(dq, dk, dv, z_a, z_ff, b_f, g_q, g_k, tm=256, push=None):
    T = z_a.shape[0]
    nt = T // tm

    def body(dq_ref, dk_ref, dv_ref, zf_ref, zff_ref, b_ref, g_ref, sc_ref, seg_ref, segt_ref,
             dz_ref, dff_ref, dg_ref, db_ref, carry):
        i = pl.program_id(0)

        @pl.when(i == 0)
        def _():
            carry[...] = jnp.zeros(carry.shape, F32)
            dg_ref[...] = jnp.zeros(dg_ref.shape, F32)
            db_ref[...] = jnp.zeros(db_ref.shape, F32)

        lane = lax.broadcasted_iota(jnp.int32, (tm, LANE), 1)
        dcm = jnp.zeros((tm, LANE), F32)
        for h in range(FOX_H):
            dcm = jnp.where(lane == h, dq_ref[h][:, L_CQ:L_CQ + 1] - dk_ref[h][:, L_CK:L_CK + 1], dcm)

        def seg_mean(v):
            return sum(_nn(t, seg_ref[...]) for t in _split3(v)) * (1.0 / FOX_D)

        def seg_bcast(v):
            return sum(_nn(t, segt_ref[...]) for t in _split3(v))

        x = zf_ref[:, :1024].astype(F32)
        dy = jnp.concatenate([dq_ref[h][:, :FOX_D] for h in range(FOX_H)]
                             + [dk_ref[h][:, :FOX_D] for h in range(FOX_H)], axis=-1) * sc_ref[...]
        rb = seg_bcast(lax.rsqrt(seg_mean(x * x) + EPS))
        xn = x * rb
        dg_ref[...] += jnp.sum(dy * xn, axis=0, keepdims=True)
        dxn = dy * g_ref[...]
        dz_ref[:, :1024] = (rb * (dxn - xn * seg_bcast(seg_mean(dxn * xn)))).astype(BF)
        dz_ref[:, 1024:] = jnp.concatenate([dv_ref[h][:, :FOX_D] for h in range(FOX_H)], axis=-1).astype(BF)

        row = lax.broadcasted_iota(jnp.int32, (tm, tm), 0)
        col = lax.broadcasted_iota(jnp.int32, (tm, tm), 1)
        tri = (row <= col).astype(BF)
        hi, mid, lo = _split3(dcm)
        dlogf = _nn(tri, hi) + _nn(tri, mid) + _nn(tri, lo) + carry[...]
        carry[...] = dlogf[0:1, :]
        dff = jnp.where(lane < FOX_H, dlogf * _sigmoid(-(zff_ref[...] + b_ref[...])), 0.0)
        dff_ref[...] = dff.astype(BF)
        db_ref[...] += jnp.sum(dff, axis=0, keepdims=True)

    rev = lambda i: nt - 1 - i
    hsp = pl.BlockSpec((FOX_H, tm, LANE), lambda i: (0, rev(i), 0))
    const = lambda r, w: pl.BlockSpec((r, w), lambda i: (0, 0))
    seg = _segment_matrix()
    g_all = jnp.concatenate([jnp.tile(g_q, (1, FOX_H)), jnp.tile(g_k, (1, FOX_H))], axis=1)
    scale = jnp.asarray(np.concatenate([np.full((1, 512), 0.125, np.float32), np.ones((1, 512), np.float32)], axis=1))
    (dz, dff, dg, db), lands = _hosted_call(
        body, "fox_post_bwd", (nt,),
        [hsp, hsp, hsp, pl.BlockSpec((tm, 1536), lambda i: (rev(i), 1)),
         pl.BlockSpec((tm, LANE), lambda i: (rev(i), 0)), const(1, LANE), const(1, 1024), const(1, 1024),
         const(1024, LANE), const(LANE, 1024)],
        [pl.BlockSpec((tm, 1536), lambda i: (rev(i), 0)), pl.BlockSpec((tm, LANE), lambda i: (rev(i), 0)),
         const(1, 1024), const(1, LANE)],
        [jax.ShapeDtypeStruct((T, 1536), BF), jax.ShapeDtypeStruct((T, LANE), BF),
         jax.ShapeDtypeStruct((1, 1024), F32), jax.ShapeDtypeStruct((1, LANE), F32)],
        [pltpu.VMEM((1, LANE), F32)], VMEM_BIG, (dq, dk, dv, z_a, z_ff, b_f, g_all, scale, seg, seg.T), push)
    dg_heads = dg.reshape(2, FOX_H, FOX_D).sum(axis=1)
    return (dz, dff, dg_heads[0:1], dg_heads[1:2], db), lands


def _in_bwd(dz_ret, dz_gt, dz_fox, dz_a, dz_ff, w_a, w_ff, x, g_mix, dx2, tm=512, push=None):
    T = x.shape[0]

    def body(r_ref, t_ref, f_ref, a_ref, ff_ref, wa_ref, wf_ref, x_ref, g_ref, dx2_ref, dx_ref, dg_ref):
        i = pl.program_id(0)

        @pl.when(i == 0)
        def _():
            dg_ref[...] = jnp.zeros(dg_ref.shape, F32)

        dh = (_nt(r_ref[...], wa_ref[:, C_RET:C_GT]) + _nt(t_ref[...], wa_ref[:, C_GT:C_FOX])
              + _nt(f_ref[...], wa_ref[:, C_FOX:C_A]) + _nt(a_ref[...], wa_ref[:, C_A:C_END])
              + _nt(ff_ref[...], wf_ref[...]))
        xv = x_ref[...]
        r = lax.rsqrt(jnp.mean(xv * xv, axis=-1, keepdims=True) + EPS)
        xn = xv * r
        dg_ref[...] += jnp.sum(dh * xn, axis=0, keepdims=True)
        dxn = dh * g_ref[...]
        dx_ref[...] = dx2_ref[...] + r * (dxn - xn * jnp.mean(dxn * xn, axis=-1, keepdims=True))

    row = lambda w: pl.BlockSpec((tm, w), lambda i: (i, 0))
    const = lambda shp: pl.BlockSpec(shp, lambda i: (0,) * len(shp))
    return _hosted_call(
        body, "in_bwd", (T // tm,),
        [row(1024), row(512), row(1536), row(2048), row(LANE), const((D_MODEL, C_END)),
         const((D_MODEL, LANE)), row(1024), const((1, 1024)), row(1024)],
        [row(1024), const((1, 1024))],
        [jax.ShapeDtypeStruct((T, 1024), F32), jax.ShapeDtypeStruct((1, 1024), F32)],
        [], VMEM_BIG, (dz_ret, dz_gt, dz_fox, dz_a, dz_ff, w_a, w_ff, x, g_mix, dx2), push)


def _mesh_pos():
    return lax.axis_index("x"), lax.axis_index("y"), lax.axis_index("c")


def _staged_place(src, name):
    stacked = src.ndim == 3
    R, C = src.shape[-2:]
    tr = _row_tile(R, min(256, R // 2), 16)
    n = R // tr
    assert n >= 2

    def body(s_ref, o_ref, buf, sem):
        i = pl.program_id(0)
        slot = i % 2
        x, y, _ = _mesh_pos()
        kme = 2 * x + y

        def out_copy(s, step):
            return pltpu.make_async_copy(buf.at[s], o_ref.at[kme, pl.ds(pl.multiple_of(step * tr, tr), tr), :], sem.at[s])

        @pl.when(i >= 2)
        def _():
            out_copy(slot, i - 2).wait()

        buf[slot] = (s_ref[kme] if stacked else s_ref[...]).astype(BF)
        out_copy(slot, i).start()

        @pl.when(i == n - 1)
        def _():
            out_copy(1 - slot, i - 1).wait()
            out_copy(slot, i).wait()

    in_spec = (pl.BlockSpec((N_CHIP, tr, C), lambda i: (0, i, 0)) if stacked else pl.BlockSpec((tr, C), lambda i: (i, 0)))
    return pl.pallas_call(
        body, name=name, grid=(n,), in_specs=[in_spec], out_specs=pl.BlockSpec(memory_space=pl.ANY),
        out_shape=jax.ShapeDtypeStruct((N_CHIP, R, C), BF),
        scratch_shapes=[pltpu.VMEM((2, tr, C), BF), pltpu.SemaphoreType.DMA((2,))],
        compiler_params=_params(("arbitrary",)),
    )(src)


def _push_copies(src, land, send_sem, recv_sem, receiving):
    x, y, c = _mesh_pos()
    kme = 2 * x + y
    cps = []
    for w in range(len(land)):
        for j, (px, py) in enumerate([(1 - x, y), (x, 1 - y), (1 - x, 1 - y)]):
            kpeer = 2 * px + py
            cps.append(pltpu.make_async_remote_copy(
                src_ref=land[w].at[kme] if src is None else src[w].at[kpeer],
                dst_ref=land[w].at[kpeer if receiving else kme],
                send_sem=send_sem.at[3 * w + j], recv_sem=recv_sem.at[3 * w + j],
                device_id=(px, py, c), device_id_type=MESH))
    return cps


def _gather_two_level(stack, name):
    _, R, C = stack.shape
    hr = R // 2

    def body(_, land, send_sem, recv_sem):
        x, y, c = _mesh_pos()
        kme = 2 * x + y
        chips = [(1 - x, y), (x, 1 - y), (1 - x, 1 - y)]

        def rows(k, core):
            return land.at[k, pl.ds(pl.multiple_of(core * hr, hr), hr), :]

        def copy(idx, k, core, to):
            return pltpu.make_async_remote_copy(src_ref=rows(k, core), dst_ref=rows(k, core), send_sem=send_sem.at[idx],
                                                recv_sem=recv_sem.at[idx], device_id=to, device_id_type=MESH)

        first = [copy(j, kme, c, (px, py, c)) for j, (px, py) in enumerate(chips)]
        for cp in first:
            cp.start()
        passed = [copy(3 + j, 2 * px + py, c, (x, y, 1 - c)) for j, (px, py) in enumerate(chips)]
        for j, (px, py) in enumerate(chips):
            copy(j, 2 * px + py, c, (px, py, c)).wait_recv()
            passed[j].start()
        for j, (px, py) in enumerate(chips):
            copy(3 + j, 2 * px + py, 1 - c, (x, y, 1 - c)).wait_recv()
        for cp in first + passed:
            cp.wait_send()

    anyspec = pl.BlockSpec(memory_space=pl.ANY)
    return pl.pallas_call(
        body, name=name, in_specs=[anyspec], out_specs=anyspec,
        out_shape=jax.ShapeDtypeStruct(stack.shape, stack.dtype), input_output_aliases={0: 0},
        scratch_shapes=[pltpu.SemaphoreType.DMA((6,)), pltpu.SemaphoreType.DMA((6,))],
    )(stack)


def _gather_small(small):
    def body(sv, svo, ssend, srecv, sloc):
        x, y, c = _mesh_pos()
        me = 4 * x + 2 * y + c
        flips = [(b >> 2 & 1, b >> 1 & 1, b & 1) for b in range(1, 8)]
        others = [(1 - x if fx else x, 1 - y if fy else y, 1 - c if fc else c) for fx, fy, fc in flips]
        local = pltpu.make_async_copy(sv, svo.at[me], sloc)
        local.start()
        sends = []
        for j, (px, py, pc) in enumerate(others):
            cp = pltpu.make_async_remote_copy(
                src_ref=sv, dst_ref=svo.at[me], send_sem=ssend.at[j], recv_sem=srecv.at[j],
                device_id=(px, py, pc), device_id_type=MESH)
            cp.start()
            sends.append(cp)
        for j, (px, py, pc) in enumerate(others):
            pltpu.make_async_remote_copy(
                src_ref=sv, dst_ref=svo.at[4 * px + 2 * py + pc], send_sem=ssend.at[j], recv_sem=srecv.at[j],
                device_id=(px, py, pc), device_id_type=MESH).wait_recv()
        for cp in sends:
            cp.wait_send()
        local.wait()

    anyspec = pl.BlockSpec(memory_space=pl.ANY)
    return pl.pallas_call(
        body, name="gather_small", in_specs=[anyspec], out_specs=anyspec,
        out_shape=jax.ShapeDtypeStruct((8,) + small.shape, small.dtype),
        scratch_shapes=[pltpu.SemaphoreType.DMA((7,)), pltpu.SemaphoreType.DMA((7,)), pltpu.SemaphoreType.DMA],
    )(small)


def _sibling_exchange(arrs):
    n = len(arrs)

    def body(*refs):
        ins, outs = refs[:n], refs[n:2 * n]
        send_sems, recv_sems = refs[2 * n:]
        x, y, c = _mesh_pos()
        cps = [pltpu.make_async_remote_copy(
            src_ref=ins[w], dst_ref=outs[w], send_sem=send_sems.at[w], recv_sem=recv_sems.at[w],
            device_id=(x, y, 1 - c), device_id_type=MESH) for w in range(n)]
        for cp in cps:
            cp.start()
        for cp in cps:
            cp.wait_recv()
        for cp in cps:
            cp.wait_send()

    anyspec = pl.BlockSpec(memory_space=pl.ANY)
    return pl.pallas_call(
        body, name="sibling_exchange",
        in_specs=[anyspec] * n, out_specs=[anyspec] * n,
        out_shape=[jax.ShapeDtypeStruct(a.shape, a.dtype) for a in arrs],
        scratch_shapes=[pltpu.SemaphoreType.DMA((n,)), pltpu.SemaphoreType.DMA((n,))],
    )(*arrs)


def _sum_stack(own, recv, name):
    _, R, C = recv.shape
    tr = _row_tile(R, 256, 16)

    def body(g_ref, r_ref, o_ref):
        x, y, _ = _mesh_pos()
        kme = 2 * x + y
        acc = g_ref[kme].astype(F32)
        for d in range(1, N_CHIP):
            acc = acc + r_ref[(kme + d) % N_CHIP].astype(F32)
        o_ref[...] = acc

    spec = pl.BlockSpec((N_CHIP, tr, C), lambda i: (0, i, 0))
    return pl.pallas_call(
        body, name=name, grid=(R // tr,), in_specs=[spec, spec],
        out_specs=pl.BlockSpec((tr, C), lambda i: (i, 0)),
        out_shape=jax.ShapeDtypeStruct((R, C), F32),
        compiler_params=_params(("parallel",)),
    )(own, recv)


def _adam_math(w, g, m, v):
    m2 = ADAM_B1 * m + (1.0 - ADAM_B1) * g
    v2 = ADAM_B2 * v + (1.0 - ADAM_B2) * (g * g)
    m_hat = m2 / (1.0 - ADAM_B1 ** ADAM_STEP)
    v_hat = v2 / (1.0 - ADAM_B2 ** ADAM_STEP)
    delta = -ADAM_LR * (m_hat / (jnp.sqrt(v_hat) + ADAM_EPS) + ADAM_WD * w)
    return delta, m2, v2


def _adamw(w, m, v, s0, s1, name):
    R, C = w.shape
    tr = _row_tile(R, 256, 8)

    def body(w_ref, m_ref, v_ref, a_ref, b_ref, g_ref, d_ref, m2_ref, v2_ref):
        g = a_ref[...] + b_ref[...]
        delta, m2, v2 = _adam_math(w_ref[...], g, m_ref[...], v_ref[...])
        g_ref[...] = g
        d_ref[...] = delta
        m2_ref[...] = m2
        v2_ref[...] = v2

    spec = pl.BlockSpec((tr, C), lambda i: (i, 0))
    shp = jax.ShapeDtypeStruct((R, C), F32)
    return pl.pallas_call(
        body, name=name, grid=(R // tr,), in_specs=[spec] * 5, out_specs=[spec] * 4, out_shape=[shp] * 4,
        compiler_params=_params(("parallel",), VMEM_BIG),
    )(w, m, v, s0, s1)


def _adamw_small(ws, ms, vs, gathered):
    n = len(SMALL)

    def body(*refs):
        w_refs, m_refs, v_refs, s_ref = refs[:n], refs[n:2 * n], refs[2 * n:3 * n], refs[3 * n]
        outs = refs[3 * n + 1:]
        g_all = s_ref[0]
        for d in range(1, 8):
            g_all = g_all + s_ref[d]
        off = 0
        for i, (_, width) in enumerate(SMALL):
            g = g_all[:, off:off + width]
            delta, m2, v2 = _adam_math(w_refs[i][...], g, m_refs[i][...], v_refs[i][...])
            for kind, val in enumerate((g, delta, m2, v2)):
                outs[kind * n + i][...] = val
            off += width + (-width % LANE)

    shapes = [jax.ShapeDtypeStruct((1, width), F32) for _, width in SMALL]
    res = pl.pallas_call(body, name="adamw_small", out_shape=shapes * 4)(*ws, *ms, *vs, gathered)
    return [dict(zip([nm for nm, _ in SMALL], res[kind * n:(kind + 1) * n])) for kind in range(4)]


SMALL = (("g_mix", 1024), ("g_ffn", 1024), ("g_ret_norm", 512), ("g_fox_q", 64), ("g_fox_k", 64), ("b_forget", 8))
SMALL_W = 3072


def _pack_small(parts):
    cols = []
    for (name, n) in SMALL:
        p = parts[name].reshape(1, -1)[:, :n]
        pad = -n % LANE
        cols.append(jnp.pad(p, ((0, 0), (0, pad))) if pad else p)
    used = sum(c.shape[1] for c in cols)
    cols.append(jnp.zeros((1, SMALL_W - used), F32))
    return jnp.concatenate(cols, axis=1)


def kernel(x, g_mix, w_in, b_forget, g_ret_norm, w_ret_o, g_fox_q, g_fox_k, w_fox_o, w_out, g_ffn, w_gate, w_up, w_down, loss_target, m_g_mix, m_w_in, m_b_forget, m_g_ret_norm, m_w_ret_o, m_g_fox_q, m_g_fox_k, m_w_fox_o, m_w_out, m_g_ffn, m_w_gate, m_w_up, m_w_down, v_g_mix, v_w_in, v_b_forget, v_g_ret_norm, v_w_ret_o, v_g_fox_q, v_g_fox_k, v_w_fox_o, v_w_out, v_g_ffn, v_w_gate, v_w_up, v_w_down):
    T = x.shape[1]
    xs = x[0]
    tgt = loss_target[0]
    big_names = ("w_in", "w_ret_o", "w_fox_o", "w_out", "w_gate", "w_up", "w_down")
    tr = lambda a: jnp.swapaxes(a[0], 0, 1)
    big_w = dict(w_in=w_in[0], w_ret_o=w_ret_o[0], w_fox_o=w_fox_o[0], w_out=w_out[0], w_gate=tr(w_gate),
                 w_up=tr(w_up), w_down=w_down[0])
    big_m = dict(w_in=m_w_in[0], w_ret_o=m_w_ret_o[0], w_fox_o=m_w_fox_o[0], w_out=m_w_out[0], w_gate=tr(m_w_gate),
                 w_up=tr(m_w_up), w_down=m_w_down[0])
    big_v = dict(w_in=v_w_in[0], w_ret_o=v_w_ret_o[0], w_fox_o=v_w_fox_o[0], w_out=v_w_out[0], w_gate=tr(v_w_gate),
                 w_up=tr(v_w_up), w_down=v_w_down[0])
    small_w = dict(g_mix=g_mix, g_ffn=g_ffn, g_ret_norm=g_ret_norm, g_fox_q=g_fox_q, g_fox_k=g_fox_k, b_forget=b_forget)
    small_m = dict(g_mix=m_g_mix, g_ffn=m_g_ffn, g_ret_norm=m_g_ret_norm, g_fox_q=m_g_fox_q, g_fox_k=m_g_fox_k,
                   b_forget=m_b_forget)
    small_v = dict(g_mix=v_g_mix, g_ffn=v_g_ffn, g_ret_norm=v_g_ret_norm, g_fox_q=v_g_fox_q, g_fox_k=v_g_fox_k,
                   b_forget=v_b_forget)

    stacks = {n: _staged_place(big_w[n], "place_" + n) for n in big_names}
    s_in = _gather_two_level(stacks["w_in"], "gather_w_in")
    w_a, w_ff = _assemble_w_in(s_in)
    b_pad = jnp.pad(b_forget, ((0, 0), (0, LANE - FOX_H)))
    cos_t, sin_t = _rope_tables(T)
    consts = _ret_consts()

    h = _rms_cast(xs, g_mix)
    z_a, (s_ro, s_fo, s_gate) = _mm_nn(
        h, w_a, "proj_in", BF, tm=1024, push=(None, [stacks["w_ret_o"], stacks["w_fox_o"], stacks["w_gate"]]))
    (qr, kr, qf, kf, vf, c_cum, nmax, z_ff), (s_out, s_up) = _mix_prep(
        z_a, h, w_ff, cos_t, sin_t, b_pad, g_fox_q, g_fox_k, push=(None, [stacks["w_out"], stacks["w_up"]]))
    jlo, end_both, end_last, tame = _prune_tables(c_cum, nmax, FOX_SUB)
    o_raw, u_r, states = _ret_fwd(qr, kr, z_a, g_ret_norm, consts)
    o_fox, q2 = _fox_fwd(jlo, tame, qf, kf, vf)
    (y_r, y_f, mrg, x2, h2, o_cat), (s_down,) = _merge_out(u_r, o_fox, z_a, xs, g_ffn, s_ro, s_fo, s_out,
                                                            push=(None, [stacks["w_down"]]))
    sa, sb, act, dy, loss_vec = _ffn_fwd(h2, x2, tgt, s_gate, s_up, s_down)
    loss = lax.psum(0.5 / D_MODEL * jnp.sum(loss_vec), ("x", "y", "c"))

    def scatter_job(grads):
        return (grads, [lax.empty(g.shape, g.dtype) for g in grads])

    dgp, dup, dx2, dg_ffn = _ffn_bwd(dy, sa, sb, x2, g_ffn, s_gate, s_up, s_down)
    (g_gate, _), (g_up, _), (g_down, _) = (_grad_astack(dgp, h2, "gw_gate"), _grad_astack(dup, h2, "gw_up"),
                                           _grad_astack(act, dy, "gw_down"))
    (d_yr, d_yf, dz_gt, dz_a, d_o, do_fox, dg_ret), (r_gate, r_up) = _out_bwd(
        dx2, z_a, y_r, y_f, o_raw, o_fox, g_ret_norm, s_ro, s_fo, s_out, push=scatter_job([g_gate, g_up]))
    dz_ret, (r_down,) = _ret_bwd(d_o, qr, kr, z_a, states, cos_t, sin_t, consts, push=scatter_job([g_down]))
    dq_f, dk_f, dv_f = _fox_bwd(end_both, end_last, q2, kf, vf, do_fox)
    g_mid = [_grad_colstack(u_r, d_yr, "gw_ret_o", 256), _grad_colstack(o_cat, d_yf, "gw_fox_o", 256),
             _grad_plain(mrg, dx2, "gw_out", BF).reshape(N_CHIP, 256, D_MODEL)]
    (dz_fox, dz_ff, dg_q, dg_k, db_f), (r_ro, r_fo, r_out) = _fox_post_bwd(
        dq_f, dk_f, dv_f, z_a, z_ff, b_pad, g_fox_q, g_fox_k, push=scatter_job(g_mid))
    gi_ret, gi_gt, gi_ff = _grad_multi(h, [dz_ret, dz_gt, dz_ff], "gw_in_small")
    gi_fox, gi_a = _grad_multi(h, [dz_fox, dz_a], "gw_in_large")
    g_in = _pack_g_in(gi_ret, gi_gt, gi_fox, gi_a, gi_ff)
    (grad_x, dg_mix), (r_in,) = _in_bwd(dz_ret, dz_gt, dz_fox, dz_a, dz_ff, w_a, w_ff, xs, g_mix, dx2,
                                        push=scatter_job([g_in]))
    small_g = _pack_small(dict(g_mix=dg_mix, g_ffn=dg_ffn, g_ret_norm=dg_ret, g_fox_q=dg_q, g_fox_k=dg_k, b_forget=db_f))

    small_all = _gather_small(small_g)
    sums = [_sum_stack(g, r, "sum_" + n) for g, r, n in zip(
        [g_in] + g_mid + [g_gate, g_up, g_down], [r_in, r_ro, r_fo, r_out, r_gate, r_up, r_down], big_names)]
    sib = _sibling_exchange(sums)
    big_out = {n: _adamw(big_w[n], big_m[n], big_v[n], sums[i], sib[i], "adamw_" + n) for i, n in enumerate(big_names)}
    small_out = _adamw_small(*[[d[nm] for nm, _ in SMALL] for d in (small_w, small_m, small_v)], small_all)

    order = ("g_mix", "w_in", "b_forget", "g_ret_norm", "w_ret_o", "g_fox_q", "g_fox_k", "w_fox_o", "w_out", "g_ffn",
             "w_gate", "w_up", "w_down")
    outs = [loss, grad_x[None]]
    for idx in range(4):
        for n in order:
            if n in ("w_gate", "w_up"):
                outs.append(jnp.swapaxes(big_out[n][idx], 0, 1)[None])
            else:
                outs.append(big_out[n][idx][None] if n in big_out else small_out[idx][n])
    return tuple(outs)
```

```python
import functools

import numpy as np
import jax
import jax.numpy as jnp
from jax import lax
from jax.experimental import pallas as pl
from jax.experimental.pallas import tpu as pltpu

F32 = jnp.float32
BF = jnp.bfloat16
MESH = pl.DeviceIdType.MESH

D_MODEL = 1024
D_FF = 2816
N_CHIP = 4
FF_SH = D_FF // N_CHIP
IN_COLS = 5128
IN_SH = IN_COLS // N_CHIP
RET_H, RET_DV = 4, 128
FOX_H, FOX_D = 8, 64
CHUNK = 256
EPS = 1e-6
NEG = -1e30
LANE = 128
C_RET, C_GT, C_FOX, C_A, C_END = 0, 1024, 1536, 3072, 5120
L_CQ, L_CK, L_LSE, L_MAX = 64, 67, 70, 73

ADAM_LR, ADAM_B1, ADAM_B2, ADAM_EPS, ADAM_WD, ADAM_STEP = 0.001, 0.9, 0.999, 1e-08, 0.01, 10
VMEM_BIG = 56 * 1024 * 1024
VMEM_HUGE = 60 * 1024 * 1024
GRAD_TK = 2048
FFN_TM = 512
FOX_SUB = 512


def _nn(a, b):
    return lax.dot_general(a, b, (((1,), (0,)), ((), ())), preferred_element_type=F32)


def _nt(a, b):
    return lax.dot_general(a, b, (((1,), (1,)), ((), ())), preferred_element_type=F32)


def _tn(a, b):
    return lax.dot_general(a, b, (((0,), (0,)), ((), ())), preferred_element_type=F32)


def _split3(x):
    hi = x.astype(BF)
    r = x - hi.astype(F32)
    mid = r.astype(BF)
    lo = (r - mid.astype(F32)).astype(BF)
    return hi, mid, lo


def _sigmoid(x):
    return 0.5 * jnp.tanh(0.5 * x) + 0.5


def _swap32(x):
    lane = lax.broadcasted_iota(jnp.int32, x.shape, 1)
    return jnp.where(lane < 32, pltpu.roll(x, 96, 1), pltpu.roll(x, 32, 1))


def _params(sem, vmem=None):
    return pltpu.CompilerParams(dimension_semantics=sem, vmem_limit_bytes=vmem)


def _row_tile(rows, cap, mult):
    return max(d for d in range(mult, cap + 1, mult) if rows % d == 0)


def _assemble_w_in(stack, tr=256):
    def body(s_ref, a_ref, f_ref):
        full = jnp.concatenate([s_ref[k].astype(F32) for k in range(N_CHIP)], axis=-1)
        a_ref[...] = jnp.concatenate([full[:, :3072], full[:, 3080:IN_COLS]], axis=-1).astype(BF)
        f_ref[...] = jnp.concatenate([full[:, 3072:3080], jnp.zeros((tr, LANE - FOX_H), F32)], axis=-1).astype(BF)

    return pl.pallas_call(
        body, name="assemble_w_in", grid=(D_MODEL // tr,),
        in_specs=[pl.BlockSpec((N_CHIP, tr, IN_SH), lambda i: (0, i, 0))],
        out_specs=[pl.BlockSpec((tr, C_END), lambda i: (i, 0)), pl.BlockSpec((tr, LANE), lambda i: (i, 0))],
        out_shape=[jax.ShapeDtypeStruct((D_MODEL, C_END), BF), jax.ShapeDtypeStruct((D_MODEL, LANE), BF)],
        compiler_params=_params(("parallel",), VMEM_BIG),
    )(stack)


def _pack_g_in(g_ret, g_gt, g_fox, g_a, g_ff, tr=256):
    def body(r_ref, t_ref, x_ref, a_ref, f_ref, o_ref):
        r, t, x, a, f = [ref[...].astype(F32) for ref in (r_ref, t_ref, x_ref, a_ref, f_ref)]
        full = jnp.concatenate([r, t, x, f[:, :FOX_H], a], axis=-1)
        for k in range(N_CHIP):
            o_ref[k] = full[:, k * IN_SH:(k + 1) * IN_SH].astype(BF)

    def spec(w):
        return pl.BlockSpec((tr, w), lambda i: (i, 0))

    return pl.pallas_call(
        body, name="pack_g_in", grid=(D_MODEL // tr,),
        in_specs=[spec(1024), spec(512), spec(1536), spec(2048), spec(LANE)],
        out_specs=pl.BlockSpec((N_CHIP, tr, IN_SH), lambda i: (0, i, 0)),
        out_shape=jax.ShapeDtypeStruct((N_CHIP, D_MODEL, IN_SH), BF),
        compiler_params=_params(("parallel",), VMEM_BIG),
    )(g_ret, g_gt, g_fox, g_a, g_ff)


def _rms_cast(x, g, tm=512):
    T = x.shape[0]

    def body(x_ref, g_ref, o_ref):
        xv = x_ref[...]
        r = lax.rsqrt(jnp.mean(xv * xv, axis=-1, keepdims=True) + EPS)
        o_ref[...] = (xv * r * g_ref[...]).astype(BF)

    return pl.pallas_call(
        body, name="rms_cast", grid=(T // tm,),
        in_specs=[pl.BlockSpec((tm, D_MODEL), lambda i: (i, 0)), pl.BlockSpec((1, D_MODEL), lambda i: (0, 0))],
        out_specs=pl.BlockSpec((tm, D_MODEL), lambda i: (i, 0)),
        out_shape=jax.ShapeDtypeStruct((T, D_MODEL), BF),
        compiler_params=_params(("parallel",)),
    )(x, g)


def _hosted_call(body, name, grid, in_specs, out_specs, out_shape, scratch_shapes, vmem, args, push, sib=None):
    sem = ("arbitrary",) * len(grid)
    if push is None:
        res = pl.pallas_call(body, name=name, grid=grid, in_specs=in_specs, out_specs=out_specs, out_shape=out_shape,
                             scratch_shapes=scratch_shapes, compiler_params=_params(sem, vmem))(*args)
        return list(res), []
    srcs, lands = push
    ns, nl, n_in, n_out = (0 if srcs is None else len(srcs)), len(lands), len(in_specs), len(out_specs)
    n_scr = len(scratch_shapes)
    nb = 0 if sib is None else len(sib)

    def wrapped(*refs):
        pos = n_in + ns + nl
        ins, x_in, b_in = refs[:n_in], refs[n_in:pos], refs[pos:pos + nb]
        pos += nb
        outs, x_out = refs[pos:pos + n_out], refs[pos + n_out:pos + n_out + nl]
        b_out = refs[pos + n_out + nl:pos + n_out + nl + nb]
        pos += n_out + nl + nb
        scr = refs[pos:pos + n_scr]
        ssem, rsem = refs[pos + n_scr], refs[pos + n_scr + 1]
        src = None if srcs is None else x_in[:ns]
        ids = [pl.program_id(a) for a in range(len(grid))]
        first = functools.reduce(lambda p, q: p & q, [ids[a] == 0 for a in range(len(grid))])
        last = functools.reduce(lambda p, q: p & q, [ids[a] == grid[a] - 1 for a in range(len(grid))])

        def sib_copies():
            x, y, c = _mesh_pos()
            bs, br = refs[pos + n_scr + 2], refs[pos + n_scr + 3]
            return [pltpu.make_async_remote_copy(src_ref=b_in[w], dst_ref=b_out[w], send_sem=bs.at[w], recv_sem=br.at[w],
                                                 device_id=(x, y, 1 - c), device_id_type=MESH) for w in range(nb)]

        @pl.when(first)
        def _():
            for cp in _push_copies(src, x_out, ssem, rsem, False):
                cp.start()
            if nb:
                for cp in sib_copies():
                    cp.start()

        body(*ins, *outs, *scr)

        @pl.when(last)
        def _():
            for cp in _push_copies(src, x_out, ssem, rsem, True):
                cp.wait_recv()
                cp.wait_send()
            if nb:
                for cp in sib_copies():
                    cp.wait_recv()
                    cp.wait_send()

    anyspec = pl.BlockSpec(memory_space=pl.ANY)
    extra = ([] if srcs is None else list(srcs)) + list(lands) + ([] if sib is None else list(sib))
    sib_sems = [pltpu.SemaphoreType.DMA((nb,)), pltpu.SemaphoreType.DMA((nb,))] if nb else []
    res = pl.pallas_call(
        wrapped, name=name, grid=grid,
        in_specs=list(in_specs) + [anyspec] * len(extra), out_specs=list(out_specs) + [anyspec] * (nl + nb),
        out_shape=list(out_shape) + [jax.ShapeDtypeStruct(a.shape, a.dtype) for a in lands]
        + [jax.ShapeDtypeStruct(a.shape, a.dtype) for a in (sib or [])],
        input_output_aliases={n_in + ns + i: n_out + i for i in range(nl)},
        scratch_shapes=list(scratch_shapes) + [pltpu.SemaphoreType.DMA((3 * nl,)), pltpu.SemaphoreType.DMA((3 * nl,))]
        + sib_sems,
        compiler_params=_params(sem, vmem),
    )(*args, *extra)
    if nb:
        return list(res[:n_out]), list(res[n_out:n_out + nl]), list(res[n_out + nl:])
    return list(res[:n_out]), list(res[n_out:])


def _mm_nn(a, b, name, out_dtype, tm=512, tn=1024, push=None):
    M, K = a.shape
    N = b.shape[1]
    tn = min(tn, N)

    def body(a_ref, b_ref, o_ref):
        o_ref[...] = _nn(a_ref[...], b_ref[...]).astype(o_ref.dtype)

    (out,), lands = _hosted_call(
        body, name, (N // tn, M // tm),
        [pl.BlockSpec((tm, K), lambda j, i: (i, 0)), pl.BlockSpec((K, tn), lambda j, i: (0, j))],
        [pl.BlockSpec((tm, tn), lambda j, i: (i, j))], [jax.ShapeDtypeStruct((M, N), out_dtype)], [], None, (a, b), push)
    return out, lands


def _mm_tn(a, b, name, grid, a_spec, b_spec, o_spec, out_shape, acc_shape):
    nk = grid[-1]

    def body(a_ref, b_ref, o_ref, acc):
        k = pl.program_id(len(grid) - 1)

        @pl.when(k == 0)
        def _():
            acc[...] = jnp.zeros(acc.shape, F32)

        acc[...] += _tn(a_ref[...].astype(BF), b_ref[...].astype(BF))

        @pl.when(k == nk - 1)
        def _():
            o_ref[...] = acc[...].astype(o_ref.dtype)

    return pl.pallas_call(
        body, name=name, grid=grid, in_specs=[a_spec, b_spec], out_specs=o_spec, out_shape=out_shape,
        scratch_shapes=[pltpu.VMEM(acc_shape, F32)],
        compiler_params=_params(("parallel",) * (len(grid) - 1) + ("arbitrary",), VMEM_BIG),
    )(a, b)


def _grad_plain(a, b, name, out_dtype, tk=GRAD_TK, tn=1024):
    T, M = a.shape
    N = b.shape[1]
    tn = min(tn, N)
    return _mm_tn(a, b, name, (N // tn, T // tk),
                  pl.BlockSpec((tk, M), lambda j, k: (k, 0)), pl.BlockSpec((tk, tn), lambda j, k: (k, j)),
                  pl.BlockSpec((M, tn), lambda j, k: (0, j)), jax.ShapeDtypeStruct((M, N), out_dtype), (M, tn))


def _grad_multi(a, bs, name, tk=1024):
    T, M = a.shape
    n = len(bs)
    nk = T // tk

    def body(*refs):
        a_ref, b_refs, o_refs, accs = refs[0], refs[1:1 + n], refs[1 + n:1 + 2 * n], refs[1 + 2 * n:]
        k = pl.program_id(0)

        @pl.when(k == 0)
        def _():
            for acc in accs:
                acc[...] = jnp.zeros(acc.shape, F32)

        av = a_ref[...]
        for i in range(n):
            accs[i][...] += _tn(av, b_refs[i][...])

        @pl.when(k == nk - 1)
        def _():
            for i in range(n):
                o_refs[i][...] = accs[i][...].astype(BF)

    widths = [b.shape[1] for b in bs]
    return pl.pallas_call(
        body, name=name, grid=(nk,),
        in_specs=[pl.BlockSpec((tk, M), lambda k: (k, 0))] + [pl.BlockSpec((tk, w), lambda k: (k, 0)) for w in widths],
        out_specs=[pl.BlockSpec((M, w), lambda k: (0, 0)) for w in widths],
        out_shape=[jax.ShapeDtypeStruct((M, w), BF) for w in widths],
        scratch_shapes=[pltpu.VMEM((M, w), F32) for w in widths],
        compiler_params=_params(("arbitrary",), VMEM_BIG),
    )(a, *bs)


def _grad_colstack(a, b, name, wcol, tk=GRAD_TK):
    T, M = a.shape
    N = b.shape[1]
    S = N // wcol
    nk = T // tk

    def body(a_ref, b_ref, o_ref, acc):
        k = pl.program_id(0)

        @pl.when(k == 0)
        def _():
            acc[...] = jnp.zeros(acc.shape, F32)

        acc[...] += _tn(a_ref[...], b_ref[...])

        @pl.when(k == nk - 1)
        def _():
            for s in range(S):
                o_ref[s] = acc[:, s * wcol:(s + 1) * wcol].astype(BF)

    return pl.pallas_call(
        body, name=name, grid=(nk,),
        in_specs=[pl.BlockSpec((tk, M), lambda k: (k, 0)), pl.BlockSpec((tk, N), lambda k: (k, 0))],
        out_specs=pl.BlockSpec((S, M, wcol), lambda k: (0, 0, 0)), out_shape=jax.ShapeDtypeStruct((S, M, wcol), BF),
        scratch_shapes=[pltpu.VMEM((M, N), F32)], compiler_params=_params(("arbitrary",), VMEM_BIG),
    )(a, b)


def _grad_astack(a, b, name, tk=1024, push=None):
    S, T, m = a.shape
    N = b.shape[1]
    nk = T // tk

    def body(a_ref, b_ref, o_ref, acc):
        k = pl.program_id(0)

        @pl.when(k == 0)
        def _():
            acc[...] = jnp.zeros(acc.shape, F32)

        bb = b_ref[...].astype(BF)
        for s in range(S):
            acc[s] += _tn(a_ref[s], bb)

        @pl.when(k == nk - 1)
        def _():
            o_ref[...] = acc[...].astype(BF)

    (out,), lands = _hosted_call(
        body, name, (nk,),
        [pl.BlockSpec((S, tk, m), lambda k: (0, k, 0)), pl.BlockSpec((tk, N), lambda k: (k, 0))],
        [pl.BlockSpec((S, m, N), lambda k: (0, 0, 0))], [jax.ShapeDtypeStruct((S, m, N), BF)],
        [pltpu.VMEM((S, m, N), F32)], VMEM_BIG, (a, b), push)
    return out, lands


def _rope_tables(T):
    half = 32
    pos = np.arange(T, dtype=np.float32)
    inv_freq = (np.float32(1.0) / (np.float32(10000.0) ** (np.arange(half, dtype=np.float32) / np.float32(half)))).astype(np.float32)
    ang = (pos[:, None] * inv_freq[None, :]).astype(np.float32)
    cos, sin = np.cos(ang).astype(np.float32), np.sin(ang).astype(np.float32)
    z = np.zeros((T, 64), np.float32)
    return (jnp.asarray(np.concatenate([cos, cos, z], axis=-1)), jnp.asarray(np.concatenate([-sin, sin, z], axis=-1)))


def _ret_consts():
    h = np.arange(RET_H, dtype=np.float32)
    log_g = np.log1p(-(np.float32(2.0) ** (-5.0 - h))).astype(np.float32)
    idx = np.arange(CHUNK, dtype=np.float32)
    diff = idx[:, None] - idx[None, :]
    decay = np.where(diff[None] >= 0, np.exp(np.maximum(diff, 0.0)[None] * log_g[:, None, None]), 0.0)
    zeta = np.exp((CHUNK - 1.0 - idx)[None, :] * log_g[:, None])
    xi = np.exp((idx + 1.0)[None, :] * log_g[:, None])
    gc = np.exp(CHUNK * log_g)
    bc = lambda v: np.broadcast_to(v[:, :, None], (RET_H, CHUNK, LANE)).astype(np.float32)
    gcb = np.broadcast_to(gc[:, None, None], (RET_H, LANE, LANE)).astype(np.float32)
    return (jnp.asarray(decay.astype(np.float32)), jnp.asarray(bc(zeta)), jnp.asarray(bc(xi)), jnp.asarray(gcb))


def _mix_prep(z_a, h, w_ff, cos_t, sin_t, b_f, g_q, g_k, tm=256, push=None):
    T = z_a.shape[0]

    def body(zqk_ref, zf_ref, h_ref, wff_ref, cos_ref, sin_ref, b_ref, g_ref, seg_ref, segt_ref,
             qr_ref, kr_ref, qf_ref, kf_ref, vf_ref, c_ref, nmax_ref, zff_ref, carry):
        i = pl.program_id(0)
        zff = _nn(h_ref[...], wff_ref[...])
        zff_ref[...] = zff

        @pl.when(i == 0)
        def _():
            carry[...] = jnp.zeros(carry.shape, F32)
            nmax_ref[...] = jnp.zeros(nmax_ref.shape, F32)

        lane = lax.broadcasted_iota(jnp.int32, (tm, LANE), 1)
        zpad = jnp.zeros((tm, 64), F32)
        cosv, sinv = cos_ref[...], sin_ref[...]
        zqk = zqk_ref[...].astype(F32)
        for h in range(RET_H):
            for src, dst, scale in ((0, qr_ref, 1.0), (256, kr_ref, 0.125)):
                xh = jnp.concatenate([zqk[:, src + 64 * h: src + 64 * h + 64], zpad], axis=-1)
                rot = xh * cosv + _swap32(xh) * sinv
                dst[h] = (rot * scale).astype(BF)

        lf_in = zff + b_ref[...]
        logf = jnp.minimum(lf_in, 0.0) - jnp.log(1.0 + jnp.exp(-jnp.abs(lf_in)))
        row = lax.broadcasted_iota(jnp.int32, (tm, tm), 0)
        col = lax.broadcasted_iota(jnp.int32, (tm, tm), 1)
        tri = (row >= col).astype(BF)
        hi, mid, lo = _split3(logf)
        cs = _nn(tri, hi) + _nn(tri, mid) + _nn(tri, lo) + carry[...]
        carry[...] = cs[tm - 1:tm, :]
        c_ref[...] = cs

        def seg_sum(v):
            return sum(_nn(t, seg_ref[...]) for t in _split3(v))

        zf = zf_ref[...].astype(F32)
        xqk = zf[:, :1024]
        rinv = lax.rsqrt(seg_sum(xqk * xqk) * (1.0 / FOX_D) + EPS)
        xn = xqk * sum(_nn(t, segt_ref[...]) for t in _split3(rinv)) * g_ref[...]
        nmax_ref[...] = jnp.maximum(nmax_ref[...], jnp.max(seg_sum(xn * xn), axis=0, keepdims=True))

        one = jnp.ones((tm, LANE), F32)
        for h in range(FOX_H):
            c = cs[:, h:h + 1]
            chi, cmid, clo = [t.astype(F32) for t in _split3(c)]
            qn = xn[:, 64 * h:64 * h + 64]
            kn = xn[:, 512 + 64 * h:512 + 64 * h + 64]
            vh = zf[:, 1024 + 64 * h:1024 + 64 * h + 64]
            qa = jnp.concatenate([qn, zpad], axis=-1)
            qa = jnp.where(lane == L_CQ, chi, jnp.where(lane == L_CQ + 1, cmid, jnp.where(lane == L_CQ + 2, clo, qa)))
            qa = jnp.where((lane >= L_CK) & (lane < L_CK + 3), one, qa)
            ka = jnp.concatenate([kn, zpad], axis=-1)
            ka = jnp.where(lane == L_CK, -chi, jnp.where(lane == L_CK + 1, -cmid, jnp.where(lane == L_CK + 2, -clo, ka)))
            ka = jnp.where(((lane >= L_CQ) & (lane < L_CQ + 3)) | ((lane >= L_LSE) & (lane < L_MAX + 3)), one, ka)
            va = jnp.concatenate([vh, zpad], axis=-1)
            va = jnp.where((lane >= 64) & (lane < 67), one, va)
            qf_ref[h] = qa.astype(BF)
            kf_ref[h] = ka.astype(BF)
            vf_ref[h] = va.astype(BF)

    hspec4 = pl.BlockSpec((RET_H, tm, LANE), lambda i: (0, i, 0))
    hspec8 = pl.BlockSpec((FOX_H, tm, LANE), lambda i: (0, i, 0))
    const = lambda r, w: pl.BlockSpec((r, w), lambda i: (0, 0))
    seg = _segment_matrix()
    g_all = jnp.concatenate([jnp.tile(g_q * 0.125, (1, FOX_H)), jnp.tile(g_k, (1, FOX_H))], axis=1)
    return _hosted_call(
        body, "mix_prep", (T // tm,),
        [pl.BlockSpec((tm, 512), lambda i: (i, 0)), pl.BlockSpec((tm, 1536), lambda i: (i, 1)),
         pl.BlockSpec((tm, D_MODEL), lambda i: (i, 0)), const(D_MODEL, LANE), pl.BlockSpec((tm, LANE), lambda i: (i, 0)),
         pl.BlockSpec((tm, LANE), lambda i: (i, 0)), const(1, LANE), const(1, 1024), const(1024, LANE), const(LANE, 1024)],
        [hspec4, hspec4, hspec8, hspec8, hspec8, pl.BlockSpec((tm, LANE), lambda i: (i, 0)), const(1, LANE),
         pl.BlockSpec((tm, LANE), lambda i: (i, 0))],
        [jax.ShapeDtypeStruct((RET_H, T, LANE), BF)] * 2 + [jax.ShapeDtypeStruct((FOX_H, T, LANE), BF)] * 3
        + [jax.ShapeDtypeStruct((T, LANE), F32), jax.ShapeDtypeStruct((1, LANE), F32), jax.ShapeDtypeStruct((T, LANE), F32)],
        [pltpu.VMEM((1, LANE), F32)], VMEM_BIG, (z_a, z_a, h, w_ff, cos_t, sin_t, b_f, g_all, seg, seg.T), push)


def _segment_matrix():
    m = np.zeros((2 * FOX_H * FOX_D, LANE), np.float32)
    m[np.arange(2 * FOX_H * FOX_D), np.arange(2 * FOX_H * FOX_D) // FOX_D] = 1.0
    return jnp.asarray(m, dtype=BF)


def _ret_fwd(qr, kr, z_a, g_ret, consts, tt=512):
    T = z_a.shape[0]
    nch = tt // CHUNK
    decay, zeta, xi, gcb = consts

    def body(q_ref, k_ref, v_ref, gt_ref, g_ref, d_ref, ze_ref, xi_ref, gc_ref, o_ref, u_ref, st_ref, r_sc):
        i = pl.program_id(0)

        @pl.when(i == 0)
        def _():
            r_sc[...] = jnp.zeros(r_sc.shape, F32)

        for c in range(nch):
            rows = slice(c * CHUNK, (c + 1) * CHUNK)
            for h in range(RET_H):
                cols = slice(h * RET_DV, (h + 1) * RET_DV)
                q, k = q_ref[h, rows, :], k_ref[h, rows, :]
                v32 = v_ref[rows, cols].astype(F32)
                r = r_sc[h]
                st_ref[h, c * CHUNK:c * CHUNK + LANE, :] = r
                s = _nt(q, k) * d_ref[h]
                o = _nn(s.astype(BF), v32.astype(BF)) + _nn(q, r.astype(BF)) * xi_ref[h]
                r_sc[h] = gc_ref[h] * r + _tn(k, (v32 * ze_ref[h]).astype(BF))
                o_ref[rows, cols] = o
                mu = jnp.mean(o, axis=-1, keepdims=True)
                xc = o - mu
                on = xc * lax.rsqrt(jnp.mean(xc * xc, axis=-1, keepdims=True) + EPS)
                gt = gt_ref[rows, cols].astype(F32)
                u_ref[rows, cols] = (gt * _sigmoid(gt) * (on * g_ref[:, cols])).astype(BF)

    hspec = pl.BlockSpec((RET_H, tt, LANE), lambda i: (0, i, 0))
    cspec = pl.BlockSpec((RET_H, CHUNK, LANE), lambda i: (0, 0, 0))
    dspec = pl.BlockSpec((RET_H, CHUNK, CHUNK), lambda i: (0, 0, 0))
    sspec = pl.BlockSpec((RET_H, LANE, LANE), lambda i: (0, 0, 0))
    return pl.pallas_call(
        body, name="ret_fwd", grid=(T // tt,),
        in_specs=[hspec, hspec, pl.BlockSpec((tt, 512), lambda i: (i, 1)), pl.BlockSpec((tt, 512), lambda i: (i, 2)),
                  pl.BlockSpec((1, 512), lambda i: (0, 0)), dspec, cspec, cspec, sspec],
        out_specs=[pl.BlockSpec((tt, 512), lambda i: (i, 0)), pl.BlockSpec((tt, 512), lambda i: (i, 0)), hspec],
        out_shape=[jax.ShapeDtypeStruct((T, 512), F32), jax.ShapeDtypeStruct((T, 512), BF),
                   jax.ShapeDtypeStruct((RET_H, T, LANE), F32)],
        scratch_shapes=[pltpu.VMEM((RET_H, LANE, LANE), F32)],
        compiler_params=_params(("arbitrary",), VMEM_BIG),
    )(qr, kr, z_a, z_a, g_ret, decay, zeta, xi, gcb)


PRUNE_LOG = -110.0
TAME_LOGIT_SPAN = 60.0


def _prune_tables(c, nmax, sub):
    n = c.shape[0] // sub
    u = jnp.sqrt(nmax[0, :FOX_H] * nmax[0, FOX_H:2 * FOX_H]) * 1.02 + 0.5
    first = c[0::sub, :FOX_H].T
    last = c[sub - 1::sub, :FOX_H].T
    blk = jnp.arange(n, dtype=jnp.int32)
    needed = (2.0 * u[:, None, None] + first[:, :, None] - last[:, None, :] >= PRUNE_LOG) | (blk[None, :] >= blk[:, None])[None]
    jlo = jnp.argmax(needed, axis=2).astype(jnp.int32)

    def end_of(key_block):
        reach = jlo[:, None, :] <= key_block[None, :, None]
        return (n - jnp.argmax(reach[:, :, ::-1], axis=2)).astype(jnp.int32)

    sup = 2 * jnp.arange(n // 2, dtype=jnp.int32)
    end_last = end_of(sup + 1)
    end_both = jnp.clip(end_of(sup), sup[None, :] + 2, end_last)
    tame = (2.0 * u < TAME_LOGIT_SPAN).astype(jnp.int32)
    return jlo, end_both, end_last, tame


def _fox_fwd(jlo, tame, q, k, v, sub=FOX_SUB):
    H, T, _ = q.shape
    tb = 2 * sub

    def body(js_ref, tame_ref, q_ref, k_ref, v_ref, o_ref, q2_ref, mx_sc, acc_sc):
        i = pl.program_id(1)
        hd = pl.program_id(0)
        starts = [jnp.minimum(js_ref[hd, 2 * i], 2 * i), jnp.minimum(js_ref[hd, 2 * i + 1], 2 * i)]
        lane = lax.broadcasted_iota(jnp.int32, (sub, LANE), 1)
        row = lax.broadcasted_iota(jnp.int32, (sub, sub), 0)
        col = lax.broadcasted_iota(jnp.int32, (sub, sub), 1)
        causal = row >= col
        qs = [q_ref[0:sub, :], q_ref[sub:tb, :]]
        d0 = pl.multiple_of(i * tb, tb)
        d1 = pl.multiple_of(i * tb + sub, sub)
        k0, k1 = k_ref[pl.ds(d0, sub), :], k_ref[pl.ds(d1, sub), :]
        v0, v1 = v_ref[pl.ds(d0, sub), :], v_ref[pl.ds(d1, sub), :]

        def lane_max(s):
            m = s[:, 0:LANE]
            for c in range(1, s.shape[1] // LANE):
                m = jnp.maximum(m, s[:, c * LANE:(c + 1) * LANE])
            return m

        def put3(base, first, val):
            hi, mid, lo = _split3(val)
            return jnp.where(lane == first, hi, jnp.where(lane == first + 1, mid, jnp.where(lane == first + 2, lo, base)))

        def row_max():
            mx_sc[...] = jnp.full(mx_sc.shape, NEG, F32)
            for a in range(2):
                def max_body(j, carry, a=a):
                    kb = k_ref[pl.ds(pl.multiple_of(j * sub, sub), sub), :]
                    mx_sc[a] = jnp.maximum(mx_sc[a], lane_max(_nt(qs[a], kb)))
                    return carry

                lax.fori_loop(starts[a], 2 * i, max_body, 0)
            mx = [jnp.maximum(mx_sc[0], lane_max(jnp.where(causal, _nt(qs[0], k0), NEG))),
                  jnp.maximum(jnp.maximum(mx_sc[1], lane_max(_nt(qs[1], k0))),
                              lane_max(jnp.where(causal, _nt(qs[1], k1), NEG)))]
            return [jnp.max(t, axis=1, keepdims=True) for t in mx]

        def diag_logit():
            return [jnp.sum(qs[a].astype(F32) * kd.astype(F32), axis=1, keepdims=True) for a, kd in enumerate((k0, k1))]

        def finish(ms):
            qm = [put3(qs[a], L_MAX, -ms[a]) for a in range(2)]
            acc_sc[...] = jnp.zeros(acc_sc.shape, F32)
            for a in range(2):
                def acc_body(j, carry, a=a):
                    off = pl.multiple_of(j * sub, sub)
                    acc_sc[a] += _nn(jnp.exp(_nt(qm[a], k_ref[pl.ds(off, sub), :])).astype(BF), v_ref[pl.ds(off, sub), :])
                    return carry

                lax.fori_loop(starts[a], 2 * i, acc_body, 0)

            def pv(qa, kk, vv, masked):
                p = jnp.exp(_nt(qa, kk))
                if masked:
                    p = jnp.where(causal, p, 0.0)
                return _nn(p.astype(BF), vv)

            accs = [acc_sc[0] + pv(qm[0], k0, v0, True),
                    acc_sc[1] + pv(qm[1], k0, v0, False) + pv(qm[1], k1, v1, True)]
            for a in range(2):
                rows = slice(a * sub, (a + 1) * sub)
                l = accs[a][:, 64:65]
                o_ref[rows, :] = jnp.where(lane < 64, accs[a] / l, 0.0)
                q2_ref[rows, :] = put3(qs[a], L_LSE, -(ms[a] + jnp.log(l)))

        tame = tame_ref[hd] == 1

        @pl.when(tame)
        def _():
            finish(diag_logit())

        @pl.when(jnp.logical_not(tame))
        def _():
            finish(row_max())

    blk = pl.BlockSpec((None, tb, LANE), lambda h, i, js, tm_: (h, i, 0))
    full = pl.BlockSpec((None, T, LANE), lambda h, i, js, tm_: (h, 0, 0))
    return pl.pallas_call(
        body, name="fox_fwd",
        grid_spec=pltpu.PrefetchScalarGridSpec(
            num_scalar_prefetch=2, grid=(H, T // tb), in_specs=[blk, full, full], out_specs=[blk, blk],
            scratch_shapes=[pltpu.VMEM((2, sub, LANE), F32), pltpu.VMEM((2, sub, LANE), F32)]),
        out_shape=[jax.ShapeDtypeStruct((H, T, LANE), F32), jax.ShapeDtypeStruct((H, T, LANE), BF)],
        compiler_params=_params(("parallel", "arbitrary"), VMEM_BIG),
    )(jlo, tame, q, k, v)


def _merge_out(u_r, o_fox, z_a, x, g_ffn, w_ro, w_fo, w_out, tm=512, push=None):
    T = x.shape[0]

    def body(u_ref, of_ref, ar_ref, af_ref, x_ref, g_ref, wro_ref, wfo_ref, wout_ref,
             yr_ref, yf_ref, m_ref, x2_ref, h2_ref, oc_ref):
        u = u_ref[...]
        oc = jnp.concatenate([of_ref[h][:, :FOX_D] for h in range(FOX_H)], axis=-1).astype(BF)
        oc_ref[...] = oc
        yr = jnp.concatenate([_nn(u, wro_ref[k]) for k in range(N_CHIP)], axis=-1)
        yf = jnp.concatenate([_nn(oc, wfo_ref[k]) for k in range(N_CHIP)], axis=-1)
        yr_ref[...] = yr.astype(BF)
        yf_ref[...] = yf.astype(BF)
        m = (_sigmoid(ar_ref[...].astype(F32)) * yr + _sigmoid(af_ref[...].astype(F32)) * yf).astype(BF)
        m_ref[...] = m
        x2 = x_ref[...]
        for k in range(N_CHIP):
            x2 = x2 + _nn(m[:, 256 * k:256 * k + 256], wout_ref[k])
        x2_ref[...] = x2
        r = lax.rsqrt(jnp.mean(x2 * x2, axis=-1, keepdims=True) + EPS)
        h2_ref[...] = (x2 * r * g_ref[...]).astype(BF)

    row = lambda w: pl.BlockSpec((tm, w), lambda i: (i, 0))
    const = lambda shp: pl.BlockSpec(shp, lambda i: (0,) * len(shp))
    return _hosted_call(
        body, "merge_out", (T // tm,),
        [row(512), pl.BlockSpec((FOX_H, tm, LANE), lambda i: (0, i, 0)),
         pl.BlockSpec((tm, 1024), lambda i: (i, 3)), pl.BlockSpec((tm, 1024), lambda i: (i, 4)),
         row(1024), const((1, 1024)), const((N_CHIP, 512, 256)), const((N_CHIP, 512, 256)),
         const((N_CHIP, 256, 1024))],
        [row(1024), row(1024), row(1024), row(1024), row(1024), row(512)],
        [jax.ShapeDtypeStruct((T, 1024), BF), jax.ShapeDtypeStruct((T, 1024), BF),
         jax.ShapeDtypeStruct((T, 1024), BF), jax.ShapeDtypeStruct((T, 1024), F32),
         jax.ShapeDtypeStruct((T, 1024), BF), jax.ShapeDtypeStruct((T, 512), BF)],
        [], VMEM_BIG, (u_r, o_fox, z_a, z_a, x, g_ffn, w_ro, w_fo, w_out), push)


def _load_resident(hbm_refs, vmem_refs, sem):
    cps = [pltpu.make_async_copy(h, v, sem.at[i]) for i, (h, v) in enumerate(zip(hbm_refs, vmem_refs))]
    for cp in cps:
        cp.start()
    for cp in cps:
        cp.wait()


def _ffn_fwd(h2, x2, tgt, w_gate, w_up, w_down, tm=FFN_TM):
    T = h2.shape[0]

    def body(h_ref, x2_ref, t_ref, wg_hbm, wu_hbm, wd_hbm, a_ref, b_ref, act_ref, dy_ref, ls_ref, wg, wu, wd, sem):
        @pl.when(pl.program_id(0) == 0)
        def _():
            _load_resident((wg_hbm, wu_hbm, wd_hbm), (wg, wu, wd), sem)
            ls_ref[...] = jnp.zeros(ls_ref.shape, F32)

        h = h_ref[...]
        err = x2_ref[...] - t_ref[...]
        for k in range(N_CHIP):
            gp = _nt(h, wg[k])
            up = _nt(h, wu[k])
            sg = _sigmoid(gp)
            silu = gp * sg
            a_ref[k] = silu.astype(BF)
            b_ref[k] = (up * sg * (1.0 + gp * (1.0 - sg))).astype(BF)
            act = (silu * up).astype(BF)
            act_ref[k] = act
            err = err + _nn(act, wd[k])
        dy_ref[...] = err * (1.0 / D_MODEL)
        ls_ref[...] += jnp.sum(err * err, axis=0, keepdims=True)

    row = pl.BlockSpec((tm, D_MODEL), lambda i: (i, 0))
    hid = pl.BlockSpec((N_CHIP, tm, FF_SH), lambda i: (0, i, 0))
    anyspec = pl.BlockSpec(memory_space=pl.ANY)
    wshape = pltpu.VMEM((N_CHIP, FF_SH, D_MODEL), BF)
    return pl.pallas_call(
        body, name="ffn_fwd", grid=(T // tm,),
        in_specs=[row, row, row, anyspec, anyspec, anyspec],
        out_specs=[hid, hid, hid, row, pl.BlockSpec((1, D_MODEL), lambda i: (0, 0))],
        out_shape=[jax.ShapeDtypeStruct((N_CHIP, T, FF_SH), BF)] * 3
        + [jax.ShapeDtypeStruct((T, D_MODEL), F32), jax.ShapeDtypeStruct((1, D_MODEL), F32)],
        scratch_shapes=[wshape, wshape, wshape, pltpu.SemaphoreType.DMA((3,))],
        compiler_params=_params(("arbitrary",), VMEM_HUGE),
    )(h2, x2, tgt, w_gate, w_up, w_down)


def _ffn_bwd(dy, sa, sb, x2, g_ffn, w_gate, w_up, w_down, tm=FFN_TM):
    T = dy.shape[0]

    def body(dy_ref, a_ref, b_ref, x2_ref, g_ref, wg_hbm, wu_hbm, wd_hbm, dgp_ref, dup_ref, dx_ref, dg_ref,
             wg, wu, wd, sem):
        @pl.when(pl.program_id(0) == 0)
        def _():
            _load_resident((wg_hbm, wu_hbm, wd_hbm), (wg, wu, wd), sem)
            dg_ref[...] = jnp.zeros(dg_ref.shape, F32)

        dy = dy_ref[...]
        dyb = dy.astype(BF)
        dh = jnp.zeros((tm, D_MODEL), F32)
        for k in range(N_CHIP):
            dact = _nt(dyb, wd[k])
            dup = (dact * a_ref[k]).astype(BF)
            dgp = (dact * b_ref[k]).astype(BF)
            dgp_ref[k] = dgp
            dup_ref[k] = dup
            dh = dh + _nn(dgp, wg[k]) + _nn(dup, wu[k])
        x2 = x2_ref[...]
        r = lax.rsqrt(jnp.mean(x2 * x2, axis=-1, keepdims=True) + EPS)
        xn = x2 * r
        dg_ref[...] += jnp.sum(dh * xn, axis=0, keepdims=True)
        dxn = dh * g_ref[...]
        dx_ref[...] = dy + r * (dxn - xn * jnp.mean(dxn * xn, axis=-1, keepdims=True))

    row = pl.BlockSpec((tm, D_MODEL), lambda i: (i, 0))
    hid = pl.BlockSpec((N_CHIP, tm, FF_SH), lambda i: (0, i, 0))
    vec = pl.BlockSpec((1, D_MODEL), lambda i: (0, 0))
    anyspec = pl.BlockSpec(memory_space=pl.ANY)
    wshape = pltpu.VMEM((N_CHIP, FF_SH, D_MODEL), BF)
    return pl.pallas_call(
        body, name="ffn_bwd", grid=(T // tm,),
        in_specs=[row, hid, hid, row, vec, anyspec, anyspec, anyspec],
        out_specs=[hid, hid, row, vec],
        out_shape=[jax.ShapeDtypeStruct((N_CHIP, T, FF_SH), BF), jax.ShapeDtypeStruct((N_CHIP, T, FF_SH), BF),
                   jax.ShapeDtypeStruct((T, D_MODEL), F32), jax.ShapeDtypeStruct((1, D_MODEL), F32)],
        scratch_shapes=[wshape, wshape, wshape, pltpu.SemaphoreType.DMA((3,))],
        compiler_params=_params(("arbitrary",), VMEM_HUGE),
    )(dy, sa, sb, x2, g_ffn, w_gate, w_up, w_down)


def _out_bwd(dx2, z_a, y_r, y_f, o_raw, o_fox, g_ret, w_ro, w_fo, w_out, tm=512, push=None):
    T = dx2.shape[0]

    def body(dx_ref, gt_ref, ar_ref, af_ref, yr_ref, yf_ref, o_ref, of_ref, g_ref, wro_ref, wfo_ref, wout_ref,
             dyr_ref, dyf_ref, dgt_ref, da_ref, do_ref, dof_ref, dg_ref):
        i = pl.program_id(0)

        @pl.when(i == 0)
        def _():
            dg_ref[...] = jnp.zeros(dg_ref.shape, F32)

        dxb = dx_ref[...].astype(BF)
        dm = jnp.concatenate([_nt(dxb, wout_ref[k]) for k in range(N_CHIP)], axis=-1)
        sr, sf = _sigmoid(ar_ref[...].astype(F32)), _sigmoid(af_ref[...].astype(F32))
        dyr = dm * sr
        dyf = dm * sf
        da_ref[:, :1024] = (dyr * yr_ref[...].astype(F32) * (1.0 - sr)).astype(BF)
        da_ref[:, 1024:] = (dyf * yf_ref[...].astype(F32) * (1.0 - sf)).astype(BF)
        dyr = dyr.astype(BF)
        dyf = dyf.astype(BF)
        dyr_ref[...] = dyr
        dyf_ref[...] = dyf
        du = jnp.zeros((tm, 512), F32)
        doc = jnp.zeros((tm, 512), F32)
        for k in range(N_CHIP):
            du = du + _nt(dyr[:, 256 * k:256 * k + 256], wro_ref[k])
            doc = doc + _nt(dyf[:, 256 * k:256 * k + 256], wfo_ref[k])

        for h in range(RET_H):
            cols = slice(h * RET_DV, (h + 1) * RET_DV)
            o = o_ref[:, cols]
            mu = jnp.mean(o, axis=-1, keepdims=True)
            xc = o - mu
            rstd = lax.rsqrt(jnp.mean(xc * xc, axis=-1, keepdims=True) + EPS)
            on = xc * rstd
            g = g_ref[:, cols]
            gt = gt_ref[:, cols].astype(F32)
            sg = _sigmoid(gt)
            duh = du[:, cols]
            dgt_ref[:, cols] = (duh * (on * g) * sg * (1.0 + gt * (1.0 - sg))).astype(BF)
            dog = duh * gt * sg
            dg_ref[:, cols] += jnp.sum(dog * on, axis=0, keepdims=True)
            don = dog * g
            do_ref[:, cols] = rstd * (don - jnp.mean(don, axis=-1, keepdims=True)
                                      - on * jnp.mean(don * on, axis=-1, keepdims=True))

        lane = lax.broadcasted_iota(jnp.int32, (tm, LANE), 1)
        zpad = jnp.zeros((tm, 64), F32)
        for h in range(FOX_H):
            doh = doc[:, 64 * h:64 * h + 64]
            delta = jnp.sum(doh * of_ref[h][:, :FOX_D], axis=-1, keepdims=True)
            hi, mid, lo = [t.astype(F32) for t in _split3(-delta)]
            da = jnp.concatenate([doh, zpad], axis=-1)
            da = jnp.where(lane == 64, hi, jnp.where(lane == 65, mid, jnp.where(lane == 66, lo, da)))
            dof_ref[h] = da.astype(BF)

    row = lambda w: pl.BlockSpec((tm, w), lambda i: (i, 0))
    const = lambda shp: pl.BlockSpec(shp, lambda i: (0,) * len(shp))
    hsp = pl.BlockSpec((FOX_H, tm, LANE), lambda i: (0, i, 0))
    return _hosted_call(
        body, "out_bwd", (T // tm,),
        [row(1024), pl.BlockSpec((tm, 512), lambda i: (i, 2)), pl.BlockSpec((tm, 1024), lambda i: (i, 3)),
         pl.BlockSpec((tm, 1024), lambda i: (i, 4)), row(1024), row(1024), row(512), hsp,
         const((1, 512)), const((N_CHIP, 512, 256)), const((N_CHIP, 512, 256)), const((N_CHIP, 256, 1024))],
        [row(1024), row(1024), row(512), row(2048), row(512), hsp, const((1, 512))],
        [jax.ShapeDtypeStruct((T, 1024), BF), jax.ShapeDtypeStruct((T, 1024), BF),
         jax.ShapeDtypeStruct((T, 512), BF), jax.ShapeDtypeStruct((T, 2048), BF),
         jax.ShapeDtypeStruct((T, 512), F32), jax.ShapeDtypeStruct((FOX_H, T, LANE), BF),
         jax.ShapeDtypeStruct((1, 512), F32)],
        [], VMEM_BIG, (dx2, z_a, z_a, z_a, y_r, y_f, o_raw, o_fox, g_ret, w_ro, w_fo, w_out), push)


def _ret_bwd(d_o, qr, kr, z_a, states, cos_t, sin_t, consts, tt=512, push=None):
    T = z_a.shape[0]
    nt = T // tt
    nch = tt // CHUNK
    decay, zeta, xi, gcb = consts

    def body(do_ref, q_ref, k_ref, v_ref, st_ref, cos_ref, sin_ref, d_ref, ze_ref, xi_ref, gc_ref, dz_ref, g_sc):
        i = pl.program_id(0)

        @pl.when(i == 0)
        def _():
            g_sc[...] = jnp.zeros(g_sc.shape, F32)

        for c in reversed(range(nch)):
            rows = slice(c * CHUNK, (c + 1) * CHUNK)
            cosv, sinv = cos_ref[rows, :], sin_ref[rows, :]
            dq_parts, dk_parts = [], []
            for h in range(RET_H):
                cols = slice(h * RET_DV, (h + 1) * RET_DV)
                q, k = q_ref[h, rows, :], k_ref[h, rows, :]
                v32 = v_ref[rows, cols].astype(F32)
                vb = v32.astype(BF)
                r = st_ref[h, c * CHUNK:c * CHUNK + LANE, :]
                g = g_sc[h]
                gb = g.astype(BF)
                d_o = do_ref[rows, cols]
                dob = d_o.astype(BF)
                dox = (d_o * xi_ref[h]).astype(BF)
                dec = d_ref[h]
                s = (_nt(q, k) * dec).astype(BF)
                ds = (_nt(dob, vb) * dec).astype(BF)
                dv = _tn(s, dob) + ze_ref[h] * _nn(k, gb)
                dq = _nn(ds, k) + _nt(dox, r.astype(BF))
                dk = _tn(ds, q) + _nt((v32 * ze_ref[h]).astype(BF), gb)
                g_sc[h] = gc_ref[h] * g + _tn(q, dox)
                dq_parts.append((dq * cosv - _swap32(dq) * sinv)[:, :64])
                dk_parts.append(((dk * cosv - _swap32(dk) * sinv) * 0.125)[:, :64])
                dz_ref[rows, 512 + h * RET_DV:512 + (h + 1) * RET_DV] = dv.astype(BF)
            dz_ref[rows, 0:256] = jnp.concatenate(dq_parts, axis=-1).astype(BF)
            dz_ref[rows, 256:512] = jnp.concatenate(dk_parts, axis=-1).astype(BF)

    rev = lambda i: nt - 1 - i
    hspec = pl.BlockSpec((RET_H, tt, LANE), lambda i: (0, rev(i), 0))
    cspec = pl.BlockSpec((RET_H, CHUNK, LANE), lambda i: (0, 0, 0))
    tab = pl.BlockSpec((tt, LANE), lambda i: (rev(i), 0))
    (dz,), lands = _hosted_call(
        body, "ret_bwd", (nt,),
        [pl.BlockSpec((tt, 512), lambda i: (rev(i), 0)), hspec, hspec,
         pl.BlockSpec((tt, 512), lambda i: (rev(i), 1)), hspec, tab, tab,
         pl.BlockSpec((RET_H, CHUNK, CHUNK), lambda i: (0, 0, 0)), cspec, cspec,
         pl.BlockSpec((RET_H, LANE, LANE), lambda i: (0, 0, 0))],
        [pl.BlockSpec((tt, 1024), lambda i: (rev(i), 0))], [jax.ShapeDtypeStruct((T, 1024), BF)],
        [pltpu.VMEM((RET_H, LANE, LANE), F32)], VMEM_BIG,
        (d_o, qr, kr, z_a, states, cos_t, sin_t, decay, zeta, xi, gcb), push)
    return dz, lands


def _fox_bwd(end_both, end_last, q2, k, v, do, sub=FOX_SUB):
    H, T, _ = k.shape
    tb = 2 * sub

    def body(eb_ref, el_ref, q_ref, do_ref, k_ref, v_ref, dq_ref, dk_ref, dv_ref, dk_sc, dv_sc):
        j = pl.program_id(1)
        n_both = eb_ref[pl.program_id(0), j]
        n_last = el_ref[pl.program_id(0), j]

        @pl.when(j == 0)
        def _():
            dq_ref[...] = jnp.zeros(dq_ref.shape, F32)

        dk_sc[...] = jnp.zeros(dk_sc.shape, F32)
        dv_sc[...] = jnp.zeros(dv_sc.shape, F32)
        krow = lax.broadcasted_iota(jnp.int32, (tb, sub), 0)
        qcol = lax.broadcasted_iota(jnp.int32, (tb, sub), 1)

        def step(i, r0, r1, shift):
            off = pl.multiple_of(i * sub, sub)
            qq = q_ref[pl.ds(off, sub), :]
            dd = do_ref[pl.ds(off, sub), :]
            kk, vv = k_ref[r0:r1, :], v_ref[r0:r1, :]
            p = jnp.exp(_nt(kk, qq))
            if shift is not None:
                p = jnp.where(qcol[0:r1 - r0, :] + shift >= krow[0:r1 - r0, :], p, 0.0)
            ds = (p * _nt(vv, dd)).astype(BF)
            dv_sc[r0:r1, :] += _nn(p.astype(BF), dd)
            dk_sc[r0:r1, :] += _nn(ds, qq)
            dq_ref[pl.ds(off, sub), :] += _tn(ds, kk)

        step(2 * j, 0, sub, 0)
        step(2 * j + 1, 0, tb, sub)

        def both_body(i, carry):
            step(i, 0, tb, None)
            return carry

        def last_body(i, carry):
            step(i, sub, tb, None)
            return carry

        lax.fori_loop(2 * j + 2, n_both, both_body, 0)
        lax.fori_loop(n_both, n_last, last_body, 0)
        dk_ref[...] = dk_sc[...]
        dv_ref[...] = dv_sc[...]

    blk = pl.BlockSpec((None, tb, LANE), lambda h, j, eb, el: (h, j, 0))
    full = pl.BlockSpec((None, T, LANE), lambda h, j, eb, el: (h, 0, 0))
    shp = jax.ShapeDtypeStruct((H, T, LANE), F32)
    return pl.pallas_call(
        body, name="fox_bwd",
        grid_spec=pltpu.PrefetchScalarGridSpec(
            num_scalar_prefetch=2, grid=(H, T // tb), in_specs=[full, full, blk, blk], out_specs=[full, blk, blk],
            scratch_shapes=[pltpu.VMEM((tb, LANE), F32), pltpu.VMEM((tb, LANE), F32)]),
        out_shape=[shp, shp, shp],
        compiler_params=_params(("arbitrary", "arbitrary"), VMEM_BIG),
    )(end_both, end_last, q2, do, k, v)


def _fox_post_bwd(dq, dk, dv, z_a, z_ff, b_f, g_q, g_k, tm=256, push=None):
    T = z_a.shape[0]
    nt = T // tm

    def body(dq_ref, dk_ref, dv_ref, zf_ref, zff_ref, b_ref, g_ref, sc_ref, seg_ref, segt_ref,
             dz_ref, dff_ref, dg_ref, db_ref, carry):
        i = pl.program_id(0)

        @pl.when(i == 0)
        def _():
            carry[...] = jnp.zeros(carry.shape, F32)
            dg_ref[...] = jnp.zeros(dg_ref.shape, F32)
            db_ref[...] = jnp.zeros(db_ref.shape, F32)

        lane = lax.broadcasted_iota(jnp.int32, (tm, LANE), 1)
        dcm = jnp.zeros((tm, LANE), F32)
        for h in range(FOX_H):
            dcm = jnp.where(lane == h, dq_ref[h][:, L_CQ:L_CQ + 1] - dk_ref[h][:, L_CK:L_CK + 1], dcm)

        def seg_mean(v):
            return sum(_nn(t, seg_ref[...]) for t in _split3(v)) * (1.0 / FOX_D)

        def seg_bcast(v):
            return sum(_nn(t, segt_ref[...]) for t in _split3(v))

        x = zf_ref[:, :1024].astype(F32)
        dy = jnp.concatenate([dq_ref[h][:, :FOX_D] for h in range(FOX_H)]
                             + [dk_ref[h][:, :FOX_D] for h in range(FOX_H)], axis=-1) * sc_ref[...]
        rb = seg_bcast(lax.rsqrt(seg_mean(x * x) + EPS))
        xn = x * rb
        dg_ref[...] += jnp.sum(dy * xn, axis=0, keepdims=True)
        dxn = dy * g_ref[...]
        dz_ref[:, :1024] = (rb * (dxn - xn * seg_bcast(seg_mean(dxn * xn)))).astype(BF)
        dz_ref[:, 1024:] = jnp.concatenate([dv_ref[h][:, :FOX_D] for h in range(FOX_H)], axis=-1).astype(BF)

        row = lax.broadcasted_iota(jnp.int32, (tm, tm), 0)
        col = lax.broadcasted_iota(jnp.int32, (tm, tm), 1)
        tri = (row <= col).astype(BF)
        hi, mid, lo = _split3(dcm)
        dlogf = _nn(tri, hi) + _nn(tri, mid) + _nn(tri, lo) + carry[...]
        carry[...] = dlogf[0:1, :]
        dff = jnp.where(lane < FOX_H, dlogf * _sigmoid(-(zff_ref[...] + b_ref[...])), 0.0)
        dff_ref[...] = dff.astype(BF)
        db_ref[...] += jnp.sum(dff, axis=0, keepdims=True)

    rev = lambda i: nt - 1 - i
    hsp = pl.BlockSpec((FOX_H, tm, LANE), lambda i: (0, rev(i), 0))
    const = lambda r, w: pl.BlockSpec((r, w), lambda i: (0, 0))
    seg = _segment_matrix()
    g_all = jnp.concatenate([jnp.tile(g_q, (1, FOX_H)), jnp.tile(g_k, (1, FOX_H))], axis=1)
    scale = jnp.asarray(np.concatenate([np.full((1, 512), 0.125, np.float32), np.ones((1, 512), np.float32)], axis=1))
    (dz, dff, dg, db), lands = _hosted_call(
        body, "fox_post_bwd", (nt,),
        [hsp, hsp, hsp, pl.BlockSpec((tm, 1536), lambda i: (rev(i), 1)),
         pl.BlockSpec((tm, LANE), lambda i: (rev(i), 0)), const(1, LANE), const(1, 1024), const(1, 1024),
         const(1024, LANE), const(LANE, 1024)],
        [pl.BlockSpec((tm, 1536), lambda i: (rev(i), 0)), pl.BlockSpec((tm, LANE), lambda i: (rev(i), 0)),
         const(1, 1024), const(1, LANE)],
        [jax.ShapeDtypeStruct((T, 1536), BF), jax.ShapeDtypeStruct((T, LANE), BF),
         jax.ShapeDtypeStruct((1, 1024), F32), jax.ShapeDtypeStruct((1, LANE), F32)],
        [pltpu.VMEM((1, LANE), F32)], VMEM_BIG, (dq, dk, dv, z_a, z_ff, b_f, g_all, scale, seg, seg.T), push)
    dg_heads = dg.reshape(2, FOX_H, FOX_D).sum(axis=1)
    return (dz, dff, dg_heads[0:1], dg_heads[1:2], db), lands


def _in_bwd(dz_ret, dz_gt, dz_fox, dz_a, dz_ff, w_a, w_ff, x, g_mix, dx2, tm=512, push=None, sib=None):
    T = x.shape[0]

    def body(r_ref, t_ref, f_ref, a_ref, ff_ref, wa_ref, wf_ref, x_ref, g_ref, dx2_ref, dx_ref, dg_ref):
        i = pl.program_id(0)

        @pl.when(i == 0)
        def _():
            dg_ref[...] = jnp.zeros(dg_ref.shape, F32)

        dh = (_nt(r_ref[...], wa_ref[:, C_RET:C_GT]) + _nt(t_ref[...], wa_ref[:, C_GT:C_FOX])
              + _nt(f_ref[...], wa_ref[:, C_FOX:C_A]) + _nt(a_ref[...], wa_ref[:, C_A:C_END])
              + _nt(ff_ref[...], wf_ref[...]))
        xv = x_ref[...]
        r = lax.rsqrt(jnp.mean(xv * xv, axis=-1, keepdims=True) + EPS)
        xn = xv * r
        dg_ref[...] += jnp.sum(dh * xn, axis=0, keepdims=True)
        dxn = dh * g_ref[...]
        dx_ref[...] = dx2_ref[...] + r * (dxn - xn * jnp.mean(dxn * xn, axis=-1, keepdims=True))

    row = lambda w: pl.BlockSpec((tm, w), lambda i: (i, 0))
    const = lambda shp: pl.BlockSpec(shp, lambda i: (0,) * len(shp))
    return _hosted_call(
        body, "in_bwd", (T // tm,),
        [row(1024), row(512), row(1536), row(2048), row(LANE), const((D_MODEL, C_END)),
         const((D_MODEL, LANE)), row(1024), const((1, 1024)), row(1024)],
        [row(1024), const((1, 1024))],
        [jax.ShapeDtypeStruct((T, 1024), F32), jax.ShapeDtypeStruct((1, 1024), F32)],
        [], VMEM_BIG, (dz_ret, dz_gt, dz_fox, dz_a, dz_ff, w_a, w_ff, x, g_mix, dx2), push, sib)


def _mesh_pos():
    return lax.axis_index("x"), lax.axis_index("y"), lax.axis_index("c")


def _staged_place(src, name):
    stacked = src.ndim == 3
    R, C = src.shape[-2:]
    tr = _row_tile(R, min(256, R // 2), 16)
    n = R // tr
    assert n >= 2

    def body(s_ref, o_ref, buf, sem):
        i = pl.program_id(0)
        slot = i % 2
        x, y, _ = _mesh_pos()
        kme = 2 * x + y

        def out_copy(s, step):
            return pltpu.make_async_copy(buf.at[s], o_ref.at[kme, pl.ds(pl.multiple_of(step * tr, tr), tr), :], sem.at[s])

        @pl.when(i >= 2)
        def _():
            out_copy(slot, i - 2).wait()

        buf[slot] = (s_ref[kme] if stacked else s_ref[...]).astype(BF)
        out_copy(slot, i).start()

        @pl.when(i == n - 1)
        def _():
            out_copy(1 - slot, i - 1).wait()
            out_copy(slot, i).wait()

    in_spec = (pl.BlockSpec((N_CHIP, tr, C), lambda i: (0, i, 0)) if stacked else pl.BlockSpec((tr, C), lambda i: (i, 0)))
    return pl.pallas_call(
        body, name=name, grid=(n,), in_specs=[in_spec], out_specs=pl.BlockSpec(memory_space=pl.ANY),
        out_shape=jax.ShapeDtypeStruct((N_CHIP, R, C), BF),
        scratch_shapes=[pltpu.VMEM((2, tr, C), BF), pltpu.SemaphoreType.DMA((2,))],
        compiler_params=_params(("arbitrary",)),
    )(src)


def _push_copies(src, land, send_sem, recv_sem, receiving):
    x, y, c = _mesh_pos()
    kme = 2 * x + y
    cps = []
    for w in range(len(land)):
        for j, (px, py) in enumerate([(1 - x, y), (x, 1 - y), (1 - x, 1 - y)]):
            kpeer = 2 * px + py
            cps.append(pltpu.make_async_remote_copy(
                src_ref=land[w].at[kme] if src is None else src[w].at[kpeer],
                dst_ref=land[w].at[kpeer if receiving else kme],
                send_sem=send_sem.at[3 * w + j], recv_sem=recv_sem.at[3 * w + j],
                device_id=(px, py, c), device_id_type=MESH))
    return cps


def _gather_two_level(stack, name):
    _, R, C = stack.shape
    hr = R // 2

    def body(_, land, send_sem, recv_sem):
        x, y, c = _mesh_pos()
        kme = 2 * x + y
        chips = [(1 - x, y), (x, 1 - y), (1 - x, 1 - y)]

        def rows(k, core):
            return land.at[k, pl.ds(pl.multiple_of(core * hr, hr), hr), :]

        def copy(idx, k, core, to):
            return pltpu.make_async_remote_copy(src_ref=rows(k, core), dst_ref=rows(k, core), send_sem=send_sem.at[idx],
                                                recv_sem=recv_sem.at[idx], device_id=to, device_id_type=MESH)

        first = [copy(j, kme, c, (px, py, c)) for j, (px, py) in enumerate(chips)]
        for cp in first:
            cp.start()
        passed = [copy(3 + j, 2 * px + py, c, (x, y, 1 - c)) for j, (px, py) in enumerate(chips)]
        for j, (px, py) in enumerate(chips):
            copy(j, 2 * px + py, c, (px, py, c)).wait_recv()
            passed[j].start()
        for j, (px, py) in enumerate(chips):
            copy(3 + j, 2 * px + py, 1 - c, (x, y, 1 - c)).wait_recv()
        for cp in first + passed:
            cp.wait_send()

    anyspec = pl.BlockSpec(memory_space=pl.ANY)
    return pl.pallas_call(
        body, name=name, in_specs=[anyspec], out_specs=anyspec,
        out_shape=jax.ShapeDtypeStruct(stack.shape, stack.dtype), input_output_aliases={0: 0},
        scratch_shapes=[pltpu.SemaphoreType.DMA((6,)), pltpu.SemaphoreType.DMA((6,))],
    )(stack)


def _gather_small(small):
    def body(sv, svo, ssend, srecv, sloc):
        x, y, c = _mesh_pos()
        me = 4 * x + 2 * y + c
        flips = [(b >> 2 & 1, b >> 1 & 1, b & 1) for b in range(1, 8)]
        others = [(1 - x if fx else x, 1 - y if fy else y, 1 - c if fc else c) for fx, fy, fc in flips]
        local = pltpu.make_async_copy(sv, svo.at[me], sloc)
        local.start()
        sends = []
        for j, (px, py, pc) in enumerate(others):
            cp = pltpu.make_async_remote_copy(
                src_ref=sv, dst_ref=svo.at[me], send_sem=ssend.at[j], recv_sem=srecv.at[j],
                device_id=(px, py, pc), device_id_type=MESH)
            cp.start()
            sends.append(cp)
        for j, (px, py, pc) in enumerate(others):
            pltpu.make_async_remote_copy(
                src_ref=sv, dst_ref=svo.at[4 * px + 2 * py + pc], send_sem=ssend.at[j], recv_sem=srecv.at[j],
                device_id=(px, py, pc), device_id_type=MESH).wait_recv()
        for cp in sends:
            cp.wait_send()
        local.wait()

    anyspec = pl.BlockSpec(memory_space=pl.ANY)
    return pl.pallas_call(
        body, name="gather_small", in_specs=[anyspec], out_specs=anyspec,
        out_shape=jax.ShapeDtypeStruct((8,) + small.shape, small.dtype),
        scratch_shapes=[pltpu.SemaphoreType.DMA((7,)), pltpu.SemaphoreType.DMA((7,)), pltpu.SemaphoreType.DMA],
    )(small)


def _sibling_exchange(arrs):
    n = len(arrs)

    def body(*refs):
        ins, outs = refs[:n], refs[n:2 * n]
        send_sems, recv_sems = refs[2 * n:]
        x, y, c = _mesh_pos()
        cps = [pltpu.make_async_remote_copy(
            src_ref=ins[w], dst_ref=outs[w], send_sem=send_sems.at[w], recv_sem=recv_sems.at[w],
            device_id=(x, y, 1 - c), device_id_type=MESH) for w in range(n)]
        for cp in cps:
            cp.start()
        for cp in cps:
            cp.wait_recv()
        for cp in cps:
            cp.wait_send()

    anyspec = pl.BlockSpec(memory_space=pl.ANY)
    return pl.pallas_call(
        body, name="sibling_exchange",
        in_specs=[anyspec] * n, out_specs=[anyspec] * n,
        out_shape=[jax.ShapeDtypeStruct(a.shape, a.dtype) for a in arrs],
        scratch_shapes=[pltpu.SemaphoreType.DMA((n,)), pltpu.SemaphoreType.DMA((n,))],
    )(*arrs)


def _sum_stack(own, recv, name):
    _, R, C = recv.shape
    tr = _row_tile(R, 256, 16)

    def body(g_ref, r_ref, o_ref):
        x, y, _ = _mesh_pos()
        kme = 2 * x + y
        acc = g_ref[kme].astype(F32)
        for d in range(1, N_CHIP):
            acc = acc + r_ref[(kme + d) % N_CHIP].astype(F32)
        o_ref[...] = acc

    spec = pl.BlockSpec((N_CHIP, tr, C), lambda i: (0, i, 0))
    return pl.pallas_call(
        body, name=name, grid=(R // tr,), in_specs=[spec, spec],
        out_specs=pl.BlockSpec((tr, C), lambda i: (i, 0)),
        out_shape=jax.ShapeDtypeStruct((R, C), F32),
        compiler_params=_params(("parallel",)),
    )(own, recv)


def _adam_math(w, g, m, v):
    m2 = ADAM_B1 * m + (1.0 - ADAM_B1) * g
    v2 = ADAM_B2 * v + (1.0 - ADAM_B2) * (g * g)
    m_hat = m2 / (1.0 - ADAM_B1 ** ADAM_STEP)
    v_hat = v2 / (1.0 - ADAM_B2 ** ADAM_STEP)
    delta = -ADAM_LR * (m_hat / (jnp.sqrt(v_hat) + ADAM_EPS) + ADAM_WD * w)
    return delta, m2, v2


def _adamw(w, m, v, s0, s1, name):
    R, C = w.shape
    tr = _row_tile(R, 256, 8)

    def body(w_ref, m_ref, v_ref, a_ref, b_ref, g_ref, d_ref, m2_ref, v2_ref):
        g = a_ref[...] + b_ref[...]
        delta, m2, v2 = _adam_math(w_ref[...], g, m_ref[...], v_ref[...])
        g_ref[...] = g
        d_ref[...] = delta
        m2_ref[...] = m2
        v2_ref[...] = v2

    spec = pl.BlockSpec((tr, C), lambda i: (i, 0))
    shp = jax.ShapeDtypeStruct((R, C), F32)
    return pl.pallas_call(
        body, name=name, grid=(R // tr,), in_specs=[spec] * 5, out_specs=[spec] * 4, out_shape=[shp] * 4,
        compiler_params=_params(("parallel",), VMEM_BIG),
    )(w, m, v, s0, s1)


def _adamw_small(ws, ms, vs, gathered):
    n = len(SMALL)

    def body(*refs):
        w_refs, m_refs, v_refs, s_ref = refs[:n], refs[n:2 * n], refs[2 * n:3 * n], refs[3 * n]
        outs = refs[3 * n + 1:]
        g_all = s_ref[0]
        for d in range(1, 8):
            g_all = g_all + s_ref[d]
        off = 0
        for i, (_, width) in enumerate(SMALL):
            g = g_all[:, off:off + width]
            delta, m2, v2 = _adam_math(w_refs[i][...], g, m_refs[i][...], v_refs[i][...])
            for kind, val in enumerate((g, delta, m2, v2)):
                outs[kind * n + i][...] = val
            off += width + (-width % LANE)

    shapes = [jax.ShapeDtypeStruct((1, width), F32) for _, width in SMALL]
    res = pl.pallas_call(body, name="adamw_small", out_shape=shapes * 4)(*ws, *ms, *vs, gathered)
    return [dict(zip([nm for nm, _ in SMALL], res[kind * n:(kind + 1) * n])) for kind in range(4)]


SMALL = (("g_mix", 1024), ("g_ffn", 1024), ("g_ret_norm", 512), ("g_fox_q", 64), ("g_fox_k", 64), ("b_forget", 8))
SMALL_W = 3072


def _pack_small(parts):
    cols = []
    for (name, n) in SMALL:
        p = parts[name].reshape(1, -1)[:, :n]
        pad = -n % LANE
        cols.append(jnp.pad(p, ((0, 0), (0, pad))) if pad else p)
    used = sum(c.shape[1] for c in cols)
    cols.append(jnp.zeros((1, SMALL_W - used), F32))
    return jnp.concatenate(cols, axis=1)


def kernel(x, g_mix, w_in, b_forget, g_ret_norm, w_ret_o, g_fox_q, g_fox_k, w_fox_o, w_out, g_ffn, w_gate, w_up, w_down, loss_target, m_g_mix, m_w_in, m_b_forget, m_g_ret_norm, m_w_ret_o, m_g_fox_q, m_g_fox_k, m_w_fox_o, m_w_out, m_g_ffn, m_w_gate, m_w_up, m_w_down, v_g_mix, v_w_in, v_b_forget, v_g_ret_norm, v_w_ret_o, v_g_fox_q, v_g_fox_k, v_w_fox_o, v_w_out, v_g_ffn, v_w_gate, v_w_up, v_w_down):
    T = x.shape[1]
    xs = x[0]
    tgt = loss_target[0]
    big_names = ("w_in", "w_ret_o", "w_fox_o", "w_out", "w_gate", "w_up", "w_down")
    tr = lambda a: jnp.swapaxes(a[0], 0, 1)
    big_w = dict(w_in=w_in[0], w_ret_o=w_ret_o[0], w_fox_o=w_fox_o[0], w_out=w_out[0], w_gate=tr(w_gate),
                 w_up=tr(w_up), w_down=w_down[0])
    big_m = dict(w_in=m_w_in[0], w_ret_o=m_w_ret_o[0], w_fox_o=m_w_fox_o[0], w_out=m_w_out[0], w_gate=tr(m_w_gate),
                 w_up=tr(m_w_up), w_down=m_w_down[0])
    big_v = dict(w_in=v_w_in[0], w_ret_o=v_w_ret_o[0], w_fox_o=v_w_fox_o[0], w_out=v_w_out[0], w_gate=tr(v_w_gate),
                 w_up=tr(v_w_up), w_down=v_w_down[0])
    small_w = dict(g_mix=g_mix, g_ffn=g_ffn, g_ret_norm=g_ret_norm, g_fox_q=g_fox_q, g_fox_k=g_fox_k, b_forget=b_forget)
    small_m = dict(g_mix=m_g_mix, g_ffn=m_g_ffn, g_ret_norm=m_g_ret_norm, g_fox_q=m_g_fox_q, g_fox_k=m_g_fox_k,
                   b_forget=m_b_forget)
    small_v = dict(g_mix=v_g_mix, g_ffn=v_g_ffn, g_ret_norm=v_g_ret_norm, g_fox_q=v_g_fox_q, g_fox_k=v_g_fox_k,
                   b_forget=v_b_forget)

    stacks = {n: _staged_place(big_w[n], "place_" + n) for n in big_names}
    s_in = _gather_two_level(stacks["w_in"], "gather_w_in")
    w_a, w_ff = _assemble_w_in(s_in)
    b_pad = jnp.pad(b_forget, ((0, 0), (0, LANE - FOX_H)))
    cos_t, sin_t = _rope_tables(T)
    consts = _ret_consts()

    h = _rms_cast(xs, g_mix)
    z_a, (s_ro, s_fo, s_gate) = _mm_nn(
        h, w_a, "proj_in", BF, tm=1024, push=(None, [stacks["w_ret_o"], stacks["w_fox_o"], stacks["w_gate"]]))
    (qr, kr, qf, kf, vf, c_cum, nmax, z_ff), (s_out, s_up) = _mix_prep(
        z_a, h, w_ff, cos_t, sin_t, b_pad, g_fox_q, g_fox_k, push=(None, [stacks["w_out"], stacks["w_up"]]))
    jlo, end_both, end_last, tame = _prune_tables(c_cum, nmax, FOX_SUB)
    o_raw, u_r, states = _ret_fwd(qr, kr, z_a, g_ret_norm, consts)
    o_fox, q2 = _fox_fwd(jlo, tame, qf, kf, vf)
    (y_r, y_f, mrg, x2, h2, o_cat), (s_down,) = _merge_out(u_r, o_fox, z_a, xs, g_ffn, s_ro, s_fo, s_out,
                                                            push=(None, [stacks["w_down"]]))
    sa, sb, act, dy, loss_vec = _ffn_fwd(h2, x2, tgt, s_gate, s_up, s_down)
    loss = lax.psum(0.5 / D_MODEL * jnp.sum(loss_vec), ("x", "y", "c"))

    def scatter_job(grads):
        return (grads, [lax.empty(g.shape, g.dtype) for g in grads])

    dgp, dup, dx2, dg_ffn = _ffn_bwd(dy, sa, sb, x2, g_ffn, s_gate, s_up, s_down)
    (g_gate, _), (g_up, _), (g_down, _) = (_grad_astack(dgp, h2, "gw_gate"), _grad_astack(dup, h2, "gw_up"),
                                           _grad_astack(act, dy, "gw_down"))
    (d_yr, d_yf, dz_gt, dz_a, d_o, do_fox, dg_ret), (r_gate, r_up) = _out_bwd(
        dx2, z_a, y_r, y_f, o_raw, o_fox, g_ret_norm, s_ro, s_fo, s_out, push=scatter_job([g_gate, g_up]))
    dz_ret, (r_down,) = _ret_bwd(d_o, qr, kr, z_a, states, cos_t, sin_t, consts, push=scatter_job([g_down]))
    dq_f, dk_f, dv_f = _fox_bwd(end_both, end_last, q2, kf, vf, do_fox)
    g_mid = [_grad_colstack(u_r, d_yr, "gw_ret_o", 256), _grad_colstack(o_cat, d_yf, "gw_fox_o", 256),
             _grad_plain(mrg, dx2, "gw_out", BF).reshape(N_CHIP, 256, D_MODEL)]
    (dz_fox, dz_ff, dg_q, dg_k, db_f), (r_ro, r_fo, r_out) = _fox_post_bwd(
        dq_f, dk_f, dv_f, z_a, z_ff, b_pad, g_fox_q, g_fox_k, push=scatter_job(g_mid))
    gi_ret, gi_gt, gi_ff = _grad_multi(h, [dz_ret, dz_gt, dz_ff], "gw_in_small")
    gi_fox, gi_a = _grad_multi(h, [dz_fox, dz_a], "gw_in_large")
    g_in = _pack_g_in(gi_ret, gi_gt, gi_fox, gi_a, gi_ff)
    sums_early = [_sum_stack(g, r, "sum_" + n) for g, r, n in zip(
        g_mid + [g_gate, g_up, g_down], [r_ro, r_fo, r_out, r_gate, r_up, r_down], big_names[1:])]
    (grad_x, dg_mix), (r_in,), sib_early = _in_bwd(dz_ret, dz_gt, dz_fox, dz_a, dz_ff, w_a, w_ff, xs, g_mix, dx2,
                                                   push=scatter_job([g_in]), sib=sums_early)
    small_g = _pack_small(dict(g_mix=dg_mix, g_ffn=dg_ffn, g_ret_norm=dg_ret, g_fox_q=dg_q, g_fox_k=dg_k, b_forget=db_f))

    small_all = _gather_small(small_g)
    sum_in = _sum_stack(g_in, r_in, "sum_w_in")
    sums = [sum_in] + sums_early
    sib = list(_sibling_exchange([sum_in])) + sib_early
    big_out = {n: _adamw(big_w[n], big_m[n], big_v[n], sums[i], sib[i], "adamw_" + n) for i, n in enumerate(big_names)}
    small_out = _adamw_small(*[[d[nm] for nm, _ in SMALL] for d in (small_w, small_m, small_v)], small_all)

    order = ("g_mix", "w_in", "b_forget", "g_ret_norm", "w_ret_o", "g_fox_q", "g_fox_k", "w_fox_o", "w_out", "g_ffn",
             "w_gate", "w_up", "w_down")
    outs = [loss, grad_x[None]]
    for idx in range(4):
        for n in order:
            if n in ("w_gate", "w_up"):
                outs.append(jnp.swapaxes(big_out[n][idx], 0, 1)[None])
            else:
                outs.append(big_out[n][idx][None] if n in big_out else small_out[idx][n])
    return tuple(outs)
```
